```python
import jax, jax.numpy as jnp
from jax import lax
import numpy as np

D_MODEL = 1024
BATCH = 8
SEQ = 2048
DEPTH = 1
DEC_BATCH = 128
DEC_SEQ = 8
PAST_LEN = 16384
PAGE_SIZE = 128

HEAD_DIM = 64
N_Q_HEADS = 8
N_KV_HEADS = 2
GROUP = N_Q_HEADS // N_KV_HEADS
ATTN_WIDTH = N_Q_HEADS * HEAD_DIM
KV_WIDTH = N_KV_HEADS * HEAD_DIM
WINDOW = 128
ROPE_THETA = 500000.0
ROPE_DIM = HEAD_DIM // 4
CONV_WIDTH = D_MODEL - ATTN_WIDTH
CONV_K = 31
MIX_WIDTH = ATTN_WIDTH + CONV_WIDTH
IN_WIDTH = ATTN_WIDTH + 2 * KV_WIDTH + 2 * CONV_WIDTH
N_EXPERTS = 32
TOP_K = 4
D_FF = D_MODEL
SWIGLU_LIMIT = 7.0
SWIGLU_ALPHA = 1.702
MOE_BLOCK = 256
RMS_EPS = 1e-5
LN_EPS = 1e-5

kernel_name = "hymba_swa_sink_conformer_conv_moe_step"


def rms_norm(x, g):
    xf = x.astype(jnp.float32)
    y = xf * lax.rsqrt(jnp.mean(xf * xf, axis=-1, keepdims=True) + RMS_EPS)
    return (y * g.astype(jnp.float32)).astype(x.dtype)


def layer_norm(x, g, b):
    xf = x.astype(jnp.float32)
    mu = jnp.mean(xf, axis=-1, keepdims=True)
    xc = xf - mu
    var = jnp.mean(xc * xc, axis=-1, keepdims=True)
    y = xc * lax.rsqrt(var + LN_EPS) * g.astype(jnp.float32) + b.astype(jnp.float32)
    return y.astype(x.dtype)


def partial_rotary(x, pos):
    half = ROPE_DIM // 2
    inv_freq = jnp.power(jnp.float32(ROPE_THETA), -jnp.arange(half, dtype=jnp.float32) * 2.0 / ROPE_DIM)
    ang = pos.astype(jnp.float32)[:, None] * inv_freq[None, :]
    cos = jnp.cos(ang)[:, None, :]
    sin = jnp.sin(ang)[:, None, :]
    xr = x[..., :ROPE_DIM].astype(jnp.float32)
    x1, x2 = xr[..., :half], xr[..., half:]
    rot = jnp.concatenate([x1 * cos - x2 * sin, x2 * cos + x1 * sin], axis=-1).astype(x.dtype)
    return jnp.concatenate([rot, x[..., ROPE_DIM:]], axis=-1)


def banded_sink_attention(q, k, v, q_pos, k_pos, sinks):
    scale = HEAD_DIM ** -0.5
    s = jnp.einsum('nbqhgd,nbkhd->nbhgqk', q.astype(jnp.float32), k.astype(jnp.float32)) * scale
    diff = q_pos[:, :, None] - k_pos[:, None, :]
    valid = (diff >= 0) & (diff < WINDOW) & (k_pos >= 0)[:, None, :]
    s = jnp.where(valid[None, :, None, None], s, -jnp.inf)
    sink = sinks.astype(jnp.float32).reshape(N_KV_HEADS, GROUP)[:, :, None, None]
    m = jnp.maximum(jnp.max(s, axis=-1, keepdims=True), sink)
    p = jnp.exp(s - m)
    p = p / (jnp.sum(p, axis=-1, keepdims=True) + jnp.exp(sink - m))
    o = jnp.einsum('nbhgqk,nbkhd->nbqhgd', p, v.astype(jnp.float32))
    return o.astype(v.dtype)


def prompt_window_attention(q, k, v, sinks):
    N, S = q.shape[0], q.shape[1]
    NB = S // WINDOW
    qb = q.reshape(N, NB, WINDOW, N_KV_HEADS, GROUP, HEAD_DIM)
    kb = k.reshape(N, NB, WINDOW, N_KV_HEADS, HEAD_DIM)
    vb = v.reshape(N, NB, WINDOW, N_KV_HEADS, HEAD_DIM)
    prev = lambda t: jnp.concatenate([jnp.zeros_like(t[:, :1]), t[:, :-1]], axis=1)
    kc = jnp.concatenate([prev(kb), kb], axis=2)
    vc = jnp.concatenate([prev(vb), vb], axis=2)
    base = jnp.arange(NB, dtype=jnp.int32)[:, None] * WINDOW
    q_pos = base + jnp.arange(WINDOW, dtype=jnp.int32)[None, :]
    k_pos = base - WINDOW + jnp.arange(2 * WINDOW, dtype=jnp.int32)[None, :]
    o = banded_sink_attention(qb, kc, vc, q_pos, k_pos, sinks)
    return o.reshape(N, S, ATTN_WIDTH)


def sample_window_attention(q, k, v, k_buf, v_buf, sinks):
    N, T = q.shape[0], q.shape[1]
    kc = jnp.concatenate([k_buf, k], axis=1)
    vc = jnp.concatenate([v_buf, v], axis=1)
    q_pos = PAST_LEN + jnp.arange(T, dtype=jnp.int32)[None, :]
    k_pos = PAST_LEN - WINDOW + jnp.arange(WINDOW + T, dtype=jnp.int32)[None, :]
    o = banded_sink_attention(q.reshape(N, 1, T, N_KV_HEADS, GROUP, HEAD_DIM),
                              kc[:, None], vc[:, None], q_pos, k_pos, sinks)
    return o.reshape(N, T, ATTN_WIDTH), kc[:, -WINDOW:], vc[:, -WINDOW:]


def conformer_conv(u, gate, ctx, w, b, ln_g, ln_b):
    a = u * jax.nn.sigmoid(gate)
    padded = jnp.concatenate([ctx.astype(a.dtype), a], axis=1)
    y = lax.conv_general_dilated(padded, w[:, None, :], window_strides=(1,), padding='VALID',
                                 dimension_numbers=('NWC', 'WIO', 'NWC'),
                                 feature_group_count=CONV_WIDTH) + b
    y = jax.nn.silu(layer_norm(y, ln_g, ln_b))
    return y, padded[:, -(CONV_K - 1):]


def clamped_swiglu(h):
    x_glu = jnp.minimum(h[..., :D_FF], SWIGLU_LIMIT)
    x_lin = jnp.clip(h[..., D_FF:], -SWIGLU_LIMIT, SWIGLU_LIMIT)
    return x_glu * jax.nn.sigmoid(SWIGLU_ALPHA * x_glu) * (x_lin + 1.0)


def moe(x, router_w, router_b, w1, b1, w2, b2):
    T = x.shape[0]
    logits = (x @ router_w + router_b).astype(jnp.float32)
    top_vals, top_idx = lax.top_k(logits, TOP_K)
    gates = jax.nn.softmax(top_vals, axis=-1).astype(x.dtype)
    A = T * TOP_K
    e_flat = top_idx.reshape(A)
    tok_flat = jnp.arange(A, dtype=jnp.int32) // TOP_K
    g_flat = gates.reshape(A)
    order = jnp.argsort(e_flat)
    e_sorted = e_flat[order]
    counts = jnp.bincount(e_flat, length=N_EXPERTS)
    starts = jnp.cumsum(counts) - counts
    padded = (counts + MOE_BLOCK - 1) // MOE_BLOCK * MOE_BLOCK
    pad_ends = jnp.cumsum(padded)
    pad_starts = pad_ends - padded
    dest = pad_starts[e_sorted] + jnp.arange(A, dtype=jnp.int32) - starts[e_sorted]
    n_blocks = -(-(A + N_EXPERTS * (MOE_BLOCK - 1)) // MOE_BLOCK)
    P = n_blocks * MOE_BLOCK
    row_tok = jnp.full((P,), T, dtype=jnp.int32).at[dest].set(tok_flat[order])
    row_gate = jnp.zeros((P,), dtype=x.dtype).at[dest].set(g_flat[order])
    blk_exp = jnp.minimum(jnp.searchsorted(pad_ends, jnp.arange(n_blocks, dtype=jnp.int32) * MOE_BLOCK,
                                           side='right'), N_EXPERTS - 1).astype(jnp.int32)
    x_pad = jnp.concatenate([x, jnp.zeros((1, x.shape[1]), x.dtype)], axis=0)

    def run_block(args):
        tok, e = args
        h = x_pad[tok] @ w1[e] + b1[e]
        return clamped_swiglu(h) @ w2[e] + b2[e]

    out = lax.map(run_block, (row_tok.reshape(n_blocks, MOE_BLOCK), blk_exp))
    out = out.reshape(P, -1) * row_gate[:, None]
    return jnp.zeros((T + 1, x.shape[1]), x.dtype).at[row_tok].add(out)[:T]


def split_in(z):
    N, S = z.shape[0], z.shape[1]
    k0 = ATTN_WIDTH
    v0 = k0 + KV_WIDTH
    u0 = v0 + KV_WIDTH
    g0 = u0 + CONV_WIDTH
    q = z[..., :k0].reshape(N, S, N_Q_HEADS, HEAD_DIM)
    k = z[..., k0:v0].reshape(N, S, N_KV_HEADS, HEAD_DIM)
    v = z[..., v0:u0].reshape(N, S, N_KV_HEADS, HEAD_DIM)
    return q, k, v, z[..., u0:g0], z[..., g0:]


def setup_inputs(seed: int = 0) -> dict:
    key = jax.random.key(seed)
    ks = jax.random.split(key, 24)
    f32 = jnp.float32
    nrm = lambda k, shape, s: jax.random.normal(k, shape, f32) * s
    return {
        "x_prompt": nrm(ks[0], (BATCH, SEQ, D_MODEL), 1.0),
        "x_sample": nrm(ks[1], (DEC_BATCH, DEC_SEQ, D_MODEL), 1.0),
        "cache_k": nrm(ks[2], (DEPTH, DEC_BATCH, WINDOW, N_KV_HEADS, HEAD_DIM), 1.0),
        "cache_v": nrm(ks[3], (DEPTH, DEC_BATCH, WINDOW, N_KV_HEADS, HEAD_DIM), 1.0),
        "state_conv": nrm(ks[4], (DEPTH, DEC_BATCH, CONV_K - 1, CONV_WIDTH), 0.5),
        "attn_norm_g": 1.0 + nrm(ks[5], (DEPTH, D_MODEL), 0.01),
        "w_in": nrm(ks[6], (DEPTH, D_MODEL, IN_WIDTH), D_MODEL ** -0.5),
        "attn_sinks": nrm(ks[7], (DEPTH, N_Q_HEADS), 0.5),
        "conv_w": nrm(ks[8], (DEPTH, CONV_K, CONV_WIDTH), CONV_K ** -0.5),
        "conv_b": nrm(ks[9], (DEPTH, CONV_WIDTH), 0.01),
        "conv_ln_g": 1.0 + nrm(ks[10], (DEPTH, CONV_WIDTH), 0.01),
        "conv_ln_b": nrm(ks[11], (DEPTH, CONV_WIDTH), 0.01),
        "w_out": nrm(ks[12], (DEPTH, MIX_WIDTH, D_MODEL), MIX_WIDTH ** -0.5),
        "ffn_norm_g": 1.0 + nrm(ks[13], (DEPTH, D_MODEL), 0.01),
        "router_w": nrm(ks[14], (DEPTH, D_MODEL, N_EXPERTS), D_MODEL ** -0.5),
        "router_b": nrm(ks[15], (DEPTH, N_EXPERTS), 0.01),
        "w1": nrm(ks[16], (DEPTH, N_EXPERTS, D_MODEL, 2 * D_FF), D_MODEL ** -0.5),
        "b1": nrm(ks[17], (DEPTH, N_EXPERTS, 2 * D_FF), 0.01),
        "w2": nrm(ks[18], (DEPTH, N_EXPERTS, D_FF, D_MODEL), D_FF ** -0.5),
        "b2": nrm(ks[19], (DEPTH, N_EXPERTS, D_MODEL), 0.01),
        "final_norm_g": 1.0 + nrm(ks[20], (D_MODEL,), 0.01),
    }


def reference(x_prompt, x_sample, cache_k, cache_v, state_conv, attn_norm_g, w_in, attn_sinks,
              conv_w, conv_b, conv_ln_g, conv_ln_b, w_out, ffn_norm_g, router_w, router_b,
              w1, b1, w2, b2, final_norm_g):
    xp, xs = x_prompt, x_sample
    Bp, Sp = xp.shape[0], xp.shape[1]
    Bs, Ss = xs.shape[0], xs.shape[1]
    pos_p = jnp.arange(Sp, dtype=jnp.int32)
    pos_s = PAST_LEN + jnp.arange(Ss, dtype=jnp.int32)
    kp_l, vp_l, cp_l, ks_l, vs_l, cs_l = [], [], [], [], [], []
    for l in range(DEPTH):
        qp, kp, vp, up, gp = split_in(rms_norm(xp, attn_norm_g[l]) @ w_in[l])
        qs, ks_, vs_, us, gs = split_in(rms_norm(xs, attn_norm_g[l]) @ w_in[l])
        qp, kp = partial_rotary(qp, pos_p), partial_rotary(kp, pos_p)
        qs, ks_ = partial_rotary(qs, pos_s), partial_rotary(ks_, pos_s)
        att_p = prompt_window_attention(qp, kp, vp, attn_sinks[l])
        att_s, nk_s, nv_s = sample_window_attention(qs, ks_, vs_, cache_k[l], cache_v[l], attn_sinks[l])
        zero_ctx = jnp.zeros((Bp, CONV_K - 1, CONV_WIDTH), up.dtype)
        cv_p, nc_p = conformer_conv(up, gp, zero_ctx, conv_w[l], conv_b[l], conv_ln_g[l], conv_ln_b[l])
        cv_s, nc_s = conformer_conv(us, gs, state_conv[l], conv_w[l], conv_b[l], conv_ln_g[l], conv_ln_b[l])
        xp = xp + jnp.concatenate([att_p, cv_p], axis=-1) @ w_out[l]
        xs = xs + jnp.concatenate([att_s, cv_s], axis=-1) @ w_out[l]
        kp_l.append(kp[:, -WINDOW:]); vp_l.append(vp[:, -WINDOW:]); cp_l.append(nc_p)
        ks_l.append(nk_s); vs_l.append(nv_s); cs_l.append(nc_s)
        hf = jnp.concatenate([rms_norm(xp, ffn_norm_g[l]).reshape(Bp * Sp, D_MODEL),
                              rms_norm(xs, ffn_norm_g[l]).reshape(Bs * Ss, D_MODEL)], axis=0)
        mo = moe(hf, router_w[l], router_b[l], w1[l], b1[l], w2[l], b2[l])
        xp = xp + mo[:Bp * Sp].reshape(Bp, Sp, D_MODEL)
        xs = xs + mo[Bp * Sp:].reshape(Bs, Ss, D_MODEL)
    y_prompt = rms_norm(xp, final_norm_g)
    y_sample = rms_norm(xs, final_norm_g)
    return (y_prompt, y_sample, jnp.stack(kp_l), jnp.stack(vp_l), jnp.stack(cp_l),
            jnp.stack(ks_l), jnp.stack(vs_l), jnp.stack(cs_l))
```

```python
import functools

import jax
import jax.numpy as jnp
from jax import lax
from jax.experimental import pallas as pl
from jax.experimental.pallas import tpu as pltpu

F32 = jnp.float32
BF16 = jnp.bfloat16

HEAD_DIM = 64
N_Q_HEADS = 8
N_KV_HEADS = 2
WINDOW = 128
ROPE_THETA = 500000.0
ROPE_DIM = 16
CONV_K = 31
N_EXPERTS = 32
TOP_K = 4
SWIGLU_LIMIT = 7.0
SWIGLU_ALPHA = 1.702
RMS_EPS = 1e-5
LN_EPS = 1e-5
PAST_LEN = 16384

LANES = 128
CONV_HALO = 32
VMEM_LIMIT = 56 * 1024 * 1024

ATTN_WIDTH = N_Q_HEADS * HEAD_DIM
KV_WIDTH = N_KV_HEADS * HEAD_DIM


def _tile(n, pref, mult=8):
    t = min(pref, n)
    while t > 0 and (n % t or t % mult):
        t -= 1
    assert t > 0, (n, pref, mult)
    return t


def _cparams(sem):
    return pltpu.CompilerParams(dimension_semantics=sem, vmem_limit_bytes=VMEM_LIMIT)


def _sigmoid(x):
    return 1.0 / (1.0 + jnp.exp(-x))


def _rope_tables(pos):
    half = ROPE_DIM // 2
    inv_freq = jnp.power(jnp.float32(ROPE_THETA), -jnp.arange(half, dtype=F32) * 2.0 / ROPE_DIM)
    ang = pos.astype(F32)[:, None] * inv_freq[None, :]
    cos, sin = jnp.cos(ang), jnp.sin(ang)
    l64 = jnp.arange(LANES) % HEAD_DIM
    f = l64 % half
    cos_l, sin_l = cos[:, f], sin[:, f]
    c = jnp.where(l64 < ROPE_DIM, cos_l, 1.0)
    s1 = jnp.where(l64 < half, -sin_l, 0.0)
    s2 = jnp.where((l64 >= half) & (l64 < ROPE_DIM), sin_l, 0.0)
    return c.astype(F32), s1.astype(F32), s2.astype(F32)


def _inproj_kernel(x_ref, g_ref, w_ref, c_ref, s1_ref, s2_ref,
                   q_ref, k_ref, v_ref, kf_ref, vf_ref, a_ref, *, conv_width):
    x = x_ref[...]
    ms = jnp.mean(x * x, axis=-1, keepdims=True)
    h = (x * lax.rsqrt(ms + RMS_EPS) * g_ref[...]).astype(BF16)
    z = jnp.dot(h, w_ref[...], preferred_element_type=F32)
    c, s1, s2 = c_ref[...], s1_ref[...], s2_ref[...]
    half = ROPE_DIM // 2

    def rot(t):
        return t * c + pltpu.roll(t, LANES - half, 1) * s1 + pltpu.roll(t, half, 1) * s2

    scale = HEAD_DIM ** -0.5
    for j in range(ATTN_WIDTH // LANES):
        q_ref[:, j * LANES:(j + 1) * LANES] = (rot(z[:, j * LANES:(j + 1) * LANES]) * scale).astype(q_ref.dtype)
    k0 = ATTN_WIDTH
    kr = rot(z[:, k0:k0 + KV_WIDTH])
    k_ref[...] = kr.astype(BF16)
    kf_ref[...] = kr
    v0 = k0 + KV_WIDTH
    vv = z[:, v0:v0 + KV_WIDTH]
    v_ref[...] = vv.astype(BF16)
    vf_ref[...] = vv
    u0 = v0 + KV_WIDTH
    g0 = u0 + conv_width
    a_ref[...] = z[:, u0:g0] * _sigmoid(z[:, g0:g0 + conv_width])


def _in_proj(x2, g, w_bf, tables, *, seq_period, q_dtype, conv_width):
    n, d = x2.shape
    in_w = w_bf.shape[1]
    if seq_period is None:
        tm = tables[0].shape[0]
        tmap = lambda i: (0, 0)
    else:
        tm = _tile(seq_period, 512, 16)
        per = seq_period // tm
        tmap = lambda i: (i % per, 0)
    assert n % tm == 0
    row = lambda w: pl.BlockSpec((tm, w), lambda i: (i, 0))
    tab = pl.BlockSpec((tm, LANES), tmap)
    return pl.pallas_call(
        functools.partial(_inproj_kernel, conv_width=conv_width),
        grid=(n // tm,),
        in_specs=[row(d), pl.BlockSpec((1, d), lambda i: (0, 0)),
                  pl.BlockSpec((d, in_w), lambda i: (0, 0)), tab, tab, tab],
        out_specs=[row(ATTN_WIDTH), row(KV_WIDTH), row(KV_WIDTH), row(KV_WIDTH), row(KV_WIDTH),
                   row(conv_width)],
        out_shape=[jax.ShapeDtypeStruct((n, ATTN_WIDTH), q_dtype),
                   jax.ShapeDtypeStruct((n, KV_WIDTH), BF16),
                   jax.ShapeDtypeStruct((n, KV_WIDTH), BF16),
                   jax.ShapeDtypeStruct((n, KV_WIDTH), F32),
                   jax.ShapeDtypeStruct((n, KV_WIDTH), F32),
                   jax.ShapeDtypeStruct((n, conv_width), F32)],
        compiler_params=_cparams(("parallel",)),
        name="in_proj",
    )(x2, g, w_bf, *tables)


def _dup_head(t, h):
    sw = pltpu.roll(t, HEAD_DIM, 1)
    low = lax.broadcasted_iota(jnp.int32, t.shape, 1) < HEAD_DIM
    return jnp.where(low, t, sw) if h == 0 else jnp.where(low, sw, t)


def _nt_dot(a, b):
    return lax.dot_general(a, b, (((1,), (1,)), ((), ())), preferred_element_type=F32)


def _attn_prompt_kernel(sink_ref, q_ref, kc_ref, kp_ref, vc_ref, vp_ref, o_ref):
    j = pl.program_id(1)
    w = WINDOW
    k2 = jnp.concatenate([kp_ref[...], kc_ref[...]], axis=0).astype(F32)
    v2 = jnp.concatenate([vp_ref[...], vc_ref[...]], axis=0).astype(F32)
    r = lax.broadcasted_iota(jnp.int32, (w, 2 * w), 0)
    kk = lax.broadcasted_iota(jnp.int32, (w, 2 * w), 1)
    valid = (kk > r) & (kk <= r + w) & ((kk >= w) | (j > 0))
    low = lax.broadcasted_iota(jnp.int32, (w, LANES), 1) < HEAD_DIM
    zero = jnp.zeros((w, LANES), BF16)
    group = N_Q_HEADS // N_KV_HEADS
    for h in range(N_KV_HEADS):
        kd = _dup_head(k2, h).astype(BF16)
        vd = _dup_head(v2, h).astype(BF16)
        for jj in range(group // 2):
            col = (h * group // 2 + jj) * LANES
            qv = q_ref[:, col:col + LANES]
            halves = []
            for half in range(2):
                head = h * group + jj * 2 + half
                qm = jnp.where(low if half == 0 else ~low, qv, zero)
                s = jnp.where(valid, _nt_dot(qm, kd), -jnp.inf)
                sink = sink_ref[head]
                m = jnp.maximum(jnp.max(s, axis=-1, keepdims=True), sink)
                p = jnp.exp(s - m)
                den = jnp.sum(p, axis=-1, keepdims=True) + jnp.exp(sink - m)
                o = jnp.dot(p.astype(BF16), vd, preferred_element_type=F32)
                halves.append(o / den)
            o_ref[:, col:col + LANES] = jnp.where(low, halves[0], halves[1]).astype(o_ref.dtype)


def _attn_prompt(q, k, v, sinks, *, batch, seq):
    nb = seq // WINDOW
    cur = lambda b, j: (b * nb + j, 0)
    prev = lambda b, j: (b * nb + jnp.maximum(j - 1, 0), 0)
    kv = lambda m: pl.BlockSpec((WINDOW, KV_WIDTH), m)
    return pl.pallas_call(
        _attn_prompt_kernel,
        grid=(batch, nb),
        in_specs=[pl.BlockSpec(memory_space=pltpu.SMEM),
                  pl.BlockSpec((WINDOW, ATTN_WIDTH), cur), kv(cur), kv(prev), kv(cur), kv(prev)],
        out_specs=pl.BlockSpec((WINDOW, ATTN_WIDTH), cur),
        out_shape=jax.ShapeDtypeStruct((batch * seq, ATTN_WIDTH), BF16),
        compiler_params=_cparams(("parallel", "parallel")),
        name="attn_prompt",
    )(sinks, q, k, k, v, v)


def _attn_sample_kernel(sink_ref, q_ref, kn_ref, vn_ref, ck_ref, cv_ref, o_ref, nk_ref, nv_ref, *, gb, ts):
    w = WINDOW
    group = N_Q_HEADS // N_KV_HEADS
    rows = group * ts
    low = lax.broadcasted_iota(jnp.int32, (ts, LANES), 1) < HEAD_DIM
    t_row = lax.broadcasted_iota(jnp.int32, (rows, 1), 0) % ts
    g_row = lax.broadcasted_iota(jnp.int32, (rows, 1), 0) // ts
    c_idx = lax.broadcasted_iota(jnp.int32, (rows, w), 1)
    n_idx = lax.broadcasted_iota(jnp.int32, (rows, 2 * ts), 1)
    valid_c = c_idx > t_row
    valid_n = n_idx <= t_row
    pad = jnp.zeros((ts, LANES), F32)
    outs = []
    for b in range(gb):
        kc, vc = ck_ref[b], cv_ref[b]
        kn, vn = kn_ref[b * ts:(b + 1) * ts, :], vn_ref[b * ts:(b + 1) * ts, :]
        nk_ref[b, 0:w - ts, :] = kc[ts:, :]
        nk_ref[b, w - ts:, :] = kn
        nv_ref[b, 0:w - ts, :] = vc[ts:, :]
        nv_ref[b, w - ts:, :] = vn
        knp = jnp.concatenate([kn, pad], axis=0)
        vnp = jnp.concatenate([vn, pad], axis=0)
        qb = q_ref[b * ts:(b + 1) * ts, :]
        cols = []
        for h in range(N_KV_HEADS):
            parts = []
            for jj in range(group // 2):
                col = (h * group // 2 + jj) * LANES
                qv = qb[:, col:col + LANES]
                parts += [jnp.where(low, qv, 0.0), jnp.where(low, 0.0, qv)]
            lhs = jnp.concatenate(parts, axis=0).astype(BF16)
            sink = jnp.zeros((rows, 1), F32)
            for g in range(group):
                sink = jnp.where(g_row == g, sink_ref[h * group + g], sink)
            s_c = jnp.where(valid_c, _nt_dot(lhs, _dup_head(kc, h).astype(BF16)), -jnp.inf)
            s_n = jnp.where(valid_n, _nt_dot(lhs, _dup_head(knp, h).astype(BF16)), -jnp.inf)
            m = jnp.maximum(jnp.maximum(jnp.max(s_c, axis=-1, keepdims=True),
                                        jnp.max(s_n, axis=-1, keepdims=True)), sink)
            p_c = jnp.exp(s_c - m)
            p_n = jnp.exp(s_n - m)
            den = (jnp.sum(p_c, axis=-1, keepdims=True) + jnp.sum(p_n, axis=-1, keepdims=True)
                   + jnp.exp(sink - m))
            o = (jnp.dot(p_c.astype(BF16), _dup_head(vc, h).astype(BF16), preferred_element_type=F32)
                 + jnp.dot(p_n.astype(BF16), _dup_head(vnp, h).astype(BF16), preferred_element_type=F32)) / den
            for jj in range(group // 2):
                lo_part = o[(2 * jj) * ts:(2 * jj + 1) * ts, :]
                hi_part = o[(2 * jj + 1) * ts:(2 * jj + 2) * ts, :]
                cols.append(jnp.where(low, lo_part, hi_part))
        outs.append(jnp.concatenate(cols, axis=1))
    o_ref[...] = jnp.concatenate(outs, axis=0).astype(o_ref.dtype)


def _attn_sample(q, kf, vf, cache_k, cache_v, sinks, *, batch, ts):
    assert ts % 8 == 0 and ts <= WINDOW
    gb = _tile(batch, 8, 2)
    tok = lambda w: pl.BlockSpec((gb * ts, w), lambda i: (i, 0))
    cache = pl.BlockSpec((gb, WINDOW, KV_WIDTH), lambda i: (i, 0, 0))
    cshape = jax.ShapeDtypeStruct((batch, WINDOW, KV_WIDTH), F32)
    return pl.pallas_call(
        functools.partial(_attn_sample_kernel, gb=gb, ts=ts),
        grid=(batch // gb,),
        in_specs=[pl.BlockSpec(memory_space=pltpu.SMEM), tok(ATTN_WIDTH), tok(KV_WIDTH), tok(KV_WIDTH),
                  cache, cache],
        out_specs=[tok(ATTN_WIDTH), cache, cache],
        out_shape=[jax.ShapeDtypeStruct((batch * ts, ATTN_WIDTH), BF16), cshape, cshape],
        compiler_params=_cparams(("parallel",)),
        name="attn_sample",
    )(sinks, q, kf, vf, cache_k, cache_v)


def _ln_swish(acc, b, lg, lb):
    y = acc + b
    mu = jnp.mean(y, axis=-1, keepdims=True)
    yc = y - mu
    var = jnp.mean(yc * yc, axis=-1, keepdims=True)
    yn = yc * lax.rsqrt(var + LN_EPS) * lg + lb
    return yn * _sigmoid(yn)


def _conv_prompt_kernel(a_ref, ap_ref, w_ref, b_ref, lg_ref, lb_ref, o_ref, win_ref, *, tt, rc):
    j = pl.program_id(1)
    win_ref[0:CONV_HALO, :] = jnp.where(j > 0, ap_ref[...], 0.0)
    win_ref[CONV_HALO:, :] = a_ref[...]
    off = CONV_HALO - (CONV_K - 1)
    b, lg, lb = b_ref[...], lg_ref[...], lb_ref[...]
    for c in range(tt // rc):
        acc = jnp.zeros((rc, a_ref.shape[1]), F32)
        for k in range(CONV_K):
            s = c * rc + off + k
            acc = acc + w_ref[k:k + 1, :] * win_ref[s:s + rc, :]
        o_ref[c * rc:(c + 1) * rc, :] = _ln_swish(acc, b, lg, lb).astype(o_ref.dtype)


def _conv_prompt(a, w, b, lg, lb, *, batch, seq):
    cw = a.shape[1]
    tt = _tile(seq, 256, CONV_HALO)
    rc = _tile(tt, 32, 16)
    nt = seq // tt
    per = tt // CONV_HALO
    cur = lambda bb, j: (bb * nt + j, 0)
    prev = lambda bb, j: (jnp.maximum((bb * nt + j) * per - 1, 0), 0)
    vec = pl.BlockSpec((1, cw), lambda bb, j: (0, 0))
    return pl.pallas_call(
        functools.partial(_conv_prompt_kernel, tt=tt, rc=rc),
        grid=(batch, nt),
        in_specs=[pl.BlockSpec((tt, cw), cur), pl.BlockSpec((CONV_HALO, cw), prev),
                  pl.BlockSpec((CONV_K, cw), lambda bb, j: (0, 0)), vec, vec, vec],
        out_specs=pl.BlockSpec((tt, cw), cur),
        out_shape=jax.ShapeDtypeStruct((batch * seq, cw), BF16),
        scratch_shapes=[pltpu.VMEM((CONV_HALO + tt, cw), F32)],
        compiler_params=_cparams(("parallel", "parallel")),
        name="conv_prompt",
    )(a, a, w, b, lg, lb)


def _conv_sample_kernel(a_ref, st_ref, w_ref, b_ref, lg_ref, lb_ref, o_ref, win_ref, *, gb, ts):
    ctx = CONV_K - 1
    b, lg, lb = b_ref[...], lg_ref[...], lb_ref[...]
    for bb in range(gb):
        win_ref[bb, 0:ctx, :] = st_ref[bb]
        win_ref[bb, ctx:ctx + ts, :] = a_ref[bb * ts:(bb + 1) * ts, :]
    outs = []
    for bb in range(gb):
        acc = jnp.zeros((ts, a_ref.shape[1]), F32)
        for k in range(CONV_K):
            acc = acc + w_ref[k:k + 1, :] * win_ref[bb, k:k + ts, :]
        outs.append(_ln_swish(acc, b, lg, lb))
    o_ref[...] = jnp.concatenate(outs, axis=0).astype(o_ref.dtype)


def _conv_sample(a, state, w, b, lg, lb, *, batch, ts):
    cw = a.shape[1]
    ctx = CONV_K - 1
    gb = _tile(batch, 8, 2)
    vec = pl.BlockSpec((1, cw), lambda i: (0, 0))
    return pl.pallas_call(
        functools.partial(_conv_sample_kernel, gb=gb, ts=ts),
        grid=(batch // gb,),
        in_specs=[pl.BlockSpec((gb * ts, cw), lambda i: (i, 0)),
                  pl.BlockSpec((gb, ctx, cw), lambda i: (i, 0, 0)),
                  pl.BlockSpec((CONV_K, cw), lambda i: (0, 0)), vec, vec, vec],
        out_specs=pl.BlockSpec((gb * ts, cw), lambda i: (i, 0)),
        out_shape=jax.ShapeDtypeStruct((batch * ts, cw), BF16),
        scratch_shapes=[pltpu.VMEM((gb, ctx + ts + 2, cw), F32)],
        compiler_params=_cparams(("parallel",)),
        name="conv_sample",
    )(a, state, w, b, lg, lb)


def _outproj_router_kernel(x_ref, att_ref, cv_ref, wo_ref, g_ref, rw_ref, rb_ref, cin_ref,
                           x1_ref, hn_ref, route_ref, cnt_ref, carry_ref):
    i = pl.program_id(0)

    @pl.when(i == 0)
    def _():
        carry_ref[...] = cin_ref[...]

    aw = att_ref.shape[1]
    mix = (jnp.dot(att_ref[...], wo_ref[0:aw, :], preferred_element_type=F32)
           + jnp.dot(cv_ref[...], wo_ref[aw:, :], preferred_element_type=F32))
    x1 = x_ref[...] + mix
    x1_ref[...] = x1
    ms = jnp.mean(x1 * x1, axis=-1, keepdims=True)
    hn = x1 * lax.rsqrt(ms + RMS_EPS) * g_ref[...]
    hn_ref[...] = hn
    logits = jnp.dot(hn.astype(BF16), rw_ref[...], preferred_element_type=F32) + rb_ref[...]
    tm = logits.shape[0]
    lane = lax.broadcasted_iota(jnp.int32, (tm, LANES), 1)
    onehot = jnp.zeros((tm, LANES), F32)
    vals, idxs = [], []
    for _ in range(TOP_K):
        m = jnp.max(logits, axis=-1, keepdims=True)
        idx = jnp.min(jnp.where(logits == m, lane, LANES), axis=-1, keepdims=True)
        sel = lane == idx
        onehot = onehot + sel.astype(F32)
        logits = jnp.where(sel, -jnp.inf, logits)
        vals.append(m)
        idxs.append(idx)
    es = [jnp.exp(v - vals[0]) for v in vals]
    den = es[0] + es[1] + es[2] + es[3]
    r = lax.broadcasted_iota(jnp.int32, (tm, tm), 0)
    c = lax.broadcasted_iota(jnp.int32, (tm, tm), 1)
    before = (c < r).astype(BF16)
    prefix = jnp.dot(before, onehot.astype(BF16), preferred_element_type=F32) + carry_ref[...]
    route = jnp.zeros((tm, LANES), F32)
    for k in range(TOP_K):
        rank = jnp.sum(jnp.where(lane == idxs[k], prefix, 0.0), axis=-1, keepdims=True)
        route = jnp.where(lane == k, idxs[k].astype(F32), route)
        route = jnp.where(lane == TOP_K + k, es[k] / den, route)
        route = jnp.where(lane == 2 * TOP_K + k, rank, route)
    route_ref[...] = route
    carry = carry_ref[...] + jnp.sum(onehot, axis=0, keepdims=True)
    carry_ref[...] = carry
    cnt_ref[...] = carry


def _outproj_router(x2, att, cv, wo_bf, g, rw_bf, rb, carry_in):
    n, d = x2.shape
    tm = _tile(n, 256, 16)
    row = lambda w: pl.BlockSpec((tm, w), lambda i: (i, 0))
    full = lambda a: pl.BlockSpec(a.shape, lambda i: (0,) * a.ndim)
    return pl.pallas_call(
        _outproj_router_kernel,
        grid=(n // tm,),
        in_specs=[row(d), row(att.shape[1]), row(cv.shape[1]), full(wo_bf), full(g), full(rw_bf), full(rb),
                  full(carry_in)],
        out_specs=[row(d), row(d), row(LANES), pl.BlockSpec((1, LANES), lambda i: (0, 0))],
        out_shape=[jax.ShapeDtypeStruct((n, d), F32), jax.ShapeDtypeStruct((n, d), F32),
                   jax.ShapeDtypeStruct((n, LANES), F32), jax.ShapeDtypeStruct((1, LANES), F32)],
        scratch_shapes=[pltpu.VMEM((1, LANES), F32)],
        compiler_params=_cparams(("arbitrary",)),
        name="outproj_router",
    )(x2, att, cv, wo_bf, g, rw_bf, rb, carry_in)


def _dispatch_kernel(fill_ref, nfill_ref, dest_ref, hn_ref, xs_ref, zero_ref, sem, zsem, *, td, tme):
    @pl.when(pl.program_id(0) == 0)
    def _():
        zero_ref[...] = jnp.zeros(zero_ref.shape, F32)

        def fill(f):
            row = pl.multiple_of(fill_ref[f] * tme, tme)
            return pltpu.make_async_copy(zero_ref, xs_ref.at[pl.ds(row, tme)], zsem)

        def start(f, carry):
            fill(f).start()
            return carry

        def wait(f, carry):
            fill(f).wait()
            return carry

        lax.fori_loop(0, nfill_ref[0], start, 0)
        lax.fori_loop(0, nfill_ref[0], wait, 0)

    def issue(r, carry):
        for k in range(TOP_K):
            d = dest_ref[0, 0, r * TOP_K + k]
            pltpu.make_async_copy(hn_ref.at[pl.ds(r, 1)], xs_ref.at[pl.ds(d, 1)], sem).start()
        return carry

    lax.fori_loop(0, td, issue, 0, unroll=4)
    for _ in range(TOP_K):
        pltpu.make_async_copy(hn_ref, xs_ref.at[pl.ds(0, td)], sem).wait()


def _dispatch(hn, dest, fill_blocks, n_fill, *, nb, tme):
    n, d = hn.shape
    td = _tile(n, 256, 8)
    dest3 = dest.reshape(n // td, 1, td * TOP_K)
    grid_spec = pltpu.PrefetchScalarGridSpec(
        num_scalar_prefetch=2,
        grid=(n // td,),
        in_specs=[pl.BlockSpec((1, 1, td * TOP_K), lambda i, fb, nf: (i, 0, 0), memory_space=pltpu.SMEM),
                  pl.BlockSpec((td, d), lambda i, fb, nf: (i, 0))],
        out_specs=pl.BlockSpec(memory_space=pl.ANY),
        scratch_shapes=[pltpu.VMEM((tme, d), F32), pltpu.SemaphoreType.DMA(()), pltpu.SemaphoreType.DMA(())],
    )
    return pl.pallas_call(
        functools.partial(_dispatch_kernel, td=td, tme=tme),
        grid_spec=grid_spec,
        out_shape=jax.ShapeDtypeStruct((nb * tme, d), F32),
        compiler_params=_cparams(("arbitrary",)),
        name="dispatch",
    )(fill_blocks, n_fill, dest3, hn)


def _experts_kernel(be_ref, bsrc_ref, nu_ref, x_ref, w1_ref, b1_ref, w2_ref, b2_ref,
                    y_ref, w1b_ref, w2b_ref):
    i = pl.program_id(0)
    e = be_ref[i]
    e_prev = be_ref[jnp.maximum(i - 1, 0)]
    d_ff = w2_ref.shape[1]

    @pl.when((i == 0) | (e != e_prev))
    def _():
        w1b_ref[...] = w1_ref[0].astype(BF16)
        w2b_ref[...] = w2_ref[0].astype(BF16)

    @pl.when(i < nu_ref[0])
    def _():
        x = x_ref[...].astype(BF16)
        h = jnp.dot(x, w1b_ref[...], preferred_element_type=F32) + b1_ref[0]
        x_glu = jnp.minimum(h[:, :d_ff], SWIGLU_LIMIT)
        x_lin = jnp.clip(h[:, d_ff:], -SWIGLU_LIMIT, SWIGLU_LIMIT)
        act = x_glu * _sigmoid(SWIGLU_ALPHA * x_glu) * (x_lin + 1.0)
        y_ref[...] = jnp.dot(act.astype(BF16), w2b_ref[...], preferred_element_type=F32) + b2_ref[0]

    @pl.when(i >= nu_ref[0])
    def _():
        y_ref[...] = jnp.zeros(y_ref.shape, F32)


def _experts(xs, w1, b1, w2, b2, blk_exp, blk_src, n_used, *, tme):
    p_rows, d = xs.shape
    n_exp, _, h2 = w1.shape
    d_ff = w2.shape[1]
    nb = p_rows // tme
    grid_spec = pltpu.PrefetchScalarGridSpec(
        num_scalar_prefetch=3,
        grid=(nb,),
        in_specs=[pl.BlockSpec((tme, d), lambda i, be, bs, nu: (bs[i], 0)),
                  pl.BlockSpec((1, d, h2), lambda i, be, bs, nu: (be[i], 0, 0)),
                  pl.BlockSpec((1, 1, h2), lambda i, be, bs, nu: (be[i], 0, 0)),
                  pl.BlockSpec((1, d_ff, d), lambda i, be, bs, nu: (be[i], 0, 0)),
                  pl.BlockSpec((1, 1, d), lambda i, be, bs, nu: (be[i], 0, 0))],
        out_specs=pl.BlockSpec((tme, d), lambda i, be, bs, nu: (i, 0)),
        scratch_shapes=[pltpu.VMEM((d, h2), BF16), pltpu.VMEM((d_ff, d), BF16)],
    )
    return pl.pallas_call(
        _experts_kernel,
        grid_spec=grid_spec,
        out_shape=jax.ShapeDtypeStruct((p_rows, d), F32),
        compiler_params=_cparams(("arbitrary",)),
        name="experts",
    )(blk_exp, blk_src, n_used, xs, w1, b1.reshape(n_exp, 1, h2), w2, b2.reshape(n_exp, 1, d))


def _combine_kernel(dest_ref, x1_ref, route_ref, g_ref, ys_ref, o_ref, buf_ref, sem, *, tc):
    def issue(r, carry):
        for k in range(TOP_K):
            d = dest_ref[0, 0, r * TOP_K + k]
            pltpu.make_async_copy(ys_ref.at[pl.ds(d, 1)], buf_ref.at[k, pl.ds(r, 1)], sem).start()
        return carry

    lax.fori_loop(0, tc, issue, 0, unroll=4)
    for k in range(TOP_K):
        pltpu.make_async_copy(ys_ref.at[pl.ds(0, tc)], buf_ref.at[k], sem).wait()
    route = route_ref[...]
    y = x1_ref[...]
    for k in range(TOP_K):
        y = y + route[:, TOP_K + k:TOP_K + k + 1] * buf_ref[k]
    ms = jnp.mean(y * y, axis=-1, keepdims=True)
    o_ref[...] = y * lax.rsqrt(ms + RMS_EPS) * g_ref[...]


def _combine(x1, route, dest, ys, g):
    n, d = x1.shape
    tc = _tile(n, 128, 8)
    dest3 = dest.reshape(n // tc, 1, tc * TOP_K)
    return pl.pallas_call(
        functools.partial(_combine_kernel, tc=tc),
        grid=(n // tc,),
        in_specs=[pl.BlockSpec((1, 1, tc * TOP_K), lambda i: (i, 0, 0), memory_space=pltpu.SMEM),
                  pl.BlockSpec((tc, d), lambda i: (i, 0)),
                  pl.BlockSpec((tc, LANES), lambda i: (i, 0)),
                  pl.BlockSpec((1, d), lambda i: (0, 0)),
                  pl.BlockSpec(memory_space=pl.ANY)],
        out_specs=pl.BlockSpec((tc, d), lambda i: (i, 0)),
        out_shape=jax.ShapeDtypeStruct((n, d), F32),
        scratch_shapes=[pltpu.VMEM((TOP_K, tc, d), F32), pltpu.SemaphoreType.DMA(())],
        compiler_params=_cparams(("arbitrary",)),
        name="combine",
    )(dest3, x1, route, g, ys)


def _routing_tables(route, counts_f, *, tme, nb):
    eidx = route[:, 0:TOP_K].astype(jnp.int32)
    rank = route[:, 2 * TOP_K:3 * TOP_K].astype(jnp.int32)
    counts = counts_f[0, :N_EXPERTS].astype(jnp.int32)
    nblk = (counts + tme - 1) // tme
    blk_end = jnp.cumsum(nblk)
    blk_start = blk_end - nblk
    n_used = blk_end[-1]
    dest = blk_start[eidx] * tme + rank
    b = jnp.arange(nb, dtype=jnp.int32)
    used = b < n_used
    blk_exp = jnp.minimum(jnp.sum((b[:, None] >= blk_end[None, :]).astype(jnp.int32), axis=1), N_EXPERTS - 1)
    last_exp = jnp.max(jnp.where(nblk > 0, jnp.arange(N_EXPERTS, dtype=jnp.int32), 0))
    blk_exp = jnp.where(used, blk_exp, last_exp).astype(jnp.int32)
    blk_src = jnp.where(used, b, 0).astype(jnp.int32)
    partial = used & (b == blk_end[blk_exp] - 1) & (counts[blk_exp] % tme != 0)
    needs_fill = partial | ~used
    fill_blocks = jnp.argsort(~needs_fill, stable=True).astype(jnp.int32)
    n_fill = jnp.sum(needs_fill).reshape(1).astype(jnp.int32)
    return dest.astype(jnp.int32), blk_exp, blk_src, n_used.reshape(1).astype(jnp.int32), fill_blocks, n_fill


def kernel(x_prompt, x_sample, cache_k, cache_v, state_conv, attn_norm_g, w_in, attn_sinks, conv_w, conv_b,
           conv_ln_g, conv_ln_b, w_out, ffn_norm_g, router_w, router_b, w1, b1, w2, b2, final_norm_g):
    depth = w_in.shape[0]
    assert depth == 1, "single-layer step"
    bp, sp, d = x_prompt.shape
    bs, ss, _ = x_sample.shape
    cw = conv_w.shape[2]
    np_, ns = bp * sp, bs * ss
    n_tok = np_ + ns
    assert sp % WINDOW == 0

    xp2 = x_prompt.reshape(np_, d)
    xs2 = x_sample.reshape(ns, d)
    w_in_bf = w_in[0].astype(BF16)
    w_out_bf = w_out[0].astype(BF16)
    g_attn = attn_norm_g[0].reshape(1, d)
    g_ffn = ffn_norm_g[0].reshape(1, d)
    sinks = attn_sinks[0]
    vec = lambda a: a.reshape(1, cw)

    tab_p = _rope_tables(jnp.arange(sp, dtype=jnp.int32))
    tms = _tile(ns, 512, max(ss, 16))
    tab_s = _rope_tables(PAST_LEN + (jnp.arange(tms, dtype=jnp.int32) % ss))
    qp, kp, vp, kfp, vfp, ap = _in_proj(xp2, g_attn, w_in_bf, tab_p, seq_period=sp, q_dtype=BF16, conv_width=cw)
    qs, _, _, kfs, vfs, as_ = _in_proj(xs2, g_attn, w_in_bf, tab_s, seq_period=None, q_dtype=F32, conv_width=cw)

    att_p = _attn_prompt(qp, kp, vp, sinks, batch=bp, seq=sp)
    ck = cache_k[0].reshape(bs, WINDOW, KV_WIDTH)
    cv_ = cache_v[0].reshape(bs, WINDOW, KV_WIDTH)
    att_s, nk_s, nv_s = _attn_sample(qs, kfs, vfs, ck, cv_, sinks, batch=bs, ts=ss)

    cv_p = _conv_prompt(ap, conv_w[0], vec(conv_b[0]), vec(conv_ln_g[0]), vec(conv_ln_b[0]), batch=bp, seq=sp)
    cv_s = _conv_sample(as_, state_conv[0], conv_w[0], vec(conv_b[0]), vec(conv_ln_g[0]), vec(conv_ln_b[0]),
                        batch=bs, ts=ss)

    rw_bf = jnp.zeros((d, LANES), BF16).at[:, :N_EXPERTS].set(router_w[0].astype(BF16))
    rb = jnp.full((1, LANES), -jnp.inf, F32).at[0, :N_EXPERTS].set(router_b[0])
    zero_carry = jnp.zeros((1, LANES), F32)
    x1p, hnp, route_p, cnt_p = _outproj_router(xp2, att_p, cv_p, w_out_bf, g_ffn, rw_bf, rb, zero_carry)
    x1s, hns, route_s, cnt = _outproj_router(xs2, att_s, cv_s, w_out_bf, g_ffn, rw_bf, rb, cnt_p)

    tme = 256
    nb = -(-(n_tok * TOP_K + N_EXPERTS * (tme - 1)) // tme)
    route = jnp.concatenate([route_p, route_s], axis=0)
    hn = jnp.concatenate([hnp, hns], axis=0)
    dest, blk_exp, blk_src, n_used, fill_blocks, n_fill = _routing_tables(route, cnt, tme=tme, nb=nb)
    xs_sorted = _dispatch(hn, dest, fill_blocks, n_fill, nb=nb, tme=tme)
    ys = _experts(xs_sorted, w1[0], b1[0], w2[0], b2[0], blk_exp, blk_src, n_used, tme=tme)
    g_fin = final_norm_g.reshape(1, d)
    y_p = _combine(x1p, route_p, dest[:np_], ys, g_fin)
    y_s = _combine(x1s, route_s, dest[np_:], ys, g_fin)

    kv5 = lambda t, bb: t.reshape(bb, -1, N_KV_HEADS, HEAD_DIM)
    new_k_p = kv5(kfp, bp)[:, -WINDOW:][None]
    new_v_p = kv5(vfp, bp)[:, -WINDOW:][None]
    ctx = CONV_K - 1
    new_c_p = ap.reshape(bp, sp, cw)[:, -ctx:][None]
    new_c_s = jnp.concatenate([state_conv[0], as_.reshape(bs, ss, cw)], axis=1)[:, -ctx:][None]
    return (y_p.reshape(bp, sp, d), y_s.reshape(bs, ss, d), new_k_p, new_v_p, new_c_p,
            kv5(nk_s, bs)[None], kv5(nv_s, bs)[None], new_c_s)
```

```python
import functools

import jax
import jax.numpy as jnp
from jax import lax
from jax.experimental import pallas as pl
from jax.experimental.pallas import tpu as pltpu

F32 = jnp.float32
BF16 = jnp.bfloat16

HEAD_DIM = 64
N_Q_HEADS = 8
N_KV_HEADS = 2
WINDOW = 128
ROPE_THETA = 500000.0
ROPE_DIM = 16
CONV_K = 31
N_EXPERTS = 32
TOP_K = 4
SWIGLU_LIMIT = 7.0
SWIGLU_ALPHA = 1.702
RMS_EPS = 1e-5
LN_EPS = 1e-5
PAST_LEN = 16384

LANES = 128
SUBLANES = 8
CONV_HALO = 32
VMEM_LIMIT = 56 * 1024 * 1024
EXPERT_BLOCK_ROWS = 512

ATTN_WIDTH = N_Q_HEADS * HEAD_DIM
KV_WIDTH = N_KV_HEADS * HEAD_DIM


def _tile(n, pref, mult=8):
    t = min(pref, n)
    while t > 0 and (n % t or t % mult):
        t -= 1
    assert t > 0, (n, pref, mult)
    return t


def _cparams(sem):
    return pltpu.CompilerParams(dimension_semantics=sem, vmem_limit_bytes=VMEM_LIMIT)


def _sigmoid(x):
    return 1.0 / (1.0 + jnp.exp(-x))


def _rope_tables(pos):
    half = ROPE_DIM // 2
    inv_freq = jnp.power(jnp.float32(ROPE_THETA), -jnp.arange(half, dtype=F32) * 2.0 / ROPE_DIM)
    ang = pos.astype(F32)[:, None] * inv_freq[None, :]
    cos, sin = jnp.cos(ang), jnp.sin(ang)
    l64 = jnp.arange(LANES) % HEAD_DIM
    f = l64 % half
    cos_l, sin_l = cos[:, f], sin[:, f]
    c = jnp.where(l64 < ROPE_DIM, cos_l, 1.0)
    s1 = jnp.where(l64 < half, -sin_l, 0.0)
    s2 = jnp.where((l64 >= half) & (l64 < ROPE_DIM), sin_l, 0.0)
    return c.astype(F32), s1.astype(F32), s2.astype(F32)


def _inproj_kernel(x_ref, g_ref, w_ref, c_ref, s1_ref, s2_ref,
                   q_ref, k_ref, v_ref, kf_ref, vf_ref, a_ref, *, conv_width):
    x = x_ref[...]
    ms = jnp.mean(x * x, axis=-1, keepdims=True)
    h = (x * lax.rsqrt(ms + RMS_EPS) * g_ref[...]).astype(BF16)
    z = jnp.dot(h, w_ref[...], preferred_element_type=F32)
    c, s1, s2 = c_ref[...], s1_ref[...], s2_ref[...]
    half = ROPE_DIM // 2

    def rot(t):
        return t * c + pltpu.roll(t, LANES - half, 1) * s1 + pltpu.roll(t, half, 1) * s2

    scale = HEAD_DIM ** -0.5
    for j in range(ATTN_WIDTH // LANES):
        q_ref[:, j * LANES:(j + 1) * LANES] = (rot(z[:, j * LANES:(j + 1) * LANES]) * scale).astype(q_ref.dtype)
    k0 = ATTN_WIDTH
    kr = rot(z[:, k0:k0 + KV_WIDTH])
    k_ref[...] = kr.astype(BF16)
    kf_ref[...] = kr
    v0 = k0 + KV_WIDTH
    vv = z[:, v0:v0 + KV_WIDTH]
    v_ref[...] = vv.astype(BF16)
    vf_ref[...] = vv
    u0 = v0 + KV_WIDTH
    g0 = u0 + conv_width
    a_ref[...] = z[:, u0:g0] * _sigmoid(z[:, g0:g0 + conv_width])


def _in_proj(x2, g, w_bf, tables, *, seq_period, q_dtype, conv_width):
    n, d = x2.shape
    in_w = w_bf.shape[1]
    if seq_period is None:
        tm = tables[0].shape[0]
        tmap = lambda i: (0, 0)
    else:
        tm = _tile(seq_period, 512, 16)
        per = seq_period // tm
        tmap = lambda i: (i % per, 0)
    assert n % tm == 0
    row = lambda w: pl.BlockSpec((tm, w), lambda i: (i, 0))
    tab = pl.BlockSpec((tm, LANES), tmap)
    return pl.pallas_call(
        functools.partial(_inproj_kernel, conv_width=conv_width),
        grid=(n // tm,),
        in_specs=[row(d), pl.BlockSpec((1, d), lambda i: (0, 0)),
                  pl.BlockSpec((d, in_w), lambda i: (0, 0)), tab, tab, tab],
        out_specs=[row(ATTN_WIDTH), row(KV_WIDTH), row(KV_WIDTH), row(KV_WIDTH), row(KV_WIDTH),
                   row(conv_width)],
        out_shape=[jax.ShapeDtypeStruct((n, ATTN_WIDTH), q_dtype),
                   jax.ShapeDtypeStruct((n, KV_WIDTH), BF16),
                   jax.ShapeDtypeStruct((n, KV_WIDTH), BF16),
                   jax.ShapeDtypeStruct((n, KV_WIDTH), F32),
                   jax.ShapeDtypeStruct((n, KV_WIDTH), F32),
                   jax.ShapeDtypeStruct((n, conv_width), F32)],
        compiler_params=_cparams(("parallel",)),
        name="in_proj",
    )(x2, g, w_bf, *tables)


def _dup_head(t, h):
    sw = pltpu.roll(t, HEAD_DIM, 1)
    low = lax.broadcasted_iota(jnp.int32, t.shape, 1) < HEAD_DIM
    return jnp.where(low, t, sw) if h == 0 else jnp.where(low, sw, t)


def _nt_dot(a, b):
    return lax.dot_general(a, b, (((1,), (1,)), ((), ())), preferred_element_type=F32)


def _attn_prompt_kernel(sink_ref, q_ref, kc_ref, kp_ref, vc_ref, vp_ref, o_ref):
    j = pl.program_id(1)
    w = WINDOW
    k2 = jnp.concatenate([kp_ref[...], kc_ref[...]], axis=0).astype(F32)
    v2 = jnp.concatenate([vp_ref[...], vc_ref[...]], axis=0).astype(F32)
    r = lax.broadcasted_iota(jnp.int32, (w, 2 * w), 0)
    kk = lax.broadcasted_iota(jnp.int32, (w, 2 * w), 1)
    valid = (kk > r) & (kk <= r + w) & ((kk >= w) | (j > 0))
    low = lax.broadcasted_iota(jnp.int32, (w, LANES), 1) < HEAD_DIM
    zero = jnp.zeros((w, LANES), BF16)
    group = N_Q_HEADS // N_KV_HEADS
    for h in range(N_KV_HEADS):
        kd = _dup_head(k2, h).astype(BF16)
        vd = _dup_head(v2, h).astype(BF16)
        for jj in range(group // 2):
            col = (h * group // 2 + jj) * LANES
            qv = q_ref[:, col:col + LANES]
            halves = []
            for half in range(2):
                head = h * group + jj * 2 + half
                qm = jnp.where(low if half == 0 else ~low, qv, zero)
                s = jnp.where(valid, _nt_dot(qm, kd), -jnp.inf)
                sink = sink_ref[head]
                m = jnp.maximum(jnp.max(s, axis=-1, keepdims=True), sink)
                p = jnp.exp(s - m)
                den = jnp.sum(p, axis=-1, keepdims=True) + jnp.exp(sink - m)
                o = jnp.dot(p.astype(BF16), vd, preferred_element_type=F32)
                halves.append(o / den)
            o_ref[:, col:col + LANES] = jnp.where(low, halves[0], halves[1]).astype(o_ref.dtype)


def _attn_prompt(q, k, v, sinks, *, batch, seq):
    nb = seq // WINDOW
    cur = lambda b, j: (b * nb + j, 0)
    prev = lambda b, j: (b * nb + jnp.maximum(j - 1, 0), 0)
    kv = lambda m: pl.BlockSpec((WINDOW, KV_WIDTH), m)
    return pl.pallas_call(
        _attn_prompt_kernel,
        grid=(batch, nb),
        in_specs=[pl.BlockSpec(memory_space=pltpu.SMEM),
                  pl.BlockSpec((WINDOW, ATTN_WIDTH), cur), kv(cur), kv(prev), kv(cur), kv(prev)],
        out_specs=pl.BlockSpec((WINDOW, ATTN_WIDTH), cur),
        out_shape=jax.ShapeDtypeStruct((batch * seq, ATTN_WIDTH), BF16),
        compiler_params=_cparams(("parallel", "parallel")),
        name="attn_prompt",
    )(sinks, q, k, k, v, v)


def _attn_sample_kernel(sink_ref, q_ref, kn_ref, vn_ref, ck_ref, cv_ref, o_ref, nk_ref, nv_ref, *, gb, ts):
    w = WINDOW
    group = N_Q_HEADS // N_KV_HEADS
    rows = group * ts
    low = lax.broadcasted_iota(jnp.int32, (ts, LANES), 1) < HEAD_DIM
    t_row = lax.broadcasted_iota(jnp.int32, (rows, 1), 0) % ts
    g_row = lax.broadcasted_iota(jnp.int32, (rows, 1), 0) // ts
    c_idx = lax.broadcasted_iota(jnp.int32, (rows, w), 1)
    n_idx = lax.broadcasted_iota(jnp.int32, (rows, 2 * ts), 1)
    valid_c = c_idx > t_row
    valid_n = n_idx <= t_row
    pad = jnp.zeros((ts, LANES), F32)
    outs = []
    for b in range(gb):
        kc, vc = ck_ref[b], cv_ref[b]
        kn, vn = kn_ref[b * ts:(b + 1) * ts, :], vn_ref[b * ts:(b + 1) * ts, :]
        nk_ref[b, 0:w - ts, :] = kc[ts:, :]
        nk_ref[b, w - ts:, :] = kn
        nv_ref[b, 0:w - ts, :] = vc[ts:, :]
        nv_ref[b, w - ts:, :] = vn
        knp = jnp.concatenate([kn, pad], axis=0)
        vnp = jnp.concatenate([vn, pad], axis=0)
        qb = q_ref[b * ts:(b + 1) * ts, :]
        cols = []
        for h in range(N_KV_HEADS):
            parts = []
            for jj in range(group // 2):
                col = (h * group // 2 + jj) * LANES
                qv = qb[:, col:col + LANES]
                parts += [jnp.where(low, qv, 0.0), jnp.where(low, 0.0, qv)]
            lhs = jnp.concatenate(parts, axis=0).astype(BF16)
            sink = jnp.zeros((rows, 1), F32)
            for g in range(group):
                sink = jnp.where(g_row == g, sink_ref[h * group + g], sink)
            s_c = jnp.where(valid_c, _nt_dot(lhs, _dup_head(kc, h).astype(BF16)), -jnp.inf)
            s_n = jnp.where(valid_n, _nt_dot(lhs, _dup_head(knp, h).astype(BF16)), -jnp.inf)
            m = jnp.maximum(jnp.maximum(jnp.max(s_c, axis=-1, keepdims=True),
                                        jnp.max(s_n, axis=-1, keepdims=True)), sink)
            p_c = jnp.exp(s_c - m)
            p_n = jnp.exp(s_n - m)
            den = (jnp.sum(p_c, axis=-1, keepdims=True) + jnp.sum(p_n, axis=-1, keepdims=True)
                   + jnp.exp(sink - m))
            o = (jnp.dot(p_c.astype(BF16), _dup_head(vc, h).astype(BF16), preferred_element_type=F32)
                 + jnp.dot(p_n.astype(BF16), _dup_head(vnp, h).astype(BF16), preferred_element_type=F32)) / den
            for jj in range(group // 2):
                lo_part = o[(2 * jj) * ts:(2 * jj + 1) * ts, :]
                hi_part = o[(2 * jj + 1) * ts:(2 * jj + 2) * ts, :]
                cols.append(jnp.where(low, lo_part, hi_part))
        outs.append(jnp.concatenate(cols, axis=1))
    o_ref[...] = jnp.concatenate(outs, axis=0).astype(o_ref.dtype)


def _attn_sample(q, kf, vf, cache_k, cache_v, sinks, *, batch, ts):
    assert ts % 8 == 0 and ts <= WINDOW
    gb = _tile(batch, 8, 2)
    tok = lambda w: pl.BlockSpec((gb * ts, w), lambda i: (i, 0))
    cache = pl.BlockSpec((gb, WINDOW, KV_WIDTH), lambda i: (i, 0, 0))
    cshape = jax.ShapeDtypeStruct((batch, WINDOW, KV_WIDTH), F32)
    return pl.pallas_call(
        functools.partial(_attn_sample_kernel, gb=gb, ts=ts),
        grid=(batch // gb,),
        in_specs=[pl.BlockSpec(memory_space=pltpu.SMEM), tok(ATTN_WIDTH), tok(KV_WIDTH), tok(KV_WIDTH),
                  cache, cache],
        out_specs=[tok(ATTN_WIDTH), cache, cache],
        out_shape=[jax.ShapeDtypeStruct((batch * ts, ATTN_WIDTH), BF16), cshape, cshape],
        compiler_params=_cparams(("parallel",)),
        name="attn_sample",
    )(sinks, q, kf, vf, cache_k, cache_v)


def _ln_swish(acc, b, lg, lb):
    y = acc + b
    mu = jnp.mean(y, axis=-1, keepdims=True)
    yc = y - mu
    var = jnp.mean(yc * yc, axis=-1, keepdims=True)
    yn = yc * lax.rsqrt(var + LN_EPS) * lg + lb
    return yn * _sigmoid(yn)


def _conv_prompt_kernel(a_ref, ap_ref, w_ref, b_ref, lg_ref, lb_ref, o_ref, win_ref, *, tt, rc):
    j = pl.program_id(1)
    n = CONV_HALO + tt
    win = jnp.concatenate([jnp.where(j > 0, ap_ref[...], 0.0), a_ref[...]], axis=0)
    win_ref[0] = win
    for r in range(1, SUBLANES):
        win_ref[r] = pltpu.roll(win, n - r, 0)
    off = CONV_HALO - (CONV_K - 1)
    b, lg, lb = b_ref[...], lg_ref[...], lb_ref[...]
    for c in range(tt // rc):
        acc = jnp.zeros((rc, a_ref.shape[1]), F32)
        for k in range(CONV_K):
            s = off + k
            base = c * rc + (s // SUBLANES) * SUBLANES
            acc = acc + w_ref[k:k + 1, :] * win_ref[s % SUBLANES, base:base + rc, :]
        o_ref[c * rc:(c + 1) * rc, :] = _ln_swish(acc, b, lg, lb).astype(o_ref.dtype)


def _conv_prompt(a, w, b, lg, lb, *, batch, seq):
    cw = a.shape[1]
    tt = _tile(seq, 256, CONV_HALO)
    rc = _tile(tt, 32, 16)
    nt = seq // tt
    per = tt // CONV_HALO
    cur = lambda bb, j: (bb * nt + j, 0)
    prev = lambda bb, j: (jnp.maximum((bb * nt + j) * per - 1, 0), 0)
    vec = pl.BlockSpec((1, cw), lambda bb, j: (0, 0))
    return pl.pallas_call(
        functools.partial(_conv_prompt_kernel, tt=tt, rc=rc),
        grid=(batch, nt),
        in_specs=[pl.BlockSpec((tt, cw), cur), pl.BlockSpec((CONV_HALO, cw), prev),
                  pl.BlockSpec((CONV_K, cw), lambda bb, j: (0, 0)), vec, vec, vec],
        out_specs=pl.BlockSpec((tt, cw), cur),
        out_shape=jax.ShapeDtypeStruct((batch * seq, cw), BF16),
        scratch_shapes=[pltpu.VMEM((SUBLANES, CONV_HALO + tt, cw), F32)],
        compiler_params=_cparams(("parallel", "parallel")),
        name="conv_prompt",
    )(a, a, w, b, lg, lb)


def _conv_sample_kernel(a_ref, st_ref, w_ref, b_ref, lg_ref, lb_ref, o_ref, win_ref, *, gb, ts):
    ctx = CONV_K - 1
    b, lg, lb = b_ref[...], lg_ref[...], lb_ref[...]
    for bb in range(gb):
        win_ref[bb, 0:ctx, :] = st_ref[bb]
        win_ref[bb, ctx:ctx + ts, :] = a_ref[bb * ts:(bb + 1) * ts, :]
    outs = []
    for bb in range(gb):
        acc = jnp.zeros((ts, a_ref.shape[1]), F32)
        for k in range(CONV_K):
            acc = acc + w_ref[k:k + 1, :] * win_ref[bb, k:k + ts, :]
        outs.append(_ln_swish(acc, b, lg, lb))
    o_ref[...] = jnp.concatenate(outs, axis=0).astype(o_ref.dtype)


def _conv_sample(a, state, w, b, lg, lb, *, batch, ts):
    cw = a.shape[1]
    ctx = CONV_K - 1
    gb = _tile(batch, 8, 2)
    vec = pl.BlockSpec((1, cw), lambda i: (0, 0))
    return pl.pallas_call(
        functools.partial(_conv_sample_kernel, gb=gb, ts=ts),
        grid=(batch // gb,),
        in_specs=[pl.BlockSpec((gb * ts, cw), lambda i: (i, 0)),
                  pl.BlockSpec((gb, ctx, cw), lambda i: (i, 0, 0)),
                  pl.BlockSpec((CONV_K, cw), lambda i: (0, 0)), vec, vec, vec],
        out_specs=pl.BlockSpec((gb * ts, cw), lambda i: (i, 0)),
        out_shape=jax.ShapeDtypeStruct((batch * ts, cw), BF16),
        scratch_shapes=[pltpu.VMEM((gb, ctx + ts + 2, cw), F32)],
        compiler_params=_cparams(("parallel",)),
        name="conv_sample",
    )(a, state, w, b, lg, lb)


def _outproj_router_kernel(x_ref, att_ref, cv_ref, wo_ref, g_ref, rw_ref, rb_ref, cin_ref,
                           x1_ref, hn_ref, route_ref, rt_ref, cnt_ref, carry_ref):
    i = pl.program_id(0)

    @pl.when(i == 0)
    def _():
        carry_ref[...] = cin_ref[...]

    aw = att_ref.shape[1]
    mix = (jnp.dot(att_ref[...], wo_ref[0:aw, :], preferred_element_type=F32)
           + jnp.dot(cv_ref[...], wo_ref[aw:, :], preferred_element_type=F32))
    x1 = x_ref[...] + mix
    x1_ref[...] = x1
    ms = jnp.mean(x1 * x1, axis=-1, keepdims=True)
    hn = x1 * lax.rsqrt(ms + RMS_EPS) * g_ref[...]
    hn_ref[...] = hn
    logits = jnp.dot(hn.astype(BF16), rw_ref[...], preferred_element_type=F32) + rb_ref[...]
    tm = logits.shape[0]
    lane = lax.broadcasted_iota(jnp.int32, (tm, LANES), 1)
    onehot = jnp.zeros((tm, LANES), F32)
    vals, idxs = [], []
    for _ in range(TOP_K):
        m = jnp.max(logits, axis=-1, keepdims=True)
        idx = jnp.min(jnp.where(logits == m, lane, LANES), axis=-1, keepdims=True)
        sel = lane == idx
        onehot = onehot + sel.astype(F32)
        logits = jnp.where(sel, -jnp.inf, logits)
        vals.append(m)
        idxs.append(idx)
    es = [jnp.exp(v - vals[0]) for v in vals]
    den = es[0] + es[1] + es[2] + es[3]
    r = lax.broadcasted_iota(jnp.int32, (tm, tm), 0)
    c = lax.broadcasted_iota(jnp.int32, (tm, tm), 1)
    before = (c < r).astype(BF16)
    prefix = jnp.dot(before, onehot.astype(BF16), preferred_element_type=F32) + carry_ref[...]
    route = jnp.zeros((tm, LANES), F32)
    for k in range(TOP_K):
        rank = jnp.sum(jnp.where(lane == idxs[k], prefix, 0.0), axis=-1, keepdims=True)
        route = jnp.where(lane == k, idxs[k].astype(F32), route)
        route = jnp.where(lane == TOP_K + k, rank, route)
        route = jnp.where(lane == 2 * TOP_K + k, es[k] / den, route)
    route_ref[...] = route
    rt_ref[...] = jnp.transpose(route)[0:2 * TOP_K, :]
    carry = carry_ref[...] + jnp.sum(onehot, axis=0, keepdims=True)
    carry_ref[...] = carry
    cnt_ref[...] = carry


def _outproj_router(x2, att, cv, wo_bf, g, rw_bf, rb, carry_in):
    n, d = x2.shape
    tm = _tile(n, 256, 16)
    row = lambda w: pl.BlockSpec((tm, w), lambda i: (i, 0))
    full = lambda a: pl.BlockSpec(a.shape, lambda i: (0,) * a.ndim)
    return pl.pallas_call(
        _outproj_router_kernel,
        grid=(n // tm,),
        in_specs=[row(d), row(att.shape[1]), row(cv.shape[1]), full(wo_bf), full(g), full(rw_bf), full(rb),
                  full(carry_in)],
        out_specs=[row(d), row(d), row(LANES), pl.BlockSpec((2 * TOP_K, tm), lambda i: (0, i)),
                   pl.BlockSpec((1, LANES), lambda i: (0, 0))],
        out_shape=[jax.ShapeDtypeStruct((n, d), F32), jax.ShapeDtypeStruct((n, d), F32),
                   jax.ShapeDtypeStruct((n, LANES), F32), jax.ShapeDtypeStruct((2 * TOP_K, n), F32),
                   jax.ShapeDtypeStruct((1, LANES), F32)],
        scratch_shapes=[pltpu.VMEM((1, LANES), F32)],
        compiler_params=_cparams(("arbitrary",)),
        name="outproj_router",
    )(x2, att, cv, wo_bf, g, rw_bf, rb, carry_in)


def _dispatch_kernel(fill_ref, nfill_ref, dest_ref, hn_ref, hn2_ref, xs_ref, zero_ref, sem, zsem,
                     *, td, tme, n_first):
    @pl.when(pl.program_id(0) == 0)
    def _():
        zero_ref[...] = jnp.zeros(zero_ref.shape, F32)

        def fill(f):
            row = pl.multiple_of(fill_ref[f] * tme, tme)
            return pltpu.make_async_copy(zero_ref, xs_ref.at[pl.ds(row, tme)], zsem)

        def start(f, carry):
            fill(f).start()
            return carry

        def wait(f, carry):
            fill(f).wait()
            return carry

        lax.fori_loop(0, nfill_ref[0], start, 0)
        lax.fori_loop(0, nfill_ref[0], wait, 0)

    def scatter_rows(src_ref):
        def issue(r, carry):
            for k in range(TOP_K):
                d = dest_ref[0, 0, k * td + r]
                pltpu.make_async_copy(src_ref.at[pl.ds(r, 1)], xs_ref.at[pl.ds(d, 1)], sem).start(priority=k % 2)
            return carry

        lax.fori_loop(0, td, issue, 0, unroll=8)
        for _ in range(TOP_K):
            pltpu.make_async_copy(src_ref, xs_ref.at[pl.ds(0, td)], sem).wait()

    @pl.when(pl.program_id(0) < n_first)
    def _():
        scatter_rows(hn_ref)

    @pl.when(pl.program_id(0) >= n_first)
    def _():
        scatter_rows(hn2_ref)


def _dest_blocks(dest, td):
    n = dest.shape[1]
    return dest.reshape(TOP_K, n // td, td).transpose(1, 0, 2).reshape(n // td, 1, TOP_K * td)


def _dispatch(hn_a, hn_b, dest, fill_blocks, n_fill, *, nb, tme):
    na, d = hn_a.shape
    nb_rows = hn_b.shape[0]
    td = _tile(nb_rows, 256, 8)
    assert na % td == 0
    n_first = na // td
    steps = n_first + nb_rows // td
    grid_spec = pltpu.PrefetchScalarGridSpec(
        num_scalar_prefetch=2,
        grid=(steps,),
        in_specs=[pl.BlockSpec((1, 1, td * TOP_K), lambda i, fb, nf: (i, 0, 0), memory_space=pltpu.SMEM),
                  pl.BlockSpec((td, d), lambda i, fb, nf: (jnp.minimum(i, n_first - 1), 0)),
                  pl.BlockSpec((td, d), lambda i, fb, nf: (jnp.maximum(i - n_first, 0), 0))],
        out_specs=pl.BlockSpec(memory_space=pl.ANY),
        scratch_shapes=[pltpu.VMEM((tme, d), F32), pltpu.SemaphoreType.DMA(()), pltpu.SemaphoreType.DMA(())],
    )
    return pl.pallas_call(
        functools.partial(_dispatch_kernel, td=td, tme=tme, n_first=n_first),
        grid_spec=grid_spec,
        out_shape=jax.ShapeDtypeStruct((nb * tme, d), F32),
        compiler_params=_cparams(("arbitrary",)),
        name="dispatch",
    )(fill_blocks, n_fill, _dest_blocks(dest, td), hn_a, hn_b)


def _experts_kernel(be_ref, bsrc_ref, nu_ref, x_ref, w1_ref, b1_ref, w2_ref, b2_ref,
                    y_ref, w1b_ref, w2b_ref):
    i = pl.program_id(0)
    e = be_ref[i]
    e_prev = be_ref[jnp.maximum(i - 1, 0)]
    d_ff = w2_ref.shape[1]

    @pl.when((i == 0) | (e != e_prev))
    def _():
        w1b_ref[...] = w1_ref[0].astype(BF16)
        w2b_ref[...] = w2_ref[0].astype(BF16)

    @pl.when(i < nu_ref[0])
    def _():
        x = x_ref[...].astype(BF16)
        h = jnp.dot(x, w1b_ref[...], preferred_element_type=F32) + b1_ref[0]
        x_glu = jnp.minimum(h[:, :d_ff], SWIGLU_LIMIT)
        x_lin = jnp.clip(h[:, d_ff:], -SWIGLU_LIMIT, SWIGLU_LIMIT)
        act = x_glu * _sigmoid(SWIGLU_ALPHA * x_glu) * (x_lin + 1.0)
        y_ref[...] = jnp.dot(act.astype(BF16), w2b_ref[...], preferred_element_type=F32) + b2_ref[0]

    @pl.when(i >= nu_ref[0])
    def _():
        y_ref[...] = jnp.zeros(y_ref.shape, F32)


def _experts(xs, w1, b1, w2, b2, blk_exp, blk_src, n_used, *, tme):
    p_rows, d = xs.shape
    n_exp, _, h2 = w1.shape
    d_ff = w2.shape[1]
    nb = p_rows // tme
    grid_spec = pltpu.PrefetchScalarGridSpec(
        num_scalar_prefetch=3,
        grid=(nb,),
        in_specs=[pl.BlockSpec((tme, d), lambda i, be, bs, nu: (bs[i], 0)),
                  pl.BlockSpec((1, d, h2), lambda i, be, bs, nu: (be[i], 0, 0)),
                  pl.BlockSpec((1, 1, h2), lambda i, be, bs, nu: (be[i], 0, 0)),
                  pl.BlockSpec((1, d_ff, d), lambda i, be, bs, nu: (be[i], 0, 0)),
                  pl.BlockSpec((1, 1, d), lambda i, be, bs, nu: (be[i], 0, 0))],
        out_specs=pl.BlockSpec((tme, d), lambda i, be, bs, nu: (i, 0)),
        scratch_shapes=[pltpu.VMEM((d, h2), BF16), pltpu.VMEM((d_ff, d), BF16)],
    )
    return pl.pallas_call(
        _experts_kernel,
        grid_spec=grid_spec,
        out_shape=jax.ShapeDtypeStruct((p_rows, d), F32),
        compiler_params=_cparams(("arbitrary",)),
        name="experts",
    )(blk_exp, blk_src, n_used, xs, w1, b1.reshape(n_exp, 1, h2), w2, b2.reshape(n_exp, 1, d))


def _combine_kernel(dcur_ref, dnext_ref, x1_ref, route_ref, g_ref, ys_ref, o_ref, buf_ref, sem, *, tc):
    i = pl.program_id(0)
    slot = i % 2

    def gather(dref, s):
        def issue(r, carry):
            for k in range(TOP_K):
                d = dref[0, 0, k * tc + r]
                pltpu.make_async_copy(ys_ref.at[pl.ds(d, 1)], buf_ref.at[s, k, pl.ds(r, 1)],
                                      sem.at[s]).start(priority=k % 2)
            return carry

        lax.fori_loop(0, tc, issue, 0, unroll=8)

    @pl.when(i == 0)
    def _():
        gather(dcur_ref, 0)

    @pl.when(i + 1 < pl.num_programs(0))
    def _():
        gather(dnext_ref, 1 - slot)

    for k in range(TOP_K):
        pltpu.make_async_copy(ys_ref.at[pl.ds(0, tc)], buf_ref.at[slot, k], sem.at[slot]).wait()
    route = route_ref[...]
    y = x1_ref[...]
    for k in range(TOP_K):
        y = y + route[:, 2 * TOP_K + k:2 * TOP_K + k + 1] * buf_ref[slot, k]
    ms = jnp.mean(y * y, axis=-1, keepdims=True)
    o_ref[...] = y * lax.rsqrt(ms + RMS_EPS) * g_ref[...]


def _combine(x1, route, dest, ys, g):
    n, d = x1.shape
    tc = _tile(n, 128, 8)
    steps = n // tc
    dest3 = _dest_blocks(dest, tc)
    dspec = lambda m: pl.BlockSpec((1, 1, tc * TOP_K), m, memory_space=pltpu.SMEM)
    return pl.pallas_call(
        functools.partial(_combine_kernel, tc=tc),
        grid=(steps,),
        in_specs=[dspec(lambda i: (i, 0, 0)), dspec(lambda i: (jnp.minimum(i + 1, steps - 1), 0, 0)),
                  pl.BlockSpec((tc, d), lambda i: (i, 0)),
                  pl.BlockSpec((tc, LANES), lambda i: (i, 0)),
                  pl.BlockSpec((1, d), lambda i: (0, 0)),
                  pl.BlockSpec(memory_space=pl.ANY)],
        out_specs=pl.BlockSpec((tc, d), lambda i: (i, 0)),
        out_shape=jax.ShapeDtypeStruct((n, d), F32),
        scratch_shapes=[pltpu.VMEM((2, TOP_K, tc, d), F32), pltpu.SemaphoreType.DMA((2,))],
        compiler_params=_cparams(("arbitrary",)),
        name="combine",
    )(dest3, dest3, x1, route, g, ys)


def _dest_rows(route_t, blk_start, tme):
    eidx = route_t[0:TOP_K].astype(jnp.int32)
    rank = route_t[TOP_K:2 * TOP_K].astype(jnp.int32)
    return (blk_start[eidx] * tme + rank).astype(jnp.int32)


def _routing_tables(counts_f, *, tme, nb):
    counts = counts_f[0, :N_EXPERTS].astype(jnp.int32)
    nblk = (counts + tme - 1) // tme
    blk_end = jnp.cumsum(nblk)
    blk_start = blk_end - nblk
    n_used = blk_end[-1]
    b = jnp.arange(nb, dtype=jnp.int32)
    used = b < n_used
    blk_exp = jnp.minimum(jnp.sum((b[:, None] >= blk_end[None, :]).astype(jnp.int32), axis=1), N_EXPERTS - 1)
    last_exp = jnp.max(jnp.where(nblk > 0, jnp.arange(N_EXPERTS, dtype=jnp.int32), 0))
    blk_exp = jnp.where(used, blk_exp, last_exp).astype(jnp.int32)
    blk_src = jnp.where(used, b, 0).astype(jnp.int32)
    partial = used & (b == blk_end[blk_exp] - 1) & (counts[blk_exp] % tme != 0)
    needs_fill = partial | ~used
    fill_blocks = jnp.argsort(~needs_fill, stable=True).astype(jnp.int32)
    n_fill = jnp.sum(needs_fill).reshape(1).astype(jnp.int32)
    return blk_start, blk_exp, blk_src, n_used.reshape(1).astype(jnp.int32), fill_blocks, n_fill


def kernel(x_prompt, x_sample, cache_k, cache_v, state_conv, attn_norm_g, w_in, attn_sinks, conv_w, conv_b,
           conv_ln_g, conv_ln_b, w_out, ffn_norm_g, router_w, router_b, w1, b1, w2, b2, final_norm_g):
    depth = w_in.shape[0]
    assert depth == 1, "single-layer step"
    bp, sp, d = x_prompt.shape
    bs, ss, _ = x_sample.shape
    cw = conv_w.shape[2]
    np_, ns = bp * sp, bs * ss
    n_tok = np_ + ns
    assert sp % WINDOW == 0

    xp2 = x_prompt.reshape(np_, d)
    xs2 = x_sample.reshape(ns, d)
    w_in_bf = w_in[0].astype(BF16)
    w_out_bf = w_out[0].astype(BF16)
    g_attn = attn_norm_g[0].reshape(1, d)
    g_ffn = ffn_norm_g[0].reshape(1, d)
    sinks = attn_sinks[0]
    vec = lambda a: a.reshape(1, cw)

    tab_p = _rope_tables(jnp.arange(sp, dtype=jnp.int32))
    tms = _tile(ns, 512, max(ss, 16))
    tab_s = _rope_tables(PAST_LEN + (jnp.arange(tms, dtype=jnp.int32) % ss))
    qp, kp, vp, kfp, vfp, ap = _in_proj(xp2, g_attn, w_in_bf, tab_p, seq_period=sp, q_dtype=BF16, conv_width=cw)
    qs, _, _, kfs, vfs, as_ = _in_proj(xs2, g_attn, w_in_bf, tab_s, seq_period=None, q_dtype=F32, conv_width=cw)

    att_p = _attn_prompt(qp, kp, vp, sinks, batch=bp, seq=sp)
    ck = cache_k[0].reshape(bs, WINDOW, KV_WIDTH)
    cv_ = cache_v[0].reshape(bs, WINDOW, KV_WIDTH)
    att_s, nk_s, nv_s = _attn_sample(qs, kfs, vfs, ck, cv_, sinks, batch=bs, ts=ss)

    cv_p = _conv_prompt(ap, conv_w[0], vec(conv_b[0]), vec(conv_ln_g[0]), vec(conv_ln_b[0]), batch=bp, seq=sp)
    cv_s = _conv_sample(as_, state_conv[0], conv_w[0], vec(conv_b[0]), vec(conv_ln_g[0]), vec(conv_ln_b[0]),
                        batch=bs, ts=ss)

    rw_bf = jnp.zeros((d, LANES), BF16).at[:, :N_EXPERTS].set(router_w[0].astype(BF16))
    rb = jnp.full((1, LANES), -jnp.inf, F32).at[0, :N_EXPERTS].set(router_b[0])
    zero_carry = jnp.zeros((1, LANES), F32)
    x1p, hnp, route_p, rt_p, cnt_p = _outproj_router(xp2, att_p, cv_p, w_out_bf, g_ffn, rw_bf, rb, zero_carry)
    x1s, hns, route_s, rt_s, cnt = _outproj_router(xs2, att_s, cv_s, w_out_bf, g_ffn, rw_bf, rb, cnt_p)

    tme = EXPERT_BLOCK_ROWS
    nb = -(-(n_tok * TOP_K + N_EXPERTS * (tme - 1)) // tme)
    blk_start, blk_exp, blk_src, n_used, fill_blocks, n_fill = _routing_tables(cnt, tme=tme, nb=nb)
    dest_p = _dest_rows(rt_p, blk_start, tme)
    dest_s = _dest_rows(rt_s, blk_start, tme)
    xs_sorted = _dispatch(hnp, hns, jnp.concatenate([dest_p, dest_s], axis=1), fill_blocks, n_fill, nb=nb, tme=tme)
    ys = _experts(xs_sorted, w1[0], b1[0], w2[0], b2[0], blk_exp, blk_src, n_used, tme=tme)
    g_fin = final_norm_g.reshape(1, d)
    y_p = _combine(x1p, route_p, dest_p, ys, g_fin)
    y_s = _combine(x1s, route_s, dest_s, ys, g_fin)

    kv5 = lambda t, bb: t.reshape(bb, -1, KV_WIDTH)[:, -WINDOW:].reshape(bb, WINDOW, N_KV_HEADS, HEAD_DIM)
    new_k_p = kv5(kfp, bp)[None]
    new_v_p = kv5(vfp, bp)[None]
    ctx = CONV_K - 1
    new_c_p = ap.reshape(bp, sp, cw)[:, -ctx:][None]
    new_c_s = jnp.concatenate([state_conv[0], as_.reshape(bs, ss, cw)], axis=1)[:, -ctx:][None]
    return (y_p.reshape(bp, sp, d), y_s.reshape(bs, ss, d), new_k_p, new_v_p, new_c_p,
            kv5(nk_s, bs)[None], kv5(nv_s, bs)[None], new_c_s)
```

```python
import functools

import jax
import jax.numpy as jnp
from jax import lax
from jax.experimental import pallas as pl
from jax.experimental.pallas import tpu as pltpu

F32 = jnp.float32
BF16 = jnp.bfloat16

HEAD_DIM = 64
N_Q_HEADS = 8
N_KV_HEADS = 2
WINDOW = 128
ROPE_THETA = 500000.0
ROPE_DIM = 16
CONV_K = 31
N_EXPERTS = 32
TOP_K = 4
SWIGLU_LIMIT = 7.0
SWIGLU_ALPHA = 1.702
RMS_EPS = 1e-5
LN_EPS = 1e-5
PAST_LEN = 16384

LANES = 128
SUBLANES = 8
CONV_HALO = 32
VMEM_LIMIT = 56 * 1024 * 1024
EXPERT_BLOCK_ROWS = 512

ATTN_WIDTH = N_Q_HEADS * HEAD_DIM
KV_WIDTH = N_KV_HEADS * HEAD_DIM


def _tile(n, pref, mult=8):
    t = min(pref, n)
    while t > 0 and (n % t or t % mult):
        t -= 1
    assert t > 0, (n, pref, mult)
    return t


def _cparams(sem):
    return pltpu.CompilerParams(dimension_semantics=sem, vmem_limit_bytes=VMEM_LIMIT)


def _sigmoid(x):
    return 1.0 / (1.0 + jnp.exp(-x))


def _rope_tables(pos):
    half = ROPE_DIM // 2
    inv_freq = jnp.power(jnp.float32(ROPE_THETA), -jnp.arange(half, dtype=F32) * 2.0 / ROPE_DIM)
    ang = pos.astype(F32)[:, None] * inv_freq[None, :]
    cos, sin = jnp.cos(ang), jnp.sin(ang)
    l64 = jnp.arange(LANES) % HEAD_DIM
    f = l64 % half
    cos_l, sin_l = cos[:, f], sin[:, f]
    c = jnp.where(l64 < ROPE_DIM, cos_l, 1.0)
    s1 = jnp.where(l64 < half, -sin_l, 0.0)
    s2 = jnp.where((l64 >= half) & (l64 < ROPE_DIM), sin_l, 0.0)
    return c.astype(F32), s1.astype(F32), s2.astype(F32)


def _inproj_kernel(x_ref, g_ref, w_ref, c_ref, s1_ref, s2_ref,
                   q_ref, k_ref, v_ref, kf_ref, vf_ref, a_ref, *, conv_width):
    x = x_ref[...]
    ms = jnp.mean(x * x, axis=-1, keepdims=True)
    h = (x * lax.rsqrt(ms + RMS_EPS) * g_ref[...]).astype(BF16)
    z = jnp.dot(h, w_ref[...], preferred_element_type=F32)
    c, s1, s2 = c_ref[...], s1_ref[...], s2_ref[...]
    half = ROPE_DIM // 2

    def rot(t):
        return t * c + pltpu.roll(t, LANES - half, 1) * s1 + pltpu.roll(t, half, 1) * s2

    scale = HEAD_DIM ** -0.5
    for j in range(ATTN_WIDTH // LANES):
        q_ref[:, j * LANES:(j + 1) * LANES] = (rot(z[:, j * LANES:(j + 1) * LANES]) * scale).astype(q_ref.dtype)
    k0 = ATTN_WIDTH
    kr = rot(z[:, k0:k0 + KV_WIDTH])
    k_ref[...] = kr.astype(BF16)
    kf_ref[...] = kr
    v0 = k0 + KV_WIDTH
    vv = z[:, v0:v0 + KV_WIDTH]
    v_ref[...] = vv.astype(BF16)
    vf_ref[...] = vv
    u0 = v0 + KV_WIDTH
    g0 = u0 + conv_width
    a_ref[...] = z[:, u0:g0] * _sigmoid(z[:, g0:g0 + conv_width])


def _in_proj(x2, g, w_bf, tables, *, seq_period, q_dtype, conv_width):
    n, d = x2.shape
    in_w = w_bf.shape[1]
    if seq_period is None:
        tm = tables[0].shape[0]
        tmap = lambda i: (0, 0)
    else:
        tm = _tile(seq_period, 512, 16)
        per = seq_period // tm
        tmap = lambda i: (i % per, 0)
    assert n % tm == 0
    row = lambda w: pl.BlockSpec((tm, w), lambda i: (i, 0))
    tab = pl.BlockSpec((tm, LANES), tmap)
    return pl.pallas_call(
        functools.partial(_inproj_kernel, conv_width=conv_width),
        grid=(n // tm,),
        in_specs=[row(d), pl.BlockSpec((1, d), lambda i: (0, 0)),
                  pl.BlockSpec((d, in_w), lambda i: (0, 0)), tab, tab, tab],
        out_specs=[row(ATTN_WIDTH), row(KV_WIDTH), row(KV_WIDTH), row(KV_WIDTH), row(KV_WIDTH),
                   row(conv_width)],
        out_shape=[jax.ShapeDtypeStruct((n, ATTN_WIDTH), q_dtype),
                   jax.ShapeDtypeStruct((n, KV_WIDTH), BF16),
                   jax.ShapeDtypeStruct((n, KV_WIDTH), BF16),
                   jax.ShapeDtypeStruct((n, KV_WIDTH), F32),
                   jax.ShapeDtypeStruct((n, KV_WIDTH), F32),
                   jax.ShapeDtypeStruct((n, conv_width), F32)],
        compiler_params=_cparams(("parallel",)),
        name="in_proj",
    )(x2, g, w_bf, *tables)


def _dup_head(t, h):
    sw = pltpu.roll(t, HEAD_DIM, 1)
    low = lax.broadcasted_iota(jnp.int32, t.shape, 1) < HEAD_DIM
    return jnp.where(low, t, sw) if h == 0 else jnp.where(low, sw, t)


def _nt_dot(a, b):
    return lax.dot_general(a, b, (((1,), (1,)), ((), ())), preferred_element_type=F32)


def _attn_prompt_kernel(sink_ref, q_ref, kc_ref, kp_ref, vc_ref, vp_ref, o_ref):
    j = pl.program_id(1)
    w = WINDOW
    k2 = jnp.concatenate([kp_ref[...], kc_ref[...]], axis=0).astype(F32)
    v2 = jnp.concatenate([vp_ref[...], vc_ref[...]], axis=0).astype(F32)
    r = lax.broadcasted_iota(jnp.int32, (w, 2 * w), 0)
    kk = lax.broadcasted_iota(jnp.int32, (w, 2 * w), 1)
    valid = (kk > r) & (kk <= r + w) & ((kk >= w) | (j > 0))
    low = lax.broadcasted_iota(jnp.int32, (w, LANES), 1) < HEAD_DIM
    zero = jnp.zeros((w, LANES), BF16)
    group = N_Q_HEADS // N_KV_HEADS
    for h in range(N_KV_HEADS):
        kd = _dup_head(k2, h).astype(BF16)
        vd = _dup_head(v2, h).astype(BF16)
        for jj in range(group // 2):
            col = (h * group // 2 + jj) * LANES
            qv = q_ref[:, col:col + LANES]
            halves = []
            for half in range(2):
                head = h * group + jj * 2 + half
                qm = jnp.where(low if half == 0 else ~low, qv, zero)
                s = jnp.where(valid, _nt_dot(qm, kd), -jnp.inf)
                sink = sink_ref[head]
                m = jnp.maximum(jnp.max(s, axis=-1, keepdims=True), sink)
                p = jnp.exp(s - m)
                den = jnp.sum(p, axis=-1, keepdims=True) + jnp.exp(sink - m)
                o = jnp.dot(p.astype(BF16), vd, preferred_element_type=F32)
                halves.append(o / den)
            o_ref[:, col:col + LANES] = jnp.where(low, halves[0], halves[1]).astype(o_ref.dtype)


def _attn_prompt(q, k, v, sinks, *, batch, seq):
    nb = seq // WINDOW
    cur = lambda b, j: (b * nb + j, 0)
    prev = lambda b, j: (b * nb + jnp.maximum(j - 1, 0), 0)
    kv = lambda m: pl.BlockSpec((WINDOW, KV_WIDTH), m)
    return pl.pallas_call(
        _attn_prompt_kernel,
        grid=(batch, nb),
        in_specs=[pl.BlockSpec(memory_space=pltpu.SMEM),
                  pl.BlockSpec((WINDOW, ATTN_WIDTH), cur), kv(cur), kv(prev), kv(cur), kv(prev)],
        out_specs=pl.BlockSpec((WINDOW, ATTN_WIDTH), cur),
        out_shape=jax.ShapeDtypeStruct((batch * seq, ATTN_WIDTH), BF16),
        compiler_params=_cparams(("parallel", "parallel")),
        name="attn_prompt",
    )(sinks, q, k, k, v, v)


def _attn_sample_kernel(sink_ref, q_ref, kn_ref, vn_ref, ck_ref, cv_ref, o_ref, nk_ref, nv_ref, *, gb, ts):
    w = WINDOW
    group = N_Q_HEADS // N_KV_HEADS
    rows = group * ts
    low = lax.broadcasted_iota(jnp.int32, (ts, LANES), 1) < HEAD_DIM
    t_row = lax.broadcasted_iota(jnp.int32, (rows, 1), 0) % ts
    g_row = lax.broadcasted_iota(jnp.int32, (rows, 1), 0) // ts
    c_idx = lax.broadcasted_iota(jnp.int32, (rows, w), 1)
    n_idx = lax.broadcasted_iota(jnp.int32, (rows, 2 * ts), 1)
    valid_c = c_idx > t_row
    valid_n = n_idx <= t_row
    pad = jnp.zeros((ts, LANES), F32)
    outs = []
    for b in range(gb):
        kc, vc = ck_ref[b], cv_ref[b]
        kn, vn = kn_ref[b * ts:(b + 1) * ts, :], vn_ref[b * ts:(b + 1) * ts, :]
        nk_ref[b, 0:w - ts, :] = kc[ts:, :]
        nk_ref[b, w - ts:, :] = kn
        nv_ref[b, 0:w - ts, :] = vc[ts:, :]
        nv_ref[b, w - ts:, :] = vn
        knp = jnp.concatenate([kn, pad], axis=0)
        vnp = jnp.concatenate([vn, pad], axis=0)
        qb = q_ref[b * ts:(b + 1) * ts, :]
        cols = []
        for h in range(N_KV_HEADS):
            parts = []
            for jj in range(group // 2):
                col = (h * group // 2 + jj) * LANES
                qv = qb[:, col:col + LANES]
                parts += [jnp.where(low, qv, 0.0), jnp.where(low, 0.0, qv)]
            lhs = jnp.concatenate(parts, axis=0).astype(BF16)
            sink = jnp.zeros((rows, 1), F32)
            for g in range(group):
                sink = jnp.where(g_row == g, sink_ref[h * group + g], sink)
            s_c = jnp.where(valid_c, _nt_dot(lhs, _dup_head(kc, h).astype(BF16)), -jnp.inf)
            s_n = jnp.where(valid_n, _nt_dot(lhs, _dup_head(knp, h).astype(BF16)), -jnp.inf)
            m = jnp.maximum(jnp.maximum(jnp.max(s_c, axis=-1, keepdims=True),
                                        jnp.max(s_n, axis=-1, keepdims=True)), sink)
            p_c = jnp.exp(s_c - m)
            p_n = jnp.exp(s_n - m)
            den = (jnp.sum(p_c, axis=-1, keepdims=True) + jnp.sum(p_n, axis=-1, keepdims=True)
                   + jnp.exp(sink - m))
            o = (jnp.dot(p_c.astype(BF16), _dup_head(vc, h).astype(BF16), preferred_element_type=F32)
                 + jnp.dot(p_n.astype(BF16), _dup_head(vnp, h).astype(BF16), preferred_element_type=F32)) / den
            for jj in range(group // 2):
                lo_part = o[(2 * jj) * ts:(2 * jj + 1) * ts, :]
                hi_part = o[(2 * jj + 1) * ts:(2 * jj + 2) * ts, :]
                cols.append(jnp.where(low, lo_part, hi_part))
        outs.append(jnp.concatenate(cols, axis=1))
    o_ref[...] = jnp.concatenate(outs, axis=0).astype(o_ref.dtype)


def _attn_sample(q, kf, vf, cache_k, cache_v, sinks, *, batch, ts):
    assert ts % 8 == 0 and ts <= WINDOW
    gb = _tile(batch, 8, 2)
    tok = lambda w: pl.BlockSpec((gb * ts, w), lambda i: (i, 0))
    cache = pl.BlockSpec((gb, WINDOW, KV_WIDTH), lambda i: (i, 0, 0))
    cshape = jax.ShapeDtypeStruct((batch, WINDOW, KV_WIDTH), F32)
    return pl.pallas_call(
        functools.partial(_attn_sample_kernel, gb=gb, ts=ts),
        grid=(batch // gb,),
        in_specs=[pl.BlockSpec(memory_space=pltpu.SMEM), tok(ATTN_WIDTH), tok(KV_WIDTH), tok(KV_WIDTH),
                  cache, cache],
        out_specs=[tok(ATTN_WIDTH), cache, cache],
        out_shape=[jax.ShapeDtypeStruct((batch * ts, ATTN_WIDTH), BF16), cshape, cshape],
        compiler_params=_cparams(("parallel",)),
        name="attn_sample",
    )(sinks, q, kf, vf, cache_k, cache_v)


def _ln_swish(acc, b, lg, lb):
    y = acc + b
    mu = jnp.mean(y, axis=-1, keepdims=True)
    yc = y - mu
    var = jnp.mean(yc * yc, axis=-1, keepdims=True)
    yn = yc * lax.rsqrt(var + LN_EPS) * lg + lb
    return yn * _sigmoid(yn)


def _conv_prompt_kernel(a_ref, ap_ref, w_ref, b_ref, lg_ref, lb_ref, o_ref, win_ref, *, tt, rc):
    j = pl.program_id(1)
    n = CONV_HALO + tt
    win = jnp.concatenate([jnp.where(j > 0, ap_ref[...], 0.0), a_ref[...]], axis=0)
    win_ref[0] = win
    for r in range(1, SUBLANES):
        win_ref[r] = pltpu.roll(win, n - r, 0)
    off = CONV_HALO - (CONV_K - 1)
    b, lg, lb = b_ref[...], lg_ref[...], lb_ref[...]
    for c in range(tt // rc):
        acc = jnp.zeros((rc, a_ref.shape[1]), F32)
        for k in range(CONV_K):
            s = off + k
            base = c * rc + (s // SUBLANES) * SUBLANES
            acc = acc + w_ref[k:k + 1, :] * win_ref[s % SUBLANES, base:base + rc, :]
        o_ref[c * rc:(c + 1) * rc, :] = _ln_swish(acc, b, lg, lb).astype(o_ref.dtype)


def _conv_prompt(a, w, b, lg, lb, *, batch, seq):
    cw = a.shape[1]
    tt = _tile(seq, 256, CONV_HALO)
    rc = _tile(tt, 32, 16)
    nt = seq // tt
    per = tt // CONV_HALO
    cur = lambda bb, j: (bb * nt + j, 0)
    prev = lambda bb, j: (jnp.maximum((bb * nt + j) * per - 1, 0), 0)
    vec = pl.BlockSpec((1, cw), lambda bb, j: (0, 0))
    return pl.pallas_call(
        functools.partial(_conv_prompt_kernel, tt=tt, rc=rc),
        grid=(batch, nt),
        in_specs=[pl.BlockSpec((tt, cw), cur), pl.BlockSpec((CONV_HALO, cw), prev),
                  pl.BlockSpec((CONV_K, cw), lambda bb, j: (0, 0)), vec, vec, vec],
        out_specs=pl.BlockSpec((tt, cw), cur),
        out_shape=jax.ShapeDtypeStruct((batch * seq, cw), BF16),
        scratch_shapes=[pltpu.VMEM((SUBLANES, CONV_HALO + tt, cw), F32)],
        compiler_params=_cparams(("parallel", "parallel")),
        name="conv_prompt",
    )(a, a, w, b, lg, lb)


def _conv_sample_kernel(a_ref, st_ref, w_ref, b_ref, lg_ref, lb_ref, o_ref, win_ref, *, gb, ts):
    ctx = CONV_K - 1
    b, lg, lb = b_ref[...], lg_ref[...], lb_ref[...]
    for bb in range(gb):
        win_ref[bb, 0:ctx, :] = st_ref[bb]
        win_ref[bb, ctx:ctx + ts, :] = a_ref[bb * ts:(bb + 1) * ts, :]
    outs = []
    for bb in range(gb):
        acc = jnp.zeros((ts, a_ref.shape[1]), F32)
        for k in range(CONV_K):
            acc = acc + w_ref[k:k + 1, :] * win_ref[bb, k:k + ts, :]
        outs.append(_ln_swish(acc, b, lg, lb))
    o_ref[...] = jnp.concatenate(outs, axis=0).astype(o_ref.dtype)


def _conv_sample(a, state, w, b, lg, lb, *, batch, ts):
    cw = a.shape[1]
    ctx = CONV_K - 1
    gb = _tile(batch, 8, 2)
    vec = pl.BlockSpec((1, cw), lambda i: (0, 0))
    return pl.pallas_call(
        functools.partial(_conv_sample_kernel, gb=gb, ts=ts),
        grid=(batch // gb,),
        in_specs=[pl.BlockSpec((gb * ts, cw), lambda i: (i, 0)),
                  pl.BlockSpec((gb, ctx, cw), lambda i: (i, 0, 0)),
                  pl.BlockSpec((CONV_K, cw), lambda i: (0, 0)), vec, vec, vec],
        out_specs=pl.BlockSpec((gb * ts, cw), lambda i: (i, 0)),
        out_shape=jax.ShapeDtypeStruct((batch * ts, cw), BF16),
        scratch_shapes=[pltpu.VMEM((gb, ctx + ts + 2, cw), F32)],
        compiler_params=_cparams(("parallel",)),
        name="conv_sample",
    )(a, state, w, b, lg, lb)


def _outproj_router_kernel(x_ref, att_ref, cv_ref, wo_ref, g_ref, rw_ref, rb_ref, cin_ref,
                           x1_ref, hn_ref, route_ref, rt_ref, cnt_ref, carry_ref):
    i = pl.program_id(0)

    @pl.when(i == 0)
    def _():
        carry_ref[...] = cin_ref[...]

    aw = att_ref.shape[1]
    mix = (jnp.dot(att_ref[...], wo_ref[0:aw, :], preferred_element_type=F32)
           + jnp.dot(cv_ref[...], wo_ref[aw:, :], preferred_element_type=F32))
    x1 = x_ref[...] + mix
    x1_ref[...] = x1
    ms = jnp.mean(x1 * x1, axis=-1, keepdims=True)
    hn = x1 * lax.rsqrt(ms + RMS_EPS) * g_ref[...]
    tm = hn.shape[0]
    _store_slabs(hn_ref, hn, tm)
    logits = _nt_dot(rw_ref[...], hn.astype(BF16)) + rb_ref[...][:, 0:1]
    n_exp = logits.shape[0]
    eid = lax.broadcasted_iota(jnp.int32, (n_exp, tm), 0)
    onehot = jnp.zeros((n_exp, tm), F32)
    vals, idxs = [], []
    for _ in range(TOP_K):
        m = jnp.max(logits, axis=0, keepdims=True)
        idx = jnp.min(jnp.where(logits == m, eid, n_exp), axis=0, keepdims=True)
        sel = eid == idx
        onehot = onehot + sel.astype(F32)
        logits = jnp.where(sel, -jnp.inf, logits)
        vals.append(m)
        idxs.append(idx)
    es = [jnp.exp(v - vals[0]) for v in vals]
    den = es[0] + es[1] + es[2] + es[3]
    r = lax.broadcasted_iota(jnp.int32, (tm, tm), 0)
    c = lax.broadcasted_iota(jnp.int32, (tm, tm), 1)
    before = (r < c).astype(BF16)
    carry = carry_ref[...]
    prefix = jnp.dot(onehot.astype(BF16), before, preferred_element_type=F32) + carry[:, 0:1]
    ranks = [jnp.sum(jnp.where(eid == idxs[k], prefix, 0.0), axis=0, keepdims=True) for k in range(TOP_K)]
    rows = [ix.astype(F32) for ix in idxs] + ranks
    rt_ref[...] = jnp.concatenate(rows, axis=0)
    fields = jnp.concatenate(rows + [e / den for e in es] + [jnp.zeros((LANES - 3 * TOP_K, tm), F32)], axis=0)
    route_ref[...] = jnp.transpose(fields)
    carry = carry + jnp.sum(onehot, axis=1, keepdims=True)
    carry_ref[...] = carry
    cnt_ref[...] = carry


def _store_slabs(ref, val, rows):
    for j in range(val.shape[1] // LANES):
        ref[pl.ds(j, rows, stride=SUBLANES), :] = val[:, j * LANES:(j + 1) * LANES]


def _load_slabs(ref, rows, dtype):
    return jnp.concatenate([ref[pl.ds(j, rows, stride=SUBLANES), :].astype(dtype) for j in range(SUBLANES)],
                           axis=1)


def _outproj_router(x2, att, cv, wo_bf, g, rwt_bf, rbt, carry_in):
    n, d = x2.shape
    assert d == SUBLANES * LANES
    n_exp = rwt_bf.shape[0]
    tm = _tile(n, 256, 16)
    row = lambda w: pl.BlockSpec((tm, w), lambda i: (i, 0))
    full = lambda a: pl.BlockSpec(a.shape, lambda i: (0,) * a.ndim)
    return pl.pallas_call(
        _outproj_router_kernel,
        grid=(n // tm,),
        in_specs=[row(d), row(att.shape[1]), row(cv.shape[1]), full(wo_bf), full(g), full(rwt_bf), full(rbt),
                  full(carry_in)],
        out_specs=[row(d), pl.BlockSpec((tm * SUBLANES, LANES), lambda i: (i, 0)), row(LANES),
                   pl.BlockSpec((2 * TOP_K, tm), lambda i: (0, i)),
                   pl.BlockSpec((n_exp, LANES), lambda i: (0, 0))],
        out_shape=[jax.ShapeDtypeStruct((n, d), F32), jax.ShapeDtypeStruct((n * SUBLANES, LANES), F32),
                   jax.ShapeDtypeStruct((n, LANES), F32), jax.ShapeDtypeStruct((2 * TOP_K, n), F32),
                   jax.ShapeDtypeStruct((n_exp, LANES), F32)],
        scratch_shapes=[pltpu.VMEM((n_exp, LANES), F32)],
        compiler_params=_cparams(("arbitrary",)),
        name="outproj_router",
    )(x2, att, cv, wo_bf, g, rwt_bf, rbt, carry_in)


def _dispatch_kernel(fill_ref, nfill_ref, dest_ref, hn_ref, hn2_ref, xs_ref, zero_ref, sem, zsem,
                     *, td, tme, n_first):
    @pl.when(pl.program_id(0) == 0)
    def _():
        zero_ref[...] = jnp.zeros(zero_ref.shape, F32)

        def fill(f):
            row = pl.multiple_of(fill_ref[f] * (tme * SUBLANES), tme * SUBLANES)
            return pltpu.make_async_copy(zero_ref, xs_ref.at[pl.ds(row, tme * SUBLANES)], zsem)

        def start(f, carry):
            fill(f).start()
            return carry

        def wait(f, carry):
            fill(f).wait()
            return carry

        lax.fori_loop(0, nfill_ref[0], start, 0)
        lax.fori_loop(0, nfill_ref[0], wait, 0)

    def scatter_rows(src_ref):
        def issue(r, carry):
            src = src_ref.at[pl.ds(pl.multiple_of(r * SUBLANES, SUBLANES), SUBLANES)]
            for k in range(TOP_K):
                d = pl.multiple_of(dest_ref[0, 0, k * td + r] * SUBLANES, SUBLANES)
                pltpu.make_async_copy(src, xs_ref.at[pl.ds(d, SUBLANES)], sem).start(priority=k % 2)
            return carry

        lax.fori_loop(0, td, issue, 0, unroll=8)
        for _ in range(TOP_K):
            pltpu.make_async_copy(src_ref, xs_ref.at[pl.ds(0, td * SUBLANES)], sem).wait()

    @pl.when(pl.program_id(0) < n_first)
    def _():
        scatter_rows(hn_ref)

    @pl.when(pl.program_id(0) >= n_first)
    def _():
        scatter_rows(hn2_ref)


def _dest_blocks(dest, td):
    n = dest.shape[1]
    return dest.reshape(TOP_K, n // td, td).transpose(1, 0, 2).reshape(n // td, 1, TOP_K * td)


def _dispatch(hn_a, hn_b, dest, fill_blocks, n_fill, *, nb, tme):
    na = hn_a.shape[0] // SUBLANES
    nb_rows = hn_b.shape[0] // SUBLANES
    td = _tile(nb_rows, 256, 8)
    assert na % td == 0
    n_first = na // td
    steps = n_first + nb_rows // td
    slab = lambda m: pl.BlockSpec((td * SUBLANES, LANES), m)
    grid_spec = pltpu.PrefetchScalarGridSpec(
        num_scalar_prefetch=2,
        grid=(steps,),
        in_specs=[pl.BlockSpec((1, 1, td * TOP_K), lambda i, fb, nf: (i, 0, 0), memory_space=pltpu.SMEM),
                  slab(lambda i, fb, nf: (jnp.minimum(i, n_first - 1), 0)),
                  slab(lambda i, fb, nf: (jnp.maximum(i - n_first, 0), 0))],
        out_specs=pl.BlockSpec(memory_space=pl.ANY),
        scratch_shapes=[pltpu.VMEM((tme * SUBLANES, LANES), F32), pltpu.SemaphoreType.DMA(()),
                        pltpu.SemaphoreType.DMA(())],
    )
    return pl.pallas_call(
        functools.partial(_dispatch_kernel, td=td, tme=tme, n_first=n_first),
        grid_spec=grid_spec,
        out_shape=jax.ShapeDtypeStruct((nb * tme * SUBLANES, LANES), F32),
        compiler_params=_cparams(("arbitrary",)),
        name="dispatch",
    )(fill_blocks, n_fill, _dest_blocks(dest, td), hn_a, hn_b)


def _experts_kernel(be_ref, bsrc_ref, nu_ref, x_ref, w1_ref, b1_ref, w2_ref, b2_ref,
                    y_ref, w1b_ref, w2b_ref, *, tme):
    i = pl.program_id(0)
    e = be_ref[i]
    e_prev = be_ref[jnp.maximum(i - 1, 0)]
    d_ff = w2_ref.shape[1]

    @pl.when((i == 0) | (e != e_prev))
    def _():
        w1b_ref[...] = w1_ref[0].astype(BF16)
        w2b_ref[...] = w2_ref[0].astype(BF16)

    @pl.when(i < nu_ref[0])
    def _():
        x = _load_slabs(x_ref, tme, BF16)
        h = jnp.dot(x, w1b_ref[...], preferred_element_type=F32) + b1_ref[0]
        x_glu = jnp.minimum(h[:, :d_ff], SWIGLU_LIMIT)
        x_lin = jnp.clip(h[:, d_ff:], -SWIGLU_LIMIT, SWIGLU_LIMIT)
        act = x_glu * _sigmoid(SWIGLU_ALPHA * x_glu) * (x_lin + 1.0)
        y = jnp.dot(act.astype(BF16), w2b_ref[...], preferred_element_type=F32) + b2_ref[0]
        _store_slabs(y_ref, y, tme)

    @pl.when(i >= nu_ref[0])
    def _():
        y_ref[...] = jnp.zeros(y_ref.shape, F32)


def _experts(xs, w1, b1, w2, b2, blk_exp, blk_src, n_used, *, tme):
    n_exp, d, h2 = w1.shape
    d_ff = w2.shape[1]
    nb = xs.shape[0] // (tme * SUBLANES)
    slab = lambda m: pl.BlockSpec((tme * SUBLANES, LANES), m)
    grid_spec = pltpu.PrefetchScalarGridSpec(
        num_scalar_prefetch=3,
        grid=(nb,),
        in_specs=[slab(lambda i, be, bs, nu: (bs[i], 0)),
                  pl.BlockSpec((1, d, h2), lambda i, be, bs, nu: (be[i], 0, 0)),
                  pl.BlockSpec((1, 1, h2), lambda i, be, bs, nu: (be[i], 0, 0)),
                  pl.BlockSpec((1, d_ff, d), lambda i, be, bs, nu: (be[i], 0, 0)),
                  pl.BlockSpec((1, 1, d), lambda i, be, bs, nu: (be[i], 0, 0))],
        out_specs=slab(lambda i, be, bs, nu: (i, 0)),
        scratch_shapes=[pltpu.VMEM((d, h2), BF16), pltpu.VMEM((d_ff, d), BF16)],
    )
    return pl.pallas_call(
        functools.partial(_experts_kernel, tme=tme),
        grid_spec=grid_spec,
        out_shape=jax.ShapeDtypeStruct(xs.shape, F32),
        compiler_params=_cparams(("arbitrary",)),
        name="experts",
    )(blk_exp, blk_src, n_used, xs, w1, b1.reshape(n_exp, 1, h2), w2, b2.reshape(n_exp, 1, d))


def _combine_kernel(dcur_ref, dnext_ref, x1_ref, route_ref, g_ref, ys_ref, o_ref, buf_ref, sem, *, tc):
    i = pl.program_id(0)
    slot = i % 2

    def gather(dref, s):
        def issue(r, carry):
            row = pl.multiple_of(r * SUBLANES, SUBLANES)
            for k in range(TOP_K):
                d = pl.multiple_of(dref[0, 0, k * tc + r] * SUBLANES, SUBLANES)
                pltpu.make_async_copy(ys_ref.at[pl.ds(d, SUBLANES)], buf_ref.at[s, k, pl.ds(row, SUBLANES)],
                                      sem.at[s]).start(priority=k % 2)
            return carry

        lax.fori_loop(0, tc, issue, 0, unroll=8)

    @pl.when(i == 0)
    def _():
        gather(dcur_ref, 0)

    @pl.when(i + 1 < pl.num_programs(0))
    def _():
        gather(dnext_ref, 1 - slot)

    for k in range(TOP_K):
        pltpu.make_async_copy(ys_ref.at[pl.ds(0, tc * SUBLANES)], buf_ref.at[slot, k], sem.at[slot]).wait()
    route = route_ref[...]
    x1 = x1_ref[...]
    chunks = []
    for j in range(SUBLANES):
        acc = x1[:, j * LANES:(j + 1) * LANES]
        for k in range(TOP_K):
            acc = acc + (route[:, 2 * TOP_K + k:2 * TOP_K + k + 1]
                         * buf_ref[slot, k, pl.ds(j, tc, stride=SUBLANES), :])
        chunks.append(acc)
    y = jnp.concatenate(chunks, axis=1)
    ms = jnp.mean(y * y, axis=-1, keepdims=True)
    o_ref[...] = y * lax.rsqrt(ms + RMS_EPS) * g_ref[...]


def _combine(x1, route, dest, ys, g):
    n, d = x1.shape
    tc = _tile(n, 128, 8)
    steps = n // tc
    dest3 = _dest_blocks(dest, tc)
    dspec = lambda m: pl.BlockSpec((1, 1, tc * TOP_K), m, memory_space=pltpu.SMEM)
    return pl.pallas_call(
        functools.partial(_combine_kernel, tc=tc),
        grid=(steps,),
        in_specs=[dspec(lambda i: (i, 0, 0)), dspec(lambda i: (jnp.minimum(i + 1, steps - 1), 0, 0)),
                  pl.BlockSpec((tc, d), lambda i: (i, 0)),
                  pl.BlockSpec((tc, LANES), lambda i: (i, 0)),
                  pl.BlockSpec((1, d), lambda i: (0, 0)),
                  pl.BlockSpec(memory_space=pl.ANY)],
        out_specs=pl.BlockSpec((tc, d), lambda i: (i, 0)),
        out_shape=jax.ShapeDtypeStruct((n, d), F32),
        scratch_shapes=[pltpu.VMEM((2, TOP_K, tc * SUBLANES, LANES), F32), pltpu.SemaphoreType.DMA((2,))],
        compiler_params=_cparams(("arbitrary",)),
        name="combine",
    )(dest3, dest3, x1, route, g, ys)


def _dest_rows(route_t, blk_start, tme):
    eidx = route_t[0:TOP_K].astype(jnp.int32)
    rank = route_t[TOP_K:2 * TOP_K].astype(jnp.int32)
    first = jnp.zeros_like(eidx)
    for e in range(N_EXPERTS):
        first = jnp.where(eidx == e, blk_start[e], first)
    return (first * tme + rank).astype(jnp.int32)


def _routing_tables(counts_f, *, tme, nb):
    counts = counts_f[:, 0].astype(jnp.int32)
    nblk = (counts + tme - 1) // tme
    blk_end = jnp.cumsum(nblk)
    blk_start = blk_end - nblk
    n_used = blk_end[-1]
    b = jnp.arange(nb, dtype=jnp.int32)
    used = b < n_used
    blk_exp = jnp.minimum(jnp.sum((b[:, None] >= blk_end[None, :]).astype(jnp.int32), axis=1), N_EXPERTS - 1)
    last_exp = jnp.max(jnp.where(nblk > 0, jnp.arange(N_EXPERTS, dtype=jnp.int32), 0))
    blk_exp = jnp.where(used, blk_exp, last_exp).astype(jnp.int32)
    blk_src = jnp.where(used, b, 0).astype(jnp.int32)
    has_tail = (nblk > 0) & (counts % tme != 0)
    partial = jnp.any((b[:, None] == blk_end[None, :] - 1) & has_tail[None, :], axis=1)
    needs_fill = partial | ~used
    fill_blocks = jnp.argsort(~needs_fill, stable=True).astype(jnp.int32)
    n_fill = jnp.sum(needs_fill).reshape(1).astype(jnp.int32)
    return blk_start, blk_exp, blk_src, n_used.reshape(1).astype(jnp.int32), fill_blocks, n_fill


def kernel(x_prompt, x_sample, cache_k, cache_v, state_conv, attn_norm_g, w_in, attn_sinks, conv_w, conv_b,
           conv_ln_g, conv_ln_b, w_out, ffn_norm_g, router_w, router_b, w1, b1, w2, b2, final_norm_g):
    depth = w_in.shape[0]
    assert depth == 1, "single-layer step"
    bp, sp, d = x_prompt.shape
    bs, ss, _ = x_sample.shape
    cw = conv_w.shape[2]
    np_, ns = bp * sp, bs * ss
    n_tok = np_ + ns
    assert sp % WINDOW == 0

    xp2 = x_prompt.reshape(np_, d)
    xs2 = x_sample.reshape(ns, d)
    w_in_bf = w_in[0].astype(BF16)
    w_out_bf = w_out[0].astype(BF16)
    g_attn = attn_norm_g[0].reshape(1, d)
    g_ffn = ffn_norm_g[0].reshape(1, d)
    sinks = attn_sinks[0]
    vec = lambda a: a.reshape(1, cw)

    tab_p = _rope_tables(jnp.arange(sp, dtype=jnp.int32))
    tms = _tile(ns, 512, max(ss, 16))
    tab_s = _rope_tables(PAST_LEN + (jnp.arange(tms, dtype=jnp.int32) % ss))
    qp, kp, vp, kfp, vfp, ap = _in_proj(xp2, g_attn, w_in_bf, tab_p, seq_period=sp, q_dtype=BF16, conv_width=cw)
    qs, _, _, kfs, vfs, as_ = _in_proj(xs2, g_attn, w_in_bf, tab_s, seq_period=None, q_dtype=F32, conv_width=cw)

    att_p = _attn_prompt(qp, kp, vp, sinks, batch=bp, seq=sp)
    ck = cache_k[0].reshape(bs, WINDOW, KV_WIDTH)
    cv_ = cache_v[0].reshape(bs, WINDOW, KV_WIDTH)
    att_s, nk_s, nv_s = _attn_sample(qs, kfs, vfs, ck, cv_, sinks, batch=bs, ts=ss)

    cv_p = _conv_prompt(ap, conv_w[0], vec(conv_b[0]), vec(conv_ln_g[0]), vec(conv_ln_b[0]), batch=bp, seq=sp)
    cv_s = _conv_sample(as_, state_conv[0], conv_w[0], vec(conv_b[0]), vec(conv_ln_g[0]), vec(conv_ln_b[0]),
                        batch=bs, ts=ss)

    n_exp = router_w.shape[2]
    assert n_exp == N_EXPERTS
    rwt_bf = router_w[0].T.astype(BF16)
    rbt = jnp.broadcast_to(router_b[0][:, None], (n_exp, LANES))
    zero_carry = jnp.zeros((n_exp, LANES), F32)
    x1p, hnp, route_p, rt_p, cnt_p = _outproj_router(xp2, att_p, cv_p, w_out_bf, g_ffn, rwt_bf, rbt, zero_carry)
    x1s, hns, route_s, rt_s, cnt = _outproj_router(xs2, att_s, cv_s, w_out_bf, g_ffn, rwt_bf, rbt, cnt_p)

    tme = EXPERT_BLOCK_ROWS
    nb = -(-(n_tok * TOP_K + N_EXPERTS * (tme - 1)) // tme)
    blk_start, blk_exp, blk_src, n_used, fill_blocks, n_fill = _routing_tables(cnt, tme=tme, nb=nb)
    dest_p = _dest_rows(rt_p, blk_start, tme)
    dest_s = _dest_rows(rt_s, blk_start, tme)
    xs_sorted = _dispatch(hnp, hns, jnp.concatenate([dest_p, dest_s], axis=1), fill_blocks, n_fill, nb=nb, tme=tme)
    ys = _experts(xs_sorted, w1[0], b1[0], w2[0], b2[0], blk_exp, blk_src, n_used, tme=tme)
    g_fin = final_norm_g.reshape(1, d)
    y_p = _combine(x1p, route_p, dest_p, ys, g_fin)
    y_s = _combine(x1s, route_s, dest_s, ys, g_fin)

    kv5 = lambda t, bb: t.reshape(bb, -1, KV_WIDTH)[:, -WINDOW:].reshape(bb, WINDOW, N_KV_HEADS, HEAD_DIM)
    new_k_p = kv5(kfp, bp)[None]
    new_v_p = kv5(vfp, bp)[None]
    ctx = CONV_K - 1
    new_c_p = ap.reshape(bp, sp, cw)[:, -ctx:][None]
    new_c_s = jnp.concatenate([state_conv[0], as_.reshape(bs, ss, cw)], axis=1)[:, -ctx:][None]
    return (y_p.reshape(bp, sp, d), y_s.reshape(bs, ss, d), new_k_p, new_v_p, new_c_p,
            kv5(nk_s, bs)[None], kv5(nv_s, bs)[None], new_c_s)
```

```python
import functools

import jax
import jax.numpy as jnp
from jax import lax
from jax.experimental import pallas as pl
from jax.experimental.pallas import tpu as pltpu

F32 = jnp.float32
BF16 = jnp.bfloat16

HEAD_DIM = 64
N_Q_HEADS = 8
N_KV_HEADS = 2
WINDOW = 128
ROPE_THETA = 500000.0
ROPE_DIM = 16
CONV_K = 31
N_EXPERTS = 32
TOP_K = 4
SWIGLU_LIMIT = 7.0
SWIGLU_ALPHA = 1.702
RMS_EPS = 1e-5
LN_EPS = 1e-5
PAST_LEN = 16384

LANES = 128
SUBLANES = 8
CONV_HALO = 32
VMEM_LIMIT = 56 * 1024 * 1024
EXPERT_BLOCK_ROWS = 512

ATTN_WIDTH = N_Q_HEADS * HEAD_DIM
KV_WIDTH = N_KV_HEADS * HEAD_DIM


def _tile(n, pref, mult=8):
    t = min(pref, n)
    while t > 0 and (n % t or t % mult):
        t -= 1
    assert t > 0, (n, pref, mult)
    return t


def _cparams(sem):
    return pltpu.CompilerParams(dimension_semantics=sem, vmem_limit_bytes=VMEM_LIMIT)


def _sigmoid(x):
    return 1.0 / (1.0 + jnp.exp(-x))


def _rope_tables(pos):
    half = ROPE_DIM // 2
    inv_freq = jnp.power(jnp.float32(ROPE_THETA), -jnp.arange(half, dtype=F32) * 2.0 / ROPE_DIM)
    ang = pos.astype(F32)[:, None] * inv_freq[None, :]
    cos, sin = jnp.cos(ang), jnp.sin(ang)
    l64 = jnp.arange(LANES) % HEAD_DIM
    f = l64 % half
    cos_l, sin_l = cos[:, f], sin[:, f]
    c = jnp.where(l64 < ROPE_DIM, cos_l, 1.0)
    s1 = jnp.where(l64 < half, -sin_l, 0.0)
    s2 = jnp.where((l64 >= half) & (l64 < ROPE_DIM), sin_l, 0.0)
    return c.astype(F32), s1.astype(F32), s2.astype(F32)


def _inproj_kernel(x_ref, g_ref, w_ref, c_ref, s1_ref, s2_ref,
                   q_ref, k_ref, v_ref, kf_ref, vf_ref, a_ref, *, conv_width):
    x = x_ref[...]
    ms = jnp.mean(x * x, axis=-1, keepdims=True)
    h = (x * lax.rsqrt(ms + RMS_EPS) * g_ref[...]).astype(BF16)
    z = jnp.dot(h, w_ref[...], preferred_element_type=F32)
    c, s1, s2 = c_ref[...], s1_ref[...], s2_ref[...]
    half = ROPE_DIM // 2

    def rot(t):
        return t * c + pltpu.roll(t, LANES - half, 1) * s1 + pltpu.roll(t, half, 1) * s2

    scale = HEAD_DIM ** -0.5
    for j in range(ATTN_WIDTH // LANES):
        q_ref[:, j * LANES:(j + 1) * LANES] = (rot(z[:, j * LANES:(j + 1) * LANES]) * scale).astype(q_ref.dtype)
    k0 = ATTN_WIDTH
    kr = rot(z[:, k0:k0 + KV_WIDTH])
    k_ref[...] = kr.astype(BF16)
    kf_ref[...] = kr
    v0 = k0 + KV_WIDTH
    vv = z[:, v0:v0 + KV_WIDTH]
    v_ref[...] = vv.astype(BF16)
    vf_ref[...] = vv
    u0 = v0 + KV_WIDTH
    g0 = u0 + conv_width
    a_ref[...] = z[:, u0:g0] * _sigmoid(z[:, g0:g0 + conv_width])


def _in_proj(x2, g, w_bf, tables, *, seq_period, q_dtype, conv_width):
    n, d = x2.shape
    in_w = w_bf.shape[1]
    if seq_period is None:
        tm = tables[0].shape[0]
        tmap = lambda i: (0, 0)
    else:
        tm = _tile(seq_period, 512, 16)
        per = seq_period // tm
        tmap = lambda i: (i % per, 0)
    assert n % tm == 0
    row = lambda w: pl.BlockSpec((tm, w), lambda i: (i, 0))
    tab = pl.BlockSpec((tm, LANES), tmap)
    return pl.pallas_call(
        functools.partial(_inproj_kernel, conv_width=conv_width),
        grid=(n // tm,),
        in_specs=[row(d), pl.BlockSpec((1, d), lambda i: (0, 0)),
                  pl.BlockSpec((d, in_w), lambda i: (0, 0)), tab, tab, tab],
        out_specs=[row(ATTN_WIDTH), row(KV_WIDTH), row(KV_WIDTH), row(KV_WIDTH), row(KV_WIDTH),
                   row(conv_width)],
        out_shape=[jax.ShapeDtypeStruct((n, ATTN_WIDTH), q_dtype),
                   jax.ShapeDtypeStruct((n, KV_WIDTH), BF16),
                   jax.ShapeDtypeStruct((n, KV_WIDTH), BF16),
                   jax.ShapeDtypeStruct((n, KV_WIDTH), F32),
                   jax.ShapeDtypeStruct((n, KV_WIDTH), F32),
                   jax.ShapeDtypeStruct((n, conv_width), F32)],
        compiler_params=_cparams(("parallel",)),
        name="in_proj",
    )(x2, g, w_bf, *tables)


def _dup_head(t, h):
    sw = pltpu.roll(t, HEAD_DIM, 1)
    low = lax.broadcasted_iota(jnp.int32, t.shape, 1) < HEAD_DIM
    return jnp.where(low, t, sw) if h == 0 else jnp.where(low, sw, t)


def _nt_dot(a, b):
    return lax.dot_general(a, b, (((1,), (1,)), ((), ())), preferred_element_type=F32)


def _attn_prompt_kernel(sink_ref, q_ref, kc_ref, kp_ref, vc_ref, vp_ref, o_ref, *, qb):
    j = pl.program_id(1)
    w = WINDOW
    k_all = jnp.concatenate([kp_ref[...], kc_ref[...]], axis=0).astype(F32)
    v_all = jnp.concatenate([vp_ref[...], vc_ref[...]], axis=0).astype(F32)
    r = lax.broadcasted_iota(jnp.int32, (w, 2 * w), 0)
    kk = lax.broadcasted_iota(jnp.int32, (w, 2 * w), 1)
    band = (kk > r) & (kk <= r + w)
    low = lax.broadcasted_iota(jnp.int32, (w, LANES), 1) < HEAD_DIM
    zero = jnp.zeros((w, LANES), BF16)
    group = N_Q_HEADS // N_KV_HEADS
    for h in range(N_KV_HEADS):
        kd_all = _dup_head(k_all, h).astype(BF16)
        vd_all = _dup_head(v_all, h).astype(BF16)
        for sub in range(qb):
            valid = band & ((kk >= w) | (j > 0)) if sub == 0 else band
            kd = kd_all[sub * w:(sub + 2) * w, :]
            vd = vd_all[sub * w:(sub + 2) * w, :]
            for jj in range(group // 2):
                col = (h * group // 2 + jj) * LANES
                qv = q_ref[sub * w:(sub + 1) * w, col:col + LANES]
                halves = []
                for half in range(2):
                    head = h * group + jj * 2 + half
                    qm = jnp.where(low if half == 0 else ~low, qv, zero)
                    s = jnp.where(valid, _nt_dot(qm, kd), -jnp.inf)
                    sink = sink_ref[head]
                    m = jnp.maximum(jnp.max(s, axis=-1, keepdims=True), sink)
                    p = jnp.exp(s - m)
                    den = jnp.sum(p, axis=-1, keepdims=True) + jnp.exp(sink - m)
                    o = jnp.dot(p.astype(BF16), vd, preferred_element_type=F32)
                    halves.append(o / den)
                o_ref[sub * w:(sub + 1) * w, col:col + LANES] = (
                    jnp.where(low, halves[0], halves[1]).astype(o_ref.dtype))


def _attn_prompt(q, k, v, sinks, *, batch, seq):
    nb = seq // WINDOW
    qb = 2 if nb % 2 == 0 else 1
    steps = nb // qb
    cur = lambda b, j: (b * steps + j, 0)
    prev = lambda b, j: (b * nb + jnp.maximum(j * qb - 1, 0), 0)
    return pl.pallas_call(
        functools.partial(_attn_prompt_kernel, qb=qb),
        grid=(batch, steps),
        in_specs=[pl.BlockSpec(memory_space=pltpu.SMEM),
                  pl.BlockSpec((qb * WINDOW, ATTN_WIDTH), cur),
                  pl.BlockSpec((qb * WINDOW, KV_WIDTH), cur), pl.BlockSpec((WINDOW, KV_WIDTH), prev),
                  pl.BlockSpec((qb * WINDOW, KV_WIDTH), cur), pl.BlockSpec((WINDOW, KV_WIDTH), prev)],
        out_specs=pl.BlockSpec((qb * WINDOW, ATTN_WIDTH), cur),
        out_shape=jax.ShapeDtypeStruct((batch * seq, ATTN_WIDTH), BF16),
        compiler_params=_cparams(("parallel", "parallel")),
        name="attn_prompt",
    )(sinks, q, k, k, v, v)


def _attn_sample_kernel(sink_ref, q_ref, kn_ref, vn_ref, ck_ref, cv_ref, o_ref, nk_ref, nv_ref, *, gb, ts):
    w = WINDOW
    group = N_Q_HEADS // N_KV_HEADS
    rows = group * ts
    low = lax.broadcasted_iota(jnp.int32, (ts, LANES), 1) < HEAD_DIM
    pad = jnp.zeros((ts, LANES), F32)
    s_c, s_n, v_dup = [], [], []
    for b in range(gb):
        kc, vc = ck_ref[b], cv_ref[b]
        kn, vn = kn_ref[b * ts:(b + 1) * ts, :], vn_ref[b * ts:(b + 1) * ts, :]
        nk_ref[b, 0:w - ts, :] = kc[ts:, :]
        nk_ref[b, w - ts:, :] = kn
        nv_ref[b, 0:w - ts, :] = vc[ts:, :]
        nv_ref[b, w - ts:, :] = vn
        knp = jnp.concatenate([kn, pad], axis=0)
        vnp = jnp.concatenate([vn, pad], axis=0)
        qb = q_ref[b * ts:(b + 1) * ts, :]
        for h in range(N_KV_HEADS):
            parts = []
            for jj in range(group // 2):
                col = (h * group // 2 + jj) * LANES
                qv = qb[:, col:col + LANES]
                parts += [jnp.where(low, qv, 0.0), jnp.where(low, 0.0, qv)]
            lhs = jnp.concatenate(parts, axis=0).astype(BF16)
            s_c.append(_nt_dot(lhs, _dup_head(kc, h).astype(BF16)))
            s_n.append(_nt_dot(lhs, _dup_head(knp, h).astype(BF16)))
            v_dup.append((_dup_head(vc, h).astype(BF16), _dup_head(vnp, h).astype(BF16)))
    s_c = jnp.concatenate(s_c, axis=0)
    s_n = jnp.concatenate(s_n, axis=0)
    n_rows = s_c.shape[0]
    ridx = lax.broadcasted_iota(jnp.int32, (n_rows, 1), 0)
    t_row = ridx % ts
    head_row = (ridx // ts) % N_Q_HEADS
    sink = jnp.zeros((n_rows, 1), F32)
    for hd in range(N_Q_HEADS):
        sink = jnp.where(head_row == hd, sink_ref[hd], sink)
    c_idx = lax.broadcasted_iota(jnp.int32, (n_rows, w), 1)
    n_idx = lax.broadcasted_iota(jnp.int32, (n_rows, 2 * ts), 1)
    s_c = jnp.where(c_idx > t_row, s_c, -jnp.inf)
    s_n = jnp.where(n_idx <= t_row, s_n, -jnp.inf)
    m = jnp.maximum(jnp.maximum(jnp.max(s_c, axis=-1, keepdims=True), jnp.max(s_n, axis=-1, keepdims=True)), sink)
    p_c = jnp.exp(s_c - m)
    p_n = jnp.exp(s_n - m)
    den = jnp.sum(p_c, axis=-1, keepdims=True) + jnp.sum(p_n, axis=-1, keepdims=True) + jnp.exp(sink - m)
    p_c = p_c.astype(BF16)
    p_n = p_n.astype(BF16)
    outs = []
    for b in range(gb):
        cols = []
        for h in range(N_KV_HEADS):
            ci = b * N_KV_HEADS + h
            sl = slice(ci * rows, (ci + 1) * rows)
            vdc, vdn = v_dup[ci]
            o = (jnp.dot(p_c[sl], vdc, preferred_element_type=F32)
                 + jnp.dot(p_n[sl], vdn, preferred_element_type=F32)) / den[sl]
            for jj in range(group // 2):
                lo_part = o[(2 * jj) * ts:(2 * jj + 1) * ts, :]
                hi_part = o[(2 * jj + 1) * ts:(2 * jj + 2) * ts, :]
                cols.append(jnp.where(low, lo_part, hi_part))
        outs.append(jnp.concatenate(cols, axis=1))
    o_ref[...] = jnp.concatenate(outs, axis=0).astype(o_ref.dtype)


def _attn_sample(q, kf, vf, cache_k, cache_v, sinks, *, batch, ts):
    assert ts % 8 == 0 and ts <= WINDOW
    gb = _tile(batch, 8, 2)
    tok = lambda w: pl.BlockSpec((gb * ts, w), lambda i: (i, 0))
    cache = pl.BlockSpec((gb, WINDOW, KV_WIDTH), lambda i: (i, 0, 0))
    cshape = jax.ShapeDtypeStruct((batch, WINDOW, KV_WIDTH), F32)
    return pl.pallas_call(
        functools.partial(_attn_sample_kernel, gb=gb, ts=ts),
        grid=(batch // gb,),
        in_specs=[pl.BlockSpec(memory_space=pltpu.SMEM), tok(ATTN_WIDTH), tok(KV_WIDTH), tok(KV_WIDTH),
                  cache, cache],
        out_specs=[tok(ATTN_WIDTH), cache, cache],
        out_shape=[jax.ShapeDtypeStruct((batch * ts, ATTN_WIDTH), BF16), cshape, cshape],
        compiler_params=_cparams(("parallel",)),
        name="attn_sample",
    )(sinks, q, kf, vf, cache_k, cache_v)


def _ln_swish(acc, b, lg, lb):
    y = acc + b
    mu = jnp.mean(y, axis=-1, keepdims=True)
    yc = y - mu
    var = jnp.mean(yc * yc, axis=-1, keepdims=True)
    yn = yc * lax.rsqrt(var + LN_EPS) * lg + lb
    return yn * _sigmoid(yn)


def _conv_prompt_kernel(a_ref, ap_ref, w_ref, b_ref, lg_ref, lb_ref, o_ref, win_ref, *, tt, rc):
    j = pl.program_id(1)
    n = CONV_HALO + tt
    win = jnp.concatenate([jnp.where(j > 0, ap_ref[...], 0.0), a_ref[...]], axis=0)
    win_ref[0] = win
    for r in range(1, SUBLANES):
        win_ref[r] = pltpu.roll(win, n - r, 0)
    off = CONV_HALO - (CONV_K - 1)
    b, lg, lb = b_ref[...], lg_ref[...], lb_ref[...]
    for c in range(tt // rc):
        acc = jnp.zeros((rc, a_ref.shape[1]), F32)
        for k in range(CONV_K):
            s = off + k
            base = c * rc + (s // SUBLANES) * SUBLANES
            acc = acc + w_ref[k:k + 1, :] * win_ref[s % SUBLANES, base:base + rc, :]
        o_ref[c * rc:(c + 1) * rc, :] = _ln_swish(acc, b, lg, lb).astype(o_ref.dtype)


def _conv_prompt(a, w, b, lg, lb, *, batch, seq):
    cw = a.shape[1]
    tt = _tile(seq, 256, CONV_HALO)
    rc = _tile(tt, 32, 16)
    nt = seq // tt
    per = tt // CONV_HALO
    cur = lambda bb, j: (bb * nt + j, 0)
    prev = lambda bb, j: (jnp.maximum((bb * nt + j) * per - 1, 0), 0)
    vec = pl.BlockSpec((1, cw), lambda bb, j: (0, 0))
    return pl.pallas_call(
        functools.partial(_conv_prompt_kernel, tt=tt, rc=rc),
        grid=(batch, nt),
        in_specs=[pl.BlockSpec((tt, cw), cur), pl.BlockSpec((CONV_HALO, cw), prev),
                  pl.BlockSpec((CONV_K, cw), lambda bb, j: (0, 0)), vec, vec, vec],
        out_specs=pl.BlockSpec((tt, cw), cur),
        out_shape=jax.ShapeDtypeStruct((batch * seq, cw), BF16),
        scratch_shapes=[pltpu.VMEM((SUBLANES, CONV_HALO + tt, cw), F32)],
        compiler_params=_cparams(("parallel", "parallel")),
        name="conv_prompt",
    )(a, a, w, b, lg, lb)


def _conv_sample_kernel(a_ref, st_ref, w_ref, b_ref, lg_ref, lb_ref, o_ref, win_ref, *, gb, ts):
    ctx = CONV_K - 1
    b, lg, lb = b_ref[...], lg_ref[...], lb_ref[...]
    for bb in range(gb):
        win_ref[bb, 0:ctx, :] = st_ref[bb]
        win_ref[bb, ctx:ctx + ts, :] = a_ref[bb * ts:(bb + 1) * ts, :]
    outs = []
    for bb in range(gb):
        acc = jnp.zeros((ts, a_ref.shape[1]), F32)
        for k in range(CONV_K):
            acc = acc + w_ref[k:k + 1, :] * win_ref[bb, k:k + ts, :]
        outs.append(_ln_swish(acc, b, lg, lb))
    o_ref[...] = jnp.concatenate(outs, axis=0).astype(o_ref.dtype)


def _conv_sample(a, state, w, b, lg, lb, *, batch, ts):
    cw = a.shape[1]
    ctx = CONV_K - 1
    gb = _tile(batch, 8, 2)
    vec = pl.BlockSpec((1, cw), lambda i: (0, 0))
    return pl.pallas_call(
        functools.partial(_conv_sample_kernel, gb=gb, ts=ts),
        grid=(batch // gb,),
        in_specs=[pl.BlockSpec((gb * ts, cw), lambda i: (i, 0)),
                  pl.BlockSpec((gb, ctx, cw), lambda i: (i, 0, 0)),
                  pl.BlockSpec((CONV_K, cw), lambda i: (0, 0)), vec, vec, vec],
        out_specs=pl.BlockSpec((gb * ts, cw), lambda i: (i, 0)),
        out_shape=jax.ShapeDtypeStruct((batch * ts, cw), BF16),
        scratch_shapes=[pltpu.VMEM((gb, ctx + ts + 2, cw), F32)],
        compiler_params=_cparams(("parallel",)),
        name="conv_sample",
    )(a, state, w, b, lg, lb)


def _outproj_router_kernel(x_ref, att_ref, cv_ref, wo_ref, g_ref, rw_ref, rb_ref, cin_ref,
                           x1_ref, hn_ref, route_ref, rt_ref, cnt_ref, carry_ref, *, ts):
    i = pl.program_id(0)

    @pl.when(i == 0)
    def _():
        carry_ref[...] = cin_ref[...]

    aw = att_ref.shape[1]
    n_exp = rw_ref.shape[0]
    r = lax.broadcasted_iota(jnp.int32, (ts, ts), 0)
    c = lax.broadcasted_iota(jnp.int32, (ts, ts), 1)
    before = (r < c).astype(BF16)
    eid = lax.broadcasted_iota(jnp.int32, (n_exp, ts), 0)
    carry = carry_ref[...][:, 0:1]
    for s in range(x_ref.shape[0] // ts):
        rs = slice(s * ts, (s + 1) * ts)
        mix = (jnp.dot(att_ref[rs, :], wo_ref[0:aw, :], preferred_element_type=F32)
               + jnp.dot(cv_ref[rs, :], wo_ref[aw:, :], preferred_element_type=F32))
        x1 = x_ref[rs, :] + mix
        x1_ref[rs, :] = x1
        ms = jnp.mean(x1 * x1, axis=-1, keepdims=True)
        hn = x1 * lax.rsqrt(ms + RMS_EPS) * g_ref[...]
        _store_slabs(hn_ref.at[pl.ds(s * ts * SUBLANES, ts * SUBLANES)], hn, ts)
        logits = _nt_dot(rw_ref[...], hn.astype(BF16)) + rb_ref[...][:, 0:1]
        onehot = jnp.zeros((n_exp, ts), F32)
        vals, idxs = [], []
        for _ in range(TOP_K):
            m = jnp.max(logits, axis=0, keepdims=True)
            idx = jnp.min(jnp.where(logits == m, eid, n_exp), axis=0, keepdims=True)
            sel = eid == idx
            onehot = onehot + sel.astype(F32)
            logits = jnp.where(sel, -jnp.inf, logits)
            vals.append(m)
            idxs.append(idx)
        es = [jnp.exp(v - vals[0]) for v in vals]
        den = es[0] + es[1] + es[2] + es[3]
        prefix = jnp.dot(onehot.astype(BF16), before, preferred_element_type=F32) + carry
        ranks = [jnp.sum(jnp.where(eid == idxs[k], prefix, 0.0), axis=0, keepdims=True) for k in range(TOP_K)]
        rows = [ix.astype(F32) for ix in idxs] + ranks
        rt_ref[:, rs] = jnp.concatenate(rows, axis=0)
        fields = jnp.concatenate(rows + [e / den for e in es] + [jnp.zeros((LANES - 3 * TOP_K, ts), F32)],
                                 axis=0)
        route_ref[rs, :] = jnp.transpose(fields)
        carry = carry + jnp.sum(onehot, axis=1, keepdims=True)
    carry_ref[...] = jnp.broadcast_to(carry, carry_ref.shape)
    cnt_ref[...] = jnp.broadcast_to(carry, cnt_ref.shape)


def _store_slabs(ref, val, rows):
    for j in range(val.shape[1] // LANES):
        ref[pl.ds(j, rows, stride=SUBLANES), :] = val[:, j * LANES:(j + 1) * LANES]


def _load_slabs(ref, rows, dtype):
    return jnp.concatenate([ref[pl.ds(j, rows, stride=SUBLANES), :].astype(dtype) for j in range(SUBLANES)],
                           axis=1)


def _outproj_router(x2, att, cv, wo_bf, g, rwt_bf, rbt, carry_in):
    n, d = x2.shape
    assert d == SUBLANES * LANES
    n_exp = rwt_bf.shape[0]
    tm = _tile(n, 512, 16)
    ts = _tile(tm, 512, 16)
    row = lambda w: pl.BlockSpec((tm, w), lambda i: (i, 0))
    full = lambda a: pl.BlockSpec(a.shape, lambda i: (0,) * a.ndim)
    return pl.pallas_call(
        functools.partial(_outproj_router_kernel, ts=ts),
        grid=(n // tm,),
        in_specs=[row(d), row(att.shape[1]), row(cv.shape[1]), full(wo_bf), full(g), full(rwt_bf), full(rbt),
                  full(carry_in)],
        out_specs=[row(d), pl.BlockSpec((tm * SUBLANES, LANES), lambda i: (i, 0)), row(LANES),
                   pl.BlockSpec((2 * TOP_K, tm), lambda i: (0, i)),
                   pl.BlockSpec((n_exp, LANES), lambda i: (0, 0))],
        out_shape=[jax.ShapeDtypeStruct((n, d), F32), jax.ShapeDtypeStruct((n * SUBLANES, LANES), F32),
                   jax.ShapeDtypeStruct((n, LANES), F32), jax.ShapeDtypeStruct((2 * TOP_K, n), F32),
                   jax.ShapeDtypeStruct((n_exp, LANES), F32)],
        scratch_shapes=[pltpu.VMEM((n_exp, LANES), F32)],
        compiler_params=_cparams(("arbitrary",)),
        name="outproj_router",
    )(x2, att, cv, wo_bf, g, rwt_bf, rbt, carry_in)


def _dispatch_kernel(fill_ref, nfill_ref, dest_ref, hn_ref, hn2_ref, xs_ref, zero_ref, sem, zsem,
                     *, td, tme, n_first):
    @pl.when(pl.program_id(0) == 0)
    def _():
        zero_ref[...] = jnp.zeros(zero_ref.shape, F32)

        def fill(f):
            row = pl.multiple_of(fill_ref[f] * (tme * SUBLANES), tme * SUBLANES)
            return pltpu.make_async_copy(zero_ref, xs_ref.at[pl.ds(row, tme * SUBLANES)], zsem)

        def start(f, carry):
            fill(f).start()
            return carry

        def wait(f, carry):
            fill(f).wait()
            return carry

        lax.fori_loop(0, nfill_ref[0], start, 0)
        lax.fori_loop(0, nfill_ref[0], wait, 0)

    def scatter_rows(src_ref):
        def issue(r, carry):
            src = src_ref.at[pl.ds(pl.multiple_of(r * SUBLANES, SUBLANES), SUBLANES)]
            for k in range(TOP_K):
                d = pl.multiple_of(dest_ref[0, 0, k * td + r] * SUBLANES, SUBLANES)
                pltpu.make_async_copy(src, xs_ref.at[pl.ds(d, SUBLANES)], sem).start(priority=k % 2)
            return carry

        lax.fori_loop(0, td, issue, 0, unroll=8)
        for _ in range(TOP_K):
            pltpu.make_async_copy(src_ref, xs_ref.at[pl.ds(0, td * SUBLANES)], sem).wait()

    @pl.when(pl.program_id(0) < n_first)
    def _():
        scatter_rows(hn_ref)

    @pl.when(pl.program_id(0) >= n_first)
    def _():
        scatter_rows(hn2_ref)


def _dest_blocks(dest, td):
    n = dest.shape[1]
    return dest.reshape(TOP_K, n // td, td).transpose(1, 0, 2).reshape(n // td, 1, TOP_K * td)


def _dispatch(hn_a, hn_b, dest, fill_blocks, n_fill, *, nb, tme):
    na = hn_a.shape[0] // SUBLANES
    nb_rows = hn_b.shape[0] // SUBLANES
    td = _tile(nb_rows, 256, 8)
    assert na % td == 0
    n_first = na // td
    steps = n_first + nb_rows // td
    slab = lambda m: pl.BlockSpec((td * SUBLANES, LANES), m)
    grid_spec = pltpu.PrefetchScalarGridSpec(
        num_scalar_prefetch=2,
        grid=(steps,),
        in_specs=[pl.BlockSpec((1, 1, td * TOP_K), lambda i, fb, nf: (i, 0, 0), memory_space=pltpu.SMEM),
                  slab(lambda i, fb, nf: (jnp.minimum(i, n_first - 1), 0)),
                  slab(lambda i, fb, nf: (jnp.maximum(i - n_first, 0), 0))],
        out_specs=pl.BlockSpec(memory_space=pl.ANY),
        scratch_shapes=[pltpu.VMEM((tme * SUBLANES, LANES), F32), pltpu.SemaphoreType.DMA(()),
                        pltpu.SemaphoreType.DMA(())],
    )
    return pl.pallas_call(
        functools.partial(_dispatch_kernel, td=td, tme=tme, n_first=n_first),
        grid_spec=grid_spec,
        out_shape=jax.ShapeDtypeStruct((nb * tme * SUBLANES, LANES), F32),
        compiler_params=_cparams(("arbitrary",)),
        name="dispatch",
    )(fill_blocks, n_fill, _dest_blocks(dest, td), hn_a, hn_b)


def _experts_kernel(be_ref, bsrc_ref, nu_ref, x_ref, w1_ref, b1_ref, w2_ref, b2_ref,
                    y_ref, w1b_ref, w2b_ref, *, tme):
    i = pl.program_id(0)
    e = be_ref[i]
    e_prev = be_ref[jnp.maximum(i - 1, 0)]
    d_ff = w2_ref.shape[1]

    @pl.when((i == 0) | (e != e_prev))
    def _():
        w1b_ref[...] = w1_ref[0].astype(BF16)
        w2b_ref[...] = w2_ref[0].astype(BF16)

    @pl.when(i < nu_ref[0])
    def _():
        x = _load_slabs(x_ref, tme, BF16)
        h = jnp.dot(x, w1b_ref[...], preferred_element_type=F32) + b1_ref[0]
        x_glu = jnp.minimum(h[:, :d_ff], SWIGLU_LIMIT)
        x_lin = jnp.clip(h[:, d_ff:], -SWIGLU_LIMIT, SWIGLU_LIMIT)
        act = x_glu * _sigmoid(SWIGLU_ALPHA * x_glu) * (x_lin + 1.0)
        y = jnp.dot(act.astype(BF16), w2b_ref[...], preferred_element_type=F32) + b2_ref[0]
        _store_slabs(y_ref, y, tme)

    @pl.when(i >= nu_ref[0])
    def _():
        y_ref[...] = jnp.zeros(y_ref.shape, F32)


def _experts(xs, w1, b1, w2, b2, blk_exp, blk_src, n_used, *, tme):
    n_exp, d, h2 = w1.shape
    d_ff = w2.shape[1]
    nb = xs.shape[0] // (tme * SUBLANES)
    slab = lambda m: pl.BlockSpec((tme * SUBLANES, LANES), m)
    grid_spec = pltpu.PrefetchScalarGridSpec(
        num_scalar_prefetch=3,
        grid=(nb,),
        in_specs=[slab(lambda i, be, bs, nu: (bs[i], 0)),
                  pl.BlockSpec((1, d, h2), lambda i, be, bs, nu: (be[i], 0, 0)),
                  pl.BlockSpec((1, 1, h2), lambda i, be, bs, nu: (be[i], 0, 0)),
                  pl.BlockSpec((1, d_ff, d), lambda i, be, bs, nu: (be[i], 0, 0)),
                  pl.BlockSpec((1, 1, d), lambda i, be, bs, nu: (be[i], 0, 0))],
        out_specs=slab(lambda i, be, bs, nu: (i, 0)),
        scratch_shapes=[pltpu.VMEM((d, h2), BF16), pltpu.VMEM((d_ff, d), BF16)],
    )
    return pl.pallas_call(
        functools.partial(_experts_kernel, tme=tme),
        grid_spec=grid_spec,
        out_shape=jax.ShapeDtypeStruct(xs.shape, F32),
        compiler_params=_cparams(("arbitrary",)),
        name="experts",
    )(blk_exp, blk_src, n_used, xs, w1, b1.reshape(n_exp, 1, h2), w2, b2.reshape(n_exp, 1, d))


def _combine_kernel(dcur_ref, dnext_ref, x1_ref, route_ref, g_ref, ys_ref, o_ref, buf_ref, sem, *, tc):
    i = pl.program_id(0)
    slot = i % 2

    def gather(dref, s):
        def issue(r, carry):
            row = pl.multiple_of(r * SUBLANES, SUBLANES)
            for k in range(TOP_K):
                d = pl.multiple_of(dref[0, 0, k * tc + r] * SUBLANES, SUBLANES)
                pltpu.make_async_copy(ys_ref.at[pl.ds(d, SUBLANES)], buf_ref.at[s, k, pl.ds(row, SUBLANES)],
                                      sem.at[s]).start(priority=k % 2)
            return carry

        lax.fori_loop(0, tc, issue, 0, unroll=8)

    @pl.when(i == 0)
    def _():
        gather(dcur_ref, 0)

    @pl.when(i + 1 < pl.num_programs(0))
    def _():
        gather(dnext_ref, 1 - slot)

    for k in range(TOP_K):
        pltpu.make_async_copy(ys_ref.at[pl.ds(0, tc * SUBLANES)], buf_ref.at[slot, k], sem.at[slot]).wait()
    route = route_ref[...]
    x1 = x1_ref[...]
    chunks = []
    for j in range(SUBLANES):
        acc = x1[:, j * LANES:(j + 1) * LANES]
        for k in range(TOP_K):
            acc = acc + (route[:, 2 * TOP_K + k:2 * TOP_K + k + 1]
                         * buf_ref[slot, k, pl.ds(j, tc, stride=SUBLANES), :])
        chunks.append(acc)
    y = jnp.concatenate(chunks, axis=1)
    ms = jnp.mean(y * y, axis=-1, keepdims=True)
    o_ref[...] = y * lax.rsqrt(ms + RMS_EPS) * g_ref[...]


def _combine(x1, route, dest, ys, g):
    n, d = x1.shape
    tc = _tile(n, 128, 8)
    steps = n // tc
    dest3 = _dest_blocks(dest, tc)
    dspec = lambda m: pl.BlockSpec((1, 1, tc * TOP_K), m, memory_space=pltpu.SMEM)
    return pl.pallas_call(
        functools.partial(_combine_kernel, tc=tc),
        grid=(steps,),
        in_specs=[dspec(lambda i: (i, 0, 0)), dspec(lambda i: (jnp.minimum(i + 1, steps - 1), 0, 0)),
                  pl.BlockSpec((tc, d), lambda i: (i, 0)),
                  pl.BlockSpec((tc, LANES), lambda i: (i, 0)),
                  pl.BlockSpec((1, d), lambda i: (0, 0)),
                  pl.BlockSpec(memory_space=pl.ANY)],
        out_specs=pl.BlockSpec((tc, d), lambda i: (i, 0)),
        out_shape=jax.ShapeDtypeStruct((n, d), F32),
        scratch_shapes=[pltpu.VMEM((2, TOP_K, tc * SUBLANES, LANES), F32), pltpu.SemaphoreType.DMA((2,))],
        compiler_params=_cparams(("arbitrary",)),
        name="combine",
    )(dest3, dest3, x1, route, g, ys)


def _dest_rows(route_t, blk_start, tme):
    eidx = route_t[0:TOP_K].astype(jnp.int32)
    rank = route_t[TOP_K:2 * TOP_K].astype(jnp.int32)
    experts = jnp.arange(N_EXPERTS, dtype=jnp.int32)[:, None, None]
    first = jnp.sum(jnp.where(eidx[None] == experts, blk_start[:, None, None], 0), axis=0)
    return (first * tme + rank).astype(jnp.int32)


def _routing_tables(counts_f, *, tme, nb):
    counts = counts_f[:, 0].astype(jnp.int32)
    nblk = (counts + tme - 1) // tme
    blk_end = jnp.cumsum(nblk)
    blk_start = blk_end - nblk
    n_used = blk_end[-1]
    b = jnp.arange(nb, dtype=jnp.int32)
    used = b < n_used
    blk_exp = jnp.minimum(jnp.sum((b[:, None] >= blk_end[None, :]).astype(jnp.int32), axis=1), N_EXPERTS - 1)
    last_exp = jnp.max(jnp.where(nblk > 0, jnp.arange(N_EXPERTS, dtype=jnp.int32), 0))
    blk_exp = jnp.where(used, blk_exp, last_exp).astype(jnp.int32)
    blk_src = jnp.where(used, b, 0).astype(jnp.int32)
    has_tail = (nblk > 0) & (counts % tme != 0)
    partial = jnp.any((b[:, None] == blk_end[None, :] - 1) & has_tail[None, :], axis=1)
    needs_fill = partial | ~used
    fill_blocks = jnp.argsort(~needs_fill, stable=True).astype(jnp.int32)
    n_fill = jnp.sum(needs_fill).reshape(1).astype(jnp.int32)
    return blk_start, blk_exp, blk_src, n_used.reshape(1).astype(jnp.int32), fill_blocks, n_fill


def kernel(x_prompt, x_sample, cache_k, cache_v, state_conv, attn_norm_g, w_in, attn_sinks, conv_w, conv_b,
           conv_ln_g, conv_ln_b, w_out, ffn_norm_g, router_w, router_b, w1, b1, w2, b2, final_norm_g):
    depth = w_in.shape[0]
    assert depth == 1, "single-layer step"
    bp, sp, d = x_prompt.shape
    bs, ss, _ = x_sample.shape
    cw = conv_w.shape[2]
    np_, ns = bp * sp, bs * ss
    n_tok = np_ + ns
    assert sp % WINDOW == 0

    xp2 = x_prompt.reshape(np_, d)
    xs2 = x_sample.reshape(ns, d)
    w_in_bf = w_in[0].astype(BF16)
    w_out_bf = w_out[0].astype(BF16)
    g_attn = attn_norm_g[0].reshape(1, d)
    g_ffn = ffn_norm_g[0].reshape(1, d)
    sinks = attn_sinks[0]
    vec = lambda a: a.reshape(1, cw)

    tab_p = _rope_tables(jnp.arange(sp, dtype=jnp.int32))
    tms = _tile(ns, 512, max(ss, 16))
    tab_s = _rope_tables(PAST_LEN + (jnp.arange(tms, dtype=jnp.int32) % ss))
    qp, kp, vp, kfp, vfp, ap = _in_proj(xp2, g_attn, w_in_bf, tab_p, seq_period=sp, q_dtype=BF16, conv_width=cw)
    qs, _, _, kfs, vfs, as_ = _in_proj(xs2, g_attn, w_in_bf, tab_s, seq_period=None, q_dtype=F32, conv_width=cw)

    att_p = _attn_prompt(qp, kp, vp, sinks, batch=bp, seq=sp)
    ck = cache_k[0].reshape(bs, WINDOW, KV_WIDTH)
    cv_ = cache_v[0].reshape(bs, WINDOW, KV_WIDTH)
    att_s, nk_s, nv_s = _attn_sample(qs, kfs, vfs, ck, cv_, sinks, batch=bs, ts=ss)

    cv_p = _conv_prompt(ap, conv_w[0], vec(conv_b[0]), vec(conv_ln_g[0]), vec(conv_ln_b[0]), batch=bp, seq=sp)
    cv_s = _conv_sample(as_, state_conv[0], conv_w[0], vec(conv_b[0]), vec(conv_ln_g[0]), vec(conv_ln_b[0]),
                        batch=bs, ts=ss)

    n_exp = router_w.shape[2]
    assert n_exp == N_EXPERTS
    rwt_bf = router_w[0].T.astype(BF16)
    rbt = jnp.broadcast_to(router_b[0][:, None], (n_exp, LANES))
    zero_carry = jnp.zeros((n_exp, LANES), F32)
    x1p, hnp, route_p, rt_p, cnt_p = _outproj_router(xp2, att_p, cv_p, w_out_bf, g_ffn, rwt_bf, rbt, zero_carry)
    x1s, hns, route_s, rt_s, cnt = _outproj_router(xs2, att_s, cv_s, w_out_bf, g_ffn, rwt_bf, rbt, cnt_p)

    tme = EXPERT_BLOCK_ROWS
    nb = -(-(n_tok * TOP_K + N_EXPERTS * (tme - 1)) // tme)
    blk_start, blk_exp, blk_src, n_used, fill_blocks, n_fill = _routing_tables(cnt, tme=tme, nb=nb)
    dest_p = _dest_rows(rt_p, blk_start, tme)
    dest_s = _dest_rows(rt_s, blk_start, tme)
    xs_sorted = _dispatch(hnp, hns, jnp.concatenate([dest_p, dest_s], axis=1), fill_blocks, n_fill, nb=nb, tme=tme)
    ys = _experts(xs_sorted, w1[0], b1[0], w2[0], b2[0], blk_exp, blk_src, n_used, tme=tme)
    g_fin = final_norm_g.reshape(1, d)
    y_p = _combine(x1p, route_p, dest_p, ys, g_fin)
    y_s = _combine(x1s, route_s, dest_s, ys, g_fin)

    kv5 = lambda t, bb: t.reshape(bb, -1, KV_WIDTH)[:, -WINDOW:].reshape(bb, WINDOW, N_KV_HEADS, HEAD_DIM)
    new_k_p = kv5(kfp, bp)[None]
    new_v_p = kv5(vfp, bp)[None]
    ctx = CONV_K - 1
    new_c_p = ap.reshape(bp, sp, cw)[:, -ctx:][None]
    new_c_s = jnp.concatenate([state_conv[0], as_.reshape(bs, ss, cw)], axis=1)[:, -ctx:][None]
    return (y_p.reshape(bp, sp, d), y_s.reshape(bs, ss, d), new_k_p, new_v_p, new_c_p,
            kv5(nk_s, bs)[None], kv5(nv_s, bs)[None], new_c_s)
```

```python
import functools

import jax
import jax.numpy as jnp
from jax import lax
from jax.experimental import pallas as pl
from jax.experimental.pallas import tpu as pltpu

F32 = jnp.float32
BF16 = jnp.bfloat16

HEAD_DIM = 64
N_Q_HEADS = 8
N_KV_HEADS = 2
WINDOW = 128
ROPE_THETA = 500000.0
ROPE_DIM = 16
CONV_K = 31
N_EXPERTS = 32
TOP_K = 4
SWIGLU_LIMIT = 7.0
SWIGLU_ALPHA = 1.702
RMS_EPS = 1e-5
LN_EPS = 1e-5
PAST_LEN = 16384

LANES = 128
SUBLANES = 8
CONV_HALO = 32
VMEM_LIMIT = 56 * 1024 * 1024
EXPERT_BLOCK_ROWS = 512

ATTN_WIDTH = N_Q_HEADS * HEAD_DIM
KV_WIDTH = N_KV_HEADS * HEAD_DIM


def _tile(n, pref, mult=8):
    t = min(pref, n)
    while t > 0 and (n % t or t % mult):
        t -= 1
    assert t > 0, (n, pref, mult)
    return t


def _cparams(sem):
    return pltpu.CompilerParams(dimension_semantics=sem, vmem_limit_bytes=VMEM_LIMIT)


def _sigmoid(x):
    return 1.0 / (1.0 + jnp.exp(-x))


def _rope_tables(pos):
    half = ROPE_DIM // 2
    inv_freq = jnp.power(jnp.float32(ROPE_THETA), -jnp.arange(half, dtype=F32) * 2.0 / ROPE_DIM)
    ang = pos.astype(F32)[:, None] * inv_freq[None, :]
    cos, sin = jnp.cos(ang), jnp.sin(ang)
    l64 = jnp.arange(LANES) % HEAD_DIM
    f = l64 % half
    cos_l, sin_l = cos[:, f], sin[:, f]
    c = jnp.where(l64 < ROPE_DIM, cos_l, 1.0)
    s1 = jnp.where(l64 < half, -sin_l, 0.0)
    s2 = jnp.where((l64 >= half) & (l64 < ROPE_DIM), sin_l, 0.0)
    return c.astype(F32), s1.astype(F32), s2.astype(F32)


def _inproj_kernel(x_ref, g_ref, w_ref, c_ref, s1_ref, s2_ref,
                   q_ref, k_ref, v_ref, kf_ref, vf_ref, a_ref, *, conv_width):
    x = x_ref[...]
    ms = jnp.mean(x * x, axis=-1, keepdims=True)
    h = (x * lax.rsqrt(ms + RMS_EPS) * g_ref[...]).astype(BF16)
    z = jnp.dot(h, w_ref[...], preferred_element_type=F32)
    c, s1, s2 = c_ref[...], s1_ref[...], s2_ref[...]
    half = ROPE_DIM // 2

    def rot(t):
        return t * c + pltpu.roll(t, LANES - half, 1) * s1 + pltpu.roll(t, half, 1) * s2

    scale = HEAD_DIM ** -0.5
    for j in range(ATTN_WIDTH // LANES):
        q_ref[:, j * LANES:(j + 1) * LANES] = (rot(z[:, j * LANES:(j + 1) * LANES]) * scale).astype(q_ref.dtype)
    k0 = ATTN_WIDTH
    kr = rot(z[:, k0:k0 + KV_WIDTH])
    k_ref[...] = kr.astype(BF16)
    kf_ref[...] = kr
    v0 = k0 + KV_WIDTH
    vv = z[:, v0:v0 + KV_WIDTH]
    v_ref[...] = vv.astype(BF16)
    vf_ref[...] = vv
    u0 = v0 + KV_WIDTH
    g0 = u0 + conv_width
    a_ref[...] = z[:, u0:g0] * _sigmoid(z[:, g0:g0 + conv_width])


def _in_proj(x2, g, w_bf, tables, *, seq_period, q_dtype, conv_width):
    n, d = x2.shape
    in_w = w_bf.shape[1]
    if seq_period is None:
        tm = tables[0].shape[0]
        tmap = lambda i: (0, 0)
    else:
        tm = _tile(seq_period, 512, 16)
        per = seq_period // tm
        tmap = lambda i: (i % per, 0)
    assert n % tm == 0
    row = lambda w: pl.BlockSpec((tm, w), lambda i: (i, 0))
    tab = pl.BlockSpec((tm, LANES), tmap)
    return pl.pallas_call(
        functools.partial(_inproj_kernel, conv_width=conv_width),
        grid=(n // tm,),
        in_specs=[row(d), pl.BlockSpec((1, d), lambda i: (0, 0)),
                  pl.BlockSpec((d, in_w), lambda i: (0, 0)), tab, tab, tab],
        out_specs=[row(ATTN_WIDTH), row(KV_WIDTH), row(KV_WIDTH), row(KV_WIDTH), row(KV_WIDTH),
                   row(conv_width)],
        out_shape=[jax.ShapeDtypeStruct((n, ATTN_WIDTH), q_dtype),
                   jax.ShapeDtypeStruct((n, KV_WIDTH), BF16),
                   jax.ShapeDtypeStruct((n, KV_WIDTH), BF16),
                   jax.ShapeDtypeStruct((n, KV_WIDTH), F32),
                   jax.ShapeDtypeStruct((n, KV_WIDTH), F32),
                   jax.ShapeDtypeStruct((n, conv_width), F32)],
        compiler_params=_cparams(("parallel",)),
        name="in_proj",
    )(x2, g, w_bf, *tables)


def _dup_head(t, h):
    sw = pltpu.roll(t, HEAD_DIM, 1)
    low = lax.broadcasted_iota(jnp.int32, t.shape, 1) < HEAD_DIM
    return jnp.where(low, t, sw) if h == 0 else jnp.where(low, sw, t)


def _nt_dot(a, b):
    return lax.dot_general(a, b, (((1,), (1,)), ((), ())), preferred_element_type=F32)


def _attn_prompt_kernel(sink_ref, q_ref, kc_ref, kp_ref, vc_ref, vp_ref, o_ref, *, qb):
    j = pl.program_id(1)
    w = WINDOW
    k_all = jnp.concatenate([kp_ref[...], kc_ref[...]], axis=0).astype(F32)
    v_all = jnp.concatenate([vp_ref[...], vc_ref[...]], axis=0).astype(F32)
    r = lax.broadcasted_iota(jnp.int32, (w, 2 * w), 0)
    kk = lax.broadcasted_iota(jnp.int32, (w, 2 * w), 1)
    band = (kk > r) & (kk <= r + w)
    low = lax.broadcasted_iota(jnp.int32, (w, LANES), 1) < HEAD_DIM
    zero = jnp.zeros((w, LANES), BF16)
    group = N_Q_HEADS // N_KV_HEADS
    for h in range(N_KV_HEADS):
        kd_all = _dup_head(k_all, h).astype(BF16)
        vd_all = _dup_head(v_all, h).astype(BF16)
        for sub in range(qb):
            valid = band & ((kk >= w) | (j > 0)) if sub == 0 else band
            kd = kd_all[sub * w:(sub + 2) * w, :]
            vd = vd_all[sub * w:(sub + 2) * w, :]
            for jj in range(group // 2):
                col = (h * group // 2 + jj) * LANES
                qv = q_ref[sub * w:(sub + 1) * w, col:col + LANES]
                halves = []
                for half in range(2):
                    head = h * group + jj * 2 + half
                    qm = jnp.where(low if half == 0 else ~low, qv, zero)
                    s = jnp.where(valid, _nt_dot(qm, kd), -jnp.inf)
                    sink = sink_ref[head]
                    m = jnp.maximum(jnp.max(s, axis=-1, keepdims=True), sink)
                    p = jnp.exp(s - m)
                    den = jnp.sum(p, axis=-1, keepdims=True) + jnp.exp(sink - m)
                    o = jnp.dot(p.astype(BF16), vd, preferred_element_type=F32)
                    halves.append(o / den)
                o_ref[sub * w:(sub + 1) * w, col:col + LANES] = (
                    jnp.where(low, halves[0], halves[1]).astype(o_ref.dtype))


def _attn_prompt(q, k, v, sinks, *, batch, seq):
    nb = seq // WINDOW
    qb = 2 if nb % 2 == 0 else 1
    steps = nb // qb
    cur = lambda b, j: (b * steps + j, 0)
    prev = lambda b, j: (b * nb + jnp.maximum(j * qb - 1, 0), 0)
    return pl.pallas_call(
        functools.partial(_attn_prompt_kernel, qb=qb),
        grid=(batch, steps),
        in_specs=[pl.BlockSpec(memory_space=pltpu.SMEM),
                  pl.BlockSpec((qb * WINDOW, ATTN_WIDTH), cur),
                  pl.BlockSpec((qb * WINDOW, KV_WIDTH), cur), pl.BlockSpec((WINDOW, KV_WIDTH), prev),
                  pl.BlockSpec((qb * WINDOW, KV_WIDTH), cur), pl.BlockSpec((WINDOW, KV_WIDTH), prev)],
        out_specs=pl.BlockSpec((qb * WINDOW, ATTN_WIDTH), cur),
        out_shape=jax.ShapeDtypeStruct((batch * seq, ATTN_WIDTH), BF16),
        compiler_params=_cparams(("parallel", "parallel")),
        name="attn_prompt",
    )(sinks, q, k, k, v, v)


def _attn_sample_kernel(sink_ref, q_ref, kn_ref, vn_ref, ck_ref, cv_ref, o_ref, nk_ref, nv_ref, *, gb, ts):
    w = WINDOW
    group = N_Q_HEADS // N_KV_HEADS
    rows = group * ts
    low = lax.broadcasted_iota(jnp.int32, (ts, LANES), 1) < HEAD_DIM
    pad = jnp.zeros((ts, LANES), F32)
    s_c, s_n, v_dup = [], [], []
    for b in range(gb):
        kc, vc = ck_ref[b], cv_ref[b]
        kn, vn = kn_ref[b * ts:(b + 1) * ts, :], vn_ref[b * ts:(b + 1) * ts, :]
        nk_ref[b, 0:w - ts, :] = kc[ts:, :]
        nk_ref[b, w - ts:, :] = kn
        nv_ref[b, 0:w - ts, :] = vc[ts:, :]
        nv_ref[b, w - ts:, :] = vn
        knp = jnp.concatenate([kn, pad], axis=0)
        vnp = jnp.concatenate([vn, pad], axis=0)
        qb = q_ref[b * ts:(b + 1) * ts, :]
        for h in range(N_KV_HEADS):
            parts = []
            for jj in range(group // 2):
                col = (h * group // 2 + jj) * LANES
                qv = qb[:, col:col + LANES]
                parts += [jnp.where(low, qv, 0.0), jnp.where(low, 0.0, qv)]
            lhs = jnp.concatenate(parts, axis=0).astype(BF16)
            s_c.append(_nt_dot(lhs, _dup_head(kc, h).astype(BF16)))
            s_n.append(_nt_dot(lhs, _dup_head(knp, h).astype(BF16)))
            v_dup.append((_dup_head(vc, h).astype(BF16), _dup_head(vnp, h).astype(BF16)))
    s_c = jnp.concatenate(s_c, axis=0)
    s_n = jnp.concatenate(s_n, axis=0)
    n_rows = s_c.shape[0]
    ridx = lax.broadcasted_iota(jnp.int32, (n_rows, 1), 0)
    t_row = ridx % ts
    head_row = (ridx // ts) % N_Q_HEADS
    sink = jnp.zeros((n_rows, 1), F32)
    for hd in range(N_Q_HEADS):
        sink = jnp.where(head_row == hd, sink_ref[hd], sink)
    c_idx = lax.broadcasted_iota(jnp.int32, (n_rows, w), 1)
    n_idx = lax.broadcasted_iota(jnp.int32, (n_rows, 2 * ts), 1)
    s_c = jnp.where(c_idx > t_row, s_c, -jnp.inf)
    s_n = jnp.where(n_idx <= t_row, s_n, -jnp.inf)
    m = jnp.maximum(jnp.maximum(jnp.max(s_c, axis=-1, keepdims=True), jnp.max(s_n, axis=-1, keepdims=True)), sink)
    p_c = jnp.exp(s_c - m)
    p_n = jnp.exp(s_n - m)
    den = jnp.sum(p_c, axis=-1, keepdims=True) + jnp.sum(p_n, axis=-1, keepdims=True) + jnp.exp(sink - m)
    p_c = p_c.astype(BF16)
    p_n = p_n.astype(BF16)
    outs = []
    for b in range(gb):
        cols = []
        for h in range(N_KV_HEADS):
            ci = b * N_KV_HEADS + h
            sl = slice(ci * rows, (ci + 1) * rows)
            vdc, vdn = v_dup[ci]
            o = (jnp.dot(p_c[sl], vdc, preferred_element_type=F32)
                 + jnp.dot(p_n[sl], vdn, preferred_element_type=F32)) / den[sl]
            for jj in range(group // 2):
                lo_part = o[(2 * jj) * ts:(2 * jj + 1) * ts, :]
                hi_part = o[(2 * jj + 1) * ts:(2 * jj + 2) * ts, :]
                cols.append(jnp.where(low, lo_part, hi_part))
        outs.append(jnp.concatenate(cols, axis=1))
    o_ref[...] = jnp.concatenate(outs, axis=0).astype(o_ref.dtype)


def _attn_sample(q, kf, vf, cache_k, cache_v, sinks, *, batch, ts):
    assert ts % 8 == 0 and ts <= WINDOW
    gb = _tile(batch, 8, 2)
    tok = lambda w: pl.BlockSpec((gb * ts, w), lambda i: (i, 0))
    cache = pl.BlockSpec((gb, WINDOW, KV_WIDTH), lambda i: (i, 0, 0))
    cshape = jax.ShapeDtypeStruct((batch, WINDOW, KV_WIDTH), F32)
    return pl.pallas_call(
        functools.partial(_attn_sample_kernel, gb=gb, ts=ts),
        grid=(batch // gb,),
        in_specs=[pl.BlockSpec(memory_space=pltpu.SMEM), tok(ATTN_WIDTH), tok(KV_WIDTH), tok(KV_WIDTH),
                  cache, cache],
        out_specs=[tok(ATTN_WIDTH), cache, cache],
        out_shape=[jax.ShapeDtypeStruct((batch * ts, ATTN_WIDTH), BF16), cshape, cshape],
        compiler_params=_cparams(("parallel",)),
        name="attn_sample",
    )(sinks, q, kf, vf, cache_k, cache_v)


def _ln_swish(acc, b, lg, lb):
    y = acc + b
    mu = jnp.mean(y, axis=-1, keepdims=True)
    yc = y - mu
    var = jnp.mean(yc * yc, axis=-1, keepdims=True)
    yn = yc * lax.rsqrt(var + LN_EPS) * lg + lb
    return yn * _sigmoid(yn)


def _conv_prompt_kernel(a_ref, ap_ref, w_ref, b_ref, lg_ref, lb_ref, o_ref, win_ref, *, tt, rc):
    j = pl.program_id(1)
    n = CONV_HALO + tt
    win = jnp.concatenate([jnp.where(j > 0, ap_ref[...], 0.0), a_ref[...]], axis=0)
    win_ref[0] = win
    for r in range(1, SUBLANES):
        win_ref[r] = pltpu.roll(win, n - r, 0)
    off = CONV_HALO - (CONV_K - 1)
    b, lg, lb = b_ref[...], lg_ref[...], lb_ref[...]
    for c in range(tt // rc):
        acc = jnp.zeros((rc, a_ref.shape[1]), F32)
        for k in range(CONV_K):
            s = off + k
            base = c * rc + (s // SUBLANES) * SUBLANES
            wk = jnp.concatenate([w_ref[k]] * (rc // SUBLANES), axis=0)
            acc = acc + wk * win_ref[s % SUBLANES, base:base + rc, :]
        o_ref[c * rc:(c + 1) * rc, :] = _ln_swish(acc, b, lg, lb).astype(o_ref.dtype)


def _conv_prompt(a, w, b, lg, lb, *, batch, seq):
    cw = a.shape[1]
    tt = _tile(seq, 256, CONV_HALO)
    rc = _tile(tt, 32, 16)
    nt = seq // tt
    per = tt // CONV_HALO
    cur = lambda bb, j: (bb * nt + j, 0)
    prev = lambda bb, j: (jnp.maximum((bb * nt + j) * per - 1, 0), 0)
    vec = pl.BlockSpec((1, cw), lambda bb, j: (0, 0))
    return pl.pallas_call(
        functools.partial(_conv_prompt_kernel, tt=tt, rc=rc),
        grid=(batch, nt),
        in_specs=[pl.BlockSpec((tt, cw), cur), pl.BlockSpec((CONV_HALO, cw), prev),
                  pl.BlockSpec((CONV_K, SUBLANES, cw), lambda bb, j: (0, 0, 0)), vec, vec, vec],
        out_specs=pl.BlockSpec((tt, cw), cur),
        out_shape=jax.ShapeDtypeStruct((batch * seq, cw), BF16),
        scratch_shapes=[pltpu.VMEM((SUBLANES, CONV_HALO + tt, cw), F32)],
        compiler_params=_cparams(("parallel", "parallel")),
        name="conv_prompt",
    )(a, a, jnp.broadcast_to(w[:, None, :], (CONV_K, SUBLANES, cw)), b, lg, lb)


def _conv_sample_kernel(a_ref, st_ref, w_ref, b_ref, lg_ref, lb_ref, o_ref, win_ref, *, gb, ts):
    ctx = CONV_K - 1
    b, lg, lb = b_ref[...], lg_ref[...], lb_ref[...]
    for bb in range(gb):
        win_ref[bb, 0:ctx, :] = st_ref[bb]
        win_ref[bb, ctx:ctx + ts, :] = a_ref[bb * ts:(bb + 1) * ts, :]
    outs = []
    for bb in range(gb):
        acc = jnp.zeros((ts, a_ref.shape[1]), F32)
        for k in range(CONV_K):
            wk = jnp.concatenate([w_ref[k]] * (ts // SUBLANES), axis=0)
            acc = acc + wk * win_ref[bb, k:k + ts, :]
        outs.append(_ln_swish(acc, b, lg, lb))
    o_ref[...] = jnp.concatenate(outs, axis=0).astype(o_ref.dtype)


def _conv_sample(a, state, w, b, lg, lb, *, batch, ts):
    cw = a.shape[1]
    ctx = CONV_K - 1
    gb = _tile(batch, 8, 2)
    vec = pl.BlockSpec((1, cw), lambda i: (0, 0))
    return pl.pallas_call(
        functools.partial(_conv_sample_kernel, gb=gb, ts=ts),
        grid=(batch // gb,),
        in_specs=[pl.BlockSpec((gb * ts, cw), lambda i: (i, 0)),
                  pl.BlockSpec((gb, ctx, cw), lambda i: (i, 0, 0)),
                  pl.BlockSpec((CONV_K, SUBLANES, cw), lambda i: (0, 0, 0)), vec, vec, vec],
        out_specs=pl.BlockSpec((gb * ts, cw), lambda i: (i, 0)),
        out_shape=jax.ShapeDtypeStruct((batch * ts, cw), BF16),
        scratch_shapes=[pltpu.VMEM((gb, ctx + ts + 2, cw), F32)],
        compiler_params=_cparams(("parallel",)),
        name="conv_sample",
    )(a, state, jnp.broadcast_to(w[:, None, :], (CONV_K, SUBLANES, cw)), b, lg, lb)


def _outproj_router_kernel(x_ref, att_ref, cv_ref, wo_ref, g_ref, rw_ref, rb_ref, cin_ref,
                           x1_ref, hn_ref, route_ref, rt_ref, cnt_ref, carry_ref, *, ts):
    i = pl.program_id(0)

    @pl.when(i == 0)
    def _():
        carry_ref[...] = cin_ref[...]

    aw = att_ref.shape[1]
    n_exp = rw_ref.shape[0]
    r = lax.broadcasted_iota(jnp.int32, (ts, ts), 0)
    c = lax.broadcasted_iota(jnp.int32, (ts, ts), 1)
    before = (r < c).astype(BF16)
    eid = lax.broadcasted_iota(jnp.int32, (n_exp, ts), 0)
    carry = carry_ref[...][:, 0:1]
    for s in range(x_ref.shape[0] // ts):
        rs = slice(s * ts, (s + 1) * ts)
        mix = (jnp.dot(att_ref[rs, :], wo_ref[0:aw, :], preferred_element_type=F32)
               + jnp.dot(cv_ref[rs, :], wo_ref[aw:, :], preferred_element_type=F32))
        x1 = x_ref[rs, :] + mix
        x1_ref[rs, :] = x1
        ms = jnp.mean(x1 * x1, axis=-1, keepdims=True)
        hn = x1 * lax.rsqrt(ms + RMS_EPS) * g_ref[...]
        _store_slabs(hn_ref.at[pl.ds(s * ts * SUBLANES, ts * SUBLANES)], hn, ts)
        logits = _nt_dot(rw_ref[...], hn.astype(BF16)) + rb_ref[...][:, 0:1]
        onehot = jnp.zeros((n_exp, ts), F32)
        vals, idxs = [], []
        for _ in range(TOP_K):
            m = jnp.max(logits, axis=0, keepdims=True)
            idx = jnp.min(jnp.where(logits == m, eid, n_exp), axis=0, keepdims=True)
            sel = eid == idx
            onehot = onehot + sel.astype(F32)
            logits = jnp.where(sel, -jnp.inf, logits)
            vals.append(m)
            idxs.append(idx)
        es = [jnp.exp(v - vals[0]) for v in vals]
        den = es[0] + es[1] + es[2] + es[3]
        prefix = jnp.dot(onehot.astype(BF16), before, preferred_element_type=F32) + carry
        ranks = [jnp.sum(jnp.where(eid == idxs[k], prefix, 0.0), axis=0, keepdims=True) for k in range(TOP_K)]
        rows = [ix.astype(F32) for ix in idxs] + ranks
        rt_ref[:, rs] = jnp.concatenate(rows, axis=0)
        fields = jnp.concatenate(rows + [e / den for e in es] + [jnp.zeros((LANES - 3 * TOP_K, ts), F32)],
                                 axis=0)
        route_ref[rs, :] = jnp.transpose(fields)
        carry = carry + jnp.sum(onehot, axis=1, keepdims=True)
    carry_ref[...] = jnp.broadcast_to(carry, carry_ref.shape)
    cnt_ref[...] = jnp.broadcast_to(carry, cnt_ref.shape)


def _store_slabs(ref, val, rows):
    for j in range(val.shape[1] // LANES):
        ref[pl.ds(j, rows, stride=SUBLANES), :] = val[:, j * LANES:(j + 1) * LANES]


def _load_slabs(ref, rows, dtype):
    return jnp.concatenate([ref[pl.ds(j, rows, stride=SUBLANES), :].astype(dtype) for j in range(SUBLANES)],
                           axis=1)


def _outproj_router(x2, att, cv, wo_bf, g, rwt_bf, rbt, carry_in):
    n, d = x2.shape
    assert d == SUBLANES * LANES
    n_exp = rwt_bf.shape[0]
    tm = _tile(n, 512, 16)
    ts = _tile(tm, 512, 16)
    row = lambda w: pl.BlockSpec((tm, w), lambda i: (i, 0))
    full = lambda a: pl.BlockSpec(a.shape, lambda i: (0,) * a.ndim)
    return pl.pallas_call(
        functools.partial(_outproj_router_kernel, ts=ts),
        grid=(n // tm,),
        in_specs=[row(d), row(att.shape[1]), row(cv.shape[1]), full(wo_bf), full(g), full(rwt_bf), full(rbt),
                  full(carry_in)],
        out_specs=[row(d), pl.BlockSpec((tm * SUBLANES, LANES), lambda i: (i, 0)), row(LANES),
                   pl.BlockSpec((2 * TOP_K, tm), lambda i: (0, i)),
                   pl.BlockSpec((n_exp, LANES), lambda i: (0, 0))],
        out_shape=[jax.ShapeDtypeStruct((n, d), F32), jax.ShapeDtypeStruct((n * SUBLANES, LANES), F32),
                   jax.ShapeDtypeStruct((n, LANES), F32), jax.ShapeDtypeStruct((2 * TOP_K, n), F32),
                   jax.ShapeDtypeStruct((n_exp, LANES), F32)],
        scratch_shapes=[pltpu.VMEM((n_exp, LANES), F32)],
        compiler_params=_cparams(("arbitrary",)),
        name="outproj_router",
    )(x2, att, cv, wo_bf, g, rwt_bf, rbt, carry_in)


def _dispatch_kernel(fill_ref, nfill_ref, dest_ref, hn_ref, hn2_ref, hna_hbm, hnb_hbm, xs_ref, zero_ref, sem, zsem,
                     *, td, tme, n_first):
    @pl.when(pl.program_id(0) == 0)
    def _():
        zero_ref[...] = jnp.zeros(zero_ref.shape, F32)

        def fill(f):
            row = pl.multiple_of(fill_ref[f] * (tme * SUBLANES), tme * SUBLANES)
            return pltpu.make_async_copy(zero_ref, xs_ref.at[pl.ds(row, tme * SUBLANES)], zsem)

        def start(f, carry):
            fill(f).start()
            return carry

        def wait(f, carry):
            fill(f).wait()
            return carry

        lax.fori_loop(0, nfill_ref[0], start, 0)
        lax.fori_loop(0, nfill_ref[0], wait, 0)

    def scatter_rows(src_ref, src_hbm, tile):
        def issue(r, carry):
            src = src_ref.at[pl.ds(pl.multiple_of(r * SUBLANES, SUBLANES), SUBLANES)]
            hsrc = src_hbm.at[pl.ds(pl.multiple_of((tile * td + r) * SUBLANES, SUBLANES), SUBLANES)]
            for k in range(TOP_K):
                d = pl.multiple_of(dest_ref[0, 0, k * td + r] * SUBLANES, SUBLANES)
                pltpu.make_async_copy(src if k < TOP_K // 2 else hsrc, xs_ref.at[pl.ds(d, SUBLANES)],
                                      sem).start(priority=k % 2)
            return carry

        lax.fori_loop(0, td, issue, 0, unroll=8)
        for _ in range(TOP_K):
            pltpu.make_async_copy(src_ref, xs_ref.at[pl.ds(0, td * SUBLANES)], sem).wait()

    i = pl.program_id(0)

    @pl.when(i < n_first)
    def _():
        scatter_rows(hn_ref, hna_hbm, i)

    @pl.when(i >= n_first)
    def _():
        scatter_rows(hn2_ref, hnb_hbm, i - n_first)


def _dest_blocks(dest, td):
    n = dest.shape[1]
    return dest.reshape(TOP_K, n // td, td).transpose(1, 0, 2).reshape(n // td, 1, TOP_K * td)


def _dispatch(hn_a, hn_b, dest, fill_blocks, n_fill, *, nb, tme):
    na = hn_a.shape[0] // SUBLANES
    nb_rows = hn_b.shape[0] // SUBLANES
    td = _tile(nb_rows, 256, 8)
    assert na % td == 0
    n_first = na // td
    steps = n_first + nb_rows // td
    slab = lambda m: pl.BlockSpec((td * SUBLANES, LANES), m)
    grid_spec = pltpu.PrefetchScalarGridSpec(
        num_scalar_prefetch=2,
        grid=(steps,),
        in_specs=[pl.BlockSpec((1, 1, td * TOP_K), lambda i, fb, nf: (i, 0, 0), memory_space=pltpu.SMEM),
                  slab(lambda i, fb, nf: (jnp.minimum(i, n_first - 1), 0)),
                  slab(lambda i, fb, nf: (jnp.maximum(i - n_first, 0), 0)),
                  pl.BlockSpec(memory_space=pl.ANY), pl.BlockSpec(memory_space=pl.ANY)],
        out_specs=pl.BlockSpec(memory_space=pl.ANY),
        scratch_shapes=[pltpu.VMEM((tme * SUBLANES, LANES), F32), pltpu.SemaphoreType.DMA(()),
                        pltpu.SemaphoreType.DMA(())],
    )
    return pl.pallas_call(
        functools.partial(_dispatch_kernel, td=td, tme=tme, n_first=n_first),
        grid_spec=grid_spec,
        out_shape=jax.ShapeDtypeStruct((nb * tme * SUBLANES, LANES), F32),
        compiler_params=_cparams(("arbitrary",)),
        name="dispatch",
    )(fill_blocks, n_fill, _dest_blocks(dest, td), hn_a, hn_b, hn_a, hn_b)


def _experts_kernel(be_ref, bsrc_ref, nv_ref, nx_ref, nu_ref, x_ref, w1_hbm, b1_ref, w2_hbm, b2_ref,
                    y_ref, w1f_ref, w2f_ref, w1b_ref, w2b_ref, par_ref, wsem, *, tme):
    i = pl.program_id(0)
    e = be_ref[i]
    e_prev = be_ref[jnp.maximum(i - 1, 0)]
    d_ff = w2f_ref.shape[1]
    half = tme // 2

    def fetch(expert, s):
        return (pltpu.make_async_copy(w1_hbm.at[expert], w1f_ref.at[s], wsem.at[0, s]),
                pltpu.make_async_copy(w2_hbm.at[expert], w2f_ref.at[s], wsem.at[1, s]))

    @pl.when(i == 0)
    def _():
        par_ref[0] = 0
        for c in fetch(e, 0):
            c.start()

    @pl.when((i == 0) | (e != e_prev))
    def _():
        s = par_ref[0]
        for c in fetch(e, s):
            c.wait()
        w1b_ref[...] = w1f_ref[s].astype(BF16)
        w2b_ref[...] = w2f_ref[s].astype(BF16)
        nxt = nx_ref[i]

        @pl.when(nxt >= 0)
        def _():
            for c in fetch(nxt, 1 - s):
                c.start()

        par_ref[0] = 1 - s

    def ffn(x):
        h = jnp.dot(x, w1b_ref[...], preferred_element_type=F32) + b1_ref[0]
        x_glu = jnp.minimum(h[:, :d_ff], SWIGLU_LIMIT)
        x_lin = jnp.clip(h[:, d_ff:], -SWIGLU_LIMIT, SWIGLU_LIMIT)
        act = x_glu * _sigmoid(SWIGLU_ALPHA * x_glu) * (x_lin + 1.0)
        return jnp.dot(act.astype(BF16), w2b_ref[...], preferred_element_type=F32) + b2_ref[0]

    used = i < nu_ref[0]
    nv = nv_ref[i]

    @pl.when(used & (nv > half))
    def _():
        _store_slabs(y_ref, ffn(_load_slabs(x_ref, tme, BF16)), tme)

    @pl.when(used & (nv <= half))
    def _():
        rows = half * SUBLANES
        _store_slabs(y_ref.at[pl.ds(0, rows)], ffn(_load_slabs(x_ref.at[pl.ds(0, rows)], half, BF16)), half)
        y_ref[pl.ds(rows, rows), :] = jnp.zeros((rows, LANES), F32)

    @pl.when(jnp.logical_not(used))
    def _():
        y_ref[...] = jnp.zeros(y_ref.shape, F32)


def _experts(xs, w1, b1, w2, b2, blk_exp, blk_src, blk_nvalid, blk_next, n_used, *, tme):
    n_exp, d, h2 = w1.shape
    d_ff = w2.shape[1]
    nb = xs.shape[0] // (tme * SUBLANES)
    slab = lambda m: pl.BlockSpec((tme * SUBLANES, LANES), m)
    grid_spec = pltpu.PrefetchScalarGridSpec(
        num_scalar_prefetch=5,
        grid=(nb,),
        in_specs=[slab(lambda i, be, bs, nv, nx, nu: (bs[i], 0)),
                  pl.BlockSpec(memory_space=pl.ANY),
                  pl.BlockSpec((1, 1, h2), lambda i, be, bs, nv, nx, nu: (be[i], 0, 0)),
                  pl.BlockSpec(memory_space=pl.ANY),
                  pl.BlockSpec((1, 1, d), lambda i, be, bs, nv, nx, nu: (be[i], 0, 0))],
        out_specs=slab(lambda i, be, bs, nv, nx, nu: (i, 0)),
        scratch_shapes=[pltpu.VMEM((2, d, h2), F32), pltpu.VMEM((2, d_ff, d), F32),
                        pltpu.VMEM((d, h2), BF16), pltpu.VMEM((d_ff, d), BF16),
                        pltpu.SMEM((1,), jnp.int32), pltpu.SemaphoreType.DMA((2, 2))],
    )
    return pl.pallas_call(
        functools.partial(_experts_kernel, tme=tme),
        grid_spec=grid_spec,
        out_shape=jax.ShapeDtypeStruct(xs.shape, F32),
        compiler_params=_cparams(("arbitrary",)),
        name="experts",
    )(blk_exp, blk_src, blk_nvalid, blk_next, n_used, xs, w1, b1.reshape(n_exp, 1, h2), w2,
      b2.reshape(n_exp, 1, d))


def _combine_kernel(dcur_ref, dnext_ref, x1_ref, route_ref, g_ref, ys_ref, o_ref, buf_ref, sem, *, tc):
    i = pl.program_id(0)
    slot = i % 2

    def gather(dref, s):
        def issue(r, carry):
            row = pl.multiple_of(r * SUBLANES, SUBLANES)
            for k in range(TOP_K):
                d = pl.multiple_of(dref[0, 0, k * tc + r] * SUBLANES, SUBLANES)
                pltpu.make_async_copy(ys_ref.at[pl.ds(d, SUBLANES)], buf_ref.at[s, k, pl.ds(row, SUBLANES)],
                                      sem.at[s]).start(priority=k % 2)
            return carry

        lax.fori_loop(0, tc, issue, 0, unroll=8)

    @pl.when(i == 0)
    def _():
        gather(dcur_ref, 0)

    @pl.when(i + 1 < pl.num_programs(0))
    def _():
        gather(dnext_ref, 1 - slot)

    for k in range(TOP_K):
        pltpu.make_async_copy(ys_ref.at[pl.ds(0, tc * SUBLANES)], buf_ref.at[slot, k], sem.at[slot]).wait()
    route = route_ref[...]
    x1 = x1_ref[...]
    chunks = []
    for j in range(SUBLANES):
        acc = x1[:, j * LANES:(j + 1) * LANES]
        for k in range(TOP_K):
            acc = acc + (route[:, 2 * TOP_K + k:2 * TOP_K + k + 1]
                         * buf_ref[slot, k, pl.ds(j, tc, stride=SUBLANES), :])
        chunks.append(acc)
    y = jnp.concatenate(chunks, axis=1)
    ms = jnp.mean(y * y, axis=-1, keepdims=True)
    o_ref[...] = y * lax.rsqrt(ms + RMS_EPS) * g_ref[...]


def _combine(x1, route, dest, ys, g):
    n, d = x1.shape
    tc = _tile(n, 128, 8)
    steps = n // tc
    dest3 = _dest_blocks(dest, tc)
    dspec = lambda m: pl.BlockSpec((1, 1, tc * TOP_K), m, memory_space=pltpu.SMEM)
    return pl.pallas_call(
        functools.partial(_combine_kernel, tc=tc),
        grid=(steps,),
        in_specs=[dspec(lambda i: (i, 0, 0)), dspec(lambda i: (jnp.minimum(i + 1, steps - 1), 0, 0)),
                  pl.BlockSpec((tc, d), lambda i: (i, 0)),
                  pl.BlockSpec((tc, LANES), lambda i: (i, 0)),
                  pl.BlockSpec((1, d), lambda i: (0, 0)),
                  pl.BlockSpec(memory_space=pl.ANY)],
        out_specs=pl.BlockSpec((tc, d), lambda i: (i, 0)),
        out_shape=jax.ShapeDtypeStruct((n, d), F32),
        scratch_shapes=[pltpu.VMEM((2, TOP_K, tc * SUBLANES, LANES), F32), pltpu.SemaphoreType.DMA((2,))],
        compiler_params=_cparams(("arbitrary",)),
        name="combine",
    )(dest3, dest3, x1, route, g, ys)


def _dest_rows(route_t, blk_start, tme):
    eidx = route_t[0:TOP_K].astype(jnp.int32)
    rank = route_t[TOP_K:2 * TOP_K].astype(jnp.int32)
    experts = jnp.arange(N_EXPERTS, dtype=jnp.int32)[:, None, None]
    first = jnp.sum(jnp.where(eidx[None] == experts, blk_start[:, None, None], 0), axis=0)
    return (first * tme + rank).astype(jnp.int32)


def _routing_tables(counts_f, *, tme, nb):
    counts = counts_f[:, 0].astype(jnp.int32)
    nblk = (counts + tme - 1) // tme
    blk_end = jnp.cumsum(nblk)
    blk_start = blk_end - nblk
    n_used = blk_end[-1]
    b = jnp.arange(nb, dtype=jnp.int32)
    used = b < n_used
    blk_exp = jnp.minimum(jnp.sum((b[:, None] >= blk_end[None, :]).astype(jnp.int32), axis=1), N_EXPERTS - 1)
    last_exp = jnp.max(jnp.where(nblk > 0, jnp.arange(N_EXPERTS, dtype=jnp.int32), 0))
    blk_exp = jnp.where(used, blk_exp, last_exp).astype(jnp.int32)
    blk_src = jnp.where(used, b, 0).astype(jnp.int32)
    experts = jnp.arange(N_EXPERTS, dtype=jnp.int32)
    mine = (b[:, None] >= blk_start[None, :]) & (b[:, None] < blk_end[None, :])
    nvalid = jnp.sum(jnp.where(mine, counts[None, :] - (b[:, None] - blk_start[None, :]) * tme, 0), axis=1)
    nvalid = jnp.clip(nvalid, 0, tme).astype(jnp.int32)
    later = (experts[None, :] > experts[:, None]) & (nblk[None, :] > 0)
    nxt_e = jnp.min(jnp.where(later, experts[None, :], N_EXPERTS), axis=1)
    nxt_e = jnp.where(nxt_e == N_EXPERTS, -1, nxt_e)
    blk_next = jnp.sum(jnp.where(blk_exp[:, None] == experts[None, :], nxt_e[None, :], 0), axis=1).astype(jnp.int32)
    has_tail = (nblk > 0) & (counts % tme != 0)
    partial = jnp.any((b[:, None] == blk_end[None, :] - 1) & has_tail[None, :], axis=1)
    needs_fill = partial | ~used
    fill_blocks = jnp.argsort(~needs_fill, stable=True).astype(jnp.int32)
    n_fill = jnp.sum(needs_fill).reshape(1).astype(jnp.int32)
    return (blk_start, blk_exp, blk_src, nvalid, blk_next, n_used.reshape(1).astype(jnp.int32), fill_blocks,
            n_fill)


def kernel(x_prompt, x_sample, cache_k, cache_v, state_conv, attn_norm_g, w_in, attn_sinks, conv_w, conv_b,
           conv_ln_g, conv_ln_b, w_out, ffn_norm_g, router_w, router_b, w1, b1, w2, b2, final_norm_g):
    depth = w_in.shape[0]
    assert depth == 1, "single-layer step"
    bp, sp, d = x_prompt.shape
    bs, ss, _ = x_sample.shape
    cw = conv_w.shape[2]
    np_, ns = bp * sp, bs * ss
    n_tok = np_ + ns
    assert sp % WINDOW == 0

    xp2 = x_prompt.reshape(np_, d)
    xs2 = x_sample.reshape(ns, d)
    w_in_bf = w_in[0].astype(BF16)
    w_out_bf = w_out[0].astype(BF16)
    g_attn = attn_norm_g[0].reshape(1, d)
    g_ffn = ffn_norm_g[0].reshape(1, d)
    sinks = attn_sinks[0]
    vec = lambda a: a.reshape(1, cw)

    tab_p = _rope_tables(jnp.arange(sp, dtype=jnp.int32))
    tms = _tile(ns, 512, max(ss, 16))
    tab_s = _rope_tables(PAST_LEN + (jnp.arange(tms, dtype=jnp.int32) % ss))
    qp, kp, vp, kfp, vfp, ap = _in_proj(xp2, g_attn, w_in_bf, tab_p, seq_period=sp, q_dtype=BF16, conv_width=cw)
    qs, _, _, kfs, vfs, as_ = _in_proj(xs2, g_attn, w_in_bf, tab_s, seq_period=None, q_dtype=F32, conv_width=cw)

    att_p = _attn_prompt(qp, kp, vp, sinks, batch=bp, seq=sp)
    ck = cache_k[0].reshape(bs, WINDOW, KV_WIDTH)
    cv_ = cache_v[0].reshape(bs, WINDOW, KV_WIDTH)
    att_s, nk_s, nv_s = _attn_sample(qs, kfs, vfs, ck, cv_, sinks, batch=bs, ts=ss)

    cv_p = _conv_prompt(ap, conv_w[0], vec(conv_b[0]), vec(conv_ln_g[0]), vec(conv_ln_b[0]), batch=bp, seq=sp)
    cv_s = _conv_sample(as_, state_conv[0], conv_w[0], vec(conv_b[0]), vec(conv_ln_g[0]), vec(conv_ln_b[0]),
                        batch=bs, ts=ss)

    n_exp = router_w.shape[2]
    assert n_exp == N_EXPERTS
    rwt_bf = router_w[0].T.astype(BF16)
    rbt = jnp.broadcast_to(router_b[0][:, None], (n_exp, LANES))
    zero_carry = jnp.zeros((n_exp, LANES), F32)
    x1p, hnp, route_p, rt_p, cnt_p = _outproj_router(xp2, att_p, cv_p, w_out_bf, g_ffn, rwt_bf, rbt, zero_carry)
    x1s, hns, route_s, rt_s, cnt = _outproj_router(xs2, att_s, cv_s, w_out_bf, g_ffn, rwt_bf, rbt, cnt_p)

    tme = EXPERT_BLOCK_ROWS
    nb = -(-(n_tok * TOP_K + N_EXPERTS * (tme - 1)) // tme)
    blk_start, blk_exp, blk_src, blk_nvalid, blk_next, n_used, fill_blocks, n_fill = _routing_tables(
        cnt, tme=tme, nb=nb)
    dest_p = _dest_rows(rt_p, blk_start, tme)
    dest_s = _dest_rows(rt_s, blk_start, tme)
    xs_sorted = _dispatch(hnp, hns, jnp.concatenate([dest_p, dest_s], axis=1), fill_blocks, n_fill, nb=nb, tme=tme)
    ys = _experts(xs_sorted, w1[0], b1[0], w2[0], b2[0], blk_exp, blk_src, blk_nvalid, blk_next, n_used, tme=tme)
    g_fin = final_norm_g.reshape(1, d)
    y_p = _combine(x1p, route_p, dest_p, ys, g_fin)
    y_s = _combine(x1s, route_s, dest_s, ys, g_fin)

    kv5 = lambda t, bb: t.reshape(bb, -1, KV_WIDTH)[:, -WINDOW:].reshape(bb, WINDOW, N_KV_HEADS, HEAD_DIM)
    new_k_p = kv5(kfp, bp)[None]
    new_v_p = kv5(vfp, bp)[None]
    ctx = CONV_K - 1
    new_c_p = ap.reshape(bp, sp, cw)[:, -ctx:][None]
    new_c_s = jnp.concatenate([state_conv[0], as_.reshape(bs, ss, cw)], axis=1)[:, -ctx:][None]
    return (y_p.reshape(bp, sp, d), y_s.reshape(bs, ss, d), new_k_p, new_v_p, new_c_p,
            kv5(nk_s, bs)[None], kv5(nv_s, bs)[None], new_c_s)
```

```python
import functools

import jax
import jax.numpy as jnp
from jax import lax
from jax.experimental import pallas as pl
from jax.experimental.pallas import tpu as pltpu

F32 = jnp.float32
BF16 = jnp.bfloat16

HEAD_DIM = 64
N_Q_HEADS = 8
N_KV_HEADS = 2
WINDOW = 128
ROPE_THETA = 500000.0
ROPE_DIM = 16
CONV_K = 31
N_EXPERTS = 32
TOP_K = 4
SWIGLU_LIMIT = 7.0
SWIGLU_ALPHA = 1.702
RMS_EPS = 1e-5
LN_EPS = 1e-5
PAST_LEN = 16384

LANES = 128
SUBLANES = 8
CONV_HALO = 32
VMEM_LIMIT = 56 * 1024 * 1024
EXPERT_BLOCK_ROWS = 512

ATTN_WIDTH = N_Q_HEADS * HEAD_DIM
KV_WIDTH = N_KV_HEADS * HEAD_DIM


def _tile(n, pref, mult=8):
    t = min(pref, n)
    while t > 0 and (n % t or t % mult):
        t -= 1
    assert t > 0, (n, pref, mult)
    return t


def _cparams(sem):
    return pltpu.CompilerParams(dimension_semantics=sem, vmem_limit_bytes=VMEM_LIMIT)


def _sigmoid(x):
    return 1.0 / (1.0 + jnp.exp(-x))


def _rope_tables(pos):
    half = ROPE_DIM // 2
    inv_freq = jnp.power(jnp.float32(ROPE_THETA), -jnp.arange(half, dtype=F32) * 2.0 / ROPE_DIM)
    ang = pos.astype(F32)[:, None] * inv_freq[None, :]
    cos, sin = jnp.cos(ang), jnp.sin(ang)
    l64 = jnp.arange(LANES) % HEAD_DIM
    f = l64 % half
    cos_l, sin_l = cos[:, f], sin[:, f]
    c = jnp.where(l64 < ROPE_DIM, cos_l, 1.0)
    s1 = jnp.where(l64 < half, -sin_l, 0.0)
    s2 = jnp.where((l64 >= half) & (l64 < ROPE_DIM), sin_l, 0.0)
    return c.astype(F32), s1.astype(F32), s2.astype(F32)


def _inproj_kernel(x_ref, g_ref, w_ref, c_ref, s1_ref, s2_ref,
                   q_ref, k_ref, v_ref, kf_ref, vf_ref, a_ref, *, conv_width):
    x = x_ref[...]
    ms = jnp.mean(x * x, axis=-1, keepdims=True)
    h = (x * lax.rsqrt(ms + RMS_EPS) * g_ref[...]).astype(BF16)
    z = jnp.dot(h, w_ref[...], preferred_element_type=F32)
    c, s1, s2 = c_ref[...], s1_ref[...], s2_ref[...]
    half = ROPE_DIM // 2

    def rot(t):
        return t * c + pltpu.roll(t, LANES - half, 1) * s1 + pltpu.roll(t, half, 1) * s2

    scale = HEAD_DIM ** -0.5
    for j in range(ATTN_WIDTH // LANES):
        q_ref[:, j * LANES:(j + 1) * LANES] = (rot(z[:, j * LANES:(j + 1) * LANES]) * scale).astype(q_ref.dtype)
    k0 = ATTN_WIDTH
    kr = rot(z[:, k0:k0 + KV_WIDTH])
    k_ref[...] = kr.astype(BF16)
    kf_ref[...] = kr
    v0 = k0 + KV_WIDTH
    vv = z[:, v0:v0 + KV_WIDTH]
    v_ref[...] = vv.astype(BF16)
    vf_ref[...] = vv
    u0 = v0 + KV_WIDTH
    g0 = u0 + conv_width
    a_ref[...] = z[:, u0:g0] * _sigmoid(z[:, g0:g0 + conv_width])


def _in_proj(x2, g, w_bf, tables, *, seq_period, q_dtype, conv_width):
    n, d = x2.shape
    in_w = w_bf.shape[1]
    if seq_period is None:
        tm = tables[0].shape[0]
        tmap = lambda i: (0, 0)
    else:
        tm = _tile(seq_period, 512, 16)
        per = seq_period // tm
        tmap = lambda i: (i % per, 0)
    assert n % tm == 0
    row = lambda w: pl.BlockSpec((tm, w), lambda i: (i, 0))
    tab = pl.BlockSpec((tm, LANES), tmap)
    return pl.pallas_call(
        functools.partial(_inproj_kernel, conv_width=conv_width),
        grid=(n // tm,),
        in_specs=[row(d), pl.BlockSpec((1, d), lambda i: (0, 0)),
                  pl.BlockSpec((d, in_w), lambda i: (0, 0)), tab, tab, tab],
        out_specs=[row(ATTN_WIDTH), row(KV_WIDTH), row(KV_WIDTH), row(KV_WIDTH), row(KV_WIDTH),
                   row(conv_width)],
        out_shape=[jax.ShapeDtypeStruct((n, ATTN_WIDTH), q_dtype),
                   jax.ShapeDtypeStruct((n, KV_WIDTH), BF16),
                   jax.ShapeDtypeStruct((n, KV_WIDTH), BF16),
                   jax.ShapeDtypeStruct((n, KV_WIDTH), F32),
                   jax.ShapeDtypeStruct((n, KV_WIDTH), F32),
                   jax.ShapeDtypeStruct((n, conv_width), F32)],
        compiler_params=_cparams(("parallel",)),
        name="in_proj",
    )(x2, g, w_bf, *tables)


def _dup_head(t, h):
    sw = pltpu.roll(t, HEAD_DIM, 1)
    low = lax.broadcasted_iota(jnp.int32, t.shape, 1) < HEAD_DIM
    return jnp.where(low, t, sw) if h == 0 else jnp.where(low, sw, t)


def _nt_dot(a, b):
    return lax.dot_general(a, b, (((1,), (1,)), ((), ())), preferred_element_type=F32)


def _attn_prompt_kernel(sink_ref, q_ref, kc_ref, kp_ref, vc_ref, vp_ref, o_ref, *, qb):
    j = pl.program_id(1)
    w = WINDOW
    k_all = jnp.concatenate([kp_ref[...], kc_ref[...]], axis=0).astype(F32)
    v_all = jnp.concatenate([vp_ref[...], vc_ref[...]], axis=0).astype(F32)
    r = lax.broadcasted_iota(jnp.int32, (w, 2 * w), 0)
    kk = lax.broadcasted_iota(jnp.int32, (w, 2 * w), 1)
    band = (kk > r) & (kk <= r + w)
    low = lax.broadcasted_iota(jnp.int32, (w, LANES), 1) < HEAD_DIM
    zero = jnp.zeros((w, LANES), BF16)
    group = N_Q_HEADS // N_KV_HEADS
    for h in range(N_KV_HEADS):
        kd_all = _dup_head(k_all, h).astype(BF16)
        vd_all = _dup_head(v_all, h).astype(BF16)
        for sub in range(qb):
            valid = band & ((kk >= w) | (j > 0)) if sub == 0 else band
            kd = kd_all[sub * w:(sub + 2) * w, :]
            vd = vd_all[sub * w:(sub + 2) * w, :]
            for jj in range(group // 2):
                col = (h * group // 2 + jj) * LANES
                qv = q_ref[sub * w:(sub + 1) * w, col:col + LANES]
                halves = []
                for half in range(2):
                    head = h * group + jj * 2 + half
                    qm = jnp.where(low if half == 0 else ~low, qv, zero)
                    s = jnp.where(valid, _nt_dot(qm, kd), -jnp.inf)
                    sink = sink_ref[head]
                    m = jnp.maximum(jnp.max(s, axis=-1, keepdims=True), sink)
                    p = jnp.exp(s - m)
                    den = jnp.sum(p, axis=-1, keepdims=True) + jnp.exp(sink - m)
                    o = jnp.dot(p.astype(BF16), vd, preferred_element_type=F32)
                    halves.append(o / den)
                o_ref[sub * w:(sub + 1) * w, col:col + LANES] = (
                    jnp.where(low, halves[0], halves[1]).astype(o_ref.dtype))


def _attn_prompt(q, k, v, sinks, *, batch, seq):
    nb = seq // WINDOW
    qb = 2 if nb % 2 == 0 else 1
    steps = nb // qb
    cur = lambda b, j: (b * steps + j, 0)
    prev = lambda b, j: (b * nb + jnp.maximum(j * qb - 1, 0), 0)
    return pl.pallas_call(
        functools.partial(_attn_prompt_kernel, qb=qb),
        grid=(batch, steps),
        in_specs=[pl.BlockSpec(memory_space=pltpu.SMEM),
                  pl.BlockSpec((qb * WINDOW, ATTN_WIDTH), cur),
                  pl.BlockSpec((qb * WINDOW, KV_WIDTH), cur), pl.BlockSpec((WINDOW, KV_WIDTH), prev),
                  pl.BlockSpec((qb * WINDOW, KV_WIDTH), cur), pl.BlockSpec((WINDOW, KV_WIDTH), prev)],
        out_specs=pl.BlockSpec((qb * WINDOW, ATTN_WIDTH), cur),
        out_shape=jax.ShapeDtypeStruct((batch * seq, ATTN_WIDTH), BF16),
        compiler_params=_cparams(("parallel", "parallel")),
        name="attn_prompt",
    )(sinks, q, k, k, v, v)


def _attn_sample_kernel(sink_ref, q_ref, kn_ref, vn_ref, ck_ref, cv_ref, o_ref, nk_ref, nv_ref, *, gb, ts):
    w = WINDOW
    group = N_Q_HEADS // N_KV_HEADS
    rows = group * ts
    low = lax.broadcasted_iota(jnp.int32, (ts, LANES), 1) < HEAD_DIM
    pad = jnp.zeros((ts, LANES), F32)
    s_c, s_n, v_dup = [], [], []
    for b in range(gb):
        kc, vc = ck_ref[b], cv_ref[b]
        kn, vn = kn_ref[b * ts:(b + 1) * ts, :], vn_ref[b * ts:(b + 1) * ts, :]
        nk_ref[b, 0:w - ts, :] = kc[ts:, :]
        nk_ref[b, w - ts:, :] = kn
        nv_ref[b, 0:w - ts, :] = vc[ts:, :]
        nv_ref[b, w - ts:, :] = vn
        knp = jnp.concatenate([kn, pad], axis=0)
        vnp = jnp.concatenate([vn, pad], axis=0)
        qb = q_ref[b * ts:(b + 1) * ts, :]
        for h in range(N_KV_HEADS):
            parts = []
            for jj in range(group // 2):
                col = (h * group // 2 + jj) * LANES
                qv = qb[:, col:col + LANES]
                parts += [jnp.where(low, qv, 0.0), jnp.where(low, 0.0, qv)]
            lhs = jnp.concatenate(parts, axis=0).astype(BF16)
            s_c.append(_nt_dot(lhs, _dup_head(kc, h).astype(BF16)))
            s_n.append(_nt_dot(lhs, _dup_head(knp, h).astype(BF16)))
            v_dup.append((_dup_head(vc, h).astype(BF16), _dup_head(vnp, h).astype(BF16)))
    s_c = jnp.concatenate(s_c, axis=0)
    s_n = jnp.concatenate(s_n, axis=0)
    n_rows = s_c.shape[0]
    ridx = lax.broadcasted_iota(jnp.int32, (n_rows, 1), 0)
    t_row = ridx % ts
    head_row = (ridx // ts) % N_Q_HEADS
    sink = jnp.zeros((n_rows, 1), F32)
    for hd in range(N_Q_HEADS):
        sink = jnp.where(head_row == hd, sink_ref[hd], sink)
    c_idx = lax.broadcasted_iota(jnp.int32, (n_rows, w), 1)
    n_idx = lax.broadcasted_iota(jnp.int32, (n_rows, 2 * ts), 1)
    s_c = jnp.where(c_idx > t_row, s_c, -jnp.inf)
    s_n = jnp.where(n_idx <= t_row, s_n, -jnp.inf)
    m = jnp.maximum(jnp.maximum(jnp.max(s_c, axis=-1, keepdims=True), jnp.max(s_n, axis=-1, keepdims=True)), sink)
    p_c = jnp.exp(s_c - m)
    p_n = jnp.exp(s_n - m)
    den = jnp.sum(p_c, axis=-1, keepdims=True) + jnp.sum(p_n, axis=-1, keepdims=True) + jnp.exp(sink - m)
    p_c = p_c.astype(BF16)
    p_n = p_n.astype(BF16)
    outs = []
    for b in range(gb):
        cols = []
        for h in range(N_KV_HEADS):
            ci = b * N_KV_HEADS + h
            sl = slice(ci * rows, (ci + 1) * rows)
            vdc, vdn = v_dup[ci]
            o = (jnp.dot(p_c[sl], vdc, preferred_element_type=F32)
                 + jnp.dot(p_n[sl], vdn, preferred_element_type=F32)) / den[sl]
            for jj in range(group // 2):
                lo_part = o[(2 * jj) * ts:(2 * jj + 1) * ts, :]
                hi_part = o[(2 * jj + 1) * ts:(2 * jj + 2) * ts, :]
                cols.append(jnp.where(low, lo_part, hi_part))
        outs.append(jnp.concatenate(cols, axis=1))
    o_ref[...] = jnp.concatenate(outs, axis=0).astype(o_ref.dtype)


def _attn_sample(q, kf, vf, cache_k, cache_v, sinks, *, batch, ts):
    assert ts % 8 == 0 and ts <= WINDOW
    gb = _tile(batch, 8, 2)
    tok = lambda w: pl.BlockSpec((gb * ts, w), lambda i: (i, 0))
    cache = pl.BlockSpec((gb, WINDOW, KV_WIDTH), lambda i: (i, 0, 0))
    cshape = jax.ShapeDtypeStruct((batch, WINDOW, KV_WIDTH), F32)
    return pl.pallas_call(
        functools.partial(_attn_sample_kernel, gb=gb, ts=ts),
        grid=(batch // gb,),
        in_specs=[pl.BlockSpec(memory_space=pltpu.SMEM), tok(ATTN_WIDTH), tok(KV_WIDTH), tok(KV_WIDTH),
                  cache, cache],
        out_specs=[tok(ATTN_WIDTH), cache, cache],
        out_shape=[jax.ShapeDtypeStruct((batch * ts, ATTN_WIDTH), BF16), cshape, cshape],
        compiler_params=_cparams(("parallel",)),
        name="attn_sample",
    )(sinks, q, kf, vf, cache_k, cache_v)


def _ln_swish(acc, b, lg, lb):
    y = acc + b
    mu = jnp.mean(y, axis=-1, keepdims=True)
    yc = y - mu
    var = jnp.mean(yc * yc, axis=-1, keepdims=True)
    yn = yc * lax.rsqrt(var + LN_EPS) * lg + lb
    return yn * _sigmoid(yn)


def _conv_prompt_kernel(a_ref, ap_ref, w_ref, b_ref, lg_ref, lb_ref, o_ref, win_ref, *, tt, rc):
    j = pl.program_id(1)
    n = CONV_HALO + tt
    win = jnp.concatenate([jnp.where(j > 0, ap_ref[...], 0.0), a_ref[...]], axis=0)
    win_ref[0] = win
    for r in range(1, SUBLANES):
        win_ref[r] = pltpu.roll(win, n - r, 0)
    off = CONV_HALO - (CONV_K - 1)
    b, lg, lb = b_ref[...], lg_ref[...], lb_ref[...]
    for c in range(tt // rc):
        acc = jnp.zeros((rc, a_ref.shape[1]), F32)
        for k in range(CONV_K):
            s = off + k
            base = c * rc + (s // SUBLANES) * SUBLANES
            wk = jnp.concatenate([w_ref[k]] * (rc // SUBLANES), axis=0)
            acc = acc + wk * win_ref[s % SUBLANES, base:base + rc, :]
        o_ref[c * rc:(c + 1) * rc, :] = _ln_swish(acc, b, lg, lb).astype(o_ref.dtype)


def _conv_prompt(a, w, b, lg, lb, *, batch, seq):
    cw = a.shape[1]
    tt = _tile(seq, 256, CONV_HALO)
    rc = _tile(tt, 32, 16)
    nt = seq // tt
    per = tt // CONV_HALO
    cur = lambda bb, j: (bb * nt + j, 0)
    prev = lambda bb, j: (jnp.maximum((bb * nt + j) * per - 1, 0), 0)
    vec = pl.BlockSpec((1, cw), lambda bb, j: (0, 0))
    return pl.pallas_call(
        functools.partial(_conv_prompt_kernel, tt=tt, rc=rc),
        grid=(batch, nt),
        in_specs=[pl.BlockSpec((tt, cw), cur), pl.BlockSpec((CONV_HALO, cw), prev),
                  pl.BlockSpec((CONV_K, SUBLANES, cw), lambda bb, j: (0, 0, 0)), vec, vec, vec],
        out_specs=pl.BlockSpec((tt, cw), cur),
        out_shape=jax.ShapeDtypeStruct((batch * seq, cw), BF16),
        scratch_shapes=[pltpu.VMEM((SUBLANES, CONV_HALO + tt, cw), F32)],
        compiler_params=_cparams(("parallel", "parallel")),
        name="conv_prompt",
    )(a, a, jnp.broadcast_to(w[:, None, :], (CONV_K, SUBLANES, cw)), b, lg, lb)


def _conv_sample_kernel(a_ref, st_ref, w_ref, b_ref, lg_ref, lb_ref, o_ref, win_ref, *, gb, ts):
    ctx = CONV_K - 1
    b, lg, lb = b_ref[...], lg_ref[...], lb_ref[...]
    for bb in range(gb):
        win_ref[bb, 0:ctx, :] = st_ref[bb]
        win_ref[bb, ctx:ctx + ts, :] = a_ref[bb * ts:(bb + 1) * ts, :]
    outs = []
    for bb in range(gb):
        acc = jnp.zeros((ts, a_ref.shape[1]), F32)
        for k in range(CONV_K):
            wk = jnp.concatenate([w_ref[k]] * (ts // SUBLANES), axis=0)
            acc = acc + wk * win_ref[bb, k:k + ts, :]
        outs.append(_ln_swish(acc, b, lg, lb))
    o_ref[...] = jnp.concatenate(outs, axis=0).astype(o_ref.dtype)


def _conv_sample(a, state, w, b, lg, lb, *, batch, ts):
    cw = a.shape[1]
    ctx = CONV_K - 1
    gb = _tile(batch, 8, 2)
    vec = pl.BlockSpec((1, cw), lambda i: (0, 0))
    return pl.pallas_call(
        functools.partial(_conv_sample_kernel, gb=gb, ts=ts),
        grid=(batch // gb,),
        in_specs=[pl.BlockSpec((gb * ts, cw), lambda i: (i, 0)),
                  pl.BlockSpec((gb, ctx, cw), lambda i: (i, 0, 0)),
                  pl.BlockSpec((CONV_K, SUBLANES, cw), lambda i: (0, 0, 0)), vec, vec, vec],
        out_specs=pl.BlockSpec((gb * ts, cw), lambda i: (i, 0)),
        out_shape=jax.ShapeDtypeStruct((batch * ts, cw), BF16),
        scratch_shapes=[pltpu.VMEM((gb, ctx + ts + 2, cw), F32)],
        compiler_params=_cparams(("parallel",)),
        name="conv_sample",
    )(a, state, jnp.broadcast_to(w[:, None, :], (CONV_K, SUBLANES, cw)), b, lg, lb)


def _outproj_router_kernel(x_ref, att_ref, cv_ref, wo_ref, g_ref, rw_ref, rb_ref, cin_ref,
                           x1_ref, hn_ref, route_ref, rt_ref, cnt_ref, carry_ref, *, ts):
    i = pl.program_id(0)

    @pl.when(i == 0)
    def _():
        carry_ref[...] = cin_ref[...]

    aw = att_ref.shape[1]
    n_exp = rw_ref.shape[0]
    r = lax.broadcasted_iota(jnp.int32, (ts, ts), 0)
    c = lax.broadcasted_iota(jnp.int32, (ts, ts), 1)
    before = (r < c).astype(BF16)
    eid = lax.broadcasted_iota(jnp.int32, (n_exp, ts), 0)
    carry = carry_ref[...][:, 0:1]
    for s in range(x_ref.shape[0] // ts):
        rs = slice(s * ts, (s + 1) * ts)
        mix = (jnp.dot(att_ref[rs, :], wo_ref[0:aw, :], preferred_element_type=F32)
               + jnp.dot(cv_ref[rs, :], wo_ref[aw:, :], preferred_element_type=F32))
        x1 = x_ref[rs, :] + mix
        x1_ref[rs, :] = x1
        ms = jnp.mean(x1 * x1, axis=-1, keepdims=True)
        hn = x1 * lax.rsqrt(ms + RMS_EPS) * g_ref[...]
        _store_slabs(hn_ref.at[pl.ds(s * ts * SUBLANES, ts * SUBLANES)], hn, ts)
        logits = _nt_dot(rw_ref[...], hn.astype(BF16)) + rb_ref[...][:, 0:1]
        onehot = jnp.zeros((n_exp, ts), F32)
        vals, idxs = [], []
        for _ in range(TOP_K):
            m = jnp.max(logits, axis=0, keepdims=True)
            idx = jnp.min(jnp.where(logits == m, eid, n_exp), axis=0, keepdims=True)
            sel = eid == idx
            onehot = onehot + sel.astype(F32)
            logits = jnp.where(sel, -jnp.inf, logits)
            vals.append(m)
            idxs.append(idx)
        es = [jnp.exp(v - vals[0]) for v in vals]
        den = es[0] + es[1] + es[2] + es[3]
        prefix = jnp.dot(onehot.astype(BF16), before, preferred_element_type=F32) + carry
        ranks = [jnp.sum(jnp.where(eid == idxs[k], prefix, 0.0), axis=0, keepdims=True) for k in range(TOP_K)]
        rows = [ix.astype(F32) for ix in idxs] + ranks
        rt_ref[:, rs] = jnp.concatenate(rows, axis=0)
        fields = jnp.concatenate(rows + [e / den for e in es] + [jnp.zeros((LANES - 3 * TOP_K, ts), F32)],
                                 axis=0)
        route_ref[rs, :] = jnp.transpose(fields)
        carry = carry + jnp.sum(onehot, axis=1, keepdims=True)
    carry_ref[...] = jnp.broadcast_to(carry, carry_ref.shape)
    cnt_ref[...] = jnp.broadcast_to(carry, cnt_ref.shape)


def _store_slabs(ref, val, rows):
    for j in range(val.shape[1] // LANES):
        ref[pl.ds(j, rows, stride=SUBLANES), :] = val[:, j * LANES:(j + 1) * LANES]


def _load_slabs(ref, rows, dtype):
    return jnp.concatenate([ref[pl.ds(j, rows, stride=SUBLANES), :].astype(dtype) for j in range(SUBLANES)],
                           axis=1)


def _outproj_router(x2, att, cv, wo_bf, g, rwt_bf, rbt, carry_in):
    n, d = x2.shape
    assert d == SUBLANES * LANES
    n_exp = rwt_bf.shape[0]
    tm = _tile(n, 512, 16)
    ts = _tile(tm, 512, 16)
    row = lambda w: pl.BlockSpec((tm, w), lambda i: (i, 0))
    full = lambda a: pl.BlockSpec(a.shape, lambda i: (0,) * a.ndim)
    return pl.pallas_call(
        functools.partial(_outproj_router_kernel, ts=ts),
        grid=(n // tm,),
        in_specs=[row(d), row(att.shape[1]), row(cv.shape[1]), full(wo_bf), full(g), full(rwt_bf), full(rbt),
                  full(carry_in)],
        out_specs=[row(d), pl.BlockSpec((tm * SUBLANES, LANES), lambda i: (i, 0)), row(LANES),
                   pl.BlockSpec((2 * TOP_K, tm), lambda i: (0, i)),
                   pl.BlockSpec((n_exp, LANES), lambda i: (0, 0))],
        out_shape=[jax.ShapeDtypeStruct((n, d), F32), jax.ShapeDtypeStruct((n * SUBLANES, LANES), F32),
                   jax.ShapeDtypeStruct((n, LANES), F32), jax.ShapeDtypeStruct((2 * TOP_K, n), F32),
                   jax.ShapeDtypeStruct((n_exp, LANES), F32)],
        scratch_shapes=[pltpu.VMEM((n_exp, LANES), F32)],
        compiler_params=_cparams(("arbitrary",)),
        name="outproj_router",
    )(x2, att, cv, wo_bf, g, rwt_bf, rbt, carry_in)


def _dispatch_kernel(fill_ref, nfill_ref, dest_ref, hn_ref, hn2_ref, xs_ref, zero_ref, sem, zsem,
                     *, td, tme, n_first):
    @pl.when(pl.program_id(0) == 0)
    def _():
        zero_ref[...] = jnp.zeros(zero_ref.shape, F32)

        def fill(f):
            row = pl.multiple_of(fill_ref[f] * (tme * SUBLANES), tme * SUBLANES)
            return pltpu.make_async_copy(zero_ref, xs_ref.at[pl.ds(row, tme * SUBLANES)], zsem)

        def start(f, carry):
            fill(f).start()
            return carry

        def wait(f, carry):
            fill(f).wait()
            return carry

        lax.fori_loop(0, nfill_ref[0], start, 0)
        lax.fori_loop(0, nfill_ref[0], wait, 0)

    def scatter_rows(src_ref):
        def issue(r, carry):
            src = src_ref.at[pl.ds(pl.multiple_of(r * SUBLANES, SUBLANES), SUBLANES)]
            for k in range(TOP_K):
                d = pl.multiple_of(dest_ref[0, 0, k * td + r] * SUBLANES, SUBLANES)
                pltpu.make_async_copy(src, xs_ref.at[pl.ds(d, SUBLANES)], sem).start(priority=k % 2)
            return carry

        lax.fori_loop(0, td, issue, 0, unroll=8)
        for _ in range(TOP_K):
            pltpu.make_async_copy(src_ref, xs_ref.at[pl.ds(0, td * SUBLANES)], sem).wait()

    @pl.when(pl.program_id(0) < n_first)
    def _():
        scatter_rows(hn_ref)

    @pl.when(pl.program_id(0) >= n_first)
    def _():
        scatter_rows(hn2_ref)


def _dest_blocks(dest, td):
    n = dest.shape[1]
    return dest.reshape(TOP_K, n // td, td).transpose(1, 0, 2).reshape(n // td, 1, TOP_K * td)


def _dispatch(hn_a, hn_b, dest, fill_blocks, n_fill, *, nb, tme):
    na = hn_a.shape[0] // SUBLANES
    nb_rows = hn_b.shape[0] // SUBLANES
    td = _tile(nb_rows, 256, 8)
    assert na % td == 0
    n_first = na // td
    steps = n_first + nb_rows // td
    slab = lambda m: pl.BlockSpec((td * SUBLANES, LANES), m)
    grid_spec = pltpu.PrefetchScalarGridSpec(
        num_scalar_prefetch=2,
        grid=(steps,),
        in_specs=[pl.BlockSpec((1, 1, td * TOP_K), lambda i, fb, nf: (i, 0, 0), memory_space=pltpu.SMEM),
                  slab(lambda i, fb, nf: (jnp.minimum(i, n_first - 1), 0)),
                  slab(lambda i, fb, nf: (jnp.maximum(i - n_first, 0), 0))],
        out_specs=pl.BlockSpec(memory_space=pl.ANY),
        scratch_shapes=[pltpu.VMEM((tme * SUBLANES, LANES), F32), pltpu.SemaphoreType.DMA(()),
                        pltpu.SemaphoreType.DMA(())],
    )
    return pl.pallas_call(
        functools.partial(_dispatch_kernel, td=td, tme=tme, n_first=n_first),
        grid_spec=grid_spec,
        out_shape=jax.ShapeDtypeStruct((nb * tme * SUBLANES, LANES), F32),
        compiler_params=_cparams(("arbitrary",)),
        name="dispatch",
    )(fill_blocks, n_fill, _dest_blocks(dest, td), hn_a, hn_b)


def _experts_kernel(be_ref, bsrc_ref, nv_ref, nx_ref, nu_ref, x_ref, w1_hbm, b1_ref, w2_hbm, b2_ref,
                    y_ref, w1f_ref, w2f_ref, w1b_ref, w2b_ref, par_ref, wsem, *, tme):
    i = pl.program_id(0)
    e = be_ref[i]
    e_prev = be_ref[jnp.maximum(i - 1, 0)]
    d_ff = w2f_ref.shape[1]
    half = tme // 2

    def fetch(expert, s):
        return (pltpu.make_async_copy(w1_hbm.at[expert], w1f_ref.at[s], wsem.at[0, s]),
                pltpu.make_async_copy(w2_hbm.at[expert], w2f_ref.at[s], wsem.at[1, s]))

    @pl.when(i == 0)
    def _():
        par_ref[0] = 0
        for c in fetch(e, 0):
            c.start()

    @pl.when((i == 0) | (e != e_prev))
    def _():
        s = par_ref[0]
        for c in fetch(e, s):
            c.wait()
        w1b_ref[...] = w1f_ref[s].astype(BF16)
        w2b_ref[...] = w2f_ref[s].astype(BF16)
        nxt = nx_ref[i]

        @pl.when(nxt >= 0)
        def _():
            for c in fetch(nxt, 1 - s):
                c.start()

        par_ref[0] = 1 - s

    def ffn(x):
        h = jnp.dot(x, w1b_ref[...], preferred_element_type=F32) + b1_ref[0]
        x_glu = jnp.minimum(h[:, :d_ff], SWIGLU_LIMIT)
        x_lin = jnp.clip(h[:, d_ff:], -SWIGLU_LIMIT, SWIGLU_LIMIT)
        act = x_glu * _sigmoid(SWIGLU_ALPHA * x_glu) * (x_lin + 1.0)
        return jnp.dot(act.astype(BF16), w2b_ref[...], preferred_element_type=F32) + b2_ref[0]

    used = i < nu_ref[0]
    nv = nv_ref[i]

    @pl.when(used & (nv > half))
    def _():
        _store_slabs(y_ref, ffn(_load_slabs(x_ref, tme, BF16)), tme)

    @pl.when(used & (nv <= half))
    def _():
        rows = half * SUBLANES
        _store_slabs(y_ref.at[pl.ds(0, rows)], ffn(_load_slabs(x_ref.at[pl.ds(0, rows)], half, BF16)), half)
        y_ref[pl.ds(rows, rows), :] = jnp.zeros((rows, LANES), F32)

    @pl.when(jnp.logical_not(used))
    def _():
        y_ref[...] = jnp.zeros(y_ref.shape, F32)


def _experts(xs, w1, b1, w2, b2, blk_exp, blk_src, blk_nvalid, blk_next, n_used, *, tme):
    n_exp, d, h2 = w1.shape
    d_ff = w2.shape[1]
    nb = xs.shape[0] // (tme * SUBLANES)
    slab = lambda m: pl.BlockSpec((tme * SUBLANES, LANES), m)
    grid_spec = pltpu.PrefetchScalarGridSpec(
        num_scalar_prefetch=5,
        grid=(nb,),
        in_specs=[slab(lambda i, be, bs, nv, nx, nu: (bs[i], 0)),
                  pl.BlockSpec(memory_space=pl.ANY),
                  pl.BlockSpec((1, 1, h2), lambda i, be, bs, nv, nx, nu: (be[i], 0, 0)),
                  pl.BlockSpec(memory_space=pl.ANY),
                  pl.BlockSpec((1, 1, d), lambda i, be, bs, nv, nx, nu: (be[i], 0, 0))],
        out_specs=slab(lambda i, be, bs, nv, nx, nu: (i, 0)),
        scratch_shapes=[pltpu.VMEM((2, d, h2), F32), pltpu.VMEM((2, d_ff, d), F32),
                        pltpu.VMEM((d, h2), BF16), pltpu.VMEM((d_ff, d), BF16),
                        pltpu.SMEM((1,), jnp.int32), pltpu.SemaphoreType.DMA((2, 2))],
    )
    return pl.pallas_call(
        functools.partial(_experts_kernel, tme=tme),
        grid_spec=grid_spec,
        out_shape=jax.ShapeDtypeStruct(xs.shape, F32),
        compiler_params=_cparams(("arbitrary",)),
        name="experts",
    )(blk_exp, blk_src, blk_nvalid, blk_next, n_used, xs, w1, b1.reshape(n_exp, 1, h2), w2,
      b2.reshape(n_exp, 1, d))


def _combine_kernel(dcur_ref, dnext_ref, x1_ref, route_ref, g_ref, ys_ref, o_ref, buf_ref, sem, *, tc):
    i = pl.program_id(0)
    slot = i % 2

    def gather(dref, s):
        def issue(r, carry):
            row = pl.multiple_of(r * SUBLANES, SUBLANES)
            for k in range(TOP_K):
                d = pl.multiple_of(dref[0, 0, k * tc + r] * SUBLANES, SUBLANES)
                pltpu.make_async_copy(ys_ref.at[pl.ds(d, SUBLANES)], buf_ref.at[s, k, pl.ds(row, SUBLANES)],
                                      sem.at[s]).start(priority=k % 2)
            return carry

        lax.fori_loop(0, tc, issue, 0, unroll=8)

    @pl.when(i == 0)
    def _():
        gather(dcur_ref, 0)

    @pl.when(i + 1 < pl.num_programs(0))
    def _():
        gather(dnext_ref, 1 - slot)

    for k in range(TOP_K):
        pltpu.make_async_copy(ys_ref.at[pl.ds(0, tc * SUBLANES)], buf_ref.at[slot, k], sem.at[slot]).wait()
    route = route_ref[...]
    x1 = x1_ref[...]
    chunks = []
    for j in range(SUBLANES):
        acc = x1[:, j * LANES:(j + 1) * LANES]
        for k in range(TOP_K):
            acc = acc + (route[:, 2 * TOP_K + k:2 * TOP_K + k + 1]
                         * buf_ref[slot, k, pl.ds(j, tc, stride=SUBLANES), :])
        chunks.append(acc)
    y = jnp.concatenate(chunks, axis=1)
    ms = jnp.mean(y * y, axis=-1, keepdims=True)
    o_ref[...] = y * lax.rsqrt(ms + RMS_EPS) * g_ref[...]


def _combine(x1, route, dest, ys, g):
    n, d = x1.shape
    tc = _tile(n, 128, 8)
    steps = n // tc
    dest3 = _dest_blocks(dest, tc)
    dspec = lambda m: pl.BlockSpec((1, 1, tc * TOP_K), m, memory_space=pltpu.SMEM)
    return pl.pallas_call(
        functools.partial(_combine_kernel, tc=tc),
        grid=(steps,),
        in_specs=[dspec(lambda i: (i, 0, 0)), dspec(lambda i: (jnp.minimum(i + 1, steps - 1), 0, 0)),
                  pl.BlockSpec((tc, d), lambda i: (i, 0)),
                  pl.BlockSpec((tc, LANES), lambda i: (i, 0)),
                  pl.BlockSpec((1, d), lambda i: (0, 0)),
                  pl.BlockSpec(memory_space=pl.ANY)],
        out_specs=pl.BlockSpec((tc, d), lambda i: (i, 0)),
        out_shape=jax.ShapeDtypeStruct((n, d), F32),
        scratch_shapes=[pltpu.VMEM((2, TOP_K, tc * SUBLANES, LANES), F32), pltpu.SemaphoreType.DMA((2,))],
        compiler_params=_cparams(("arbitrary",)),
        name="combine",
    )(dest3, dest3, x1, route, g, ys)


def _dest_rows(route_t, blk_start, tme):
    eidx = route_t[0:TOP_K].astype(jnp.int32)
    rank = route_t[TOP_K:2 * TOP_K].astype(jnp.int32)
    experts = jnp.arange(N_EXPERTS, dtype=jnp.int32)[:, None, None]
    first = jnp.sum(jnp.where(eidx[None] == experts, blk_start[:, None, None], 0), axis=0)
    return (first * tme + rank).astype(jnp.int32)


def _routing_tables(counts_f, *, tme, nb):
    counts = counts_f[:, 0].astype(jnp.int32)
    nblk = (counts + tme - 1) // tme
    blk_end = jnp.cumsum(nblk)
    blk_start = blk_end - nblk
    n_used = blk_end[-1]
    b = jnp.arange(nb, dtype=jnp.int32)
    used = b < n_used
    blk_exp = jnp.minimum(jnp.sum((b[:, None] >= blk_end[None, :]).astype(jnp.int32), axis=1), N_EXPERTS - 1)
    last_exp = jnp.max(jnp.where(nblk > 0, jnp.arange(N_EXPERTS, dtype=jnp.int32), 0))
    blk_exp = jnp.where(used, blk_exp, last_exp).astype(jnp.int32)
    blk_src = jnp.where(used, b, 0).astype(jnp.int32)
    experts = jnp.arange(N_EXPERTS, dtype=jnp.int32)
    mine = (b[:, None] >= blk_start[None, :]) & (b[:, None] < blk_end[None, :])
    nvalid = jnp.sum(jnp.where(mine, counts[None, :] - (b[:, None] - blk_start[None, :]) * tme, 0), axis=1)
    nvalid = jnp.clip(nvalid, 0, tme).astype(jnp.int32)
    later = (experts[None, :] > experts[:, None]) & (nblk[None, :] > 0)
    nxt_e = jnp.min(jnp.where(later, experts[None, :], N_EXPERTS), axis=1)
    nxt_e = jnp.where(nxt_e == N_EXPERTS, -1, nxt_e)
    blk_next = jnp.sum(jnp.where(blk_exp[:, None] == experts[None, :], nxt_e[None, :], 0), axis=1).astype(jnp.int32)
    has_tail = (nblk > 0) & (counts % tme != 0)
    partial = jnp.any((b[:, None] == blk_end[None, :] - 1) & has_tail[None, :], axis=1)
    needs_fill = partial | ~used
    fill_blocks = jnp.argsort(~needs_fill, stable=True).astype(jnp.int32)
    n_fill = jnp.sum(needs_fill).reshape(1).astype(jnp.int32)
    return (blk_start, blk_exp, blk_src, nvalid, blk_next, n_used.reshape(1).astype(jnp.int32), fill_blocks,
            n_fill)


def kernel(x_prompt, x_sample, cache_k, cache_v, state_conv, attn_norm_g, w_in, attn_sinks, conv_w, conv_b,
           conv_ln_g, conv_ln_b, w_out, ffn_norm_g, router_w, router_b, w1, b1, w2, b2, final_norm_g):
    depth = w_in.shape[0]
    assert depth == 1, "single-layer step"
    bp, sp, d = x_prompt.shape
    bs, ss, _ = x_sample.shape
    cw = conv_w.shape[2]
    np_, ns = bp * sp, bs * ss
    n_tok = np_ + ns
    assert sp % WINDOW == 0

    xp2 = x_prompt.reshape(np_, d)
    xs2 = x_sample.reshape(ns, d)
    w_in_bf = w_in[0].astype(BF16)
    w_out_bf = w_out[0].astype(BF16)
    g_attn = attn_norm_g[0].reshape(1, d)
    g_ffn = ffn_norm_g[0].reshape(1, d)
    sinks = attn_sinks[0]
    vec = lambda a: a.reshape(1, cw)

    tab_p = _rope_tables(jnp.arange(sp, dtype=jnp.int32))
    tms = _tile(ns, 512, max(ss, 16))
    tab_s = _rope_tables(PAST_LEN + (jnp.arange(tms, dtype=jnp.int32) % ss))
    qp, kp, vp, kfp, vfp, ap = _in_proj(xp2, g_attn, w_in_bf, tab_p, seq_period=sp, q_dtype=BF16, conv_width=cw)
    qs, _, _, kfs, vfs, as_ = _in_proj(xs2, g_attn, w_in_bf, tab_s, seq_period=None, q_dtype=F32, conv_width=cw)

    att_p = _attn_prompt(qp, kp, vp, sinks, batch=bp, seq=sp)
    ck = cache_k[0].reshape(bs, WINDOW, KV_WIDTH)
    cv_ = cache_v[0].reshape(bs, WINDOW, KV_WIDTH)
    att_s, nk_s, nv_s = _attn_sample(qs, kfs, vfs, ck, cv_, sinks, batch=bs, ts=ss)

    cv_p = _conv_prompt(ap, conv_w[0], vec(conv_b[0]), vec(conv_ln_g[0]), vec(conv_ln_b[0]), batch=bp, seq=sp)
    cv_s = _conv_sample(as_, state_conv[0], conv_w[0], vec(conv_b[0]), vec(conv_ln_g[0]), vec(conv_ln_b[0]),
                        batch=bs, ts=ss)

    n_exp = router_w.shape[2]
    assert n_exp == N_EXPERTS
    rwt_bf = router_w[0].T.astype(BF16)
    rbt = jnp.broadcast_to(router_b[0][:, None], (n_exp, LANES))
    zero_carry = jnp.zeros((n_exp, LANES), F32)
    x1p, hnp, route_p, rt_p, cnt_p = _outproj_router(xp2, att_p, cv_p, w_out_bf, g_ffn, rwt_bf, rbt, zero_carry)
    x1s, hns, route_s, rt_s, cnt = _outproj_router(xs2, att_s, cv_s, w_out_bf, g_ffn, rwt_bf, rbt, cnt_p)

    tme = EXPERT_BLOCK_ROWS
    nb = -(-(n_tok * TOP_K + N_EXPERTS * (tme - 1)) // tme)
    blk_start, blk_exp, blk_src, blk_nvalid, blk_next, n_used, fill_blocks, n_fill = _routing_tables(
        cnt, tme=tme, nb=nb)
    dest_p = _dest_rows(rt_p, blk_start, tme)
    dest_s = _dest_rows(rt_s, blk_start, tme)
    xs_sorted = _dispatch(hnp, hns, jnp.concatenate([dest_p, dest_s], axis=1), fill_blocks, n_fill, nb=nb, tme=tme)
    ys = _experts(xs_sorted, w1[0], b1[0], w2[0], b2[0], blk_exp, blk_src, blk_nvalid, blk_next, n_used, tme=tme)
    g_fin = final_norm_g.reshape(1, d)
    y_p = _combine(x1p, route_p, dest_p, ys, g_fin)
    y_s = _combine(x1s, route_s, dest_s, ys, g_fin)

    kv5 = lambda t, bb: t.reshape(bb, -1, KV_WIDTH)[:, -WINDOW:].reshape(bb, WINDOW, N_KV_HEADS, HEAD_DIM)
    new_k_p = kv5(kfp, bp)[None]
    new_v_p = kv5(vfp, bp)[None]
    ctx = CONV_K - 1
    new_c_p = ap.reshape(bp, sp, cw)[:, -ctx:][None]
    new_c_s = jnp.concatenate([state_conv[0], as_.reshape(bs, ss, cw)], axis=1)[:, -ctx:][None]
    return (y_p.reshape(bp, sp, d), y_s.reshape(bs, ss, d), new_k_p, new_v_p, new_c_p,
            kv5(nk_s, bs)[None], kv5(nv_s, bs)[None], new_c_s)
```

```python
import functools

import jax
import jax.numpy as jnp
from jax import lax
from jax.experimental import pallas as pl
from jax.experimental.pallas import tpu as pltpu

F32 = jnp.float32
BF16 = jnp.bfloat16

HEAD_DIM = 64
N_Q_HEADS = 8
N_KV_HEADS = 2
WINDOW = 128
ROPE_THETA = 500000.0
ROPE_DIM = 16
CONV_K = 31
N_EXPERTS = 32
TOP_K = 4
SWIGLU_LIMIT = 7.0
SWIGLU_ALPHA = 1.702
RMS_EPS = 1e-5
LN_EPS = 1e-5
PAST_LEN = 16384

LANES = 128
SUBLANES = 8
CONV_HALO = 32
VMEM_LIMIT = 56 * 1024 * 1024
EXPERT_BLOCK_ROWS = 512
COMBINE_CHUNK = 128
COMBINE_WINDOW = 8

ATTN_WIDTH = N_Q_HEADS * HEAD_DIM
KV_WIDTH = N_KV_HEADS * HEAD_DIM


def _tile(n, pref, mult=8):
    t = min(pref, n)
    while t > 0 and (n % t or t % mult):
        t -= 1
    assert t > 0, (n, pref, mult)
    return t


def _cparams(sem):
    return pltpu.CompilerParams(dimension_semantics=sem, vmem_limit_bytes=VMEM_LIMIT)


def _sigmoid(x):
    return 1.0 / (1.0 + jnp.exp(-x))


def _rope_tables(pos):
    half = ROPE_DIM // 2
    inv_freq = jnp.power(jnp.float32(ROPE_THETA), -jnp.arange(half, dtype=F32) * 2.0 / ROPE_DIM)
    ang = pos.astype(F32)[:, None] * inv_freq[None, :]
    cos, sin = jnp.cos(ang), jnp.sin(ang)
    l64 = jnp.arange(LANES) % HEAD_DIM
    f = l64 % half
    cos_l, sin_l = cos[:, f], sin[:, f]
    c = jnp.where(l64 < ROPE_DIM, cos_l, 1.0)
    s1 = jnp.where(l64 < half, -sin_l, 0.0)
    s2 = jnp.where((l64 >= half) & (l64 < ROPE_DIM), sin_l, 0.0)
    return c.astype(F32), s1.astype(F32), s2.astype(F32)


def _inproj_kernel(x_ref, g_ref, w_ref, c_ref, s1_ref, s2_ref,
                   q_ref, k_ref, v_ref, kf_ref, vf_ref, a_ref, *, conv_width):
    x = x_ref[...]
    ms = jnp.mean(x * x, axis=-1, keepdims=True)
    h = (x * lax.rsqrt(ms + RMS_EPS) * g_ref[...]).astype(BF16)
    z = jnp.dot(h, w_ref[...], preferred_element_type=F32)
    c, s1, s2 = c_ref[...], s1_ref[...], s2_ref[...]
    half = ROPE_DIM // 2

    def rot(t):
        return t * c + pltpu.roll(t, LANES - half, 1) * s1 + pltpu.roll(t, half, 1) * s2

    scale = HEAD_DIM ** -0.5
    for j in range(ATTN_WIDTH // LANES):
        q_ref[:, j * LANES:(j + 1) * LANES] = (rot(z[:, j * LANES:(j + 1) * LANES]) * scale).astype(q_ref.dtype)
    k0 = ATTN_WIDTH
    kr = rot(z[:, k0:k0 + KV_WIDTH])
    k_ref[...] = kr.astype(BF16)
    kf_ref[...] = kr
    v0 = k0 + KV_WIDTH
    vv = z[:, v0:v0 + KV_WIDTH]
    v_ref[...] = vv.astype(BF16)
    vf_ref[...] = vv
    u0 = v0 + KV_WIDTH
    g0 = u0 + conv_width
    a_ref[...] = z[:, u0:g0] * _sigmoid(z[:, g0:g0 + conv_width])


def _in_proj(x2, g, w_bf, tables, *, seq_period, q_dtype, conv_width):
    n, d = x2.shape
    in_w = w_bf.shape[1]
    if seq_period is None:
        tm = tables[0].shape[0]
        tmap = lambda i: (0, 0)
    else:
        tm = _tile(seq_period, 512, 16)
        per = seq_period // tm
        tmap = lambda i: (i % per, 0)
    assert n % tm == 0
    row = lambda w: pl.BlockSpec((tm, w), lambda i: (i, 0))
    tab = pl.BlockSpec((tm, LANES), tmap)
    return pl.pallas_call(
        functools.partial(_inproj_kernel, conv_width=conv_width),
        grid=(n // tm,),
        in_specs=[row(d), pl.BlockSpec((1, d), lambda i: (0, 0)),
                  pl.BlockSpec((d, in_w), lambda i: (0, 0)), tab, tab, tab],
        out_specs=[row(ATTN_WIDTH), row(KV_WIDTH), row(KV_WIDTH), row(KV_WIDTH), row(KV_WIDTH),
                   row(conv_width)],
        out_shape=[jax.ShapeDtypeStruct((n, ATTN_WIDTH), q_dtype),
                   jax.ShapeDtypeStruct((n, KV_WIDTH), BF16),
                   jax.ShapeDtypeStruct((n, KV_WIDTH), BF16),
                   jax.ShapeDtypeStruct((n, KV_WIDTH), F32),
                   jax.ShapeDtypeStruct((n, KV_WIDTH), F32),
                   jax.ShapeDtypeStruct((n, conv_width), F32)],
        compiler_params=_cparams(("parallel",)),
        name="in_proj",
    )(x2, g, w_bf, *tables)


def _dup_head(t, h):
    sw = pltpu.roll(t, HEAD_DIM, 1)
    low = lax.broadcasted_iota(jnp.int32, t.shape, 1) < HEAD_DIM
    return jnp.where(low, t, sw) if h == 0 else jnp.where(low, sw, t)


def _nt_dot(a, b):
    return lax.dot_general(a, b, (((1,), (1,)), ((), ())), preferred_element_type=F32)


def _attn_prompt_kernel(sink_ref, q_ref, kc_ref, kp_ref, vc_ref, vp_ref, o_ref, *, qb):
    j = pl.program_id(1)
    w = WINDOW
    k_all = jnp.concatenate([kp_ref[...], kc_ref[...]], axis=0).astype(F32)
    v_all = jnp.concatenate([vp_ref[...], vc_ref[...]], axis=0).astype(F32)
    r = lax.broadcasted_iota(jnp.int32, (w, 2 * w), 0)
    kk = lax.broadcasted_iota(jnp.int32, (w, 2 * w), 1)
    band = (kk > r) & (kk <= r + w)
    low = lax.broadcasted_iota(jnp.int32, (w, LANES), 1) < HEAD_DIM
    zero = jnp.zeros((w, LANES), BF16)
    group = N_Q_HEADS // N_KV_HEADS
    for h in range(N_KV_HEADS):
        kd_all = _dup_head(k_all, h).astype(BF16)
        vd_all = _dup_head(v_all, h).astype(BF16)
        for sub in range(qb):
            valid = band & ((kk >= w) | (j > 0)) if sub == 0 else band
            kd = kd_all[sub * w:(sub + 2) * w, :]
            vd = vd_all[sub * w:(sub + 2) * w, :]
            for jj in range(group // 2):
                col = (h * group // 2 + jj) * LANES
                qv = q_ref[sub * w:(sub + 1) * w, col:col + LANES]
                halves = []
                for half in range(2):
                    head = h * group + jj * 2 + half
                    qm = jnp.where(low if half == 0 else ~low, qv, zero)
                    s = jnp.where(valid, _nt_dot(qm, kd), -jnp.inf)
                    sink = sink_ref[head]
                    m = jnp.maximum(jnp.max(s, axis=-1, keepdims=True), sink)
                    p = jnp.exp(s - m)
                    den = jnp.sum(p, axis=-1, keepdims=True) + jnp.exp(sink - m)
                    o = jnp.dot(p.astype(BF16), vd, preferred_element_type=F32)
                    halves.append(o / den)
                o_ref[sub * w:(sub + 1) * w, col:col + LANES] = (
                    jnp.where(low, halves[0], halves[1]).astype(o_ref.dtype))


def _attn_prompt(q, k, v, sinks, *, batch, seq):
    nb = seq // WINDOW
    qb = 2 if nb % 2 == 0 else 1
    steps = nb // qb
    cur = lambda b, j: (b * steps + j, 0)
    prev = lambda b, j: (b * nb + jnp.maximum(j * qb - 1, 0), 0)
    return pl.pallas_call(
        functools.partial(_attn_prompt_kernel, qb=qb),
        grid=(batch, steps),
        in_specs=[pl.BlockSpec(memory_space=pltpu.SMEM),
                  pl.BlockSpec((qb * WINDOW, ATTN_WIDTH), cur),
                  pl.BlockSpec((qb * WINDOW, KV_WIDTH), cur), pl.BlockSpec((WINDOW, KV_WIDTH), prev),
                  pl.BlockSpec((qb * WINDOW, KV_WIDTH), cur), pl.BlockSpec((WINDOW, KV_WIDTH), prev)],
        out_specs=pl.BlockSpec((qb * WINDOW, ATTN_WIDTH), cur),
        out_shape=jax.ShapeDtypeStruct((batch * seq, ATTN_WIDTH), BF16),
        compiler_params=_cparams(("parallel", "parallel")),
        name="attn_prompt",
    )(sinks, q, k, k, v, v)


def _attn_sample_kernel(sink_ref, q_ref, kn_ref, vn_ref, ck_ref, cv_ref, o_ref, nk_ref, nv_ref, *, gb, ts):
    w = WINDOW
    group = N_Q_HEADS // N_KV_HEADS
    rows = group * ts
    low = lax.broadcasted_iota(jnp.int32, (ts, LANES), 1) < HEAD_DIM
    pad = jnp.zeros((ts, LANES), F32)
    s_c, s_n, v_dup = [], [], []
    for b in range(gb):
        kc, vc = ck_ref[b], cv_ref[b]
        kn, vn = kn_ref[b * ts:(b + 1) * ts, :], vn_ref[b * ts:(b + 1) * ts, :]
        nk_ref[b, 0:w - ts, :] = kc[ts:, :]
        nk_ref[b, w - ts:, :] = kn
        nv_ref[b, 0:w - ts, :] = vc[ts:, :]
        nv_ref[b, w - ts:, :] = vn
        knp = jnp.concatenate([kn, pad], axis=0)
        vnp = jnp.concatenate([vn, pad], axis=0)
        qb = q_ref[b * ts:(b + 1) * ts, :]
        for h in range(N_KV_HEADS):
            parts = []
            for jj in range(group // 2):
                col = (h * group // 2 + jj) * LANES
                qv = qb[:, col:col + LANES]
                parts += [jnp.where(low, qv, 0.0), jnp.where(low, 0.0, qv)]
            lhs = jnp.concatenate(parts, axis=0).astype(BF16)
            s_c.append(_nt_dot(lhs, _dup_head(kc, h).astype(BF16)))
            s_n.append(_nt_dot(lhs, _dup_head(knp, h).astype(BF16)))
            v_dup.append((_dup_head(vc, h).astype(BF16), _dup_head(vnp, h).astype(BF16)))
    s_c = jnp.concatenate(s_c, axis=0)
    s_n = jnp.concatenate(s_n, axis=0)
    n_rows = s_c.shape[0]
    ridx = lax.broadcasted_iota(jnp.int32, (n_rows, 1), 0)
    t_row = ridx % ts
    head_row = (ridx // ts) % N_Q_HEADS
    sink = jnp.zeros((n_rows, 1), F32)
    for hd in range(N_Q_HEADS):
        sink = jnp.where(head_row == hd, sink_ref[hd], sink)
    c_idx = lax.broadcasted_iota(jnp.int32, (n_rows, w), 1)
    n_idx = lax.broadcasted_iota(jnp.int32, (n_rows, 2 * ts), 1)
    s_c = jnp.where(c_idx > t_row, s_c, -jnp.inf)
    s_n = jnp.where(n_idx <= t_row, s_n, -jnp.inf)
    m = jnp.maximum(jnp.maximum(jnp.max(s_c, axis=-1, keepdims=True), jnp.max(s_n, axis=-1, keepdims=True)), sink)
    p_c = jnp.exp(s_c - m)
    p_n = jnp.exp(s_n - m)
    den = jnp.sum(p_c, axis=-1, keepdims=True) + jnp.sum(p_n, axis=-1, keepdims=True) + jnp.exp(sink - m)
    p_c = p_c.astype(BF16)
    p_n = p_n.astype(BF16)
    outs = []
    for b in range(gb):
        cols = []
        for h in range(N_KV_HEADS):
            ci = b * N_KV_HEADS + h
            sl = slice(ci * rows, (ci + 1) * rows)
            vdc, vdn = v_dup[ci]
            o = (jnp.dot(p_c[sl], vdc, preferred_element_type=F32)
                 + jnp.dot(p_n[sl], vdn, preferred_element_type=F32)) / den[sl]
            for jj in range(group // 2):
                lo_part = o[(2 * jj) * ts:(2 * jj + 1) * ts, :]
                hi_part = o[(2 * jj + 1) * ts:(2 * jj + 2) * ts, :]
                cols.append(jnp.where(low, lo_part, hi_part))
        outs.append(jnp.concatenate(cols, axis=1))
    o_ref[...] = jnp.concatenate(outs, axis=0).astype(o_ref.dtype)


def _attn_sample(q, kf, vf, cache_k, cache_v, sinks, *, batch, ts):
    assert ts % 8 == 0 and ts <= WINDOW
    gb = _tile(batch, 8, 2)
    tok = lambda w: pl.BlockSpec((gb * ts, w), lambda i: (i, 0))
    cache = pl.BlockSpec((gb, WINDOW, KV_WIDTH), lambda i: (i, 0, 0))
    cshape = jax.ShapeDtypeStruct((batch, WINDOW, KV_WIDTH), F32)
    return pl.pallas_call(
        functools.partial(_attn_sample_kernel, gb=gb, ts=ts),
        grid=(batch // gb,),
        in_specs=[pl.BlockSpec(memory_space=pltpu.SMEM), tok(ATTN_WIDTH), tok(KV_WIDTH), tok(KV_WIDTH),
                  cache, cache],
        out_specs=[tok(ATTN_WIDTH), cache, cache],
        out_shape=[jax.ShapeDtypeStruct((batch * ts, ATTN_WIDTH), BF16), cshape, cshape],
        compiler_params=_cparams(("parallel",)),
        name="attn_sample",
    )(sinks, q, kf, vf, cache_k, cache_v)


def _ln_swish(acc, b, lg, lb):
    y = acc + b
    mu = jnp.mean(y, axis=-1, keepdims=True)
    yc = y - mu
    var = jnp.mean(yc * yc, axis=-1, keepdims=True)
    yn = yc * lax.rsqrt(var + LN_EPS) * lg + lb
    return yn * _sigmoid(yn)


def _conv_prompt_kernel(a_ref, ap_ref, w_ref, b_ref, lg_ref, lb_ref, o_ref, win_ref, *, tt, rc):
    j = pl.program_id(1)
    n = CONV_HALO + tt
    win = jnp.concatenate([jnp.where(j > 0, ap_ref[...], 0.0), a_ref[...]], axis=0)
    win_ref[0] = win
    for r in range(1, SUBLANES):
        win_ref[r] = pltpu.roll(win, n - r, 0)
    off = CONV_HALO - (CONV_K - 1)
    b, lg, lb = b_ref[...], lg_ref[...], lb_ref[...]
    for c in range(tt // rc):
        acc = jnp.zeros((rc, a_ref.shape[1]), F32)
        for k in range(CONV_K):
            s = off + k
            base = c * rc + (s // SUBLANES) * SUBLANES
            wk = jnp.concatenate([w_ref[k]] * (rc // SUBLANES), axis=0)
            acc = acc + wk * win_ref[s % SUBLANES, base:base + rc, :]
        o_ref[c * rc:(c + 1) * rc, :] = _ln_swish(acc, b, lg, lb).astype(o_ref.dtype)


def _conv_prompt(a, w, b, lg, lb, *, batch, seq):
    cw = a.shape[1]
    tt = _tile(seq, 256, CONV_HALO)
    rc = _tile(tt, 32, 16)
    nt = seq // tt
    per = tt // CONV_HALO
    cur = lambda bb, j: (bb * nt + j, 0)
    prev = lambda bb, j: (jnp.maximum((bb * nt + j) * per - 1, 0), 0)
    vec = pl.BlockSpec((1, cw), lambda bb, j: (0, 0))
    return pl.pallas_call(
        functools.partial(_conv_prompt_kernel, tt=tt, rc=rc),
        grid=(batch, nt),
        in_specs=[pl.BlockSpec((tt, cw), cur), pl.BlockSpec((CONV_HALO, cw), prev),
                  pl.BlockSpec((CONV_K, SUBLANES, cw), lambda bb, j: (0, 0, 0)), vec, vec, vec],
        out_specs=pl.BlockSpec((tt, cw), cur),
        out_shape=jax.ShapeDtypeStruct((batch * seq, cw), BF16),
        scratch_shapes=[pltpu.VMEM((SUBLANES, CONV_HALO + tt, cw), F32)],
        compiler_params=_cparams(("parallel", "parallel")),
        name="conv_prompt",
    )(a, a, jnp.broadcast_to(w[:, None, :], (CONV_K, SUBLANES, cw)), b, lg, lb)


def _conv_sample_kernel(a_ref, st_ref, w_ref, b_ref, lg_ref, lb_ref, o_ref, win_ref, *, gb, ts):
    ctx = CONV_K - 1
    b, lg, lb = b_ref[...], lg_ref[...], lb_ref[...]
    for bb in range(gb):
        win_ref[bb, 0:ctx, :] = st_ref[bb]
        win_ref[bb, ctx:ctx + ts, :] = a_ref[bb * ts:(bb + 1) * ts, :]
    outs = []
    for bb in range(gb):
        acc = jnp.zeros((ts, a_ref.shape[1]), F32)
        for k in range(CONV_K):
            wk = jnp.concatenate([w_ref[k]] * (ts // SUBLANES), axis=0)
            acc = acc + wk * win_ref[bb, k:k + ts, :]
        outs.append(_ln_swish(acc, b, lg, lb))
    o_ref[...] = jnp.concatenate(outs, axis=0).astype(o_ref.dtype)


def _conv_sample(a, state, w, b, lg, lb, *, batch, ts):
    cw = a.shape[1]
    ctx = CONV_K - 1
    gb = _tile(batch, 8, 2)
    vec = pl.BlockSpec((1, cw), lambda i: (0, 0))
    return pl.pallas_call(
        functools.partial(_conv_sample_kernel, gb=gb, ts=ts),
        grid=(batch // gb,),
        in_specs=[pl.BlockSpec((gb * ts, cw), lambda i: (i, 0)),
                  pl.BlockSpec((gb, ctx, cw), lambda i: (i, 0, 0)),
                  pl.BlockSpec((CONV_K, SUBLANES, cw), lambda i: (0, 0, 0)), vec, vec, vec],
        out_specs=pl.BlockSpec((gb * ts, cw), lambda i: (i, 0)),
        out_shape=jax.ShapeDtypeStruct((batch * ts, cw), BF16),
        scratch_shapes=[pltpu.VMEM((gb, ctx + ts + 2, cw), F32)],
        compiler_params=_cparams(("parallel",)),
        name="conv_sample",
    )(a, state, jnp.broadcast_to(w[:, None, :], (CONV_K, SUBLANES, cw)), b, lg, lb)


def _outproj_router_kernel(x_ref, att_ref, cv_ref, wo_ref, g_ref, rw_ref, rb_ref, cin_ref,
                           x1_ref, hn_ref, route_ref, rt_ref, chunk_ref, cnt_ref, carry_ref, *, ts):
    i = pl.program_id(0)

    @pl.when(i == 0)
    def _():
        carry_ref[...] = cin_ref[...]

    aw = att_ref.shape[1]
    n_exp = rw_ref.shape[0]
    r = lax.broadcasted_iota(jnp.int32, (ts, ts), 0)
    c = lax.broadcasted_iota(jnp.int32, (ts, ts), 1)
    before = (r < c).astype(BF16)
    eid = lax.broadcasted_iota(jnp.int32, (n_exp, ts), 0)
    carry = carry_ref[...][:, 0:1]
    for s in range(x_ref.shape[0] // ts):
        rs = slice(s * ts, (s + 1) * ts)
        mix = (jnp.dot(att_ref[rs, :], wo_ref[0:aw, :], preferred_element_type=F32)
               + jnp.dot(cv_ref[rs, :], wo_ref[aw:, :], preferred_element_type=F32))
        x1 = x_ref[rs, :] + mix
        x1_ref[rs, :] = x1
        ms = jnp.mean(x1 * x1, axis=-1, keepdims=True)
        hn = x1 * lax.rsqrt(ms + RMS_EPS) * g_ref[...]
        _store_slabs(hn_ref.at[pl.ds(s * ts * SUBLANES, ts * SUBLANES)], hn, ts)
        logits = _nt_dot(rw_ref[...], hn.astype(BF16)) + rb_ref[...][:, 0:1]
        onehot = jnp.zeros((n_exp, ts), F32)
        vals, idxs = [], []
        for _ in range(TOP_K):
            m = jnp.max(logits, axis=0, keepdims=True)
            idx = jnp.min(jnp.where(logits == m, eid, n_exp), axis=0, keepdims=True)
            sel = eid == idx
            onehot = onehot + sel.astype(F32)
            logits = jnp.where(sel, -jnp.inf, logits)
            vals.append(m)
            idxs.append(idx)
        es = [jnp.exp(v - vals[0]) for v in vals]
        den = es[0] + es[1] + es[2] + es[3]
        prefix = jnp.dot(onehot.astype(BF16), before, preferred_element_type=F32) + carry
        ranks = [jnp.sum(jnp.where(eid == idxs[k], prefix, 0.0), axis=0, keepdims=True) for k in range(TOP_K)]
        rows = [ix.astype(F32) for ix in idxs] + ranks
        rt_ref[:, rs] = jnp.concatenate(rows, axis=0)
        fields = jnp.concatenate(rows + [e / den for e in es] + [jnp.zeros((LANES - 3 * TOP_K, ts), F32)],
                                 axis=0)
        route_ref[rs, :] = jnp.transpose(fields)
        tok = lax.broadcasted_iota(jnp.int32, (ts, LANES), 0)
        col = lax.broadcasted_iota(jnp.int32, (ts, LANES), 1)
        in_chunk = ((tok >= col * COMBINE_CHUNK) & (tok < (col + 1) * COMBINE_CHUNK)).astype(BF16)
        chunk_ref[s] = jnp.dot(onehot.astype(BF16), in_chunk, preferred_element_type=F32)
        carry = carry + jnp.sum(onehot, axis=1, keepdims=True)
    carry_ref[...] = jnp.broadcast_to(carry, carry_ref.shape)
    cnt_ref[...] = jnp.broadcast_to(carry, cnt_ref.shape)


def _store_slabs(ref, val, rows):
    for j in range(val.shape[1] // LANES):
        ref[pl.ds(j, rows, stride=SUBLANES), :] = val[:, j * LANES:(j + 1) * LANES]


def _load_slabs(ref, rows, dtype):
    return jnp.concatenate([ref[pl.ds(j, rows, stride=SUBLANES), :].astype(dtype) for j in range(SUBLANES)],
                           axis=1)


def _outproj_router(x2, att, cv, wo_bf, g, rwt_bf, rbt, carry_in):
    n, d = x2.shape
    assert d == SUBLANES * LANES
    n_exp = rwt_bf.shape[0]
    tm = _tile(n, 512, 16)
    ts = _tile(tm, 512, 16)
    assert ts % COMBINE_CHUNK == 0 and ts // COMBINE_CHUNK <= LANES
    row = lambda w: pl.BlockSpec((tm, w), lambda i: (i, 0))
    full = lambda a: pl.BlockSpec(a.shape, lambda i: (0,) * a.ndim)
    x1, hn, route, route_t, chunk3, cnt = pl.pallas_call(
        functools.partial(_outproj_router_kernel, ts=ts),
        grid=(n // tm,),
        in_specs=[row(d), row(att.shape[1]), row(cv.shape[1]), full(wo_bf), full(g), full(rwt_bf), full(rbt),
                  full(carry_in)],
        out_specs=[row(d), pl.BlockSpec((tm * SUBLANES, LANES), lambda i: (i, 0)), row(LANES),
                   pl.BlockSpec((2 * TOP_K, tm), lambda i: (0, i)),
                   pl.BlockSpec((tm // ts, n_exp, LANES), lambda i: (i, 0, 0)),
                   pl.BlockSpec((n_exp, LANES), lambda i: (0, 0))],
        out_shape=[jax.ShapeDtypeStruct((n, d), F32), jax.ShapeDtypeStruct((n * SUBLANES, LANES), F32),
                   jax.ShapeDtypeStruct((n, LANES), F32), jax.ShapeDtypeStruct((2 * TOP_K, n), F32),
                   jax.ShapeDtypeStruct((n // ts, n_exp, LANES), F32),
                   jax.ShapeDtypeStruct((n_exp, LANES), F32)],
        scratch_shapes=[pltpu.VMEM((n_exp, LANES), F32)],
        compiler_params=_cparams(("arbitrary",)),
        name="outproj_router",
    )(x2, att, cv, wo_bf, g, rwt_bf, rbt, carry_in)
    chunk_counts = chunk3[:, :, :ts // COMBINE_CHUNK].transpose(0, 2, 1).reshape(n // COMBINE_CHUNK, n_exp)
    return x1, hn, route, route_t, chunk_counts, cnt


def _dispatch_kernel(fill_ref, nfill_ref, dest_ref, hn_ref, hn2_ref, xs_ref, zero_ref, sem, zsem,
                     *, td, tme, n_first):
    @pl.when(pl.program_id(0) == 0)
    def _():
        zero_ref[...] = jnp.zeros(zero_ref.shape, F32)

        def fill(f):
            row = pl.multiple_of(fill_ref[f] * (tme * SUBLANES), tme * SUBLANES)
            return pltpu.make_async_copy(zero_ref, xs_ref.at[pl.ds(row, tme * SUBLANES)], zsem)

        def start(f, carry):
            fill(f).start()
            return carry

        def wait(f, carry):
            fill(f).wait()
            return carry

        lax.fori_loop(0, nfill_ref[0], start, 0)
        lax.fori_loop(0, nfill_ref[0], wait, 0)

    def scatter_rows(src_ref):
        def issue(r, carry):
            src = src_ref.at[pl.ds(pl.multiple_of(r * SUBLANES, SUBLANES), SUBLANES)]
            for k in range(TOP_K):
                d = pl.multiple_of(dest_ref[0, 0, k * td + r] * SUBLANES, SUBLANES)
                pltpu.make_async_copy(src, xs_ref.at[pl.ds(d, SUBLANES)], sem).start(priority=k % 2)
            return carry

        lax.fori_loop(0, td, issue, 0, unroll=8)
        for _ in range(TOP_K):
            pltpu.make_async_copy(src_ref, xs_ref.at[pl.ds(0, td * SUBLANES)], sem).wait()

    @pl.when(pl.program_id(0) < n_first)
    def _():
        scatter_rows(hn_ref)

    @pl.when(pl.program_id(0) >= n_first)
    def _():
        scatter_rows(hn2_ref)


def _dest_blocks(dest, td):
    n = dest.shape[1]
    return dest.reshape(TOP_K, n // td, td).transpose(1, 0, 2).reshape(n // td, 1, TOP_K * td)


def _dispatch(hn_a, hn_b, dest, fill_blocks, n_fill, *, nb, tme):
    na = hn_a.shape[0] // SUBLANES
    nb_rows = hn_b.shape[0] // SUBLANES
    td = _tile(nb_rows, 256, 8)
    assert na % td == 0
    n_first = na // td
    steps = n_first + nb_rows // td
    slab = lambda m: pl.BlockSpec((td * SUBLANES, LANES), m)
    grid_spec = pltpu.PrefetchScalarGridSpec(
        num_scalar_prefetch=2,
        grid=(steps,),
        in_specs=[pl.BlockSpec((1, 1, td * TOP_K), lambda i, fb, nf: (i, 0, 0), memory_space=pltpu.SMEM),
                  slab(lambda i, fb, nf: (jnp.minimum(i, n_first - 1), 0)),
                  slab(lambda i, fb, nf: (jnp.maximum(i - n_first, 0), 0))],
        out_specs=pl.BlockSpec(memory_space=pl.ANY),
        scratch_shapes=[pltpu.VMEM((tme * SUBLANES, LANES), F32), pltpu.SemaphoreType.DMA(()),
                        pltpu.SemaphoreType.DMA(())],
    )
    return pl.pallas_call(
        functools.partial(_dispatch_kernel, td=td, tme=tme, n_first=n_first),
        grid_spec=grid_spec,
        out_shape=jax.ShapeDtypeStruct((nb * tme * SUBLANES, LANES), F32),
        compiler_params=_cparams(("arbitrary",)),
        name="dispatch",
    )(fill_blocks, n_fill, _dest_blocks(dest, td), hn_a, hn_b)


def _experts_kernel(be_ref, bsrc_ref, nv_ref, nx_ref, nu_ref, x_ref, w1_hbm, b1_ref, w2_hbm, b2_ref,
                    y_ref, w1f_ref, w2f_ref, w1b_ref, w2b_ref, par_ref, wsem, *, tme):
    i = pl.program_id(0)
    e = be_ref[i]
    e_prev = be_ref[jnp.maximum(i - 1, 0)]
    d_ff = w2f_ref.shape[1]
    half = tme // 2

    def fetch(expert, s):
        return (pltpu.make_async_copy(w1_hbm.at[expert], w1f_ref.at[s], wsem.at[0, s]),
                pltpu.make_async_copy(w2_hbm.at[expert], w2f_ref.at[s], wsem.at[1, s]))

    @pl.when(i == 0)
    def _():
        par_ref[0] = 0
        for c in fetch(e, 0):
            c.start()

    @pl.when((i == 0) | (e != e_prev))
    def _():
        s = par_ref[0]
        for c in fetch(e, s):
            c.wait()
        w1b_ref[...] = w1f_ref[s].astype(BF16)
        w2b_ref[...] = w2f_ref[s].astype(BF16)
        nxt = nx_ref[i]

        @pl.when(nxt >= 0)
        def _():
            for c in fetch(nxt, 1 - s):
                c.start()

        par_ref[0] = 1 - s

    def ffn(x):
        h = jnp.dot(x, w1b_ref[...], preferred_element_type=F32) + b1_ref[0]
        x_glu = jnp.minimum(h[:, :d_ff], SWIGLU_LIMIT)
        x_lin = jnp.clip(h[:, d_ff:], -SWIGLU_LIMIT, SWIGLU_LIMIT)
        act = x_glu * _sigmoid(SWIGLU_ALPHA * x_glu) * (x_lin + 1.0)
        return jnp.dot(act.astype(BF16), w2b_ref[...], preferred_element_type=F32) + b2_ref[0]

    used = i < nu_ref[0]
    nv = nv_ref[i]

    @pl.when(used & (nv > half))
    def _():
        _store_slabs(y_ref, ffn(_load_slabs(x_ref, tme, BF16)), tme)

    @pl.when(used & (nv <= half))
    def _():
        rows = half * SUBLANES
        _store_slabs(y_ref.at[pl.ds(0, rows)], ffn(_load_slabs(x_ref.at[pl.ds(0, rows)], half, BF16)), half)
        y_ref[pl.ds(rows, rows), :] = jnp.zeros((rows, LANES), F32)

    @pl.when(jnp.logical_not(used))
    def _():
        y_ref[...] = jnp.zeros(y_ref.shape, F32)


def _experts(xs, w1, b1, w2, b2, blk_exp, blk_src, blk_nvalid, blk_next, n_used, *, tme):
    n_exp, d, h2 = w1.shape
    d_ff = w2.shape[1]
    nb = xs.shape[0] // (tme * SUBLANES)
    slab = lambda m: pl.BlockSpec((tme * SUBLANES, LANES), m)
    grid_spec = pltpu.PrefetchScalarGridSpec(
        num_scalar_prefetch=5,
        grid=(nb,),
        in_specs=[slab(lambda i, be, bs, nv, nx, nu: (bs[i], 0)),
                  pl.BlockSpec(memory_space=pl.ANY),
                  pl.BlockSpec((1, 1, h2), lambda i, be, bs, nv, nx, nu: (be[i], 0, 0)),
                  pl.BlockSpec(memory_space=pl.ANY),
                  pl.BlockSpec((1, 1, d), lambda i, be, bs, nv, nx, nu: (be[i], 0, 0))],
        out_specs=slab(lambda i, be, bs, nv, nx, nu: (i, 0)),
        scratch_shapes=[pltpu.VMEM((2, d, h2), F32), pltpu.VMEM((2, d_ff, d), F32),
                        pltpu.VMEM((d, h2), BF16), pltpu.VMEM((d_ff, d), BF16),
                        pltpu.SMEM((1,), jnp.int32), pltpu.SemaphoreType.DMA((2, 2))],
    )
    return pl.pallas_call(
        functools.partial(_experts_kernel, tme=tme),
        grid_spec=grid_spec,
        out_shape=jax.ShapeDtypeStruct(xs.shape, F32),
        compiler_params=_cparams(("arbitrary",)),
        name="experts",
    )(blk_exp, blk_src, blk_nvalid, blk_next, n_used, xs, w1, b1.reshape(n_exp, 1, h2), w2,
      b2.reshape(n_exp, 1, d))


def _combine_kernel(tcur_ref, tnext_ref, x1_ref, route_ref, base_ref, g_ref, ys_ref, o_ref, buf_ref, sem,
                    *, tc, krows):
    i = pl.program_id(0)
    slot = i % 2
    n_exp = N_EXPERTS
    wrows = COMBINE_WINDOW * SUBLANES

    def window(tab, s, e, w):
        src = pl.multiple_of((tab[0, 0, e] + w * COMBINE_WINDOW) * SUBLANES, SUBLANES)
        dst = pl.multiple_of((tab[0, 0, 2 * n_exp + e] + w * COMBINE_WINDOW) * SUBLANES, SUBLANES)
        return pltpu.make_async_copy(ys_ref.at[pl.ds(src, wrows)], buf_ref.at[s, pl.ds(dst, wrows)], sem.at[s])

    def fetch(tab, s):
        for e in range(n_exp):
            def issue(w, carry, e=e):
                window(tab, s, e, w).start(priority=e % 2)
                return carry
            lax.fori_loop(0, tab[0, 0, n_exp + e], issue, 0)

    @pl.when(i == 0)
    def _():
        buf_ref[...] = jnp.zeros(buf_ref.shape, F32)
        fetch(tcur_ref, 0)

    @pl.when(i + 1 < pl.num_programs(0))
    def _():
        fetch(tnext_ref, 1 - slot)

    def wait_one(w, carry):
        window(tcur_ref, slot, 0, 0).wait()
        return carry

    lax.fori_loop(0, tcur_ref[0, 0, 3 * n_exp], wait_one, 0)

    route = route_ref[...]
    lane = lax.broadcasted_iota(jnp.int32, (tc, LANES), 1)
    pos_iota = lax.broadcasted_iota(jnp.int32, (tc, krows), 1)
    gmat = jnp.zeros((tc, krows), F32)
    for k in range(TOP_K):
        eidx = route[:, k:k + 1].astype(jnp.int32)
        offset = jnp.sum(jnp.where(lane == eidx, base_ref[0], 0.0), axis=-1, keepdims=True)
        pos = (route[:, TOP_K + k:TOP_K + k + 1] + offset).astype(jnp.int32)
        gmat = gmat + jnp.where(pos_iota == pos, route[:, 2 * TOP_K + k:2 * TOP_K + k + 1], 0.0)
    g_hi = gmat.astype(BF16)
    g_lo = (gmat - g_hi.astype(F32)).astype(BF16)
    rows = _load_slabs(buf_ref.at[slot], krows, BF16)
    moe = (jnp.dot(g_hi, rows, preferred_element_type=F32) + jnp.dot(g_lo, rows, preferred_element_type=F32))
    y = x1_ref[...] + moe
    ms = jnp.mean(y * y, axis=-1, keepdims=True)
    o_ref[...] = y * lax.rsqrt(ms + RMS_EPS) * g_ref[...]


def _combine(x1, route, tables, bases, ys, g, *, chunk0):
    n, d = x1.shape
    tc = COMBINE_CHUNK
    assert n % tc == 0
    steps = n // tc
    krows = -(-(TOP_K * tc + N_EXPERTS * (COMBINE_WINDOW - 1)) // LANES) * LANES
    tspec = lambda m: pl.BlockSpec((1, 1, LANES), m, memory_space=pltpu.SMEM)
    return pl.pallas_call(
        functools.partial(_combine_kernel, tc=tc, krows=krows),
        grid=(steps,),
        in_specs=[tspec(lambda i: (chunk0 + i, 0, 0)),
                  tspec(lambda i: (chunk0 + jnp.minimum(i + 1, steps - 1), 0, 0)),
                  pl.BlockSpec((tc, d), lambda i: (i, 0)),
                  pl.BlockSpec((tc, LANES), lambda i: (i, 0)),
                  pl.BlockSpec((1, 1, LANES), lambda i: (chunk0 + i, 0, 0)),
                  pl.BlockSpec((1, d), lambda i: (0, 0)),
                  pl.BlockSpec(memory_space=pl.ANY)],
        out_specs=pl.BlockSpec((tc, d), lambda i: (i, 0)),
        out_shape=jax.ShapeDtypeStruct((n, d), F32),
        scratch_shapes=[pltpu.VMEM((2, krows * SUBLANES, LANES), F32), pltpu.SemaphoreType.DMA((2,))],
        compiler_params=_cparams(("arbitrary",)),
        name="combine",
    )(tables, tables, x1, route, bases, g, ys)


def _combine_tables(chunk_counts, blk_start, tme):
    cc = chunk_counts.astype(jnp.int32)
    before = jnp.cumsum(cc, axis=0) - cc
    first_row = blk_start[None, :] * tme + before
    n_win = (cc + COMBINE_WINDOW - 1) // COMBINE_WINDOW
    win_rows = n_win * COMBINE_WINDOW
    first_win = jnp.cumsum(win_rows, axis=1) - win_rows
    total = jnp.sum(n_win, axis=1, keepdims=True)
    pad = jnp.zeros((cc.shape[0], LANES - 3 * N_EXPERTS - 1), jnp.int32)
    tables = jnp.concatenate([first_row, n_win, first_win, total, pad], axis=1).astype(jnp.int32)
    bases = jnp.concatenate([(first_win - before).astype(F32),
                             jnp.zeros((cc.shape[0], LANES - N_EXPERTS), F32)], axis=1)
    return tables[:, None, :], bases[:, None, :]


def _dest_rows(route_t, blk_start, tme):
    eidx = route_t[0:TOP_K].astype(jnp.int32)
    rank = route_t[TOP_K:2 * TOP_K].astype(jnp.int32)
    experts = jnp.arange(N_EXPERTS, dtype=jnp.int32)[:, None, None]
    first = jnp.sum(jnp.where(eidx[None] == experts, blk_start[:, None, None], 0), axis=0)
    return (first * tme + rank).astype(jnp.int32)


def _routing_tables(counts_f, *, tme, nb):
    counts = counts_f[:, 0].astype(jnp.int32)
    nblk = (counts + tme - 1) // tme
    blk_end = jnp.cumsum(nblk)
    blk_start = blk_end - nblk
    n_used = blk_end[-1]
    b = jnp.arange(nb, dtype=jnp.int32)
    used = b < n_used
    blk_exp = jnp.minimum(jnp.sum((b[:, None] >= blk_end[None, :]).astype(jnp.int32), axis=1), N_EXPERTS - 1)
    last_exp = jnp.max(jnp.where(nblk > 0, jnp.arange(N_EXPERTS, dtype=jnp.int32), 0))
    blk_exp = jnp.where(used, blk_exp, last_exp).astype(jnp.int32)
    blk_src = jnp.where(used, b, 0).astype(jnp.int32)
    experts = jnp.arange(N_EXPERTS, dtype=jnp.int32)
    mine = (b[:, None] >= blk_start[None, :]) & (b[:, None] < blk_end[None, :])
    nvalid = jnp.sum(jnp.where(mine, counts[None, :] - (b[:, None] - blk_start[None, :]) * tme, 0), axis=1)
    nvalid = jnp.clip(nvalid, 0, tme).astype(jnp.int32)
    later = (experts[None, :] > experts[:, None]) & (nblk[None, :] > 0)
    nxt_e = jnp.min(jnp.where(later, experts[None, :], N_EXPERTS), axis=1)
    nxt_e = jnp.where(nxt_e == N_EXPERTS, -1, nxt_e)
    blk_next = jnp.sum(jnp.where(blk_exp[:, None] == experts[None, :], nxt_e[None, :], 0), axis=1).astype(jnp.int32)
    has_tail = (nblk > 0) & (counts % tme != 0)
    partial = jnp.any((b[:, None] == blk_end[None, :] - 1) & has_tail[None, :], axis=1)
    needs_fill = partial | ~used
    fill_blocks = jnp.argsort(~needs_fill, stable=True).astype(jnp.int32)
    n_fill = jnp.sum(needs_fill).reshape(1).astype(jnp.int32)
    return (blk_start, blk_exp, blk_src, nvalid, blk_next, n_used.reshape(1).astype(jnp.int32), fill_blocks,
            n_fill)


def kernel(x_prompt, x_sample, cache_k, cache_v, state_conv, attn_norm_g, w_in, attn_sinks, conv_w, conv_b,
           conv_ln_g, conv_ln_b, w_out, ffn_norm_g, router_w, router_b, w1, b1, w2, b2, final_norm_g):
    depth = w_in.shape[0]
    assert depth == 1, "single-layer step"
    bp, sp, d = x_prompt.shape
    bs, ss, _ = x_sample.shape
    cw = conv_w.shape[2]
    np_, ns = bp * sp, bs * ss
    n_tok = np_ + ns
    assert sp % WINDOW == 0

    xp2 = x_prompt.reshape(np_, d)
    xs2 = x_sample.reshape(ns, d)
    w_in_bf = w_in[0].astype(BF16)
    w_out_bf = w_out[0].astype(BF16)
    g_attn = attn_norm_g[0].reshape(1, d)
    g_ffn = ffn_norm_g[0].reshape(1, d)
    sinks = attn_sinks[0]
    vec = lambda a: a.reshape(1, cw)

    tab_p = _rope_tables(jnp.arange(sp, dtype=jnp.int32))
    tms = _tile(ns, 512, max(ss, 16))
    tab_s = _rope_tables(PAST_LEN + (jnp.arange(tms, dtype=jnp.int32) % ss))
    qp, kp, vp, kfp, vfp, ap = _in_proj(xp2, g_attn, w_in_bf, tab_p, seq_period=sp, q_dtype=BF16, conv_width=cw)
    qs, _, _, kfs, vfs, as_ = _in_proj(xs2, g_attn, w_in_bf, tab_s, seq_period=None, q_dtype=F32, conv_width=cw)

    att_p = _attn_prompt(qp, kp, vp, sinks, batch=bp, seq=sp)
    ck = cache_k[0].reshape(bs, WINDOW, KV_WIDTH)
    cv_ = cache_v[0].reshape(bs, WINDOW, KV_WIDTH)
    att_s, nk_s, nv_s = _attn_sample(qs, kfs, vfs, ck, cv_, sinks, batch=bs, ts=ss)

    cv_p = _conv_prompt(ap, conv_w[0], vec(conv_b[0]), vec(conv_ln_g[0]), vec(conv_ln_b[0]), batch=bp, seq=sp)
    cv_s = _conv_sample(as_, state_conv[0], conv_w[0], vec(conv_b[0]), vec(conv_ln_g[0]), vec(conv_ln_b[0]),
                        batch=bs, ts=ss)

    n_exp = router_w.shape[2]
    assert n_exp == N_EXPERTS
    rwt_bf = router_w[0].T.astype(BF16)
    rbt = jnp.broadcast_to(router_b[0][:, None], (n_exp, LANES))
    zero_carry = jnp.zeros((n_exp, LANES), F32)
    x1p, hnp, route_p, rt_p, cc_p, cnt_p = _outproj_router(xp2, att_p, cv_p, w_out_bf, g_ffn, rwt_bf, rbt,
                                                        zero_carry)
    x1s, hns, route_s, rt_s, cc_s, cnt = _outproj_router(xs2, att_s, cv_s, w_out_bf, g_ffn, rwt_bf, rbt, cnt_p)

    tme = EXPERT_BLOCK_ROWS
    nb = -(-(n_tok * TOP_K + N_EXPERTS * (tme - 1)) // tme) + 1
    blk_start, blk_exp, blk_src, blk_nvalid, blk_next, n_used, fill_blocks, n_fill = _routing_tables(
        cnt, tme=tme, nb=nb)
    dest_p = _dest_rows(rt_p, blk_start, tme)
    dest_s = _dest_rows(rt_s, blk_start, tme)
    xs_sorted = _dispatch(hnp, hns, jnp.concatenate([dest_p, dest_s], axis=1), fill_blocks, n_fill, nb=nb, tme=tme)
    ys = _experts(xs_sorted, w1[0], b1[0], w2[0], b2[0], blk_exp, blk_src, blk_nvalid, blk_next, n_used, tme=tme)
    g_fin = final_norm_g.reshape(1, d)
    tables, bases = _combine_tables(jnp.concatenate([cc_p, cc_s], axis=0), blk_start, tme)
    y_p = _combine(x1p, route_p, tables, bases, ys, g_fin, chunk0=0)
    y_s = _combine(x1s, route_s, tables, bases, ys, g_fin, chunk0=np_ // COMBINE_CHUNK)

    kv5 = lambda t, bb: t.reshape(bb, -1, KV_WIDTH)[:, -WINDOW:].reshape(bb, WINDOW, N_KV_HEADS, HEAD_DIM)
    new_k_p = kv5(kfp, bp)[None]
    new_v_p = kv5(vfp, bp)[None]
    ctx = CONV_K - 1
    new_c_p = ap.reshape(bp, sp, cw)[:, -ctx:][None]
    new_c_s = jnp.concatenate([state_conv[0], as_.reshape(bs, ss, cw)], axis=1)[:, -ctx:][None]
    return (y_p.reshape(bp, sp, d), y_s.reshape(bs, ss, d), new_k_p, new_v_p, new_c_p,
            kv5(nk_s, bs)[None], kv5(nv_s, bs)[None], new_c_s)
```

```python
import functools

import jax
import jax.numpy as jnp
from jax import lax
from jax.experimental import pallas as pl
from jax.experimental.pallas import tpu as pltpu

F32 = jnp.float32
BF16 = jnp.bfloat16

HEAD_DIM = 64
N_Q_HEADS = 8
N_KV_HEADS = 2
WINDOW = 128
ROPE_THETA = 500000.0
ROPE_DIM = 16
CONV_K = 31
N_EXPERTS = 32
TOP_K = 4
SWIGLU_LIMIT = 7.0
SWIGLU_ALPHA = 1.702
RMS_EPS = 1e-5
LN_EPS = 1e-5
PAST_LEN = 16384

LANES = 128
SUBLANES = 8
CONV_HALO = 32
VMEM_LIMIT = 56 * 1024 * 1024
EXPERT_BLOCK_ROWS = 512

ATTN_WIDTH = N_Q_HEADS * HEAD_DIM
KV_WIDTH = N_KV_HEADS * HEAD_DIM


def _tile(n, pref, mult=8):
    t = min(pref, n)
    while t > 0 and (n % t or t % mult):
        t -= 1
    assert t > 0, (n, pref, mult)
    return t


def _cparams(sem):
    return pltpu.CompilerParams(dimension_semantics=sem, vmem_limit_bytes=VMEM_LIMIT)


def _sigmoid(x):
    return 1.0 / (1.0 + jnp.exp(-x))


def _rope_tables(pos):
    half = ROPE_DIM // 2
    inv_freq = jnp.power(jnp.float32(ROPE_THETA), -jnp.arange(half, dtype=F32) * 2.0 / ROPE_DIM)
    ang = pos.astype(F32)[:, None] * inv_freq[None, :]
    cos, sin = jnp.cos(ang), jnp.sin(ang)
    l64 = jnp.arange(LANES) % HEAD_DIM
    f = l64 % half
    cos_l, sin_l = cos[:, f], sin[:, f]
    c = jnp.where(l64 < ROPE_DIM, cos_l, 1.0)
    s1 = jnp.where(l64 < half, -sin_l, 0.0)
    s2 = jnp.where((l64 >= half) & (l64 < ROPE_DIM), sin_l, 0.0)
    return c.astype(F32), s1.astype(F32), s2.astype(F32)


def _inproj_kernel(x_ref, g_ref, w_ref, c_ref, s1_ref, s2_ref,
                   q_ref, k_ref, v_ref, kf_ref, vf_ref, a_ref, *, conv_width):
    x = x_ref[...]
    ms = jnp.mean(x * x, axis=-1, keepdims=True)
    h = (x * lax.rsqrt(ms + RMS_EPS) * g_ref[...]).astype(BF16)
    z = jnp.dot(h, w_ref[...], preferred_element_type=F32)
    c, s1, s2 = c_ref[...], s1_ref[...], s2_ref[...]
    half = ROPE_DIM // 2

    def rot(t):
        return t * c + pltpu.roll(t, LANES - half, 1) * s1 + pltpu.roll(t, half, 1) * s2

    scale = HEAD_DIM ** -0.5
    for j in range(ATTN_WIDTH // LANES):
        q_ref[:, j * LANES:(j + 1) * LANES] = (rot(z[:, j * LANES:(j + 1) * LANES]) * scale).astype(q_ref.dtype)
    k0 = ATTN_WIDTH
    kr = rot(z[:, k0:k0 + KV_WIDTH])
    k_ref[...] = kr.astype(BF16)
    kf_ref[...] = kr
    v0 = k0 + KV_WIDTH
    vv = z[:, v0:v0 + KV_WIDTH]
    v_ref[...] = vv.astype(BF16)
    vf_ref[...] = vv
    u0 = v0 + KV_WIDTH
    g0 = u0 + conv_width
    a_ref[...] = z[:, u0:g0] * _sigmoid(z[:, g0:g0 + conv_width])


def _in_proj(x2, g, w_bf, tables, *, seq_period, q_dtype, conv_width):
    n, d = x2.shape
    in_w = w_bf.shape[1]
    if seq_period is None:
        tm = tables[0].shape[0]
        tmap = lambda i: (0, 0)
    else:
        tm = _tile(seq_period, 1024, 16)
        per = seq_period // tm
        tmap = lambda i: (i % per, 0)
    assert n % tm == 0
    row = lambda w: pl.BlockSpec((tm, w), lambda i: (i, 0))
    tab = pl.BlockSpec((tm, LANES), tmap)
    return pl.pallas_call(
        functools.partial(_inproj_kernel, conv_width=conv_width),
        grid=(n // tm,),
        in_specs=[row(d), pl.BlockSpec((1, d), lambda i: (0, 0)),
                  pl.BlockSpec((d, in_w), lambda i: (0, 0)), tab, tab, tab],
        out_specs=[row(ATTN_WIDTH), row(KV_WIDTH), row(KV_WIDTH), row(KV_WIDTH), row(KV_WIDTH),
                   row(conv_width)],
        out_shape=[jax.ShapeDtypeStruct((n, ATTN_WIDTH), q_dtype),
                   jax.ShapeDtypeStruct((n, KV_WIDTH), BF16),
                   jax.ShapeDtypeStruct((n, KV_WIDTH), BF16),
                   jax.ShapeDtypeStruct((n, KV_WIDTH), F32),
                   jax.ShapeDtypeStruct((n, KV_WIDTH), F32),
                   jax.ShapeDtypeStruct((n, conv_width), F32)],
        compiler_params=_cparams(("parallel",)),
        name="in_proj",
    )(x2, g, w_bf, *tables)


def _dup_head(t, h):
    sw = pltpu.roll(t, HEAD_DIM, 1)
    low = lax.broadcasted_iota(jnp.int32, t.shape, 1) < HEAD_DIM
    return jnp.where(low, t, sw) if h == 0 else jnp.where(low, sw, t)


def _nt_dot(a, b):
    return lax.dot_general(a, b, (((1,), (1,)), ((), ())), preferred_element_type=F32)


def _attn_prompt_kernel(sink_ref, q_ref, kc_ref, kp_ref, vc_ref, vp_ref, o_ref, *, qb):
    j = pl.program_id(1)
    w = WINDOW
    k_all = jnp.concatenate([kp_ref[...], kc_ref[...]], axis=0).astype(F32)
    v_all = jnp.concatenate([vp_ref[...], vc_ref[...]], axis=0).astype(F32)
    r = lax.broadcasted_iota(jnp.int32, (w, 2 * w), 0)
    kk = lax.broadcasted_iota(jnp.int32, (w, 2 * w), 1)
    band = (kk > r) & (kk <= r + w)
    low = lax.broadcasted_iota(jnp.int32, (w, LANES), 1) < HEAD_DIM
    zero = jnp.zeros((w, LANES), BF16)
    group = N_Q_HEADS // N_KV_HEADS
    for h in range(N_KV_HEADS):
        kd_all = _dup_head(k_all, h).astype(BF16)
        vd_all = _dup_head(v_all, h).astype(BF16)
        for sub in range(qb):
            valid = band & ((kk >= w) | (j > 0)) if sub == 0 else band
            kd = kd_all[sub * w:(sub + 2) * w, :]
            vd = vd_all[sub * w:(sub + 2) * w, :]
            for jj in range(group // 2):
                col = (h * group // 2 + jj) * LANES
                qv = q_ref[sub * w:(sub + 1) * w, col:col + LANES]
                halves = []
                for half in range(2):
                    head = h * group + jj * 2 + half
                    qm = jnp.where(low if half == 0 else ~low, qv, zero)
                    s = jnp.where(valid, _nt_dot(qm, kd), -jnp.inf)
                    sink = sink_ref[head]
                    m = jnp.maximum(jnp.max(s, axis=-1, keepdims=True), sink)
                    p = jnp.exp(s - m)
                    den = jnp.sum(p, axis=-1, keepdims=True) + jnp.exp(sink - m)
                    o = jnp.dot(p.astype(BF16), vd, preferred_element_type=F32)
                    halves.append(o / den)
                o_ref[sub * w:(sub + 1) * w, col:col + LANES] = (
                    jnp.where(low, halves[0], halves[1]).astype(o_ref.dtype))


def _attn_prompt(q, k, v, sinks, *, batch, seq):
    nb = seq // WINDOW
    qb = next(c for c in (4, 2, 1) if nb % c == 0)
    steps = nb // qb
    cur = lambda b, j: (b * steps + j, 0)
    prev = lambda b, j: (b * nb + jnp.maximum(j * qb - 1, 0), 0)
    return pl.pallas_call(
        functools.partial(_attn_prompt_kernel, qb=qb),
        grid=(batch, steps),
        in_specs=[pl.BlockSpec(memory_space=pltpu.SMEM),
                  pl.BlockSpec((qb * WINDOW, ATTN_WIDTH), cur),
                  pl.BlockSpec((qb * WINDOW, KV_WIDTH), cur), pl.BlockSpec((WINDOW, KV_WIDTH), prev),
                  pl.BlockSpec((qb * WINDOW, KV_WIDTH), cur), pl.BlockSpec((WINDOW, KV_WIDTH), prev)],
        out_specs=pl.BlockSpec((qb * WINDOW, ATTN_WIDTH), cur),
        out_shape=jax.ShapeDtypeStruct((batch * seq, ATTN_WIDTH), BF16),
        compiler_params=_cparams(("parallel", "parallel")),
        name="attn_prompt",
    )(sinks, q, k, k, v, v)


def _attn_sample_kernel(sink_ref, q_ref, kn_ref, vn_ref, ck_ref, cv_ref, o_ref, nk_ref, nv_ref, *, gb, ts):
    w = WINDOW
    group = N_Q_HEADS // N_KV_HEADS
    rows = group * ts
    low = lax.broadcasted_iota(jnp.int32, (ts, LANES), 1) < HEAD_DIM
    pad = jnp.zeros((ts, LANES), F32)
    s_c, s_n, v_dup = [], [], []
    for b in range(gb):
        kc, vc = ck_ref[b], cv_ref[b]
        kn, vn = kn_ref[b * ts:(b + 1) * ts, :], vn_ref[b * ts:(b + 1) * ts, :]
        nk_ref[b, 0:w - ts, :] = kc[ts:, :]
        nk_ref[b, w - ts:, :] = kn
        nv_ref[b, 0:w - ts, :] = vc[ts:, :]
        nv_ref[b, w - ts:, :] = vn
        knp = jnp.concatenate([kn, pad], axis=0)
        vnp = jnp.concatenate([vn, pad], axis=0)
        qb = q_ref[b * ts:(b + 1) * ts, :]
        for h in range(N_KV_HEADS):
            parts = []
            for jj in range(group // 2):
                col = (h * group // 2 + jj) * LANES
                qv = qb[:, col:col + LANES]
                parts += [jnp.where(low, qv, 0.0), jnp.where(low, 0.0, qv)]
            lhs = jnp.concatenate(parts, axis=0).astype(BF16)
            s_c.append(_nt_dot(lhs, _dup_head(kc, h).astype(BF16)))
            s_n.append(_nt_dot(lhs, _dup_head(knp, h).astype(BF16)))
            v_dup.append((_dup_head(vc, h).astype(BF16), _dup_head(vnp, h).astype(BF16)))
    s_c = jnp.concatenate(s_c, axis=0)
    s_n = jnp.concatenate(s_n, axis=0)
    n_rows = s_c.shape[0]
    ridx = lax.broadcasted_iota(jnp.int32, (n_rows, 1), 0)
    t_row = ridx % ts
    head_row = (ridx // ts) % N_Q_HEADS
    sink = jnp.zeros((n_rows, 1), F32)
    for hd in range(N_Q_HEADS):
        sink = jnp.where(head_row == hd, sink_ref[hd], sink)
    c_idx = lax.broadcasted_iota(jnp.int32, (n_rows, w), 1)
    n_idx = lax.broadcasted_iota(jnp.int32, (n_rows, 2 * ts), 1)
    s_c = jnp.where(c_idx > t_row, s_c, -jnp.inf)
    s_n = jnp.where(n_idx <= t_row, s_n, -jnp.inf)
    m = jnp.maximum(jnp.maximum(jnp.max(s_c, axis=-1, keepdims=True), jnp.max(s_n, axis=-1, keepdims=True)), sink)
    p_c = jnp.exp(s_c - m)
    p_n = jnp.exp(s_n - m)
    den = jnp.sum(p_c, axis=-1, keepdims=True) + jnp.sum(p_n, axis=-1, keepdims=True) + jnp.exp(sink - m)
    p_c = p_c.astype(BF16)
    p_n = p_n.astype(BF16)
    outs = []
    for b in range(gb):
        cols = []
        for h in range(N_KV_HEADS):
            ci = b * N_KV_HEADS + h
            sl = slice(ci * rows, (ci + 1) * rows)
            vdc, vdn = v_dup[ci]
            o = (jnp.dot(p_c[sl], vdc, preferred_element_type=F32)
                 + jnp.dot(p_n[sl], vdn, preferred_element_type=F32)) / den[sl]
            for jj in range(group // 2):
                lo_part = o[(2 * jj) * ts:(2 * jj + 1) * ts, :]
                hi_part = o[(2 * jj + 1) * ts:(2 * jj + 2) * ts, :]
                cols.append(jnp.where(low, lo_part, hi_part))
        outs.append(jnp.concatenate(cols, axis=1))
    o_ref[...] = jnp.concatenate(outs, axis=0).astype(o_ref.dtype)


def _attn_sample(q, kf, vf, cache_k, cache_v, sinks, *, batch, ts):
    assert ts % 8 == 0 and ts <= WINDOW
    gb = _tile(batch, 8, 2)
    tok = lambda w: pl.BlockSpec((gb * ts, w), lambda i: (i, 0))
    cache = pl.BlockSpec((gb, WINDOW, KV_WIDTH), lambda i: (i, 0, 0))
    cshape = jax.ShapeDtypeStruct((batch, WINDOW, KV_WIDTH), F32)
    return pl.pallas_call(
        functools.partial(_attn_sample_kernel, gb=gb, ts=ts),
        grid=(batch // gb,),
        in_specs=[pl.BlockSpec(memory_space=pltpu.SMEM), tok(ATTN_WIDTH), tok(KV_WIDTH), tok(KV_WIDTH),
                  cache, cache],
        out_specs=[tok(ATTN_WIDTH), cache, cache],
        out_shape=[jax.ShapeDtypeStruct((batch * ts, ATTN_WIDTH), BF16), cshape, cshape],
        compiler_params=_cparams(("parallel",)),
        name="attn_sample",
    )(sinks, q, kf, vf, cache_k, cache_v)


def _ln_swish(acc, b, lg, lb):
    y = acc + b
    mu = jnp.mean(y, axis=-1, keepdims=True)
    yc = y - mu
    var = jnp.mean(yc * yc, axis=-1, keepdims=True)
    yn = yc * lax.rsqrt(var + LN_EPS) * lg + lb
    return yn * _sigmoid(yn)


def _conv_prompt_kernel(a_ref, ap_ref, w_ref, b_ref, lg_ref, lb_ref, o_ref, win_ref, *, tt, rc):
    j = pl.program_id(1)
    n = CONV_HALO + tt
    win = jnp.concatenate([jnp.where(j > 0, ap_ref[...], 0.0), a_ref[...]], axis=0)
    win_ref[0] = win
    for r in range(1, SUBLANES):
        win_ref[r] = pltpu.roll(win, n - r, 0)
    off = CONV_HALO - (CONV_K - 1)
    b, lg, lb = b_ref[...], lg_ref[...], lb_ref[...]
    for c in range(tt // rc):
        acc = jnp.zeros((rc, a_ref.shape[1]), F32)
        for k in range(CONV_K):
            s = off + k
            base = c * rc + (s // SUBLANES) * SUBLANES
            wk = jnp.concatenate([w_ref[k]] * (rc // SUBLANES), axis=0)
            acc = acc + wk * win_ref[s % SUBLANES, base:base + rc, :]
        o_ref[c * rc:(c + 1) * rc, :] = _ln_swish(acc, b, lg, lb).astype(o_ref.dtype)


def _conv_prompt(a, w, b, lg, lb, *, batch, seq):
    cw = a.shape[1]
    tt = _tile(seq, 512, CONV_HALO)
    rc = _tile(tt, 32, 16)
    nt = seq // tt
    per = tt // CONV_HALO
    cur = lambda bb, j: (bb * nt + j, 0)
    prev = lambda bb, j: (jnp.maximum((bb * nt + j) * per - 1, 0), 0)
    vec = pl.BlockSpec((1, cw), lambda bb, j: (0, 0))
    return pl.pallas_call(
        functools.partial(_conv_prompt_kernel, tt=tt, rc=rc),
        grid=(batch, nt),
        in_specs=[pl.BlockSpec((tt, cw), cur), pl.BlockSpec((CONV_HALO, cw), prev),
                  pl.BlockSpec((CONV_K, SUBLANES, cw), lambda bb, j: (0, 0, 0)), vec, vec, vec],
        out_specs=pl.BlockSpec((tt, cw), cur),
        out_shape=jax.ShapeDtypeStruct((batch * seq, cw), BF16),
        scratch_shapes=[pltpu.VMEM((SUBLANES, CONV_HALO + tt, cw), F32)],
        compiler_params=_cparams(("parallel", "parallel")),
        name="conv_prompt",
    )(a, a, jnp.broadcast_to(w[:, None, :], (CONV_K, SUBLANES, cw)), b, lg, lb)


def _conv_sample_kernel(a_ref, st_ref, w_ref, b_ref, lg_ref, lb_ref, o_ref, win_ref, *, gb, ts):
    ctx = CONV_K - 1
    b, lg, lb = b_ref[...], lg_ref[...], lb_ref[...]
    for bb in range(gb):
        win_ref[bb, 0:ctx, :] = st_ref[bb]
        win_ref[bb, ctx:ctx + ts, :] = a_ref[bb * ts:(bb + 1) * ts, :]
    outs = []
    for bb in range(gb):
        acc = jnp.zeros((ts, a_ref.shape[1]), F32)
        for k in range(CONV_K):
            wk = jnp.concatenate([w_ref[k]] * (ts // SUBLANES), axis=0)
            acc = acc + wk * win_ref[bb, k:k + ts, :]
        outs.append(_ln_swish(acc, b, lg, lb))
    o_ref[...] = jnp.concatenate(outs, axis=0).astype(o_ref.dtype)


def _conv_sample(a, state, w, b, lg, lb, *, batch, ts):
    cw = a.shape[1]
    ctx = CONV_K - 1
    gb = _tile(batch, 8, 2)
    vec = pl.BlockSpec((1, cw), lambda i: (0, 0))
    return pl.pallas_call(
        functools.partial(_conv_sample_kernel, gb=gb, ts=ts),
        grid=(batch // gb,),
        in_specs=[pl.BlockSpec((gb * ts, cw), lambda i: (i, 0)),
                  pl.BlockSpec((gb, ctx, cw), lambda i: (i, 0, 0)),
                  pl.BlockSpec((CONV_K, SUBLANES, cw), lambda i: (0, 0, 0)), vec, vec, vec],
        out_specs=pl.BlockSpec((gb * ts, cw), lambda i: (i, 0)),
        out_shape=jax.ShapeDtypeStruct((batch * ts, cw), BF16),
        scratch_shapes=[pltpu.VMEM((gb, ctx + ts + 2, cw), F32)],
        compiler_params=_cparams(("parallel",)),
        name="conv_sample",
    )(a, state, jnp.broadcast_to(w[:, None, :], (CONV_K, SUBLANES, cw)), b, lg, lb)


def _outproj_router_kernel(x_ref, att_ref, cv_ref, wo_ref, g_ref, rw_ref, rb_ref, cin_ref,
                           x1_ref, hn_ref, route_ref, rt_ref, cnt_ref, carry_ref, *, ts):
    i = pl.program_id(0)

    @pl.when(i == 0)
    def _():
        carry_ref[...] = cin_ref[...]

    aw = att_ref.shape[1]
    n_exp = rw_ref.shape[0]
    r = lax.broadcasted_iota(jnp.int32, (ts, ts), 0)
    c = lax.broadcasted_iota(jnp.int32, (ts, ts), 1)
    before = (r < c).astype(BF16)
    eid = lax.broadcasted_iota(jnp.int32, (n_exp, ts), 0)
    carry = carry_ref[...][:, 0:1]
    for s in range(x_ref.shape[0] // ts):
        rs = slice(s * ts, (s + 1) * ts)
        mix = (jnp.dot(att_ref[rs, :], wo_ref[0:aw, :], preferred_element_type=F32)
               + jnp.dot(cv_ref[rs, :], wo_ref[aw:, :], preferred_element_type=F32))
        x1 = x_ref[rs, :] + mix
        x1_ref[rs, :] = x1
        ms = jnp.mean(x1 * x1, axis=-1, keepdims=True)
        hn = x1 * lax.rsqrt(ms + RMS_EPS) * g_ref[...]
        _store_slabs(hn_ref.at[pl.ds(s * ts * SUBLANES, ts * SUBLANES)], hn, ts)
        logits = _nt_dot(rw_ref[...], hn.astype(BF16)) + rb_ref[...][:, 0:1]
        onehot = jnp.zeros((n_exp, ts), F32)
        vals, idxs = [], []
        for _ in range(TOP_K):
            m = jnp.max(logits, axis=0, keepdims=True)
            idx = jnp.min(jnp.where(logits == m, eid, n_exp), axis=0, keepdims=True)
            sel = eid == idx
            onehot = onehot + sel.astype(F32)
            logits = jnp.where(sel, -jnp.inf, logits)
            vals.append(m)
            idxs.append(idx)
        es = [jnp.exp(v - vals[0]) for v in vals]
        den = es[0] + es[1] + es[2] + es[3]
        prefix = jnp.dot(onehot.astype(BF16), before, preferred_element_type=F32) + carry
        ranks = [jnp.sum(jnp.where(eid == idxs[k], prefix, 0.0), axis=0, keepdims=True) for k in range(TOP_K)]
        rows = [ix.astype(F32) for ix in idxs] + ranks
        rt_ref[:, rs] = jnp.concatenate(rows, axis=0)
        fields = jnp.concatenate(rows + [e / den for e in es] + [jnp.zeros((LANES - 3 * TOP_K, ts), F32)],
                                 axis=0)
        route_ref[rs, :] = jnp.transpose(fields)
        carry = carry + jnp.sum(onehot, axis=1, keepdims=True)
    carry_ref[...] = jnp.broadcast_to(carry, carry_ref.shape)
    cnt_ref[...] = jnp.broadcast_to(carry, cnt_ref.shape)


def _store_slabs(ref, val, rows):
    for j in range(val.shape[1] // LANES):
        ref[pl.ds(j, rows, stride=SUBLANES), :] = val[:, j * LANES:(j + 1) * LANES]


def _load_slabs(ref, rows, dtype):
    return jnp.concatenate([ref[pl.ds(j, rows, stride=SUBLANES), :].astype(dtype) for j in range(SUBLANES)],
                           axis=1)


def _outproj_router(x2, att, cv, wo_bf, g, rwt_bf, rbt, carry_in):
    n, d = x2.shape
    assert d == SUBLANES * LANES
    n_exp = rwt_bf.shape[0]
    tm = _tile(n, 512, 16)
    ts = _tile(tm, 512, 16)
    row = lambda w: pl.BlockSpec((tm, w), lambda i: (i, 0))
    full = lambda a: pl.BlockSpec(a.shape, lambda i: (0,) * a.ndim)
    return pl.pallas_call(
        functools.partial(_outproj_router_kernel, ts=ts),
        grid=(n // tm,),
        in_specs=[row(d), row(att.shape[1]), row(cv.shape[1]), full(wo_bf), full(g), full(rwt_bf), full(rbt),
                  full(carry_in)],
        out_specs=[row(d), pl.BlockSpec((tm * SUBLANES, LANES), lambda i: (i, 0)), row(LANES),
                   pl.BlockSpec((2 * TOP_K, tm), lambda i: (0, i)),
                   pl.BlockSpec((n_exp, LANES), lambda i: (0, 0))],
        out_shape=[jax.ShapeDtypeStruct((n, d), F32), jax.ShapeDtypeStruct((n * SUBLANES, LANES), F32),
                   jax.ShapeDtypeStruct((n, LANES), F32), jax.ShapeDtypeStruct((2 * TOP_K, n), F32),
                   jax.ShapeDtypeStruct((n_exp, LANES), F32)],
        scratch_shapes=[pltpu.VMEM((n_exp, LANES), F32)],
        compiler_params=_cparams(("arbitrary",)),
        name="outproj_router",
    )(x2, att, cv, wo_bf, g, rwt_bf, rbt, carry_in)


def _dispatch_kernel(fill_ref, nfill_ref, dest_ref, hn_ref, hn2_ref, xs_ref, zero_ref, sem, zsem,
                     *, td, tme, n_first):
    @pl.when(pl.program_id(0) == 0)
    def _():
        zero_ref[...] = jnp.zeros(zero_ref.shape, F32)

        def fill(f):
            row = pl.multiple_of(fill_ref[f] * (tme * SUBLANES), tme * SUBLANES)
            return pltpu.make_async_copy(zero_ref, xs_ref.at[pl.ds(row, tme * SUBLANES)], zsem)

        def start(f, carry):
            fill(f).start()
            return carry

        def wait(f, carry):
            fill(f).wait()
            return carry

        lax.fori_loop(0, nfill_ref[0], start, 0)
        lax.fori_loop(0, nfill_ref[0], wait, 0)

    def scatter_rows(src_ref):
        def issue(r, carry):
            src = src_ref.at[pl.ds(pl.multiple_of(r * SUBLANES, SUBLANES), SUBLANES)]
            for k in range(TOP_K):
                d = pl.multiple_of(dest_ref[0, 0, k * td + r] * SUBLANES, SUBLANES)
                pltpu.make_async_copy(src, xs_ref.at[pl.ds(d, SUBLANES)], sem).start(priority=k % 2)
            return carry

        lax.fori_loop(0, td, issue, 0, unroll=8)
        for _ in range(TOP_K):
            pltpu.make_async_copy(src_ref, xs_ref.at[pl.ds(0, td * SUBLANES)], sem).wait()

    @pl.when(pl.program_id(0) < n_first)
    def _():
        scatter_rows(hn_ref)

    @pl.when(pl.program_id(0) >= n_first)
    def _():
        scatter_rows(hn2_ref)


def _dest_blocks(dest, td):
    n = dest.shape[1]
    return dest.reshape(TOP_K, n // td, td).transpose(1, 0, 2).reshape(n // td, 1, TOP_K * td)


def _dispatch(hn_a, hn_b, dest, fill_blocks, n_fill, *, nb, tme):
    na = hn_a.shape[0] // SUBLANES
    nb_rows = hn_b.shape[0] // SUBLANES
    td = _tile(nb_rows, 256, 8)
    assert na % td == 0
    n_first = na // td
    steps = n_first + nb_rows // td
    slab = lambda m: pl.BlockSpec((td * SUBLANES, LANES), m)
    grid_spec = pltpu.PrefetchScalarGridSpec(
        num_scalar_prefetch=2,
        grid=(steps,),
        in_specs=[pl.BlockSpec((1, 1, td * TOP_K), lambda i, fb, nf: (i, 0, 0), memory_space=pltpu.SMEM),
                  slab(lambda i, fb, nf: (jnp.minimum(i, n_first - 1), 0)),
                  slab(lambda i, fb, nf: (jnp.maximum(i - n_first, 0), 0))],
        out_specs=pl.BlockSpec(memory_space=pl.ANY),
        scratch_shapes=[pltpu.VMEM((tme * SUBLANES, LANES), F32), pltpu.SemaphoreType.DMA(()),
                        pltpu.SemaphoreType.DMA(())],
    )
    return pl.pallas_call(
        functools.partial(_dispatch_kernel, td=td, tme=tme, n_first=n_first),
        grid_spec=grid_spec,
        out_shape=jax.ShapeDtypeStruct((nb * tme * SUBLANES, LANES), F32),
        compiler_params=_cparams(("arbitrary",)),
        name="dispatch",
    )(fill_blocks, n_fill, _dest_blocks(dest, td), hn_a, hn_b)


def _experts_kernel(be_ref, bsrc_ref, nv_ref, nx_ref, nu_ref, x_ref, w1_hbm, b1_ref, w2_hbm, b2_ref,
                    y_ref, w1f_ref, w2f_ref, w1b_ref, w2b_ref, par_ref, wsem, *, tme):
    i = pl.program_id(0)
    e = be_ref[i]
    e_prev = be_ref[jnp.maximum(i - 1, 0)]
    d_ff = w2f_ref.shape[1]
    half = tme // 2

    def fetch(expert, s):
        return (pltpu.make_async_copy(w1_hbm.at[expert], w1f_ref.at[s], wsem.at[0, s]),
                pltpu.make_async_copy(w2_hbm.at[expert], w2f_ref.at[s], wsem.at[1, s]))

    @pl.when(i == 0)
    def _():
        par_ref[0] = 0
        for c in fetch(e, 0):
            c.start()

    @pl.when((i == 0) | (e != e_prev))
    def _():
        s = par_ref[0]
        for c in fetch(e, s):
            c.wait()
        w1b_ref[...] = w1f_ref[s].astype(BF16)
        w2b_ref[...] = w2f_ref[s].astype(BF16)
        nxt = nx_ref[i]

        @pl.when(nxt >= 0)
        def _():
            for c in fetch(nxt, 1 - s):
                c.start()

        par_ref[0] = 1 - s

    def ffn(x):
        h = jnp.dot(x, w1b_ref[...], preferred_element_type=F32) + b1_ref[0]
        x_glu = jnp.minimum(h[:, :d_ff], SWIGLU_LIMIT)
        x_lin = jnp.clip(h[:, d_ff:], -SWIGLU_LIMIT, SWIGLU_LIMIT)
        act = x_glu * _sigmoid(SWIGLU_ALPHA * x_glu) * (x_lin + 1.0)
        return jnp.dot(act.astype(BF16), w2b_ref[...], preferred_element_type=F32) + b2_ref[0]

    used = i < nu_ref[0]
    nv = nv_ref[i]

    @pl.when(used & (nv > half))
    def _():
        _store_slabs(y_ref, ffn(_load_slabs(x_ref, tme, BF16)), tme)

    @pl.when(used & (nv <= half))
    def _():
        rows = half * SUBLANES
        _store_slabs(y_ref.at[pl.ds(0, rows)], ffn(_load_slabs(x_ref.at[pl.ds(0, rows)], half, BF16)), half)
        y_ref[pl.ds(rows, rows), :] = jnp.zeros((rows, LANES), F32)

    @pl.when(jnp.logical_not(used))
    def _():
        y_ref[...] = jnp.zeros(y_ref.shape, F32)


def _experts(xs, w1, b1, w2, b2, blk_exp, blk_src, blk_nvalid, blk_next, n_used, *, tme):
    n_exp, d, h2 = w1.shape
    d_ff = w2.shape[1]
    nb = xs.shape[0] // (tme * SUBLANES)
    slab = lambda m: pl.BlockSpec((tme * SUBLANES, LANES), m)
    grid_spec = pltpu.PrefetchScalarGridSpec(
        num_scalar_prefetch=5,
        grid=(nb,),
        in_specs=[slab(lambda i, be, bs, nv, nx, nu: (bs[i], 0)),
                  pl.BlockSpec(memory_space=pl.ANY),
                  pl.BlockSpec((1, 1, h2), lambda i, be, bs, nv, nx, nu: (be[i], 0, 0)),
                  pl.BlockSpec(memory_space=pl.ANY),
                  pl.BlockSpec((1, 1, d), lambda i, be, bs, nv, nx, nu: (be[i], 0, 0))],
        out_specs=slab(lambda i, be, bs, nv, nx, nu: (i, 0)),
        scratch_shapes=[pltpu.VMEM((2, d, h2), F32), pltpu.VMEM((2, d_ff, d), F32),
                        pltpu.VMEM((d, h2), BF16), pltpu.VMEM((d_ff, d), BF16),
                        pltpu.SMEM((1,), jnp.int32), pltpu.SemaphoreType.DMA((2, 2))],
    )
    return pl.pallas_call(
        functools.partial(_experts_kernel, tme=tme),
        grid_spec=grid_spec,
        out_shape=jax.ShapeDtypeStruct(xs.shape, F32),
        compiler_params=_cparams(("arbitrary",)),
        name="experts",
    )(blk_exp, blk_src, blk_nvalid, blk_next, n_used, xs, w1, b1.reshape(n_exp, 1, h2), w2,
      b2.reshape(n_exp, 1, d))


def _combine_kernel(dcur_ref, dnext_ref, x1_ref, route_ref, g_ref, ys_ref, o_ref, buf_ref, sem, *, tc):
    i = pl.program_id(0)
    slot = i % 2

    def gather(dref, s):
        def issue(r, carry):
            row = pl.multiple_of(r * SUBLANES, SUBLANES)
            for k in range(TOP_K):
                d = pl.multiple_of(dref[0, 0, k * tc + r] * SUBLANES, SUBLANES)
                pltpu.make_async_copy(ys_ref.at[pl.ds(d, SUBLANES)], buf_ref.at[s, k, pl.ds(row, SUBLANES)],
                                      sem.at[s]).start(priority=k % 2)
            return carry

        lax.fori_loop(0, tc, issue, 0, unroll=8)

    @pl.when(i == 0)
    def _():
        gather(dcur_ref, 0)

    @pl.when(i + 1 < pl.num_programs(0))
    def _():
        gather(dnext_ref, 1 - slot)

    for k in range(TOP_K):
        pltpu.make_async_copy(ys_ref.at[pl.ds(0, tc * SUBLANES)], buf_ref.at[slot, k], sem.at[slot]).wait()
    route = route_ref[...]
    x1 = x1_ref[...]
    chunks = []
    for j in range(SUBLANES):
        acc = x1[:, j * LANES:(j + 1) * LANES]
        for k in range(TOP_K):
            acc = acc + (route[:, 2 * TOP_K + k:2 * TOP_K + k + 1]
                         * buf_ref[slot, k, pl.ds(j, tc, stride=SUBLANES), :])
        chunks.append(acc)
    y = jnp.concatenate(chunks, axis=1)
    ms = jnp.mean(y * y, axis=-1, keepdims=True)
    o_ref[...] = y * lax.rsqrt(ms + RMS_EPS) * g_ref[...]


def _combine(x1, route, dest, ys, g):
    n, d = x1.shape
    tc = _tile(n, 128, 8)
    steps = n // tc
    dest3 = _dest_blocks(dest, tc)
    dspec = lambda m: pl.BlockSpec((1, 1, tc * TOP_K), m, memory_space=pltpu.SMEM)
    return pl.pallas_call(
        functools.partial(_combine_kernel, tc=tc),
        grid=(steps,),
        in_specs=[dspec(lambda i: (i, 0, 0)), dspec(lambda i: (jnp.minimum(i + 1, steps - 1), 0, 0)),
                  pl.BlockSpec((tc, d), lambda i: (i, 0)),
                  pl.BlockSpec((tc, LANES), lambda i: (i, 0)),
                  pl.BlockSpec((1, d), lambda i: (0, 0)),
                  pl.BlockSpec(memory_space=pl.ANY)],
        out_specs=pl.BlockSpec((tc, d), lambda i: (i, 0)),
        out_shape=jax.ShapeDtypeStruct((n, d), F32),
        scratch_shapes=[pltpu.VMEM((2, TOP_K, tc * SUBLANES, LANES), F32), pltpu.SemaphoreType.DMA((2,))],
        compiler_params=_cparams(("arbitrary",)),
        name="combine",
    )(dest3, dest3, x1, route, g, ys)


def _dest_rows(route_t, blk_start, tme):
    eidx = route_t[0:TOP_K].astype(jnp.int32)
    rank = route_t[TOP_K:2 * TOP_K].astype(jnp.int32)
    experts = jnp.arange(N_EXPERTS, dtype=jnp.int32)[:, None, None]
    first = jnp.sum(jnp.where(eidx[None] == experts, blk_start[:, None, None], 0), axis=0)
    return (first * tme + rank).astype(jnp.int32)


def _routing_tables(counts_f, *, tme, nb):
    counts = counts_f[:, 0].astype(jnp.int32)
    nblk = (counts + tme - 1) // tme
    blk_end = jnp.cumsum(nblk)
    blk_start = blk_end - nblk
    n_used = blk_end[-1]
    b = jnp.arange(nb, dtype=jnp.int32)
    used = b < n_used
    blk_exp = jnp.minimum(jnp.sum((b[:, None] >= blk_end[None, :]).astype(jnp.int32), axis=1), N_EXPERTS - 1)
    last_exp = jnp.max(jnp.where(nblk > 0, jnp.arange(N_EXPERTS, dtype=jnp.int32), 0))
    blk_exp = jnp.where(used, blk_exp, last_exp).astype(jnp.int32)
    blk_src = jnp.where(used, b, 0).astype(jnp.int32)
    experts = jnp.arange(N_EXPERTS, dtype=jnp.int32)
    mine = (b[:, None] >= blk_start[None, :]) & (b[:, None] < blk_end[None, :])
    nvalid = jnp.sum(jnp.where(mine, counts[None, :] - (b[:, None] - blk_start[None, :]) * tme, 0), axis=1)
    nvalid = jnp.clip(nvalid, 0, tme).astype(jnp.int32)
    later = (experts[None, :] > experts[:, None]) & (nblk[None, :] > 0)
    nxt_e = jnp.min(jnp.where(later, experts[None, :], N_EXPERTS), axis=1)
    nxt_e = jnp.where(nxt_e == N_EXPERTS, -1, nxt_e)
    blk_next = jnp.sum(jnp.where(blk_exp[:, None] == experts[None, :], nxt_e[None, :], 0), axis=1).astype(jnp.int32)
    has_tail = (nblk > 0) & (counts % tme != 0)
    partial = jnp.any((b[:, None] == blk_end[None, :] - 1) & has_tail[None, :], axis=1)
    needs_fill = partial | ~used
    fill_blocks = jnp.argsort(~needs_fill, stable=True).astype(jnp.int32)
    n_fill = jnp.sum(needs_fill).reshape(1).astype(jnp.int32)
    return (blk_start, blk_exp, blk_src, nvalid, blk_next, n_used.reshape(1).astype(jnp.int32), fill_blocks,
            n_fill)


def kernel(x_prompt, x_sample, cache_k, cache_v, state_conv, attn_norm_g, w_in, attn_sinks, conv_w, conv_b,
           conv_ln_g, conv_ln_b, w_out, ffn_norm_g, router_w, router_b, w1, b1, w2, b2, final_norm_g):
    depth = w_in.shape[0]
    assert depth == 1, "single-layer step"
    bp, sp, d = x_prompt.shape
    bs, ss, _ = x_sample.shape
    cw = conv_w.shape[2]
    np_, ns = bp * sp, bs * ss
    n_tok = np_ + ns
    assert sp % WINDOW == 0

    xp2 = x_prompt.reshape(np_, d)
    xs2 = x_sample.reshape(ns, d)
    w_in_bf = w_in[0].astype(BF16)
    w_out_bf = w_out[0].astype(BF16)
    g_attn = attn_norm_g[0].reshape(1, d)
    g_ffn = ffn_norm_g[0].reshape(1, d)
    sinks = attn_sinks[0]
    vec = lambda a: a.reshape(1, cw)

    tab_p = _rope_tables(jnp.arange(sp, dtype=jnp.int32))
    tms = _tile(ns, 512, max(ss, 16))
    tab_s = _rope_tables(PAST_LEN + (jnp.arange(tms, dtype=jnp.int32) % ss))
    qp, kp, vp, kfp, vfp, ap = _in_proj(xp2, g_attn, w_in_bf, tab_p, seq_period=sp, q_dtype=BF16, conv_width=cw)
    qs, _, _, kfs, vfs, as_ = _in_proj(xs2, g_attn, w_in_bf, tab_s, seq_period=None, q_dtype=F32, conv_width=cw)

    att_p = _attn_prompt(qp, kp, vp, sinks, batch=bp, seq=sp)
    ck = cache_k[0].reshape(bs, WINDOW, KV_WIDTH)
    cv_ = cache_v[0].reshape(bs, WINDOW, KV_WIDTH)
    att_s, nk_s, nv_s = _attn_sample(qs, kfs, vfs, ck, cv_, sinks, batch=bs, ts=ss)

    cv_p = _conv_prompt(ap, conv_w[0], vec(conv_b[0]), vec(conv_ln_g[0]), vec(conv_ln_b[0]), batch=bp, seq=sp)
    cv_s = _conv_sample(as_, state_conv[0], conv_w[0], vec(conv_b[0]), vec(conv_ln_g[0]), vec(conv_ln_b[0]),
                        batch=bs, ts=ss)

    n_exp = router_w.shape[2]
    assert n_exp == N_EXPERTS
    rwt_bf = router_w[0].T.astype(BF16)
    rbt = jnp.broadcast_to(router_b[0][:, None], (n_exp, LANES))
    zero_carry = jnp.zeros((n_exp, LANES), F32)
    x1p, hnp, route_p, rt_p, cnt_p = _outproj_router(xp2, att_p, cv_p, w_out_bf, g_ffn, rwt_bf, rbt, zero_carry)
    x1s, hns, route_s, rt_s, cnt = _outproj_router(xs2, att_s, cv_s, w_out_bf, g_ffn, rwt_bf, rbt, cnt_p)

    tme = EXPERT_BLOCK_ROWS
    nb = -(-(n_tok * TOP_K + N_EXPERTS * (tme - 1)) // tme)
    blk_start, blk_exp, blk_src, blk_nvalid, blk_next, n_used, fill_blocks, n_fill = _routing_tables(
        cnt, tme=tme, nb=nb)
    dest_p = _dest_rows(rt_p, blk_start, tme)
    dest_s = _dest_rows(rt_s, blk_start, tme)
    xs_sorted = _dispatch(hnp, hns, jnp.concatenate([dest_p, dest_s], axis=1), fill_blocks, n_fill, nb=nb, tme=tme)
    ys = _experts(xs_sorted, w1[0], b1[0], w2[0], b2[0], blk_exp, blk_src, blk_nvalid, blk_next, n_used, tme=tme)
    g_fin = final_norm_g.reshape(1, d)
    y_p = _combine(x1p, route_p, dest_p, ys, g_fin)
    y_s = _combine(x1s, route_s, dest_s, ys, g_fin)

    kv5 = lambda t, bb: t.reshape(bb, -1, KV_WIDTH)[:, -WINDOW:].reshape(bb, WINDOW, N_KV_HEADS, HEAD_DIM)
    new_k_p = kv5(kfp, bp)[None]
    new_v_p = kv5(vfp, bp)[None]
    ctx = CONV_K - 1
    new_c_p = ap.reshape(bp, sp, cw)[:, -ctx:][None]
    new_c_s = jnp.concatenate([state_conv[0], as_.reshape(bs, ss, cw)], axis=1)[:, -ctx:][None]
    return (y_p.reshape(bp, sp, d), y_s.reshape(bs, ss, d), new_k_p, new_v_p, new_c_p,
            kv5(nk_s, bs)[None], kv5(nv_s, bs)[None], new_c_s)
```

```python
import functools

import jax
import jax.numpy as jnp
from jax import lax
from jax.experimental import pallas as pl
from jax.experimental.pallas import tpu as pltpu

F32 = jnp.float32
BF16 = jnp.bfloat16

HEAD_DIM = 64
N_Q_HEADS = 8
N_KV_HEADS = 2
WINDOW = 128
ROPE_THETA = 500000.0
ROPE_DIM = 16
CONV_K = 31
N_EXPERTS = 32
TOP_K = 4
SWIGLU_LIMIT = 7.0
SWIGLU_ALPHA = 1.702
RMS_EPS = 1e-5
LN_EPS = 1e-5
PAST_LEN = 16384

LANES = 128
SUBLANES = 8
CONV_HALO = 32
VMEM_LIMIT = 56 * 1024 * 1024
EXPERT_BLOCK_ROWS = 512

ATTN_WIDTH = N_Q_HEADS * HEAD_DIM
KV_WIDTH = N_KV_HEADS * HEAD_DIM


def _tile(n, pref, mult=8):
    t = min(pref, n)
    while t > 0 and (n % t or t % mult):
        t -= 1
    assert t > 0, (n, pref, mult)
    return t


def _cparams(sem):
    return pltpu.CompilerParams(dimension_semantics=sem, vmem_limit_bytes=VMEM_LIMIT)


def _sigmoid(x):
    return 1.0 / (1.0 + jnp.exp(-x))


def _rope_tables(pos):
    half = ROPE_DIM // 2
    inv_freq = jnp.power(jnp.float32(ROPE_THETA), -jnp.arange(half, dtype=F32) * 2.0 / ROPE_DIM)
    ang = pos.astype(F32)[:, None] * inv_freq[None, :]
    cos, sin = jnp.cos(ang), jnp.sin(ang)
    l64 = jnp.arange(LANES) % HEAD_DIM
    f = l64 % half
    cos_l, sin_l = cos[:, f], sin[:, f]
    c = jnp.where(l64 < ROPE_DIM, cos_l, 1.0)
    s1 = jnp.where(l64 < half, -sin_l, 0.0)
    s2 = jnp.where((l64 >= half) & (l64 < ROPE_DIM), sin_l, 0.0)
    return c.astype(F32), s1.astype(F32), s2.astype(F32)


def _inproj_kernel(x_ref, g_ref, w_ref, c_ref, s1_ref, s2_ref,
                   q_ref, k_ref, v_ref, kf_ref, vf_ref, a_ref, *, conv_width):
    x = x_ref[...]
    ms = jnp.mean(x * x, axis=-1, keepdims=True)
    h = (x * lax.rsqrt(ms + RMS_EPS) * g_ref[...]).astype(BF16)
    z = jnp.dot(h, w_ref[...], preferred_element_type=F32)
    c, s1, s2 = c_ref[...], s1_ref[...], s2_ref[...]
    half = ROPE_DIM // 2

    def rot(t):
        return t * c + pltpu.roll(t, LANES - half, 1) * s1 + pltpu.roll(t, half, 1) * s2

    scale = HEAD_DIM ** -0.5
    for j in range(ATTN_WIDTH // LANES):
        q_ref[:, j * LANES:(j + 1) * LANES] = (rot(z[:, j * LANES:(j + 1) * LANES]) * scale).astype(q_ref.dtype)
    k0 = ATTN_WIDTH
    kr = rot(z[:, k0:k0 + KV_WIDTH])
    k_ref[...] = kr.astype(BF16)
    kf_ref[...] = kr
    v0 = k0 + KV_WIDTH
    vv = z[:, v0:v0 + KV_WIDTH]
    v_ref[...] = vv.astype(BF16)
    vf_ref[...] = vv
    u0 = v0 + KV_WIDTH
    g0 = u0 + conv_width
    a_ref[...] = z[:, u0:g0] * _sigmoid(z[:, g0:g0 + conv_width])


def _in_proj(x2, g, w_bf, tables, *, seq_period, q_dtype, conv_width):
    n, d = x2.shape
    in_w = w_bf.shape[1]
    if seq_period is None:
        tm = tables[0].shape[0]
        tmap = lambda i: (0, 0)
    else:
        tm = _tile(seq_period, 1024, 16)
        per = seq_period // tm
        tmap = lambda i: (i % per, 0)
    assert n % tm == 0
    row = lambda w: pl.BlockSpec((tm, w), lambda i: (i, 0))
    tab = pl.BlockSpec((tm, LANES), tmap)
    return pl.pallas_call(
        functools.partial(_inproj_kernel, conv_width=conv_width),
        grid=(n // tm,),
        in_specs=[row(d), pl.BlockSpec((1, d), lambda i: (0, 0)),
                  pl.BlockSpec((d, in_w), lambda i: (0, 0)), tab, tab, tab],
        out_specs=[row(ATTN_WIDTH), row(KV_WIDTH), row(KV_WIDTH), row(KV_WIDTH), row(KV_WIDTH),
                   row(conv_width)],
        out_shape=[jax.ShapeDtypeStruct((n, ATTN_WIDTH), q_dtype),
                   jax.ShapeDtypeStruct((n, KV_WIDTH), BF16),
                   jax.ShapeDtypeStruct((n, KV_WIDTH), BF16),
                   jax.ShapeDtypeStruct((n, KV_WIDTH), F32),
                   jax.ShapeDtypeStruct((n, KV_WIDTH), F32),
                   jax.ShapeDtypeStruct((n, conv_width), F32)],
        compiler_params=_cparams(("parallel",)),
        name="in_proj",
    )(x2, g, w_bf, *tables)


def _dup_head(t, h):
    sw = pltpu.roll(t, HEAD_DIM, 1)
    low = lax.broadcasted_iota(jnp.int32, t.shape, 1) < HEAD_DIM
    return jnp.where(low, t, sw) if h == 0 else jnp.where(low, sw, t)


def _nt_dot(a, b):
    return lax.dot_general(a, b, (((1,), (1,)), ((), ())), preferred_element_type=F32)


def _attn_prompt_kernel(sink_ref, q_ref, kc_ref, kp_ref, vc_ref, vp_ref, o_ref, *, qb):
    j = pl.program_id(1)
    w = WINDOW
    k_all = jnp.concatenate([kp_ref[...], kc_ref[...]], axis=0).astype(F32)
    v_all = jnp.concatenate([vp_ref[...], vc_ref[...]], axis=0).astype(F32)
    r = lax.broadcasted_iota(jnp.int32, (w, 2 * w), 0)
    kk = lax.broadcasted_iota(jnp.int32, (w, 2 * w), 1)
    band = (kk > r) & (kk <= r + w)
    low = lax.broadcasted_iota(jnp.int32, (w, LANES), 1) < HEAD_DIM
    zero = jnp.zeros((w, LANES), BF16)
    group = N_Q_HEADS // N_KV_HEADS
    for h in range(N_KV_HEADS):
        kd_all = _dup_head(k_all, h).astype(BF16)
        vd_all = _dup_head(v_all, h).astype(BF16)
        for sub in range(qb):
            valid = band & ((kk >= w) | (j > 0)) if sub == 0 else band
            kd = kd_all[sub * w:(sub + 2) * w, :]
            vd = vd_all[sub * w:(sub + 2) * w, :]
            for jj in range(group // 2):
                col = (h * group // 2 + jj) * LANES
                qv = q_ref[sub * w:(sub + 1) * w, col:col + LANES]
                halves = []
                for half in range(2):
                    head = h * group + jj * 2 + half
                    qm = jnp.where(low if half == 0 else ~low, qv, zero)
                    s = jnp.where(valid, _nt_dot(qm, kd), -jnp.inf)
                    sink = sink_ref[head]
                    m = jnp.maximum(jnp.max(s, axis=-1, keepdims=True), sink)
                    p = jnp.exp(s - m)
                    den = jnp.sum(p, axis=-1, keepdims=True) + jnp.exp(sink - m)
                    o = jnp.dot(p.astype(BF16), vd, preferred_element_type=F32)
                    halves.append(o / den)
                o_ref[sub * w:(sub + 1) * w, col:col + LANES] = (
                    jnp.where(low, halves[0], halves[1]).astype(o_ref.dtype))


def _attn_prompt(q, k, v, sinks, *, batch, seq):
    nb = seq // WINDOW
    qb = next(c for c in (4, 2, 1) if nb % c == 0)
    steps = nb // qb
    cur = lambda b, j: (b * steps + j, 0)
    prev = lambda b, j: (b * nb + jnp.maximum(j * qb - 1, 0), 0)
    return pl.pallas_call(
        functools.partial(_attn_prompt_kernel, qb=qb),
        grid=(batch, steps),
        in_specs=[pl.BlockSpec(memory_space=pltpu.SMEM),
                  pl.BlockSpec((qb * WINDOW, ATTN_WIDTH), cur),
                  pl.BlockSpec((qb * WINDOW, KV_WIDTH), cur), pl.BlockSpec((WINDOW, KV_WIDTH), prev),
                  pl.BlockSpec((qb * WINDOW, KV_WIDTH), cur), pl.BlockSpec((WINDOW, KV_WIDTH), prev)],
        out_specs=pl.BlockSpec((qb * WINDOW, ATTN_WIDTH), cur),
        out_shape=jax.ShapeDtypeStruct((batch * seq, ATTN_WIDTH), BF16),
        compiler_params=_cparams(("parallel", "parallel")),
        name="attn_prompt",
    )(sinks, q, k, k, v, v)


def _attn_sample_kernel(sink_ref, q_ref, kn_ref, vn_ref, ck_ref, cv_ref, o_ref, nk_ref, nv_ref, *, gb, ts):
    w = WINDOW
    group = N_Q_HEADS // N_KV_HEADS
    rows = group * ts
    low = lax.broadcasted_iota(jnp.int32, (ts, LANES), 1) < HEAD_DIM
    pad = jnp.zeros((ts, LANES), F32)
    s_c, s_n, v_dup = [], [], []
    for b in range(gb):
        kc, vc = ck_ref[b], cv_ref[b]
        kn, vn = kn_ref[b * ts:(b + 1) * ts, :], vn_ref[b * ts:(b + 1) * ts, :]
        nk_ref[b, 0:w - ts, :] = kc[ts:, :]
        nk_ref[b, w - ts:, :] = kn
        nv_ref[b, 0:w - ts, :] = vc[ts:, :]
        nv_ref[b, w - ts:, :] = vn
        knp = jnp.concatenate([kn, pad], axis=0)
        vnp = jnp.concatenate([vn, pad], axis=0)
        qb = q_ref[b * ts:(b + 1) * ts, :]
        for h in range(N_KV_HEADS):
            parts = []
            for jj in range(group // 2):
                col = (h * group // 2 + jj) * LANES
                qv = qb[:, col:col + LANES]
                parts += [jnp.where(low, qv, 0.0), jnp.where(low, 0.0, qv)]
            lhs = jnp.concatenate(parts, axis=0).astype(BF16)
            s_c.append(_nt_dot(lhs, _dup_head(kc, h).astype(BF16)))
            s_n.append(_nt_dot(lhs, _dup_head(knp, h).astype(BF16)))
            v_dup.append((_dup_head(vc, h).astype(BF16), _dup_head(vnp, h).astype(BF16)))
    s_c = jnp.concatenate(s_c, axis=0)
    s_n = jnp.concatenate(s_n, axis=0)
    n_rows = s_c.shape[0]
    ridx = lax.broadcasted_iota(jnp.int32, (n_rows, 1), 0)
    t_row = ridx % ts
    head_row = (ridx // ts) % N_Q_HEADS
    sink = jnp.zeros((n_rows, 1), F32)
    for hd in range(N_Q_HEADS):
        sink = jnp.where(head_row == hd, sink_ref[hd], sink)
    c_idx = lax.broadcasted_iota(jnp.int32, (n_rows, w), 1)
    n_idx = lax.broadcasted_iota(jnp.int32, (n_rows, 2 * ts), 1)
    s_c = jnp.where(c_idx > t_row, s_c, -jnp.inf)
    s_n = jnp.where(n_idx <= t_row, s_n, -jnp.inf)
    m = jnp.maximum(jnp.maximum(jnp.max(s_c, axis=-1, keepdims=True), jnp.max(s_n, axis=-1, keepdims=True)), sink)
    p_c = jnp.exp(s_c - m)
    p_n = jnp.exp(s_n - m)
    den = jnp.sum(p_c, axis=-1, keepdims=True) + jnp.sum(p_n, axis=-1, keepdims=True) + jnp.exp(sink - m)
    p_c = p_c.astype(BF16)
    p_n = p_n.astype(BF16)
    outs = []
    for b in range(gb):
        cols = []
        for h in range(N_KV_HEADS):
            ci = b * N_KV_HEADS + h
            sl = slice(ci * rows, (ci + 1) * rows)
            vdc, vdn = v_dup[ci]
            o = (jnp.dot(p_c[sl], vdc, preferred_element_type=F32)
                 + jnp.dot(p_n[sl], vdn, preferred_element_type=F32)) / den[sl]
            for jj in range(group // 2):
                lo_part = o[(2 * jj) * ts:(2 * jj + 1) * ts, :]
                hi_part = o[(2 * jj + 1) * ts:(2 * jj + 2) * ts, :]
                cols.append(jnp.where(low, lo_part, hi_part))
        outs.append(jnp.concatenate(cols, axis=1))
    o_ref[...] = jnp.concatenate(outs, axis=0).astype(o_ref.dtype)


def _attn_sample(q, kf, vf, cache_k, cache_v, sinks, *, batch, ts):
    assert ts % 8 == 0 and ts <= WINDOW
    gb = _tile(batch, 8, 2)
    tok = lambda w: pl.BlockSpec((gb * ts, w), lambda i: (i, 0))
    cache = pl.BlockSpec((gb, WINDOW, KV_WIDTH), lambda i: (i, 0, 0))
    cshape = jax.ShapeDtypeStruct((batch, WINDOW, KV_WIDTH), F32)
    return pl.pallas_call(
        functools.partial(_attn_sample_kernel, gb=gb, ts=ts),
        grid=(batch // gb,),
        in_specs=[pl.BlockSpec(memory_space=pltpu.SMEM), tok(ATTN_WIDTH), tok(KV_WIDTH), tok(KV_WIDTH),
                  cache, cache],
        out_specs=[tok(ATTN_WIDTH), cache, cache],
        out_shape=[jax.ShapeDtypeStruct((batch * ts, ATTN_WIDTH), BF16), cshape, cshape],
        compiler_params=_cparams(("parallel",)),
        name="attn_sample",
    )(sinks, q, kf, vf, cache_k, cache_v)


def _ln_swish(acc, b, lg, lb):
    y = acc + b
    mu = jnp.mean(y, axis=-1, keepdims=True)
    yc = y - mu
    var = jnp.mean(yc * yc, axis=-1, keepdims=True)
    yn = yc * lax.rsqrt(var + LN_EPS) * lg + lb
    return yn * _sigmoid(yn)


def _conv_prompt_kernel(a_ref, ap_ref, w_ref, b_ref, lg_ref, lb_ref, o_ref, win_ref, *, tt, rc):
    j = pl.program_id(1)
    n = CONV_HALO + tt
    win = jnp.concatenate([jnp.where(j > 0, ap_ref[...], 0.0), a_ref[...]], axis=0)
    win_ref[0] = win
    for r in range(1, SUBLANES):
        win_ref[r] = pltpu.roll(win, n - r, 0)
    off = CONV_HALO - (CONV_K - 1)
    b, lg, lb = b_ref[...], lg_ref[...], lb_ref[...]
    for c in range(tt // rc):
        acc = jnp.zeros((rc, a_ref.shape[1]), F32)
        for k in range(CONV_K):
            s = off + k
            base = c * rc + (s // SUBLANES) * SUBLANES
            wk = jnp.concatenate([w_ref[k]] * (rc // SUBLANES), axis=0)
            acc = acc + wk * win_ref[s % SUBLANES, base:base + rc, :]
        o_ref[c * rc:(c + 1) * rc, :] = _ln_swish(acc, b, lg, lb).astype(o_ref.dtype)


def _conv_prompt(a, w, b, lg, lb, *, batch, seq):
    cw = a.shape[1]
    tt = _tile(seq, 512, CONV_HALO)
    rc = _tile(tt, 32, 16)
    nt = seq // tt
    per = tt // CONV_HALO
    cur = lambda bb, j: (bb * nt + j, 0)
    prev = lambda bb, j: (jnp.maximum((bb * nt + j) * per - 1, 0), 0)
    vec = pl.BlockSpec((1, cw), lambda bb, j: (0, 0))
    return pl.pallas_call(
        functools.partial(_conv_prompt_kernel, tt=tt, rc=rc),
        grid=(batch, nt),
        in_specs=[pl.BlockSpec((tt, cw), cur), pl.BlockSpec((CONV_HALO, cw), prev),
                  pl.BlockSpec((CONV_K, SUBLANES, cw), lambda bb, j: (0, 0, 0)), vec, vec, vec],
        out_specs=pl.BlockSpec((tt, cw), cur),
        out_shape=jax.ShapeDtypeStruct((batch * seq, cw), BF16),
        scratch_shapes=[pltpu.VMEM((SUBLANES, CONV_HALO + tt, cw), F32)],
        compiler_params=_cparams(("parallel", "parallel")),
        name="conv_prompt",
    )(a, a, jnp.broadcast_to(w[:, None, :], (CONV_K, SUBLANES, cw)), b, lg, lb)


def _conv_sample_kernel(a_ref, st_ref, w_ref, b_ref, lg_ref, lb_ref, o_ref, win_ref, *, gb, ts):
    ctx = CONV_K - 1
    b, lg, lb = b_ref[...], lg_ref[...], lb_ref[...]
    for bb in range(gb):
        win_ref[bb, 0:ctx, :] = st_ref[bb]
        win_ref[bb, ctx:ctx + ts, :] = a_ref[bb * ts:(bb + 1) * ts, :]
    outs = []
    for bb in range(gb):
        acc = jnp.zeros((ts, a_ref.shape[1]), F32)
        for k in range(CONV_K):
            wk = jnp.concatenate([w_ref[k]] * (ts // SUBLANES), axis=0)
            acc = acc + wk * win_ref[bb, k:k + ts, :]
        outs.append(_ln_swish(acc, b, lg, lb))
    o_ref[...] = jnp.concatenate(outs, axis=0).astype(o_ref.dtype)


def _conv_sample(a, state, w, b, lg, lb, *, batch, ts):
    cw = a.shape[1]
    ctx = CONV_K - 1
    gb = _tile(batch, 8, 2)
    vec = pl.BlockSpec((1, cw), lambda i: (0, 0))
    return pl.pallas_call(
        functools.partial(_conv_sample_kernel, gb=gb, ts=ts),
        grid=(batch // gb,),
        in_specs=[pl.BlockSpec((gb * ts, cw), lambda i: (i, 0)),
                  pl.BlockSpec((gb, ctx, cw), lambda i: (i, 0, 0)),
                  pl.BlockSpec((CONV_K, SUBLANES, cw), lambda i: (0, 0, 0)), vec, vec, vec],
        out_specs=pl.BlockSpec((gb * ts, cw), lambda i: (i, 0)),
        out_shape=jax.ShapeDtypeStruct((batch * ts, cw), BF16),
        scratch_shapes=[pltpu.VMEM((gb, ctx + ts + 2, cw), F32)],
        compiler_params=_cparams(("parallel",)),
        name="conv_sample",
    )(a, state, jnp.broadcast_to(w[:, None, :], (CONV_K, SUBLANES, cw)), b, lg, lb)


def _outproj_router_kernel(xa_ref, atta_ref, cva_ref, xb_ref, attb_ref, cvb_ref, wo_ref, g_ref, rw_ref, rb_ref,
                           x1_ref, hn_ref, route_ref, rt_ref, cnt_ref, carry_ref, *, ts, steps_a):
    i = pl.program_id(0)

    @pl.when(i == 0)
    def _():
        carry_ref[...] = jnp.zeros(carry_ref.shape, F32)

    tile = functools.partial(_outproj_router_tile, wo_ref=wo_ref, g_ref=g_ref, rw_ref=rw_ref, rb_ref=rb_ref,
                             x1_ref=x1_ref, hn_ref=hn_ref, route_ref=route_ref, rt_ref=rt_ref, cnt_ref=cnt_ref,
                             carry_ref=carry_ref, ts=ts)

    @pl.when(i < steps_a)
    def _():
        tile(xa_ref, atta_ref, cva_ref)

    @pl.when(i >= steps_a)
    def _():
        tile(xb_ref, attb_ref, cvb_ref)


def _outproj_router_tile(x_ref, att_ref, cv_ref, *, wo_ref, g_ref, rw_ref, rb_ref,
                         x1_ref, hn_ref, route_ref, rt_ref, cnt_ref, carry_ref, ts):
    aw = att_ref.shape[1]
    n_exp = rw_ref.shape[0]
    r = lax.broadcasted_iota(jnp.int32, (ts, ts), 0)
    c = lax.broadcasted_iota(jnp.int32, (ts, ts), 1)
    before = (r < c).astype(BF16)
    eid = lax.broadcasted_iota(jnp.int32, (n_exp, ts), 0)
    carry = carry_ref[...][:, 0:1]
    for s in range(x_ref.shape[0] // ts):
        rs = slice(s * ts, (s + 1) * ts)
        mix = (jnp.dot(att_ref[rs, :], wo_ref[0:aw, :], preferred_element_type=F32)
               + jnp.dot(cv_ref[rs, :], wo_ref[aw:, :], preferred_element_type=F32))
        x1 = x_ref[rs, :] + mix
        x1_ref[rs, :] = x1
        ms = jnp.mean(x1 * x1, axis=-1, keepdims=True)
        hn = x1 * lax.rsqrt(ms + RMS_EPS) * g_ref[...]
        _store_slabs(hn_ref.at[pl.ds(s * ts * SUBLANES, ts * SUBLANES)], hn, ts)
        logits = _nt_dot(rw_ref[...], hn.astype(BF16)) + rb_ref[...][:, 0:1]
        onehot = jnp.zeros((n_exp, ts), F32)
        vals, idxs = [], []
        for _ in range(TOP_K):
            m = jnp.max(logits, axis=0, keepdims=True)
            idx = jnp.min(jnp.where(logits == m, eid, n_exp), axis=0, keepdims=True)
            sel = eid == idx
            onehot = onehot + sel.astype(F32)
            logits = jnp.where(sel, -jnp.inf, logits)
            vals.append(m)
            idxs.append(idx)
        es = [jnp.exp(v - vals[0]) for v in vals]
        den = es[0] + es[1] + es[2] + es[3]
        prefix = jnp.dot(onehot.astype(BF16), before, preferred_element_type=F32) + carry
        ranks = [jnp.sum(jnp.where(eid == idxs[k], prefix, 0.0), axis=0, keepdims=True) for k in range(TOP_K)]
        rows = [ix.astype(F32) for ix in idxs] + ranks
        rt_ref[:, rs] = jnp.concatenate(rows, axis=0)
        fields = jnp.concatenate(rows + [e / den for e in es] + [jnp.zeros((LANES - 3 * TOP_K, ts), F32)],
                                 axis=0)
        route_ref[rs, :] = jnp.transpose(fields)
        carry = carry + jnp.sum(onehot, axis=1, keepdims=True)
    carry_ref[...] = jnp.broadcast_to(carry, carry_ref.shape)
    cnt_ref[...] = jnp.broadcast_to(carry, cnt_ref.shape)


def _store_slabs(ref, val, rows):
    for j in range(val.shape[1] // LANES):
        ref[pl.ds(j, rows, stride=SUBLANES), :] = val[:, j * LANES:(j + 1) * LANES]


def _load_slabs(ref, rows, dtype):
    return jnp.concatenate([ref[pl.ds(j, rows, stride=SUBLANES), :].astype(dtype) for j in range(SUBLANES)],
                           axis=1)


def _outproj_router(group_a, group_b, wo_bf, g, rwt_bf, rbt):
    (xa, atta, cva), (xb, attb, cvb) = group_a, group_b
    na, d = xa.shape
    nb_rows = xb.shape[0]
    n = na + nb_rows
    assert d == SUBLANES * LANES
    n_exp = rwt_bf.shape[0]
    tm = _tile(nb_rows, 512, 16)
    assert na % tm == 0
    ts = _tile(tm, 512, 16)
    steps_a = na // tm
    rows_a = lambda w: pl.BlockSpec((tm, w), lambda i: (jnp.minimum(i, steps_a - 1), 0))
    rows_b = lambda w: pl.BlockSpec((tm, w), lambda i: (jnp.maximum(i - steps_a, 0), 0))
    row = lambda w: pl.BlockSpec((tm, w), lambda i: (i, 0))
    full = lambda a: pl.BlockSpec(a.shape, lambda i: (0,) * a.ndim)
    return pl.pallas_call(
        functools.partial(_outproj_router_kernel, ts=ts, steps_a=steps_a),
        grid=(n // tm,),
        in_specs=[rows_a(d), rows_a(atta.shape[1]), rows_a(cva.shape[1]),
                  rows_b(d), rows_b(attb.shape[1]), rows_b(cvb.shape[1]),
                  full(wo_bf), full(g), full(rwt_bf), full(rbt)],
        out_specs=[row(d), pl.BlockSpec((tm * SUBLANES, LANES), lambda i: (i, 0)), row(LANES),
                   pl.BlockSpec((2 * TOP_K, tm), lambda i: (0, i)),
                   pl.BlockSpec((n_exp, LANES), lambda i: (0, 0))],
        out_shape=[jax.ShapeDtypeStruct((n, d), F32), jax.ShapeDtypeStruct((n * SUBLANES, LANES), F32),
                   jax.ShapeDtypeStruct((n, LANES), F32), jax.ShapeDtypeStruct((2 * TOP_K, n), F32),
                   jax.ShapeDtypeStruct((n_exp, LANES), F32)],
        scratch_shapes=[pltpu.VMEM((n_exp, LANES), F32)],
        compiler_params=_cparams(("arbitrary",)),
        name="outproj_router",
    )(xa, atta, cva, xb, attb, cvb, wo_bf, g, rwt_bf, rbt)


def _dispatch_kernel(fill_ref, nfill_ref, dest_ref, hn_ref, hn2_ref, xs_ref, zero_ref, sem, zsem,
                     *, td, tme, n_first):
    @pl.when(pl.program_id(0) == 0)
    def _():
        zero_ref[...] = jnp.zeros(zero_ref.shape, F32)

        def fill(f):
            row = pl.multiple_of(fill_ref[f] * (tme * SUBLANES), tme * SUBLANES)
            return pltpu.make_async_copy(zero_ref, xs_ref.at[pl.ds(row, tme * SUBLANES)], zsem)

        def start(f, carry):
            fill(f).start()
            return carry

        def wait(f, carry):
            fill(f).wait()
            return carry

        lax.fori_loop(0, nfill_ref[0], start, 0)
        lax.fori_loop(0, nfill_ref[0], wait, 0)

    def scatter_rows(src_ref):
        def issue(r, carry):
            src = src_ref.at[pl.ds(pl.multiple_of(r * SUBLANES, SUBLANES), SUBLANES)]
            for k in range(TOP_K):
                d = pl.multiple_of(dest_ref[0, 0, k * td + r] * SUBLANES, SUBLANES)
                pltpu.make_async_copy(src, xs_ref.at[pl.ds(d, SUBLANES)], sem).start(priority=k % 2)
            return carry

        lax.fori_loop(0, td, issue, 0, unroll=8)
        for _ in range(TOP_K):
            pltpu.make_async_copy(src_ref, xs_ref.at[pl.ds(0, td * SUBLANES)], sem).wait()

    @pl.when(pl.program_id(0) < n_first)
    def _():
        scatter_rows(hn_ref)

    @pl.when(pl.program_id(0) >= n_first)
    def _():
        scatter_rows(hn2_ref)


def _dest_blocks(dest, td):
    n = dest.shape[1]
    return dest.reshape(TOP_K, n // td, td).transpose(1, 0, 2).reshape(n // td, 1, TOP_K * td)


def _dispatch(hn_a, hn_b, dest, fill_blocks, n_fill, *, nb, tme):
    na = hn_a.shape[0] // SUBLANES
    nb_rows = hn_b.shape[0] // SUBLANES
    td = _tile(nb_rows, 256, 8)
    assert na % td == 0
    n_first = na // td
    steps = n_first + nb_rows // td
    slab = lambda m: pl.BlockSpec((td * SUBLANES, LANES), m)
    grid_spec = pltpu.PrefetchScalarGridSpec(
        num_scalar_prefetch=2,
        grid=(steps,),
        in_specs=[pl.BlockSpec((1, 1, td * TOP_K), lambda i, fb, nf: (i, 0, 0), memory_space=pltpu.SMEM),
                  slab(lambda i, fb, nf: (jnp.minimum(i, n_first - 1), 0)),
                  slab(lambda i, fb, nf: (jnp.maximum(i - n_first, 0), 0))],
        out_specs=pl.BlockSpec(memory_space=pl.ANY),
        scratch_shapes=[pltpu.VMEM((tme * SUBLANES, LANES), F32), pltpu.SemaphoreType.DMA(()),
                        pltpu.SemaphoreType.DMA(())],
    )
    return pl.pallas_call(
        functools.partial(_dispatch_kernel, td=td, tme=tme, n_first=n_first),
        grid_spec=grid_spec,
        out_shape=jax.ShapeDtypeStruct((nb * tme * SUBLANES, LANES), F32),
        compiler_params=_cparams(("arbitrary",)),
        name="dispatch",
    )(fill_blocks, n_fill, _dest_blocks(dest, td), hn_a, hn_b)


def _experts_kernel(be_ref, nv_ref, nx_ref, nu_ref, tokc_ref, tokn_ref, hn_hbm, w1_hbm, b1_ref, w2_hbm, b2_ref,
                    y_ref, xbuf_ref, w1f_ref, w2f_ref, w1b_ref, w2b_ref, par_ref, gsem, wsem, *, tme):
    i = pl.program_id(0)
    e = be_ref[i]
    e_prev = be_ref[jnp.maximum(i - 1, 0)]
    d_ff = w2f_ref.shape[1]
    half = tme // 2
    slot = i % 2
    n_used = nu_ref[0]

    def gather_rows(tok_ref, s, lo, hi):
        for r in range(lo, hi):
            t = pl.multiple_of(tok_ref[0, 0, r] * SUBLANES, SUBLANES)
            pltpu.make_async_copy(hn_hbm.at[pl.ds(t, SUBLANES)], xbuf_ref.at[s, pl.ds(r * SUBLANES, SUBLANES)],
                                  gsem.at[s]).start(priority=r % 2)

    @pl.when(i == 0)
    def _():
        gather_rows(tokc_ref, 0, 0, tme)

    @pl.when(i <= n_used)
    def _():
        pltpu.make_async_copy(hn_hbm.at[pl.ds(0, tme * SUBLANES)], xbuf_ref.at[slot], gsem.at[slot]).wait()

    x_ref = xbuf_ref.at[slot]

    def fetch(expert, s):
        return (pltpu.make_async_copy(w1_hbm.at[expert], w1f_ref.at[s], wsem.at[0, s]),
                pltpu.make_async_copy(w2_hbm.at[expert], w2f_ref.at[s], wsem.at[1, s]))

    @pl.when(i == 0)
    def _():
        par_ref[0] = 0
        for c in fetch(e, 0):
            c.start()

    @pl.when((i == 0) | (e != e_prev))
    def _():
        s = par_ref[0]
        for c in fetch(e, s):
            c.wait()
        w1b_ref[...] = w1f_ref[s].astype(BF16)
        w2b_ref[...] = w2f_ref[s].astype(BF16)
        nxt = nx_ref[i]

        @pl.when(nxt >= 0)
        def _():
            for c in fetch(nxt, 1 - s):
                c.start()

        par_ref[0] = 1 - s

    def ffn(x):
        n1 = w1b_ref.shape[1] // 2
        hs = []
        for c in range(2):
            hs.append(jnp.dot(x, w1b_ref[:, c * n1:(c + 1) * n1], preferred_element_type=F32)
                      + b1_ref[0][:, c * n1:(c + 1) * n1])
            gather_rows(tokn_ref, 1 - slot, c * half, (c + 1) * half)
        x_glu = jnp.minimum(hs[0], SWIGLU_LIMIT)
        x_lin = jnp.clip(hs[1], -SWIGLU_LIMIT, SWIGLU_LIMIT)
        act = x_glu * _sigmoid(SWIGLU_ALPHA * x_glu) * (x_lin + 1.0)
        return jnp.dot(act.astype(BF16), w2b_ref[...], preferred_element_type=F32) + b2_ref[0]

    used = i < n_used
    nv = nv_ref[i]

    @pl.when(used & (nv > half))
    def _():
        _store_slabs(y_ref, ffn(_load_slabs(x_ref, tme, BF16)), tme)

    @pl.when(used & (nv <= half))
    def _():
        rows = half * SUBLANES
        _store_slabs(y_ref.at[pl.ds(0, rows)], ffn(_load_slabs(x_ref.at[pl.ds(0, rows)], half, BF16)), half)
        y_ref[pl.ds(rows, rows), :] = jnp.zeros((rows, LANES), F32)

    @pl.when(jnp.logical_not(used))
    def _():
        y_ref[...] = jnp.zeros(y_ref.shape, F32)


def _experts(hn, tok_of_row, w1, b1, w2, b2, blk_exp, blk_nvalid, blk_next, n_used, *, tme):
    n_exp, d, h2 = w1.shape
    assert h2 == 2 * w2.shape[1]
    d_ff = w2.shape[1]
    nb = tok_of_row.shape[0]
    tok3 = tok_of_row.reshape(nb, 1, tme)
    idx = lambda m: pl.BlockSpec((1, 1, tme), m, memory_space=pltpu.SMEM)
    grid_spec = pltpu.PrefetchScalarGridSpec(
        num_scalar_prefetch=4,
        grid=(nb,),
        in_specs=[idx(lambda i, be, nv, nx, nu: (i, 0, 0)),
                  idx(lambda i, be, nv, nx, nu: (jnp.minimum(i + 1, nb - 1), 0, 0)),
                  pl.BlockSpec(memory_space=pl.ANY),
                  pl.BlockSpec(memory_space=pl.ANY),
                  pl.BlockSpec((1, 1, h2), lambda i, be, nv, nx, nu: (be[i], 0, 0)),
                  pl.BlockSpec(memory_space=pl.ANY),
                  pl.BlockSpec((1, 1, d), lambda i, be, nv, nx, nu: (be[i], 0, 0))],
        out_specs=pl.BlockSpec((tme * SUBLANES, LANES), lambda i, be, nv, nx, nu: (i, 0)),
        scratch_shapes=[pltpu.VMEM((2, tme * SUBLANES, LANES), F32),
                        pltpu.VMEM((2, d, h2), F32), pltpu.VMEM((2, d_ff, d), F32),
                        pltpu.VMEM((d, h2), BF16), pltpu.VMEM((d_ff, d), BF16),
                        pltpu.SMEM((1,), jnp.int32), pltpu.SemaphoreType.DMA((2,)),
                        pltpu.SemaphoreType.DMA((2, 2))],
    )
    return pl.pallas_call(
        functools.partial(_experts_kernel, tme=tme),
        grid_spec=grid_spec,
        out_shape=jax.ShapeDtypeStruct((nb * tme * SUBLANES, LANES), F32),
        compiler_params=_cparams(("arbitrary",)),
        name="experts",
    )(blk_exp, blk_nvalid, blk_next, n_used, tok3, tok3, hn, w1, b1.reshape(n_exp, 1, h2), w2,
      b2.reshape(n_exp, 1, d))


def _combine_kernel(dcur_ref, dnext_ref, x1_ref, route_ref, g_ref, ys_ref, o_ref, buf_ref, sem, *, tc):
    i = pl.program_id(0)
    slot = i % 2

    def gather(dref, s):
        def issue(r, carry):
            row = pl.multiple_of(r * SUBLANES, SUBLANES)
            for k in range(TOP_K):
                d = pl.multiple_of(dref[0, 0, k * tc + r] * SUBLANES, SUBLANES)
                pltpu.make_async_copy(ys_ref.at[pl.ds(d, SUBLANES)], buf_ref.at[s, k, pl.ds(row, SUBLANES)],
                                      sem.at[s]).start(priority=k % 2)
            return carry

        lax.fori_loop(0, tc, issue, 0, unroll=8)

    @pl.when(i == 0)
    def _():
        gather(dcur_ref, 0)

    @pl.when(i + 1 < pl.num_programs(0))
    def _():
        gather(dnext_ref, 1 - slot)

    for k in range(TOP_K):
        pltpu.make_async_copy(ys_ref.at[pl.ds(0, tc * SUBLANES)], buf_ref.at[slot, k], sem.at[slot]).wait()
    route = route_ref[...]
    x1 = x1_ref[...]
    chunks = []
    for j in range(SUBLANES):
        acc = x1[:, j * LANES:(j + 1) * LANES]
        for k in range(TOP_K):
            acc = acc + (route[:, 2 * TOP_K + k:2 * TOP_K + k + 1]
                         * buf_ref[slot, k, pl.ds(j, tc, stride=SUBLANES), :])
        chunks.append(acc)
    y = jnp.concatenate(chunks, axis=1)
    ms = jnp.mean(y * y, axis=-1, keepdims=True)
    o_ref[...] = y * lax.rsqrt(ms + RMS_EPS) * g_ref[...]


def _combine(x1, route, dest, ys, g, *, first, n):
    d = x1.shape[1]
    tc = _tile(n, 128, 8)
    assert first % tc == 0
    steps, t0 = n // tc, first // tc
    dest3 = _dest_blocks(dest, tc)
    dspec = lambda m: pl.BlockSpec((1, 1, tc * TOP_K), m, memory_space=pltpu.SMEM)
    return pl.pallas_call(
        functools.partial(_combine_kernel, tc=tc),
        grid=(steps,),
        in_specs=[dspec(lambda i: (t0 + i, 0, 0)), dspec(lambda i: (t0 + jnp.minimum(i + 1, steps - 1), 0, 0)),
                  pl.BlockSpec((tc, d), lambda i: (t0 + i, 0)),
                  pl.BlockSpec((tc, LANES), lambda i: (t0 + i, 0)),
                  pl.BlockSpec((1, d), lambda i: (0, 0)),
                  pl.BlockSpec(memory_space=pl.ANY)],
        out_specs=pl.BlockSpec((tc, d), lambda i: (i, 0)),
        out_shape=jax.ShapeDtypeStruct((n, d), F32),
        scratch_shapes=[pltpu.VMEM((2, TOP_K, tc * SUBLANES, LANES), F32), pltpu.SemaphoreType.DMA((2,))],
        compiler_params=_cparams(("arbitrary",)),
        name="combine",
    )(dest3, dest3, x1, route, g, ys)


def _dest_rows(route_t, blk_start, tme):
    eidx = route_t[0:TOP_K].astype(jnp.int32)
    rank = route_t[TOP_K:2 * TOP_K].astype(jnp.int32)
    experts = jnp.arange(N_EXPERTS, dtype=jnp.int32)[:, None, None]
    first = jnp.sum(jnp.where(eidx[None] == experts, blk_start[:, None, None], 0), axis=0)
    return (first * tme + rank).astype(jnp.int32)


def _routing_tables(counts_f, *, tme, nb):
    counts = counts_f[:, 0].astype(jnp.int32)
    nblk = (counts + tme - 1) // tme
    blk_end = jnp.cumsum(nblk)
    blk_start = blk_end - nblk
    n_used = blk_end[-1]
    b = jnp.arange(nb, dtype=jnp.int32)
    used = b < n_used
    blk_exp = jnp.minimum(jnp.sum((b[:, None] >= blk_end[None, :]).astype(jnp.int32), axis=1), N_EXPERTS - 1)
    last_exp = jnp.max(jnp.where(nblk > 0, jnp.arange(N_EXPERTS, dtype=jnp.int32), 0))
    blk_exp = jnp.where(used, blk_exp, last_exp).astype(jnp.int32)
    experts = jnp.arange(N_EXPERTS, dtype=jnp.int32)
    mine = (b[:, None] >= blk_start[None, :]) & (b[:, None] < blk_end[None, :])
    inside = (b[:, None] - blk_start[None, :]) * tme
    nvalid = jnp.clip(jnp.sum(jnp.where(mine, counts[None, :] - inside, 0), axis=1), 0, tme).astype(jnp.int32)
    dense_start = jnp.cumsum(counts) - counts
    blk_off = jnp.sum(jnp.where(mine, dense_start[None, :] + inside, 0), axis=1).astype(jnp.int32)
    later = (experts[None, :] > experts[:, None]) & (nblk[None, :] > 0)
    nxt_e = jnp.min(jnp.where(later, experts[None, :], N_EXPERTS), axis=1)
    nxt_e = jnp.where(nxt_e == N_EXPERTS, -1, nxt_e)
    blk_next = jnp.sum(jnp.where(blk_exp[:, None] == experts[None, :], nxt_e[None, :], 0), axis=1).astype(jnp.int32)
    return blk_start, blk_exp, nvalid, blk_off, blk_next, n_used.reshape(1).astype(jnp.int32)


def _tokens_of_rows(dest, blk_off, tme):
    n_tok = dest.shape[1]
    tok = jnp.broadcast_to(jnp.arange(n_tok, dtype=jnp.int32)[None, :], dest.shape)
    _, tok_sorted = lax.sort((dest.reshape(-1), tok.reshape(-1)), num_keys=1)
    tok_sorted = jnp.concatenate([tok_sorted, jnp.zeros((tme,), jnp.int32)])
    return jax.vmap(lambda o: lax.dynamic_slice(tok_sorted, (o,), (tme,)))(blk_off)


def kernel(x_prompt, x_sample, cache_k, cache_v, state_conv, attn_norm_g, w_in, attn_sinks, conv_w, conv_b,
           conv_ln_g, conv_ln_b, w_out, ffn_norm_g, router_w, router_b, w1, b1, w2, b2, final_norm_g):
    depth = w_in.shape[0]
    assert depth == 1, "single-layer step"
    bp, sp, d = x_prompt.shape
    bs, ss, _ = x_sample.shape
    cw = conv_w.shape[2]
    np_, ns = bp * sp, bs * ss
    n_tok = np_ + ns
    assert sp % WINDOW == 0

    xp2 = x_prompt.reshape(np_, d)
    xs2 = x_sample.reshape(ns, d)
    w_in_bf = w_in[0].astype(BF16)
    w_out_bf = w_out[0].astype(BF16)
    g_attn = attn_norm_g[0].reshape(1, d)
    g_ffn = ffn_norm_g[0].reshape(1, d)
    sinks = attn_sinks[0]
    vec = lambda a: a.reshape(1, cw)

    tab_p = _rope_tables(jnp.arange(sp, dtype=jnp.int32))
    tms = _tile(ns, 512, max(ss, 16))
    tab_s = _rope_tables(PAST_LEN + (jnp.arange(tms, dtype=jnp.int32) % ss))
    qp, kp, vp, kfp, vfp, ap = _in_proj(xp2, g_attn, w_in_bf, tab_p, seq_period=sp, q_dtype=BF16, conv_width=cw)
    qs, _, _, kfs, vfs, as_ = _in_proj(xs2, g_attn, w_in_bf, tab_s, seq_period=None, q_dtype=F32, conv_width=cw)

    att_p = _attn_prompt(qp, kp, vp, sinks, batch=bp, seq=sp)
    ck = cache_k[0].reshape(bs, WINDOW, KV_WIDTH)
    cv_ = cache_v[0].reshape(bs, WINDOW, KV_WIDTH)
    att_s, nk_s, nv_s = _attn_sample(qs, kfs, vfs, ck, cv_, sinks, batch=bs, ts=ss)

    cv_p = _conv_prompt(ap, conv_w[0], vec(conv_b[0]), vec(conv_ln_g[0]), vec(conv_ln_b[0]), batch=bp, seq=sp)
    cv_s = _conv_sample(as_, state_conv[0], conv_w[0], vec(conv_b[0]), vec(conv_ln_g[0]), vec(conv_ln_b[0]),
                        batch=bs, ts=ss)

    n_exp = router_w.shape[2]
    assert n_exp == N_EXPERTS
    rwt_bf = router_w[0].T.astype(BF16)
    rbt = jnp.broadcast_to(router_b[0][:, None], (n_exp, LANES))
    x1, hn, route, route_t, cnt = _outproj_router((xp2, att_p, cv_p), (xs2, att_s, cv_s), w_out_bf, g_ffn,
                                                  rwt_bf, rbt)

    tme = EXPERT_BLOCK_ROWS
    nb = -(-(n_tok * TOP_K + N_EXPERTS * (tme - 1)) // tme) + 1
    blk_start, blk_exp, blk_nvalid, blk_off, blk_next, n_used = _routing_tables(cnt, tme=tme, nb=nb)
    dest = _dest_rows(route_t, blk_start, tme)
    tok_of_row = _tokens_of_rows(dest, blk_off, tme)
    ys = _experts(hn, tok_of_row, w1[0], b1[0], w2[0], b2[0], blk_exp, blk_nvalid, blk_next, n_used, tme=tme)
    g_fin = final_norm_g.reshape(1, d)
    y_p = _combine(x1, route, dest, ys, g_fin, first=0, n=np_)
    y_s = _combine(x1, route, dest, ys, g_fin, first=np_, n=ns)

    kv5 = lambda t, bb: t.reshape(bb, -1, KV_WIDTH)[:, -WINDOW:].reshape(bb, WINDOW, N_KV_HEADS, HEAD_DIM)
    new_k_p = kv5(kfp, bp)[None]
    new_v_p = kv5(vfp, bp)[None]
    ctx = CONV_K - 1
    new_c_p = ap.reshape(bp, sp, cw)[:, -ctx:][None]
    new_c_s = jnp.concatenate([state_conv[0], as_.reshape(bs, ss, cw)], axis=1)[:, -ctx:][None]
    return (y_p.reshape(bp, sp, d), y_s.reshape(bs, ss, d), new_k_p, new_v_p, new_c_p,
            kv5(nk_s, bs)[None], kv5(nv_s, bs)[None], new_c_s)
```

```python
import functools

import jax
import jax.numpy as jnp
from jax import lax
from jax.experimental import pallas as pl
from jax.experimental.pallas import tpu as pltpu

F32 = jnp.float32
BF16 = jnp.bfloat16

HEAD_DIM = 64
N_Q_HEADS = 8
N_KV_HEADS = 2
WINDOW = 128
ROPE_THETA = 500000.0
ROPE_DIM = 16
CONV_K = 31
N_EXPERTS = 32
TOP_K = 4
SWIGLU_LIMIT = 7.0
SWIGLU_ALPHA = 1.702
RMS_EPS = 1e-5
LN_EPS = 1e-5
PAST_LEN = 16384

LANES = 128
SUBLANES = 8
CONV_HALO = 32
VMEM_LIMIT = 56 * 1024 * 1024
EXPERT_BLOCK_ROWS = 512

ATTN_WIDTH = N_Q_HEADS * HEAD_DIM
KV_WIDTH = N_KV_HEADS * HEAD_DIM


def _tile(n, pref, mult=8):
    t = min(pref, n)
    while t > 0 and (n % t or t % mult):
        t -= 1
    assert t > 0, (n, pref, mult)
    return t


def _cparams(sem):
    return pltpu.CompilerParams(dimension_semantics=sem, vmem_limit_bytes=VMEM_LIMIT)


def _sigmoid(x):
    return 1.0 / (1.0 + jnp.exp(-x))


def _shr(x, pow2):
    return lax.shift_right_logical(x, jnp.int32(pow2.bit_length() - 1))


def _rope_tables(pos):
    half = ROPE_DIM // 2
    inv_freq = jnp.power(jnp.float32(ROPE_THETA), -jnp.arange(half, dtype=F32) * 2.0 / ROPE_DIM)
    ang = pos.astype(F32)[:, None] * inv_freq[None, :]
    cos, sin = jnp.cos(ang), jnp.sin(ang)
    l64 = jnp.arange(LANES) % HEAD_DIM
    f = l64 % half
    cos_l, sin_l = cos[:, f], sin[:, f]
    c = jnp.where(l64 < ROPE_DIM, cos_l, 1.0)
    s1 = jnp.where(l64 < half, -sin_l, 0.0)
    s2 = jnp.where((l64 >= half) & (l64 < ROPE_DIM), sin_l, 0.0)
    return c.astype(F32), s1.astype(F32), s2.astype(F32)


def _inproj_kernel(x_ref, g_ref, w_ref, c_ref, s1_ref, s2_ref,
                   q_ref, k_ref, v_ref, kf_ref, vf_ref, a_ref, *, conv_width):
    x = x_ref[...]
    ms = jnp.mean(x * x, axis=-1, keepdims=True)
    h = (x * lax.rsqrt(ms + RMS_EPS) * g_ref[...]).astype(BF16)
    z = jnp.dot(h, w_ref[...], preferred_element_type=F32)
    c, s1, s2 = c_ref[...], s1_ref[...], s2_ref[...]
    half = ROPE_DIM // 2

    def rot(t):
        return t * c + pltpu.roll(t, LANES - half, 1) * s1 + pltpu.roll(t, half, 1) * s2

    scale = HEAD_DIM ** -0.5
    for j in range(ATTN_WIDTH // LANES):
        q_ref[:, j * LANES:(j + 1) * LANES] = (rot(z[:, j * LANES:(j + 1) * LANES]) * scale).astype(q_ref.dtype)
    k0 = ATTN_WIDTH
    kr = rot(z[:, k0:k0 + KV_WIDTH])
    k_ref[...] = kr.astype(BF16)
    kf_ref[...] = kr
    v0 = k0 + KV_WIDTH
    vv = z[:, v0:v0 + KV_WIDTH]
    v_ref[...] = vv.astype(BF16)
    vf_ref[...] = vv
    u0 = v0 + KV_WIDTH
    g0 = u0 + conv_width
    a_ref[...] = z[:, u0:g0] * _sigmoid(z[:, g0:g0 + conv_width])


def _in_proj(x2, g, w_bf, tables, *, seq_period, q_dtype, conv_width):
    n, d = x2.shape
    in_w = w_bf.shape[1]
    if seq_period is None:
        tm = tables[0].shape[0]
        tmap = lambda i: (0, 0)
    else:
        tm = _tile(seq_period, 1024, 16)
        per = seq_period // tm
        tmap = lambda i: (i % per, 0)
    assert n % tm == 0
    row = lambda w: pl.BlockSpec((tm, w), lambda i: (i, 0))
    tab = pl.BlockSpec((tm, LANES), tmap)
    return pl.pallas_call(
        functools.partial(_inproj_kernel, conv_width=conv_width),
        grid=(n // tm,),
        in_specs=[row(d), pl.BlockSpec((1, d), lambda i: (0, 0)),
                  pl.BlockSpec((d, in_w), lambda i: (0, 0)), tab, tab, tab],
        out_specs=[row(ATTN_WIDTH), row(KV_WIDTH), row(KV_WIDTH), row(KV_WIDTH), row(KV_WIDTH),
                   row(conv_width)],
        out_shape=[jax.ShapeDtypeStruct((n, ATTN_WIDTH), q_dtype),
                   jax.ShapeDtypeStruct((n, KV_WIDTH), BF16),
                   jax.ShapeDtypeStruct((n, KV_WIDTH), BF16),
                   jax.ShapeDtypeStruct((n, KV_WIDTH), F32),
                   jax.ShapeDtypeStruct((n, KV_WIDTH), F32),
                   jax.ShapeDtypeStruct((n, conv_width), F32)],
        compiler_params=_cparams(("parallel",)),
        name="in_proj",
    )(x2, g, w_bf, *tables)


def _dup_head(t, h):
    sw = pltpu.roll(t, HEAD_DIM, 1)
    low = lax.broadcasted_iota(jnp.int32, t.shape, 1) < HEAD_DIM
    return jnp.where(low, t, sw) if h == 0 else jnp.where(low, sw, t)


def _nt_dot(a, b):
    return lax.dot_general(a, b, (((1,), (1,)), ((), ())), preferred_element_type=F32)


def _attn_prompt_kernel(sink_ref, q_ref, kc_ref, kp_ref, vc_ref, vp_ref, o_ref, *, qb):
    j = pl.program_id(1)
    w = WINDOW
    k_all = jnp.concatenate([kp_ref[...], kc_ref[...]], axis=0).astype(F32)
    v_all = jnp.concatenate([vp_ref[...], vc_ref[...]], axis=0).astype(F32)
    r = lax.broadcasted_iota(jnp.int32, (w, 2 * w), 0)
    kk = lax.broadcasted_iota(jnp.int32, (w, 2 * w), 1)
    band = (kk > r) & (kk <= r + w)
    low = lax.broadcasted_iota(jnp.int32, (w, LANES), 1) < HEAD_DIM
    zero = jnp.zeros((w, LANES), BF16)
    group = N_Q_HEADS // N_KV_HEADS
    for h in range(N_KV_HEADS):
        kd_all = _dup_head(k_all, h).astype(BF16)
        vd_all = _dup_head(v_all, h).astype(BF16)
        for sub in range(qb):
            valid = band & ((kk >= w) | (j > 0)) if sub == 0 else band
            kd = kd_all[sub * w:(sub + 2) * w, :]
            vd = vd_all[sub * w:(sub + 2) * w, :]
            for jj in range(group // 2):
                col = (h * group // 2 + jj) * LANES
                qv = q_ref[sub * w:(sub + 1) * w, col:col + LANES]
                halves = []
                for half in range(2):
                    head = h * group + jj * 2 + half
                    qm = jnp.where(low if half == 0 else ~low, qv, zero)
                    s = jnp.where(valid, _nt_dot(qm, kd), -jnp.inf)
                    sink = sink_ref[head]
                    m = jnp.maximum(jnp.max(s, axis=-1, keepdims=True), sink)
                    p = jnp.exp(s - m)
                    den = jnp.sum(p, axis=-1, keepdims=True) + jnp.exp(sink - m)
                    o = jnp.dot(p.astype(BF16), vd, preferred_element_type=F32)
                    halves.append(o / den)
                o_ref[sub * w:(sub + 1) * w, col:col + LANES] = (
                    jnp.where(low, halves[0], halves[1]).astype(o_ref.dtype))


def _attn_prompt(q, k, v, sinks, *, batch, seq):
    nb = seq // WINDOW
    qb = next(c for c in (4, 2, 1) if nb % c == 0)
    steps = nb // qb
    cur = lambda b, j: (b * steps + j, 0)
    prev = lambda b, j: (b * nb + jnp.maximum(j * qb - 1, 0), 0)
    return pl.pallas_call(
        functools.partial(_attn_prompt_kernel, qb=qb),
        grid=(batch, steps),
        in_specs=[pl.BlockSpec(memory_space=pltpu.SMEM),
                  pl.BlockSpec((qb * WINDOW, ATTN_WIDTH), cur),
                  pl.BlockSpec((qb * WINDOW, KV_WIDTH), cur), pl.BlockSpec((WINDOW, KV_WIDTH), prev),
                  pl.BlockSpec((qb * WINDOW, KV_WIDTH), cur), pl.BlockSpec((WINDOW, KV_WIDTH), prev)],
        out_specs=pl.BlockSpec((qb * WINDOW, ATTN_WIDTH), cur),
        out_shape=jax.ShapeDtypeStruct((batch * seq, ATTN_WIDTH), BF16),
        compiler_params=_cparams(("parallel", "parallel")),
        name="attn_prompt",
    )(sinks, q, k, k, v, v)


def _attn_sample_kernel(sink_ref, q_ref, kn_ref, vn_ref, ck_ref, cv_ref, o_ref, nk_ref, nv_ref, *, gb, ts):
    w = WINDOW
    group = N_Q_HEADS // N_KV_HEADS
    rows = group * ts
    low = lax.broadcasted_iota(jnp.int32, (ts, LANES), 1) < HEAD_DIM
    pad = jnp.zeros((ts, LANES), F32)
    s_c, s_n, v_dup = [], [], []
    for b in range(gb):
        kc, vc = ck_ref[b], cv_ref[b]
        kn, vn = kn_ref[b * ts:(b + 1) * ts, :], vn_ref[b * ts:(b + 1) * ts, :]
        nk_ref[b, 0:w - ts, :] = kc[ts:, :]
        nk_ref[b, w - ts:, :] = kn
        nv_ref[b, 0:w - ts, :] = vc[ts:, :]
        nv_ref[b, w - ts:, :] = vn
        knp = jnp.concatenate([kn, pad], axis=0)
        vnp = jnp.concatenate([vn, pad], axis=0)
        qb = q_ref[b * ts:(b + 1) * ts, :]
        for h in range(N_KV_HEADS):
            parts = []
            for jj in range(group // 2):
                col = (h * group // 2 + jj) * LANES
                qv = qb[:, col:col + LANES]
                parts += [jnp.where(low, qv, 0.0), jnp.where(low, 0.0, qv)]
            lhs = jnp.concatenate(parts, axis=0).astype(BF16)
            s_c.append(_nt_dot(lhs, _dup_head(kc, h).astype(BF16)))
            s_n.append(_nt_dot(lhs, _dup_head(knp, h).astype(BF16)))
            v_dup.append((_dup_head(vc, h).astype(BF16), _dup_head(vnp, h).astype(BF16)))
    s_c = jnp.concatenate(s_c, axis=0)
    s_n = jnp.concatenate(s_n, axis=0)
    n_rows = s_c.shape[0]
    ridx = lax.broadcasted_iota(jnp.int32, (n_rows, 1), 0)
    t_row = ridx % ts
    head_row = (ridx // ts) % N_Q_HEADS
    sink = jnp.zeros((n_rows, 1), F32)
    for hd in range(N_Q_HEADS):
        sink = jnp.where(head_row == hd, sink_ref[hd], sink)
    c_idx = lax.broadcasted_iota(jnp.int32, (n_rows, w), 1)
    n_idx = lax.broadcasted_iota(jnp.int32, (n_rows, 2 * ts), 1)
    s_c = jnp.where(c_idx > t_row, s_c, -jnp.inf)
    s_n = jnp.where(n_idx <= t_row, s_n, -jnp.inf)
    m = jnp.maximum(jnp.maximum(jnp.max(s_c, axis=-1, keepdims=True), jnp.max(s_n, axis=-1, keepdims=True)), sink)
    p_c = jnp.exp(s_c - m)
    p_n = jnp.exp(s_n - m)
    den = jnp.sum(p_c, axis=-1, keepdims=True) + jnp.sum(p_n, axis=-1, keepdims=True) + jnp.exp(sink - m)
    p_c = p_c.astype(BF16)
    p_n = p_n.astype(BF16)
    outs = []
    for b in range(gb):
        cols = []
        for h in range(N_KV_HEADS):
            ci = b * N_KV_HEADS + h
            sl = slice(ci * rows, (ci + 1) * rows)
            vdc, vdn = v_dup[ci]
            o = (jnp.dot(p_c[sl], vdc, preferred_element_type=F32)
                 + jnp.dot(p_n[sl], vdn, preferred_element_type=F32)) / den[sl]
            for jj in range(group // 2):
                lo_part = o[(2 * jj) * ts:(2 * jj + 1) * ts, :]
                hi_part = o[(2 * jj + 1) * ts:(2 * jj + 2) * ts, :]
                cols.append(jnp.where(low, lo_part, hi_part))
        outs.append(jnp.concatenate(cols, axis=1))
    o_ref[...] = jnp.concatenate(outs, axis=0).astype(o_ref.dtype)


def _attn_sample(q, kf, vf, cache_k, cache_v, sinks, *, batch, ts):
    assert ts % 8 == 0 and ts <= WINDOW
    gb = _tile(batch, 8, 2)
    tok = lambda w: pl.BlockSpec((gb * ts, w), lambda i: (i, 0))
    cache = pl.BlockSpec((gb, WINDOW, KV_WIDTH), lambda i: (i, 0, 0))
    cshape = jax.ShapeDtypeStruct((batch, WINDOW, KV_WIDTH), F32)
    return pl.pallas_call(
        functools.partial(_attn_sample_kernel, gb=gb, ts=ts),
        grid=(batch // gb,),
        in_specs=[pl.BlockSpec(memory_space=pltpu.SMEM), tok(ATTN_WIDTH), tok(KV_WIDTH), tok(KV_WIDTH),
                  cache, cache],
        out_specs=[tok(ATTN_WIDTH), cache, cache],
        out_shape=[jax.ShapeDtypeStruct((batch * ts, ATTN_WIDTH), BF16), cshape, cshape],
        compiler_params=_cparams(("parallel",)),
        name="attn_sample",
    )(sinks, q, kf, vf, cache_k, cache_v)


def _ln_swish(acc, b, lg, lb):
    y = acc + b
    mu = jnp.mean(y, axis=-1, keepdims=True)
    yc = y - mu
    var = jnp.mean(yc * yc, axis=-1, keepdims=True)
    yn = yc * lax.rsqrt(var + LN_EPS) * lg + lb
    return yn * _sigmoid(yn)


def _conv_prompt_kernel(a_ref, ap_ref, w_ref, b_ref, lg_ref, lb_ref, o_ref, win_ref, *, tt, rc):
    j = pl.program_id(1)
    n = CONV_HALO + tt
    win = jnp.concatenate([jnp.where(j > 0, ap_ref[...], 0.0), a_ref[...]], axis=0)
    win_ref[0] = win
    for r in range(1, SUBLANES):
        win_ref[r] = pltpu.roll(win, n - r, 0)
    off = CONV_HALO - (CONV_K - 1)
    b, lg, lb = b_ref[...], lg_ref[...], lb_ref[...]
    for c in range(tt // rc):
        acc = jnp.zeros((rc, a_ref.shape[1]), F32)
        for k in range(CONV_K):
            s = off + k
            base = c * rc + (s // SUBLANES) * SUBLANES
            wk = jnp.concatenate([w_ref[k]] * (rc // SUBLANES), axis=0)
            acc = acc + wk * win_ref[s % SUBLANES, base:base + rc, :]
        o_ref[c * rc:(c + 1) * rc, :] = _ln_swish(acc, b, lg, lb).astype(o_ref.dtype)


def _conv_prompt(a, w, b, lg, lb, *, batch, seq):
    cw = a.shape[1]
    tt = _tile(seq, 512, CONV_HALO)
    rc = _tile(tt, 32, 16)
    nt = seq // tt
    per = tt // CONV_HALO
    cur = lambda bb, j: (bb * nt + j, 0)
    prev = lambda bb, j: (jnp.maximum((bb * nt + j) * per - 1, 0), 0)
    vec = pl.BlockSpec((1, cw), lambda bb, j: (0, 0))
    return pl.pallas_call(
        functools.partial(_conv_prompt_kernel, tt=tt, rc=rc),
        grid=(batch, nt),
        in_specs=[pl.BlockSpec((tt, cw), cur), pl.BlockSpec((CONV_HALO, cw), prev),
                  pl.BlockSpec((CONV_K, SUBLANES, cw), lambda bb, j: (0, 0, 0)), vec, vec, vec],
        out_specs=pl.BlockSpec((tt, cw), cur),
        out_shape=jax.ShapeDtypeStruct((batch * seq, cw), BF16),
        scratch_shapes=[pltpu.VMEM((SUBLANES, CONV_HALO + tt, cw), F32)],
        compiler_params=_cparams(("parallel", "parallel")),
        name="conv_prompt",
    )(a, a, jnp.broadcast_to(w[:, None, :], (CONV_K, SUBLANES, cw)), b, lg, lb)


def _conv_sample_kernel(a_ref, st_ref, w_ref, b_ref, lg_ref, lb_ref, o_ref, win_ref, *, gb, ts):
    ctx = CONV_K - 1
    b, lg, lb = b_ref[...], lg_ref[...], lb_ref[...]
    for bb in range(gb):
        win_ref[bb, 0:ctx, :] = st_ref[bb]
        win_ref[bb, ctx:ctx + ts, :] = a_ref[bb * ts:(bb + 1) * ts, :]
    outs = []
    for bb in range(gb):
        acc = jnp.zeros((ts, a_ref.shape[1]), F32)
        for k in range(CONV_K):
            wk = jnp.concatenate([w_ref[k]] * (ts // SUBLANES), axis=0)
            acc = acc + wk * win_ref[bb, k:k + ts, :]
        outs.append(_ln_swish(acc, b, lg, lb))
    o_ref[...] = jnp.concatenate(outs, axis=0).astype(o_ref.dtype)


def _conv_sample(a, state, w, b, lg, lb, *, batch, ts):
    cw = a.shape[1]
    ctx = CONV_K - 1
    gb = _tile(batch, 8, 2)
    vec = pl.BlockSpec((1, cw), lambda i: (0, 0))
    return pl.pallas_call(
        functools.partial(_conv_sample_kernel, gb=gb, ts=ts),
        grid=(batch // gb,),
        in_specs=[pl.BlockSpec((gb * ts, cw), lambda i: (i, 0)),
                  pl.BlockSpec((gb, ctx, cw), lambda i: (i, 0, 0)),
                  pl.BlockSpec((CONV_K, SUBLANES, cw), lambda i: (0, 0, 0)), vec, vec, vec],
        out_specs=pl.BlockSpec((gb * ts, cw), lambda i: (i, 0)),
        out_shape=jax.ShapeDtypeStruct((batch * ts, cw), BF16),
        scratch_shapes=[pltpu.VMEM((gb, ctx + ts + 2, cw), F32)],
        compiler_params=_cparams(("parallel",)),
        name="conv_sample",
    )(a, state, jnp.broadcast_to(w[:, None, :], (CONV_K, SUBLANES, cw)), b, lg, lb)


def _outproj_router_kernel(xa_ref, atta_ref, cva_ref, xb_ref, attb_ref, cvb_ref, wo_ref, g_ref, rw_ref, rb_ref,
                           x1_ref, hn_ref, route_ref, rt_ref, cnt_ref, carry_ref, *, ts, steps_a):
    i = pl.program_id(0)

    @pl.when(i == 0)
    def _():
        carry_ref[...] = jnp.zeros(carry_ref.shape, F32)

    tile = functools.partial(_outproj_router_tile, wo_ref=wo_ref, g_ref=g_ref, rw_ref=rw_ref, rb_ref=rb_ref,
                             x1_ref=x1_ref, hn_ref=hn_ref, route_ref=route_ref, rt_ref=rt_ref, cnt_ref=cnt_ref,
                             carry_ref=carry_ref, ts=ts)

    @pl.when(i < steps_a)
    def _():
        tile(xa_ref, atta_ref, cva_ref)

    @pl.when(i >= steps_a)
    def _():
        tile(xb_ref, attb_ref, cvb_ref)


def _outproj_router_tile(x_ref, att_ref, cv_ref, *, wo_ref, g_ref, rw_ref, rb_ref,
                         x1_ref, hn_ref, route_ref, rt_ref, cnt_ref, carry_ref, ts):
    aw = att_ref.shape[1]
    n_exp = rw_ref.shape[0]
    r = lax.broadcasted_iota(jnp.int32, (ts, ts), 0)
    c = lax.broadcasted_iota(jnp.int32, (ts, ts), 1)
    before = (r < c).astype(BF16)
    eid = lax.broadcasted_iota(jnp.int32, (n_exp, ts), 0)
    carry = carry_ref[...][:, 0:1]
    for s in range(x_ref.shape[0] // ts):
        rs = slice(s * ts, (s + 1) * ts)
        mix = (jnp.dot(att_ref[rs, :], wo_ref[0:aw, :], preferred_element_type=F32)
               + jnp.dot(cv_ref[rs, :], wo_ref[aw:, :], preferred_element_type=F32))
        x1 = x_ref[rs, :] + mix
        x1_ref[rs, :] = x1
        ms = jnp.mean(x1 * x1, axis=-1, keepdims=True)
        hn = x1 * lax.rsqrt(ms + RMS_EPS) * g_ref[...]
        _store_slabs(hn_ref.at[pl.ds(s * ts * SUBLANES, ts * SUBLANES)], hn, ts)
        logits = _nt_dot(rw_ref[...], hn.astype(BF16)) + rb_ref[...][:, 0:1]
        onehot = jnp.zeros((n_exp, ts), F32)
        vals, idxs = [], []
        for _ in range(TOP_K):
            m = jnp.max(logits, axis=0, keepdims=True)
            idx = jnp.min(jnp.where(logits == m, eid, n_exp), axis=0, keepdims=True)
            sel = eid == idx
            onehot = onehot + sel.astype(F32)
            logits = jnp.where(sel, -jnp.inf, logits)
            vals.append(m)
            idxs.append(idx)
        es = [jnp.exp(v - vals[0]) for v in vals]
        den = es[0] + es[1] + es[2] + es[3]
        prefix = jnp.dot(onehot.astype(BF16), before, preferred_element_type=F32) + carry
        ranks = [jnp.sum(jnp.where(eid == idxs[k], prefix, 0.0), axis=0, keepdims=True) for k in range(TOP_K)]
        rows = [ix.astype(F32) for ix in idxs] + ranks
        rt_ref[:, rs] = jnp.concatenate(rows, axis=0)
        fields = jnp.concatenate(rows + [e / den for e in es] + [jnp.zeros((LANES - 3 * TOP_K, ts), F32)],
                                 axis=0)
        route_ref[rs, :] = jnp.transpose(fields)
        carry = carry + jnp.sum(onehot, axis=1, keepdims=True)
    carry_ref[...] = jnp.broadcast_to(carry, carry_ref.shape)
    cnt_ref[...] = jnp.broadcast_to(carry, cnt_ref.shape)


def _store_slabs(ref, val, rows):
    for j in range(val.shape[1] // LANES):
        ref[pl.ds(j, rows, stride=SUBLANES), :] = val[:, j * LANES:(j + 1) * LANES]


def _load_slabs(ref, rows, dtype):
    return jnp.concatenate([ref[pl.ds(j, rows, stride=SUBLANES), :].astype(dtype) for j in range(SUBLANES)],
                           axis=1)


def _outproj_router(group_a, group_b, wo_bf, g, rwt_bf, rbt):
    (xa, atta, cva), (xb, attb, cvb) = group_a, group_b
    na, d = xa.shape
    nb_rows = xb.shape[0]
    n = na + nb_rows
    assert d == SUBLANES * LANES
    n_exp = rwt_bf.shape[0]
    tm = _tile(nb_rows, 512, 16)
    assert na % tm == 0
    ts = _tile(tm, 512, 16)
    steps_a = na // tm
    rows_a = lambda w: pl.BlockSpec((tm, w), lambda i: (jnp.minimum(i, steps_a - 1), 0))
    rows_b = lambda w: pl.BlockSpec((tm, w), lambda i: (jnp.maximum(i - steps_a, 0), 0))
    row = lambda w: pl.BlockSpec((tm, w), lambda i: (i, 0))
    full = lambda a: pl.BlockSpec(a.shape, lambda i: (0,) * a.ndim)
    return pl.pallas_call(
        functools.partial(_outproj_router_kernel, ts=ts, steps_a=steps_a),
        grid=(n // tm,),
        in_specs=[rows_a(d), rows_a(atta.shape[1]), rows_a(cva.shape[1]),
                  rows_b(d), rows_b(attb.shape[1]), rows_b(cvb.shape[1]),
                  full(wo_bf), full(g), full(rwt_bf), full(rbt)],
        out_specs=[row(d), pl.BlockSpec((tm * SUBLANES, LANES), lambda i: (i, 0)), row(LANES),
                   pl.BlockSpec((2 * TOP_K, tm), lambda i: (0, i)),
                   pl.BlockSpec((n_exp, LANES), lambda i: (0, 0))],
        out_shape=[jax.ShapeDtypeStruct((n, d), F32), jax.ShapeDtypeStruct((n * SUBLANES, LANES), F32),
                   jax.ShapeDtypeStruct((n, LANES), F32), jax.ShapeDtypeStruct((2 * TOP_K, n), F32),
                   jax.ShapeDtypeStruct((n_exp, LANES), F32)],
        scratch_shapes=[pltpu.VMEM((n_exp, LANES), F32)],
        compiler_params=_cparams(("arbitrary",)),
        name="outproj_router",
    )(xa, atta, cva, xb, attb, cvb, wo_bf, g, rwt_bf, rbt)


def _dispatch_kernel(fill_ref, nfill_ref, dest_ref, hn_ref, hn2_ref, xs_ref, zero_ref, sem, zsem,
                     *, td, tme, n_first):
    @pl.when(pl.program_id(0) == 0)
    def _():
        zero_ref[...] = jnp.zeros(zero_ref.shape, F32)

        def fill(f):
            row = pl.multiple_of(fill_ref[f] * (tme * SUBLANES), tme * SUBLANES)
            return pltpu.make_async_copy(zero_ref, xs_ref.at[pl.ds(row, tme * SUBLANES)], zsem)

        def start(f, carry):
            fill(f).start()
            return carry

        def wait(f, carry):
            fill(f).wait()
            return carry

        lax.fori_loop(0, nfill_ref[0], start, 0)
        lax.fori_loop(0, nfill_ref[0], wait, 0)

    def scatter_rows(src_ref):
        def issue(r, carry):
            src = src_ref.at[pl.ds(pl.multiple_of(r * SUBLANES, SUBLANES), SUBLANES)]
            for k in range(TOP_K):
                d = pl.multiple_of(dest_ref[0, 0, k * td + r] * SUBLANES, SUBLANES)
                pltpu.make_async_copy(src, xs_ref.at[pl.ds(d, SUBLANES)], sem).start(priority=k % 2)
            return carry

        lax.fori_loop(0, td, issue, 0, unroll=8)
        for _ in range(TOP_K):
            pltpu.make_async_copy(src_ref, xs_ref.at[pl.ds(0, td * SUBLANES)], sem).wait()

    @pl.when(pl.program_id(0) < n_first)
    def _():
        scatter_rows(hn_ref)

    @pl.when(pl.program_id(0) >= n_first)
    def _():
        scatter_rows(hn2_ref)


def _dest_blocks(dest, td):
    n = dest.shape[1]
    return dest.reshape(TOP_K, n // td, td).transpose(1, 0, 2).reshape(n // td, 1, TOP_K * td)


def _dispatch(hn_a, hn_b, dest, fill_blocks, n_fill, *, nb, tme):
    na = hn_a.shape[0] // SUBLANES
    nb_rows = hn_b.shape[0] // SUBLANES
    td = _tile(nb_rows, 256, 8)
    assert na % td == 0
    n_first = na // td
    steps = n_first + nb_rows // td
    slab = lambda m: pl.BlockSpec((td * SUBLANES, LANES), m)
    grid_spec = pltpu.PrefetchScalarGridSpec(
        num_scalar_prefetch=2,
        grid=(steps,),
        in_specs=[pl.BlockSpec((1, 1, td * TOP_K), lambda i, fb, nf: (i, 0, 0), memory_space=pltpu.SMEM),
                  slab(lambda i, fb, nf: (jnp.minimum(i, n_first - 1), 0)),
                  slab(lambda i, fb, nf: (jnp.maximum(i - n_first, 0), 0))],
        out_specs=pl.BlockSpec(memory_space=pl.ANY),
        scratch_shapes=[pltpu.VMEM((tme * SUBLANES, LANES), F32), pltpu.SemaphoreType.DMA(()),
                        pltpu.SemaphoreType.DMA(())],
    )
    return pl.pallas_call(
        functools.partial(_dispatch_kernel, td=td, tme=tme, n_first=n_first),
        grid_spec=grid_spec,
        out_shape=jax.ShapeDtypeStruct((nb * tme * SUBLANES, LANES), F32),
        compiler_params=_cparams(("arbitrary",)),
        name="dispatch",
    )(fill_blocks, n_fill, _dest_blocks(dest, td), hn_a, hn_b)


def _experts_kernel(be_ref, nv_ref, nx_ref, off_ref, nu_ref, tok_hbm, hn_hbm, w1_hbm, b1_ref, w2_hbm, b2_ref,
                    y_ref, xbuf_ref, idx_ref, w1f_ref, w2f_ref, w1b_ref, w2b_ref, par_ref, gsem, isem, wsem,
                    *, tme):
    i = pl.program_id(0)
    last = pl.num_programs(0) - 1
    e = be_ref[i]
    e_prev = be_ref[jnp.maximum(i - 1, 0)]
    d_ff = w2f_ref.shape[1]
    half = tme // 2
    slot = i % 2
    n_used = nu_ref[0]
    window = SUBLANES * LANES

    span = idx_ref.shape[0] // 2

    def idx_copy(b):
        first = pl.multiple_of(_shr(off_ref[b], window) * window, window)
        dst = pl.multiple_of((b % 2) * span, window)
        return pltpu.make_async_copy(tok_hbm.at[pl.ds(first, span)], idx_ref.at[pl.ds(dst, span)], isem.at[b % 2])

    def gather_rows(b, lo, hi):
        base = (b % 2) * span + (off_ref[b] & (window - 1))
        for r in range(lo, hi):
            t = pl.multiple_of(idx_ref[base + r] * SUBLANES, SUBLANES)
            pltpu.make_async_copy(hn_hbm.at[pl.ds(t, SUBLANES)],
                                  xbuf_ref.at[b % 2, pl.ds(r * SUBLANES, SUBLANES)], gsem.at[b % 2]).start()

    @pl.when(i == 0)
    def _():
        idx_copy(0).start()
        idx_copy(1).start()
        idx_copy(0).wait()
        gather_rows(0, 0, tme)

    @pl.when(i <= n_used)
    def _():
        pltpu.make_async_copy(hn_hbm.at[pl.ds(0, tme * SUBLANES)], xbuf_ref.at[slot], gsem.at[slot]).wait()

    @pl.when(i < last)
    def _():
        idx_copy(i + 1).wait()

    x_ref = xbuf_ref.at[slot]

    def fetch(expert, s):
        return (pltpu.make_async_copy(w1_hbm.at[expert], w1f_ref.at[s], wsem.at[0, s]),
                pltpu.make_async_copy(w2_hbm.at[expert], w2f_ref.at[s], wsem.at[1, s]))

    @pl.when(i == 0)
    def _():
        par_ref[0] = 0
        for c in fetch(e, 0):
            c.start(priority=1)

    @pl.when((i == 0) | (e != e_prev))
    def _():
        s = par_ref[0]
        for c in fetch(e, s):
            c.wait()
        w1b_ref[...] = w1f_ref[s].astype(BF16)
        w2b_ref[...] = w2f_ref[s].astype(BF16)
        nxt = nx_ref[i]

        @pl.when(nxt >= 0)
        def _():
            for c in fetch(nxt, 1 - s):
                c.start(priority=1)

        par_ref[0] = 1 - s

    def ffn(x):
        n1 = w1b_ref.shape[1] // 2
        hs = []
        for c in range(2):
            hs.append(jnp.dot(x, w1b_ref[:, c * n1:(c + 1) * n1], preferred_element_type=F32)
                      + b1_ref[0][:, c * n1:(c + 1) * n1])
            gather_rows(i + 1, c * half, (c + 1) * half)
        x_glu = jnp.minimum(hs[0], SWIGLU_LIMIT)
        x_lin = jnp.clip(hs[1], -SWIGLU_LIMIT, SWIGLU_LIMIT)
        act = x_glu * _sigmoid(SWIGLU_ALPHA * x_glu) * (x_lin + 1.0)
        return jnp.dot(act.astype(BF16), w2b_ref[...], preferred_element_type=F32) + b2_ref[0]

    used = i < n_used
    nv = nv_ref[i]

    @pl.when(used & (nv > half))
    def _():
        _store_slabs(y_ref, ffn(_load_slabs(x_ref, tme, BF16)), tme)

    @pl.when(used & (nv <= half))
    def _():
        rows = half * SUBLANES
        _store_slabs(y_ref.at[pl.ds(0, rows)], ffn(_load_slabs(x_ref.at[pl.ds(0, rows)], half, BF16)), half)
        y_ref[pl.ds(rows, rows), :] = jnp.zeros((rows, LANES), F32)

    @pl.when(jnp.logical_not(used))
    def _():
        y_ref[...] = jnp.zeros(y_ref.shape, F32)

    @pl.when(i + 2 <= last)
    def _():
        idx_copy(i + 2).start()


def _experts(hn, tok_sorted, w1, b1, w2, b2, blk_exp, blk_nvalid, blk_off, blk_next, n_used, *, tme):
    n_exp, d, h2 = w1.shape
    assert h2 == 2 * w2.shape[1]
    d_ff = w2.shape[1]
    nb = blk_exp.shape[0]
    group = SUBLANES * LANES
    span = (-(-tme // group) + 1) * group
    n_entries = -(-tok_sorted.shape[0] // group) * group + span
    tok1 = jnp.pad(tok_sorted, (0, n_entries - tok_sorted.shape[0]))
    grid_spec = pltpu.PrefetchScalarGridSpec(
        num_scalar_prefetch=5,
        grid=(nb,),
        in_specs=[pl.BlockSpec(memory_space=pl.ANY),
                  pl.BlockSpec(memory_space=pl.ANY),
                  pl.BlockSpec(memory_space=pl.ANY),
                  pl.BlockSpec((1, 1, h2), lambda i, be, nv, nx, of, nu: (be[i], 0, 0)),
                  pl.BlockSpec(memory_space=pl.ANY),
                  pl.BlockSpec((1, 1, d), lambda i, be, nv, nx, of, nu: (be[i], 0, 0))],
        out_specs=pl.BlockSpec((tme * SUBLANES, LANES), lambda i, be, nv, nx, of, nu: (i, 0)),
        scratch_shapes=[pltpu.VMEM((2, tme * SUBLANES, LANES), F32), pltpu.SMEM((2 * span,), jnp.int32),
                        pltpu.VMEM((2, d, h2), F32), pltpu.VMEM((2, d_ff, d), F32),
                        pltpu.VMEM((d, h2), BF16), pltpu.VMEM((d_ff, d), BF16),
                        pltpu.SMEM((1,), jnp.int32), pltpu.SemaphoreType.DMA((2,)), pltpu.SemaphoreType.DMA((2,)),
                        pltpu.SemaphoreType.DMA((2, 2))],
    )
    return pl.pallas_call(
        functools.partial(_experts_kernel, tme=tme),
        grid_spec=grid_spec,
        out_shape=jax.ShapeDtypeStruct((nb * tme * SUBLANES, LANES), F32),
        compiler_params=_cparams(("arbitrary",)),
        name="experts",
    )(blk_exp, blk_nvalid, blk_next, blk_off, n_used, tok1, hn, w1, b1.reshape(n_exp, 1, h2), w2,
      b2.reshape(n_exp, 1, d))


def _combine_kernel(dcur_ref, dnext_ref, x1_ref, route_ref, g_ref, ys_ref, o_ref, buf_ref, sem, *, tc):
    i = pl.program_id(0)
    slot = i % 2

    def gather(dref, s):
        def issue(r, carry):
            row = pl.multiple_of(r * SUBLANES, SUBLANES)
            for k in range(TOP_K):
                d = pl.multiple_of(dref[0, 0, k * tc + r] * SUBLANES, SUBLANES)
                pltpu.make_async_copy(ys_ref.at[pl.ds(d, SUBLANES)], buf_ref.at[s, k, pl.ds(row, SUBLANES)],
                                      sem.at[s]).start(priority=k % 2)
            return carry

        lax.fori_loop(0, tc, issue, 0, unroll=8)

    @pl.when(i == 0)
    def _():
        gather(dcur_ref, 0)

    @pl.when(i + 1 < pl.num_programs(0))
    def _():
        gather(dnext_ref, 1 - slot)

    for k in range(TOP_K):
        pltpu.make_async_copy(ys_ref.at[pl.ds(0, tc * SUBLANES)], buf_ref.at[slot, k], sem.at[slot]).wait()
    route = route_ref[...]
    x1 = x1_ref[...]
    chunks = []
    for j in range(SUBLANES):
        acc = x1[:, j * LANES:(j + 1) * LANES]
        for k in range(TOP_K):
            acc = acc + (route[:, 2 * TOP_K + k:2 * TOP_K + k + 1]
                         * buf_ref[slot, k, pl.ds(j, tc, stride=SUBLANES), :])
        chunks.append(acc)
    y = jnp.concatenate(chunks, axis=1)
    ms = jnp.mean(y * y, axis=-1, keepdims=True)
    o_ref[...] = y * lax.rsqrt(ms + RMS_EPS) * g_ref[...]


def _combine(x1, route, dest, ys, g, *, first, n):
    d = x1.shape[1]
    tc = _tile(n, 128, 8)
    assert first % tc == 0
    steps, t0 = n // tc, first // tc
    dest3 = _dest_blocks(dest, tc)
    dspec = lambda m: pl.BlockSpec((1, 1, tc * TOP_K), m, memory_space=pltpu.SMEM)
    return pl.pallas_call(
        functools.partial(_combine_kernel, tc=tc),
        grid=(steps,),
        in_specs=[dspec(lambda i: (t0 + i, 0, 0)), dspec(lambda i: (t0 + jnp.minimum(i + 1, steps - 1), 0, 0)),
                  pl.BlockSpec((tc, d), lambda i: (t0 + i, 0)),
                  pl.BlockSpec((tc, LANES), lambda i: (t0 + i, 0)),
                  pl.BlockSpec((1, d), lambda i: (0, 0)),
                  pl.BlockSpec(memory_space=pl.ANY)],
        out_specs=pl.BlockSpec((tc, d), lambda i: (i, 0)),
        out_shape=jax.ShapeDtypeStruct((n, d), F32),
        scratch_shapes=[pltpu.VMEM((2, TOP_K, tc * SUBLANES, LANES), F32), pltpu.SemaphoreType.DMA((2,))],
        compiler_params=_cparams(("arbitrary",)),
        name="combine",
    )(dest3, dest3, x1, route, g, ys)


def _dest_rows(route_t, blk_start, tme):
    eidx = route_t[0:TOP_K].astype(jnp.int32)
    rank = route_t[TOP_K:2 * TOP_K].astype(jnp.int32)
    experts = jnp.arange(N_EXPERTS, dtype=jnp.int32)[:, None, None]
    first = jnp.sum(jnp.where(eidx[None] == experts, blk_start[:, None, None], 0), axis=0)
    return (first * tme + rank).astype(jnp.int32)


def _routing_tables(counts_f, *, tme, nb):
    counts = counts_f[:, 0].astype(jnp.int32)
    nblk = (counts + tme - 1) // tme
    blk_end = jnp.cumsum(nblk)
    blk_start = blk_end - nblk
    n_used = blk_end[-1]
    b = jnp.arange(nb, dtype=jnp.int32)
    used = b < n_used
    blk_exp = jnp.minimum(jnp.sum((b[:, None] >= blk_end[None, :]).astype(jnp.int32), axis=1), N_EXPERTS - 1)
    last_exp = jnp.max(jnp.where(nblk > 0, jnp.arange(N_EXPERTS, dtype=jnp.int32), 0))
    blk_exp = jnp.where(used, blk_exp, last_exp).astype(jnp.int32)
    experts = jnp.arange(N_EXPERTS, dtype=jnp.int32)
    mine = (b[:, None] >= blk_start[None, :]) & (b[:, None] < blk_end[None, :])
    inside = (b[:, None] - blk_start[None, :]) * tme
    nvalid = jnp.clip(jnp.sum(jnp.where(mine, counts[None, :] - inside, 0), axis=1), 0, tme).astype(jnp.int32)
    dense_start = jnp.cumsum(counts) - counts
    blk_off = jnp.sum(jnp.where(mine, dense_start[None, :] + inside, 0), axis=1).astype(jnp.int32)
    later = (experts[None, :] > experts[:, None]) & (nblk[None, :] > 0)
    nxt_e = jnp.min(jnp.where(later, experts[None, :], N_EXPERTS), axis=1)
    nxt_e = jnp.where(nxt_e == N_EXPERTS, -1, nxt_e)
    blk_next = jnp.sum(jnp.where(blk_exp[:, None] == experts[None, :], nxt_e[None, :], 0), axis=1).astype(jnp.int32)
    return blk_start, blk_exp, nvalid, blk_off, blk_next, n_used.reshape(1).astype(jnp.int32)


def _tokens_of_rows(dest):
    n_tok = dest.shape[1]
    tok = jnp.broadcast_to(jnp.arange(n_tok, dtype=jnp.int32)[None, :], dest.shape)
    _, tok_sorted = lax.sort((dest.reshape(-1), tok.reshape(-1)), num_keys=1)
    return tok_sorted


def kernel(x_prompt, x_sample, cache_k, cache_v, state_conv, attn_norm_g, w_in, attn_sinks, conv_w, conv_b,
           conv_ln_g, conv_ln_b, w_out, ffn_norm_g, router_w, router_b, w1, b1, w2, b2, final_norm_g):
    depth = w_in.shape[0]
    assert depth == 1, "single-layer step"
    bp, sp, d = x_prompt.shape
    bs, ss, _ = x_sample.shape
    cw = conv_w.shape[2]
    np_, ns = bp * sp, bs * ss
    n_tok = np_ + ns
    assert sp % WINDOW == 0

    xp2 = x_prompt.reshape(np_, d)
    xs2 = x_sample.reshape(ns, d)
    w_in_bf = w_in[0].astype(BF16)
    w_out_bf = w_out[0].astype(BF16)
    g_attn = attn_norm_g[0].reshape(1, d)
    g_ffn = ffn_norm_g[0].reshape(1, d)
    sinks = attn_sinks[0]
    vec = lambda a: a.reshape(1, cw)

    tab_p = _rope_tables(jnp.arange(sp, dtype=jnp.int32))
    tms = _tile(ns, 512, max(ss, 16))
    tab_s = _rope_tables(PAST_LEN + (jnp.arange(tms, dtype=jnp.int32) % ss))
    qp, kp, vp, kfp, vfp, ap = _in_proj(xp2, g_attn, w_in_bf, tab_p, seq_period=sp, q_dtype=BF16, conv_width=cw)
    qs, _, _, kfs, vfs, as_ = _in_proj(xs2, g_attn, w_in_bf, tab_s, seq_period=None, q_dtype=F32, conv_width=cw)

    att_p = _attn_prompt(qp, kp, vp, sinks, batch=bp, seq=sp)
    ck = cache_k[0].reshape(bs, WINDOW, KV_WIDTH)
    cv_ = cache_v[0].reshape(bs, WINDOW, KV_WIDTH)
    att_s, nk_s, nv_s = _attn_sample(qs, kfs, vfs, ck, cv_, sinks, batch=bs, ts=ss)

    cv_p = _conv_prompt(ap, conv_w[0], vec(conv_b[0]), vec(conv_ln_g[0]), vec(conv_ln_b[0]), batch=bp, seq=sp)
    cv_s = _conv_sample(as_, state_conv[0], conv_w[0], vec(conv_b[0]), vec(conv_ln_g[0]), vec(conv_ln_b[0]),
                        batch=bs, ts=ss)

    n_exp = router_w.shape[2]
    assert n_exp == N_EXPERTS
    rwt_bf = router_w[0].T.astype(BF16)
    rbt = jnp.broadcast_to(router_b[0][:, None], (n_exp, LANES))
    x1, hn, route, route_t, cnt = _outproj_router((xp2, att_p, cv_p), (xs2, att_s, cv_s), w_out_bf, g_ffn,
                                                  rwt_bf, rbt)

    tme = EXPERT_BLOCK_ROWS
    nb = -(-(n_tok * TOP_K + N_EXPERTS * (tme - 1)) // tme) + 1
    blk_start, blk_exp, blk_nvalid, blk_off, blk_next, n_used = _routing_tables(cnt, tme=tme, nb=nb)
    dest = _dest_rows(route_t, blk_start, tme)
    ys = _experts(hn, _tokens_of_rows(dest), w1[0], b1[0], w2[0], b2[0], blk_exp, blk_nvalid, blk_off, blk_next,
                  n_used, tme=tme)
    g_fin = final_norm_g.reshape(1, d)
    y_p = _combine(x1, route, dest, ys, g_fin, first=0, n=np_)
    y_s = _combine(x1, route, dest, ys, g_fin, first=np_, n=ns)

    kv5 = lambda t, bb: t.reshape(bb, -1, KV_WIDTH)[:, -WINDOW:].reshape(bb, WINDOW, N_KV_HEADS, HEAD_DIM)
    new_k_p = kv5(kfp, bp)[None]
    new_v_p = kv5(vfp, bp)[None]
    ctx = CONV_K - 1
    new_c_p = ap.reshape(bp, sp, cw)[:, -ctx:][None]
    new_c_s = jnp.concatenate([state_conv[0], as_.reshape(bs, ss, cw)], axis=1)[:, -ctx:][None]
    return (y_p.reshape(bp, sp, d), y_s.reshape(bs, ss, d), new_k_p, new_v_p, new_c_p,
            kv5(nk_s, bs)[None], kv5(nv_s, bs)[None], new_c_s)
```

```python
import functools

import jax
import jax.numpy as jnp
from jax import lax
from jax.experimental import pallas as pl
from jax.experimental.pallas import tpu as pltpu

F32 = jnp.float32
BF16 = jnp.bfloat16

HEAD_DIM = 64
N_Q_HEADS = 8
N_KV_HEADS = 2
WINDOW = 128
ROPE_THETA = 500000.0
ROPE_DIM = 16
CONV_K = 31
N_EXPERTS = 32
TOP_K = 4
SWIGLU_LIMIT = 7.0
SWIGLU_ALPHA = 1.702
RMS_EPS = 1e-5
LN_EPS = 1e-5
PAST_LEN = 16384

LANES = 128
SUBLANES = 8
CONV_HALO = 32
VMEM_LIMIT = 56 * 1024 * 1024
EXPERT_BLOCK_ROWS = 512

ATTN_WIDTH = N_Q_HEADS * HEAD_DIM
KV_WIDTH = N_KV_HEADS * HEAD_DIM


def _tile(n, pref, mult=8):
    t = min(pref, n)
    while t > 0 and (n % t or t % mult):
        t -= 1
    assert t > 0, (n, pref, mult)
    return t


def _cparams(sem):
    return pltpu.CompilerParams(dimension_semantics=sem, vmem_limit_bytes=VMEM_LIMIT)


def _sigmoid(x):
    return 1.0 / (1.0 + jnp.exp(-x))


def _rope_tables(pos):
    half = ROPE_DIM // 2
    inv_freq = jnp.power(jnp.float32(ROPE_THETA), -jnp.arange(half, dtype=F32) * 2.0 / ROPE_DIM)
    ang = pos.astype(F32)[:, None] * inv_freq[None, :]
    cos, sin = jnp.cos(ang), jnp.sin(ang)
    l64 = jnp.arange(LANES) % HEAD_DIM
    f = l64 % half
    cos_l, sin_l = cos[:, f], sin[:, f]
    c = jnp.where(l64 < ROPE_DIM, cos_l, 1.0)
    s1 = jnp.where(l64 < half, -sin_l, 0.0)
    s2 = jnp.where((l64 >= half) & (l64 < ROPE_DIM), sin_l, 0.0)
    return c.astype(F32), s1.astype(F32), s2.astype(F32)


def _inproj_kernel(x_ref, g_ref, w_ref, c_ref, s1_ref, s2_ref,
                   q_ref, k_ref, v_ref, kf_ref, vf_ref, a_ref, *, conv_width):
    x = x_ref[...]
    ms = jnp.mean(x * x, axis=-1, keepdims=True)
    h = (x * lax.rsqrt(ms + RMS_EPS) * g_ref[...]).astype(BF16)
    z = jnp.dot(h, w_ref[...], preferred_element_type=F32)
    c, s1, s2 = c_ref[...], s1_ref[...], s2_ref[...]
    half = ROPE_DIM // 2

    def rot(t):
        return t * c + pltpu.roll(t, LANES - half, 1) * s1 + pltpu.roll(t, half, 1) * s2

    scale = HEAD_DIM ** -0.5
    for j in range(ATTN_WIDTH // LANES):
        q_ref[:, j * LANES:(j + 1) * LANES] = (rot(z[:, j * LANES:(j + 1) * LANES]) * scale).astype(q_ref.dtype)
    k0 = ATTN_WIDTH
    kr = rot(z[:, k0:k0 + KV_WIDTH])
    k_ref[...] = kr.astype(BF16)
    kf_ref[...] = kr
    v0 = k0 + KV_WIDTH
    vv = z[:, v0:v0 + KV_WIDTH]
    v_ref[...] = vv.astype(BF16)
    vf_ref[...] = vv
    u0 = v0 + KV_WIDTH
    g0 = u0 + conv_width
    a_ref[...] = z[:, u0:g0] * _sigmoid(z[:, g0:g0 + conv_width])


def _in_proj(x2, g, w_bf, tables, *, seq_period, q_dtype, conv_width):
    n, d = x2.shape
    in_w = w_bf.shape[1]
    if seq_period is None:
        tm = tables[0].shape[0]
        tmap = lambda i: (0, 0)
    else:
        tm = _tile(seq_period, 1024, 16)
        per = seq_period // tm
        tmap = lambda i: (i % per, 0)
    assert n % tm == 0
    row = lambda w: pl.BlockSpec((tm, w), lambda i: (i, 0))
    tab = pl.BlockSpec((tm, LANES), tmap)
    return pl.pallas_call(
        functools.partial(_inproj_kernel, conv_width=conv_width),
        grid=(n // tm,),
        in_specs=[row(d), pl.BlockSpec((1, d), lambda i: (0, 0)),
                  pl.BlockSpec((d, in_w), lambda i: (0, 0)), tab, tab, tab],
        out_specs=[row(ATTN_WIDTH), row(KV_WIDTH), row(KV_WIDTH), row(KV_WIDTH), row(KV_WIDTH),
                   row(conv_width)],
        out_shape=[jax.ShapeDtypeStruct((n, ATTN_WIDTH), q_dtype),
                   jax.ShapeDtypeStruct((n, KV_WIDTH), BF16),
                   jax.ShapeDtypeStruct((n, KV_WIDTH), BF16),
                   jax.ShapeDtypeStruct((n, KV_WIDTH), F32),
                   jax.ShapeDtypeStruct((n, KV_WIDTH), F32),
                   jax.ShapeDtypeStruct((n, conv_width), F32)],
        compiler_params=_cparams(("parallel",)),
        name="in_proj",
    )(x2, g, w_bf, *tables)


def _dup_head(t, h):
    sw = pltpu.roll(t, HEAD_DIM, 1)
    low = lax.broadcasted_iota(jnp.int32, t.shape, 1) < HEAD_DIM
    return jnp.where(low, t, sw) if h == 0 else jnp.where(low, sw, t)


def _nt_dot(a, b):
    return lax.dot_general(a, b, (((1,), (1,)), ((), ())), preferred_element_type=F32)


def _attn_prompt_kernel(sink_ref, q_ref, kc_ref, kp_ref, vc_ref, vp_ref, o_ref, *, qb):
    j = pl.program_id(1)
    w = WINDOW
    k_all = jnp.concatenate([kp_ref[...], kc_ref[...]], axis=0).astype(F32)
    v_all = jnp.concatenate([vp_ref[...], vc_ref[...]], axis=0).astype(F32)
    r = lax.broadcasted_iota(jnp.int32, (w, 2 * w), 0)
    kk = lax.broadcasted_iota(jnp.int32, (w, 2 * w), 1)
    band = (kk > r) & (kk <= r + w)
    low = lax.broadcasted_iota(jnp.int32, (w, LANES), 1) < HEAD_DIM
    zero = jnp.zeros((w, LANES), BF16)
    group = N_Q_HEADS // N_KV_HEADS
    for h in range(N_KV_HEADS):
        kd_all = _dup_head(k_all, h).astype(BF16)
        vd_all = _dup_head(v_all, h).astype(BF16)
        for sub in range(qb):
            valid = band & ((kk >= w) | (j > 0)) if sub == 0 else band
            kd = kd_all[sub * w:(sub + 2) * w, :]
            vd = vd_all[sub * w:(sub + 2) * w, :]
            for jj in range(group // 2):
                col = (h * group // 2 + jj) * LANES
                qv = q_ref[sub * w:(sub + 1) * w, col:col + LANES]
                halves = []
                for half in range(2):
                    head = h * group + jj * 2 + half
                    qm = jnp.where(low if half == 0 else ~low, qv, zero)
                    s = jnp.where(valid, _nt_dot(qm, kd), -jnp.inf)
                    sink = sink_ref[head]
                    m = jnp.maximum(jnp.max(s, axis=-1, keepdims=True), sink)
                    p = jnp.exp(s - m)
                    den = jnp.sum(p, axis=-1, keepdims=True) + jnp.exp(sink - m)
                    o = jnp.dot(p.astype(BF16), vd, preferred_element_type=F32)
                    halves.append(o / den)
                o_ref[sub * w:(sub + 1) * w, col:col + LANES] = (
                    jnp.where(low, halves[0], halves[1]).astype(o_ref.dtype))


def _attn_prompt(q, k, v, sinks, *, batch, seq):
    nb = seq // WINDOW
    qb = next(c for c in (4, 2, 1) if nb % c == 0)
    steps = nb // qb
    cur = lambda b, j: (b * steps + j, 0)
    prev = lambda b, j: (b * nb + jnp.maximum(j * qb - 1, 0), 0)
    return pl.pallas_call(
        functools.partial(_attn_prompt_kernel, qb=qb),
        grid=(batch, steps),
        in_specs=[pl.BlockSpec(memory_space=pltpu.SMEM),
                  pl.BlockSpec((qb * WINDOW, ATTN_WIDTH), cur),
                  pl.BlockSpec((qb * WINDOW, KV_WIDTH), cur), pl.BlockSpec((WINDOW, KV_WIDTH), prev),
                  pl.BlockSpec((qb * WINDOW, KV_WIDTH), cur), pl.BlockSpec((WINDOW, KV_WIDTH), prev)],
        out_specs=pl.BlockSpec((qb * WINDOW, ATTN_WIDTH), cur),
        out_shape=jax.ShapeDtypeStruct((batch * seq, ATTN_WIDTH), BF16),
        compiler_params=_cparams(("parallel", "parallel")),
        name="attn_prompt",
    )(sinks, q, k, k, v, v)


def _attn_sample_kernel(sink_ref, q_ref, kn_ref, vn_ref, ck_ref, cv_ref, o_ref, nk_ref, nv_ref, *, gb, ts):
    w = WINDOW
    group = N_Q_HEADS // N_KV_HEADS
    rows = group * ts
    low = lax.broadcasted_iota(jnp.int32, (ts, LANES), 1) < HEAD_DIM
    pad = jnp.zeros((ts, LANES), F32)
    s_c, s_n, v_dup = [], [], []
    for b in range(gb):
        kc, vc = ck_ref[b], cv_ref[b]
        kn, vn = kn_ref[b * ts:(b + 1) * ts, :], vn_ref[b * ts:(b + 1) * ts, :]
        nk_ref[b, 0:w - ts, :] = kc[ts:, :]
        nk_ref[b, w - ts:, :] = kn
        nv_ref[b, 0:w - ts, :] = vc[ts:, :]
        nv_ref[b, w - ts:, :] = vn
        knp = jnp.concatenate([kn, pad], axis=0)
        vnp = jnp.concatenate([vn, pad], axis=0)
        qb = q_ref[b * ts:(b + 1) * ts, :]
        for h in range(N_KV_HEADS):
            parts = []
            for jj in range(group // 2):
                col = (h * group // 2 + jj) * LANES
                qv = qb[:, col:col + LANES]
                parts += [jnp.where(low, qv, 0.0), jnp.where(low, 0.0, qv)]
            lhs = jnp.concatenate(parts, axis=0).astype(BF16)
            s_c.append(_nt_dot(lhs, _dup_head(kc, h).astype(BF16)))
            s_n.append(_nt_dot(lhs, _dup_head(knp, h).astype(BF16)))
            v_dup.append((_dup_head(vc, h).astype(BF16), _dup_head(vnp, h).astype(BF16)))
    s_c = jnp.concatenate(s_c, axis=0)
    s_n = jnp.concatenate(s_n, axis=0)
    n_rows = s_c.shape[0]
    ridx = lax.broadcasted_iota(jnp.int32, (n_rows, 1), 0)
    t_row = ridx % ts
    head_row = (ridx // ts) % N_Q_HEADS
    sink = jnp.zeros((n_rows, 1), F32)
    for hd in range(N_Q_HEADS):
        sink = jnp.where(head_row == hd, sink_ref[hd], sink)
    c_idx = lax.broadcasted_iota(jnp.int32, (n_rows, w), 1)
    n_idx = lax.broadcasted_iota(jnp.int32, (n_rows, 2 * ts), 1)
    s_c = jnp.where(c_idx > t_row, s_c, -jnp.inf)
    s_n = jnp.where(n_idx <= t_row, s_n, -jnp.inf)
    m = jnp.maximum(jnp.maximum(jnp.max(s_c, axis=-1, keepdims=True), jnp.max(s_n, axis=-1, keepdims=True)), sink)
    p_c = jnp.exp(s_c - m)
    p_n = jnp.exp(s_n - m)
    den = jnp.sum(p_c, axis=-1, keepdims=True) + jnp.sum(p_n, axis=-1, keepdims=True) + jnp.exp(sink - m)
    p_c = p_c.astype(BF16)
    p_n = p_n.astype(BF16)
    outs = []
    for b in range(gb):
        cols = []
        for h in range(N_KV_HEADS):
            ci = b * N_KV_HEADS + h
            sl = slice(ci * rows, (ci + 1) * rows)
            vdc, vdn = v_dup[ci]
            o = (jnp.dot(p_c[sl], vdc, preferred_element_type=F32)
                 + jnp.dot(p_n[sl], vdn, preferred_element_type=F32)) / den[sl]
            for jj in range(group // 2):
                lo_part = o[(2 * jj) * ts:(2 * jj + 1) * ts, :]
                hi_part = o[(2 * jj + 1) * ts:(2 * jj + 2) * ts, :]
                cols.append(jnp.where(low, lo_part, hi_part))
        outs.append(jnp.concatenate(cols, axis=1))
    o_ref[...] = jnp.concatenate(outs, axis=0).astype(o_ref.dtype)


def _attn_sample(q, kf, vf, cache_k, cache_v, sinks, *, batch, ts):
    assert ts % 8 == 0 and ts <= WINDOW
    gb = _tile(batch, 8, 2)
    tok = lambda w: pl.BlockSpec((gb * ts, w), lambda i: (i, 0))
    cache = pl.BlockSpec((gb, WINDOW, KV_WIDTH), lambda i: (i, 0, 0))
    cshape = jax.ShapeDtypeStruct((batch, WINDOW, KV_WIDTH), F32)
    return pl.pallas_call(
        functools.partial(_attn_sample_kernel, gb=gb, ts=ts),
        grid=(batch // gb,),
        in_specs=[pl.BlockSpec(memory_space=pltpu.SMEM), tok(ATTN_WIDTH), tok(KV_WIDTH), tok(KV_WIDTH),
                  cache, cache],
        out_specs=[tok(ATTN_WIDTH), cache, cache],
        out_shape=[jax.ShapeDtypeStruct((batch * ts, ATTN_WIDTH), BF16), cshape, cshape],
        compiler_params=_cparams(("parallel",)),
        name="attn_sample",
    )(sinks, q, kf, vf, cache_k, cache_v)


def _ln_swish(acc, b, lg, lb):
    y = acc + b
    mu = jnp.mean(y, axis=-1, keepdims=True)
    yc = y - mu
    var = jnp.mean(yc * yc, axis=-1, keepdims=True)
    yn = yc * lax.rsqrt(var + LN_EPS) * lg + lb
    return yn * _sigmoid(yn)


def _conv_prompt_kernel(a_ref, ap_ref, w_ref, b_ref, lg_ref, lb_ref, o_ref, win_ref, *, tt, rc):
    j = pl.program_id(1)
    n = CONV_HALO + tt
    win = jnp.concatenate([jnp.where(j > 0, ap_ref[...], 0.0), a_ref[...]], axis=0)
    win_ref[0] = win
    for r in range(1, SUBLANES):
        win_ref[r] = pltpu.roll(win, n - r, 0)
    off = CONV_HALO - (CONV_K - 1)
    b, lg, lb = b_ref[...], lg_ref[...], lb_ref[...]
    for c in range(tt // rc):
        acc = jnp.zeros((rc, a_ref.shape[1]), F32)
        for k in range(CONV_K):
            s = off + k
            base = c * rc + (s // SUBLANES) * SUBLANES
            wk = jnp.concatenate([w_ref[k]] * (rc // SUBLANES), axis=0)
            acc = acc + wk * win_ref[s % SUBLANES, base:base + rc, :]
        o_ref[c * rc:(c + 1) * rc, :] = _ln_swish(acc, b, lg, lb).astype(o_ref.dtype)


def _conv_prompt(a, w, b, lg, lb, *, batch, seq):
    cw = a.shape[1]
    tt = _tile(seq, 512, CONV_HALO)
    rc = _tile(tt, 32, 16)
    nt = seq // tt
    per = tt // CONV_HALO
    cur = lambda bb, j: (bb * nt + j, 0)
    prev = lambda bb, j: (jnp.maximum((bb * nt + j) * per - 1, 0), 0)
    vec = pl.BlockSpec((1, cw), lambda bb, j: (0, 0))
    return pl.pallas_call(
        functools.partial(_conv_prompt_kernel, tt=tt, rc=rc),
        grid=(batch, nt),
        in_specs=[pl.BlockSpec((tt, cw), cur), pl.BlockSpec((CONV_HALO, cw), prev),
                  pl.BlockSpec((CONV_K, SUBLANES, cw), lambda bb, j: (0, 0, 0)), vec, vec, vec],
        out_specs=pl.BlockSpec((tt, cw), cur),
        out_shape=jax.ShapeDtypeStruct((batch * seq, cw), BF16),
        scratch_shapes=[pltpu.VMEM((SUBLANES, CONV_HALO + tt, cw), F32)],
        compiler_params=_cparams(("parallel", "parallel")),
        name="conv_prompt",
    )(a, a, jnp.broadcast_to(w[:, None, :], (CONV_K, SUBLANES, cw)), b, lg, lb)


def _conv_sample_kernel(a_ref, st_ref, w_ref, b_ref, lg_ref, lb_ref, o_ref, win_ref, *, gb, ts):
    ctx = CONV_K - 1
    b, lg, lb = b_ref[...], lg_ref[...], lb_ref[...]
    for bb in range(gb):
        win_ref[bb, 0:ctx, :] = st_ref[bb]
        win_ref[bb, ctx:ctx + ts, :] = a_ref[bb * ts:(bb + 1) * ts, :]
    outs = []
    for bb in range(gb):
        acc = jnp.zeros((ts, a_ref.shape[1]), F32)
        for k in range(CONV_K):
            wk = jnp.concatenate([w_ref[k]] * (ts // SUBLANES), axis=0)
            acc = acc + wk * win_ref[bb, k:k + ts, :]
        outs.append(_ln_swish(acc, b, lg, lb))
    o_ref[...] = jnp.concatenate(outs, axis=0).astype(o_ref.dtype)


def _conv_sample(a, state, w, b, lg, lb, *, batch, ts):
    cw = a.shape[1]
    ctx = CONV_K - 1
    gb = _tile(batch, 8, 2)
    vec = pl.BlockSpec((1, cw), lambda i: (0, 0))
    return pl.pallas_call(
        functools.partial(_conv_sample_kernel, gb=gb, ts=ts),
        grid=(batch // gb,),
        in_specs=[pl.BlockSpec((gb * ts, cw), lambda i: (i, 0)),
                  pl.BlockSpec((gb, ctx, cw), lambda i: (i, 0, 0)),
                  pl.BlockSpec((CONV_K, SUBLANES, cw), lambda i: (0, 0, 0)), vec, vec, vec],
        out_specs=pl.BlockSpec((gb * ts, cw), lambda i: (i, 0)),
        out_shape=jax.ShapeDtypeStruct((batch * ts, cw), BF16),
        scratch_shapes=[pltpu.VMEM((gb, ctx + ts + 2, cw), F32)],
        compiler_params=_cparams(("parallel",)),
        name="conv_sample",
    )(a, state, jnp.broadcast_to(w[:, None, :], (CONV_K, SUBLANES, cw)), b, lg, lb)


def _outproj_router_kernel(x_ref, att_ref, cv_ref, wo_ref, g_ref, rw_ref, rb_ref, cin_ref,
                           x1_ref, hn_ref, route_ref, rt_ref, cnt_ref, carry_ref, *, ts):
    i = pl.program_id(0)

    @pl.when(i == 0)
    def _():
        carry_ref[...] = cin_ref[...]

    aw = att_ref.shape[1]
    n_exp = rw_ref.shape[0]
    r = lax.broadcasted_iota(jnp.int32, (ts, ts), 0)
    c = lax.broadcasted_iota(jnp.int32, (ts, ts), 1)
    before = (r < c).astype(BF16)
    eid = lax.broadcasted_iota(jnp.int32, (n_exp, ts), 0)
    carry = carry_ref[...][:, 0:1]
    for s in range(x_ref.shape[0] // ts):
        rs = slice(s * ts, (s + 1) * ts)
        mix = (jnp.dot(att_ref[rs, :], wo_ref[0:aw, :], preferred_element_type=F32)
               + jnp.dot(cv_ref[rs, :], wo_ref[aw:, :], preferred_element_type=F32))
        x1 = x_ref[rs, :] + mix
        x1_ref[rs, :] = x1
        ms = jnp.mean(x1 * x1, axis=-1, keepdims=True)
        hn = x1 * lax.rsqrt(ms + RMS_EPS) * g_ref[...]
        _store_slabs(hn_ref.at[pl.ds(s * ts * SUBLANES, ts * SUBLANES)], hn, ts)
        logits = _nt_dot(rw_ref[...], hn.astype(BF16)) + rb_ref[...][:, 0:1]
        onehot = jnp.zeros((n_exp, ts), F32)
        vals, idxs = [], []
        for _ in range(TOP_K):
            m = jnp.max(logits, axis=0, keepdims=True)
            idx = jnp.min(jnp.where(logits == m, eid, n_exp), axis=0, keepdims=True)
            sel = eid == idx
            onehot = onehot + sel.astype(F32)
            logits = jnp.where(sel, -jnp.inf, logits)
            vals.append(m)
            idxs.append(idx)
        es = [jnp.exp(v - vals[0]) for v in vals]
        den = es[0] + es[1] + es[2] + es[3]
        prefix = jnp.dot(onehot.astype(BF16), before, preferred_element_type=F32) + carry
        ranks = [jnp.sum(jnp.where(eid == idxs[k], prefix, 0.0), axis=0, keepdims=True) for k in range(TOP_K)]
        rows = [ix.astype(F32) for ix in idxs] + ranks
        rt_ref[:, rs] = jnp.concatenate(rows, axis=0)
        fields = jnp.concatenate(rows + [e / den for e in es] + [jnp.zeros((LANES - 3 * TOP_K, ts), F32)],
                                 axis=0)
        route_ref[rs, :] = jnp.transpose(fields)
        carry = carry + jnp.sum(onehot, axis=1, keepdims=True)
    carry_ref[...] = jnp.broadcast_to(carry, carry_ref.shape)
    cnt_ref[...] = jnp.broadcast_to(carry, cnt_ref.shape)


def _store_slabs(ref, val, rows):
    for j in range(val.shape[1] // LANES):
        ref[pl.ds(j, rows, stride=SUBLANES), :] = val[:, j * LANES:(j + 1) * LANES]


def _load_slabs(ref, rows, dtype):
    return jnp.concatenate([ref[pl.ds(j, rows, stride=SUBLANES), :].astype(dtype) for j in range(SUBLANES)],
                           axis=1)


def _outproj_router(x2, att, cv, wo_bf, g, rwt_bf, rbt, carry_in):
    n, d = x2.shape
    assert d == SUBLANES * LANES
    n_exp = rwt_bf.shape[0]
    tm = _tile(n, 512, 16)
    ts = _tile(tm, 512, 16)
    row = lambda w: pl.BlockSpec((tm, w), lambda i: (i, 0))
    full = lambda a: pl.BlockSpec(a.shape, lambda i: (0,) * a.ndim)
    return pl.pallas_call(
        functools.partial(_outproj_router_kernel, ts=ts),
        grid=(n // tm,),
        in_specs=[row(d), row(att.shape[1]), row(cv.shape[1]), full(wo_bf), full(g), full(rwt_bf), full(rbt),
                  full(carry_in)],
        out_specs=[row(d), pl.BlockSpec((tm * SUBLANES, LANES), lambda i: (i, 0)), row(LANES),
                   pl.BlockSpec((2 * TOP_K, tm), lambda i: (0, i)),
                   pl.BlockSpec((n_exp, LANES), lambda i: (0, 0))],
        out_shape=[jax.ShapeDtypeStruct((n, d), F32), jax.ShapeDtypeStruct((n * SUBLANES, LANES), F32),
                   jax.ShapeDtypeStruct((n, LANES), F32), jax.ShapeDtypeStruct((2 * TOP_K, n), F32),
                   jax.ShapeDtypeStruct((n_exp, LANES), F32)],
        scratch_shapes=[pltpu.VMEM((n_exp, LANES), F32)],
        compiler_params=_cparams(("arbitrary",)),
        name="outproj_router",
    )(x2, att, cv, wo_bf, g, rwt_bf, rbt, carry_in)


def _dispatch_kernel(fill_ref, nfill_ref, dest_ref, hn_ref, hn2_ref, xs_ref, zero_ref, sem, zsem,
                     *, td, tme, n_first):
    def fill(f, s):
        row = pl.multiple_of(fill_ref[f] * (tme * SUBLANES), tme * SUBLANES)
        return pltpu.make_async_copy(zero_ref, xs_ref.at[pl.ds(row, tme * SUBLANES)], zsem.at[s])

    def fills(lo, hi, s, act):
        def body(f, carry):
            act(fill(f, s))
            return carry

        lax.fori_loop(lo, hi, body, 0)

    @pl.when(pl.program_id(0) == 0)
    def _():
        zero_ref[...] = jnp.zeros(zero_ref.shape, F32)
        fills(0, nfill_ref[0], 0, lambda c: c.start())
        fills(nfill_ref[0], nfill_ref[1], 1, lambda c: c.start())
        fills(0, nfill_ref[0], 0, lambda c: c.wait())

    def scatter_rows(src_ref):
        def issue(r, carry):
            src = src_ref.at[pl.ds(pl.multiple_of(r * SUBLANES, SUBLANES), SUBLANES)]
            for k in range(TOP_K):
                d = pl.multiple_of(dest_ref[0, 0, k * td + r] * SUBLANES, SUBLANES)
                pltpu.make_async_copy(src, xs_ref.at[pl.ds(d, SUBLANES)], sem).start(priority=k % 2)
            return carry

        lax.fori_loop(0, td, issue, 0, unroll=8)
        for _ in range(TOP_K):
            pltpu.make_async_copy(src_ref, xs_ref.at[pl.ds(0, td * SUBLANES)], sem).wait()

    @pl.when(pl.program_id(0) < n_first)
    def _():
        scatter_rows(hn_ref)

    @pl.when(pl.program_id(0) >= n_first)
    def _():
        scatter_rows(hn2_ref)

    @pl.when(pl.program_id(0) == pl.num_programs(0) - 1)
    def _():
        fills(nfill_ref[0], nfill_ref[1], 1, lambda c: c.wait())


def _dest_blocks(dest, td):
    n = dest.shape[1]
    return dest.reshape(TOP_K, n // td, td).transpose(1, 0, 2).reshape(n // td, 1, TOP_K * td)


def _dispatch(hn_a, hn_b, dest, fill_blocks, n_fill, *, nb, tme):
    na = hn_a.shape[0] // SUBLANES
    nb_rows = hn_b.shape[0] // SUBLANES
    td = _tile(nb_rows, 256, 8)
    assert na % td == 0
    n_first = na // td
    steps = n_first + nb_rows // td
    slab = lambda m: pl.BlockSpec((td * SUBLANES, LANES), m)
    grid_spec = pltpu.PrefetchScalarGridSpec(
        num_scalar_prefetch=2,
        grid=(steps,),
        in_specs=[pl.BlockSpec((1, 1, td * TOP_K), lambda i, fb, nf: (i, 0, 0), memory_space=pltpu.SMEM),
                  slab(lambda i, fb, nf: (jnp.minimum(i, n_first - 1), 0)),
                  slab(lambda i, fb, nf: (jnp.maximum(i - n_first, 0), 0))],
        out_specs=pl.BlockSpec(memory_space=pl.ANY),
        scratch_shapes=[pltpu.VMEM((tme * SUBLANES, LANES), F32), pltpu.SemaphoreType.DMA(()),
                        pltpu.SemaphoreType.DMA((2,))],
    )
    return pl.pallas_call(
        functools.partial(_dispatch_kernel, td=td, tme=tme, n_first=n_first),
        grid_spec=grid_spec,
        out_shape=jax.ShapeDtypeStruct((nb * tme * SUBLANES, LANES), F32),
        compiler_params=_cparams(("arbitrary",)),
        name="dispatch",
    )(fill_blocks, n_fill, _dest_blocks(dest, td), hn_a, hn_b)


def _experts_kernel(be_ref, bsrc_ref, nv_ref, nx_ref, nu_ref, x_ref, w1_hbm, b1_ref, w2_hbm, b2_ref,
                    y_ref, w1f_ref, w2f_ref, w1b_ref, w2b_ref, par_ref, wsem, *, tme):
    i = pl.program_id(0)
    e = be_ref[i]
    e_prev = be_ref[jnp.maximum(i - 1, 0)]
    d_ff = w2f_ref.shape[1]
    half = tme // 2

    def fetch(expert, s):
        return (pltpu.make_async_copy(w1_hbm.at[expert], w1f_ref.at[s], wsem.at[0, s]),
                pltpu.make_async_copy(w2_hbm.at[expert], w2f_ref.at[s], wsem.at[1, s]))

    @pl.when(i == 0)
    def _():
        par_ref[0] = 0
        for c in fetch(e, 0):
            c.start()

    @pl.when((i == 0) | (e != e_prev))
    def _():
        s = par_ref[0]
        for c in fetch(e, s):
            c.wait()
        w1b_ref[...] = w1f_ref[s].astype(BF16)
        w2b_ref[...] = w2f_ref[s].astype(BF16)
        nxt = nx_ref[i]

        @pl.when(nxt >= 0)
        def _():
            for c in fetch(nxt, 1 - s):
                c.start()

        par_ref[0] = 1 - s

    def ffn(x):
        h = jnp.dot(x, w1b_ref[...], preferred_element_type=F32) + b1_ref[0]
        x_glu = jnp.minimum(h[:, :d_ff], SWIGLU_LIMIT)
        x_lin = jnp.clip(h[:, d_ff:], -SWIGLU_LIMIT, SWIGLU_LIMIT)
        act = x_glu * _sigmoid(SWIGLU_ALPHA * x_glu) * (x_lin + 1.0)
        return jnp.dot(act.astype(BF16), w2b_ref[...], preferred_element_type=F32) + b2_ref[0]

    used = i < nu_ref[0]
    nv = nv_ref[i]

    @pl.when(used & (nv > half))
    def _():
        _store_slabs(y_ref, ffn(_load_slabs(x_ref, tme, BF16)), tme)

    @pl.when(used & (nv <= half))
    def _():
        rows = half * SUBLANES
        _store_slabs(y_ref.at[pl.ds(0, rows)], ffn(_load_slabs(x_ref.at[pl.ds(0, rows)], half, BF16)), half)
        y_ref[pl.ds(rows, rows), :] = jnp.zeros((rows, LANES), F32)

    @pl.when(i == nu_ref[0])
    def _():
        y_ref[...] = jnp.zeros(y_ref.shape, F32)


def _experts(xs, w1, b1, w2, b2, blk_exp, blk_src, blk_nvalid, blk_next, n_used, *, tme):
    n_exp, d, h2 = w1.shape
    d_ff = w2.shape[1]
    nb = xs.shape[0] // (tme * SUBLANES)
    slab = lambda m: pl.BlockSpec((tme * SUBLANES, LANES), m)
    grid_spec = pltpu.PrefetchScalarGridSpec(
        num_scalar_prefetch=5,
        grid=(nb,),
        in_specs=[slab(lambda i, be, bs, nv, nx, nu: (bs[i], 0)),
                  pl.BlockSpec(memory_space=pl.ANY),
                  pl.BlockSpec((1, 1, h2), lambda i, be, bs, nv, nx, nu: (be[i], 0, 0)),
                  pl.BlockSpec(memory_space=pl.ANY),
                  pl.BlockSpec((1, 1, d), lambda i, be, bs, nv, nx, nu: (be[i], 0, 0))],
        out_specs=slab(lambda i, be, bs, nv, nx, nu: (jnp.minimum(i, nu[0]), 0)),
        scratch_shapes=[pltpu.VMEM((2, d, h2), F32), pltpu.VMEM((2, d_ff, d), F32),
                        pltpu.VMEM((d, h2), BF16), pltpu.VMEM((d_ff, d), BF16),
                        pltpu.SMEM((1,), jnp.int32), pltpu.SemaphoreType.DMA((2, 2))],
    )
    return pl.pallas_call(
        functools.partial(_experts_kernel, tme=tme),
        grid_spec=grid_spec,
        out_shape=jax.ShapeDtypeStruct(xs.shape, F32),
        input_output_aliases={5: 0},
        compiler_params=_cparams(("arbitrary",)),
        name="experts",
    )(blk_exp, blk_src, blk_nvalid, blk_next, n_used, xs, w1, b1.reshape(n_exp, 1, h2), w2,
      b2.reshape(n_exp, 1, d))


def _combine_kernel(dcur_ref, dnext_ref, x1_ref, route_ref, g_ref, ys_ref, o_ref, buf_ref, sem, *, tc):
    i = pl.program_id(0)
    slot = i % 2

    def gather(dref, s):
        def issue(r, carry):
            row = pl.multiple_of(r * SUBLANES, SUBLANES)
            for k in range(TOP_K):
                d = pl.multiple_of(dref[0, 0, k * tc + r] * SUBLANES, SUBLANES)
                pltpu.make_async_copy(ys_ref.at[pl.ds(d, SUBLANES)], buf_ref.at[s, k, pl.ds(row, SUBLANES)],
                                      sem.at[s]).start(priority=k % 2)
            return carry

        lax.fori_loop(0, tc, issue, 0, unroll=8)

    @pl.when(i == 0)
    def _():
        gather(dcur_ref, 0)

    @pl.when(i + 1 < pl.num_programs(0))
    def _():
        gather(dnext_ref, 1 - slot)

    for k in range(TOP_K):
        pltpu.make_async_copy(ys_ref.at[pl.ds(0, tc * SUBLANES)], buf_ref.at[slot, k], sem.at[slot]).wait()
    route = route_ref[...]
    x1 = x1_ref[...]
    chunks = []
    for j in range(SUBLANES):
        acc = x1[:, j * LANES:(j + 1) * LANES]
        for k in range(TOP_K):
            acc = acc + (route[:, 2 * TOP_K + k:2 * TOP_K + k + 1]
                         * buf_ref[slot, k, pl.ds(j, tc, stride=SUBLANES), :])
        chunks.append(acc)
    y = jnp.concatenate(chunks, axis=1)
    ms = jnp.mean(y * y, axis=-1, keepdims=True)
    o_ref[...] = y * lax.rsqrt(ms + RMS_EPS) * g_ref[...]


def _combine(x1, route, dest, ys, g):
    n, d = x1.shape
    tc = _tile(n, 128, 8)
    steps = n // tc
    dest3 = _dest_blocks(dest, tc)
    dspec = lambda m: pl.BlockSpec((1, 1, tc * TOP_K), m, memory_space=pltpu.SMEM)
    return pl.pallas_call(
        functools.partial(_combine_kernel, tc=tc),
        grid=(steps,),
        in_specs=[dspec(lambda i: (i, 0, 0)), dspec(lambda i: (jnp.minimum(i + 1, steps - 1), 0, 0)),
                  pl.BlockSpec((tc, d), lambda i: (i, 0)),
                  pl.BlockSpec((tc, LANES), lambda i: (i, 0)),
                  pl.BlockSpec((1, d), lambda i: (0, 0)),
                  pl.BlockSpec(memory_space=pl.ANY)],
        out_specs=pl.BlockSpec((tc, d), lambda i: (i, 0)),
        out_shape=jax.ShapeDtypeStruct((n, d), F32),
        scratch_shapes=[pltpu.VMEM((2, TOP_K, tc * SUBLANES, LANES), F32), pltpu.SemaphoreType.DMA((2,))],
        compiler_params=_cparams(("arbitrary",)),
        name="combine",
    )(dest3, dest3, x1, route, g, ys)


def _dest_rows(route_t, blk_start, tme):
    eidx = route_t[0:TOP_K].astype(jnp.int32)
    rank = route_t[TOP_K:2 * TOP_K].astype(jnp.int32)
    experts = jnp.arange(N_EXPERTS, dtype=jnp.int32)[:, None, None]
    first = jnp.sum(jnp.where(eidx[None] == experts, blk_start[:, None, None], 0), axis=0)
    return (first * tme + rank).astype(jnp.int32)


def _routing_tables(counts_f, *, tme, nb):
    counts = counts_f[:, 0].astype(jnp.int32)
    nblk = (counts + tme - 1) // tme
    blk_end = jnp.cumsum(nblk)
    blk_start = blk_end - nblk
    n_used = blk_end[-1]
    b = jnp.arange(nb, dtype=jnp.int32)
    used = b < n_used
    blk_exp = jnp.minimum(jnp.sum((b[:, None] >= blk_end[None, :]).astype(jnp.int32), axis=1), N_EXPERTS - 1)
    last_exp = jnp.max(jnp.where(nblk > 0, jnp.arange(N_EXPERTS, dtype=jnp.int32), 0))
    blk_exp = jnp.where(used, blk_exp, last_exp).astype(jnp.int32)
    blk_src = jnp.minimum(b, n_used - 1).astype(jnp.int32)
    experts = jnp.arange(N_EXPERTS, dtype=jnp.int32)
    mine = (b[:, None] >= blk_start[None, :]) & (b[:, None] < blk_end[None, :])
    nvalid = jnp.sum(jnp.where(mine, counts[None, :] - (b[:, None] - blk_start[None, :]) * tme, 0), axis=1)
    nvalid = jnp.clip(nvalid, 0, tme).astype(jnp.int32)
    later = (experts[None, :] > experts[:, None]) & (nblk[None, :] > 0)
    nxt_e = jnp.min(jnp.where(later, experts[None, :], N_EXPERTS), axis=1)
    nxt_e = jnp.where(nxt_e == N_EXPERTS, -1, nxt_e)
    blk_next = jnp.sum(jnp.where(blk_exp[:, None] == experts[None, :], nxt_e[None, :], 0), axis=1).astype(jnp.int32)
    has_tail = (nblk > 0) & (counts % tme != 0)
    partial = jnp.any((b[:, None] == blk_end[None, :] - 1) & has_tail[None, :], axis=1)
    fill_order = jnp.where(partial, 0, jnp.where(used, 2, 1))
    fill_blocks = jnp.argsort(fill_order, stable=True).astype(jnp.int32)
    n_fill = jnp.stack([jnp.sum(partial), jnp.sum(partial | ~used)]).astype(jnp.int32)
    return (blk_start, blk_exp, blk_src, nvalid, blk_next, n_used.reshape(1).astype(jnp.int32), fill_blocks,
            n_fill)


def kernel(x_prompt, x_sample, cache_k, cache_v, state_conv, attn_norm_g, w_in, attn_sinks, conv_w, conv_b,
           conv_ln_g, conv_ln_b, w_out, ffn_norm_g, router_w, router_b, w1, b1, w2, b2, final_norm_g):
    depth = w_in.shape[0]
    assert depth == 1, "single-layer step"
    bp, sp, d = x_prompt.shape
    bs, ss, _ = x_sample.shape
    cw = conv_w.shape[2]
    np_, ns = bp * sp, bs * ss
    n_tok = np_ + ns
    assert sp % WINDOW == 0

    xp2 = x_prompt.reshape(np_, d)
    xs2 = x_sample.reshape(ns, d)
    w_in_bf = w_in[0].astype(BF16)
    w_out_bf = w_out[0].astype(BF16)
    g_attn = attn_norm_g[0].reshape(1, d)
    g_ffn = ffn_norm_g[0].reshape(1, d)
    sinks = attn_sinks[0]
    vec = lambda a: a.reshape(1, cw)

    tab_p = _rope_tables(jnp.arange(sp, dtype=jnp.int32))
    tms = _tile(ns, 512, max(ss, 16))
    tab_s = _rope_tables(PAST_LEN + (jnp.arange(tms, dtype=jnp.int32) % ss))
    qp, kp, vp, kfp, vfp, ap = _in_proj(xp2, g_attn, w_in_bf, tab_p, seq_period=sp, q_dtype=BF16, conv_width=cw)
    qs, _, _, kfs, vfs, as_ = _in_proj(xs2, g_attn, w_in_bf, tab_s, seq_period=None, q_dtype=F32, conv_width=cw)

    att_p = _attn_prompt(qp, kp, vp, sinks, batch=bp, seq=sp)
    ck = cache_k[0].reshape(bs, WINDOW, KV_WIDTH)
    cv_ = cache_v[0].reshape(bs, WINDOW, KV_WIDTH)
    att_s, nk_s, nv_s = _attn_sample(qs, kfs, vfs, ck, cv_, sinks, batch=bs, ts=ss)

    cv_p = _conv_prompt(ap, conv_w[0], vec(conv_b[0]), vec(conv_ln_g[0]), vec(conv_ln_b[0]), batch=bp, seq=sp)
    cv_s = _conv_sample(as_, state_conv[0], conv_w[0], vec(conv_b[0]), vec(conv_ln_g[0]), vec(conv_ln_b[0]),
                        batch=bs, ts=ss)

    n_exp = router_w.shape[2]
    assert n_exp == N_EXPERTS
    rwt_bf = router_w[0].T.astype(BF16)
    rbt = jnp.broadcast_to(router_b[0][:, None], (n_exp, LANES))
    zero_carry = jnp.zeros((n_exp, LANES), F32)
    x1p, hnp, route_p, rt_p, cnt_p = _outproj_router(xp2, att_p, cv_p, w_out_bf, g_ffn, rwt_bf, rbt, zero_carry)
    x1s, hns, route_s, rt_s, cnt = _outproj_router(xs2, att_s, cv_s, w_out_bf, g_ffn, rwt_bf, rbt, cnt_p)

    tme = EXPERT_BLOCK_ROWS
    nb = -(-(n_tok * TOP_K + N_EXPERTS * (tme - 1)) // tme)
    blk_start, blk_exp, blk_src, blk_nvalid, blk_next, n_used, fill_blocks, n_fill = _routing_tables(
        cnt, tme=tme, nb=nb)
    dest_p = _dest_rows(rt_p, blk_start, tme)
    dest_s = _dest_rows(rt_s, blk_start, tme)
    xs_sorted = _dispatch(hnp, hns, jnp.concatenate([dest_p, dest_s], axis=1), fill_blocks, n_fill, nb=nb, tme=tme)
    ys = _experts(xs_sorted, w1[0], b1[0], w2[0], b2[0], blk_exp, blk_src, blk_nvalid, blk_next, n_used, tme=tme)
    g_fin = final_norm_g.reshape(1, d)
    y_p = _combine(x1p, route_p, dest_p, ys, g_fin)
    y_s = _combine(x1s, route_s, dest_s, ys, g_fin)

    kv5 = lambda t, bb: t.reshape(bb, -1, KV_WIDTH)[:, -WINDOW:].reshape(bb, WINDOW, N_KV_HEADS, HEAD_DIM)
    new_k_p = kv5(kfp, bp)[None]
    new_v_p = kv5(vfp, bp)[None]
    ctx = CONV_K - 1
    new_c_p = ap.reshape(bp, sp, cw)[:, -ctx:][None]
    new_c_s = jnp.concatenate([state_conv[0], as_.reshape(bs, ss, cw)], axis=1)[:, -ctx:][None]
    return (y_p.reshape(bp, sp, d), y_s.reshape(bs, ss, d), new_k_p, new_v_p, new_c_p,
            kv5(nk_s, bs)[None], kv5(nv_s, bs)[None], new_c_s)
```

```python
import functools

import jax
import jax.numpy as jnp
from jax import lax
from jax.experimental import pallas as pl
from jax.experimental.pallas import tpu as pltpu

F32 = jnp.float32
BF16 = jnp.bfloat16

HEAD_DIM = 64
N_Q_HEADS = 8
N_KV_HEADS = 2
WINDOW = 128
ROPE_THETA = 500000.0
ROPE_DIM = 16
CONV_K = 31
N_EXPERTS = 32
TOP_K = 4
SWIGLU_LIMIT = 7.0
SWIGLU_ALPHA = 1.702
RMS_EPS = 1e-5
LN_EPS = 1e-5
PAST_LEN = 16384

LANES = 128
SUBLANES = 8
CONV_HALO = 32
VMEM_LIMIT = 56 * 1024 * 1024
EXPERT_BLOCK_ROWS = 512
RUN = 8
ROUTE_T_ROWS = 16
ISSUE_UNROLL = 8

ATTN_WIDTH = N_Q_HEADS * HEAD_DIM
KV_WIDTH = N_KV_HEADS * HEAD_DIM


def _tile(n, pref, mult=8):
    t = min(pref, n)
    while t > 0 and (n % t or t % mult):
        t -= 1
    assert t > 0, (n, pref, mult)
    return t


def _cparams(sem):
    return pltpu.CompilerParams(dimension_semantics=sem, vmem_limit_bytes=VMEM_LIMIT)


def _sigmoid(x):
    return 1.0 / (1.0 + jnp.exp(-x))


def _rope_tables(pos):
    half = ROPE_DIM // 2
    inv_freq = jnp.power(jnp.float32(ROPE_THETA), -jnp.arange(half, dtype=F32) * 2.0 / ROPE_DIM)
    ang = pos.astype(F32)[:, None] * inv_freq[None, :]
    cos, sin = jnp.cos(ang), jnp.sin(ang)
    l64 = jnp.arange(LANES) % HEAD_DIM
    f = l64 % half
    cos_l, sin_l = cos[:, f], sin[:, f]
    c = jnp.where(l64 < ROPE_DIM, cos_l, 1.0)
    s1 = jnp.where(l64 < half, -sin_l, 0.0)
    s2 = jnp.where((l64 >= half) & (l64 < ROPE_DIM), sin_l, 0.0)
    return c.astype(F32), s1.astype(F32), s2.astype(F32)


def _inproj_kernel(x_ref, g_ref, w_ref, c_ref, s1_ref, s2_ref,
                   q_ref, k_ref, v_ref, kf_ref, vf_ref, a_ref, *, conv_width):
    x = x_ref[...]
    ms = jnp.mean(x * x, axis=-1, keepdims=True)
    h = (x * lax.rsqrt(ms + RMS_EPS) * g_ref[...]).astype(BF16)
    z = jnp.dot(h, w_ref[...], preferred_element_type=F32)
    c, s1, s2 = c_ref[...], s1_ref[...], s2_ref[...]
    half = ROPE_DIM // 2

    def rot(t):
        return t * c + pltpu.roll(t, LANES - half, 1) * s1 + pltpu.roll(t, half, 1) * s2

    scale = HEAD_DIM ** -0.5
    for j in range(ATTN_WIDTH // LANES):
        q_ref[:, j * LANES:(j + 1) * LANES] = (rot(z[:, j * LANES:(j + 1) * LANES]) * scale).astype(q_ref.dtype)
    k0 = ATTN_WIDTH
    kr = rot(z[:, k0:k0 + KV_WIDTH])
    k_ref[...] = kr.astype(BF16)
    kf_ref[...] = kr
    v0 = k0 + KV_WIDTH
    vv = z[:, v0:v0 + KV_WIDTH]
    v_ref[...] = vv.astype(BF16)
    vf_ref[...] = vv
    u0 = v0 + KV_WIDTH
    g0 = u0 + conv_width
    a_ref[...] = z[:, u0:g0] * _sigmoid(z[:, g0:g0 + conv_width])


def _in_proj(x2, g, w_bf, tables, *, seq_period, q_dtype, conv_width):
    n, d = x2.shape
    in_w = w_bf.shape[1]
    if seq_period is None:
        tm = tables[0].shape[0]
        tmap = lambda i: (0, 0)
    else:
        tm = _tile(seq_period, 1024, 16)
        per = seq_period // tm
        tmap = lambda i: (i % per, 0)
    assert n % tm == 0
    row = lambda w: pl.BlockSpec((tm, w), lambda i: (i, 0))
    tab = pl.BlockSpec((tm, LANES), tmap)
    return pl.pallas_call(
        functools.partial(_inproj_kernel, conv_width=conv_width),
        grid=(n // tm,),
        in_specs=[row(d), pl.BlockSpec((1, d), lambda i: (0, 0)),
                  pl.BlockSpec((d, in_w), lambda i: (0, 0)), tab, tab, tab],
        out_specs=[row(ATTN_WIDTH), row(KV_WIDTH), row(KV_WIDTH), row(KV_WIDTH), row(KV_WIDTH),
                   row(conv_width)],
        out_shape=[jax.ShapeDtypeStruct((n, ATTN_WIDTH), q_dtype),
                   jax.ShapeDtypeStruct((n, KV_WIDTH), BF16),
                   jax.ShapeDtypeStruct((n, KV_WIDTH), BF16),
                   jax.ShapeDtypeStruct((n, KV_WIDTH), F32),
                   jax.ShapeDtypeStruct((n, KV_WIDTH), F32),
                   jax.ShapeDtypeStruct((n, conv_width), F32)],
        compiler_params=_cparams(("parallel",)),
        name="in_proj",
    )(x2, g, w_bf, *tables)


def _dup_head(t, h):
    sw = pltpu.roll(t, HEAD_DIM, 1)
    low = lax.broadcasted_iota(jnp.int32, t.shape, 1) < HEAD_DIM
    return jnp.where(low, t, sw) if h == 0 else jnp.where(low, sw, t)


def _nt_dot(a, b):
    return lax.dot_general(a, b, (((1,), (1,)), ((), ())), preferred_element_type=F32)


def _attn_prompt_kernel(sink_ref, q_ref, kc_ref, kp_ref, vc_ref, vp_ref, o_ref, *, qb):
    j = pl.program_id(1)
    w = WINDOW
    k_all = jnp.concatenate([kp_ref[...], kc_ref[...]], axis=0).astype(F32)
    v_all = jnp.concatenate([vp_ref[...], vc_ref[...]], axis=0).astype(F32)
    r = lax.broadcasted_iota(jnp.int32, (w, 2 * w), 0)
    kk = lax.broadcasted_iota(jnp.int32, (w, 2 * w), 1)
    band = (kk > r) & (kk <= r + w)
    low = lax.broadcasted_iota(jnp.int32, (w, LANES), 1) < HEAD_DIM
    zero = jnp.zeros((w, LANES), BF16)
    group = N_Q_HEADS // N_KV_HEADS
    for h in range(N_KV_HEADS):
        kd_all = _dup_head(k_all, h).astype(BF16)
        vd_all = _dup_head(v_all, h).astype(BF16)
        for sub in range(qb):
            valid = band & ((kk >= w) | (j > 0)) if sub == 0 else band
            kd = kd_all[sub * w:(sub + 2) * w, :]
            vd = vd_all[sub * w:(sub + 2) * w, :]
            for jj in range(group // 2):
                col = (h * group // 2 + jj) * LANES
                qv = q_ref[sub * w:(sub + 1) * w, col:col + LANES]
                halves = []
                for half in range(2):
                    head = h * group + jj * 2 + half
                    qm = jnp.where(low if half == 0 else ~low, qv, zero)
                    s = jnp.where(valid, _nt_dot(qm, kd), -jnp.inf)
                    sink = sink_ref[head]
                    m = jnp.maximum(jnp.max(s, axis=-1, keepdims=True), sink)
                    p = jnp.exp(s - m)
                    den = jnp.sum(p, axis=-1, keepdims=True) + jnp.exp(sink - m)
                    o = jnp.dot(p.astype(BF16), vd, preferred_element_type=F32)
                    halves.append(o / den)
                o_ref[sub * w:(sub + 1) * w, col:col + LANES] = (
                    jnp.where(low, halves[0], halves[1]).astype(o_ref.dtype))


def _attn_prompt(q, k, v, sinks, *, batch, seq):
    nb = seq // WINDOW
    qb = next(c for c in (4, 2, 1) if nb % c == 0)
    steps = nb // qb
    cur = lambda b, j: (b * steps + j, 0)
    prev = lambda b, j: (b * nb + jnp.maximum(j * qb - 1, 0), 0)
    return pl.pallas_call(
        functools.partial(_attn_prompt_kernel, qb=qb),
        grid=(batch, steps),
        in_specs=[pl.BlockSpec(memory_space=pltpu.SMEM),
                  pl.BlockSpec((qb * WINDOW, ATTN_WIDTH), cur),
                  pl.BlockSpec((qb * WINDOW, KV_WIDTH), cur), pl.BlockSpec((WINDOW, KV_WIDTH), prev),
                  pl.BlockSpec((qb * WINDOW, KV_WIDTH), cur), pl.BlockSpec((WINDOW, KV_WIDTH), prev)],
        out_specs=pl.BlockSpec((qb * WINDOW, ATTN_WIDTH), cur),
        out_shape=jax.ShapeDtypeStruct((batch * seq, ATTN_WIDTH), BF16),
        compiler_params=_cparams(("parallel", "parallel")),
        name="attn_prompt",
    )(sinks, q, k, k, v, v)


def _attn_sample_kernel(sink_ref, q_ref, kn_ref, vn_ref, ck_ref, cv_ref, o_ref, nk_ref, nv_ref, *, gb, ts):
    w = WINDOW
    group = N_Q_HEADS // N_KV_HEADS
    rows = group * ts
    low = lax.broadcasted_iota(jnp.int32, (ts, LANES), 1) < HEAD_DIM
    pad = jnp.zeros((ts, LANES), F32)
    s_c, s_n, v_dup = [], [], []
    for b in range(gb):
        kc, vc = ck_ref[b], cv_ref[b]
        kn, vn = kn_ref[b * ts:(b + 1) * ts, :], vn_ref[b * ts:(b + 1) * ts, :]
        nk_ref[b, 0:w - ts, :] = kc[ts:, :]
        nk_ref[b, w - ts:, :] = kn
        nv_ref[b, 0:w - ts, :] = vc[ts:, :]
        nv_ref[b, w - ts:, :] = vn
        knp = jnp.concatenate([kn, pad], axis=0)
        vnp = jnp.concatenate([vn, pad], axis=0)
        qb = q_ref[b * ts:(b + 1) * ts, :]
        for h in range(N_KV_HEADS):
            parts = []
            for jj in range(group // 2):
                col = (h * group // 2 + jj) * LANES
                qv = qb[:, col:col + LANES]
                parts += [jnp.where(low, qv, 0.0), jnp.where(low, 0.0, qv)]
            lhs = jnp.concatenate(parts, axis=0).astype(BF16)
            s_c.append(_nt_dot(lhs, _dup_head(kc, h).astype(BF16)))
            s_n.append(_nt_dot(lhs, _dup_head(knp, h).astype(BF16)))
            v_dup.append((_dup_head(vc, h).astype(BF16), _dup_head(vnp, h).astype(BF16)))
    s_c = jnp.concatenate(s_c, axis=0)
    s_n = jnp.concatenate(s_n, axis=0)
    n_rows = s_c.shape[0]
    ridx = lax.broadcasted_iota(jnp.int32, (n_rows, 1), 0)
    t_row = ridx % ts
    head_row = (ridx // ts) % N_Q_HEADS
    sink = jnp.zeros((n_rows, 1), F32)
    for hd in range(N_Q_HEADS):
        sink = jnp.where(head_row == hd, sink_ref[hd], sink)
    c_idx = lax.broadcasted_iota(jnp.int32, (n_rows, w), 1)
    n_idx = lax.broadcasted_iota(jnp.int32, (n_rows, 2 * ts), 1)
    s_c = jnp.where(c_idx > t_row, s_c, -jnp.inf)
    s_n = jnp.where(n_idx <= t_row, s_n, -jnp.inf)
    m = jnp.maximum(jnp.maximum(jnp.max(s_c, axis=-1, keepdims=True), jnp.max(s_n, axis=-1, keepdims=True)), sink)
    p_c = jnp.exp(s_c - m)
    p_n = jnp.exp(s_n - m)
    den = jnp.sum(p_c, axis=-1, keepdims=True) + jnp.sum(p_n, axis=-1, keepdims=True) + jnp.exp(sink - m)
    p_c = p_c.astype(BF16)
    p_n = p_n.astype(BF16)
    outs = []
    for b in range(gb):
        cols = []
        for h in range(N_KV_HEADS):
            ci = b * N_KV_HEADS + h
            sl = slice(ci * rows, (ci + 1) * rows)
            vdc, vdn = v_dup[ci]
            o = (jnp.dot(p_c[sl], vdc, preferred_element_type=F32)
                 + jnp.dot(p_n[sl], vdn, preferred_element_type=F32)) / den[sl]
            for jj in range(group // 2):
                lo_part = o[(2 * jj) * ts:(2 * jj + 1) * ts, :]
                hi_part = o[(2 * jj + 1) * ts:(2 * jj + 2) * ts, :]
                cols.append(jnp.where(low, lo_part, hi_part))
        outs.append(jnp.concatenate(cols, axis=1))
    o_ref[...] = jnp.concatenate(outs, axis=0).astype(o_ref.dtype)


def _attn_sample(q, kf, vf, cache_k, cache_v, sinks, *, batch, ts):
    assert ts % 8 == 0 and ts <= WINDOW
    gb = _tile(batch, 8, 2)
    tok = lambda w: pl.BlockSpec((gb * ts, w), lambda i: (i, 0))
    cache = pl.BlockSpec((gb, WINDOW, KV_WIDTH), lambda i: (i, 0, 0))
    cshape = jax.ShapeDtypeStruct((batch, WINDOW, KV_WIDTH), F32)
    return pl.pallas_call(
        functools.partial(_attn_sample_kernel, gb=gb, ts=ts),
        grid=(batch // gb,),
        in_specs=[pl.BlockSpec(memory_space=pltpu.SMEM), tok(ATTN_WIDTH), tok(KV_WIDTH), tok(KV_WIDTH),
                  cache, cache],
        out_specs=[tok(ATTN_WIDTH), cache, cache],
        out_shape=[jax.ShapeDtypeStruct((batch * ts, ATTN_WIDTH), BF16), cshape, cshape],
        compiler_params=_cparams(("parallel",)),
        name="attn_sample",
    )(sinks, q, kf, vf, cache_k, cache_v)


def _ln_swish(acc, b, lg, lb):
    y = acc + b
    mu = jnp.mean(y, axis=-1, keepdims=True)
    yc = y - mu
    var = jnp.mean(yc * yc, axis=-1, keepdims=True)
    yn = yc * lax.rsqrt(var + LN_EPS) * lg + lb
    return yn * _sigmoid(yn)


def _conv_prompt_kernel(a_ref, ap_ref, w_ref, b_ref, lg_ref, lb_ref, o_ref, win_ref, *, tt, rc):
    j = pl.program_id(1)
    n = CONV_HALO + tt
    win = jnp.concatenate([jnp.where(j > 0, ap_ref[...], 0.0), a_ref[...]], axis=0)
    win_ref[0] = win
    for r in range(1, SUBLANES):
        win_ref[r] = pltpu.roll(win, n - r, 0)
    off = CONV_HALO - (CONV_K - 1)
    b, lg, lb = b_ref[...], lg_ref[...], lb_ref[...]
    for c in range(tt // rc):
        acc = jnp.zeros((rc, a_ref.shape[1]), F32)
        for k in range(CONV_K):
            s = off + k
            base = c * rc + (s // SUBLANES) * SUBLANES
            wk = jnp.concatenate([w_ref[k]] * (rc // SUBLANES), axis=0)
            acc = acc + wk * win_ref[s % SUBLANES, base:base + rc, :]
        o_ref[c * rc:(c + 1) * rc, :] = _ln_swish(acc, b, lg, lb).astype(o_ref.dtype)


def _conv_prompt(a, w, b, lg, lb, *, batch, seq):
    cw = a.shape[1]
    tt = _tile(seq, 512, CONV_HALO)
    rc = _tile(tt, 32, 16)
    nt = seq // tt
    per = tt // CONV_HALO
    cur = lambda bb, j: (bb * nt + j, 0)
    prev = lambda bb, j: (jnp.maximum((bb * nt + j) * per - 1, 0), 0)
    vec = pl.BlockSpec((1, cw), lambda bb, j: (0, 0))
    return pl.pallas_call(
        functools.partial(_conv_prompt_kernel, tt=tt, rc=rc),
        grid=(batch, nt),
        in_specs=[pl.BlockSpec((tt, cw), cur), pl.BlockSpec((CONV_HALO, cw), prev),
                  pl.BlockSpec((CONV_K, SUBLANES, cw), lambda bb, j: (0, 0, 0)), vec, vec, vec],
        out_specs=pl.BlockSpec((tt, cw), cur),
        out_shape=jax.ShapeDtypeStruct((batch * seq, cw), BF16),
        scratch_shapes=[pltpu.VMEM((SUBLANES, CONV_HALO + tt, cw), F32)],
        compiler_params=_cparams(("parallel", "parallel")),
        name="conv_prompt",
    )(a, a, jnp.broadcast_to(w[:, None, :], (CONV_K, SUBLANES, cw)), b, lg, lb)


def _conv_sample_kernel(a_ref, st_ref, w_ref, b_ref, lg_ref, lb_ref, o_ref, win_ref, *, gb, ts):
    ctx = CONV_K - 1
    b, lg, lb = b_ref[...], lg_ref[...], lb_ref[...]
    for bb in range(gb):
        win_ref[bb, 0:ctx, :] = st_ref[bb]
        win_ref[bb, ctx:ctx + ts, :] = a_ref[bb * ts:(bb + 1) * ts, :]
    outs = []
    for bb in range(gb):
        acc = jnp.zeros((ts, a_ref.shape[1]), F32)
        for k in range(CONV_K):
            wk = jnp.concatenate([w_ref[k]] * (ts // SUBLANES), axis=0)
            acc = acc + wk * win_ref[bb, k:k + ts, :]
        outs.append(_ln_swish(acc, b, lg, lb))
    o_ref[...] = jnp.concatenate(outs, axis=0).astype(o_ref.dtype)


def _conv_sample(a, state, w, b, lg, lb, *, batch, ts):
    cw = a.shape[1]
    ctx = CONV_K - 1
    gb = _tile(batch, 8, 2)
    vec = pl.BlockSpec((1, cw), lambda i: (0, 0))
    return pl.pallas_call(
        functools.partial(_conv_sample_kernel, gb=gb, ts=ts),
        grid=(batch // gb,),
        in_specs=[pl.BlockSpec((gb * ts, cw), lambda i: (i, 0)),
                  pl.BlockSpec((gb, ctx, cw), lambda i: (i, 0, 0)),
                  pl.BlockSpec((CONV_K, SUBLANES, cw), lambda i: (0, 0, 0)), vec, vec, vec],
        out_specs=pl.BlockSpec((gb * ts, cw), lambda i: (i, 0)),
        out_shape=jax.ShapeDtypeStruct((batch * ts, cw), BF16),
        scratch_shapes=[pltpu.VMEM((gb, ctx + ts + 2, cw), F32)],
        compiler_params=_cparams(("parallel",)),
        name="conv_sample",
    )(a, state, jnp.broadcast_to(w[:, None, :], (CONV_K, SUBLANES, cw)), b, lg, lb)


def _outproj_router_kernel(x_ref, att_ref, cv_ref, wo_ref, g_ref, rw_ref, rb_ref, cin_ref,
                           x1_ref, hn_ref, route_ref, rt_ref, tcar_ref, cnt_ref, carry_ref):
    i = pl.program_id(0)

    @pl.when(i == 0)
    def _():
        carry_ref[...] = cin_ref[...]

    tm = x_ref.shape[0]
    aw = att_ref.shape[1]
    n_exp = rw_ref.shape[0]
    r = lax.broadcasted_iota(jnp.int32, (tm, tm), 0)
    c = lax.broadcasted_iota(jnp.int32, (tm, tm), 1)
    before = (r < c).astype(BF16)
    eid = lax.broadcasted_iota(jnp.int32, (n_exp, tm), 0)
    carry = carry_ref[...][:, 0:1]
    tcar_ref[0] = carry_ref[...]
    mix = (jnp.dot(att_ref[...], wo_ref[0:aw, :], preferred_element_type=F32)
           + jnp.dot(cv_ref[...], wo_ref[aw:, :], preferred_element_type=F32))
    x1 = x_ref[...] + mix
    x1_ref[...] = x1
    ms = jnp.mean(x1 * x1, axis=-1, keepdims=True)
    hn = (x1 * lax.rsqrt(ms + RMS_EPS) * g_ref[...]).astype(BF16)
    hn_ref[...] = hn
    logits = _nt_dot(rw_ref[...], hn) + rb_ref[...][:, 0:1]
    onehot = jnp.zeros((n_exp, tm), F32)
    vals, idxs = [], []
    for _ in range(TOP_K):
        m = jnp.max(logits, axis=0, keepdims=True)
        idx = jnp.min(jnp.where(logits == m, eid, n_exp), axis=0, keepdims=True)
        sel = eid == idx
        onehot = onehot + sel.astype(F32)
        logits = jnp.where(sel, -jnp.inf, logits)
        vals.append(m)
        idxs.append(idx)
    es = [jnp.exp(v - vals[0]) for v in vals]
    den = es[0] + es[1] + es[2] + es[3]
    local = jnp.dot(onehot.astype(BF16), before, preferred_element_type=F32)
    count = jnp.sum(onehot, axis=1, keepdims=True)
    padded = jnp.ceil(count * (1.0 / RUN)) * RUN
    er = lax.broadcasted_iota(jnp.int32, (n_exp, n_exp), 0)
    ec = lax.broadcasted_iota(jnp.int32, (n_exp, n_exp), 1)
    start = jnp.dot((ec < er).astype(BF16), jnp.broadcast_to(padded, (n_exp, LANES)).astype(BF16),
                    preferred_element_type=F32)[:, 0:1]
    pick = lambda k, table: jnp.sum(jnp.where(eid == idxs[k], table, 0.0), axis=0, keepdims=True)
    rows = ([ix.astype(F32) for ix in idxs] + [pick(k, local + carry) for k in range(TOP_K)])
    slots = [pick(k, local + start) for k in range(TOP_K)]
    rt_ref[...] = jnp.concatenate(rows + slots + [jnp.zeros((rt_ref.shape[0] - 3 * TOP_K, tm), F32)], axis=0)
    fields = jnp.concatenate(rows + [e / den for e in es] + [jnp.zeros((LANES - 3 * TOP_K, tm), F32)], axis=0)
    route_ref[...] = jnp.transpose(fields)
    carry = carry + count
    carry_ref[...] = jnp.broadcast_to(carry, carry_ref.shape)
    cnt_ref[...] = jnp.broadcast_to(carry, cnt_ref.shape)


def _store_slabs(ref, val, rows):
    for j in range(val.shape[1] // LANES):
        ref[pl.ds(j, rows, stride=SUBLANES), :] = val[:, j * LANES:(j + 1) * LANES]


def _load_slabs(ref, rows, dtype):
    return jnp.concatenate([ref[pl.ds(j, rows, stride=SUBLANES), :].astype(dtype) for j in range(SUBLANES)],
                           axis=1)


def _outproj_router(x2, att, cv, wo_bf, g, rwt_bf, rbt, carry_in, *, tm):
    n, d = x2.shape
    assert d == SUBLANES * LANES and n % tm == 0
    n_exp = rwt_bf.shape[0]
    row = lambda w: pl.BlockSpec((tm, w), lambda i: (i, 0))
    full = lambda a: pl.BlockSpec(a.shape, lambda i: (0,) * a.ndim)
    return pl.pallas_call(
        _outproj_router_kernel,
        grid=(n // tm,),
        in_specs=[row(d), row(att.shape[1]), row(cv.shape[1]), full(wo_bf), full(g), full(rwt_bf), full(rbt),
                  full(carry_in)],
        out_specs=[row(d), row(d), row(LANES),
                   pl.BlockSpec((ROUTE_T_ROWS, tm), lambda i: (0, i)),
                   pl.BlockSpec((1, n_exp, LANES), lambda i: (i, 0, 0)),
                   pl.BlockSpec((n_exp, LANES), lambda i: (0, 0))],
        out_shape=[jax.ShapeDtypeStruct((n, d), F32), jax.ShapeDtypeStruct((n, d), BF16),
                   jax.ShapeDtypeStruct((n, LANES), F32), jax.ShapeDtypeStruct((ROUTE_T_ROWS, n), F32),
                   jax.ShapeDtypeStruct((n // tm, n_exp, LANES), F32),
                   jax.ShapeDtypeStruct((n_exp, LANES), F32)],
        scratch_shapes=[pltpu.VMEM((n_exp, LANES), F32)],
        compiler_params=_cparams(("arbitrary",)),
        name="outproj_router",
    )(x2, att, cv, wo_bf, g, rwt_bf, rbt, carry_in)


def _dispatch_kernel(fill_ref, nfill_ref, tot_ref, dst_ref, rt_ref, hn_ref, hn2_ref,
                     xs_ref, zero_ref, stage_ref, sem, zsem, *, td, tme, n_first, chunk):
    i = pl.program_id(0)
    last = pl.num_programs(0) - 1
    par = i % 2
    n_slots = stage_ref.shape[1] // SUBLANES

    def fill(f, s):
        row = pl.multiple_of(fill_ref[f] * (tme * SUBLANES), tme * SUBLANES)
        return pltpu.make_async_copy(zero_ref, xs_ref.at[pl.ds(row, tme * SUBLANES)], zsem.at[s])

    def fills(lo, hi, s, act):
        def body(f, carry):
            act(fill(f, s))
            return carry

        lax.fori_loop(lo, hi, body, 0)

    @pl.when(i == 0)
    def _():
        zero_ref[...] = jnp.zeros(zero_ref.shape, F32)
        fills(0, nfill_ref[0], 0, lambda c: c.start())
        fills(nfill_ref[0], nfill_ref[1], 1, lambda c: c.start())
        fills(0, nfill_ref[0], 0, lambda c: c.wait())

    def group_by_expert(tok_ref):
        hn = tok_ref[...]
        slots = rt_ref[2 * TOP_K:3 * TOP_K, :].astype(jnp.int32)
        for c in range(n_slots // chunk):
            s = lax.broadcasted_iota(jnp.int32, (chunk, td), 0) + c * chunk
            hit = s == slots[0:1, :]
            for k in range(1, TOP_K):
                hit = hit | (s == slots[k:k + 1, :])
            rows = jnp.dot(jnp.where(hit, 1.0, 0.0).astype(BF16), hn, preferred_element_type=F32)
            _store_slabs(stage_ref.at[par, pl.ds(c * chunk * SUBLANES, chunk * SUBLANES)], rows, chunk)

    @pl.when(i < n_first)
    def _():
        group_by_expert(hn_ref)

    @pl.when(i >= n_first)
    def _():
        group_by_expert(hn2_ref)

    def run_copy(buf, src_slot, dst_row, runs=1):
        src = pl.multiple_of(src_slot * SUBLANES, RUN * SUBLANES)
        dst = pl.multiple_of(dst_row * SUBLANES, SUBLANES)
        return pltpu.make_async_copy(stage_ref.at[buf, pl.ds(src, runs * RUN * SUBLANES)],
                                     xs_ref.at[pl.ds(dst, runs * RUN * SUBLANES)], sem)

    def wait_runs(tile):
        def wait_many(q, carry):
            run_copy(0, 0, 0, ISSUE_UNROLL).wait()
            return carry

        def wait_one(j, carry):
            run_copy(0, 0, 0).wait()
            return carry

        n_many = tot_ref[tile] // ISSUE_UNROLL
        lax.fori_loop(0, n_many, wait_many, 0)
        lax.fori_loop(n_many * ISSUE_UNROLL, tot_ref[tile], wait_one, 0)

    @pl.when(i > 0)
    def _():
        wait_runs(i - 1)

    def issue(j):
        run_copy(par, j * RUN, dst_ref[0, 0, j]).start()

    def issue_many(q, carry):
        for u in range(ISSUE_UNROLL):
            issue(q * ISSUE_UNROLL + u)
        return carry

    def issue_one(j, carry):
        issue(j)
        return carry

    n_full = tot_ref[i] // ISSUE_UNROLL
    lax.fori_loop(0, n_full, issue_many, 0)
    lax.fori_loop(n_full * ISSUE_UNROLL, tot_ref[i], issue_one, 0)

    @pl.when(i == last)
    def _():
        wait_runs(i)
        fills(nfill_ref[0], nfill_ref[1], 1, lambda c: c.wait())


def _stage_slots(td):
    return td * TOP_K + N_EXPERTS * RUN


def _dest_blocks(dest, td):
    n = dest.shape[1]
    return dest.reshape(TOP_K, n // td, td).transpose(1, 0, 2).reshape(n // td, 1, TOP_K * td)


def _dispatch(hn_a, hn_b, route_t, fill_blocks, n_fill, runs_per_tile, run_dst, *, td, nb, tme):
    na, d = hn_a.shape
    assert na % td == 0 and hn_b.shape[0] % td == 0 and td % RUN == 0
    n_first = na // td
    steps = n_first + hn_b.shape[0] // td
    n_slots = _stage_slots(td)
    chunk = _tile(n_slots, 768, SUBLANES)
    tok = lambda m: pl.BlockSpec((td, d), m)
    grid_spec = pltpu.PrefetchScalarGridSpec(
        num_scalar_prefetch=3,
        grid=(steps,),
        in_specs=[pl.BlockSpec((1, 1, n_slots // RUN), lambda i, *_: (i, 0, 0), memory_space=pltpu.SMEM),
                  pl.BlockSpec((ROUTE_T_ROWS, td), lambda i, *_: (0, i)),
                  tok(lambda i, *_: (jnp.minimum(i, n_first - 1), 0)),
                  tok(lambda i, *_: (jnp.maximum(i - n_first, 0), 0))],
        out_specs=pl.BlockSpec(memory_space=pl.ANY),
        scratch_shapes=[pltpu.VMEM((tme * SUBLANES, LANES), F32),
                        pltpu.VMEM((2, n_slots * SUBLANES, LANES), F32),
                        pltpu.SemaphoreType.DMA(()), pltpu.SemaphoreType.DMA((2,))],
    )
    return pl.pallas_call(
        functools.partial(_dispatch_kernel, td=td, tme=tme, n_first=n_first, chunk=chunk),
        grid_spec=grid_spec,
        out_shape=jax.ShapeDtypeStruct((nb * tme * SUBLANES, LANES), F32),
        compiler_params=_cparams(("arbitrary",)),
        name="dispatch",
    )(fill_blocks, n_fill, runs_per_tile, run_dst, route_t, hn_a, hn_b)


def _experts_kernel(be_ref, bsrc_ref, nv_ref, nx_ref, nu_ref, x_ref, w1_hbm, b1_ref, w2_hbm, b2_ref,
                    y_ref, w1f_ref, w2f_ref, w1b_ref, w2b_ref, par_ref, wsem, *, tme):
    i = pl.program_id(0)
    e = be_ref[i]
    e_prev = be_ref[jnp.maximum(i - 1, 0)]
    d_ff = w2f_ref.shape[1]
    half = tme // 2

    def fetch(expert, s):
        return (pltpu.make_async_copy(w1_hbm.at[expert], w1f_ref.at[s], wsem.at[0, s]),
                pltpu.make_async_copy(w2_hbm.at[expert], w2f_ref.at[s], wsem.at[1, s]))

    @pl.when(i == 0)
    def _():
        par_ref[0] = 0
        for c in fetch(e, 0):
            c.start()

    @pl.when((i == 0) | (e != e_prev))
    def _():
        s = par_ref[0]
        for c in fetch(e, s):
            c.wait()
        w1b_ref[...] = w1f_ref[s].astype(BF16)
        w2b_ref[...] = w2f_ref[s].astype(BF16)
        nxt = nx_ref[i]

        @pl.when(nxt >= 0)
        def _():
            for c in fetch(nxt, 1 - s):
                c.start()

        par_ref[0] = 1 - s

    def ffn(x):
        h = jnp.dot(x, w1b_ref[...], preferred_element_type=F32) + b1_ref[0]
        x_glu = jnp.minimum(h[:, :d_ff], SWIGLU_LIMIT)
        x_lin = jnp.clip(h[:, d_ff:], -SWIGLU_LIMIT, SWIGLU_LIMIT)
        act = x_glu * _sigmoid(SWIGLU_ALPHA * x_glu) * (x_lin + 1.0)
        return jnp.dot(act.astype(BF16), w2b_ref[...], preferred_element_type=F32) + b2_ref[0]

    used = i < nu_ref[0]
    nv = nv_ref[i]

    @pl.when(used & (nv > half))
    def _():
        _store_slabs(y_ref, ffn(_load_slabs(x_ref, tme, BF16)), tme)

    @pl.when(used & (nv <= half))
    def _():
        rows = half * SUBLANES
        _store_slabs(y_ref.at[pl.ds(0, rows)], ffn(_load_slabs(x_ref.at[pl.ds(0, rows)], half, BF16)), half)
        y_ref[pl.ds(rows, rows), :] = jnp.zeros((rows, LANES), F32)

    @pl.when(i == nu_ref[0])
    def _():
        y_ref[...] = jnp.zeros(y_ref.shape, F32)


def _experts(xs, w1, b1, w2, b2, blk_exp, blk_src, blk_nvalid, blk_next, n_used, *, tme):
    n_exp, d, h2 = w1.shape
    d_ff = w2.shape[1]
    nb = xs.shape[0] // (tme * SUBLANES)
    slab = lambda m: pl.BlockSpec((tme * SUBLANES, LANES), m)
    grid_spec = pltpu.PrefetchScalarGridSpec(
        num_scalar_prefetch=5,
        grid=(nb,),
        in_specs=[slab(lambda i, be, bs, nv, nx, nu: (bs[i], 0)),
                  pl.BlockSpec(memory_space=pl.ANY),
                  pl.BlockSpec((1, 1, h2), lambda i, be, bs, nv, nx, nu: (be[i], 0, 0)),
                  pl.BlockSpec(memory_space=pl.ANY),
                  pl.BlockSpec((1, 1, d), lambda i, be, bs, nv, nx, nu: (be[i], 0, 0))],
        out_specs=slab(lambda i, be, bs, nv, nx, nu: (jnp.minimum(i, nu[0]), 0)),
        scratch_shapes=[pltpu.VMEM((2, d, h2), F32), pltpu.VMEM((2, d_ff, d), F32),
                        pltpu.VMEM((d, h2), BF16), pltpu.VMEM((d_ff, d), BF16),
                        pltpu.SMEM((1,), jnp.int32), pltpu.SemaphoreType.DMA((2, 2))],
    )
    return pl.pallas_call(
        functools.partial(_experts_kernel, tme=tme),
        grid_spec=grid_spec,
        out_shape=jax.ShapeDtypeStruct(xs.shape, F32),
        input_output_aliases={5: 0},
        compiler_params=_cparams(("arbitrary",)),
        name="experts",
    )(blk_exp, blk_src, blk_nvalid, blk_next, n_used, xs, w1, b1.reshape(n_exp, 1, h2), w2,
      b2.reshape(n_exp, 1, d))


def _combine_kernel(dcur_ref, dnext_ref, x1_ref, route_ref, g_ref, ys_ref, o_ref, buf_ref, sem, *, tc):
    i = pl.program_id(0)
    slot = i % 2

    def gather(dref, s):
        def issue(r, carry):
            row = pl.multiple_of(r * SUBLANES, SUBLANES)
            for k in range(TOP_K):
                d = pl.multiple_of(dref[0, 0, k * tc + r] * SUBLANES, SUBLANES)
                pltpu.make_async_copy(ys_ref.at[pl.ds(d, SUBLANES)], buf_ref.at[s, k, pl.ds(row, SUBLANES)],
                                      sem.at[s]).start(priority=k % 2)
            return carry

        lax.fori_loop(0, tc, issue, 0, unroll=8)

    @pl.when(i == 0)
    def _():
        gather(dcur_ref, 0)

    @pl.when(i + 1 < pl.num_programs(0))
    def _():
        gather(dnext_ref, 1 - slot)

    for k in range(TOP_K):
        pltpu.make_async_copy(ys_ref.at[pl.ds(0, tc * SUBLANES)], buf_ref.at[slot, k], sem.at[slot]).wait()
    route = route_ref[...]
    x1 = x1_ref[...]
    chunks = []
    for j in range(SUBLANES):
        acc = x1[:, j * LANES:(j + 1) * LANES]
        for k in range(TOP_K):
            acc = acc + (route[:, 2 * TOP_K + k:2 * TOP_K + k + 1]
                         * buf_ref[slot, k, pl.ds(j, tc, stride=SUBLANES), :])
        chunks.append(acc)
    y = jnp.concatenate(chunks, axis=1)
    ms = jnp.mean(y * y, axis=-1, keepdims=True)
    o_ref[...] = y * lax.rsqrt(ms + RMS_EPS) * g_ref[...]


def _combine(x1, route, dest, ys, g):
    n, d = x1.shape
    tc = _tile(n, 128, 8)
    steps = n // tc
    dest3 = _dest_blocks(dest, tc)
    dspec = lambda m: pl.BlockSpec((1, 1, tc * TOP_K), m, memory_space=pltpu.SMEM)
    return pl.pallas_call(
        functools.partial(_combine_kernel, tc=tc),
        grid=(steps,),
        in_specs=[dspec(lambda i: (i, 0, 0)), dspec(lambda i: (jnp.minimum(i + 1, steps - 1), 0, 0)),
                  pl.BlockSpec((tc, d), lambda i: (i, 0)),
                  pl.BlockSpec((tc, LANES), lambda i: (i, 0)),
                  pl.BlockSpec((1, d), lambda i: (0, 0)),
                  pl.BlockSpec(memory_space=pl.ANY)],
        out_specs=pl.BlockSpec((tc, d), lambda i: (i, 0)),
        out_shape=jax.ShapeDtypeStruct((n, d), F32),
        scratch_shapes=[pltpu.VMEM((2, TOP_K, tc * SUBLANES, LANES), F32), pltpu.SemaphoreType.DMA((2,))],
        compiler_params=_cparams(("arbitrary",)),
        name="combine",
    )(dest3, dest3, x1, route, g, ys)


def _dest_rows(route_t, blk_start, tme):
    eidx = route_t[0:TOP_K].astype(jnp.int32)
    rank = route_t[TOP_K:2 * TOP_K].astype(jnp.int32)
    experts = jnp.arange(N_EXPERTS, dtype=jnp.int32)[:, None, None]
    first = jnp.sum(jnp.where(eidx[None] == experts, blk_start[:, None, None], 0), axis=0)
    return (first * tme + rank).astype(jnp.int32)


def _routing_tables(counts_f, *, tme, nb):
    counts = counts_f[:, 0].astype(jnp.int32)
    nblk = jnp.where(counts > 0, (counts + RUN - 1 + tme - 1) // tme, 0)
    blk_end = jnp.cumsum(nblk)
    blk_start = blk_end - nblk
    n_used = blk_end[-1]
    b = jnp.arange(nb, dtype=jnp.int32)
    used = b < n_used
    blk_exp = jnp.minimum(jnp.sum((b[:, None] >= blk_end[None, :]).astype(jnp.int32), axis=1), N_EXPERTS - 1)
    last_exp = jnp.max(jnp.where(nblk > 0, jnp.arange(N_EXPERTS, dtype=jnp.int32), 0))
    blk_exp = jnp.where(used, blk_exp, last_exp).astype(jnp.int32)
    blk_src = jnp.minimum(b, n_used - 1).astype(jnp.int32)
    experts = jnp.arange(N_EXPERTS, dtype=jnp.int32)
    mine = (b[:, None] >= blk_start[None, :]) & (b[:, None] < blk_end[None, :])
    nvalid = jnp.sum(jnp.where(mine, counts[None, :] - (b[:, None] - blk_start[None, :]) * tme, 0), axis=1)
    nvalid = jnp.clip(nvalid, 0, tme).astype(jnp.int32)
    later = (experts[None, :] > experts[:, None]) & (nblk[None, :] > 0)
    nxt_e = jnp.min(jnp.where(later, experts[None, :], N_EXPERTS), axis=1)
    nxt_e = jnp.where(nxt_e == N_EXPERTS, -1, nxt_e)
    blk_next = jnp.sum(jnp.where(blk_exp[:, None] == experts[None, :], nxt_e[None, :], 0), axis=1).astype(jnp.int32)
    partial = used & (nvalid < tme)
    fill_order = jnp.where(partial, 0, jnp.where(used, 2, 1))
    fill_blocks = jnp.argsort(fill_order, stable=True).astype(jnp.int32)
    n_fill = jnp.stack([jnp.sum(partial), jnp.sum(partial | ~used)]).astype(jnp.int32)
    return (blk_start, blk_exp, blk_src, nvalid, blk_next, n_used.reshape(1).astype(jnp.int32), fill_blocks,
            n_fill)


def _run_tables(tile_carry_f, counts_f, blk_start, *, td, tme):
    before = tile_carry_f[:, :, 0].astype(jnp.int32)
    counts = counts_f[:, 0].astype(jnp.int32)
    in_tile = jnp.concatenate([before[1:], counts[None, :]], axis=0) - before
    n_runs = (in_tile + RUN - 1) // RUN
    ends = jnp.cumsum(n_runs, axis=1)
    j = jnp.arange(_stage_slots(td) // RUN, dtype=jnp.int32)
    owner = jnp.sum((j[None, :, None] >= ends[:, None, :]).astype(jnp.int32), axis=2)
    base = blk_start[None, :] * tme + before - (ends - n_runs) * RUN
    mine = owner[:, :, None] == jnp.arange(N_EXPERTS, dtype=jnp.int32)[None, None, :]
    dst = jnp.sum(jnp.where(mine, base[:, None, :], 0), axis=2) + j[None, :] * RUN
    return ends[:, -1].astype(jnp.int32), dst[:, None, :].astype(jnp.int32)


def kernel(x_prompt, x_sample, cache_k, cache_v, state_conv, attn_norm_g, w_in, attn_sinks, conv_w, conv_b,
           conv_ln_g, conv_ln_b, w_out, ffn_norm_g, router_w, router_b, w1, b1, w2, b2, final_norm_g):
    depth = w_in.shape[0]
    assert depth == 1, "single-layer step"
    bp, sp, d = x_prompt.shape
    bs, ss, _ = x_sample.shape
    cw = conv_w.shape[2]
    np_, ns = bp * sp, bs * ss
    n_tok = np_ + ns
    assert sp % WINDOW == 0

    xp2 = x_prompt.reshape(np_, d)
    xs2 = x_sample.reshape(ns, d)
    w_in_bf = w_in[0].astype(BF16)
    w_out_bf = w_out[0].astype(BF16)
    g_attn = attn_norm_g[0].reshape(1, d)
    g_ffn = ffn_norm_g[0].reshape(1, d)
    sinks = attn_sinks[0]
    vec = lambda a: a.reshape(1, cw)

    tab_p = _rope_tables(jnp.arange(sp, dtype=jnp.int32))
    tms = _tile(ns, 512, max(ss, 16))
    tab_s = _rope_tables(PAST_LEN + (jnp.arange(tms, dtype=jnp.int32) % ss))
    qp, kp, vp, kfp, vfp, ap = _in_proj(xp2, g_attn, w_in_bf, tab_p, seq_period=sp, q_dtype=BF16, conv_width=cw)
    qs, _, _, kfs, vfs, as_ = _in_proj(xs2, g_attn, w_in_bf, tab_s, seq_period=None, q_dtype=F32, conv_width=cw)

    att_p = _attn_prompt(qp, kp, vp, sinks, batch=bp, seq=sp)
    ck = cache_k[0].reshape(bs, WINDOW, KV_WIDTH)
    cv_ = cache_v[0].reshape(bs, WINDOW, KV_WIDTH)
    att_s, nk_s, nv_s = _attn_sample(qs, kfs, vfs, ck, cv_, sinks, batch=bs, ts=ss)

    cv_p = _conv_prompt(ap, conv_w[0], vec(conv_b[0]), vec(conv_ln_g[0]), vec(conv_ln_b[0]), batch=bp, seq=sp)
    cv_s = _conv_sample(as_, state_conv[0], conv_w[0], vec(conv_b[0]), vec(conv_ln_g[0]), vec(conv_ln_b[0]),
                        batch=bs, ts=ss)

    n_exp = router_w.shape[2]
    assert n_exp == N_EXPERTS
    rwt_bf = router_w[0].T.astype(BF16)
    rbt = jnp.broadcast_to(router_b[0][:, None], (n_exp, LANES))
    zero_carry = jnp.zeros((n_exp, LANES), F32)
    tr = _tile(ns, 512, 16)
    assert np_ % tr == 0
    x1p, hnp, route_p, rt_p, tcar_p, cnt_p = _outproj_router(xp2, att_p, cv_p, w_out_bf, g_ffn, rwt_bf, rbt,
                                                             zero_carry, tm=tr)
    x1s, hns, route_s, rt_s, tcar_s, cnt = _outproj_router(xs2, att_s, cv_s, w_out_bf, g_ffn, rwt_bf, rbt,
                                                           cnt_p, tm=tr)

    tme = EXPERT_BLOCK_ROWS
    nb = -(-(n_tok * TOP_K + N_EXPERTS * (tme - 1 + RUN - 1)) // tme)
    blk_start, blk_exp, blk_src, blk_nvalid, blk_next, n_used, fill_blocks, n_fill = _routing_tables(
        cnt, tme=tme, nb=nb)
    dest_p = _dest_rows(rt_p, blk_start, tme)
    dest_s = _dest_rows(rt_s, blk_start, tme)
    runs_per_tile, run_dst = _run_tables(jnp.concatenate([tcar_p, tcar_s], axis=0), cnt, blk_start, td=tr, tme=tme)
    xs_sorted = _dispatch(hnp, hns, jnp.concatenate([rt_p, rt_s], axis=1), fill_blocks, n_fill, runs_per_tile,
                          run_dst, td=tr, nb=nb, tme=tme)
    ys = _experts(xs_sorted, w1[0], b1[0], w2[0], b2[0], blk_exp, blk_src, blk_nvalid, blk_next, n_used, tme=tme)
    g_fin = final_norm_g.reshape(1, d)
    y_p = _combine(x1p, route_p, dest_p, ys, g_fin)
    y_s = _combine(x1s, route_s, dest_s, ys, g_fin)

    kv5 = lambda t, bb: t.reshape(bb, -1, KV_WIDTH)[:, -WINDOW:].reshape(bb, WINDOW, N_KV_HEADS, HEAD_DIM)
    new_k_p = kv5(kfp, bp)[None]
    new_v_p = kv5(vfp, bp)[None]
    ctx = CONV_K - 1
    new_c_p = ap.reshape(bp, sp, cw)[:, -ctx:][None]
    new_c_s = jnp.concatenate([state_conv[0], as_.reshape(bs, ss, cw)], axis=1)[:, -ctx:][None]
    return (y_p.reshape(bp, sp, d), y_s.reshape(bs, ss, d), new_k_p, new_v_p, new_c_p,
            kv5(nk_s, bs)[None], kv5(nv_s, bs)[None], new_c_s)
```

```python
import functools

import jax
import jax.numpy as jnp
from jax import lax
from jax.experimental import pallas as pl
from jax.experimental.pallas import tpu as pltpu

F32 = jnp.float32
BF16 = jnp.bfloat16

HEAD_DIM = 64
N_Q_HEADS = 8
N_KV_HEADS = 2
WINDOW = 128
ROPE_THETA = 500000.0
ROPE_DIM = 16
CONV_K = 31
N_EXPERTS = 32
TOP_K = 4
SWIGLU_LIMIT = 7.0
SWIGLU_ALPHA = 1.702
RMS_EPS = 1e-5
LN_EPS = 1e-5
PAST_LEN = 16384

LANES = 128
SUBLANES = 8
CONV_HALO = 32
VMEM_LIMIT = 56 * 1024 * 1024
EXPERT_BLOCK_ROWS = 512
RUN = 8
ROUTE_T_ROWS = 16
ISSUE_UNROLL = 8

ATTN_WIDTH = N_Q_HEADS * HEAD_DIM
KV_WIDTH = N_KV_HEADS * HEAD_DIM


def _tile(n, pref, mult=8):
    t = min(pref, n)
    while t > 0 and (n % t or t % mult):
        t -= 1
    assert t > 0, (n, pref, mult)
    return t


def _cparams(sem):
    return pltpu.CompilerParams(dimension_semantics=sem, vmem_limit_bytes=VMEM_LIMIT)


def _sigmoid(x):
    return 1.0 / (1.0 + jnp.exp(-x))


def _rope_tables(pos):
    half = ROPE_DIM // 2
    inv_freq = jnp.power(jnp.float32(ROPE_THETA), -jnp.arange(half, dtype=F32) * 2.0 / ROPE_DIM)
    ang = pos.astype(F32)[:, None] * inv_freq[None, :]
    cos, sin = jnp.cos(ang), jnp.sin(ang)
    l64 = jnp.arange(LANES) % HEAD_DIM
    f = l64 % half
    cos_l, sin_l = cos[:, f], sin[:, f]
    c = jnp.where(l64 < ROPE_DIM, cos_l, 1.0)
    s1 = jnp.where(l64 < half, -sin_l, 0.0)
    s2 = jnp.where((l64 >= half) & (l64 < ROPE_DIM), sin_l, 0.0)
    return c.astype(F32), s1.astype(F32), s2.astype(F32)


def _inproj_kernel(x_ref, g_ref, w_ref, c_ref, s1_ref, s2_ref,
                   q_ref, k_ref, v_ref, kf_ref, vf_ref, a_ref, *, conv_width):
    x = x_ref[...]
    ms = jnp.mean(x * x, axis=-1, keepdims=True)
    h = (x * lax.rsqrt(ms + RMS_EPS) * g_ref[...]).astype(BF16)
    z = jnp.dot(h, w_ref[...], preferred_element_type=F32)
    c, s1, s2 = c_ref[...], s1_ref[...], s2_ref[...]
    half = ROPE_DIM // 2

    def rot(t):
        return t * c + pltpu.roll(t, LANES - half, 1) * s1 + pltpu.roll(t, half, 1) * s2

    scale = HEAD_DIM ** -0.5
    for j in range(ATTN_WIDTH // LANES):
        q_ref[:, j * LANES:(j + 1) * LANES] = (rot(z[:, j * LANES:(j + 1) * LANES]) * scale).astype(q_ref.dtype)
    k0 = ATTN_WIDTH
    kr = rot(z[:, k0:k0 + KV_WIDTH])
    k_ref[...] = kr.astype(BF16)
    kf_ref[...] = kr
    v0 = k0 + KV_WIDTH
    vv = z[:, v0:v0 + KV_WIDTH]
    v_ref[...] = vv.astype(BF16)
    vf_ref[...] = vv
    u0 = v0 + KV_WIDTH
    g0 = u0 + conv_width
    a_ref[...] = z[:, u0:g0] * _sigmoid(z[:, g0:g0 + conv_width])


def _in_proj(x2, g, w_bf, tables, *, seq_period, q_dtype, conv_width):
    n, d = x2.shape
    in_w = w_bf.shape[1]
    if seq_period is None:
        tm = tables[0].shape[0]
        tmap = lambda i: (0, 0)
    else:
        tm = _tile(seq_period, 1024, 16)
        per = seq_period // tm
        tmap = lambda i: (i % per, 0)
    assert n % tm == 0
    row = lambda w: pl.BlockSpec((tm, w), lambda i: (i, 0))
    tab = pl.BlockSpec((tm, LANES), tmap)
    return pl.pallas_call(
        functools.partial(_inproj_kernel, conv_width=conv_width),
        grid=(n // tm,),
        in_specs=[row(d), pl.BlockSpec((1, d), lambda i: (0, 0)),
                  pl.BlockSpec((d, in_w), lambda i: (0, 0)), tab, tab, tab],
        out_specs=[row(ATTN_WIDTH), row(KV_WIDTH), row(KV_WIDTH), row(KV_WIDTH), row(KV_WIDTH),
                   row(conv_width)],
        out_shape=[jax.ShapeDtypeStruct((n, ATTN_WIDTH), q_dtype),
                   jax.ShapeDtypeStruct((n, KV_WIDTH), BF16),
                   jax.ShapeDtypeStruct((n, KV_WIDTH), BF16),
                   jax.ShapeDtypeStruct((n, KV_WIDTH), F32),
                   jax.ShapeDtypeStruct((n, KV_WIDTH), F32),
                   jax.ShapeDtypeStruct((n, conv_width), F32)],
        compiler_params=_cparams(("parallel",)),
        name="in_proj",
    )(x2, g, w_bf, *tables)


def _dup_head(t, h):
    sw = pltpu.roll(t, HEAD_DIM, 1)
    low = lax.broadcasted_iota(jnp.int32, t.shape, 1) < HEAD_DIM
    return jnp.where(low, t, sw) if h == 0 else jnp.where(low, sw, t)


def _nt_dot(a, b):
    return lax.dot_general(a, b, (((1,), (1,)), ((), ())), preferred_element_type=F32)


def _attn_prompt_kernel(sink_ref, q_ref, kc_ref, kp_ref, vc_ref, vp_ref, o_ref, *, qb):
    j = pl.program_id(1)
    w = WINDOW
    k_all = jnp.concatenate([kp_ref[...], kc_ref[...]], axis=0).astype(F32)
    v_all = jnp.concatenate([vp_ref[...], vc_ref[...]], axis=0).astype(F32)
    r = lax.broadcasted_iota(jnp.int32, (w, 2 * w), 0)
    kk = lax.broadcasted_iota(jnp.int32, (w, 2 * w), 1)
    band = (kk > r) & (kk <= r + w)
    low = lax.broadcasted_iota(jnp.int32, (w, LANES), 1) < HEAD_DIM
    zero = jnp.zeros((w, LANES), BF16)
    group = N_Q_HEADS // N_KV_HEADS
    for h in range(N_KV_HEADS):
        kd_all = _dup_head(k_all, h).astype(BF16)
        vd_all = _dup_head(v_all, h).astype(BF16)
        for sub in range(qb):
            valid = band & ((kk >= w) | (j > 0)) if sub == 0 else band
            kd = kd_all[sub * w:(sub + 2) * w, :]
            vd = vd_all[sub * w:(sub + 2) * w, :]
            for jj in range(group // 2):
                col = (h * group // 2 + jj) * LANES
                qv = q_ref[sub * w:(sub + 1) * w, col:col + LANES]
                halves = []
                for half in range(2):
                    head = h * group + jj * 2 + half
                    qm = jnp.where(low if half == 0 else ~low, qv, zero)
                    s = jnp.where(valid, _nt_dot(qm, kd), -jnp.inf)
                    sink = sink_ref[head]
                    m = jnp.maximum(jnp.max(s, axis=-1, keepdims=True), sink)
                    p = jnp.exp(s - m)
                    den = jnp.sum(p, axis=-1, keepdims=True) + jnp.exp(sink - m)
                    o = jnp.dot(p.astype(BF16), vd, preferred_element_type=F32)
                    halves.append(o / den)
                o_ref[sub * w:(sub + 1) * w, col:col + LANES] = (
                    jnp.where(low, halves[0], halves[1]).astype(o_ref.dtype))


def _attn_prompt(q, k, v, sinks, *, batch, seq):
    nb = seq // WINDOW
    qb = next(c for c in (4, 2, 1) if nb % c == 0)
    steps = nb // qb
    cur = lambda b, j: (b * steps + j, 0)
    prev = lambda b, j: (b * nb + jnp.maximum(j * qb - 1, 0), 0)
    return pl.pallas_call(
        functools.partial(_attn_prompt_kernel, qb=qb),
        grid=(batch, steps),
        in_specs=[pl.BlockSpec(memory_space=pltpu.SMEM),
                  pl.BlockSpec((qb * WINDOW, ATTN_WIDTH), cur),
                  pl.BlockSpec((qb * WINDOW, KV_WIDTH), cur), pl.BlockSpec((WINDOW, KV_WIDTH), prev),
                  pl.BlockSpec((qb * WINDOW, KV_WIDTH), cur), pl.BlockSpec((WINDOW, KV_WIDTH), prev)],
        out_specs=pl.BlockSpec((qb * WINDOW, ATTN_WIDTH), cur),
        out_shape=jax.ShapeDtypeStruct((batch * seq, ATTN_WIDTH), BF16),
        compiler_params=_cparams(("parallel", "parallel")),
        name="attn_prompt",
    )(sinks, q, k, k, v, v)


def _attn_sample_kernel(sink_ref, q_ref, kn_ref, vn_ref, ck_ref, cv_ref, o_ref, nk_ref, nv_ref, *, gb, ts):
    w = WINDOW
    group = N_Q_HEADS // N_KV_HEADS
    rows = group * ts
    low = lax.broadcasted_iota(jnp.int32, (ts, LANES), 1) < HEAD_DIM
    pad = jnp.zeros((ts, LANES), F32)
    s_c, s_n, v_dup = [], [], []
    for b in range(gb):
        kc, vc = ck_ref[b], cv_ref[b]
        kn, vn = kn_ref[b * ts:(b + 1) * ts, :], vn_ref[b * ts:(b + 1) * ts, :]
        nk_ref[b, 0:w - ts, :] = kc[ts:, :]
        nk_ref[b, w - ts:, :] = kn
        nv_ref[b, 0:w - ts, :] = vc[ts:, :]
        nv_ref[b, w - ts:, :] = vn
        knp = jnp.concatenate([kn, pad], axis=0)
        vnp = jnp.concatenate([vn, pad], axis=0)
        qb = q_ref[b * ts:(b + 1) * ts, :]
        for h in range(N_KV_HEADS):
            parts = []
            for jj in range(group // 2):
                col = (h * group // 2 + jj) * LANES
                qv = qb[:, col:col + LANES]
                parts += [jnp.where(low, qv, 0.0), jnp.where(low, 0.0, qv)]
            lhs = jnp.concatenate(parts, axis=0).astype(BF16)
            s_c.append(_nt_dot(lhs, _dup_head(kc, h).astype(BF16)))
            s_n.append(_nt_dot(lhs, _dup_head(knp, h).astype(BF16)))
            v_dup.append((_dup_head(vc, h).astype(BF16), _dup_head(vnp, h).astype(BF16)))
    s_c = jnp.concatenate(s_c, axis=0)
    s_n = jnp.concatenate(s_n, axis=0)
    n_rows = s_c.shape[0]
    ridx = lax.broadcasted_iota(jnp.int32, (n_rows, 1), 0)
    t_row = ridx % ts
    head_row = (ridx // ts) % N_Q_HEADS
    sink = jnp.zeros((n_rows, 1), F32)
    for hd in range(N_Q_HEADS):
        sink = jnp.where(head_row == hd, sink_ref[hd], sink)
    c_idx = lax.broadcasted_iota(jnp.int32, (n_rows, w), 1)
    n_idx = lax.broadcasted_iota(jnp.int32, (n_rows, 2 * ts), 1)
    s_c = jnp.where(c_idx > t_row, s_c, -jnp.inf)
    s_n = jnp.where(n_idx <= t_row, s_n, -jnp.inf)
    m = jnp.maximum(jnp.maximum(jnp.max(s_c, axis=-1, keepdims=True), jnp.max(s_n, axis=-1, keepdims=True)), sink)
    p_c = jnp.exp(s_c - m)
    p_n = jnp.exp(s_n - m)
    den = jnp.sum(p_c, axis=-1, keepdims=True) + jnp.sum(p_n, axis=-1, keepdims=True) + jnp.exp(sink - m)
    p_c = p_c.astype(BF16)
    p_n = p_n.astype(BF16)
    outs = []
    for b in range(gb):
        cols = []
        for h in range(N_KV_HEADS):
            ci = b * N_KV_HEADS + h
            sl = slice(ci * rows, (ci + 1) * rows)
            vdc, vdn = v_dup[ci]
            o = (jnp.dot(p_c[sl], vdc, preferred_element_type=F32)
                 + jnp.dot(p_n[sl], vdn, preferred_element_type=F32)) / den[sl]
            for jj in range(group // 2):
                lo_part = o[(2 * jj) * ts:(2 * jj + 1) * ts, :]
                hi_part = o[(2 * jj + 1) * ts:(2 * jj + 2) * ts, :]
                cols.append(jnp.where(low, lo_part, hi_part))
        outs.append(jnp.concatenate(cols, axis=1))
    o_ref[...] = jnp.concatenate(outs, axis=0).astype(o_ref.dtype)


def _attn_sample(q, kf, vf, cache_k, cache_v, sinks, *, batch, ts):
    assert ts % 8 == 0 and ts <= WINDOW
    gb = _tile(batch, 8, 2)
    tok = lambda w: pl.BlockSpec((gb * ts, w), lambda i: (i, 0))
    cache = pl.BlockSpec((gb, WINDOW, KV_WIDTH), lambda i: (i, 0, 0))
    cshape = jax.ShapeDtypeStruct((batch, WINDOW, KV_WIDTH), F32)
    return pl.pallas_call(
        functools.partial(_attn_sample_kernel, gb=gb, ts=ts),
        grid=(batch // gb,),
        in_specs=[pl.BlockSpec(memory_space=pltpu.SMEM), tok(ATTN_WIDTH), tok(KV_WIDTH), tok(KV_WIDTH),
                  cache, cache],
        out_specs=[tok(ATTN_WIDTH), cache, cache],
        out_shape=[jax.ShapeDtypeStruct((batch * ts, ATTN_WIDTH), BF16), cshape, cshape],
        compiler_params=_cparams(("parallel",)),
        name="attn_sample",
    )(sinks, q, kf, vf, cache_k, cache_v)


def _ln_swish(acc, b, lg, lb):
    y = acc + b
    mu = jnp.mean(y, axis=-1, keepdims=True)
    yc = y - mu
    var = jnp.mean(yc * yc, axis=-1, keepdims=True)
    yn = yc * lax.rsqrt(var + LN_EPS) * lg + lb
    return yn * _sigmoid(yn)


def _conv_prompt_kernel(a_ref, ap_ref, w_ref, b_ref, lg_ref, lb_ref, o_ref, win_ref, *, tt, rc):
    j = pl.program_id(1)
    n = CONV_HALO + tt
    win = jnp.concatenate([jnp.where(j > 0, ap_ref[...], 0.0), a_ref[...]], axis=0)
    win_ref[0] = win
    for r in range(1, SUBLANES):
        win_ref[r] = pltpu.roll(win, n - r, 0)
    off = CONV_HALO - (CONV_K - 1)
    b, lg, lb = b_ref[...], lg_ref[...], lb_ref[...]
    for c in range(tt // rc):
        acc = jnp.zeros((rc, a_ref.shape[1]), F32)
        for k in range(CONV_K):
            s = off + k
            base = c * rc + (s // SUBLANES) * SUBLANES
            wk = jnp.concatenate([w_ref[k]] * (rc // SUBLANES), axis=0)
            acc = acc + wk * win_ref[s % SUBLANES, base:base + rc, :]
        o_ref[c * rc:(c + 1) * rc, :] = _ln_swish(acc, b, lg, lb).astype(o_ref.dtype)


def _conv_prompt(a, w, b, lg, lb, *, batch, seq):
    cw = a.shape[1]
    tt = _tile(seq, 512, CONV_HALO)
    rc = _tile(tt, 32, 16)
    nt = seq // tt
    per = tt // CONV_HALO
    cur = lambda bb, j: (bb * nt + j, 0)
    prev = lambda bb, j: (jnp.maximum((bb * nt + j) * per - 1, 0), 0)
    vec = pl.BlockSpec((1, cw), lambda bb, j: (0, 0))
    return pl.pallas_call(
        functools.partial(_conv_prompt_kernel, tt=tt, rc=rc),
        grid=(batch, nt),
        in_specs=[pl.BlockSpec((tt, cw), cur), pl.BlockSpec((CONV_HALO, cw), prev),
                  pl.BlockSpec((CONV_K, SUBLANES, cw), lambda bb, j: (0, 0, 0)), vec, vec, vec],
        out_specs=pl.BlockSpec((tt, cw), cur),
        out_shape=jax.ShapeDtypeStruct((batch * seq, cw), BF16),
        scratch_shapes=[pltpu.VMEM((SUBLANES, CONV_HALO + tt, cw), F32)],
        compiler_params=_cparams(("parallel", "parallel")),
        name="conv_prompt",
    )(a, a, jnp.broadcast_to(w[:, None, :], (CONV_K, SUBLANES, cw)), b, lg, lb)


def _conv_sample_kernel(a_ref, st_ref, w_ref, b_ref, lg_ref, lb_ref, o_ref, win_ref, *, gb, ts):
    ctx = CONV_K - 1
    b, lg, lb = b_ref[...], lg_ref[...], lb_ref[...]
    for bb in range(gb):
        win_ref[bb, 0:ctx, :] = st_ref[bb]
        win_ref[bb, ctx:ctx + ts, :] = a_ref[bb * ts:(bb + 1) * ts, :]
    outs = []
    for bb in range(gb):
        acc = jnp.zeros((ts, a_ref.shape[1]), F32)
        for k in range(CONV_K):
            wk = jnp.concatenate([w_ref[k]] * (ts // SUBLANES), axis=0)
            acc = acc + wk * win_ref[bb, k:k + ts, :]
        outs.append(_ln_swish(acc, b, lg, lb))
    o_ref[...] = jnp.concatenate(outs, axis=0).astype(o_ref.dtype)


def _conv_sample(a, state, w, b, lg, lb, *, batch, ts):
    cw = a.shape[1]
    ctx = CONV_K - 1
    gb = _tile(batch, 8, 2)
    vec = pl.BlockSpec((1, cw), lambda i: (0, 0))
    return pl.pallas_call(
        functools.partial(_conv_sample_kernel, gb=gb, ts=ts),
        grid=(batch // gb,),
        in_specs=[pl.BlockSpec((gb * ts, cw), lambda i: (i, 0)),
                  pl.BlockSpec((gb, ctx, cw), lambda i: (i, 0, 0)),
                  pl.BlockSpec((CONV_K, SUBLANES, cw), lambda i: (0, 0, 0)), vec, vec, vec],
        out_specs=pl.BlockSpec((gb * ts, cw), lambda i: (i, 0)),
        out_shape=jax.ShapeDtypeStruct((batch * ts, cw), BF16),
        scratch_shapes=[pltpu.VMEM((gb, ctx + ts + 2, cw), F32)],
        compiler_params=_cparams(("parallel",)),
        name="conv_sample",
    )(a, state, jnp.broadcast_to(w[:, None, :], (CONV_K, SUBLANES, cw)), b, lg, lb)


def _outproj_router_kernel(x_ref, att_ref, cv_ref, wo_ref, g_ref, rw_ref, rb_ref, cin_ref,
                           x1_ref, hn_ref, route_ref, rt_ref, tcar_ref, cnt_ref, carry_ref):
    i = pl.program_id(0)

    @pl.when(i == 0)
    def _():
        carry_ref[...] = cin_ref[...]

    tm = x_ref.shape[0]
    aw = att_ref.shape[1]
    n_exp = rw_ref.shape[0]
    r = lax.broadcasted_iota(jnp.int32, (tm, tm), 0)
    c = lax.broadcasted_iota(jnp.int32, (tm, tm), 1)
    before = (r < c).astype(BF16)
    eid = lax.broadcasted_iota(jnp.int32, (n_exp, tm), 0)
    carry = carry_ref[...][:, 0:1]
    tcar_ref[0] = carry_ref[...]
    mix = (jnp.dot(att_ref[...], wo_ref[0:aw, :], preferred_element_type=F32)
           + jnp.dot(cv_ref[...], wo_ref[aw:, :], preferred_element_type=F32))
    x1 = x_ref[...] + mix
    x1_ref[...] = x1
    ms = jnp.mean(x1 * x1, axis=-1, keepdims=True)
    hn = (x1 * lax.rsqrt(ms + RMS_EPS) * g_ref[...]).astype(BF16)
    hn_ref[...] = hn
    logits = _nt_dot(rw_ref[...], hn) + rb_ref[...][:, 0:1]
    onehot = jnp.zeros((n_exp, tm), F32)
    vals, idxs = [], []
    for _ in range(TOP_K):
        m = jnp.max(logits, axis=0, keepdims=True)
        idx = jnp.min(jnp.where(logits == m, eid, n_exp), axis=0, keepdims=True)
        sel = eid == idx
        onehot = onehot + sel.astype(F32)
        logits = jnp.where(sel, -jnp.inf, logits)
        vals.append(m)
        idxs.append(idx)
    es = [jnp.exp(v - vals[0]) for v in vals]
    den = es[0] + es[1] + es[2] + es[3]
    local = jnp.dot(onehot.astype(BF16), before, preferred_element_type=F32)
    count = jnp.sum(onehot, axis=1, keepdims=True)
    padded = jnp.ceil(count * (1.0 / RUN)) * RUN
    er = lax.broadcasted_iota(jnp.int32, (n_exp, n_exp), 0)
    ec = lax.broadcasted_iota(jnp.int32, (n_exp, n_exp), 1)
    start = jnp.dot((ec < er).astype(BF16), jnp.broadcast_to(padded, (n_exp, LANES)).astype(BF16),
                    preferred_element_type=F32)[:, 0:1]
    pick = lambda k, table: jnp.sum(jnp.where(eid == idxs[k], table, 0.0), axis=0, keepdims=True)
    rows = ([ix.astype(F32) for ix in idxs] + [pick(k, local + carry) for k in range(TOP_K)])
    slots = [pick(k, local + start) for k in range(TOP_K)]
    rt_ref[...] = jnp.concatenate(rows + slots + [jnp.zeros((rt_ref.shape[0] - 3 * TOP_K, tm), F32)], axis=0)
    fields = jnp.concatenate(rows + [e / den for e in es] + [jnp.zeros((LANES - 3 * TOP_K, tm), F32)], axis=0)
    route_ref[...] = jnp.transpose(fields)
    carry = carry + count
    carry_ref[...] = jnp.broadcast_to(carry, carry_ref.shape)
    cnt_ref[...] = jnp.broadcast_to(carry, cnt_ref.shape)


def _store_slabs(ref, val, rows):
    for j in range(val.shape[1] // LANES):
        ref[pl.ds(j, rows, stride=SUBLANES), :] = val[:, j * LANES:(j + 1) * LANES]


def _load_slabs(ref, rows, dtype):
    return jnp.concatenate([ref[pl.ds(j, rows, stride=SUBLANES), :].astype(dtype) for j in range(SUBLANES)],
                           axis=1)


def _outproj_router(x2, att, cv, wo_bf, g, rwt_bf, rbt, carry_in, *, tm):
    n, d = x2.shape
    assert d == SUBLANES * LANES and n % tm == 0
    n_exp = rwt_bf.shape[0]
    row = lambda w: pl.BlockSpec((tm, w), lambda i: (i, 0))
    full = lambda a: pl.BlockSpec(a.shape, lambda i: (0,) * a.ndim)
    return pl.pallas_call(
        _outproj_router_kernel,
        grid=(n // tm,),
        in_specs=[row(d), row(att.shape[1]), row(cv.shape[1]), full(wo_bf), full(g), full(rwt_bf), full(rbt),
                  full(carry_in)],
        out_specs=[row(d), row(d), row(LANES),
                   pl.BlockSpec((ROUTE_T_ROWS, tm), lambda i: (0, i)),
                   pl.BlockSpec((1, n_exp, LANES), lambda i: (i, 0, 0)),
                   pl.BlockSpec((n_exp, LANES), lambda i: (0, 0))],
        out_shape=[jax.ShapeDtypeStruct((n, d), F32), jax.ShapeDtypeStruct((n, d), BF16),
                   jax.ShapeDtypeStruct((n, LANES), F32), jax.ShapeDtypeStruct((ROUTE_T_ROWS, n), F32),
                   jax.ShapeDtypeStruct((n // tm, n_exp, LANES), F32),
                   jax.ShapeDtypeStruct((n_exp, LANES), F32)],
        scratch_shapes=[pltpu.VMEM((n_exp, LANES), F32)],
        compiler_params=_cparams(("arbitrary",)),
        name="outproj_router",
    )(x2, att, cv, wo_bf, g, rwt_bf, rbt, carry_in)


def _dispatch_kernel(fill_ref, nfill_ref, tot_ref, dst_ref, rt_ref, hn_ref, hn2_ref,
                     xs_ref, zero_ref, stage_ref, sem, zsem, *, td, tme, n_first, chunk):
    i = pl.program_id(0)
    last = pl.num_programs(0) - 1
    par = i % 2
    n_slots = stage_ref.shape[1] // SUBLANES

    def fill(f, s):
        row = pl.multiple_of(fill_ref[f] * (tme * SUBLANES), tme * SUBLANES)
        return pltpu.make_async_copy(zero_ref, xs_ref.at[pl.ds(row, tme * SUBLANES)], zsem.at[s])

    def fills(lo, hi, s, act):
        def body(f, carry):
            act(fill(f, s))
            return carry

        lax.fori_loop(lo, hi, body, 0)

    @pl.when(i == 0)
    def _():
        zero_ref[...] = jnp.zeros(zero_ref.shape, F32)
        fills(0, nfill_ref[0], 0, lambda c: c.start())
        fills(nfill_ref[0], nfill_ref[1], 1, lambda c: c.start())
        fills(0, nfill_ref[0], 0, lambda c: c.wait())

    def group_by_expert(tok_ref):
        hn = tok_ref[...]
        slots = rt_ref[2 * TOP_K:3 * TOP_K, :].astype(jnp.int32)
        for c in range(n_slots // chunk):
            s = lax.broadcasted_iota(jnp.int32, (chunk, td), 0) + c * chunk
            hit = s == slots[0:1, :]
            for k in range(1, TOP_K):
                hit = hit | (s == slots[k:k + 1, :])
            rows = jnp.dot(jnp.where(hit, 1.0, 0.0).astype(BF16), hn, preferred_element_type=F32)
            _store_slabs(stage_ref.at[par, pl.ds(c * chunk * SUBLANES, chunk * SUBLANES)], rows, chunk)

    def run_copy(buf, src_slot, dst_row, runs=1):
        src = pl.multiple_of(src_slot * SUBLANES, RUN * SUBLANES)
        dst = pl.multiple_of(dst_row * SUBLANES, SUBLANES)
        return pltpu.make_async_copy(stage_ref.at[buf, pl.ds(src, runs * RUN * SUBLANES)],
                                     xs_ref.at[pl.ds(dst, runs * RUN * SUBLANES)], sem)

    def wait_runs(tile):
        def wait_many(q, carry):
            run_copy(0, 0, 0, ISSUE_UNROLL).wait()
            return carry

        def wait_one(j, carry):
            run_copy(0, 0, 0).wait()
            return carry

        n_many = tot_ref[tile] // ISSUE_UNROLL
        lax.fori_loop(0, n_many, wait_many, 0)
        lax.fori_loop(n_many * ISSUE_UNROLL, tot_ref[tile], wait_one, 0)

    @pl.when(i >= 2)
    def _():
        wait_runs(i - 2)

    @pl.when(i >= 1)
    def _():
        def issue(j):
            run_copy(1 - par, j * RUN, dst_ref[0, 0, j]).start()

        def issue_many(q, carry):
            for u in range(ISSUE_UNROLL):
                issue(q * ISSUE_UNROLL + u)
            return carry

        def issue_one(j, carry):
            issue(j)
            return carry

        n_full = tot_ref[i - 1] // ISSUE_UNROLL
        lax.fori_loop(0, n_full, issue_many, 0)
        lax.fori_loop(n_full * ISSUE_UNROLL, tot_ref[i - 1], issue_one, 0)

    @pl.when(i < n_first)
    def _():
        group_by_expert(hn_ref)

    @pl.when((i >= n_first) & (i < last))
    def _():
        group_by_expert(hn2_ref)

    @pl.when(i == last)
    def _():
        wait_runs(i - 1)
        fills(nfill_ref[0], nfill_ref[1], 1, lambda c: c.wait())


def _stage_slots(td):
    return td * TOP_K + N_EXPERTS * RUN


def _dest_blocks(dest, td):
    n = dest.shape[1]
    return dest.reshape(TOP_K, n // td, td).transpose(1, 0, 2).reshape(n // td, 1, TOP_K * td)


def _dispatch(hn_a, hn_b, route_t, fill_blocks, n_fill, runs_per_tile, run_dst, *, td, nb, tme):
    na, d = hn_a.shape
    assert na % td == 0 and hn_b.shape[0] % td == 0 and td % RUN == 0
    n_first = na // td
    n_second = hn_b.shape[0] // td
    tiles = n_first + n_second
    n_slots = _stage_slots(td)
    chunk = _tile(n_slots, 768, SUBLANES)
    tok = lambda m: pl.BlockSpec((td, d), m)
    grid_spec = pltpu.PrefetchScalarGridSpec(
        num_scalar_prefetch=3,
        grid=(tiles + 1,),
        in_specs=[pl.BlockSpec((1, 1, n_slots // RUN), lambda i, *_: (jnp.maximum(i - 1, 0), 0, 0),
                               memory_space=pltpu.SMEM),
                  pl.BlockSpec((ROUTE_T_ROWS, td), lambda i, *_: (0, jnp.minimum(i, tiles - 1))),
                  tok(lambda i, *_: (jnp.minimum(i, n_first - 1), 0)),
                  tok(lambda i, *_: (jnp.clip(i - n_first, 0, n_second - 1), 0))],
        out_specs=pl.BlockSpec(memory_space=pl.ANY),
        scratch_shapes=[pltpu.VMEM((tme * SUBLANES, LANES), F32),
                        pltpu.VMEM((2, n_slots * SUBLANES, LANES), F32),
                        pltpu.SemaphoreType.DMA(()), pltpu.SemaphoreType.DMA((2,))],
    )
    return pl.pallas_call(
        functools.partial(_dispatch_kernel, td=td, tme=tme, n_first=n_first, chunk=chunk),
        grid_spec=grid_spec,
        out_shape=jax.ShapeDtypeStruct((nb * tme * SUBLANES, LANES), F32),
        compiler_params=_cparams(("arbitrary",)),
        name="dispatch",
    )(fill_blocks, n_fill, runs_per_tile, run_dst, route_t, hn_a, hn_b)


def _experts_kernel(be_ref, bsrc_ref, nv_ref, nx_ref, nu_ref, x_ref, w1_hbm, b1_ref, w2_hbm, b2_ref,
                    y_ref, w1f_ref, w2f_ref, w1b_ref, w2b_ref, par_ref, wsem, *, tme):
    i = pl.program_id(0)
    e = be_ref[i]
    e_prev = be_ref[jnp.maximum(i - 1, 0)]
    d_ff = w2f_ref.shape[1]
    half = tme // 2

    def fetch(expert, s):
        return (pltpu.make_async_copy(w1_hbm.at[expert], w1f_ref.at[s], wsem.at[0, s]),
                pltpu.make_async_copy(w2_hbm.at[expert], w2f_ref.at[s], wsem.at[1, s]))

    @pl.when(i == 0)
    def _():
        par_ref[0] = 0
        for c in fetch(e, 0):
            c.start()

    @pl.when((i == 0) | (e != e_prev))
    def _():
        s = par_ref[0]
        for c in fetch(e, s):
            c.wait()
        w1b_ref[...] = w1f_ref[s].astype(BF16)
        w2b_ref[...] = w2f_ref[s].astype(BF16)
        nxt = nx_ref[i]

        @pl.when(nxt >= 0)
        def _():
            for c in fetch(nxt, 1 - s):
                c.start()

        par_ref[0] = 1 - s

    def ffn(x):
        h = jnp.dot(x, w1b_ref[...], preferred_element_type=F32) + b1_ref[0]
        x_glu = jnp.minimum(h[:, :d_ff], SWIGLU_LIMIT)
        x_lin = jnp.clip(h[:, d_ff:], -SWIGLU_LIMIT, SWIGLU_LIMIT)
        act = x_glu * _sigmoid(SWIGLU_ALPHA * x_glu) * (x_lin + 1.0)
        return jnp.dot(act.astype(BF16), w2b_ref[...], preferred_element_type=F32) + b2_ref[0]

    used = i < nu_ref[0]
    nv = nv_ref[i]

    @pl.when(used & (nv > half))
    def _():
        _store_slabs(y_ref, ffn(_load_slabs(x_ref, tme, BF16)), tme)

    @pl.when(used & (nv <= half))
    def _():
        rows = half * SUBLANES
        _store_slabs(y_ref.at[pl.ds(0, rows)], ffn(_load_slabs(x_ref.at[pl.ds(0, rows)], half, BF16)), half)
        y_ref[pl.ds(rows, rows), :] = jnp.zeros((rows, LANES), F32)

    @pl.when(i == nu_ref[0])
    def _():
        y_ref[...] = jnp.zeros(y_ref.shape, F32)


def _experts(xs, w1, b1, w2, b2, blk_exp, blk_src, blk_nvalid, blk_next, n_used, *, tme):
    n_exp, d, h2 = w1.shape
    d_ff = w2.shape[1]
    nb = xs.shape[0] // (tme * SUBLANES)
    slab = lambda m: pl.BlockSpec((tme * SUBLANES, LANES), m)
    grid_spec = pltpu.PrefetchScalarGridSpec(
        num_scalar_prefetch=5,
        grid=(nb,),
        in_specs=[slab(lambda i, be, bs, nv, nx, nu: (bs[i], 0)),
                  pl.BlockSpec(memory_space=pl.ANY),
                  pl.BlockSpec((1, 1, h2), lambda i, be, bs, nv, nx, nu: (be[i], 0, 0)),
                  pl.BlockSpec(memory_space=pl.ANY),
                  pl.BlockSpec((1, 1, d), lambda i, be, bs, nv, nx, nu: (be[i], 0, 0))],
        out_specs=slab(lambda i, be, bs, nv, nx, nu: (jnp.minimum(i, nu[0]), 0)),
        scratch_shapes=[pltpu.VMEM((2, d, h2), F32), pltpu.VMEM((2, d_ff, d), F32),
                        pltpu.VMEM((d, h2), BF16), pltpu.VMEM((d_ff, d), BF16),
                        pltpu.SMEM((1,), jnp.int32), pltpu.SemaphoreType.DMA((2, 2))],
    )
    return pl.pallas_call(
        functools.partial(_experts_kernel, tme=tme),
        grid_spec=grid_spec,
        out_shape=jax.ShapeDtypeStruct(xs.shape, F32),
        input_output_aliases={5: 0},
        compiler_params=_cparams(("arbitrary",)),
        name="experts",
    )(blk_exp, blk_src, blk_nvalid, blk_next, n_used, xs, w1, b1.reshape(n_exp, 1, h2), w2,
      b2.reshape(n_exp, 1, d))


def _combine_kernel(dcur_ref, dnext_ref, x1_ref, route_ref, g_ref, ys_ref, o_ref, buf_ref, sem, *, tc):
    i = pl.program_id(0)
    slot = i % 2

    def gather(dref, s):
        def issue(r, carry):
            row = pl.multiple_of(r * SUBLANES, SUBLANES)
            for k in range(TOP_K):
                d = pl.multiple_of(dref[0, 0, k * tc + r] * SUBLANES, SUBLANES)
                pltpu.make_async_copy(ys_ref.at[pl.ds(d, SUBLANES)], buf_ref.at[s, k, pl.ds(row, SUBLANES)],
                                      sem.at[s]).start(priority=k % 2)
            return carry

        lax.fori_loop(0, tc, issue, 0, unroll=8)

    @pl.when(i == 0)
    def _():
        gather(dcur_ref, 0)

    @pl.when(i + 1 < pl.num_programs(0))
    def _():
        gather(dnext_ref, 1 - slot)

    for k in range(TOP_K):
        pltpu.make_async_copy(ys_ref.at[pl.ds(0, tc * SUBLANES)], buf_ref.at[slot, k], sem.at[slot]).wait()
    route = route_ref[...]
    x1 = x1_ref[...]
    chunks = []
    for j in range(SUBLANES):
        acc = x1[:, j * LANES:(j + 1) * LANES]
        for k in range(TOP_K):
            acc = acc + (route[:, 2 * TOP_K + k:2 * TOP_K + k + 1]
                         * buf_ref[slot, k, pl.ds(j, tc, stride=SUBLANES), :])
        chunks.append(acc)
    y = jnp.concatenate(chunks, axis=1)
    ms = jnp.mean(y * y, axis=-1, keepdims=True)
    o_ref[...] = y * lax.rsqrt(ms + RMS_EPS) * g_ref[...]


def _combine(x1, route, dest, ys, g):
    n, d = x1.shape
    tc = _tile(n, 128, 8)
    steps = n // tc
    dest3 = _dest_blocks(dest, tc)
    dspec = lambda m: pl.BlockSpec((1, 1, tc * TOP_K), m, memory_space=pltpu.SMEM)
    return pl.pallas_call(
        functools.partial(_combine_kernel, tc=tc),
        grid=(steps,),
        in_specs=[dspec(lambda i: (i, 0, 0)), dspec(lambda i: (jnp.minimum(i + 1, steps - 1), 0, 0)),
                  pl.BlockSpec((tc, d), lambda i: (i, 0)),
                  pl.BlockSpec((tc, LANES), lambda i: (i, 0)),
                  pl.BlockSpec((1, d), lambda i: (0, 0)),
                  pl.BlockSpec(memory_space=pl.ANY)],
        out_specs=pl.BlockSpec((tc, d), lambda i: (i, 0)),
        out_shape=jax.ShapeDtypeStruct((n, d), F32),
        scratch_shapes=[pltpu.VMEM((2, TOP_K, tc * SUBLANES, LANES), F32), pltpu.SemaphoreType.DMA((2,))],
        compiler_params=_cparams(("arbitrary",)),
        name="combine",
    )(dest3, dest3, x1, route, g, ys)


def _dest_rows(route_t, blk_start, tme):
    eidx = route_t[0:TOP_K].astype(jnp.int32)
    rank = route_t[TOP_K:2 * TOP_K].astype(jnp.int32)
    experts = jnp.arange(N_EXPERTS, dtype=jnp.int32)[:, None, None]
    first = jnp.sum(jnp.where(eidx[None] == experts, blk_start[:, None, None], 0), axis=0)
    return (first * tme + rank).astype(jnp.int32)


def _routing_tables(counts_f, *, tme, nb):
    counts = counts_f[:, 0].astype(jnp.int32)
    nblk = jnp.where(counts > 0, (counts + RUN - 1 + tme - 1) // tme, 0)
    blk_end = jnp.cumsum(nblk)
    blk_start = blk_end - nblk
    n_used = blk_end[-1]
    b = jnp.arange(nb, dtype=jnp.int32)
    used = b < n_used
    blk_exp = jnp.minimum(jnp.sum((b[:, None] >= blk_end[None, :]).astype(jnp.int32), axis=1), N_EXPERTS - 1)
    last_exp = jnp.max(jnp.where(nblk > 0, jnp.arange(N_EXPERTS, dtype=jnp.int32), 0))
    blk_exp = jnp.where(used, blk_exp, last_exp).astype(jnp.int32)
    blk_src = jnp.minimum(b, n_used - 1).astype(jnp.int32)
    experts = jnp.arange(N_EXPERTS, dtype=jnp.int32)
    mine = (b[:, None] >= blk_start[None, :]) & (b[:, None] < blk_end[None, :])
    nvalid = jnp.sum(jnp.where(mine, counts[None, :] - (b[:, None] - blk_start[None, :]) * tme, 0), axis=1)
    nvalid = jnp.clip(nvalid, 0, tme).astype(jnp.int32)
    later = (experts[None, :] > experts[:, None]) & (nblk[None, :] > 0)
    nxt_e = jnp.min(jnp.where(later, experts[None, :], N_EXPERTS), axis=1)
    nxt_e = jnp.where(nxt_e == N_EXPERTS, -1, nxt_e)
    blk_next = jnp.sum(jnp.where(blk_exp[:, None] == experts[None, :], nxt_e[None, :], 0), axis=1).astype(jnp.int32)
    partial = used & (nvalid < tme)
    fill_order = jnp.where(partial, 0, jnp.where(used, 2, 1))
    fill_blocks = jnp.argsort(fill_order, stable=True).astype(jnp.int32)
    n_fill = jnp.stack([jnp.sum(partial), jnp.sum(partial | ~used)]).astype(jnp.int32)
    return (blk_start, blk_exp, blk_src, nvalid, blk_next, n_used.reshape(1).astype(jnp.int32), fill_blocks,
            n_fill)


def _run_tables(tile_carry_f, counts_f, blk_start, *, td, tme):
    before = tile_carry_f[:, :, 0].astype(jnp.int32)
    counts = counts_f[:, 0].astype(jnp.int32)
    in_tile = jnp.concatenate([before[1:], counts[None, :]], axis=0) - before
    n_runs = (in_tile + RUN - 1) // RUN
    ends = jnp.cumsum(n_runs, axis=1)
    j = jnp.arange(_stage_slots(td) // RUN, dtype=jnp.int32)
    owner = jnp.sum((j[None, :, None] >= ends[:, None, :]).astype(jnp.int32), axis=2)
    base = blk_start[None, :] * tme + before - (ends - n_runs) * RUN
    mine = owner[:, :, None] == jnp.arange(N_EXPERTS, dtype=jnp.int32)[None, None, :]
    dst = jnp.sum(jnp.where(mine, base[:, None, :], 0), axis=2) + j[None, :] * RUN
    return ends[:, -1].astype(jnp.int32), dst[:, None, :].astype(jnp.int32)


def kernel(x_prompt, x_sample, cache_k, cache_v, state_conv, attn_norm_g, w_in, attn_sinks, conv_w, conv_b,
           conv_ln_g, conv_ln_b, w_out, ffn_norm_g, router_w, router_b, w1, b1, w2, b2, final_norm_g):
    depth = w_in.shape[0]
    assert depth == 1, "single-layer step"
    bp, sp, d = x_prompt.shape
    bs, ss, _ = x_sample.shape
    cw = conv_w.shape[2]
    np_, ns = bp * sp, bs * ss
    n_tok = np_ + ns
    assert sp % WINDOW == 0

    xp2 = x_prompt.reshape(np_, d)
    xs2 = x_sample.reshape(ns, d)
    w_in_bf = w_in[0].astype(BF16)
    w_out_bf = w_out[0].astype(BF16)
    g_attn = attn_norm_g[0].reshape(1, d)
    g_ffn = ffn_norm_g[0].reshape(1, d)
    sinks = attn_sinks[0]
    vec = lambda a: a.reshape(1, cw)

    tab_p = _rope_tables(jnp.arange(sp, dtype=jnp.int32))
    tms = _tile(ns, 512, max(ss, 16))
    tab_s = _rope_tables(PAST_LEN + (jnp.arange(tms, dtype=jnp.int32) % ss))
    qp, kp, vp, kfp, vfp, ap = _in_proj(xp2, g_attn, w_in_bf, tab_p, seq_period=sp, q_dtype=BF16, conv_width=cw)
    qs, _, _, kfs, vfs, as_ = _in_proj(xs2, g_attn, w_in_bf, tab_s, seq_period=None, q_dtype=F32, conv_width=cw)

    att_p = _attn_prompt(qp, kp, vp, sinks, batch=bp, seq=sp)
    ck = cache_k[0].reshape(bs, WINDOW, KV_WIDTH)
    cv_ = cache_v[0].reshape(bs, WINDOW, KV_WIDTH)
    att_s, nk_s, nv_s = _attn_sample(qs, kfs, vfs, ck, cv_, sinks, batch=bs, ts=ss)

    cv_p = _conv_prompt(ap, conv_w[0], vec(conv_b[0]), vec(conv_ln_g[0]), vec(conv_ln_b[0]), batch=bp, seq=sp)
    cv_s = _conv_sample(as_, state_conv[0], conv_w[0], vec(conv_b[0]), vec(conv_ln_g[0]), vec(conv_ln_b[0]),
                        batch=bs, ts=ss)

    n_exp = router_w.shape[2]
    assert n_exp == N_EXPERTS
    rwt_bf = router_w[0].T.astype(BF16)
    rbt = jnp.broadcast_to(router_b[0][:, None], (n_exp, LANES))
    zero_carry = jnp.zeros((n_exp, LANES), F32)
    tr = _tile(ns, 512, 16)
    assert np_ % tr == 0
    x1p, hnp, route_p, rt_p, tcar_p, cnt_p = _outproj_router(xp2, att_p, cv_p, w_out_bf, g_ffn, rwt_bf, rbt,
                                                             zero_carry, tm=tr)
    x1s, hns, route_s, rt_s, tcar_s, cnt = _outproj_router(xs2, att_s, cv_s, w_out_bf, g_ffn, rwt_bf, rbt,
                                                           cnt_p, tm=tr)

    tme = EXPERT_BLOCK_ROWS
    nb = -(-(n_tok * TOP_K + N_EXPERTS * (tme - 1 + RUN - 1)) // tme)
    blk_start, blk_exp, blk_src, blk_nvalid, blk_next, n_used, fill_blocks, n_fill = _routing_tables(
        cnt, tme=tme, nb=nb)
    dest_p = _dest_rows(rt_p, blk_start, tme)
    dest_s = _dest_rows(rt_s, blk_start, tme)
    runs_per_tile, run_dst = _run_tables(jnp.concatenate([tcar_p, tcar_s], axis=0), cnt, blk_start, td=tr, tme=tme)
    xs_sorted = _dispatch(hnp, hns, jnp.concatenate([rt_p, rt_s], axis=1), fill_blocks, n_fill, runs_per_tile,
                          run_dst, td=tr, nb=nb, tme=tme)
    ys = _experts(xs_sorted, w1[0], b1[0], w2[0], b2[0], blk_exp, blk_src, blk_nvalid, blk_next, n_used, tme=tme)
    g_fin = final_norm_g.reshape(1, d)
    y_p = _combine(x1p, route_p, dest_p, ys, g_fin)
    y_s = _combine(x1s, route_s, dest_s, ys, g_fin)

    kv5 = lambda t, bb: t.reshape(bb, -1, KV_WIDTH)[:, -WINDOW:].reshape(bb, WINDOW, N_KV_HEADS, HEAD_DIM)
    new_k_p = kv5(kfp, bp)[None]
    new_v_p = kv5(vfp, bp)[None]
    ctx = CONV_K - 1
    new_c_p = ap.reshape(bp, sp, cw)[:, -ctx:][None]
    new_c_s = jnp.concatenate([state_conv[0], as_.reshape(bs, ss, cw)], axis=1)[:, -ctx:][None]
    return (y_p.reshape(bp, sp, d), y_s.reshape(bs, ss, d), new_k_p, new_v_p, new_c_p,
            kv5(nk_s, bs)[None], kv5(nv_s, bs)[None], new_c_s)
```

```python
import functools

import jax
import jax.numpy as jnp
from jax import lax
from jax.experimental import pallas as pl
from jax.experimental.pallas import tpu as pltpu

F32 = jnp.float32
BF16 = jnp.bfloat16

HEAD_DIM = 64
N_Q_HEADS = 8
N_KV_HEADS = 2
WINDOW = 128
ROPE_THETA = 500000.0
ROPE_DIM = 16
CONV_K = 31
N_EXPERTS = 32
TOP_K = 4
SWIGLU_LIMIT = 7.0
SWIGLU_ALPHA = 1.702
RMS_EPS = 1e-5
LN_EPS = 1e-5
PAST_LEN = 16384

LANES = 128
SUBLANES = 8
CONV_HALO = 32
VMEM_LIMIT = 56 * 1024 * 1024
EXPERT_BLOCK_ROWS = 512
RUN = 8
ROUTE_T_ROWS = 2 * TOP_K
ISSUE_UNROLL = 8

ATTN_WIDTH = N_Q_HEADS * HEAD_DIM
KV_WIDTH = N_KV_HEADS * HEAD_DIM


def _tile(n, pref, mult=8):
    t = min(pref, n)
    while t > 0 and (n % t or t % mult):
        t -= 1
    assert t > 0, (n, pref, mult)
    return t


def _cparams(sem):
    return pltpu.CompilerParams(dimension_semantics=sem, vmem_limit_bytes=VMEM_LIMIT)


def _sigmoid(x):
    return 1.0 / (1.0 + jnp.exp(-x))


def _rope_tables(pos):
    half = ROPE_DIM // 2
    inv_freq = jnp.power(jnp.float32(ROPE_THETA), -jnp.arange(half, dtype=F32) * 2.0 / ROPE_DIM)
    ang = pos.astype(F32)[:, None] * inv_freq[None, :]
    cos, sin = jnp.cos(ang), jnp.sin(ang)
    l64 = jnp.arange(LANES) % HEAD_DIM
    f = l64 % half
    cos_l, sin_l = cos[:, f], sin[:, f]
    c = jnp.where(l64 < ROPE_DIM, cos_l, 1.0)
    s1 = jnp.where(l64 < half, -sin_l, 0.0)
    s2 = jnp.where((l64 >= half) & (l64 < ROPE_DIM), sin_l, 0.0)
    return c.astype(F32), s1.astype(F32), s2.astype(F32)


def _inproj_kernel(x_ref, g_ref, w_ref, c_ref, s1_ref, s2_ref,
                   q_ref, k_ref, v_ref, kf_ref, vf_ref, a_ref, *, conv_width):
    x = x_ref[...]
    ms = jnp.mean(x * x, axis=-1, keepdims=True)
    h = (x * lax.rsqrt(ms + RMS_EPS) * g_ref[...]).astype(BF16)
    z = jnp.dot(h, w_ref[...], preferred_element_type=F32)
    c, s1, s2 = c_ref[...], s1_ref[...], s2_ref[...]
    half = ROPE_DIM // 2

    def rot(t):
        return t * c + pltpu.roll(t, LANES - half, 1) * s1 + pltpu.roll(t, half, 1) * s2

    scale = HEAD_DIM ** -0.5
    for j in range(ATTN_WIDTH // LANES):
        q_ref[:, j * LANES:(j + 1) * LANES] = (rot(z[:, j * LANES:(j + 1) * LANES]) * scale).astype(q_ref.dtype)
    k0 = ATTN_WIDTH
    kr = rot(z[:, k0:k0 + KV_WIDTH])
    k_ref[...] = kr.astype(BF16)
    kf_ref[...] = kr
    v0 = k0 + KV_WIDTH
    vv = z[:, v0:v0 + KV_WIDTH]
    v_ref[...] = vv.astype(BF16)
    vf_ref[...] = vv
    u0 = v0 + KV_WIDTH
    g0 = u0 + conv_width
    a_ref[...] = z[:, u0:g0] * _sigmoid(z[:, g0:g0 + conv_width])


def _in_proj(x2, g, w_bf, tables, *, seq_period, q_dtype, conv_width):
    n, d = x2.shape
    in_w = w_bf.shape[1]
    if seq_period is None:
        tm = tables[0].shape[0]
        tmap = lambda i: (0, 0)
    else:
        tm = _tile(seq_period, 1024, 16)
        per = seq_period // tm
        tmap = lambda i: (i % per, 0)
    assert n % tm == 0
    row = lambda w: pl.BlockSpec((tm, w), lambda i: (i, 0))
    tab = pl.BlockSpec((tm, LANES), tmap)
    return pl.pallas_call(
        functools.partial(_inproj_kernel, conv_width=conv_width),
        grid=(n // tm,),
        in_specs=[row(d), pl.BlockSpec((1, d), lambda i: (0, 0)),
                  pl.BlockSpec((d, in_w), lambda i: (0, 0)), tab, tab, tab],
        out_specs=[row(ATTN_WIDTH), row(KV_WIDTH), row(KV_WIDTH), row(KV_WIDTH), row(KV_WIDTH),
                   row(conv_width)],
        out_shape=[jax.ShapeDtypeStruct((n, ATTN_WIDTH), q_dtype),
                   jax.ShapeDtypeStruct((n, KV_WIDTH), BF16),
                   jax.ShapeDtypeStruct((n, KV_WIDTH), BF16),
                   jax.ShapeDtypeStruct((n, KV_WIDTH), F32),
                   jax.ShapeDtypeStruct((n, KV_WIDTH), F32),
                   jax.ShapeDtypeStruct((n, conv_width), F32)],
        compiler_params=_cparams(("parallel",)),
        name="in_proj",
    )(x2, g, w_bf, *tables)


def _dup_head(t, h):
    sw = pltpu.roll(t, HEAD_DIM, 1)
    low = lax.broadcasted_iota(jnp.int32, t.shape, 1) < HEAD_DIM
    return jnp.where(low, t, sw) if h == 0 else jnp.where(low, sw, t)


def _nt_dot(a, b):
    return lax.dot_general(a, b, (((1,), (1,)), ((), ())), preferred_element_type=F32)


def _attn_prompt_kernel(sink_ref, q_ref, kc_ref, kp_ref, vc_ref, vp_ref, o_ref, *, qb):
    j = pl.program_id(1)
    w = WINDOW
    k_all = jnp.concatenate([kp_ref[...], kc_ref[...]], axis=0).astype(F32)
    v_all = jnp.concatenate([vp_ref[...], vc_ref[...]], axis=0).astype(F32)
    r = lax.broadcasted_iota(jnp.int32, (w, 2 * w), 0)
    kk = lax.broadcasted_iota(jnp.int32, (w, 2 * w), 1)
    band = (kk > r) & (kk <= r + w)
    low = lax.broadcasted_iota(jnp.int32, (w, LANES), 1) < HEAD_DIM
    zero = jnp.zeros((w, LANES), BF16)
    group = N_Q_HEADS // N_KV_HEADS
    for h in range(N_KV_HEADS):
        kd_all = _dup_head(k_all, h).astype(BF16)
        vd_all = _dup_head(v_all, h).astype(BF16)
        for sub in range(qb):
            valid = band & ((kk >= w) | (j > 0)) if sub == 0 else band
            kd = kd_all[sub * w:(sub + 2) * w, :]
            vd = vd_all[sub * w:(sub + 2) * w, :]
            for jj in range(group // 2):
                col = (h * group // 2 + jj) * LANES
                qv = q_ref[sub * w:(sub + 1) * w, col:col + LANES]
                halves = []
                for half in range(2):
                    head = h * group + jj * 2 + half
                    qm = jnp.where(low if half == 0 else ~low, qv, zero)
                    s = jnp.where(valid, _nt_dot(qm, kd), -jnp.inf)
                    sink = sink_ref[head]
                    m = jnp.maximum(jnp.max(s, axis=-1, keepdims=True), sink)
                    p = jnp.exp(s - m)
                    den = jnp.sum(p, axis=-1, keepdims=True) + jnp.exp(sink - m)
                    o = jnp.dot(p.astype(BF16), vd, preferred_element_type=F32)
                    halves.append(o / den)
                o_ref[sub * w:(sub + 1) * w, col:col + LANES] = (
                    jnp.where(low, halves[0], halves[1]).astype(o_ref.dtype))


def _attn_prompt(q, k, v, sinks, *, batch, seq):
    nb = seq // WINDOW
    qb = next(c for c in (4, 2, 1) if nb % c == 0)
    steps = nb // qb
    cur = lambda b, j: (b * steps + j, 0)
    prev = lambda b, j: (b * nb + jnp.maximum(j * qb - 1, 0), 0)
    return pl.pallas_call(
        functools.partial(_attn_prompt_kernel, qb=qb),
        grid=(batch, steps),
        in_specs=[pl.BlockSpec(memory_space=pltpu.SMEM),
                  pl.BlockSpec((qb * WINDOW, ATTN_WIDTH), cur),
                  pl.BlockSpec((qb * WINDOW, KV_WIDTH), cur), pl.BlockSpec((WINDOW, KV_WIDTH), prev),
                  pl.BlockSpec((qb * WINDOW, KV_WIDTH), cur), pl.BlockSpec((WINDOW, KV_WIDTH), prev)],
        out_specs=pl.BlockSpec((qb * WINDOW, ATTN_WIDTH), cur),
        out_shape=jax.ShapeDtypeStruct((batch * seq, ATTN_WIDTH), BF16),
        compiler_params=_cparams(("parallel", "parallel")),
        name="attn_prompt",
    )(sinks, q, k, k, v, v)


def _attn_sample_kernel(sink_ref, q_ref, kn_ref, vn_ref, ck_ref, cv_ref, o_ref, nk_ref, nv_ref, *, gb, ts):
    w = WINDOW
    group = N_Q_HEADS // N_KV_HEADS
    rows = group * ts
    low = lax.broadcasted_iota(jnp.int32, (ts, LANES), 1) < HEAD_DIM
    pad = jnp.zeros((ts, LANES), F32)
    s_c, s_n, v_dup = [], [], []
    for b in range(gb):
        kc, vc = ck_ref[b], cv_ref[b]
        kn, vn = kn_ref[b * ts:(b + 1) * ts, :], vn_ref[b * ts:(b + 1) * ts, :]
        nk_ref[b, 0:w - ts, :] = kc[ts:, :]
        nk_ref[b, w - ts:, :] = kn
        nv_ref[b, 0:w - ts, :] = vc[ts:, :]
        nv_ref[b, w - ts:, :] = vn
        knp = jnp.concatenate([kn, pad], axis=0)
        vnp = jnp.concatenate([vn, pad], axis=0)
        qb = q_ref[b * ts:(b + 1) * ts, :]
        for h in range(N_KV_HEADS):
            parts = []
            for jj in range(group // 2):
                col = (h * group // 2 + jj) * LANES
                qv = qb[:, col:col + LANES]
                parts += [jnp.where(low, qv, 0.0), jnp.where(low, 0.0, qv)]
            lhs = jnp.concatenate(parts, axis=0).astype(BF16)
            s_c.append(_nt_dot(lhs, _dup_head(kc, h).astype(BF16)))
            s_n.append(_nt_dot(lhs, _dup_head(knp, h).astype(BF16)))
            v_dup.append((_dup_head(vc, h).astype(BF16), _dup_head(vnp, h).astype(BF16)))
    s_c = jnp.concatenate(s_c, axis=0)
    s_n = jnp.concatenate(s_n, axis=0)
    n_rows = s_c.shape[0]
    ridx = lax.broadcasted_iota(jnp.int32, (n_rows, 1), 0)
    t_row = ridx % ts
    head_row = (ridx // ts) % N_Q_HEADS
    sink = jnp.zeros((n_rows, 1), F32)
    for hd in range(N_Q_HEADS):
        sink = jnp.where(head_row == hd, sink_ref[hd], sink)
    c_idx = lax.broadcasted_iota(jnp.int32, (n_rows, w), 1)
    n_idx = lax.broadcasted_iota(jnp.int32, (n_rows, 2 * ts), 1)
    s_c = jnp.where(c_idx > t_row, s_c, -jnp.inf)
    s_n = jnp.where(n_idx <= t_row, s_n, -jnp.inf)
    m = jnp.maximum(jnp.maximum(jnp.max(s_c, axis=-1, keepdims=True), jnp.max(s_n, axis=-1, keepdims=True)), sink)
    p_c = jnp.exp(s_c - m)
    p_n = jnp.exp(s_n - m)
    den = jnp.sum(p_c, axis=-1, keepdims=True) + jnp.sum(p_n, axis=-1, keepdims=True) + jnp.exp(sink - m)
    p_c = p_c.astype(BF16)
    p_n = p_n.astype(BF16)
    outs = []
    for b in range(gb):
        cols = []
        for h in range(N_KV_HEADS):
            ci = b * N_KV_HEADS + h
            sl = slice(ci * rows, (ci + 1) * rows)
            vdc, vdn = v_dup[ci]
            o = (jnp.dot(p_c[sl], vdc, preferred_element_type=F32)
                 + jnp.dot(p_n[sl], vdn, preferred_element_type=F32)) / den[sl]
            for jj in range(group // 2):
                lo_part = o[(2 * jj) * ts:(2 * jj + 1) * ts, :]
                hi_part = o[(2 * jj + 1) * ts:(2 * jj + 2) * ts, :]
                cols.append(jnp.where(low, lo_part, hi_part))
        outs.append(jnp.concatenate(cols, axis=1))
    o_ref[...] = jnp.concatenate(outs, axis=0).astype(o_ref.dtype)


def _attn_sample(q, kf, vf, cache_k, cache_v, sinks, *, batch, ts):
    assert ts % 8 == 0 and ts <= WINDOW
    gb = _tile(batch, 8, 2)
    tok = lambda w: pl.BlockSpec((gb * ts, w), lambda i: (i, 0))
    cache = pl.BlockSpec((gb, WINDOW, KV_WIDTH), lambda i: (i, 0, 0))
    cshape = jax.ShapeDtypeStruct((batch, WINDOW, KV_WIDTH), F32)
    return pl.pallas_call(
        functools.partial(_attn_sample_kernel, gb=gb, ts=ts),
        grid=(batch // gb,),
        in_specs=[pl.BlockSpec(memory_space=pltpu.SMEM), tok(ATTN_WIDTH), tok(KV_WIDTH), tok(KV_WIDTH),
                  cache, cache],
        out_specs=[tok(ATTN_WIDTH), cache, cache],
        out_shape=[jax.ShapeDtypeStruct((batch * ts, ATTN_WIDTH), BF16), cshape, cshape],
        compiler_params=_cparams(("parallel",)),
        name="attn_sample",
    )(sinks, q, kf, vf, cache_k, cache_v)


def _ln_swish(acc, b, lg, lb):
    y = acc + b
    mu = jnp.mean(y, axis=-1, keepdims=True)
    yc = y - mu
    var = jnp.mean(yc * yc, axis=-1, keepdims=True)
    yn = yc * lax.rsqrt(var + LN_EPS) * lg + lb
    return yn * _sigmoid(yn)


def _conv_prompt_kernel(a_ref, ap_ref, w_ref, b_ref, lg_ref, lb_ref, o_ref, win_ref, *, tt, rc):
    j = pl.program_id(1)
    n = CONV_HALO + tt
    win = jnp.concatenate([jnp.where(j > 0, ap_ref[...], 0.0), a_ref[...]], axis=0)
    win_ref[0] = win
    for r in range(1, SUBLANES):
        win_ref[r] = pltpu.roll(win, n - r, 0)
    off = CONV_HALO - (CONV_K - 1)
    b, lg, lb = b_ref[...], lg_ref[...], lb_ref[...]
    for c in range(tt // rc):
        acc = jnp.zeros((rc, a_ref.shape[1]), F32)
        for k in range(CONV_K):
            s = off + k
            base = c * rc + (s // SUBLANES) * SUBLANES
            wk = jnp.concatenate([w_ref[k]] * (rc // SUBLANES), axis=0)
            acc = acc + wk * win_ref[s % SUBLANES, base:base + rc, :]
        o_ref[c * rc:(c + 1) * rc, :] = _ln_swish(acc, b, lg, lb).astype(o_ref.dtype)


def _conv_prompt(a, w, b, lg, lb, *, batch, seq):
    cw = a.shape[1]
    tt = _tile(seq, 512, CONV_HALO)
    rc = _tile(tt, 32, 16)
    nt = seq // tt
    per = tt // CONV_HALO
    cur = lambda bb, j: (bb * nt + j, 0)
    prev = lambda bb, j: (jnp.maximum((bb * nt + j) * per - 1, 0), 0)
    vec = pl.BlockSpec((1, cw), lambda bb, j: (0, 0))
    return pl.pallas_call(
        functools.partial(_conv_prompt_kernel, tt=tt, rc=rc),
        grid=(batch, nt),
        in_specs=[pl.BlockSpec((tt, cw), cur), pl.BlockSpec((CONV_HALO, cw), prev),
                  pl.BlockSpec((CONV_K, SUBLANES, cw), lambda bb, j: (0, 0, 0)), vec, vec, vec],
        out_specs=pl.BlockSpec((tt, cw), cur),
        out_shape=jax.ShapeDtypeStruct((batch * seq, cw), BF16),
        scratch_shapes=[pltpu.VMEM((SUBLANES, CONV_HALO + tt, cw), F32)],
        compiler_params=_cparams(("parallel", "parallel")),
        name="conv_prompt",
    )(a, a, jnp.broadcast_to(w[:, None, :], (CONV_K, SUBLANES, cw)), b, lg, lb)


def _conv_sample_kernel(a_ref, st_ref, w_ref, b_ref, lg_ref, lb_ref, o_ref, win_ref, *, gb, ts):
    ctx = CONV_K - 1
    b, lg, lb = b_ref[...], lg_ref[...], lb_ref[...]
    for bb in range(gb):
        win_ref[bb, 0:ctx, :] = st_ref[bb]
        win_ref[bb, ctx:ctx + ts, :] = a_ref[bb * ts:(bb + 1) * ts, :]
    outs = []
    for bb in range(gb):
        acc = jnp.zeros((ts, a_ref.shape[1]), F32)
        for k in range(CONV_K):
            wk = jnp.concatenate([w_ref[k]] * (ts // SUBLANES), axis=0)
            acc = acc + wk * win_ref[bb, k:k + ts, :]
        outs.append(_ln_swish(acc, b, lg, lb))
    o_ref[...] = jnp.concatenate(outs, axis=0).astype(o_ref.dtype)


def _conv_sample(a, state, w, b, lg, lb, *, batch, ts):
    cw = a.shape[1]
    ctx = CONV_K - 1
    gb = _tile(batch, 8, 2)
    vec = pl.BlockSpec((1, cw), lambda i: (0, 0))
    return pl.pallas_call(
        functools.partial(_conv_sample_kernel, gb=gb, ts=ts),
        grid=(batch // gb,),
        in_specs=[pl.BlockSpec((gb * ts, cw), lambda i: (i, 0)),
                  pl.BlockSpec((gb, ctx, cw), lambda i: (i, 0, 0)),
                  pl.BlockSpec((CONV_K, SUBLANES, cw), lambda i: (0, 0, 0)), vec, vec, vec],
        out_specs=pl.BlockSpec((gb * ts, cw), lambda i: (i, 0)),
        out_shape=jax.ShapeDtypeStruct((batch * ts, cw), BF16),
        scratch_shapes=[pltpu.VMEM((gb, ctx + ts + 2, cw), F32)],
        compiler_params=_cparams(("parallel",)),
        name="conv_sample",
    )(a, state, jnp.broadcast_to(w[:, None, :], (CONV_K, SUBLANES, cw)), b, lg, lb)


def _outproj_router_kernel(x_ref, att_ref, cv_ref, wo_ref, g_ref, rw_ref, rb_ref, cin_ref,
                           x1_ref, hn_ref, rt_ref, tcar_ref, cnt_ref, carry_ref):
    i = pl.program_id(0)

    @pl.when(i == 0)
    def _():
        carry_ref[...] = cin_ref[...]

    tm = x_ref.shape[0]
    aw = att_ref.shape[1]
    n_exp = rw_ref.shape[0]
    r = lax.broadcasted_iota(jnp.int32, (tm, tm), 0)
    c = lax.broadcasted_iota(jnp.int32, (tm, tm), 1)
    before = (r < c).astype(BF16)
    eid = lax.broadcasted_iota(jnp.int32, (n_exp, tm), 0)
    carry = carry_ref[...][:, 0:1]
    tcar_ref[0] = carry_ref[...]
    mix = (jnp.dot(att_ref[...], wo_ref[0:aw, :], preferred_element_type=F32)
           + jnp.dot(cv_ref[...], wo_ref[aw:, :], preferred_element_type=F32))
    x1 = x_ref[...] + mix
    x1_ref[...] = x1
    ms = jnp.mean(x1 * x1, axis=-1, keepdims=True)
    hn = (x1 * lax.rsqrt(ms + RMS_EPS) * g_ref[...]).astype(BF16)
    hn_ref[...] = hn
    logits = _nt_dot(rw_ref[...], hn) + rb_ref[...][:, 0:1]
    onehot = jnp.zeros((n_exp, tm), F32)
    vals, idxs = [], []
    for _ in range(TOP_K):
        m = jnp.max(logits, axis=0, keepdims=True)
        idx = jnp.min(jnp.where(logits == m, eid, n_exp), axis=0, keepdims=True)
        sel = eid == idx
        onehot = onehot + sel.astype(F32)
        logits = jnp.where(sel, -jnp.inf, logits)
        vals.append(m)
        idxs.append(idx)
    es = [jnp.exp(v - vals[0]) for v in vals]
    den = es[0] + es[1] + es[2] + es[3]
    local = jnp.dot(onehot.astype(BF16), before, preferred_element_type=F32)
    count = jnp.sum(onehot, axis=1, keepdims=True)
    padded = jnp.ceil(count * (1.0 / RUN)) * RUN
    er = lax.broadcasted_iota(jnp.int32, (n_exp, n_exp), 0)
    ec = lax.broadcasted_iota(jnp.int32, (n_exp, n_exp), 1)
    start = jnp.dot((ec < er).astype(BF16), jnp.broadcast_to(padded, (n_exp, LANES)).astype(BF16),
                    preferred_element_type=F32)[:, 0:1]
    slot_of = local + start
    slots = [jnp.sum(jnp.where(eid == idxs[k], slot_of, 0.0), axis=0, keepdims=True) for k in range(TOP_K)]
    rt_ref[...] = jnp.concatenate(slots + [e / den for e in es], axis=0)
    carry = carry + count
    carry_ref[...] = jnp.broadcast_to(carry, carry_ref.shape)
    cnt_ref[...] = jnp.broadcast_to(carry, cnt_ref.shape)


def _store_slabs(ref, val, rows):
    for j in range(val.shape[1] // LANES):
        ref[pl.ds(j, rows, stride=SUBLANES), :] = val[:, j * LANES:(j + 1) * LANES]


def _load_slabs(ref, rows, dtype):
    return jnp.concatenate([ref[pl.ds(j, rows, stride=SUBLANES), :].astype(dtype) for j in range(SUBLANES)],
                           axis=1)


def _outproj_router(x2, att, cv, wo_bf, g, rwt_bf, rbt, carry_in, *, tm):
    n, d = x2.shape
    assert d == SUBLANES * LANES and n % tm == 0
    n_exp = rwt_bf.shape[0]
    row = lambda w: pl.BlockSpec((tm, w), lambda i: (i, 0))
    full = lambda a: pl.BlockSpec(a.shape, lambda i: (0,) * a.ndim)
    return pl.pallas_call(
        _outproj_router_kernel,
        grid=(n // tm,),
        in_specs=[row(d), row(att.shape[1]), row(cv.shape[1]), full(wo_bf), full(g), full(rwt_bf), full(rbt),
                  full(carry_in)],
        out_specs=[row(d), row(d),
                   pl.BlockSpec((ROUTE_T_ROWS, tm), lambda i: (0, i)),
                   pl.BlockSpec((1, n_exp, LANES), lambda i: (i, 0, 0)),
                   pl.BlockSpec((n_exp, LANES), lambda i: (0, 0))],
        out_shape=[jax.ShapeDtypeStruct((n, d), F32), jax.ShapeDtypeStruct((n, d), BF16),
                   jax.ShapeDtypeStruct((ROUTE_T_ROWS, n), F32),
                   jax.ShapeDtypeStruct((n // tm, n_exp, LANES), F32),
                   jax.ShapeDtypeStruct((n_exp, LANES), F32)],
        scratch_shapes=[pltpu.VMEM((n_exp, LANES), F32)],
        compiler_params=_cparams(("arbitrary",)),
        name="outproj_router",
    )(x2, att, cv, wo_bf, g, rwt_bf, rbt, carry_in)


def _dispatch_kernel(fill_ref, nfill_ref, tot_ref, dst_ref, rt_ref, hn_ref, hn2_ref,
                     xs_ref, zero_ref, stage_ref, sem, zsem, *, td, tme, n_first, chunk):
    i = pl.program_id(0)
    last = pl.num_programs(0) - 1
    par = i % 2
    n_slots = stage_ref.shape[1] // SUBLANES

    def fill(f, s):
        row = pl.multiple_of(fill_ref[f] * (tme * SUBLANES), tme * SUBLANES)
        return pltpu.make_async_copy(zero_ref, xs_ref.at[pl.ds(row, tme * SUBLANES)], zsem.at[s])

    def fills(lo, hi, s, act):
        def body(f, carry):
            act(fill(f, s))
            return carry

        lax.fori_loop(lo, hi, body, 0)

    @pl.when(i == 0)
    def _():
        zero_ref[...] = jnp.zeros(zero_ref.shape, F32)
        fills(0, nfill_ref[0], 0, lambda c: c.start())
        fills(nfill_ref[0], nfill_ref[1], 1, lambda c: c.start())
        fills(0, nfill_ref[0], 0, lambda c: c.wait())

    def group_by_expert(tok_ref):
        hn = tok_ref[...]
        slots = rt_ref[0:TOP_K, :].astype(jnp.int32)
        for c in range(n_slots // chunk):
            s = lax.broadcasted_iota(jnp.int32, (chunk, td), 0) + c * chunk
            hit = s == slots[0:1, :]
            for k in range(1, TOP_K):
                hit = hit | (s == slots[k:k + 1, :])
            rows = jnp.dot(jnp.where(hit, 1.0, 0.0).astype(BF16), hn, preferred_element_type=F32)
            _store_slabs(stage_ref.at[par, pl.ds(c * chunk * SUBLANES, chunk * SUBLANES)], rows, chunk)

    def run_copy(buf, src_slot, dst_row, runs=1):
        src = pl.multiple_of(src_slot * SUBLANES, RUN * SUBLANES)
        dst = pl.multiple_of(dst_row * SUBLANES, SUBLANES)
        return pltpu.make_async_copy(stage_ref.at[buf, pl.ds(src, runs * RUN * SUBLANES)],
                                     xs_ref.at[pl.ds(dst, runs * RUN * SUBLANES)], sem)

    def wait_runs(tile):
        def wait_many(q, carry):
            run_copy(0, 0, 0, ISSUE_UNROLL).wait()
            return carry

        def wait_one(j, carry):
            run_copy(0, 0, 0).wait()
            return carry

        n_many = tot_ref[tile] // ISSUE_UNROLL
        lax.fori_loop(0, n_many, wait_many, 0)
        lax.fori_loop(n_many * ISSUE_UNROLL, tot_ref[tile], wait_one, 0)

    @pl.when(i >= 2)
    def _():
        wait_runs(i - 2)

    @pl.when(i >= 1)
    def _():
        def issue(j):
            run_copy(1 - par, j * RUN, dst_ref[0, 0, j]).start()

        def issue_many(q, carry):
            for u in range(ISSUE_UNROLL):
                issue(q * ISSUE_UNROLL + u)
            return carry

        def issue_one(j, carry):
            issue(j)
            return carry

        n_full = tot_ref[i - 1] // ISSUE_UNROLL
        lax.fori_loop(0, n_full, issue_many, 0)
        lax.fori_loop(n_full * ISSUE_UNROLL, tot_ref[i - 1], issue_one, 0)

    @pl.when(i < n_first)
    def _():
        group_by_expert(hn_ref)

    @pl.when((i >= n_first) & (i < last))
    def _():
        group_by_expert(hn2_ref)

    @pl.when(i == last)
    def _():
        wait_runs(i - 1)
        fills(nfill_ref[0], nfill_ref[1], 1, lambda c: c.wait())


def _stage_slots(td):
    return td * TOP_K + N_EXPERTS * RUN


def _by_tile(field, td):
    n = field.shape[1]
    return field.reshape(TOP_K, n // td, td).transpose(1, 0, 2).reshape(n // td, 1, TOP_K * td)


def _dispatch(hn_a, hn_b, route_t, fill_blocks, n_fill, runs_per_tile, run_dst, *, td, nb, tme):
    na, d = hn_a.shape
    assert na % td == 0 and hn_b.shape[0] % td == 0 and td % RUN == 0
    n_first = na // td
    n_second = hn_b.shape[0] // td
    tiles = n_first + n_second
    n_slots = _stage_slots(td)
    chunk = _tile(n_slots, 768, SUBLANES)
    tok = lambda m: pl.BlockSpec((td, d), m)
    grid_spec = pltpu.PrefetchScalarGridSpec(
        num_scalar_prefetch=3,
        grid=(tiles + 1,),
        in_specs=[pl.BlockSpec((1, 1, n_slots // RUN), lambda i, *_: (jnp.maximum(i - 1, 0), 0, 0),
                               memory_space=pltpu.SMEM),
                  pl.BlockSpec((ROUTE_T_ROWS, td), lambda i, *_: (0, jnp.minimum(i, tiles - 1))),
                  tok(lambda i, *_: (jnp.minimum(i, n_first - 1), 0)),
                  tok(lambda i, *_: (jnp.clip(i - n_first, 0, n_second - 1), 0))],
        out_specs=pl.BlockSpec(memory_space=pl.ANY),
        scratch_shapes=[pltpu.VMEM((tme * SUBLANES, LANES), F32),
                        pltpu.VMEM((2, n_slots * SUBLANES, LANES), F32),
                        pltpu.SemaphoreType.DMA(()), pltpu.SemaphoreType.DMA((2,))],
    )
    return pl.pallas_call(
        functools.partial(_dispatch_kernel, td=td, tme=tme, n_first=n_first, chunk=chunk),
        grid_spec=grid_spec,
        out_shape=jax.ShapeDtypeStruct((nb * tme * SUBLANES, LANES), F32),
        compiler_params=_cparams(("arbitrary",)),
        name="dispatch",
    )(fill_blocks, n_fill, runs_per_tile, run_dst, route_t, hn_a, hn_b)


def _experts_kernel(be_ref, bsrc_ref, nv_ref, nx_ref, nu_ref, x_ref, w1_hbm, b1_ref, w2_hbm, b2_ref,
                    y_ref, w1f_ref, w2f_ref, w1b_ref, w2b_ref, par_ref, wsem, *, tme):
    i = pl.program_id(0)
    e = be_ref[i]
    e_prev = be_ref[jnp.maximum(i - 1, 0)]
    d_ff = w2f_ref.shape[1]
    half = tme // 2

    def fetch(expert, s):
        return (pltpu.make_async_copy(w1_hbm.at[expert], w1f_ref.at[s], wsem.at[0, s]),
                pltpu.make_async_copy(w2_hbm.at[expert], w2f_ref.at[s], wsem.at[1, s]))

    @pl.when(i == 0)
    def _():
        par_ref[0] = 0
        for c in fetch(e, 0):
            c.start()

    @pl.when((i == 0) | (e != e_prev))
    def _():
        s = par_ref[0]
        for c in fetch(e, s):
            c.wait()
        w1b_ref[...] = w1f_ref[s].astype(BF16)
        w2b_ref[...] = w2f_ref[s].astype(BF16)
        nxt = nx_ref[i]

        @pl.when(nxt >= 0)
        def _():
            for c in fetch(nxt, 1 - s):
                c.start()

        par_ref[0] = 1 - s

    def ffn(x):
        h = jnp.dot(x, w1b_ref[...], preferred_element_type=F32) + b1_ref[0]
        x_glu = jnp.minimum(h[:, :d_ff], SWIGLU_LIMIT)
        x_lin = jnp.clip(h[:, d_ff:], -SWIGLU_LIMIT, SWIGLU_LIMIT)
        act = x_glu * _sigmoid(SWIGLU_ALPHA * x_glu) * (x_lin + 1.0)
        return jnp.dot(act.astype(BF16), w2b_ref[...], preferred_element_type=F32) + b2_ref[0]

    used = i < nu_ref[0]
    nv = nv_ref[i]

    @pl.when(used & (nv > half))
    def _():
        _store_slabs(y_ref, ffn(_load_slabs(x_ref, tme, BF16)), tme)

    @pl.when(used & (nv <= half))
    def _():
        rows = half * SUBLANES
        _store_slabs(y_ref.at[pl.ds(0, rows)], ffn(_load_slabs(x_ref.at[pl.ds(0, rows)], half, BF16)), half)
        y_ref[pl.ds(rows, rows), :] = jnp.zeros((rows, LANES), F32)

    @pl.when(i == nu_ref[0])
    def _():
        y_ref[...] = jnp.zeros(y_ref.shape, F32)


def _experts(xs, w1, b1, w2, b2, blk_exp, blk_src, blk_nvalid, blk_next, n_used, *, tme):
    n_exp, d, h2 = w1.shape
    d_ff = w2.shape[1]
    nb = xs.shape[0] // (tme * SUBLANES)
    slab = lambda m: pl.BlockSpec((tme * SUBLANES, LANES), m)
    grid_spec = pltpu.PrefetchScalarGridSpec(
        num_scalar_prefetch=5,
        grid=(nb,),
        in_specs=[slab(lambda i, be, bs, nv, nx, nu: (bs[i], 0)),
                  pl.BlockSpec(memory_space=pl.ANY),
                  pl.BlockSpec((1, 1, h2), lambda i, be, bs, nv, nx, nu: (be[i], 0, 0)),
                  pl.BlockSpec(memory_space=pl.ANY),
                  pl.BlockSpec((1, 1, d), lambda i, be, bs, nv, nx, nu: (be[i], 0, 0))],
        out_specs=slab(lambda i, be, bs, nv, nx, nu: (jnp.minimum(i, nu[0]), 0)),
        scratch_shapes=[pltpu.VMEM((2, d, h2), F32), pltpu.VMEM((2, d_ff, d), F32),
                        pltpu.VMEM((d, h2), BF16), pltpu.VMEM((d_ff, d), BF16),
                        pltpu.SMEM((1,), jnp.int32), pltpu.SemaphoreType.DMA((2, 2))],
    )
    return pl.pallas_call(
        functools.partial(_experts_kernel, tme=tme),
        grid_spec=grid_spec,
        out_shape=jax.ShapeDtypeStruct(xs.shape, F32),
        input_output_aliases={5: 0},
        compiler_params=_cparams(("arbitrary",)),
        name="experts",
    )(blk_exp, blk_src, blk_nvalid, blk_next, n_used, xs, w1, b1.reshape(n_exp, 1, h2), w2,
      b2.reshape(n_exp, 1, d))


def _combine_kernel(tot_ref, rcur_ref, rnext_ref, slot_ref, gate_ref, x1_ref, g_ref, ys_ref, o_ref,
                    stage_ref, moe_ref, sem, *, tr, first, sub):
    i = pl.program_id(0)
    par = i % 2

    def run_copy(buf, j, src_row, runs=1):
        src = pl.multiple_of(src_row * SUBLANES, SUBLANES)
        dst = pl.multiple_of(j * (RUN * SUBLANES), RUN * SUBLANES)
        return pltpu.make_async_copy(ys_ref.at[pl.ds(src, runs * RUN * SUBLANES)],
                                     stage_ref.at[buf, pl.ds(dst, runs * RUN * SUBLANES)], sem.at[buf])

    def gather(rref, buf, n):
        def issue_many(q, carry):
            for u in range(ISSUE_UNROLL):
                j = q * ISSUE_UNROLL + u
                run_copy(buf, j, rref[0, 0, j]).start()
            return carry

        def issue_one(j, carry):
            run_copy(buf, j, rref[0, 0, j]).start()
            return carry

        lax.fori_loop(0, n // ISSUE_UNROLL, issue_many, 0)
        lax.fori_loop((n // ISSUE_UNROLL) * ISSUE_UNROLL, n, issue_one, 0)

    @pl.when(i == 0)
    def _():
        gather(rcur_ref, 0, tot_ref[first])

    @pl.when(i + 1 < pl.num_programs(0))
    def _():
        gather(rnext_ref, 1 - par, tot_ref[first + i + 1])

    def wait_many(q, carry):
        run_copy(par, 0, 0, ISSUE_UNROLL).wait()
        return carry

    def wait_one(j, carry):
        run_copy(par, 0, 0).wait()
        return carry

    n_runs = tot_ref[first + i]
    lax.fori_loop(0, n_runs // ISSUE_UNROLL, wait_many, 0)
    lax.fori_loop((n_runs // ISSUE_UNROLL) * ISSUE_UNROLL, n_runs, wait_one, 0)

    def token(r, carry):
        acc = None
        for k in range(TOP_K):
            s = pl.multiple_of(slot_ref[0, 0, k * tr + r], SUBLANES)
            term = gate_ref[0, 0, k * tr + r] * stage_ref[par, pl.ds(s, SUBLANES), :]
            acc = term if acc is None else acc + term
        moe_ref[pl.ds(pl.multiple_of(r * SUBLANES, SUBLANES), SUBLANES), :] = acc
        return carry

    lax.fori_loop(0, tr, token, 0, unroll=ISSUE_UNROLL)
    for c in range(tr // sub):
        rows = slice(c * sub, (c + 1) * sub)
        y = x1_ref[rows, :] + _load_slabs(moe_ref.at[pl.ds(c * sub * SUBLANES, sub * SUBLANES)], sub, F32)
        ms = jnp.mean(y * y, axis=-1, keepdims=True)
        o_ref[rows, :] = y * lax.rsqrt(ms + RMS_EPS) * g_ref[...]


def _combine(x1, slots, gates, runs_per_tile, run_src, ys, g, *, tr, first):
    n, d = x1.shape
    steps = n // tr
    n_slots = _stage_slots(tr)
    smem = lambda a, m: pl.BlockSpec((1, 1, a.shape[2]), m, memory_space=pltpu.SMEM)
    cur = lambda i, tot: (i, 0, 0)
    nxt = lambda i, tot: (jnp.minimum(i + 1, steps - 1), 0, 0)
    grid_spec = pltpu.PrefetchScalarGridSpec(
        num_scalar_prefetch=1,
        grid=(steps,),
        in_specs=[smem(run_src, cur), smem(run_src, nxt), smem(slots, cur), smem(gates, cur),
                  pl.BlockSpec((tr, d), lambda i, tot: (i, 0)),
                  pl.BlockSpec((1, d), lambda i, tot: (0, 0)),
                  pl.BlockSpec(memory_space=pl.ANY)],
        out_specs=pl.BlockSpec((tr, d), lambda i, tot: (i, 0)),
        scratch_shapes=[pltpu.VMEM((2, n_slots * SUBLANES, LANES), F32), pltpu.VMEM((tr * SUBLANES, LANES), F32),
                        pltpu.SemaphoreType.DMA((2,))],
    )
    return pl.pallas_call(
        functools.partial(_combine_kernel, tr=tr, first=first, sub=_tile(tr, 128, 8)),
        grid_spec=grid_spec,
        out_shape=jax.ShapeDtypeStruct((n, d), F32),
        compiler_params=_cparams(("arbitrary",)),
        name="combine",
    )(runs_per_tile, run_src, run_src, slots, gates, x1, g, ys)


def _routing_tables(counts_f, *, tme, nb):
    counts = counts_f[:, 0].astype(jnp.int32)
    nblk = jnp.where(counts > 0, (counts + RUN - 1 + tme - 1) // tme, 0)
    blk_end = jnp.cumsum(nblk)
    blk_start = blk_end - nblk
    n_used = blk_end[-1]
    b = jnp.arange(nb, dtype=jnp.int32)
    used = b < n_used
    blk_exp = jnp.minimum(jnp.sum((b[:, None] >= blk_end[None, :]).astype(jnp.int32), axis=1), N_EXPERTS - 1)
    last_exp = jnp.max(jnp.where(nblk > 0, jnp.arange(N_EXPERTS, dtype=jnp.int32), 0))
    blk_exp = jnp.where(used, blk_exp, last_exp).astype(jnp.int32)
    blk_src = jnp.minimum(b, n_used - 1).astype(jnp.int32)
    experts = jnp.arange(N_EXPERTS, dtype=jnp.int32)
    mine = (b[:, None] >= blk_start[None, :]) & (b[:, None] < blk_end[None, :])
    nvalid = jnp.sum(jnp.where(mine, counts[None, :] - (b[:, None] - blk_start[None, :]) * tme, 0), axis=1)
    nvalid = jnp.clip(nvalid, 0, tme).astype(jnp.int32)
    later = (experts[None, :] > experts[:, None]) & (nblk[None, :] > 0)
    nxt_e = jnp.min(jnp.where(later, experts[None, :], N_EXPERTS), axis=1)
    nxt_e = jnp.where(nxt_e == N_EXPERTS, -1, nxt_e)
    blk_next = jnp.sum(jnp.where(blk_exp[:, None] == experts[None, :], nxt_e[None, :], 0), axis=1).astype(jnp.int32)
    partial = used & (nvalid < tme)
    fill_order = jnp.where(partial, 0, jnp.where(used, 2, 1))
    fill_blocks = jnp.argsort(fill_order, stable=True).astype(jnp.int32)
    n_fill = jnp.stack([jnp.sum(partial), jnp.sum(partial | ~used)]).astype(jnp.int32)
    return (blk_start, blk_exp, blk_src, nvalid, blk_next, n_used.reshape(1).astype(jnp.int32), fill_blocks,
            n_fill)


def _run_tables(tile_carry_f, counts_f, blk_start, *, td, tme):
    before = tile_carry_f[:, :, 0].astype(jnp.int32)
    counts = counts_f[:, 0].astype(jnp.int32)
    in_tile = jnp.concatenate([before[1:], counts[None, :]], axis=0) - before
    n_runs = (in_tile + RUN - 1) // RUN
    ends = jnp.cumsum(n_runs, axis=1)
    j = jnp.arange(_stage_slots(td) // RUN, dtype=jnp.int32)
    owner = jnp.sum((j[None, :, None] >= ends[:, None, :]).astype(jnp.int32), axis=2)
    base = blk_start[None, :] * tme + before - (ends - n_runs) * RUN
    mine = owner[:, :, None] == jnp.arange(N_EXPERTS, dtype=jnp.int32)[None, None, :]
    dst = jnp.sum(jnp.where(mine, base[:, None, :], 0), axis=2) + j[None, :] * RUN
    return ends[:, -1].astype(jnp.int32), dst[:, None, :].astype(jnp.int32)


def kernel(x_prompt, x_sample, cache_k, cache_v, state_conv, attn_norm_g, w_in, attn_sinks, conv_w, conv_b,
           conv_ln_g, conv_ln_b, w_out, ffn_norm_g, router_w, router_b, w1, b1, w2, b2, final_norm_g):
    depth = w_in.shape[0]
    assert depth == 1, "single-layer step"
    bp, sp, d = x_prompt.shape
    bs, ss, _ = x_sample.shape
    cw = conv_w.shape[2]
    np_, ns = bp * sp, bs * ss
    n_tok = np_ + ns
    assert sp % WINDOW == 0

    xp2 = x_prompt.reshape(np_, d)
    xs2 = x_sample.reshape(ns, d)
    w_in_bf = w_in[0].astype(BF16)
    w_out_bf = w_out[0].astype(BF16)
    g_attn = attn_norm_g[0].reshape(1, d)
    g_ffn = ffn_norm_g[0].reshape(1, d)
    sinks = attn_sinks[0]
    vec = lambda a: a.reshape(1, cw)

    tab_p = _rope_tables(jnp.arange(sp, dtype=jnp.int32))
    tms = _tile(ns, 512, max(ss, 16))
    tab_s = _rope_tables(PAST_LEN + (jnp.arange(tms, dtype=jnp.int32) % ss))
    qp, kp, vp, kfp, vfp, ap = _in_proj(xp2, g_attn, w_in_bf, tab_p, seq_period=sp, q_dtype=BF16, conv_width=cw)
    qs, _, _, kfs, vfs, as_ = _in_proj(xs2, g_attn, w_in_bf, tab_s, seq_period=None, q_dtype=F32, conv_width=cw)

    att_p = _attn_prompt(qp, kp, vp, sinks, batch=bp, seq=sp)
    ck = cache_k[0].reshape(bs, WINDOW, KV_WIDTH)
    cv_ = cache_v[0].reshape(bs, WINDOW, KV_WIDTH)
    att_s, nk_s, nv_s = _attn_sample(qs, kfs, vfs, ck, cv_, sinks, batch=bs, ts=ss)

    cv_p = _conv_prompt(ap, conv_w[0], vec(conv_b[0]), vec(conv_ln_g[0]), vec(conv_ln_b[0]), batch=bp, seq=sp)
    cv_s = _conv_sample(as_, state_conv[0], conv_w[0], vec(conv_b[0]), vec(conv_ln_g[0]), vec(conv_ln_b[0]),
                        batch=bs, ts=ss)

    n_exp = router_w.shape[2]
    assert n_exp == N_EXPERTS
    rwt_bf = router_w[0].T.astype(BF16)
    rbt = jnp.broadcast_to(router_b[0][:, None], (n_exp, LANES))
    zero_carry = jnp.zeros((n_exp, LANES), F32)
    tr = _tile(ns, 512, 16)
    assert np_ % tr == 0
    x1p, hnp, rt_p, tcar_p, cnt_p = _outproj_router(xp2, att_p, cv_p, w_out_bf, g_ffn, rwt_bf, rbt, zero_carry, tm=tr)
    x1s, hns, rt_s, tcar_s, cnt = _outproj_router(xs2, att_s, cv_s, w_out_bf, g_ffn, rwt_bf, rbt, cnt_p, tm=tr)

    tme = EXPERT_BLOCK_ROWS
    nb = -(-(n_tok * TOP_K + N_EXPERTS * (tme - 1 + RUN - 1)) // tme)
    blk_start, blk_exp, blk_src, blk_nvalid, blk_next, n_used, fill_blocks, n_fill = _routing_tables(
        cnt, tme=tme, nb=nb)
    runs_per_tile, run_dst = _run_tables(jnp.concatenate([tcar_p, tcar_s], axis=0), cnt, blk_start, td=tr, tme=tme)
    xs_sorted = _dispatch(hnp, hns, jnp.concatenate([rt_p, rt_s], axis=1), fill_blocks, n_fill, runs_per_tile,
                          run_dst, td=tr, nb=nb, tme=tme)
    ys = _experts(xs_sorted, w1[0], b1[0], w2[0], b2[0], blk_exp, blk_src, blk_nvalid, blk_next, n_used, tme=tme)
    g_fin = final_norm_g.reshape(1, d)
    tiles_p = np_ // tr
    copy_slots = lambda rt: _by_tile(rt[0:TOP_K].astype(jnp.int32) * SUBLANES, tr)
    copy_gates = lambda rt: _by_tile(rt[TOP_K:2 * TOP_K], tr)
    y_p = _combine(x1p, copy_slots(rt_p), copy_gates(rt_p), runs_per_tile, run_dst[:tiles_p], ys, g_fin,
                   tr=tr, first=0)
    y_s = _combine(x1s, copy_slots(rt_s), copy_gates(rt_s), runs_per_tile, run_dst[tiles_p:], ys, g_fin,
                   tr=tr, first=tiles_p)

    kv5 = lambda t, bb: t.reshape(bb, -1, KV_WIDTH)[:, -WINDOW:].reshape(bb, WINDOW, N_KV_HEADS, HEAD_DIM)
    new_k_p = kv5(kfp, bp)[None]
    new_v_p = kv5(vfp, bp)[None]
    ctx = CONV_K - 1
    new_c_p = ap.reshape(bp, sp, cw)[:, -ctx:][None]
    new_c_s = jnp.concatenate([state_conv[0], as_.reshape(bs, ss, cw)], axis=1)[:, -ctx:][None]
    return (y_p.reshape(bp, sp, d), y_s.reshape(bs, ss, d), new_k_p, new_v_p, new_c_p,
            kv5(nk_s, bs)[None], kv5(nv_s, bs)[None], new_c_s)
```

```python
import functools

import jax
import jax.numpy as jnp
from jax import lax
from jax.experimental import pallas as pl
from jax.experimental.pallas import tpu as pltpu

F32 = jnp.float32
BF16 = jnp.bfloat16

HEAD_DIM = 64
N_Q_HEADS = 8
N_KV_HEADS = 2
WINDOW = 128
ROPE_THETA = 500000.0
ROPE_DIM = 16
CONV_K = 31
N_EXPERTS = 32
TOP_K = 4
SWIGLU_LIMIT = 7.0
SWIGLU_ALPHA = 1.702
RMS_EPS = 1e-5
LN_EPS = 1e-5
PAST_LEN = 16384

LANES = 128
SUBLANES = 8
CONV_HALO = 32
VMEM_LIMIT = 56 * 1024 * 1024
EXPERT_BLOCK_ROWS = 512
RUN = 8
ROUTE_T_ROWS = 2 * TOP_K
ISSUE_UNROLL = 8

ATTN_WIDTH = N_Q_HEADS * HEAD_DIM
KV_WIDTH = N_KV_HEADS * HEAD_DIM


def _tile(n, pref, mult=8):
    t = min(pref, n)
    while t > 0 and (n % t or t % mult):
        t -= 1
    assert t > 0, (n, pref, mult)
    return t


def _cparams(sem):
    return pltpu.CompilerParams(dimension_semantics=sem, vmem_limit_bytes=VMEM_LIMIT)


def _sigmoid(x):
    return 1.0 / (1.0 + jnp.exp(-x))


def _rope_tables(pos):
    half = ROPE_DIM // 2
    inv_freq = jnp.power(jnp.float32(ROPE_THETA), -jnp.arange(half, dtype=F32) * 2.0 / ROPE_DIM)
    ang = pos.astype(F32)[:, None] * inv_freq[None, :]
    cos, sin = jnp.cos(ang), jnp.sin(ang)
    l64 = jnp.arange(LANES) % HEAD_DIM
    f = l64 % half
    cos_l, sin_l = cos[:, f], sin[:, f]
    c = jnp.where(l64 < ROPE_DIM, cos_l, 1.0)
    s1 = jnp.where(l64 < half, -sin_l, 0.0)
    s2 = jnp.where((l64 >= half) & (l64 < ROPE_DIM), sin_l, 0.0)
    return c.astype(F32), s1.astype(F32), s2.astype(F32)


def _inproj_kernel(x_ref, g_ref, w_ref, c_ref, s1_ref, s2_ref,
                   q_ref, k_ref, v_ref, kf_ref, vf_ref, a_ref, *, conv_width):
    x = x_ref[...]
    ms = jnp.mean(x * x, axis=-1, keepdims=True)
    h = (x * lax.rsqrt(ms + RMS_EPS) * g_ref[...]).astype(BF16)
    z = jnp.dot(h, w_ref[...], preferred_element_type=F32)
    c, s1, s2 = c_ref[...], s1_ref[...], s2_ref[...]
    half = ROPE_DIM // 2

    def rot(t):
        return t * c + pltpu.roll(t, LANES - half, 1) * s1 + pltpu.roll(t, half, 1) * s2

    scale = HEAD_DIM ** -0.5
    for j in range(ATTN_WIDTH // LANES):
        q_ref[:, j * LANES:(j + 1) * LANES] = (rot(z[:, j * LANES:(j + 1) * LANES]) * scale).astype(q_ref.dtype)
    k0 = ATTN_WIDTH
    kr = rot(z[:, k0:k0 + KV_WIDTH])
    k_ref[...] = kr.astype(BF16)
    kf_ref[...] = kr
    v0 = k0 + KV_WIDTH
    vv = z[:, v0:v0 + KV_WIDTH]
    v_ref[...] = vv.astype(BF16)
    vf_ref[...] = vv
    u0 = v0 + KV_WIDTH
    g0 = u0 + conv_width
    a_ref[...] = z[:, u0:g0] * _sigmoid(z[:, g0:g0 + conv_width])


def _in_proj(x2, g, w_bf, tables, *, seq_period, q_dtype, conv_width):
    n, d = x2.shape
    in_w = w_bf.shape[1]
    if seq_period is None:
        tm = tables[0].shape[0]
        tmap = lambda i: (0, 0)
    else:
        tm = _tile(seq_period, 1024, 16)
        per = seq_period // tm
        tmap = lambda i: (i % per, 0)
    assert n % tm == 0
    row = lambda w: pl.BlockSpec((tm, w), lambda i: (i, 0))
    tab = pl.BlockSpec((tm, LANES), tmap)
    return pl.pallas_call(
        functools.partial(_inproj_kernel, conv_width=conv_width),
        grid=(n // tm,),
        in_specs=[row(d), pl.BlockSpec((1, d), lambda i: (0, 0)),
                  pl.BlockSpec((d, in_w), lambda i: (0, 0)), tab, tab, tab],
        out_specs=[row(ATTN_WIDTH), row(KV_WIDTH), row(KV_WIDTH), row(KV_WIDTH), row(KV_WIDTH),
                   row(conv_width)],
        out_shape=[jax.ShapeDtypeStruct((n, ATTN_WIDTH), q_dtype),
                   jax.ShapeDtypeStruct((n, KV_WIDTH), BF16),
                   jax.ShapeDtypeStruct((n, KV_WIDTH), BF16),
                   jax.ShapeDtypeStruct((n, KV_WIDTH), F32),
                   jax.ShapeDtypeStruct((n, KV_WIDTH), F32),
                   jax.ShapeDtypeStruct((n, conv_width), F32)],
        compiler_params=_cparams(("parallel",)),
        name="in_proj",
    )(x2, g, w_bf, *tables)


def _dup_head(t, h):
    sw = pltpu.roll(t, HEAD_DIM, 1)
    low = lax.broadcasted_iota(jnp.int32, t.shape, 1) < HEAD_DIM
    return jnp.where(low, t, sw) if h == 0 else jnp.where(low, sw, t)


def _nt_dot(a, b):
    return lax.dot_general(a, b, (((1,), (1,)), ((), ())), preferred_element_type=F32)


def _attn_prompt_kernel(sink_ref, q_ref, kc_ref, kp_ref, vc_ref, vp_ref, o_ref, *, qb):
    j = pl.program_id(1)
    w = WINDOW
    k_all = jnp.concatenate([kp_ref[...], kc_ref[...]], axis=0).astype(F32)
    v_all = jnp.concatenate([vp_ref[...], vc_ref[...]], axis=0).astype(F32)
    r = lax.broadcasted_iota(jnp.int32, (w, 2 * w), 0)
    kk = lax.broadcasted_iota(jnp.int32, (w, 2 * w), 1)
    band = (kk > r) & (kk <= r + w)
    low = lax.broadcasted_iota(jnp.int32, (w, LANES), 1) < HEAD_DIM
    zero = jnp.zeros((w, LANES), BF16)
    group = N_Q_HEADS // N_KV_HEADS
    for h in range(N_KV_HEADS):
        kd_all = _dup_head(k_all, h).astype(BF16)
        vd_all = _dup_head(v_all, h).astype(BF16)
        for sub in range(qb):
            valid = band & ((kk >= w) | (j > 0)) if sub == 0 else band
            kd = kd_all[sub * w:(sub + 2) * w, :]
            vd = vd_all[sub * w:(sub + 2) * w, :]
            for jj in range(group // 2):
                col = (h * group // 2 + jj) * LANES
                qv = q_ref[sub * w:(sub + 1) * w, col:col + LANES]
                halves = []
                for half in range(2):
                    head = h * group + jj * 2 + half
                    qm = jnp.where(low if half == 0 else ~low, qv, zero)
                    s = jnp.where(valid, _nt_dot(qm, kd), -jnp.inf)
                    sink = sink_ref[head]
                    m = jnp.maximum(jnp.max(s, axis=-1, keepdims=True), sink)
                    p = jnp.exp(s - m)
                    den = jnp.sum(p, axis=-1, keepdims=True) + jnp.exp(sink - m)
                    o = jnp.dot(p.astype(BF16), vd, preferred_element_type=F32)
                    halves.append(o / den)
                o_ref[sub * w:(sub + 1) * w, col:col + LANES] = (
                    jnp.where(low, halves[0], halves[1]).astype(o_ref.dtype))


def _attn_prompt(q, k, v, sinks, *, batch, seq):
    nb = seq // WINDOW
    qb = next(c for c in (4, 2, 1) if nb % c == 0)
    steps = nb // qb
    cur = lambda b, j: (b * steps + j, 0)
    prev = lambda b, j: (b * nb + jnp.maximum(j * qb - 1, 0), 0)
    return pl.pallas_call(
        functools.partial(_attn_prompt_kernel, qb=qb),
        grid=(batch, steps),
        in_specs=[pl.BlockSpec(memory_space=pltpu.SMEM),
                  pl.BlockSpec((qb * WINDOW, ATTN_WIDTH), cur),
                  pl.BlockSpec((qb * WINDOW, KV_WIDTH), cur), pl.BlockSpec((WINDOW, KV_WIDTH), prev),
                  pl.BlockSpec((qb * WINDOW, KV_WIDTH), cur), pl.BlockSpec((WINDOW, KV_WIDTH), prev)],
        out_specs=pl.BlockSpec((qb * WINDOW, ATTN_WIDTH), cur),
        out_shape=jax.ShapeDtypeStruct((batch * seq, ATTN_WIDTH), BF16),
        compiler_params=_cparams(("parallel", "parallel")),
        name="attn_prompt",
    )(sinks, q, k, k, v, v)


def _attn_sample_kernel(sink_ref, q_ref, kn_ref, vn_ref, ck_ref, cv_ref, o_ref, nk_ref, nv_ref, *, gb, ts):
    w = WINDOW
    group = N_Q_HEADS // N_KV_HEADS
    rows = group * ts
    low = lax.broadcasted_iota(jnp.int32, (ts, LANES), 1) < HEAD_DIM
    pad = jnp.zeros((ts, LANES), F32)
    s_c, s_n, v_dup = [], [], []
    for b in range(gb):
        kc, vc = ck_ref[b], cv_ref[b]
        kn, vn = kn_ref[b * ts:(b + 1) * ts, :], vn_ref[b * ts:(b + 1) * ts, :]
        nk_ref[b, 0:w - ts, :] = kc[ts:, :]
        nk_ref[b, w - ts:, :] = kn
        nv_ref[b, 0:w - ts, :] = vc[ts:, :]
        nv_ref[b, w - ts:, :] = vn
        knp = jnp.concatenate([kn, pad], axis=0)
        vnp = jnp.concatenate([vn, pad], axis=0)
        qb = q_ref[b * ts:(b + 1) * ts, :]
        for h in range(N_KV_HEADS):
            parts = []
            for jj in range(group // 2):
                col = (h * group // 2 + jj) * LANES
                qv = qb[:, col:col + LANES]
                parts += [jnp.where(low, qv, 0.0), jnp.where(low, 0.0, qv)]
            lhs = jnp.concatenate(parts, axis=0).astype(BF16)
            s_c.append(_nt_dot(lhs, _dup_head(kc, h).astype(BF16)))
            s_n.append(_nt_dot(lhs, _dup_head(knp, h).astype(BF16)))
            v_dup.append((_dup_head(vc, h).astype(BF16), _dup_head(vnp, h).astype(BF16)))
    s_c = jnp.concatenate(s_c, axis=0)
    s_n = jnp.concatenate(s_n, axis=0)
    n_rows = s_c.shape[0]
    ridx = lax.broadcasted_iota(jnp.int32, (n_rows, 1), 0)
    t_row = ridx % ts
    head_row = (ridx // ts) % N_Q_HEADS
    sink = jnp.zeros((n_rows, 1), F32)
    for hd in range(N_Q_HEADS):
        sink = jnp.where(head_row == hd, sink_ref[hd], sink)
    c_idx = lax.broadcasted_iota(jnp.int32, (n_rows, w), 1)
    n_idx = lax.broadcasted_iota(jnp.int32, (n_rows, 2 * ts), 1)
    s_c = jnp.where(c_idx > t_row, s_c, -jnp.inf)
    s_n = jnp.where(n_idx <= t_row, s_n, -jnp.inf)
    m = jnp.maximum(jnp.maximum(jnp.max(s_c, axis=-1, keepdims=True), jnp.max(s_n, axis=-1, keepdims=True)), sink)
    p_c = jnp.exp(s_c - m)
    p_n = jnp.exp(s_n - m)
    den = jnp.sum(p_c, axis=-1, keepdims=True) + jnp.sum(p_n, axis=-1, keepdims=True) + jnp.exp(sink - m)
    p_c = p_c.astype(BF16)
    p_n = p_n.astype(BF16)
    outs = []
    for b in range(gb):
        cols = []
        for h in range(N_KV_HEADS):
            ci = b * N_KV_HEADS + h
            sl = slice(ci * rows, (ci + 1) * rows)
            vdc, vdn = v_dup[ci]
            o = (jnp.dot(p_c[sl], vdc, preferred_element_type=F32)
                 + jnp.dot(p_n[sl], vdn, preferred_element_type=F32)) / den[sl]
            for jj in range(group // 2):
                lo_part = o[(2 * jj) * ts:(2 * jj + 1) * ts, :]
                hi_part = o[(2 * jj + 1) * ts:(2 * jj + 2) * ts, :]
                cols.append(jnp.where(low, lo_part, hi_part))
        outs.append(jnp.concatenate(cols, axis=1))
    o_ref[...] = jnp.concatenate(outs, axis=0).astype(o_ref.dtype)


def _attn_sample(q, kf, vf, cache_k, cache_v, sinks, *, batch, ts):
    assert ts % 8 == 0 and ts <= WINDOW
    gb = _tile(batch, 8, 2)
    tok = lambda w: pl.BlockSpec((gb * ts, w), lambda i: (i, 0))
    cache = pl.BlockSpec((gb, WINDOW, KV_WIDTH), lambda i: (i, 0, 0))
    cshape = jax.ShapeDtypeStruct((batch, WINDOW, KV_WIDTH), F32)
    return pl.pallas_call(
        functools.partial(_attn_sample_kernel, gb=gb, ts=ts),
        grid=(batch // gb,),
        in_specs=[pl.BlockSpec(memory_space=pltpu.SMEM), tok(ATTN_WIDTH), tok(KV_WIDTH), tok(KV_WIDTH),
                  cache, cache],
        out_specs=[tok(ATTN_WIDTH), cache, cache],
        out_shape=[jax.ShapeDtypeStruct((batch * ts, ATTN_WIDTH), BF16), cshape, cshape],
        compiler_params=_cparams(("parallel",)),
        name="attn_sample",
    )(sinks, q, kf, vf, cache_k, cache_v)


def _ln_swish(acc, b, lg, lb):
    y = acc + b
    mu = jnp.mean(y, axis=-1, keepdims=True)
    yc = y - mu
    var = jnp.mean(yc * yc, axis=-1, keepdims=True)
    yn = yc * lax.rsqrt(var + LN_EPS) * lg + lb
    return yn * _sigmoid(yn)


def _conv_prompt_kernel(a_ref, ap_ref, w_ref, b_ref, lg_ref, lb_ref, o_ref, win_ref, *, tt, rc):
    j = pl.program_id(1)
    n = CONV_HALO + tt
    win = jnp.concatenate([jnp.where(j > 0, ap_ref[...], 0.0), a_ref[...]], axis=0)
    win_ref[0] = win
    for r in range(1, SUBLANES):
        win_ref[r] = pltpu.roll(win, n - r, 0)
    off = CONV_HALO - (CONV_K - 1)
    b, lg, lb = b_ref[...], lg_ref[...], lb_ref[...]
    for c in range(tt // rc):
        acc = jnp.zeros((rc, a_ref.shape[1]), F32)
        for k in range(CONV_K):
            s = off + k
            base = c * rc + (s // SUBLANES) * SUBLANES
            wk = jnp.concatenate([w_ref[k]] * (rc // SUBLANES), axis=0)
            acc = acc + wk * win_ref[s % SUBLANES, base:base + rc, :]
        o_ref[c * rc:(c + 1) * rc, :] = _ln_swish(acc, b, lg, lb).astype(o_ref.dtype)


def _conv_prompt(a, w, b, lg, lb, *, batch, seq):
    cw = a.shape[1]
    tt = _tile(seq, 512, CONV_HALO)
    rc = _tile(tt, 32, 16)
    nt = seq // tt
    per = tt // CONV_HALO
    cur = lambda bb, j: (bb * nt + j, 0)
    prev = lambda bb, j: (jnp.maximum((bb * nt + j) * per - 1, 0), 0)
    vec = pl.BlockSpec((1, cw), lambda bb, j: (0, 0))
    return pl.pallas_call(
        functools.partial(_conv_prompt_kernel, tt=tt, rc=rc),
        grid=(batch, nt),
        in_specs=[pl.BlockSpec((tt, cw), cur), pl.BlockSpec((CONV_HALO, cw), prev),
                  pl.BlockSpec((CONV_K, SUBLANES, cw), lambda bb, j: (0, 0, 0)), vec, vec, vec],
        out_specs=pl.BlockSpec((tt, cw), cur),
        out_shape=jax.ShapeDtypeStruct((batch * seq, cw), BF16),
        scratch_shapes=[pltpu.VMEM((SUBLANES, CONV_HALO + tt, cw), F32)],
        compiler_params=_cparams(("parallel", "parallel")),
        name="conv_prompt",
    )(a, a, jnp.broadcast_to(w[:, None, :], (CONV_K, SUBLANES, cw)), b, lg, lb)


def _conv_sample_kernel(a_ref, st_ref, w_ref, b_ref, lg_ref, lb_ref, o_ref, win_ref, *, gb, ts):
    ctx = CONV_K - 1
    b, lg, lb = b_ref[...], lg_ref[...], lb_ref[...]
    for bb in range(gb):
        win_ref[bb, 0:ctx, :] = st_ref[bb]
        win_ref[bb, ctx:ctx + ts, :] = a_ref[bb * ts:(bb + 1) * ts, :]
    outs = []
    for bb in range(gb):
        acc = jnp.zeros((ts, a_ref.shape[1]), F32)
        for k in range(CONV_K):
            wk = jnp.concatenate([w_ref[k]] * (ts // SUBLANES), axis=0)
            acc = acc + wk * win_ref[bb, k:k + ts, :]
        outs.append(_ln_swish(acc, b, lg, lb))
    o_ref[...] = jnp.concatenate(outs, axis=0).astype(o_ref.dtype)


def _conv_sample(a, state, w, b, lg, lb, *, batch, ts):
    cw = a.shape[1]
    ctx = CONV_K - 1
    gb = _tile(batch, 8, 2)
    vec = pl.BlockSpec((1, cw), lambda i: (0, 0))
    return pl.pallas_call(
        functools.partial(_conv_sample_kernel, gb=gb, ts=ts),
        grid=(batch // gb,),
        in_specs=[pl.BlockSpec((gb * ts, cw), lambda i: (i, 0)),
                  pl.BlockSpec((gb, ctx, cw), lambda i: (i, 0, 0)),
                  pl.BlockSpec((CONV_K, SUBLANES, cw), lambda i: (0, 0, 0)), vec, vec, vec],
        out_specs=pl.BlockSpec((gb * ts, cw), lambda i: (i, 0)),
        out_shape=jax.ShapeDtypeStruct((batch * ts, cw), BF16),
        scratch_shapes=[pltpu.VMEM((gb, ctx + ts + 2, cw), F32)],
        compiler_params=_cparams(("parallel",)),
        name="conv_sample",
    )(a, state, jnp.broadcast_to(w[:, None, :], (CONV_K, SUBLANES, cw)), b, lg, lb)


def _outproj_router_kernel(x_ref, att_ref, cv_ref, wo_ref, g_ref, rw_ref, rb_ref, cin_ref,
                           x1_ref, hn_ref, rt_ref, tcar_ref, cnt_ref, carry_ref):
    i = pl.program_id(0)

    @pl.when(i == 0)
    def _():
        carry_ref[...] = cin_ref[...]

    tm = x_ref.shape[0]
    aw = att_ref.shape[1]
    n_exp = rw_ref.shape[0]
    r = lax.broadcasted_iota(jnp.int32, (tm, tm), 0)
    c = lax.broadcasted_iota(jnp.int32, (tm, tm), 1)
    before = (r < c).astype(BF16)
    eid = lax.broadcasted_iota(jnp.int32, (n_exp, tm), 0)
    carry = carry_ref[...][:, 0:1]
    tcar_ref[0] = carry_ref[...]
    mix = (jnp.dot(att_ref[...], wo_ref[0:aw, :], preferred_element_type=F32)
           + jnp.dot(cv_ref[...], wo_ref[aw:, :], preferred_element_type=F32))
    x1 = x_ref[...] + mix
    x1_ref[...] = x1
    ms = jnp.mean(x1 * x1, axis=-1, keepdims=True)
    hn = (x1 * lax.rsqrt(ms + RMS_EPS) * g_ref[...]).astype(BF16)
    hn_ref[...] = hn
    logits = _nt_dot(rw_ref[...], hn) + rb_ref[...][:, 0:1]
    onehot = jnp.zeros((n_exp, tm), F32)
    vals, idxs = [], []
    for _ in range(TOP_K):
        m = jnp.max(logits, axis=0, keepdims=True)
        idx = jnp.min(jnp.where(logits == m, eid, n_exp), axis=0, keepdims=True)
        sel = eid == idx
        onehot = onehot + sel.astype(F32)
        logits = jnp.where(sel, -jnp.inf, logits)
        vals.append(m)
        idxs.append(idx)
    es = [jnp.exp(v - vals[0]) for v in vals]
    den = es[0] + es[1] + es[2] + es[3]
    local = jnp.dot(onehot.astype(BF16), before, preferred_element_type=F32)
    count = jnp.sum(onehot, axis=1, keepdims=True)
    padded = jnp.ceil(count * (1.0 / RUN)) * RUN
    er = lax.broadcasted_iota(jnp.int32, (n_exp, n_exp), 0)
    ec = lax.broadcasted_iota(jnp.int32, (n_exp, n_exp), 1)
    start = jnp.dot((ec < er).astype(BF16), jnp.broadcast_to(padded, (n_exp, LANES)).astype(BF16),
                    preferred_element_type=F32)[:, 0:1]
    slot_of = local + start
    slots = [jnp.sum(jnp.where(eid == idxs[k], slot_of, 0.0), axis=0, keepdims=True) for k in range(TOP_K)]
    rt_ref[...] = jnp.concatenate(slots + [e / den for e in es], axis=0)
    carry = carry + count
    carry_ref[...] = jnp.broadcast_to(carry, carry_ref.shape)
    cnt_ref[...] = jnp.broadcast_to(carry, cnt_ref.shape)


def _store_slabs(ref, val, rows):
    for j in range(val.shape[1] // LANES):
        ref[pl.ds(j, rows, stride=SUBLANES), :] = val[:, j * LANES:(j + 1) * LANES]


def _load_slabs(ref, rows, dtype):
    return jnp.concatenate([ref[pl.ds(j, rows, stride=SUBLANES), :].astype(dtype) for j in range(SUBLANES)],
                           axis=1)


def _outproj_router(x2, att, cv, wo_bf, g, rwt_bf, rbt, carry_in, *, tm):
    n, d = x2.shape
    assert d == SUBLANES * LANES and n % tm == 0
    n_exp = rwt_bf.shape[0]
    row = lambda w: pl.BlockSpec((tm, w), lambda i: (i, 0))
    full = lambda a: pl.BlockSpec(a.shape, lambda i: (0,) * a.ndim)
    return pl.pallas_call(
        _outproj_router_kernel,
        grid=(n // tm,),
        in_specs=[row(d), row(att.shape[1]), row(cv.shape[1]), full(wo_bf), full(g), full(rwt_bf), full(rbt),
                  full(carry_in)],
        out_specs=[row(d), row(d),
                   pl.BlockSpec((ROUTE_T_ROWS, tm), lambda i: (0, i)),
                   pl.BlockSpec((1, n_exp, LANES), lambda i: (i, 0, 0)),
                   pl.BlockSpec((n_exp, LANES), lambda i: (0, 0))],
        out_shape=[jax.ShapeDtypeStruct((n, d), F32), jax.ShapeDtypeStruct((n, d), BF16),
                   jax.ShapeDtypeStruct((ROUTE_T_ROWS, n), F32),
                   jax.ShapeDtypeStruct((n // tm, n_exp, LANES), F32),
                   jax.ShapeDtypeStruct((n_exp, LANES), F32)],
        scratch_shapes=[pltpu.VMEM((n_exp, LANES), F32)],
        compiler_params=_cparams(("arbitrary",)),
        name="outproj_router",
    )(x2, att, cv, wo_bf, g, rwt_bf, rbt, carry_in)


def _dispatch_kernel(fill_ref, nfill_ref, tot_ref, dst_ref, rt_ref, hn_ref, hn2_ref,
                     xs_ref, zero_ref, stage_ref, sem, zsem, *, td, tme, n_first, chunk):
    i = pl.program_id(0)
    last = pl.num_programs(0) - 1
    par = i % 2
    n_slots = stage_ref.shape[1] // SUBLANES

    def fill(f, s):
        row = pl.multiple_of(fill_ref[f] * (tme * SUBLANES), tme * SUBLANES)
        return pltpu.make_async_copy(zero_ref, xs_ref.at[pl.ds(row, tme * SUBLANES)], zsem.at[s])

    def fills(lo, hi, s, act):
        def body(f, carry):
            act(fill(f, s))
            return carry

        lax.fori_loop(lo, hi, body, 0)

    @pl.when(i == 0)
    def _():
        zero_ref[...] = jnp.zeros(zero_ref.shape, F32)
        fills(0, nfill_ref[0], 0, lambda c: c.start())
        fills(nfill_ref[0], nfill_ref[1], 1, lambda c: c.start())
        fills(0, nfill_ref[0], 0, lambda c: c.wait())

    def group_by_expert(tok_ref):
        hn = tok_ref[...]
        slots = rt_ref[0:TOP_K, :].astype(jnp.int32)
        for c in range(n_slots // chunk):
            s = lax.broadcasted_iota(jnp.int32, (chunk, td), 0) + c * chunk
            hit = s == slots[0:1, :]
            for k in range(1, TOP_K):
                hit = hit | (s == slots[k:k + 1, :])
            rows = jnp.dot(jnp.where(hit, 1.0, 0.0).astype(BF16), hn, preferred_element_type=F32)
            _store_slabs(stage_ref.at[par, pl.ds(c * chunk * SUBLANES, chunk * SUBLANES)], rows, chunk)

    def run_copy(buf, src_slot, dst_row, runs=1):
        src = pl.multiple_of(src_slot * SUBLANES, RUN * SUBLANES)
        dst = pl.multiple_of(dst_row * SUBLANES, SUBLANES)
        return pltpu.make_async_copy(stage_ref.at[buf, pl.ds(src, runs * RUN * SUBLANES)],
                                     xs_ref.at[pl.ds(dst, runs * RUN * SUBLANES)], sem)

    def wait_runs(tile):
        def wait_many(q, carry):
            run_copy(0, 0, 0, ISSUE_UNROLL).wait()
            return carry

        def wait_one(j, carry):
            run_copy(0, 0, 0).wait()
            return carry

        n_many = tot_ref[tile] // ISSUE_UNROLL
        lax.fori_loop(0, n_many, wait_many, 0)
        lax.fori_loop(n_many * ISSUE_UNROLL, tot_ref[tile], wait_one, 0)

    @pl.when(i >= 2)
    def _():
        wait_runs(i - 2)

    @pl.when(i >= 1)
    def _():
        def issue(j):
            run_copy(1 - par, j * RUN, dst_ref[0, 0, j]).start()

        def issue_many(q, carry):
            for u in range(ISSUE_UNROLL):
                issue(q * ISSUE_UNROLL + u)
            return carry

        def issue_one(j, carry):
            issue(j)
            return carry

        n_full = tot_ref[i - 1] // ISSUE_UNROLL
        lax.fori_loop(0, n_full, issue_many, 0)
        lax.fori_loop(n_full * ISSUE_UNROLL, tot_ref[i - 1], issue_one, 0)

    @pl.when(i < n_first)
    def _():
        group_by_expert(hn_ref)

    @pl.when((i >= n_first) & (i < last))
    def _():
        group_by_expert(hn2_ref)

    @pl.when(i == last)
    def _():
        wait_runs(i - 1)
        fills(nfill_ref[0], nfill_ref[1], 1, lambda c: c.wait())


def _stage_slots(td):
    return td * TOP_K + N_EXPERTS * RUN


def _by_tile(field, td):
    n = field.shape[1]
    return field.T.reshape(n // td, 1, td * TOP_K)


def _dispatch(hn_a, hn_b, route_t, fill_blocks, n_fill, runs_per_tile, run_dst, *, td, nb, tme):
    na, d = hn_a.shape
    assert na % td == 0 and hn_b.shape[0] % td == 0 and td % RUN == 0
    n_first = na // td
    n_second = hn_b.shape[0] // td
    tiles = n_first + n_second
    n_slots = _stage_slots(td)
    chunk = _tile(n_slots, 768, SUBLANES)
    tok = lambda m: pl.BlockSpec((td, d), m)
    grid_spec = pltpu.PrefetchScalarGridSpec(
        num_scalar_prefetch=3,
        grid=(tiles + 1,),
        in_specs=[pl.BlockSpec((1, 1, n_slots // RUN), lambda i, *_: (jnp.maximum(i - 1, 0), 0, 0),
                               memory_space=pltpu.SMEM),
                  pl.BlockSpec((ROUTE_T_ROWS, td), lambda i, *_: (0, jnp.minimum(i, tiles - 1))),
                  tok(lambda i, *_: (jnp.minimum(i, n_first - 1), 0)),
                  tok(lambda i, *_: (jnp.clip(i - n_first, 0, n_second - 1), 0))],
        out_specs=pl.BlockSpec(memory_space=pl.ANY),
        scratch_shapes=[pltpu.VMEM((tme * SUBLANES, LANES), F32),
                        pltpu.VMEM((2, n_slots * SUBLANES, LANES), F32),
                        pltpu.SemaphoreType.DMA(()), pltpu.SemaphoreType.DMA((2,))],
    )
    return pl.pallas_call(
        functools.partial(_dispatch_kernel, td=td, tme=tme, n_first=n_first, chunk=chunk),
        grid_spec=grid_spec,
        out_shape=jax.ShapeDtypeStruct((nb * tme * SUBLANES, LANES), F32),
        compiler_params=_cparams(("arbitrary",)),
        name="dispatch",
    )(fill_blocks, n_fill, runs_per_tile, run_dst, route_t, hn_a, hn_b)


def _experts_kernel(be_ref, bsrc_ref, nv_ref, nx_ref, nu_ref, x_ref, w1_hbm, b1_ref, w2_hbm, b2_ref,
                    y_ref, w1f_ref, w2f_ref, w1b_ref, w2b_ref, par_ref, wsem, *, tme):
    i = pl.program_id(0)
    e = be_ref[i]
    e_prev = be_ref[jnp.maximum(i - 1, 0)]
    d_ff = w2f_ref.shape[1]
    half = tme // 2

    def fetch(expert, s):
        return (pltpu.make_async_copy(w1_hbm.at[expert], w1f_ref.at[s], wsem.at[0, s]),
                pltpu.make_async_copy(w2_hbm.at[expert], w2f_ref.at[s], wsem.at[1, s]))

    @pl.when(i == 0)
    def _():
        par_ref[0] = 0
        for c in fetch(e, 0):
            c.start()

    @pl.when((i == 0) | (e != e_prev))
    def _():
        s = par_ref[0]
        for c in fetch(e, s):
            c.wait()
        w1b_ref[...] = w1f_ref[s].astype(BF16)
        w2b_ref[...] = w2f_ref[s].astype(BF16)
        nxt = nx_ref[i]

        @pl.when(nxt >= 0)
        def _():
            for c in fetch(nxt, 1 - s):
                c.start()

        par_ref[0] = 1 - s

    def ffn(x):
        h = jnp.dot(x, w1b_ref[...], preferred_element_type=F32) + b1_ref[0]
        x_glu = jnp.minimum(h[:, :d_ff], SWIGLU_LIMIT)
        x_lin = jnp.clip(h[:, d_ff:], -SWIGLU_LIMIT, SWIGLU_LIMIT)
        act = x_glu * _sigmoid(SWIGLU_ALPHA * x_glu) * (x_lin + 1.0)
        return jnp.dot(act.astype(BF16), w2b_ref[...], preferred_element_type=F32) + b2_ref[0]

    used = i < nu_ref[0]
    nv = nv_ref[i]

    @pl.when(used & (nv > half))
    def _():
        _store_slabs(y_ref, ffn(_load_slabs(x_ref, tme, BF16)), tme)

    @pl.when(used & (nv <= half))
    def _():
        rows = half * SUBLANES
        _store_slabs(y_ref.at[pl.ds(0, rows)], ffn(_load_slabs(x_ref.at[pl.ds(0, rows)], half, BF16)), half)
        y_ref[pl.ds(rows, rows), :] = jnp.zeros((rows, LANES), F32)

    @pl.when(i == nu_ref[0])
    def _():
        y_ref[...] = jnp.zeros(y_ref.shape, F32)


def _experts(xs, w1, b1, w2, b2, blk_exp, blk_src, blk_nvalid, blk_next, n_used, *, tme):
    n_exp, d, h2 = w1.shape
    d_ff = w2.shape[1]
    nb = xs.shape[0] // (tme * SUBLANES)
    slab = lambda m: pl.BlockSpec((tme * SUBLANES, LANES), m)
    grid_spec = pltpu.PrefetchScalarGridSpec(
        num_scalar_prefetch=5,
        grid=(nb,),
        in_specs=[slab(lambda i, be, bs, nv, nx, nu: (bs[i], 0)),
                  pl.BlockSpec(memory_space=pl.ANY),
                  pl.BlockSpec((1, 1, h2), lambda i, be, bs, nv, nx, nu: (be[i], 0, 0)),
                  pl.BlockSpec(memory_space=pl.ANY),
                  pl.BlockSpec((1, 1, d), lambda i, be, bs, nv, nx, nu: (be[i], 0, 0))],
        out_specs=slab(lambda i, be, bs, nv, nx, nu: (jnp.minimum(i, nu[0]), 0)),
        scratch_shapes=[pltpu.VMEM((2, d, h2), F32), pltpu.VMEM((2, d_ff, d), F32),
                        pltpu.VMEM((d, h2), BF16), pltpu.VMEM((d_ff, d), BF16),
                        pltpu.SMEM((1,), jnp.int32), pltpu.SemaphoreType.DMA((2, 2))],
    )
    return pl.pallas_call(
        functools.partial(_experts_kernel, tme=tme),
        grid_spec=grid_spec,
        out_shape=jax.ShapeDtypeStruct(xs.shape, F32),
        input_output_aliases={5: 0},
        compiler_params=_cparams(("arbitrary",)),
        name="experts",
    )(blk_exp, blk_src, blk_nvalid, blk_next, n_used, xs, w1, b1.reshape(n_exp, 1, h2), w2,
      b2.reshape(n_exp, 1, d))


def _combine_kernel(tot_ref, rcur_ref, rnext_ref, slot_ref, gate_ref, x1_ref, g_ref, ys_ref, o_ref,
                    stage_ref, moe_ref, sem, *, tr, first, sub):
    i = pl.program_id(0)
    par = i % 2

    def run_copy(buf, j, src_row, runs=1):
        src = pl.multiple_of(src_row * SUBLANES, SUBLANES)
        dst = pl.multiple_of(j * (RUN * SUBLANES), RUN * SUBLANES)
        return pltpu.make_async_copy(ys_ref.at[pl.ds(src, runs * RUN * SUBLANES)],
                                     stage_ref.at[buf, pl.ds(dst, runs * RUN * SUBLANES)], sem.at[buf])

    def gather(rref, buf, n):
        def issue_many(q, carry):
            for u in range(ISSUE_UNROLL):
                j = q * ISSUE_UNROLL + u
                run_copy(buf, j, rref[0, 0, j]).start()
            return carry

        def issue_one(j, carry):
            run_copy(buf, j, rref[0, 0, j]).start()
            return carry

        lax.fori_loop(0, n // ISSUE_UNROLL, issue_many, 0)
        lax.fori_loop((n // ISSUE_UNROLL) * ISSUE_UNROLL, n, issue_one, 0)

    @pl.when(i == 0)
    def _():
        gather(rcur_ref, 0, tot_ref[first])

    @pl.when(i + 1 < pl.num_programs(0))
    def _():
        gather(rnext_ref, 1 - par, tot_ref[first + i + 1])

    def wait_many(q, carry):
        run_copy(par, 0, 0, ISSUE_UNROLL).wait()
        return carry

    def wait_one(j, carry):
        run_copy(par, 0, 0).wait()
        return carry

    n_runs = tot_ref[first + i]
    lax.fori_loop(0, n_runs // ISSUE_UNROLL, wait_many, 0)
    lax.fori_loop((n_runs // ISSUE_UNROLL) * ISSUE_UNROLL, n_runs, wait_one, 0)

    def token(buf, r, carry):
        acc = None
        for k in range(TOP_K):
            s = pl.multiple_of(slot_ref[0, 0, r * TOP_K + k], SUBLANES)
            term = gate_ref[0, 0, r * TOP_K + k] * stage_ref[buf, pl.ds(s, SUBLANES), :]
            acc = term if acc is None else acc + term
        moe_ref[pl.ds(pl.multiple_of(r * SUBLANES, SUBLANES), SUBLANES), :] = acc
        return carry

    for buf in range(2):
        @pl.when(par == buf)
        def _():
            lax.fori_loop(0, tr, functools.partial(token, buf), 0, unroll=ISSUE_UNROLL)

    for c in range(tr // sub):
        rows = slice(c * sub, (c + 1) * sub)
        y = x1_ref[rows, :] + _load_slabs(moe_ref.at[pl.ds(c * sub * SUBLANES, sub * SUBLANES)], sub, F32)
        ms = jnp.mean(y * y, axis=-1, keepdims=True)
        o_ref[rows, :] = y * lax.rsqrt(ms + RMS_EPS) * g_ref[...]


def _combine(x1, slots, gates, runs_per_tile, run_src, ys, g, *, tr, first):
    n, d = x1.shape
    steps = n // tr
    n_slots = _stage_slots(tr)
    smem = lambda a, m: pl.BlockSpec((1, 1, a.shape[2]), m, memory_space=pltpu.SMEM)
    cur = lambda i, tot: (i, 0, 0)
    nxt = lambda i, tot: (jnp.minimum(i + 1, steps - 1), 0, 0)
    grid_spec = pltpu.PrefetchScalarGridSpec(
        num_scalar_prefetch=1,
        grid=(steps,),
        in_specs=[smem(run_src, cur), smem(run_src, nxt), smem(slots, cur), smem(gates, cur),
                  pl.BlockSpec((tr, d), lambda i, tot: (i, 0)),
                  pl.BlockSpec((1, d), lambda i, tot: (0, 0)),
                  pl.BlockSpec(memory_space=pl.ANY)],
        out_specs=pl.BlockSpec((tr, d), lambda i, tot: (i, 0)),
        scratch_shapes=[pltpu.VMEM((2, n_slots * SUBLANES, LANES), F32), pltpu.VMEM((tr * SUBLANES, LANES), F32),
                        pltpu.SemaphoreType.DMA((2,))],
    )
    return pl.pallas_call(
        functools.partial(_combine_kernel, tr=tr, first=first, sub=_tile(tr, 128, 8)),
        grid_spec=grid_spec,
        out_shape=jax.ShapeDtypeStruct((n, d), F32),
        compiler_params=_cparams(("arbitrary",)),
        name="combine",
    )(runs_per_tile, run_src, run_src, slots, gates, x1, g, ys)


def _routing_tables(counts_f, *, tme, nb):
    counts = counts_f[:, 0].astype(jnp.int32)
    nblk = jnp.where(counts > 0, (counts + RUN - 1 + tme - 1) // tme, 0)
    blk_end = jnp.cumsum(nblk)
    blk_start = blk_end - nblk
    n_used = blk_end[-1]
    b = jnp.arange(nb, dtype=jnp.int32)
    used = b < n_used
    blk_exp = jnp.minimum(jnp.sum((b[:, None] >= blk_end[None, :]).astype(jnp.int32), axis=1), N_EXPERTS - 1)
    last_exp = jnp.max(jnp.where(nblk > 0, jnp.arange(N_EXPERTS, dtype=jnp.int32), 0))
    blk_exp = jnp.where(used, blk_exp, last_exp).astype(jnp.int32)
    blk_src = jnp.minimum(b, n_used - 1).astype(jnp.int32)
    experts = jnp.arange(N_EXPERTS, dtype=jnp.int32)
    mine = (b[:, None] >= blk_start[None, :]) & (b[:, None] < blk_end[None, :])
    nvalid = jnp.sum(jnp.where(mine, counts[None, :] - (b[:, None] - blk_start[None, :]) * tme, 0), axis=1)
    nvalid = jnp.clip(nvalid, 0, tme).astype(jnp.int32)
    later = (experts[None, :] > experts[:, None]) & (nblk[None, :] > 0)
    nxt_e = jnp.min(jnp.where(later, experts[None, :], N_EXPERTS), axis=1)
    nxt_e = jnp.where(nxt_e == N_EXPERTS, -1, nxt_e)
    blk_next = jnp.sum(jnp.where(blk_exp[:, None] == experts[None, :], nxt_e[None, :], 0), axis=1).astype(jnp.int32)
    partial = used & (nvalid < tme)
    fill_order = jnp.where(partial, 0, jnp.where(used, 2, 1))
    fill_blocks = jnp.argsort(fill_order, stable=True).astype(jnp.int32)
    n_fill = jnp.stack([jnp.sum(partial), jnp.sum(partial | ~used)]).astype(jnp.int32)
    return (blk_start, blk_exp, blk_src, nvalid, blk_next, n_used.reshape(1).astype(jnp.int32), fill_blocks,
            n_fill)


def _run_tables(tile_carry_f, counts_f, blk_start, *, td, tme):
    before = tile_carry_f[:, :, 0].astype(jnp.int32)
    counts = counts_f[:, 0].astype(jnp.int32)
    in_tile = jnp.concatenate([before[1:], counts[None, :]], axis=0) - before
    n_runs = (in_tile + RUN - 1) // RUN
    ends = jnp.cumsum(n_runs, axis=1)
    j = jnp.arange(_stage_slots(td) // RUN, dtype=jnp.int32)
    owner = jnp.sum((j[None, :, None] >= ends[:, None, :]).astype(jnp.int32), axis=2)
    base = blk_start[None, :] * tme + before - (ends - n_runs) * RUN
    mine = owner[:, :, None] == jnp.arange(N_EXPERTS, dtype=jnp.int32)[None, None, :]
    dst = jnp.sum(jnp.where(mine, base[:, None, :], 0), axis=2) + j[None, :] * RUN
    return ends[:, -1].astype(jnp.int32), dst[:, None, :].astype(jnp.int32)


def kernel(x_prompt, x_sample, cache_k, cache_v, state_conv, attn_norm_g, w_in, attn_sinks, conv_w, conv_b,
           conv_ln_g, conv_ln_b, w_out, ffn_norm_g, router_w, router_b, w1, b1, w2, b2, final_norm_g):
    depth = w_in.shape[0]
    assert depth == 1, "single-layer step"
    bp, sp, d = x_prompt.shape
    bs, ss, _ = x_sample.shape
    cw = conv_w.shape[2]
    np_, ns = bp * sp, bs * ss
    n_tok = np_ + ns
    assert sp % WINDOW == 0

    xp2 = x_prompt.reshape(np_, d)
    xs2 = x_sample.reshape(ns, d)
    w_in_bf = w_in[0].astype(BF16)
    w_out_bf = w_out[0].astype(BF16)
    g_attn = attn_norm_g[0].reshape(1, d)
    g_ffn = ffn_norm_g[0].reshape(1, d)
    sinks = attn_sinks[0]
    vec = lambda a: a.reshape(1, cw)

    tab_p = _rope_tables(jnp.arange(sp, dtype=jnp.int32))
    tms = _tile(ns, 512, max(ss, 16))
    tab_s = _rope_tables(PAST_LEN + (jnp.arange(tms, dtype=jnp.int32) % ss))
    qp, kp, vp, kfp, vfp, ap = _in_proj(xp2, g_attn, w_in_bf, tab_p, seq_period=sp, q_dtype=BF16, conv_width=cw)
    qs, _, _, kfs, vfs, as_ = _in_proj(xs2, g_attn, w_in_bf, tab_s, seq_period=None, q_dtype=F32, conv_width=cw)

    att_p = _attn_prompt(qp, kp, vp, sinks, batch=bp, seq=sp)
    ck = cache_k[0].reshape(bs, WINDOW, KV_WIDTH)
    cv_ = cache_v[0].reshape(bs, WINDOW, KV_WIDTH)
    att_s, nk_s, nv_s = _attn_sample(qs, kfs, vfs, ck, cv_, sinks, batch=bs, ts=ss)

    cv_p = _conv_prompt(ap, conv_w[0], vec(conv_b[0]), vec(conv_ln_g[0]), vec(conv_ln_b[0]), batch=bp, seq=sp)
    cv_s = _conv_sample(as_, state_conv[0], conv_w[0], vec(conv_b[0]), vec(conv_ln_g[0]), vec(conv_ln_b[0]),
                        batch=bs, ts=ss)

    n_exp = router_w.shape[2]
    assert n_exp == N_EXPERTS
    rwt_bf = router_w[0].T.astype(BF16)
    rbt = jnp.broadcast_to(router_b[0][:, None], (n_exp, LANES))
    zero_carry = jnp.zeros((n_exp, LANES), F32)
    tr = _tile(ns, 512, 16)
    assert np_ % tr == 0
    x1p, hnp, rt_p, tcar_p, cnt_p = _outproj_router(xp2, att_p, cv_p, w_out_bf, g_ffn, rwt_bf, rbt, zero_carry, tm=tr)
    x1s, hns, rt_s, tcar_s, cnt = _outproj_router(xs2, att_s, cv_s, w_out_bf, g_ffn, rwt_bf, rbt, cnt_p, tm=tr)

    tme = EXPERT_BLOCK_ROWS
    nb = -(-(n_tok * TOP_K + N_EXPERTS * (tme - 1 + RUN - 1)) // tme)
    blk_start, blk_exp, blk_src, blk_nvalid, blk_next, n_used, fill_blocks, n_fill = _routing_tables(
        cnt, tme=tme, nb=nb)
    runs_per_tile, run_dst = _run_tables(jnp.concatenate([tcar_p, tcar_s], axis=0), cnt, blk_start, td=tr, tme=tme)
    xs_sorted = _dispatch(hnp, hns, jnp.concatenate([rt_p, rt_s], axis=1), fill_blocks, n_fill, runs_per_tile,
                          run_dst, td=tr, nb=nb, tme=tme)
    ys = _experts(xs_sorted, w1[0], b1[0], w2[0], b2[0], blk_exp, blk_src, blk_nvalid, blk_next, n_used, tme=tme)
    g_fin = final_norm_g.reshape(1, d)
    tiles_p = np_ // tr
    copy_slots = lambda rt: _by_tile(rt[0:TOP_K].astype(jnp.int32) * SUBLANES, tr)
    copy_gates = lambda rt: _by_tile(rt[TOP_K:2 * TOP_K], tr)
    y_p = _combine(x1p, copy_slots(rt_p), copy_gates(rt_p), runs_per_tile, run_dst[:tiles_p], ys, g_fin,
                   tr=tr, first=0)
    y_s = _combine(x1s, copy_slots(rt_s), copy_gates(rt_s), runs_per_tile, run_dst[tiles_p:], ys, g_fin,
                   tr=tr, first=tiles_p)

    kv5 = lambda t, bb: t.reshape(bb, -1, KV_WIDTH)[:, -WINDOW:].reshape(bb, WINDOW, N_KV_HEADS, HEAD_DIM)
    new_k_p = kv5(kfp, bp)[None]
    new_v_p = kv5(vfp, bp)[None]
    ctx = CONV_K - 1
    new_c_p = ap.reshape(bp, sp, cw)[:, -ctx:][None]
    new_c_s = jnp.concatenate([state_conv[0], as_.reshape(bs, ss, cw)], axis=1)[:, -ctx:][None]
    return (y_p.reshape(bp, sp, d), y_s.reshape(bs, ss, d), new_k_p, new_v_p, new_c_p,
            kv5(nk_s, bs)[None], kv5(nv_s, bs)[None], new_c_s)
```

```python
import functools

import jax
import jax.numpy as jnp
from jax import lax
from jax.experimental import pallas as pl
from jax.experimental.pallas import tpu as pltpu

F32 = jnp.float32
BF16 = jnp.bfloat16

HEAD_DIM = 64
N_Q_HEADS = 8
N_KV_HEADS = 2
WINDOW = 128
ROPE_THETA = 500000.0
ROPE_DIM = 16
CONV_K = 31
N_EXPERTS = 32
TOP_K = 4
SWIGLU_LIMIT = 7.0
SWIGLU_ALPHA = 1.702
RMS_EPS = 1e-5
LN_EPS = 1e-5
PAST_LEN = 16384

LANES = 128
SUBLANES = 8
CONV_HALO = 32
VMEM_LIMIT = 56 * 1024 * 1024
EXPERT_BLOCK_ROWS = 512
RUN = 8
ROUTE_T_ROWS = 2 * TOP_K
ISSUE_UNROLL = 8

ATTN_WIDTH = N_Q_HEADS * HEAD_DIM
KV_WIDTH = N_KV_HEADS * HEAD_DIM


def _tile(n, pref, mult=8):
    t = min(pref, n)
    while t > 0 and (n % t or t % mult):
        t -= 1
    assert t > 0, (n, pref, mult)
    return t


def _cparams(sem):
    return pltpu.CompilerParams(dimension_semantics=sem, vmem_limit_bytes=VMEM_LIMIT)


def _sigmoid(x):
    return 1.0 / (1.0 + jnp.exp(-x))


def _rope_tables(pos):
    half = ROPE_DIM // 2
    inv_freq = jnp.power(jnp.float32(ROPE_THETA), -jnp.arange(half, dtype=F32) * 2.0 / ROPE_DIM)
    ang = pos.astype(F32)[:, None] * inv_freq[None, :]
    cos, sin = jnp.cos(ang), jnp.sin(ang)
    l64 = jnp.arange(LANES) % HEAD_DIM
    f = l64 % half
    cos_l, sin_l = cos[:, f], sin[:, f]
    c = jnp.where(l64 < ROPE_DIM, cos_l, 1.0)
    s1 = jnp.where(l64 < half, -sin_l, 0.0)
    s2 = jnp.where((l64 >= half) & (l64 < ROPE_DIM), sin_l, 0.0)
    return c.astype(F32), s1.astype(F32), s2.astype(F32)


def _inproj_kernel(x_ref, g_ref, w_ref, c_ref, s1_ref, s2_ref,
                   q_ref, k_ref, v_ref, kf_ref, vf_ref, a_ref, *, conv_width):
    x = x_ref[...]
    ms = jnp.mean(x * x, axis=-1, keepdims=True)
    h = (x * lax.rsqrt(ms + RMS_EPS) * g_ref[...]).astype(BF16)
    z = jnp.dot(h, w_ref[...], preferred_element_type=F32)
    c, s1, s2 = c_ref[...], s1_ref[...], s2_ref[...]
    half = ROPE_DIM // 2

    def rot(t):
        return t * c + pltpu.roll(t, LANES - half, 1) * s1 + pltpu.roll(t, half, 1) * s2

    scale = HEAD_DIM ** -0.5
    for j in range(ATTN_WIDTH // LANES):
        q_ref[:, j * LANES:(j + 1) * LANES] = (rot(z[:, j * LANES:(j + 1) * LANES]) * scale).astype(q_ref.dtype)
    k0 = ATTN_WIDTH
    kr = rot(z[:, k0:k0 + KV_WIDTH])
    k_ref[...] = kr.astype(BF16)
    kf_ref[...] = kr
    v0 = k0 + KV_WIDTH
    vv = z[:, v0:v0 + KV_WIDTH]
    v_ref[...] = vv.astype(BF16)
    vf_ref[...] = vv
    u0 = v0 + KV_WIDTH
    g0 = u0 + conv_width
    a_ref[...] = z[:, u0:g0] * _sigmoid(z[:, g0:g0 + conv_width])


def _in_proj(x2, g, w_bf, tables, *, seq_period, q_dtype, conv_width):
    n, d = x2.shape
    in_w = w_bf.shape[1]
    if seq_period is None:
        tm = tables[0].shape[0]
        tmap = lambda i: (0, 0)
    else:
        tm = _tile(seq_period, 1024, 16)
        per = seq_period // tm
        tmap = lambda i: (i % per, 0)
    assert n % tm == 0
    row = lambda w: pl.BlockSpec((tm, w), lambda i: (i, 0))
    tab = pl.BlockSpec((tm, LANES), tmap)
    return pl.pallas_call(
        functools.partial(_inproj_kernel, conv_width=conv_width),
        grid=(n // tm,),
        in_specs=[row(d), pl.BlockSpec((1, d), lambda i: (0, 0)),
                  pl.BlockSpec((d, in_w), lambda i: (0, 0)), tab, tab, tab],
        out_specs=[row(ATTN_WIDTH), row(KV_WIDTH), row(KV_WIDTH), row(KV_WIDTH), row(KV_WIDTH),
                   row(conv_width)],
        out_shape=[jax.ShapeDtypeStruct((n, ATTN_WIDTH), q_dtype),
                   jax.ShapeDtypeStruct((n, KV_WIDTH), BF16),
                   jax.ShapeDtypeStruct((n, KV_WIDTH), BF16),
                   jax.ShapeDtypeStruct((n, KV_WIDTH), F32),
                   jax.ShapeDtypeStruct((n, KV_WIDTH), F32),
                   jax.ShapeDtypeStruct((n, conv_width), F32)],
        compiler_params=_cparams(("parallel",)),
        name="in_proj",
    )(x2, g, w_bf, *tables)


def _dup_head(t, h):
    sw = pltpu.roll(t, HEAD_DIM, 1)
    low = lax.broadcasted_iota(jnp.int32, t.shape, 1) < HEAD_DIM
    return jnp.where(low, t, sw) if h == 0 else jnp.where(low, sw, t)


def _nt_dot(a, b):
    return lax.dot_general(a, b, (((1,), (1,)), ((), ())), preferred_element_type=F32)


def _attn_prompt_kernel(sink_ref, q_ref, kc_ref, kp_ref, vc_ref, vp_ref, o_ref, *, qb):
    j = pl.program_id(1)
    w = WINDOW
    k_all = jnp.concatenate([kp_ref[...], kc_ref[...]], axis=0).astype(F32)
    v_all = jnp.concatenate([vp_ref[...], vc_ref[...]], axis=0).astype(F32)
    r = lax.broadcasted_iota(jnp.int32, (w, 2 * w), 0)
    kk = lax.broadcasted_iota(jnp.int32, (w, 2 * w), 1)
    band = (kk > r) & (kk <= r + w)
    low = lax.broadcasted_iota(jnp.int32, (w, LANES), 1) < HEAD_DIM
    zero = jnp.zeros((w, LANES), BF16)
    group = N_Q_HEADS // N_KV_HEADS
    for h in range(N_KV_HEADS):
        kd_all = _dup_head(k_all, h).astype(BF16)
        vd_all = _dup_head(v_all, h).astype(BF16)
        for sub in range(qb):
            valid = band & ((kk >= w) | (j > 0)) if sub == 0 else band
            kd = kd_all[sub * w:(sub + 2) * w, :]
            vd = vd_all[sub * w:(sub + 2) * w, :]
            for jj in range(group // 2):
                col = (h * group // 2 + jj) * LANES
                qv = q_ref[sub * w:(sub + 1) * w, col:col + LANES]
                halves = []
                for half in range(2):
                    head = h * group + jj * 2 + half
                    qm = jnp.where(low if half == 0 else ~low, qv, zero)
                    s = jnp.where(valid, _nt_dot(qm, kd), -jnp.inf)
                    sink = sink_ref[head]
                    m = jnp.maximum(jnp.max(s, axis=-1, keepdims=True), sink)
                    p = jnp.exp(s - m)
                    den = jnp.sum(p, axis=-1, keepdims=True) + jnp.exp(sink - m)
                    o = jnp.dot(p.astype(BF16), vd, preferred_element_type=F32)
                    halves.append(o / den)
                o_ref[sub * w:(sub + 1) * w, col:col + LANES] = (
                    jnp.where(low, halves[0], halves[1]).astype(o_ref.dtype))


def _attn_prompt(q, k, v, sinks, *, batch, seq):
    nb = seq // WINDOW
    qb = next(c for c in (4, 2, 1) if nb % c == 0)
    steps = nb // qb
    cur = lambda b, j: (b * steps + j, 0)
    prev = lambda b, j: (b * nb + jnp.maximum(j * qb - 1, 0), 0)
    return pl.pallas_call(
        functools.partial(_attn_prompt_kernel, qb=qb),
        grid=(batch, steps),
        in_specs=[pl.BlockSpec(memory_space=pltpu.SMEM),
                  pl.BlockSpec((qb * WINDOW, ATTN_WIDTH), cur),
                  pl.BlockSpec((qb * WINDOW, KV_WIDTH), cur), pl.BlockSpec((WINDOW, KV_WIDTH), prev),
                  pl.BlockSpec((qb * WINDOW, KV_WIDTH), cur), pl.BlockSpec((WINDOW, KV_WIDTH), prev)],
        out_specs=pl.BlockSpec((qb * WINDOW, ATTN_WIDTH), cur),
        out_shape=jax.ShapeDtypeStruct((batch * seq, ATTN_WIDTH), BF16),
        compiler_params=_cparams(("parallel", "parallel")),
        name="attn_prompt",
    )(sinks, q, k, k, v, v)


def _attn_sample_kernel(sink_ref, q_ref, kn_ref, vn_ref, ck_ref, cv_ref, o_ref, nk_ref, nv_ref, *, gb, ts):
    w = WINDOW
    group = N_Q_HEADS // N_KV_HEADS
    rows = group * ts
    low = lax.broadcasted_iota(jnp.int32, (ts, LANES), 1) < HEAD_DIM
    pad = jnp.zeros((ts, LANES), F32)
    s_c, s_n, v_dup = [], [], []
    for b in range(gb):
        kc, vc = ck_ref[b], cv_ref[b]
        kn, vn = kn_ref[b * ts:(b + 1) * ts, :], vn_ref[b * ts:(b + 1) * ts, :]
        nk_ref[b, 0:w - ts, :] = kc[ts:, :]
        nk_ref[b, w - ts:, :] = kn
        nv_ref[b, 0:w - ts, :] = vc[ts:, :]
        nv_ref[b, w - ts:, :] = vn
        knp = jnp.concatenate([kn, pad], axis=0)
        vnp = jnp.concatenate([vn, pad], axis=0)
        qb = q_ref[b * ts:(b + 1) * ts, :]
        for h in range(N_KV_HEADS):
            parts = []
            for jj in range(group // 2):
                col = (h * group // 2 + jj) * LANES
                qv = qb[:, col:col + LANES]
                parts += [jnp.where(low, qv, 0.0), jnp.where(low, 0.0, qv)]
            lhs = jnp.concatenate(parts, axis=0).astype(BF16)
            s_c.append(_nt_dot(lhs, _dup_head(kc, h).astype(BF16)))
            s_n.append(_nt_dot(lhs, _dup_head(knp, h).astype(BF16)))
            v_dup.append((_dup_head(vc, h).astype(BF16), _dup_head(vnp, h).astype(BF16)))
    s_c = jnp.concatenate(s_c, axis=0)
    s_n = jnp.concatenate(s_n, axis=0)
    n_rows = s_c.shape[0]
    ridx = lax.broadcasted_iota(jnp.int32, (n_rows, 1), 0)
    t_row = ridx % ts
    head_row = (ridx // ts) % N_Q_HEADS
    sink = jnp.zeros((n_rows, 1), F32)
    for hd in range(N_Q_HEADS):
        sink = jnp.where(head_row == hd, sink_ref[hd], sink)
    c_idx = lax.broadcasted_iota(jnp.int32, (n_rows, w), 1)
    n_idx = lax.broadcasted_iota(jnp.int32, (n_rows, 2 * ts), 1)
    s_c = jnp.where(c_idx > t_row, s_c, -jnp.inf)
    s_n = jnp.where(n_idx <= t_row, s_n, -jnp.inf)
    m = jnp.maximum(jnp.maximum(jnp.max(s_c, axis=-1, keepdims=True), jnp.max(s_n, axis=-1, keepdims=True)), sink)
    p_c = jnp.exp(s_c - m)
    p_n = jnp.exp(s_n - m)
    den = jnp.sum(p_c, axis=-1, keepdims=True) + jnp.sum(p_n, axis=-1, keepdims=True) + jnp.exp(sink - m)
    p_c = p_c.astype(BF16)
    p_n = p_n.astype(BF16)
    outs = []
    for b in range(gb):
        cols = []
        for h in range(N_KV_HEADS):
            ci = b * N_KV_HEADS + h
            sl = slice(ci * rows, (ci + 1) * rows)
            vdc, vdn = v_dup[ci]
            o = (jnp.dot(p_c[sl], vdc, preferred_element_type=F32)
                 + jnp.dot(p_n[sl], vdn, preferred_element_type=F32)) / den[sl]
            for jj in range(group // 2):
                lo_part = o[(2 * jj) * ts:(2 * jj + 1) * ts, :]
                hi_part = o[(2 * jj + 1) * ts:(2 * jj + 2) * ts, :]
                cols.append(jnp.where(low, lo_part, hi_part))
        outs.append(jnp.concatenate(cols, axis=1))
    o_ref[...] = jnp.concatenate(outs, axis=0).astype(o_ref.dtype)


def _attn_sample(q, kf, vf, cache_k, cache_v, sinks, *, batch, ts):
    assert ts % 8 == 0 and ts <= WINDOW
    gb = _tile(batch, 8, 2)
    tok = lambda w: pl.BlockSpec((gb * ts, w), lambda i: (i, 0))
    cache = pl.BlockSpec((gb, WINDOW, KV_WIDTH), lambda i: (i, 0, 0))
    cshape = jax.ShapeDtypeStruct((batch, WINDOW, KV_WIDTH), F32)
    return pl.pallas_call(
        functools.partial(_attn_sample_kernel, gb=gb, ts=ts),
        grid=(batch // gb,),
        in_specs=[pl.BlockSpec(memory_space=pltpu.SMEM), tok(ATTN_WIDTH), tok(KV_WIDTH), tok(KV_WIDTH),
                  cache, cache],
        out_specs=[tok(ATTN_WIDTH), cache, cache],
        out_shape=[jax.ShapeDtypeStruct((batch * ts, ATTN_WIDTH), BF16), cshape, cshape],
        compiler_params=_cparams(("parallel",)),
        name="attn_sample",
    )(sinks, q, kf, vf, cache_k, cache_v)


def _ln_swish(acc, b, lg, lb):
    y = acc + b
    mu = jnp.mean(y, axis=-1, keepdims=True)
    yc = y - mu
    var = jnp.mean(yc * yc, axis=-1, keepdims=True)
    yn = yc * lax.rsqrt(var + LN_EPS) * lg + lb
    return yn * _sigmoid(yn)


def _conv_prompt_kernel(a_ref, ap_ref, w_ref, b_ref, lg_ref, lb_ref, o_ref, win_ref, *, tt, rc):
    j = pl.program_id(1)
    n = CONV_HALO + tt
    win = jnp.concatenate([jnp.where(j > 0, ap_ref[...], 0.0), a_ref[...]], axis=0)
    win_ref[0] = win
    for r in range(1, SUBLANES):
        win_ref[r] = pltpu.roll(win, n - r, 0)
    off = CONV_HALO - (CONV_K - 1)
    b, lg, lb = b_ref[...], lg_ref[...], lb_ref[...]
    for c in range(tt // rc):
        acc = jnp.zeros((rc, a_ref.shape[1]), F32)
        for k in range(CONV_K):
            s = off + k
            base = c * rc + (s // SUBLANES) * SUBLANES
            wk = jnp.concatenate([w_ref[k]] * (rc // SUBLANES), axis=0)
            acc = acc + wk * win_ref[s % SUBLANES, base:base + rc, :]
        o_ref[c * rc:(c + 1) * rc, :] = _ln_swish(acc, b, lg, lb).astype(o_ref.dtype)


def _conv_prompt(a, w, b, lg, lb, *, batch, seq):
    cw = a.shape[1]
    tt = _tile(seq, 512, CONV_HALO)
    rc = _tile(tt, 32, 16)
    nt = seq // tt
    per = tt // CONV_HALO
    cur = lambda bb, j: (bb * nt + j, 0)
    prev = lambda bb, j: (jnp.maximum((bb * nt + j) * per - 1, 0), 0)
    vec = pl.BlockSpec((1, cw), lambda bb, j: (0, 0))
    return pl.pallas_call(
        functools.partial(_conv_prompt_kernel, tt=tt, rc=rc),
        grid=(batch, nt),
        in_specs=[pl.BlockSpec((tt, cw), cur), pl.BlockSpec((CONV_HALO, cw), prev),
                  pl.BlockSpec((CONV_K, SUBLANES, cw), lambda bb, j: (0, 0, 0)), vec, vec, vec],
        out_specs=pl.BlockSpec((tt, cw), cur),
        out_shape=jax.ShapeDtypeStruct((batch * seq, cw), BF16),
        scratch_shapes=[pltpu.VMEM((SUBLANES, CONV_HALO + tt, cw), F32)],
        compiler_params=_cparams(("parallel", "parallel")),
        name="conv_prompt",
    )(a, a, jnp.broadcast_to(w[:, None, :], (CONV_K, SUBLANES, cw)), b, lg, lb)


def _conv_sample_kernel(a_ref, st_ref, w_ref, b_ref, lg_ref, lb_ref, o_ref, win_ref, *, gb, ts):
    ctx = CONV_K - 1
    b, lg, lb = b_ref[...], lg_ref[...], lb_ref[...]
    for bb in range(gb):
        win_ref[bb, 0:ctx, :] = st_ref[bb]
        win_ref[bb, ctx:ctx + ts, :] = a_ref[bb * ts:(bb + 1) * ts, :]
    outs = []
    for bb in range(gb):
        acc = jnp.zeros((ts, a_ref.shape[1]), F32)
        for k in range(CONV_K):
            wk = jnp.concatenate([w_ref[k]] * (ts // SUBLANES), axis=0)
            acc = acc + wk * win_ref[bb, k:k + ts, :]
        outs.append(_ln_swish(acc, b, lg, lb))
    o_ref[...] = jnp.concatenate(outs, axis=0).astype(o_ref.dtype)


def _conv_sample(a, state, w, b, lg, lb, *, batch, ts):
    cw = a.shape[1]
    ctx = CONV_K - 1
    gb = _tile(batch, 8, 2)
    vec = pl.BlockSpec((1, cw), lambda i: (0, 0))
    return pl.pallas_call(
        functools.partial(_conv_sample_kernel, gb=gb, ts=ts),
        grid=(batch // gb,),
        in_specs=[pl.BlockSpec((gb * ts, cw), lambda i: (i, 0)),
                  pl.BlockSpec((gb, ctx, cw), lambda i: (i, 0, 0)),
                  pl.BlockSpec((CONV_K, SUBLANES, cw), lambda i: (0, 0, 0)), vec, vec, vec],
        out_specs=pl.BlockSpec((gb * ts, cw), lambda i: (i, 0)),
        out_shape=jax.ShapeDtypeStruct((batch * ts, cw), BF16),
        scratch_shapes=[pltpu.VMEM((gb, ctx + ts + 2, cw), F32)],
        compiler_params=_cparams(("parallel",)),
        name="conv_sample",
    )(a, state, jnp.broadcast_to(w[:, None, :], (CONV_K, SUBLANES, cw)), b, lg, lb)


def _outproj_router_kernel(x_ref, att_ref, cv_ref, wo_ref, g_ref, rw_ref, rb_ref, cin_ref,
                           x1_ref, hn_ref, rt_ref, tcar_ref, cnt_ref, carry_ref):
    i = pl.program_id(0)

    @pl.when(i == 0)
    def _():
        carry_ref[...] = cin_ref[...]

    tm = x_ref.shape[0]
    aw = att_ref.shape[1]
    n_exp = rw_ref.shape[0]
    r = lax.broadcasted_iota(jnp.int32, (tm, tm), 0)
    c = lax.broadcasted_iota(jnp.int32, (tm, tm), 1)
    before = (r < c).astype(BF16)
    eid = lax.broadcasted_iota(jnp.int32, (n_exp, tm), 0)
    carry = carry_ref[...][:, 0:1]
    tcar_ref[0] = carry_ref[...]
    mix = (jnp.dot(att_ref[...], wo_ref[0:aw, :], preferred_element_type=F32)
           + jnp.dot(cv_ref[...], wo_ref[aw:, :], preferred_element_type=F32))
    x1 = x_ref[...] + mix
    x1_ref[...] = x1
    ms = jnp.mean(x1 * x1, axis=-1, keepdims=True)
    hn = (x1 * lax.rsqrt(ms + RMS_EPS) * g_ref[...]).astype(BF16)
    hn_ref[...] = hn
    logits = _nt_dot(rw_ref[...], hn) + rb_ref[...][:, 0:1]
    onehot = jnp.zeros((n_exp, tm), F32)
    vals, idxs = [], []
    for _ in range(TOP_K):
        m = jnp.max(logits, axis=0, keepdims=True)
        idx = jnp.min(jnp.where(logits == m, eid, n_exp), axis=0, keepdims=True)
        sel = eid == idx
        onehot = onehot + sel.astype(F32)
        logits = jnp.where(sel, -jnp.inf, logits)
        vals.append(m)
        idxs.append(idx)
    es = [jnp.exp(v - vals[0]) for v in vals]
    den = es[0] + es[1] + es[2] + es[3]
    local = jnp.dot(onehot.astype(BF16), before, preferred_element_type=F32)
    count = jnp.sum(onehot, axis=1, keepdims=True)
    padded = jnp.ceil(count * (1.0 / RUN)) * RUN
    er = lax.broadcasted_iota(jnp.int32, (n_exp, n_exp), 0)
    ec = lax.broadcasted_iota(jnp.int32, (n_exp, n_exp), 1)
    start = jnp.dot((ec < er).astype(BF16), jnp.broadcast_to(padded, (n_exp, LANES)).astype(BF16),
                    preferred_element_type=F32)[:, 0:1]
    slot_of = local + start
    slots = [jnp.sum(jnp.where(eid == idxs[k], slot_of, 0.0), axis=0, keepdims=True) for k in range(TOP_K)]
    rt_ref[...] = jnp.concatenate(slots + [e / den for e in es], axis=0)
    carry = carry + count
    carry_ref[...] = jnp.broadcast_to(carry, carry_ref.shape)
    cnt_ref[...] = jnp.broadcast_to(carry, cnt_ref.shape)


def _store_slabs(ref, val, rows):
    for j in range(val.shape[1] // LANES):
        ref[pl.ds(j, rows, stride=SUBLANES), :] = val[:, j * LANES:(j + 1) * LANES]


def _load_slabs(ref, rows, dtype):
    return jnp.concatenate([ref[pl.ds(j, rows, stride=SUBLANES), :].astype(dtype) for j in range(SUBLANES)],
                           axis=1)


def _outproj_router(x2, att, cv, wo_bf, g, rwt_bf, rbt, carry_in, *, tm):
    n, d = x2.shape
    assert d == SUBLANES * LANES and n % tm == 0
    n_exp = rwt_bf.shape[0]
    row = lambda w: pl.BlockSpec((tm, w), lambda i: (i, 0))
    full = lambda a: pl.BlockSpec(a.shape, lambda i: (0,) * a.ndim)
    return pl.pallas_call(
        _outproj_router_kernel,
        grid=(n // tm,),
        in_specs=[row(d), row(att.shape[1]), row(cv.shape[1]), full(wo_bf), full(g), full(rwt_bf), full(rbt),
                  full(carry_in)],
        out_specs=[row(d), row(d),
                   pl.BlockSpec((ROUTE_T_ROWS, tm), lambda i: (0, i)),
                   pl.BlockSpec((1, n_exp, LANES), lambda i: (i, 0, 0)),
                   pl.BlockSpec((n_exp, LANES), lambda i: (0, 0))],
        out_shape=[jax.ShapeDtypeStruct((n, d), F32), jax.ShapeDtypeStruct((n, d), BF16),
                   jax.ShapeDtypeStruct((ROUTE_T_ROWS, n), F32),
                   jax.ShapeDtypeStruct((n // tm, n_exp, LANES), F32),
                   jax.ShapeDtypeStruct((n_exp, LANES), F32)],
        scratch_shapes=[pltpu.VMEM((n_exp, LANES), F32)],
        compiler_params=_cparams(("arbitrary",)),
        name="outproj_router",
    )(x2, att, cv, wo_bf, g, rwt_bf, rbt, carry_in)


def _dispatch_kernel(fill_ref, nfill_ref, tot_ref, dst_ref, rt_ref, hn_ref, hn2_ref,
                     xs_ref, zero_ref, stage_ref, sem, zsem, *, td, tme, n_first, chunk):
    i = pl.program_id(0)
    last = pl.num_programs(0) - 1
    par = i % 2
    n_slots = stage_ref.shape[1] // SUBLANES

    def fill(f, s):
        row = pl.multiple_of(fill_ref[f] * (tme * SUBLANES), tme * SUBLANES)
        return pltpu.make_async_copy(zero_ref, xs_ref.at[pl.ds(row, tme * SUBLANES)], zsem.at[s])

    def fills(lo, hi, s, act):
        def body(f, carry):
            act(fill(f, s))
            return carry

        lax.fori_loop(lo, hi, body, 0)

    @pl.when(i == 0)
    def _():
        zero_ref[...] = jnp.zeros(zero_ref.shape, F32)
        fills(0, nfill_ref[0], 0, lambda c: c.start())
        fills(nfill_ref[0], nfill_ref[1], 1, lambda c: c.start())
        fills(0, nfill_ref[0], 0, lambda c: c.wait())

    def group_by_expert(tok_ref):
        hn = tok_ref[...]
        slots = rt_ref[0:TOP_K, :].astype(jnp.int32)
        for c in range(n_slots // chunk):
            s = lax.broadcasted_iota(jnp.int32, (chunk, td), 0) + c * chunk
            hit = s == slots[0:1, :]
            for k in range(1, TOP_K):
                hit = hit | (s == slots[k:k + 1, :])
            rows = jnp.dot(jnp.where(hit, 1.0, 0.0).astype(BF16), hn, preferred_element_type=F32)
            _store_slabs(stage_ref.at[par, pl.ds(c * chunk * SUBLANES, chunk * SUBLANES)], rows, chunk)

    def run_copy(buf, src_slot, dst_row, runs=1):
        src = pl.multiple_of(src_slot * SUBLANES, RUN * SUBLANES)
        dst = pl.multiple_of(dst_row * SUBLANES, SUBLANES)
        return pltpu.make_async_copy(stage_ref.at[buf, pl.ds(src, runs * RUN * SUBLANES)],
                                     xs_ref.at[pl.ds(dst, runs * RUN * SUBLANES)], sem)

    def wait_runs(tile):
        def wait_many(q, carry):
            run_copy(0, 0, 0, ISSUE_UNROLL).wait()
            return carry

        def wait_one(j, carry):
            run_copy(0, 0, 0).wait()
            return carry

        n_many = tot_ref[tile] // ISSUE_UNROLL
        lax.fori_loop(0, n_many, wait_many, 0)
        lax.fori_loop(n_many * ISSUE_UNROLL, tot_ref[tile], wait_one, 0)

    @pl.when(i >= 2)
    def _():
        wait_runs(i - 2)

    @pl.when(i >= 1)
    def _():
        def issue(j, queue):
            run_copy(1 - par, j * RUN, dst_ref[0, 0, j]).start(priority=queue)

        def issue_many(q, carry):
            for u in range(ISSUE_UNROLL):
                issue(q * ISSUE_UNROLL + u, u % 2)
            return carry

        def issue_one(j, carry):
            issue(j, 0)
            return carry

        n_full = tot_ref[i - 1] // ISSUE_UNROLL
        lax.fori_loop(0, n_full, issue_many, 0)
        lax.fori_loop(n_full * ISSUE_UNROLL, tot_ref[i - 1], issue_one, 0)

    @pl.when(i < n_first)
    def _():
        group_by_expert(hn_ref)

    @pl.when((i >= n_first) & (i < last))
    def _():
        group_by_expert(hn2_ref)

    @pl.when(i == last)
    def _():
        wait_runs(i - 1)
        fills(nfill_ref[0], nfill_ref[1], 1, lambda c: c.wait())


def _stage_slots(td):
    return td * TOP_K + N_EXPERTS * RUN


def _by_tile(field, td):
    n = field.shape[1]
    return field.reshape(TOP_K, n // td, td).transpose(1, 0, 2).reshape(n // td, 1, TOP_K * td)


def _dispatch(hn_a, hn_b, route_t, fill_blocks, n_fill, runs_per_tile, run_dst, *, td, nb, tme):
    na, d = hn_a.shape
    assert na % td == 0 and hn_b.shape[0] % td == 0 and td % RUN == 0
    n_first = na // td
    n_second = hn_b.shape[0] // td
    tiles = n_first + n_second
    n_slots = _stage_slots(td)
    chunk = _tile(n_slots, 768, SUBLANES)
    tok = lambda m: pl.BlockSpec((td, d), m)
    grid_spec = pltpu.PrefetchScalarGridSpec(
        num_scalar_prefetch=3,
        grid=(tiles + 1,),
        in_specs=[pl.BlockSpec((1, 1, n_slots // RUN), lambda i, *_: (jnp.maximum(i - 1, 0), 0, 0),
                               memory_space=pltpu.SMEM),
                  pl.BlockSpec((ROUTE_T_ROWS, td), lambda i, *_: (0, jnp.minimum(i, tiles - 1))),
                  tok(lambda i, *_: (jnp.minimum(i, n_first - 1), 0)),
                  tok(lambda i, *_: (jnp.clip(i - n_first, 0, n_second - 1), 0))],
        out_specs=pl.BlockSpec(memory_space=pl.ANY),
        scratch_shapes=[pltpu.VMEM((tme * SUBLANES, LANES), F32),
                        pltpu.VMEM((2, n_slots * SUBLANES, LANES), F32),
                        pltpu.SemaphoreType.DMA(()), pltpu.SemaphoreType.DMA((2,))],
    )
    return pl.pallas_call(
        functools.partial(_dispatch_kernel, td=td, tme=tme, n_first=n_first, chunk=chunk),
        grid_spec=grid_spec,
        out_shape=jax.ShapeDtypeStruct((nb * tme * SUBLANES, LANES), F32),
        compiler_params=_cparams(("arbitrary",)),
        name="dispatch",
    )(fill_blocks, n_fill, runs_per_tile, run_dst, route_t, hn_a, hn_b)


def _experts_kernel(be_ref, bsrc_ref, nv_ref, nx_ref, nu_ref, x_ref, w1_hbm, b1_ref, w2_hbm, b2_ref,
                    y_ref, w1f_ref, w2f_ref, w1b_ref, w2b_ref, par_ref, wsem, *, tme):
    i = pl.program_id(0)
    e = be_ref[i]
    e_prev = be_ref[jnp.maximum(i - 1, 0)]
    d_ff = w2f_ref.shape[1]
    half = tme // 2

    def fetch(expert, s):
        return (pltpu.make_async_copy(w1_hbm.at[expert], w1f_ref.at[s], wsem.at[0, s]),
                pltpu.make_async_copy(w2_hbm.at[expert], w2f_ref.at[s], wsem.at[1, s]))

    @pl.when(i == 0)
    def _():
        par_ref[0] = 0
        for c in fetch(e, 0):
            c.start()

    @pl.when((i == 0) | (e != e_prev))
    def _():
        s = par_ref[0]
        for c in fetch(e, s):
            c.wait()
        w1b_ref[...] = w1f_ref[s].astype(BF16)
        w2b_ref[...] = w2f_ref[s].astype(BF16)
        nxt = nx_ref[i]

        @pl.when(nxt >= 0)
        def _():
            for c in fetch(nxt, 1 - s):
                c.start()

        par_ref[0] = 1 - s

    def ffn(x):
        h = jnp.dot(x, w1b_ref[...], preferred_element_type=F32) + b1_ref[0]
        x_glu = jnp.minimum(h[:, :d_ff], SWIGLU_LIMIT)
        x_lin = jnp.clip(h[:, d_ff:], -SWIGLU_LIMIT, SWIGLU_LIMIT)
        act = x_glu * _sigmoid(SWIGLU_ALPHA * x_glu) * (x_lin + 1.0)
        return jnp.dot(act.astype(BF16), w2b_ref[...], preferred_element_type=F32) + b2_ref[0]

    used = i < nu_ref[0]
    nv = nv_ref[i]

    @pl.when(used & (nv > half))
    def _():
        _store_slabs(y_ref, ffn(_load_slabs(x_ref, tme, BF16)), tme)

    @pl.when(used & (nv <= half))
    def _():
        rows = half * SUBLANES
        _store_slabs(y_ref.at[pl.ds(0, rows)], ffn(_load_slabs(x_ref.at[pl.ds(0, rows)], half, BF16)), half)
        y_ref[pl.ds(rows, rows), :] = jnp.zeros((rows, LANES), F32)

    @pl.when(i == nu_ref[0])
    def _():
        y_ref[...] = jnp.zeros(y_ref.shape, F32)


def _experts(xs, w1, b1, w2, b2, blk_exp, blk_src, blk_nvalid, blk_next, n_used, *, tme):
    n_exp, d, h2 = w1.shape
    d_ff = w2.shape[1]
    nb = xs.shape[0] // (tme * SUBLANES)
    slab = lambda m: pl.BlockSpec((tme * SUBLANES, LANES), m)
    grid_spec = pltpu.PrefetchScalarGridSpec(
        num_scalar_prefetch=5,
        grid=(nb,),
        in_specs=[slab(lambda i, be, bs, nv, nx, nu: (bs[i], 0)),
                  pl.BlockSpec(memory_space=pl.ANY),
                  pl.BlockSpec((1, 1, h2), lambda i, be, bs, nv, nx, nu: (be[i], 0, 0)),
                  pl.BlockSpec(memory_space=pl.ANY),
                  pl.BlockSpec((1, 1, d), lambda i, be, bs, nv, nx, nu: (be[i], 0, 0))],
        out_specs=slab(lambda i, be, bs, nv, nx, nu: (jnp.minimum(i, nu[0]), 0)),
        scratch_shapes=[pltpu.VMEM((2, d, h2), F32), pltpu.VMEM((2, d_ff, d), F32),
                        pltpu.VMEM((d, h2), BF16), pltpu.VMEM((d_ff, d), BF16),
                        pltpu.SMEM((1,), jnp.int32), pltpu.SemaphoreType.DMA((2, 2))],
    )
    return pl.pallas_call(
        functools.partial(_experts_kernel, tme=tme),
        grid_spec=grid_spec,
        out_shape=jax.ShapeDtypeStruct(xs.shape, F32),
        input_output_aliases={5: 0},
        compiler_params=_cparams(("arbitrary",)),
        name="experts",
    )(blk_exp, blk_src, blk_nvalid, blk_next, n_used, xs, w1, b1.reshape(n_exp, 1, h2), w2,
      b2.reshape(n_exp, 1, d))


def _combine_kernel(tot_ref, rcur_ref, rnext_ref, slot_ref, gate_ref, x1_ref, g_ref, ys_ref, o_ref,
                    stage_ref, moe_ref, sem, *, tr, first, sub):
    i = pl.program_id(0)
    par = i % 2

    def run_copy(buf, j, src_row, runs=1):
        src = pl.multiple_of(src_row * SUBLANES, SUBLANES)
        dst = pl.multiple_of(j * (RUN * SUBLANES), RUN * SUBLANES)
        return pltpu.make_async_copy(ys_ref.at[pl.ds(src, runs * RUN * SUBLANES)],
                                     stage_ref.at[buf, pl.ds(dst, runs * RUN * SUBLANES)], sem.at[buf])

    def gather(rref, buf, n):
        def issue_many(q, carry):
            for u in range(ISSUE_UNROLL):
                j = q * ISSUE_UNROLL + u
                run_copy(buf, j, rref[0, 0, j]).start()
            return carry

        def issue_one(j, carry):
            run_copy(buf, j, rref[0, 0, j]).start()
            return carry

        lax.fori_loop(0, n // ISSUE_UNROLL, issue_many, 0)
        lax.fori_loop((n // ISSUE_UNROLL) * ISSUE_UNROLL, n, issue_one, 0)

    @pl.when(i == 0)
    def _():
        gather(rcur_ref, 0, tot_ref[first])

    @pl.when(i + 1 < pl.num_programs(0))
    def _():
        gather(rnext_ref, 1 - par, tot_ref[first + i + 1])

    def wait_many(q, carry):
        run_copy(par, 0, 0, ISSUE_UNROLL).wait()
        return carry

    def wait_one(j, carry):
        run_copy(par, 0, 0).wait()
        return carry

    n_runs = tot_ref[first + i]
    lax.fori_loop(0, n_runs // ISSUE_UNROLL, wait_many, 0)
    lax.fori_loop((n_runs // ISSUE_UNROLL) * ISSUE_UNROLL, n_runs, wait_one, 0)

    def token(buf, r, carry):
        acc = None
        for k in range(TOP_K):
            s = pl.multiple_of(slot_ref[0, 0, k * tr + r], SUBLANES)
            term = gate_ref[0, 0, k * tr + r] * stage_ref[buf, pl.ds(s, SUBLANES), :]
            acc = term if acc is None else acc + term
        moe_ref[pl.ds(pl.multiple_of(r * SUBLANES, SUBLANES), SUBLANES), :] = acc
        return carry

    for buf in range(2):
        @pl.when(par == buf)
        def _():
            lax.fori_loop(0, tr, functools.partial(token, buf), 0, unroll=ISSUE_UNROLL)

    for c in range(tr // sub):
        rows = slice(c * sub, (c + 1) * sub)
        y = x1_ref[rows, :] + _load_slabs(moe_ref.at[pl.ds(c * sub * SUBLANES, sub * SUBLANES)], sub, F32)
        ms = jnp.mean(y * y, axis=-1, keepdims=True)
        o_ref[rows, :] = y * lax.rsqrt(ms + RMS_EPS) * g_ref[...]


def _combine(x1, slots, gates, runs_per_tile, run_src, ys, g, *, tr, first):
    n, d = x1.shape
    steps = n // tr
    n_slots = _stage_slots(tr)
    smem = lambda a, m: pl.BlockSpec((1, 1, a.shape[2]), m, memory_space=pltpu.SMEM)
    cur = lambda i, tot: (i, 0, 0)
    nxt = lambda i, tot: (jnp.minimum(i + 1, steps - 1), 0, 0)
    grid_spec = pltpu.PrefetchScalarGridSpec(
        num_scalar_prefetch=1,
        grid=(steps,),
        in_specs=[smem(run_src, cur), smem(run_src, nxt), smem(slots, cur), smem(gates, cur),
                  pl.BlockSpec((tr, d), lambda i, tot: (i, 0)),
                  pl.BlockSpec((1, d), lambda i, tot: (0, 0)),
                  pl.BlockSpec(memory_space=pl.ANY)],
        out_specs=pl.BlockSpec((tr, d), lambda i, tot: (i, 0)),
        scratch_shapes=[pltpu.VMEM((2, n_slots * SUBLANES, LANES), F32), pltpu.VMEM((tr * SUBLANES, LANES), F32),
                        pltpu.SemaphoreType.DMA((2,))],
    )
    return pl.pallas_call(
        functools.partial(_combine_kernel, tr=tr, first=first, sub=_tile(tr, 128, 8)),
        grid_spec=grid_spec,
        out_shape=jax.ShapeDtypeStruct((n, d), F32),
        compiler_params=_cparams(("arbitrary",)),
        name="combine",
    )(runs_per_tile, run_src, run_src, slots, gates, x1, g, ys)


def _routing_tables(counts_f, *, tme, nb):
    counts = counts_f[:, 0].astype(jnp.int32)
    nblk = jnp.where(counts > 0, (counts + RUN - 1 + tme - 1) // tme, 0)
    blk_end = jnp.cumsum(nblk)
    blk_start = blk_end - nblk
    n_used = blk_end[-1]
    b = jnp.arange(nb, dtype=jnp.int32)
    used = b < n_used
    blk_exp = jnp.minimum(jnp.sum((b[:, None] >= blk_end[None, :]).astype(jnp.int32), axis=1), N_EXPERTS - 1)
    last_exp = jnp.max(jnp.where(nblk > 0, jnp.arange(N_EXPERTS, dtype=jnp.int32), 0))
    blk_exp = jnp.where(used, blk_exp, last_exp).astype(jnp.int32)
    blk_src = jnp.minimum(b, n_used - 1).astype(jnp.int32)
    experts = jnp.arange(N_EXPERTS, dtype=jnp.int32)
    mine = (b[:, None] >= blk_start[None, :]) & (b[:, None] < blk_end[None, :])
    nvalid = jnp.sum(jnp.where(mine, counts[None, :] - (b[:, None] - blk_start[None, :]) * tme, 0), axis=1)
    nvalid = jnp.clip(nvalid, 0, tme).astype(jnp.int32)
    later = (experts[None, :] > experts[:, None]) & (nblk[None, :] > 0)
    nxt_e = jnp.min(jnp.where(later, experts[None, :], N_EXPERTS), axis=1)
    nxt_e = jnp.where(nxt_e == N_EXPERTS, -1, nxt_e)
    blk_next = jnp.sum(jnp.where(blk_exp[:, None] == experts[None, :], nxt_e[None, :], 0), axis=1).astype(jnp.int32)
    partial = used & (nvalid < tme)
    fill_order = jnp.where(partial, 0, jnp.where(used, 2, 1))
    fill_blocks = jnp.argsort(fill_order, stable=True).astype(jnp.int32)
    n_fill = jnp.stack([jnp.sum(partial), jnp.sum(partial | ~used)]).astype(jnp.int32)
    return (blk_start, blk_exp, blk_src, nvalid, blk_next, n_used.reshape(1).astype(jnp.int32), fill_blocks,
            n_fill)


def _run_tables(tile_carry_f, counts_f, blk_start, *, td, tme):
    before = tile_carry_f[:, :, 0].astype(jnp.int32)
    counts = counts_f[:, 0].astype(jnp.int32)
    in_tile = jnp.concatenate([before[1:], counts[None, :]], axis=0) - before
    n_runs = (in_tile + RUN - 1) // RUN
    ends = jnp.cumsum(n_runs, axis=1)
    j = jnp.arange(_stage_slots(td) // RUN, dtype=jnp.int32)
    owner = jnp.sum((j[None, :, None] >= ends[:, None, :]).astype(jnp.int32), axis=2)
    base = blk_start[None, :] * tme + before - (ends - n_runs) * RUN
    mine = owner[:, :, None] == jnp.arange(N_EXPERTS, dtype=jnp.int32)[None, None, :]
    dst = jnp.sum(jnp.where(mine, base[:, None, :], 0), axis=2) + j[None, :] * RUN
    return ends[:, -1].astype(jnp.int32), dst[:, None, :].astype(jnp.int32)


def kernel(x_prompt, x_sample, cache_k, cache_v, state_conv, attn_norm_g, w_in, attn_sinks, conv_w, conv_b,
           conv_ln_g, conv_ln_b, w_out, ffn_norm_g, router_w, router_b, w1, b1, w2, b2, final_norm_g):
    depth = w_in.shape[0]
    assert depth == 1, "single-layer step"
    bp, sp, d = x_prompt.shape
    bs, ss, _ = x_sample.shape
    cw = conv_w.shape[2]
    np_, ns = bp * sp, bs * ss
    n_tok = np_ + ns
    assert sp % WINDOW == 0

    xp2 = x_prompt.reshape(np_, d)
    xs2 = x_sample.reshape(ns, d)
    w_in_bf = w_in[0].astype(BF16)
    w_out_bf = w_out[0].astype(BF16)
    g_attn = attn_norm_g[0].reshape(1, d)
    g_ffn = ffn_norm_g[0].reshape(1, d)
    sinks = attn_sinks[0]
    vec = lambda a: a.reshape(1, cw)

    tab_p = _rope_tables(jnp.arange(sp, dtype=jnp.int32))
    tms = _tile(ns, 512, max(ss, 16))
    tab_s = _rope_tables(PAST_LEN + (jnp.arange(tms, dtype=jnp.int32) % ss))
    qp, kp, vp, kfp, vfp, ap = _in_proj(xp2, g_attn, w_in_bf, tab_p, seq_period=sp, q_dtype=BF16, conv_width=cw)
    qs, _, _, kfs, vfs, as_ = _in_proj(xs2, g_attn, w_in_bf, tab_s, seq_period=None, q_dtype=F32, conv_width=cw)

    att_p = _attn_prompt(qp, kp, vp, sinks, batch=bp, seq=sp)
    ck = cache_k[0].reshape(bs, WINDOW, KV_WIDTH)
    cv_ = cache_v[0].reshape(bs, WINDOW, KV_WIDTH)
    att_s, nk_s, nv_s = _attn_sample(qs, kfs, vfs, ck, cv_, sinks, batch=bs, ts=ss)

    cv_p = _conv_prompt(ap, conv_w[0], vec(conv_b[0]), vec(conv_ln_g[0]), vec(conv_ln_b[0]), batch=bp, seq=sp)
    cv_s = _conv_sample(as_, state_conv[0], conv_w[0], vec(conv_b[0]), vec(conv_ln_g[0]), vec(conv_ln_b[0]),
                        batch=bs, ts=ss)

    n_exp = router_w.shape[2]
    assert n_exp == N_EXPERTS
    rwt_bf = router_w[0].T.astype(BF16)
    rbt = jnp.broadcast_to(router_b[0][:, None], (n_exp, LANES))
    zero_carry = jnp.zeros((n_exp, LANES), F32)
    tr = _tile(ns, 512, 16)
    assert np_ % tr == 0
    x1p, hnp, rt_p, tcar_p, cnt_p = _outproj_router(xp2, att_p, cv_p, w_out_bf, g_ffn, rwt_bf, rbt, zero_carry, tm=tr)
    x1s, hns, rt_s, tcar_s, cnt = _outproj_router(xs2, att_s, cv_s, w_out_bf, g_ffn, rwt_bf, rbt, cnt_p, tm=tr)

    tme = EXPERT_BLOCK_ROWS
    nb = -(-(n_tok * TOP_K + N_EXPERTS * (tme - 1 + RUN - 1)) // tme)
    blk_start, blk_exp, blk_src, blk_nvalid, blk_next, n_used, fill_blocks, n_fill = _routing_tables(
        cnt, tme=tme, nb=nb)
    runs_per_tile, run_dst = _run_tables(jnp.concatenate([tcar_p, tcar_s], axis=0), cnt, blk_start, td=tr, tme=tme)
    xs_sorted = _dispatch(hnp, hns, jnp.concatenate([rt_p, rt_s], axis=1), fill_blocks, n_fill, runs_per_tile,
                          run_dst, td=tr, nb=nb, tme=tme)
    ys = _experts(xs_sorted, w1[0], b1[0], w2[0], b2[0], blk_exp, blk_src, blk_nvalid, blk_next, n_used, tme=tme)
    g_fin = final_norm_g.reshape(1, d)
    tiles_p = np_ // tr
    copy_slots = lambda rt: _by_tile(rt[0:TOP_K].astype(jnp.int32) * SUBLANES, tr)
    copy_gates = lambda rt: _by_tile(rt[TOP_K:2 * TOP_K], tr)
    y_p = _combine(x1p, copy_slots(rt_p), copy_gates(rt_p), runs_per_tile, run_dst[:tiles_p], ys, g_fin,
                   tr=tr, first=0)
    y_s = _combine(x1s, copy_slots(rt_s), copy_gates(rt_s), runs_per_tile, run_dst[tiles_p:], ys, g_fin,
                   tr=tr, first=tiles_p)

    kv5 = lambda t, bb: t.reshape(bb, -1, KV_WIDTH)[:, -WINDOW:].reshape(bb, WINDOW, N_KV_HEADS, HEAD_DIM)
    new_k_p = kv5(kfp, bp)[None]
    new_v_p = kv5(vfp, bp)[None]
    ctx = CONV_K - 1
    new_c_p = ap.reshape(bp, sp, cw)[:, -ctx:][None]
    new_c_s = jnp.concatenate([state_conv[0], as_.reshape(bs, ss, cw)], axis=1)[:, -ctx:][None]
    return (y_p.reshape(bp, sp, d), y_s.reshape(bs, ss, d), new_k_p, new_v_p, new_c_p,
            kv5(nk_s, bs)[None], kv5(nv_s, bs)[None], new_c_s)
```

```python
import functools

import jax
import jax.numpy as jnp
from jax import lax
from jax.experimental import pallas as pl
from jax.experimental.pallas import tpu as pltpu

F32 = jnp.float32
BF16 = jnp.bfloat16

HEAD_DIM = 64
N_Q_HEADS = 8
N_KV_HEADS = 2
WINDOW = 128
ROPE_THETA = 500000.0
ROPE_DIM = 16
CONV_K = 31
N_EXPERTS = 32
TOP_K = 4
SWIGLU_LIMIT = 7.0
SWIGLU_ALPHA = 1.702
RMS_EPS = 1e-5
LN_EPS = 1e-5
PAST_LEN = 16384

LANES = 128
SUBLANES = 8
CONV_HALO = 32
VMEM_LIMIT = 56 * 1024 * 1024
EXPERT_BLOCK_ROWS = 512
RUN = 8
FILL_ROWS = 64
ROUTE_T_ROWS = 2 * TOP_K
ISSUE_UNROLL = 8

ATTN_WIDTH = N_Q_HEADS * HEAD_DIM
KV_WIDTH = N_KV_HEADS * HEAD_DIM


def _tile(n, pref, mult=8):
    t = min(pref, n)
    while t > 0 and (n % t or t % mult):
        t -= 1
    assert t > 0, (n, pref, mult)
    return t


def _cparams(sem):
    return pltpu.CompilerParams(dimension_semantics=sem, vmem_limit_bytes=VMEM_LIMIT)


def _sigmoid(x):
    return 1.0 / (1.0 + jnp.exp(-x))


def _rope_tables(pos):
    half = ROPE_DIM // 2
    inv_freq = jnp.power(jnp.float32(ROPE_THETA), -jnp.arange(half, dtype=F32) * 2.0 / ROPE_DIM)
    l64 = jnp.arange(LANES) % HEAD_DIM
    assert HEAD_DIM % half == 0
    ang = pos.astype(F32)[:, None] * jnp.tile(inv_freq, LANES // half)[None, :]
    cos_l, sin_l = jnp.cos(ang), jnp.sin(ang)
    c = jnp.where(l64 < ROPE_DIM, cos_l, 1.0)
    s1 = jnp.where(l64 < half, -sin_l, 0.0)
    s2 = jnp.where((l64 >= half) & (l64 < ROPE_DIM), sin_l, 0.0)
    return c.astype(F32), s1.astype(F32), s2.astype(F32)


def _inproj_kernel(x_ref, g_ref, w_ref, c_ref, s1_ref, s2_ref,
                   q_ref, k_ref, v_ref, kf_ref, vf_ref, a_ref, *, conv_width):
    x = x_ref[...]
    ms = jnp.mean(x * x, axis=-1, keepdims=True)
    h = (x * lax.rsqrt(ms + RMS_EPS) * g_ref[...]).astype(BF16)
    z = jnp.dot(h, w_ref[...], preferred_element_type=F32)
    c, s1, s2 = c_ref[...], s1_ref[...], s2_ref[...]
    half = ROPE_DIM // 2

    def rot(t):
        return t * c + pltpu.roll(t, LANES - half, 1) * s1 + pltpu.roll(t, half, 1) * s2

    scale = HEAD_DIM ** -0.5
    for j in range(ATTN_WIDTH // LANES):
        q_ref[:, j * LANES:(j + 1) * LANES] = (rot(z[:, j * LANES:(j + 1) * LANES]) * scale).astype(q_ref.dtype)
    k0 = ATTN_WIDTH
    kr = rot(z[:, k0:k0 + KV_WIDTH])
    k_ref[...] = kr.astype(BF16)
    kf_ref[...] = kr
    v0 = k0 + KV_WIDTH
    vv = z[:, v0:v0 + KV_WIDTH]
    v_ref[...] = vv.astype(BF16)
    vf_ref[...] = vv
    u0 = v0 + KV_WIDTH
    g0 = u0 + conv_width
    a_ref[...] = z[:, u0:g0] * _sigmoid(z[:, g0:g0 + conv_width])


def _in_proj(x2, g, w_bf, tables, *, seq_period, q_dtype, conv_width):
    n, d = x2.shape
    in_w = w_bf.shape[1]
    if seq_period is None:
        tm = tables[0].shape[0]
        tmap = lambda i: (0, 0)
    else:
        tm = _tile(seq_period, 1024, 16)
        per = seq_period // tm
        tmap = lambda i: (i % per, 0)
    assert n % tm == 0
    row = lambda w: pl.BlockSpec((tm, w), lambda i: (i, 0))
    tab = pl.BlockSpec((tm, LANES), tmap)
    return pl.pallas_call(
        functools.partial(_inproj_kernel, conv_width=conv_width),
        grid=(n // tm,),
        in_specs=[row(d), pl.BlockSpec((1, d), lambda i: (0, 0)),
                  pl.BlockSpec((d, in_w), lambda i: (0, 0)), tab, tab, tab],
        out_specs=[row(ATTN_WIDTH), row(KV_WIDTH), row(KV_WIDTH), row(KV_WIDTH), row(KV_WIDTH),
                   row(conv_width)],
        out_shape=[jax.ShapeDtypeStruct((n, ATTN_WIDTH), q_dtype),
                   jax.ShapeDtypeStruct((n, KV_WIDTH), BF16),
                   jax.ShapeDtypeStruct((n, KV_WIDTH), BF16),
                   jax.ShapeDtypeStruct((n, KV_WIDTH), F32),
                   jax.ShapeDtypeStruct((n, KV_WIDTH), F32),
                   jax.ShapeDtypeStruct((n, conv_width), F32)],
        compiler_params=_cparams(("parallel",)),
        name="in_proj",
    )(x2, g, w_bf, *tables)


def _dup_head(t, h):
    sw = pltpu.roll(t, HEAD_DIM, 1)
    low = lax.broadcasted_iota(jnp.int32, t.shape, 1) < HEAD_DIM
    return jnp.where(low, t, sw) if h == 0 else jnp.where(low, sw, t)


def _nt_dot(a, b):
    return lax.dot_general(a, b, (((1,), (1,)), ((), ())), preferred_element_type=F32)


def _attn_prompt_kernel(sink_ref, q_ref, kc_ref, kp_ref, vc_ref, vp_ref, o_ref, *, qb):
    j = pl.program_id(1)
    w = WINDOW
    k_all = jnp.concatenate([kp_ref[...], kc_ref[...]], axis=0).astype(F32)
    v_all = jnp.concatenate([vp_ref[...], vc_ref[...]], axis=0).astype(F32)
    r = lax.broadcasted_iota(jnp.int32, (w, 2 * w), 0)
    kk = lax.broadcasted_iota(jnp.int32, (w, 2 * w), 1)
    band = (kk > r) & (kk <= r + w)
    low = lax.broadcasted_iota(jnp.int32, (w, LANES), 1) < HEAD_DIM
    zero = jnp.zeros((w, LANES), BF16)
    group = N_Q_HEADS // N_KV_HEADS
    for h in range(N_KV_HEADS):
        kd_all = _dup_head(k_all, h).astype(BF16)
        vd_all = _dup_head(v_all, h).astype(BF16)
        for sub in range(qb):
            valid = band & ((kk >= w) | (j > 0)) if sub == 0 else band
            kd = kd_all[sub * w:(sub + 2) * w, :]
            vd = vd_all[sub * w:(sub + 2) * w, :]
            for jj in range(group // 2):
                col = (h * group // 2 + jj) * LANES
                qv = q_ref[sub * w:(sub + 1) * w, col:col + LANES]
                halves = []
                for half in range(2):
                    head = h * group + jj * 2 + half
                    qm = jnp.where(low if half == 0 else ~low, qv, zero)
                    s = jnp.where(valid, _nt_dot(qm, kd), -jnp.inf)
                    sink = sink_ref[head]
                    m = jnp.maximum(jnp.max(s, axis=-1, keepdims=True), sink)
                    p = jnp.exp(s - m)
                    den = jnp.sum(p, axis=-1, keepdims=True) + jnp.exp(sink - m)
                    o = jnp.dot(p.astype(BF16), vd, preferred_element_type=F32)
                    halves.append(o / den)
                o_ref[sub * w:(sub + 1) * w, col:col + LANES] = (
                    jnp.where(low, halves[0], halves[1]).astype(o_ref.dtype))


def _attn_prompt(q, k, v, sinks, *, batch, seq):
    nb = seq // WINDOW
    qb = next(c for c in (4, 2, 1) if nb % c == 0)
    steps = nb // qb
    cur = lambda b, j: (b * steps + j, 0)
    prev = lambda b, j: (b * nb + jnp.maximum(j * qb - 1, 0), 0)
    return pl.pallas_call(
        functools.partial(_attn_prompt_kernel, qb=qb),
        grid=(batch, steps),
        in_specs=[pl.BlockSpec(memory_space=pltpu.SMEM),
                  pl.BlockSpec((qb * WINDOW, ATTN_WIDTH), cur),
                  pl.BlockSpec((qb * WINDOW, KV_WIDTH), cur), pl.BlockSpec((WINDOW, KV_WIDTH), prev),
                  pl.BlockSpec((qb * WINDOW, KV_WIDTH), cur), pl.BlockSpec((WINDOW, KV_WIDTH), prev)],
        out_specs=pl.BlockSpec((qb * WINDOW, ATTN_WIDTH), cur),
        out_shape=jax.ShapeDtypeStruct((batch * seq, ATTN_WIDTH), BF16),
        compiler_params=_cparams(("parallel", "parallel")),
        name="attn_prompt",
    )(sinks, q, k, k, v, v)


def _attn_sample_kernel(sink_ref, q_ref, kn_ref, vn_ref, ck_ref, cv_ref, o_ref, nk_ref, nv_ref, *, gb, ts):
    w = WINDOW
    group = N_Q_HEADS // N_KV_HEADS
    rows = group * ts
    low = lax.broadcasted_iota(jnp.int32, (ts, LANES), 1) < HEAD_DIM
    pad = jnp.zeros((ts, LANES), F32)
    s_c, s_n, v_dup = [], [], []
    for b in range(gb):
        kc, vc = ck_ref[b], cv_ref[b]
        kn, vn = kn_ref[b * ts:(b + 1) * ts, :], vn_ref[b * ts:(b + 1) * ts, :]
        nk_ref[b, 0:w - ts, :] = kc[ts:, :]
        nk_ref[b, w - ts:, :] = kn
        nv_ref[b, 0:w - ts, :] = vc[ts:, :]
        nv_ref[b, w - ts:, :] = vn
        knp = jnp.concatenate([kn, pad], axis=0)
        vnp = jnp.concatenate([vn, pad], axis=0)
        qb = q_ref[b * ts:(b + 1) * ts, :]
        for h in range(N_KV_HEADS):
            parts = []
            for jj in range(group // 2):
                col = (h * group // 2 + jj) * LANES
                qv = qb[:, col:col + LANES]
                parts += [jnp.where(low, qv, 0.0), jnp.where(low, 0.0, qv)]
            lhs = jnp.concatenate(parts, axis=0).astype(BF16)
            s_c.append(_nt_dot(lhs, _dup_head(kc, h).astype(BF16)))
            s_n.append(_nt_dot(lhs, _dup_head(knp, h).astype(BF16)))
            v_dup.append((_dup_head(vc, h).astype(BF16), _dup_head(vnp, h).astype(BF16)))
    s_c = jnp.concatenate(s_c, axis=0)
    s_n = jnp.concatenate(s_n, axis=0)
    n_rows = s_c.shape[0]
    ridx = lax.broadcasted_iota(jnp.int32, (n_rows, 1), 0)
    t_row = ridx % ts
    head_row = (ridx // ts) % N_Q_HEADS
    sink = jnp.zeros((n_rows, 1), F32)
    for hd in range(N_Q_HEADS):
        sink = jnp.where(head_row == hd, sink_ref[hd], sink)
    c_idx = lax.broadcasted_iota(jnp.int32, (n_rows, w), 1)
    n_idx = lax.broadcasted_iota(jnp.int32, (n_rows, 2 * ts), 1)
    s_c = jnp.where(c_idx > t_row, s_c, -jnp.inf)
    s_n = jnp.where(n_idx <= t_row, s_n, -jnp.inf)
    m = jnp.maximum(jnp.maximum(jnp.max(s_c, axis=-1, keepdims=True), jnp.max(s_n, axis=-1, keepdims=True)), sink)
    p_c = jnp.exp(s_c - m)
    p_n = jnp.exp(s_n - m)
    den = jnp.sum(p_c, axis=-1, keepdims=True) + jnp.sum(p_n, axis=-1, keepdims=True) + jnp.exp(sink - m)
    p_c = p_c.astype(BF16)
    p_n = p_n.astype(BF16)
    outs = []
    for b in range(gb):
        cols = []
        for h in range(N_KV_HEADS):
            ci = b * N_KV_HEADS + h
            sl = slice(ci * rows, (ci + 1) * rows)
            vdc, vdn = v_dup[ci]
            o = (jnp.dot(p_c[sl], vdc, preferred_element_type=F32)
                 + jnp.dot(p_n[sl], vdn, preferred_element_type=F32)) / den[sl]
            for jj in range(group // 2):
                lo_part = o[(2 * jj) * ts:(2 * jj + 1) * ts, :]
                hi_part = o[(2 * jj + 1) * ts:(2 * jj + 2) * ts, :]
                cols.append(jnp.where(low, lo_part, hi_part))
        outs.append(jnp.concatenate(cols, axis=1))
    o_ref[...] = jnp.concatenate(outs, axis=0).astype(o_ref.dtype)


def _attn_sample(q, kf, vf, cache_k, cache_v, sinks, *, batch, ts):
    assert ts % 8 == 0 and ts <= WINDOW
    gb = _tile(batch, 8, 2)
    tok = lambda w: pl.BlockSpec((gb * ts, w), lambda i: (i, 0))
    cache = pl.BlockSpec((gb, WINDOW, KV_WIDTH), lambda i: (i, 0, 0))
    cshape = jax.ShapeDtypeStruct((batch, WINDOW, KV_WIDTH), F32)
    return pl.pallas_call(
        functools.partial(_attn_sample_kernel, gb=gb, ts=ts),
        grid=(batch // gb,),
        in_specs=[pl.BlockSpec(memory_space=pltpu.SMEM), tok(ATTN_WIDTH), tok(KV_WIDTH), tok(KV_WIDTH),
                  cache, cache],
        out_specs=[tok(ATTN_WIDTH), cache, cache],
        out_shape=[jax.ShapeDtypeStruct((batch * ts, ATTN_WIDTH), BF16), cshape, cshape],
        compiler_params=_cparams(("parallel",)),
        name="attn_sample",
    )(sinks, q, kf, vf, cache_k, cache_v)


def _ln_swish(acc, b, lg, lb):
    y = acc + b
    mu = jnp.mean(y, axis=-1, keepdims=True)
    yc = y - mu
    var = jnp.mean(yc * yc, axis=-1, keepdims=True)
    yn = yc * lax.rsqrt(var + LN_EPS) * lg + lb
    return yn * _sigmoid(yn)


def _conv_prompt_kernel(a_ref, ap_ref, w_ref, b_ref, lg_ref, lb_ref, o_ref, win_ref, *, tt, rc):
    j = pl.program_id(1)
    n = CONV_HALO + tt
    win = jnp.concatenate([jnp.where(j > 0, ap_ref[...], 0.0), a_ref[...]], axis=0)
    win_ref[0] = win
    for r in range(1, SUBLANES):
        win_ref[r] = pltpu.roll(win, n - r, 0)
    off = CONV_HALO - (CONV_K - 1)
    b, lg, lb = b_ref[...], lg_ref[...], lb_ref[...]
    for c in range(tt // rc):
        acc = jnp.zeros((rc, a_ref.shape[1]), F32)
        for k in range(CONV_K):
            s = off + k
            base = c * rc + (s // SUBLANES) * SUBLANES
            wk = jnp.concatenate([w_ref[k]] * (rc // SUBLANES), axis=0)
            acc = acc + wk * win_ref[s % SUBLANES, base:base + rc, :]
        o_ref[c * rc:(c + 1) * rc, :] = _ln_swish(acc, b, lg, lb).astype(o_ref.dtype)


def _conv_prompt(a, w, b, lg, lb, *, batch, seq):
    cw = a.shape[1]
    tt = _tile(seq, 512, CONV_HALO)
    rc = _tile(tt, 32, 16)
    nt = seq // tt
    per = tt // CONV_HALO
    cur = lambda bb, j: (bb * nt + j, 0)
    prev = lambda bb, j: (jnp.maximum((bb * nt + j) * per - 1, 0), 0)
    vec = pl.BlockSpec((1, cw), lambda bb, j: (0, 0))
    return pl.pallas_call(
        functools.partial(_conv_prompt_kernel, tt=tt, rc=rc),
        grid=(batch, nt),
        in_specs=[pl.BlockSpec((tt, cw), cur), pl.BlockSpec((CONV_HALO, cw), prev),
                  pl.BlockSpec((CONV_K, SUBLANES, cw), lambda bb, j: (0, 0, 0)), vec, vec, vec],
        out_specs=pl.BlockSpec((tt, cw), cur),
        out_shape=jax.ShapeDtypeStruct((batch * seq, cw), BF16),
        scratch_shapes=[pltpu.VMEM((SUBLANES, CONV_HALO + tt, cw), F32)],
        compiler_params=_cparams(("parallel", "parallel")),
        name="conv_prompt",
    )(a, a, jnp.broadcast_to(w[:, None, :], (CONV_K, SUBLANES, cw)), b, lg, lb)


def _conv_sample_kernel(a_ref, st_ref, w_ref, b_ref, lg_ref, lb_ref, o_ref, win_ref, *, gb, ts):
    ctx = CONV_K - 1
    b, lg, lb = b_ref[...], lg_ref[...], lb_ref[...]
    for bb in range(gb):
        win_ref[bb, 0:ctx, :] = st_ref[bb]
        win_ref[bb, ctx:ctx + ts, :] = a_ref[bb * ts:(bb + 1) * ts, :]
    outs = []
    for bb in range(gb):
        acc = jnp.zeros((ts, a_ref.shape[1]), F32)
        for k in range(CONV_K):
            wk = jnp.concatenate([w_ref[k]] * (ts // SUBLANES), axis=0)
            acc = acc + wk * win_ref[bb, k:k + ts, :]
        outs.append(_ln_swish(acc, b, lg, lb))
    o_ref[...] = jnp.concatenate(outs, axis=0).astype(o_ref.dtype)


def _conv_sample(a, state, w, b, lg, lb, *, batch, ts):
    cw = a.shape[1]
    ctx = CONV_K - 1
    gb = _tile(batch, 8, 2)
    vec = pl.BlockSpec((1, cw), lambda i: (0, 0))
    return pl.pallas_call(
        functools.partial(_conv_sample_kernel, gb=gb, ts=ts),
        grid=(batch // gb,),
        in_specs=[pl.BlockSpec((gb * ts, cw), lambda i: (i, 0)),
                  pl.BlockSpec((gb, ctx, cw), lambda i: (i, 0, 0)),
                  pl.BlockSpec((CONV_K, SUBLANES, cw), lambda i: (0, 0, 0)), vec, vec, vec],
        out_specs=pl.BlockSpec((gb * ts, cw), lambda i: (i, 0)),
        out_shape=jax.ShapeDtypeStruct((batch * ts, cw), BF16),
        scratch_shapes=[pltpu.VMEM((gb, ctx + ts + 2, cw), F32)],
        compiler_params=_cparams(("parallel",)),
        name="conv_sample",
    )(a, state, jnp.broadcast_to(w[:, None, :], (CONV_K, SUBLANES, cw)), b, lg, lb)


def _outproj_router_kernel(x_ref, att_ref, cv_ref, wo_ref, g_ref, rw_ref, rb_ref, cin_ref,
                           x1_ref, hn_ref, rt_ref, tcar_ref, cnt_ref, carry_ref):
    i = pl.program_id(0)

    @pl.when(i == 0)
    def _():
        carry_ref[...] = cin_ref[...]

    tm = x_ref.shape[0]
    aw = att_ref.shape[1]
    n_exp = rw_ref.shape[0]
    r = lax.broadcasted_iota(jnp.int32, (tm, tm), 0)
    c = lax.broadcasted_iota(jnp.int32, (tm, tm), 1)
    before = (r < c).astype(BF16)
    eid = lax.broadcasted_iota(jnp.int32, (n_exp, tm), 0)
    carry = carry_ref[...][:, 0:1]
    tcar_ref[0] = carry_ref[...]
    mix = (jnp.dot(att_ref[...], wo_ref[0:aw, :], preferred_element_type=F32)
           + jnp.dot(cv_ref[...], wo_ref[aw:, :], preferred_element_type=F32))
    x1 = x_ref[...] + mix
    x1_ref[...] = x1
    ms = jnp.mean(x1 * x1, axis=-1, keepdims=True)
    hn = (x1 * lax.rsqrt(ms + RMS_EPS) * g_ref[...]).astype(BF16)
    hn_ref[...] = hn
    logits = _nt_dot(rw_ref[...], hn) + rb_ref[...][:, 0:1]
    onehot = jnp.zeros((n_exp, tm), F32)
    vals, idxs = [], []
    for _ in range(TOP_K):
        m = jnp.max(logits, axis=0, keepdims=True)
        idx = jnp.min(jnp.where(logits == m, eid, n_exp), axis=0, keepdims=True)
        sel = eid == idx
        onehot = onehot + sel.astype(F32)
        logits = jnp.where(sel, -jnp.inf, logits)
        vals.append(m)
        idxs.append(idx)
    es = [jnp.exp(v - vals[0]) for v in vals]
    den = es[0] + es[1] + es[2] + es[3]
    local = jnp.dot(onehot.astype(BF16), before, preferred_element_type=F32)
    count = jnp.sum(onehot, axis=1, keepdims=True)
    padded = jnp.ceil(count * (1.0 / RUN)) * RUN
    er = lax.broadcasted_iota(jnp.int32, (n_exp, n_exp), 0)
    ec = lax.broadcasted_iota(jnp.int32, (n_exp, n_exp), 1)
    start = jnp.dot((ec < er).astype(BF16), jnp.broadcast_to(padded, (n_exp, LANES)).astype(BF16),
                    preferred_element_type=F32)[:, 0:1]
    slot_of = local + start
    slots = [jnp.sum(jnp.where(eid == idxs[k], slot_of, 0.0), axis=0, keepdims=True) for k in range(TOP_K)]
    rt_ref[...] = jnp.concatenate(slots + [e / den for e in es], axis=0)
    carry = carry + count
    carry_ref[...] = jnp.broadcast_to(carry, carry_ref.shape)
    cnt_ref[...] = jnp.broadcast_to(carry, cnt_ref.shape)


def _store_slabs(ref, val, rows):
    for j in range(val.shape[1] // LANES):
        ref[pl.ds(j, rows, stride=SUBLANES), :] = val[:, j * LANES:(j + 1) * LANES]


def _load_slabs(ref, rows, dtype):
    return jnp.concatenate([ref[pl.ds(j, rows, stride=SUBLANES), :].astype(dtype) for j in range(SUBLANES)],
                           axis=1)


def _outproj_router(x2, att, cv, wo_bf, g, rwt_bf, rbt, carry_in, *, tm):
    n, d = x2.shape
    assert d == SUBLANES * LANES and n % tm == 0
    n_exp = rwt_bf.shape[0]
    row = lambda w: pl.BlockSpec((tm, w), lambda i: (i, 0))
    full = lambda a: pl.BlockSpec(a.shape, lambda i: (0,) * a.ndim)
    return pl.pallas_call(
        _outproj_router_kernel,
        grid=(n // tm,),
        in_specs=[row(d), row(att.shape[1]), row(cv.shape[1]), full(wo_bf), full(g), full(rwt_bf), full(rbt),
                  full(carry_in)],
        out_specs=[row(d), row(d),
                   pl.BlockSpec((ROUTE_T_ROWS, tm), lambda i: (0, i)),
                   pl.BlockSpec((1, n_exp, LANES), lambda i: (i, 0, 0)),
                   pl.BlockSpec((n_exp, LANES), lambda i: (0, 0))],
        out_shape=[jax.ShapeDtypeStruct((n, d), F32), jax.ShapeDtypeStruct((n, d), BF16),
                   jax.ShapeDtypeStruct((ROUTE_T_ROWS, n), F32),
                   jax.ShapeDtypeStruct((n // tm, n_exp, LANES), F32),
                   jax.ShapeDtypeStruct((n_exp, LANES), F32)],
        scratch_shapes=[pltpu.VMEM((n_exp, LANES), F32)],
        compiler_params=_cparams(("arbitrary",)),
        name="outproj_router",
    )(x2, att, cv, wo_bf, g, rwt_bf, rbt, carry_in)


def _dispatch_kernel(fill_ref, nfill_ref, tot_ref, dst_ref, rt_ref, hn_ref, hn2_ref,
                     xs_ref, zero_ref, stage_ref, sem, zsem, *, td, n_first, chunk):
    i = pl.program_id(0)
    last = pl.num_programs(0) - 1
    par = i % 2
    n_slots = stage_ref.shape[1] // SUBLANES

    def fill(f, s):
        row = pl.multiple_of(fill_ref[f] * (FILL_ROWS * SUBLANES), FILL_ROWS * SUBLANES)
        return pltpu.make_async_copy(zero_ref, xs_ref.at[pl.ds(row, FILL_ROWS * SUBLANES)], zsem.at[s])

    def fills(lo, hi, s, act):
        def body(f, carry):
            act(fill(f, s))
            return carry

        lax.fori_loop(lo, hi, body, 0)

    @pl.when(i == 0)
    def _():
        zero_ref[...] = jnp.zeros(zero_ref.shape, F32)
        fills(0, nfill_ref[0], 0, lambda c: c.start())
        fills(nfill_ref[0], nfill_ref[1], 1, lambda c: c.start())
        fills(0, nfill_ref[0], 0, lambda c: c.wait())

    def group_by_expert(tok_ref):
        hn = tok_ref[...]
        slots = rt_ref[0:TOP_K, :].astype(jnp.int32)
        for c in range(n_slots // chunk):
            s = lax.broadcasted_iota(jnp.int32, (chunk, td), 0) + c * chunk
            hit = s == slots[0:1, :]
            for k in range(1, TOP_K):
                hit = hit | (s == slots[k:k + 1, :])
            rows = jnp.dot(jnp.where(hit, 1.0, 0.0).astype(BF16), hn, preferred_element_type=F32)
            _store_slabs(stage_ref.at[par, pl.ds(c * chunk * SUBLANES, chunk * SUBLANES)], rows, chunk)

    def run_copy(buf, src_slot, dst_row, runs=1):
        src = pl.multiple_of(src_slot * SUBLANES, RUN * SUBLANES)
        dst = pl.multiple_of(dst_row * SUBLANES, SUBLANES)
        return pltpu.make_async_copy(stage_ref.at[buf, pl.ds(src, runs * RUN * SUBLANES)],
                                     xs_ref.at[pl.ds(dst, runs * RUN * SUBLANES)], sem)

    def wait_runs(tile):
        def wait_many(q, carry):
            run_copy(0, 0, 0, ISSUE_UNROLL).wait()
            return carry

        def wait_one(j, carry):
            run_copy(0, 0, 0).wait()
            return carry

        n_many = tot_ref[tile] // ISSUE_UNROLL
        lax.fori_loop(0, n_many, wait_many, 0)
        lax.fori_loop(n_many * ISSUE_UNROLL, tot_ref[tile], wait_one, 0)

    @pl.when(i >= 2)
    def _():
        wait_runs(i - 2)

    @pl.when(i >= 1)
    def _():
        def issue(j, queue):
            run_copy(1 - par, j * RUN, dst_ref[0, 0, j]).start(priority=queue)

        def issue_many(q, carry):
            for u in range(ISSUE_UNROLL):
                issue(q * ISSUE_UNROLL + u, u % 2)
            return carry

        def issue_one(j, carry):
            issue(j, 0)
            return carry

        n_full = tot_ref[i - 1] // ISSUE_UNROLL
        lax.fori_loop(0, n_full, issue_many, 0)
        lax.fori_loop(n_full * ISSUE_UNROLL, tot_ref[i - 1], issue_one, 0)

    @pl.when(i < n_first)
    def _():
        group_by_expert(hn_ref)

    @pl.when((i >= n_first) & (i < last))
    def _():
        group_by_expert(hn2_ref)

    @pl.when(i == last)
    def _():
        wait_runs(i - 1)
        fills(nfill_ref[0], nfill_ref[1], 1, lambda c: c.wait())


def _stage_slots(td):
    return td * TOP_K + N_EXPERTS * RUN


def _by_tile(field, td):
    n = field.shape[1]
    return field.reshape(TOP_K, n // td, td).transpose(1, 0, 2).reshape(n // td, 1, TOP_K * td)


def _dispatch(hn_a, hn_b, route_t, fill_pieces, n_fill, runs_per_tile, run_dst, *, td, nb, tme):
    na, d = hn_a.shape
    assert na % td == 0 and hn_b.shape[0] % td == 0 and td % RUN == 0
    n_first = na // td
    n_second = hn_b.shape[0] // td
    tiles = n_first + n_second
    n_slots = _stage_slots(td)
    chunk = _tile(n_slots, 768, SUBLANES)
    tok = lambda m: pl.BlockSpec((td, d), m)
    grid_spec = pltpu.PrefetchScalarGridSpec(
        num_scalar_prefetch=3,
        grid=(tiles + 1,),
        in_specs=[pl.BlockSpec((1, 1, n_slots // RUN), lambda i, *_: (jnp.maximum(i - 1, 0), 0, 0),
                               memory_space=pltpu.SMEM),
                  pl.BlockSpec((ROUTE_T_ROWS, td), lambda i, *_: (0, jnp.minimum(i, tiles - 1))),
                  tok(lambda i, *_: (jnp.minimum(i, n_first - 1), 0)),
                  tok(lambda i, *_: (jnp.clip(i - n_first, 0, n_second - 1), 0))],
        out_specs=pl.BlockSpec(memory_space=pl.ANY),
        scratch_shapes=[pltpu.VMEM((FILL_ROWS * SUBLANES, LANES), F32),
                        pltpu.VMEM((2, n_slots * SUBLANES, LANES), F32),
                        pltpu.SemaphoreType.DMA(()), pltpu.SemaphoreType.DMA((2,))],
    )
    return pl.pallas_call(
        functools.partial(_dispatch_kernel, td=td, n_first=n_first, chunk=chunk),
        grid_spec=grid_spec,
        out_shape=jax.ShapeDtypeStruct((nb * tme * SUBLANES, LANES), F32),
        compiler_params=_cparams(("arbitrary",)),
        name="dispatch",
    )(fill_pieces, n_fill, runs_per_tile, run_dst, route_t, hn_a, hn_b)


def _experts_kernel(be_ref, bsrc_ref, nv_ref, nx_ref, nu_ref, x_ref, w1_hbm, b1_ref, w2_hbm, b2_ref,
                    y_ref, w1f_ref, w2f_ref, w1b_ref, w2b_ref, par_ref, wsem, *, tme):
    i = pl.program_id(0)
    e = be_ref[i]
    e_prev = be_ref[jnp.maximum(i - 1, 0)]
    d_ff = w2f_ref.shape[1]
    half = tme // 2

    def fetch(expert, s):
        return (pltpu.make_async_copy(w1_hbm.at[expert], w1f_ref.at[s], wsem.at[0, s]),
                pltpu.make_async_copy(w2_hbm.at[expert], w2f_ref.at[s], wsem.at[1, s]))

    @pl.when(i == 0)
    def _():
        par_ref[0] = 0
        for c in fetch(e, 0):
            c.start()

    @pl.when((i == 0) | (e != e_prev))
    def _():
        s = par_ref[0]
        for c in fetch(e, s):
            c.wait()
        w1b_ref[...] = w1f_ref[s].astype(BF16)
        w2b_ref[...] = w2f_ref[s].astype(BF16)
        nxt = nx_ref[i]

        @pl.when(nxt >= 0)
        def _():
            for c in fetch(nxt, 1 - s):
                c.start()

        par_ref[0] = 1 - s

    def ffn(x):
        h = jnp.dot(x, w1b_ref[...], preferred_element_type=F32) + b1_ref[0]
        x_glu = jnp.minimum(h[:, :d_ff], SWIGLU_LIMIT)
        x_lin = jnp.clip(h[:, d_ff:], -SWIGLU_LIMIT, SWIGLU_LIMIT)
        act = x_glu * _sigmoid(SWIGLU_ALPHA * x_glu) * (x_lin + 1.0)
        return jnp.dot(act.astype(BF16), w2b_ref[...], preferred_element_type=F32) + b2_ref[0]

    used = i < nu_ref[0]
    nv = nv_ref[i]

    @pl.when(used & (nv > half))
    def _():
        _store_slabs(y_ref, ffn(_load_slabs(x_ref, tme, BF16)), tme)

    @pl.when(used & (nv <= half))
    def _():
        rows = half * SUBLANES
        _store_slabs(y_ref.at[pl.ds(0, rows)], ffn(_load_slabs(x_ref.at[pl.ds(0, rows)], half, BF16)), half)
        y_ref[pl.ds(rows, rows), :] = jnp.zeros((rows, LANES), F32)

    @pl.when(i == nu_ref[0])
    def _():
        y_ref[...] = jnp.zeros(y_ref.shape, F32)


def _experts(xs, w1, b1, w2, b2, blk_exp, blk_src, blk_nvalid, blk_next, n_used, *, tme):
    n_exp, d, h2 = w1.shape
    d_ff = w2.shape[1]
    nb = xs.shape[0] // (tme * SUBLANES)
    slab = lambda m: pl.BlockSpec((tme * SUBLANES, LANES), m)
    grid_spec = pltpu.PrefetchScalarGridSpec(
        num_scalar_prefetch=5,
        grid=(nb,),
        in_specs=[slab(lambda i, be, bs, nv, nx, nu: (bs[i], 0)),
                  pl.BlockSpec(memory_space=pl.ANY),
                  pl.BlockSpec((1, 1, h2), lambda i, be, bs, nv, nx, nu: (be[i], 0, 0)),
                  pl.BlockSpec(memory_space=pl.ANY),
                  pl.BlockSpec((1, 1, d), lambda i, be, bs, nv, nx, nu: (be[i], 0, 0))],
        out_specs=slab(lambda i, be, bs, nv, nx, nu: (jnp.minimum(i, nu[0]), 0)),
        scratch_shapes=[pltpu.VMEM((2, d, h2), F32), pltpu.VMEM((2, d_ff, d), F32),
                        pltpu.VMEM((d, h2), BF16), pltpu.VMEM((d_ff, d), BF16),
                        pltpu.SMEM((1,), jnp.int32), pltpu.SemaphoreType.DMA((2, 2))],
    )
    return pl.pallas_call(
        functools.partial(_experts_kernel, tme=tme),
        grid_spec=grid_spec,
        out_shape=jax.ShapeDtypeStruct(xs.shape, F32),
        input_output_aliases={5: 0},
        compiler_params=_cparams(("arbitrary",)),
        name="experts",
    )(blk_exp, blk_src, blk_nvalid, blk_next, n_used, xs, w1, b1.reshape(n_exp, 1, h2), w2,
      b2.reshape(n_exp, 1, d))


def _combine_kernel(tot_ref, rcur_ref, rnext_ref, slot_ref, gate_ref, x1_ref, g_ref, ys_ref, o_ref,
                    stage_ref, moe_ref, sem, *, tr, first, sub):
    i = pl.program_id(0)
    par = i % 2

    def run_copy(buf, j, src_row, runs=1):
        src = pl.multiple_of(src_row * SUBLANES, SUBLANES)
        dst = pl.multiple_of(j * (RUN * SUBLANES), RUN * SUBLANES)
        return pltpu.make_async_copy(ys_ref.at[pl.ds(src, runs * RUN * SUBLANES)],
                                     stage_ref.at[buf, pl.ds(dst, runs * RUN * SUBLANES)], sem.at[buf])

    def gather(rref, buf, n):
        def issue_many(q, carry):
            for u in range(ISSUE_UNROLL):
                j = q * ISSUE_UNROLL + u
                run_copy(buf, j, rref[0, 0, j]).start()
            return carry

        def issue_one(j, carry):
            run_copy(buf, j, rref[0, 0, j]).start()
            return carry

        lax.fori_loop(0, n // ISSUE_UNROLL, issue_many, 0)
        lax.fori_loop((n // ISSUE_UNROLL) * ISSUE_UNROLL, n, issue_one, 0)

    @pl.when(i == 0)
    def _():
        gather(rcur_ref, 0, tot_ref[first])

    @pl.when(i + 1 < pl.num_programs(0))
    def _():
        gather(rnext_ref, 1 - par, tot_ref[first + i + 1])

    def wait_many(q, carry):
        run_copy(par, 0, 0, ISSUE_UNROLL).wait()
        return carry

    def wait_one(j, carry):
        run_copy(par, 0, 0).wait()
        return carry

    n_runs = tot_ref[first + i]
    lax.fori_loop(0, n_runs // ISSUE_UNROLL, wait_many, 0)
    lax.fori_loop((n_runs // ISSUE_UNROLL) * ISSUE_UNROLL, n_runs, wait_one, 0)

    def token(buf, r, carry):
        acc = None
        for k in range(TOP_K):
            s = pl.multiple_of(slot_ref[0, 0, k * tr + r], SUBLANES)
            term = gate_ref[0, 0, k * tr + r] * stage_ref[buf, pl.ds(s, SUBLANES), :]
            acc = term if acc is None else acc + term
        moe_ref[pl.ds(pl.multiple_of(r * SUBLANES, SUBLANES), SUBLANES), :] = acc
        return carry

    for buf in range(2):
        @pl.when(par == buf)
        def _():
            lax.fori_loop(0, tr, functools.partial(token, buf), 0, unroll=ISSUE_UNROLL)

    for c in range(tr // sub):
        rows = slice(c * sub, (c + 1) * sub)
        y = x1_ref[rows, :] + _load_slabs(moe_ref.at[pl.ds(c * sub * SUBLANES, sub * SUBLANES)], sub, F32)
        ms = jnp.mean(y * y, axis=-1, keepdims=True)
        o_ref[rows, :] = y * lax.rsqrt(ms + RMS_EPS) * g_ref[...]


def _combine(x1, slots, gates, runs_per_tile, run_src, ys, g, *, tr, first):
    n, d = x1.shape
    steps = n // tr
    n_slots = _stage_slots(tr)
    smem = lambda a, m: pl.BlockSpec((1, 1, a.shape[2]), m, memory_space=pltpu.SMEM)
    cur = lambda i, tot: (i, 0, 0)
    nxt = lambda i, tot: (jnp.minimum(i + 1, steps - 1), 0, 0)
    grid_spec = pltpu.PrefetchScalarGridSpec(
        num_scalar_prefetch=1,
        grid=(steps,),
        in_specs=[smem(run_src, cur), smem(run_src, nxt), smem(slots, cur), smem(gates, cur),
                  pl.BlockSpec((tr, d), lambda i, tot: (i, 0)),
                  pl.BlockSpec((1, d), lambda i, tot: (0, 0)),
                  pl.BlockSpec(memory_space=pl.ANY)],
        out_specs=pl.BlockSpec((tr, d), lambda i, tot: (i, 0)),
        scratch_shapes=[pltpu.VMEM((2, n_slots * SUBLANES, LANES), F32), pltpu.VMEM((tr * SUBLANES, LANES), F32),
                        pltpu.SemaphoreType.DMA((2,))],
    )
    return pl.pallas_call(
        functools.partial(_combine_kernel, tr=tr, first=first, sub=_tile(tr, 128, 8)),
        grid_spec=grid_spec,
        out_shape=jax.ShapeDtypeStruct((n, d), F32),
        compiler_params=_cparams(("arbitrary",)),
        name="combine",
    )(runs_per_tile, run_src, run_src, slots, gates, x1, g, ys)


def _routing_tables(counts_f, *, tme, nb):
    counts = counts_f[:, 0].astype(jnp.int32)
    nblk = jnp.where(counts > 0, (counts + RUN - 1 + tme - 1) // tme, 0)
    blk_end = jnp.cumsum(nblk)
    blk_start = blk_end - nblk
    n_used = blk_end[-1]
    b = jnp.arange(nb, dtype=jnp.int32)
    used = b < n_used
    blk_exp = jnp.minimum(jnp.sum((b[:, None] >= blk_end[None, :]).astype(jnp.int32), axis=1), N_EXPERTS - 1)
    last_exp = jnp.max(jnp.where(nblk > 0, jnp.arange(N_EXPERTS, dtype=jnp.int32), 0))
    blk_exp = jnp.where(used, blk_exp, last_exp).astype(jnp.int32)
    blk_src = jnp.minimum(b, n_used - 1).astype(jnp.int32)
    experts = jnp.arange(N_EXPERTS, dtype=jnp.int32)
    mine = (b[:, None] >= blk_start[None, :]) & (b[:, None] < blk_end[None, :])
    nvalid = jnp.sum(jnp.where(mine, counts[None, :] - (b[:, None] - blk_start[None, :]) * tme, 0), axis=1)
    nvalid = jnp.clip(nvalid, 0, tme).astype(jnp.int32)
    later = (experts[None, :] > experts[:, None]) & (nblk[None, :] > 0)
    nxt_e = jnp.min(jnp.where(later, experts[None, :], N_EXPERTS), axis=1)
    nxt_e = jnp.where(nxt_e == N_EXPERTS, -1, nxt_e)
    blk_next = jnp.sum(jnp.where(blk_exp[:, None] == experts[None, :], nxt_e[None, :], 0), axis=1).astype(jnp.int32)
    per_blk = tme // FILL_ROWS
    piece = jnp.arange(per_blk, dtype=jnp.int32)[None, :]
    tail = (used[:, None] & (piece >= (nvalid // FILL_ROWS)[:, None])).reshape(-1)
    idle = jnp.repeat(~used, per_blk)
    fill_pieces = jnp.argsort(jnp.where(tail, 0, jnp.where(idle, 1, 2)), stable=True).astype(jnp.int32)
    n_fill = jnp.stack([jnp.sum(tail), jnp.sum(tail | idle)]).astype(jnp.int32)
    return (blk_start, blk_exp, blk_src, nvalid, blk_next, n_used.reshape(1).astype(jnp.int32), fill_pieces,
            n_fill)


def _run_tables(tile_carry_f, counts_f, blk_start, *, td, tme):
    before = tile_carry_f[:, :, 0].astype(jnp.int32)
    counts = counts_f[:, 0].astype(jnp.int32)
    in_tile = jnp.concatenate([before[1:], counts[None, :]], axis=0) - before
    n_runs = (in_tile + RUN - 1) // RUN
    ends = jnp.cumsum(n_runs, axis=1)
    j = jnp.arange(_stage_slots(td) // RUN, dtype=jnp.int32)
    owner = jnp.sum((j[None, :, None] >= ends[:, None, :]).astype(jnp.int32), axis=2)
    base = blk_start[None, :] * tme + before - (ends - n_runs) * RUN
    mine = owner[:, :, None] == jnp.arange(N_EXPERTS, dtype=jnp.int32)[None, None, :]
    dst = jnp.sum(jnp.where(mine, base[:, None, :], 0), axis=2) + j[None, :] * RUN
    return ends[:, -1].astype(jnp.int32), dst[:, None, :].astype(jnp.int32)


def kernel(x_prompt, x_sample, cache_k, cache_v, state_conv, attn_norm_g, w_in, attn_sinks, conv_w, conv_b,
           conv_ln_g, conv_ln_b, w_out, ffn_norm_g, router_w, router_b, w1, b1, w2, b2, final_norm_g):
    depth = w_in.shape[0]
    assert depth == 1, "single-layer step"
    bp, sp, d = x_prompt.shape
    bs, ss, _ = x_sample.shape
    cw = conv_w.shape[2]
    np_, ns = bp * sp, bs * ss
    n_tok = np_ + ns
    assert sp % WINDOW == 0

    xp2 = x_prompt.reshape(np_, d)
    xs2 = x_sample.reshape(ns, d)
    w_in_bf = w_in[0].astype(BF16)
    w_out_bf = w_out[0].astype(BF16)
    g_attn = attn_norm_g[0].reshape(1, d)
    g_ffn = ffn_norm_g[0].reshape(1, d)
    sinks = attn_sinks[0]
    vec = lambda a: a.reshape(1, cw)

    tab_p = _rope_tables(jnp.arange(sp, dtype=jnp.int32))
    tms = _tile(ns, 512, max(ss, 16))
    tab_s = _rope_tables(PAST_LEN + (jnp.arange(tms, dtype=jnp.int32) % ss))
    qp, kp, vp, kfp, vfp, ap = _in_proj(xp2, g_attn, w_in_bf, tab_p, seq_period=sp, q_dtype=BF16, conv_width=cw)
    qs, _, _, kfs, vfs, as_ = _in_proj(xs2, g_attn, w_in_bf, tab_s, seq_period=None, q_dtype=F32, conv_width=cw)

    att_p = _attn_prompt(qp, kp, vp, sinks, batch=bp, seq=sp)
    ck = cache_k[0].reshape(bs, WINDOW, KV_WIDTH)
    cv_ = cache_v[0].reshape(bs, WINDOW, KV_WIDTH)
    att_s, nk_s, nv_s = _attn_sample(qs, kfs, vfs, ck, cv_, sinks, batch=bs, ts=ss)

    cv_p = _conv_prompt(ap, conv_w[0], vec(conv_b[0]), vec(conv_ln_g[0]), vec(conv_ln_b[0]), batch=bp, seq=sp)
    cv_s = _conv_sample(as_, state_conv[0], conv_w[0], vec(conv_b[0]), vec(conv_ln_g[0]), vec(conv_ln_b[0]),
                        batch=bs, ts=ss)

    n_exp = router_w.shape[2]
    assert n_exp == N_EXPERTS
    rwt_bf = router_w[0].T.astype(BF16)
    rbt = jnp.broadcast_to(router_b[0][:, None], (n_exp, LANES))
    zero_carry = jnp.zeros((n_exp, LANES), F32)
    tr = _tile(ns, 512, 16)
    assert np_ % tr == 0
    x1p, hnp, rt_p, tcar_p, cnt_p = _outproj_router(xp2, att_p, cv_p, w_out_bf, g_ffn, rwt_bf, rbt, zero_carry, tm=tr)
    x1s, hns, rt_s, tcar_s, cnt = _outproj_router(xs2, att_s, cv_s, w_out_bf, g_ffn, rwt_bf, rbt, cnt_p, tm=tr)

    tme = EXPERT_BLOCK_ROWS
    nb = -(-(n_tok * TOP_K + N_EXPERTS * (tme - 1 + RUN - 1)) // tme)
    blk_start, blk_exp, blk_src, blk_nvalid, blk_next, n_used, fill_pieces, n_fill = _routing_tables(
        cnt, tme=tme, nb=nb)
    runs_per_tile, run_dst = _run_tables(jnp.concatenate([tcar_p, tcar_s], axis=0), cnt, blk_start, td=tr, tme=tme)
    xs_sorted = _dispatch(hnp, hns, jnp.concatenate([rt_p, rt_s], axis=1), fill_pieces, n_fill, runs_per_tile,
                          run_dst, td=tr, nb=nb, tme=tme)
    ys = _experts(xs_sorted, w1[0], b1[0], w2[0], b2[0], blk_exp, blk_src, blk_nvalid, blk_next, n_used, tme=tme)
    g_fin = final_norm_g.reshape(1, d)
    tiles_p = np_ // tr
    copy_slots = lambda rt: _by_tile(rt[0:TOP_K].astype(jnp.int32) * SUBLANES, tr)
    copy_gates = lambda rt: _by_tile(rt[TOP_K:2 * TOP_K], tr)
    y_p = _combine(x1p, copy_slots(rt_p), copy_gates(rt_p), runs_per_tile, run_dst[:tiles_p], ys, g_fin,
                   tr=tr, first=0)
    y_s = _combine(x1s, copy_slots(rt_s), copy_gates(rt_s), runs_per_tile, run_dst[tiles_p:], ys, g_fin,
                   tr=tr, first=tiles_p)

    kv5 = lambda t, bb: t.reshape(bb, -1, KV_WIDTH)[:, -WINDOW:].reshape(bb, WINDOW, N_KV_HEADS, HEAD_DIM)
    new_k_p = kv5(kfp, bp)[None]
    new_v_p = kv5(vfp, bp)[None]
    ctx = CONV_K - 1
    new_c_p = ap.reshape(bp, sp, cw)[:, -ctx:][None]
    new_c_s = jnp.concatenate([state_conv[0], as_.reshape(bs, ss, cw)], axis=1)[:, -ctx:][None]
    return (y_p.reshape(bp, sp, d), y_s.reshape(bs, ss, d), new_k_p, new_v_p, new_c_p,
            kv5(nk_s, bs)[None], kv5(nv_s, bs)[None], new_c_s)
```

```python
import functools

import jax
import jax.numpy as jnp
from jax import lax
from jax.experimental import pallas as pl
from jax.experimental.pallas import tpu as pltpu

F32 = jnp.float32
BF16 = jnp.bfloat16

HEAD_DIM = 64
N_Q_HEADS = 8
N_KV_HEADS = 2
WINDOW = 128
ROPE_THETA = 500000.0
ROPE_DIM = 16
CONV_K = 31
N_EXPERTS = 32
TOP_K = 4
SWIGLU_LIMIT = 7.0
SWIGLU_ALPHA = 1.702
RMS_EPS = 1e-5
LN_EPS = 1e-5
PAST_LEN = 16384

LANES = 128
SUBLANES = 8
CONV_HALO = 32
VMEM_LIMIT = 56 * 1024 * 1024
EXPERT_BLOCK_ROWS = 512
RUN = 8
FILL_ROWS = 64
ROUTE_T_ROWS = 2 * TOP_K
ISSUE_UNROLL = 8

ATTN_WIDTH = N_Q_HEADS * HEAD_DIM
KV_WIDTH = N_KV_HEADS * HEAD_DIM


def _tile(n, pref, mult=8):
    t = min(pref, n)
    while t > 0 and (n % t or t % mult):
        t -= 1
    assert t > 0, (n, pref, mult)
    return t


def _cparams(sem):
    return pltpu.CompilerParams(dimension_semantics=sem, vmem_limit_bytes=VMEM_LIMIT)


def _sigmoid(x):
    return 1.0 / (1.0 + jnp.exp(-x))


def _rope_tables(pos):
    half = ROPE_DIM // 2
    inv_freq = jnp.power(jnp.float32(ROPE_THETA), -jnp.arange(half, dtype=F32) * 2.0 / ROPE_DIM)
    l64 = jnp.arange(LANES) % HEAD_DIM
    assert HEAD_DIM % half == 0
    ang = pos.astype(F32)[:, None] * jnp.tile(inv_freq, LANES // half)[None, :]
    cos_l, sin_l = jnp.cos(ang), jnp.sin(ang)
    c = jnp.where(l64 < ROPE_DIM, cos_l, 1.0)
    s1 = jnp.where(l64 < half, -sin_l, 0.0)
    s2 = jnp.where((l64 >= half) & (l64 < ROPE_DIM), sin_l, 0.0)
    return c.astype(F32), s1.astype(F32), s2.astype(F32)


def _inproj_kernel(x_ref, g_ref, w_ref, c_ref, s1_ref, s2_ref,
                   q_ref, k_ref, v_ref, kf_ref, vf_ref, a_ref, *, conv_width):
    x = x_ref[...]
    ms = jnp.mean(x * x, axis=-1, keepdims=True)
    h = (x * lax.rsqrt(ms + RMS_EPS) * g_ref[...]).astype(BF16)
    z = jnp.dot(h, w_ref[...], preferred_element_type=F32)
    c, s1, s2 = c_ref[...], s1_ref[...], s2_ref[...]
    half = ROPE_DIM // 2

    def rot(t):
        return t * c + pltpu.roll(t, LANES - half, 1) * s1 + pltpu.roll(t, half, 1) * s2

    scale = HEAD_DIM ** -0.5
    for j in range(ATTN_WIDTH // LANES):
        q_ref[:, j * LANES:(j + 1) * LANES] = (rot(z[:, j * LANES:(j + 1) * LANES]) * scale).astype(q_ref.dtype)
    k0 = ATTN_WIDTH
    kr = rot(z[:, k0:k0 + KV_WIDTH])
    k_ref[...] = kr.astype(BF16)
    kf_ref[...] = kr
    v0 = k0 + KV_WIDTH
    vv = z[:, v0:v0 + KV_WIDTH]
    v_ref[...] = vv.astype(BF16)
    vf_ref[...] = vv
    u0 = v0 + KV_WIDTH
    g0 = u0 + conv_width
    a_ref[...] = z[:, u0:g0] * _sigmoid(z[:, g0:g0 + conv_width])


def _in_proj(x2, g, w_bf, tables, *, seq_period, q_dtype, conv_width):
    n, d = x2.shape
    in_w = w_bf.shape[1]
    if seq_period is None:
        tm = tables[0].shape[0]
        tmap = lambda i: (0, 0)
    else:
        tm = _tile(seq_period, 1024, 16)
        per = seq_period // tm
        tmap = lambda i: (i % per, 0)
    assert n % tm == 0
    row = lambda w: pl.BlockSpec((tm, w), lambda i: (i, 0))
    tab = pl.BlockSpec((tm, LANES), tmap)
    return pl.pallas_call(
        functools.partial(_inproj_kernel, conv_width=conv_width),
        grid=(n // tm,),
        in_specs=[row(d), pl.BlockSpec((1, d), lambda i: (0, 0)),
                  pl.BlockSpec((d, in_w), lambda i: (0, 0)), tab, tab, tab],
        out_specs=[row(ATTN_WIDTH), row(KV_WIDTH), row(KV_WIDTH), row(KV_WIDTH), row(KV_WIDTH),
                   row(conv_width)],
        out_shape=[jax.ShapeDtypeStruct((n, ATTN_WIDTH), q_dtype),
                   jax.ShapeDtypeStruct((n, KV_WIDTH), BF16),
                   jax.ShapeDtypeStruct((n, KV_WIDTH), BF16),
                   jax.ShapeDtypeStruct((n, KV_WIDTH), F32),
                   jax.ShapeDtypeStruct((n, KV_WIDTH), F32),
                   jax.ShapeDtypeStruct((n, conv_width), F32)],
        compiler_params=_cparams(("parallel",)),
        name="in_proj",
    )(x2, g, w_bf, *tables)


def _dup_head(t, h):
    sw = pltpu.roll(t, HEAD_DIM, 1)
    low = lax.broadcasted_iota(jnp.int32, t.shape, 1) < HEAD_DIM
    return jnp.where(low, t, sw) if h == 0 else jnp.where(low, sw, t)


def _nt_dot(a, b):
    return lax.dot_general(a, b, (((1,), (1,)), ((), ())), preferred_element_type=F32)


def _attn_prompt_kernel(sink_ref, q_ref, kc_ref, kp_ref, vc_ref, vp_ref, o_ref, *, qb):
    j = pl.program_id(1)
    w = WINDOW
    k_all = jnp.concatenate([kp_ref[...], kc_ref[...]], axis=0).astype(F32)
    v_all = jnp.concatenate([vp_ref[...], vc_ref[...]], axis=0).astype(F32)
    r = lax.broadcasted_iota(jnp.int32, (w, 2 * w), 0)
    kk = lax.broadcasted_iota(jnp.int32, (w, 2 * w), 1)
    band = (kk > r) & (kk <= r + w)
    low = lax.broadcasted_iota(jnp.int32, (w, LANES), 1) < HEAD_DIM
    zero = jnp.zeros((w, LANES), BF16)
    group = N_Q_HEADS // N_KV_HEADS
    for h in range(N_KV_HEADS):
        kd_all = _dup_head(k_all, h).astype(BF16)
        vd_all = _dup_head(v_all, h).astype(BF16)
        for sub in range(qb):
            valid = band & ((kk >= w) | (j > 0)) if sub == 0 else band
            kd = kd_all[sub * w:(sub + 2) * w, :]
            vd = vd_all[sub * w:(sub + 2) * w, :]
            for jj in range(group // 2):
                col = (h * group // 2 + jj) * LANES
                qv = q_ref[sub * w:(sub + 1) * w, col:col + LANES]
                halves = []
                for half in range(2):
                    head = h * group + jj * 2 + half
                    qm = jnp.where(low if half == 0 else ~low, qv, zero)
                    s = jnp.where(valid, _nt_dot(qm, kd), -jnp.inf)
                    sink = sink_ref[head]
                    m = jnp.maximum(jnp.max(s, axis=-1, keepdims=True), sink)
                    p = jnp.exp(s - m)
                    den = jnp.sum(p, axis=-1, keepdims=True) + jnp.exp(sink - m)
                    o = jnp.dot(p.astype(BF16), vd, preferred_element_type=F32)
                    halves.append(o / den)
                o_ref[sub * w:(sub + 1) * w, col:col + LANES] = (
                    jnp.where(low, halves[0], halves[1]).astype(o_ref.dtype))


def _attn_prompt(q, k, v, sinks, *, batch, seq):
    nb = seq // WINDOW
    qb = next(c for c in (4, 2, 1) if nb % c == 0)
    steps = nb // qb
    cur = lambda b, j: (b * steps + j, 0)
    prev = lambda b, j: (b * nb + jnp.maximum(j * qb - 1, 0), 0)
    return pl.pallas_call(
        functools.partial(_attn_prompt_kernel, qb=qb),
        grid=(batch, steps),
        in_specs=[pl.BlockSpec(memory_space=pltpu.SMEM),
                  pl.BlockSpec((qb * WINDOW, ATTN_WIDTH), cur),
                  pl.BlockSpec((qb * WINDOW, KV_WIDTH), cur), pl.BlockSpec((WINDOW, KV_WIDTH), prev),
                  pl.BlockSpec((qb * WINDOW, KV_WIDTH), cur), pl.BlockSpec((WINDOW, KV_WIDTH), prev)],
        out_specs=pl.BlockSpec((qb * WINDOW, ATTN_WIDTH), cur),
        out_shape=jax.ShapeDtypeStruct((batch * seq, ATTN_WIDTH), BF16),
        compiler_params=_cparams(("parallel", "parallel")),
        name="attn_prompt",
    )(sinks, q, k, k, v, v)


def _attn_sample_kernel(sink_ref, q_ref, kn_ref, vn_ref, ck_ref, cv_ref, o_ref, nk_ref, nv_ref, *, gb, ts):
    w = WINDOW
    group = N_Q_HEADS // N_KV_HEADS
    rows = group * ts
    low = lax.broadcasted_iota(jnp.int32, (ts, LANES), 1) < HEAD_DIM
    pad = jnp.zeros((ts, LANES), F32)
    s_c, s_n, v_dup = [], [], []
    for b in range(gb):
        kc, vc = ck_ref[b], cv_ref[b]
        kn, vn = kn_ref[b * ts:(b + 1) * ts, :], vn_ref[b * ts:(b + 1) * ts, :]
        nk_ref[b, 0:w - ts, :] = kc[ts:, :]
        nk_ref[b, w - ts:, :] = kn
        nv_ref[b, 0:w - ts, :] = vc[ts:, :]
        nv_ref[b, w - ts:, :] = vn
        knp = jnp.concatenate([kn, pad], axis=0)
        vnp = jnp.concatenate([vn, pad], axis=0)
        qb = q_ref[b * ts:(b + 1) * ts, :]
        for h in range(N_KV_HEADS):
            parts = []
            for jj in range(group // 2):
                col = (h * group // 2 + jj) * LANES
                qv = qb[:, col:col + LANES]
                parts += [jnp.where(low, qv, 0.0), jnp.where(low, 0.0, qv)]
            lhs = jnp.concatenate(parts, axis=0).astype(BF16)
            s_c.append(_nt_dot(lhs, _dup_head(kc, h).astype(BF16)))
            s_n.append(_nt_dot(lhs, _dup_head(knp, h).astype(BF16)))
            v_dup.append((_dup_head(vc, h).astype(BF16), _dup_head(vnp, h).astype(BF16)))
    s_c = jnp.concatenate(s_c, axis=0)
    s_n = jnp.concatenate(s_n, axis=0)
    n_rows = s_c.shape[0]
    ridx = lax.broadcasted_iota(jnp.int32, (n_rows, 1), 0)
    t_row = ridx % ts
    head_row = (ridx // ts) % N_Q_HEADS
    sink = jnp.zeros((n_rows, 1), F32)
    for hd in range(N_Q_HEADS):
        sink = jnp.where(head_row == hd, sink_ref[hd], sink)
    c_idx = lax.broadcasted_iota(jnp.int32, (n_rows, w), 1)
    n_idx = lax.broadcasted_iota(jnp.int32, (n_rows, 2 * ts), 1)
    s_c = jnp.where(c_idx > t_row, s_c, -jnp.inf)
    s_n = jnp.where(n_idx <= t_row, s_n, -jnp.inf)
    m = jnp.maximum(jnp.maximum(jnp.max(s_c, axis=-1, keepdims=True), jnp.max(s_n, axis=-1, keepdims=True)), sink)
    p_c = jnp.exp(s_c - m)
    p_n = jnp.exp(s_n - m)
    den = jnp.sum(p_c, axis=-1, keepdims=True) + jnp.sum(p_n, axis=-1, keepdims=True) + jnp.exp(sink - m)
    p_c = p_c.astype(BF16)
    p_n = p_n.astype(BF16)
    outs = []
    for b in range(gb):
        cols = []
        for h in range(N_KV_HEADS):
            ci = b * N_KV_HEADS + h
            sl = slice(ci * rows, (ci + 1) * rows)
            vdc, vdn = v_dup[ci]
            o = (jnp.dot(p_c[sl], vdc, preferred_element_type=F32)
                 + jnp.dot(p_n[sl], vdn, preferred_element_type=F32)) / den[sl]
            for jj in range(group // 2):
                lo_part = o[(2 * jj) * ts:(2 * jj + 1) * ts, :]
                hi_part = o[(2 * jj + 1) * ts:(2 * jj + 2) * ts, :]
                cols.append(jnp.where(low, lo_part, hi_part))
        outs.append(jnp.concatenate(cols, axis=1))
    o_ref[...] = jnp.concatenate(outs, axis=0).astype(o_ref.dtype)


def _attn_sample(q, kf, vf, cache_k, cache_v, sinks, *, batch, ts):
    assert ts % 8 == 0 and ts <= WINDOW
    gb = _tile(batch, 8, 2)
    tok = lambda w: pl.BlockSpec((gb * ts, w), lambda i: (i, 0))
    cache = pl.BlockSpec((gb, WINDOW, KV_WIDTH), lambda i: (i, 0, 0))
    cshape = jax.ShapeDtypeStruct((batch, WINDOW, KV_WIDTH), F32)
    return pl.pallas_call(
        functools.partial(_attn_sample_kernel, gb=gb, ts=ts),
        grid=(batch // gb,),
        in_specs=[pl.BlockSpec(memory_space=pltpu.SMEM), tok(ATTN_WIDTH), tok(KV_WIDTH), tok(KV_WIDTH),
                  cache, cache],
        out_specs=[tok(ATTN_WIDTH), cache, cache],
        out_shape=[jax.ShapeDtypeStruct((batch * ts, ATTN_WIDTH), BF16), cshape, cshape],
        compiler_params=_cparams(("parallel",)),
        name="attn_sample",
    )(sinks, q, kf, vf, cache_k, cache_v)


def _ln_swish(acc, b, lg, lb):
    y = acc + b
    mu = jnp.mean(y, axis=-1, keepdims=True)
    yc = y - mu
    var = jnp.mean(yc * yc, axis=-1, keepdims=True)
    yn = yc * lax.rsqrt(var + LN_EPS) * lg + lb
    return yn * _sigmoid(yn)


def _conv_prompt_kernel(a_ref, ap_ref, w_ref, b_ref, lg_ref, lb_ref, o_ref, win_ref, *, tt, rc):
    j = pl.program_id(1)
    n = CONV_HALO + tt
    win = jnp.concatenate([jnp.where(j > 0, ap_ref[...], 0.0), a_ref[...]], axis=0)
    win_ref[0] = win
    for r in range(1, SUBLANES):
        win_ref[r] = pltpu.roll(win, n - r, 0)
    off = CONV_HALO - (CONV_K - 1)
    b, lg, lb = b_ref[...], lg_ref[...], lb_ref[...]
    for c in range(tt // rc):
        acc = jnp.zeros((rc, a_ref.shape[1]), F32)
        for k in range(CONV_K):
            s = off + k
            base = c * rc + (s // SUBLANES) * SUBLANES
            wk = jnp.concatenate([w_ref[k]] * (rc // SUBLANES), axis=0)
            acc = acc + wk * win_ref[s % SUBLANES, base:base + rc, :]
        o_ref[c * rc:(c + 1) * rc, :] = _ln_swish(acc, b, lg, lb).astype(o_ref.dtype)


def _conv_prompt(a, w, b, lg, lb, *, batch, seq):
    cw = a.shape[1]
    tt = _tile(seq, 512, CONV_HALO)
    rc = _tile(tt, 32, 16)
    nt = seq // tt
    per = tt // CONV_HALO
    cur = lambda bb, j: (bb * nt + j, 0)
    prev = lambda bb, j: (jnp.maximum((bb * nt + j) * per - 1, 0), 0)
    vec = pl.BlockSpec((1, cw), lambda bb, j: (0, 0))
    return pl.pallas_call(
        functools.partial(_conv_prompt_kernel, tt=tt, rc=rc),
        grid=(batch, nt),
        in_specs=[pl.BlockSpec((tt, cw), cur), pl.BlockSpec((CONV_HALO, cw), prev),
                  pl.BlockSpec((CONV_K, SUBLANES, cw), lambda bb, j: (0, 0, 0)), vec, vec, vec],
        out_specs=pl.BlockSpec((tt, cw), cur),
        out_shape=jax.ShapeDtypeStruct((batch * seq, cw), BF16),
        scratch_shapes=[pltpu.VMEM((SUBLANES, CONV_HALO + tt, cw), F32)],
        compiler_params=_cparams(("parallel", "parallel")),
        name="conv_prompt",
    )(a, a, jnp.broadcast_to(w[:, None, :], (CONV_K, SUBLANES, cw)), b, lg, lb)


def _conv_sample_kernel(a_ref, st_ref, w_ref, b_ref, lg_ref, lb_ref, o_ref, win_ref, *, gb, ts):
    ctx = CONV_K - 1
    b, lg, lb = b_ref[...], lg_ref[...], lb_ref[...]
    for bb in range(gb):
        win_ref[bb, 0:ctx, :] = st_ref[bb]
        win_ref[bb, ctx:ctx + ts, :] = a_ref[bb * ts:(bb + 1) * ts, :]
    outs = []
    for bb in range(gb):
        acc = jnp.zeros((ts, a_ref.shape[1]), F32)
        for k in range(CONV_K):
            wk = jnp.concatenate([w_ref[k]] * (ts // SUBLANES), axis=0)
            acc = acc + wk * win_ref[bb, k:k + ts, :]
        outs.append(_ln_swish(acc, b, lg, lb))
    o_ref[...] = jnp.concatenate(outs, axis=0).astype(o_ref.dtype)


def _conv_sample(a, state, w, b, lg, lb, *, batch, ts):
    cw = a.shape[1]
    ctx = CONV_K - 1
    gb = _tile(batch, 8, 2)
    vec = pl.BlockSpec((1, cw), lambda i: (0, 0))
    return pl.pallas_call(
        functools.partial(_conv_sample_kernel, gb=gb, ts=ts),
        grid=(batch // gb,),
        in_specs=[pl.BlockSpec((gb * ts, cw), lambda i: (i, 0)),
                  pl.BlockSpec((gb, ctx, cw), lambda i: (i, 0, 0)),
                  pl.BlockSpec((CONV_K, SUBLANES, cw), lambda i: (0, 0, 0)), vec, vec, vec],
        out_specs=pl.BlockSpec((gb * ts, cw), lambda i: (i, 0)),
        out_shape=jax.ShapeDtypeStruct((batch * ts, cw), BF16),
        scratch_shapes=[pltpu.VMEM((gb, ctx + ts + 2, cw), F32)],
        compiler_params=_cparams(("parallel",)),
        name="conv_sample",
    )(a, state, jnp.broadcast_to(w[:, None, :], (CONV_K, SUBLANES, cw)), b, lg, lb)


def _outproj_router_kernel(x_ref, att_ref, cv_ref, wo_ref, g_ref, rw_ref, rb_ref, cin_ref,
                           x1_ref, hn_ref, rt_ref, tcar_ref, cnt_ref, carry_ref):
    i = pl.program_id(0)

    @pl.when(i == 0)
    def _():
        carry_ref[...] = cin_ref[...]

    tm = x_ref.shape[0]
    aw = att_ref.shape[1]
    n_exp = rw_ref.shape[0]
    r = lax.broadcasted_iota(jnp.int32, (tm, tm), 0)
    c = lax.broadcasted_iota(jnp.int32, (tm, tm), 1)
    before = (r < c).astype(BF16)
    eid = lax.broadcasted_iota(jnp.int32, (n_exp, tm), 0)
    carry = carry_ref[...][:, 0:1]
    tcar_ref[0] = carry_ref[...]
    mix = (jnp.dot(att_ref[...], wo_ref[0:aw, :], preferred_element_type=F32)
           + jnp.dot(cv_ref[...], wo_ref[aw:, :], preferred_element_type=F32))
    x1 = x_ref[...] + mix
    x1_ref[...] = x1
    ms = jnp.mean(x1 * x1, axis=-1, keepdims=True)
    hn = (x1 * lax.rsqrt(ms + RMS_EPS) * g_ref[...]).astype(BF16)
    hn_ref[...] = hn
    logits = _nt_dot(rw_ref[...], hn) + rb_ref[...][:, 0:1]
    onehot = jnp.zeros((n_exp, tm), F32)
    vals, idxs = [], []
    for _ in range(TOP_K):
        m = jnp.max(logits, axis=0, keepdims=True)
        idx = jnp.min(jnp.where(logits == m, eid, n_exp), axis=0, keepdims=True)
        sel = eid == idx
        onehot = onehot + sel.astype(F32)
        logits = jnp.where(sel, -jnp.inf, logits)
        vals.append(m)
        idxs.append(idx)
    es = [jnp.exp(v - vals[0]) for v in vals]
    den = es[0] + es[1] + es[2] + es[3]
    local = jnp.dot(onehot.astype(BF16), before, preferred_element_type=F32)
    count = jnp.sum(onehot, axis=1, keepdims=True)
    padded = jnp.ceil(count * (1.0 / RUN)) * RUN
    er = lax.broadcasted_iota(jnp.int32, (n_exp, n_exp), 0)
    ec = lax.broadcasted_iota(jnp.int32, (n_exp, n_exp), 1)
    start = jnp.dot((ec < er).astype(BF16), jnp.broadcast_to(padded, (n_exp, LANES)).astype(BF16),
                    preferred_element_type=F32)[:, 0:1]
    slot_of = local + start
    slots = [jnp.sum(jnp.where(eid == idxs[k], slot_of, 0.0), axis=0, keepdims=True) for k in range(TOP_K)]
    rt_ref[...] = jnp.concatenate(slots + [e / den for e in es], axis=0)
    carry = carry + count
    carry_ref[...] = jnp.broadcast_to(carry, carry_ref.shape)
    cnt_ref[...] = jnp.broadcast_to(carry, cnt_ref.shape)


def _store_slabs(ref, val, rows):
    for j in range(val.shape[1] // LANES):
        ref[pl.ds(j, rows, stride=SUBLANES), :] = val[:, j * LANES:(j + 1) * LANES]


def _load_slabs(ref, rows, dtype):
    return jnp.concatenate([ref[pl.ds(j, rows, stride=SUBLANES), :].astype(dtype) for j in range(SUBLANES)],
                           axis=1)


def _outproj_router(x2, att, cv, wo_bf, g, rwt_bf, rbt, carry_in, *, tm):
    n, d = x2.shape
    assert d == SUBLANES * LANES and n % tm == 0
    n_exp = rwt_bf.shape[0]
    row = lambda w: pl.BlockSpec((tm, w), lambda i: (i, 0))
    full = lambda a: pl.BlockSpec(a.shape, lambda i: (0,) * a.ndim)
    return pl.pallas_call(
        _outproj_router_kernel,
        grid=(n // tm,),
        in_specs=[row(d), row(att.shape[1]), row(cv.shape[1]), full(wo_bf), full(g), full(rwt_bf), full(rbt),
                  full(carry_in)],
        out_specs=[row(d), row(d),
                   pl.BlockSpec((ROUTE_T_ROWS, tm), lambda i: (0, i)),
                   pl.BlockSpec((1, n_exp, LANES), lambda i: (i, 0, 0)),
                   pl.BlockSpec((n_exp, LANES), lambda i: (0, 0))],
        out_shape=[jax.ShapeDtypeStruct((n, d), F32), jax.ShapeDtypeStruct((n, d), BF16),
                   jax.ShapeDtypeStruct((ROUTE_T_ROWS, n), F32),
                   jax.ShapeDtypeStruct((n // tm, n_exp, LANES), F32),
                   jax.ShapeDtypeStruct((n_exp, LANES), F32)],
        scratch_shapes=[pltpu.VMEM((n_exp, LANES), F32)],
        compiler_params=_cparams(("arbitrary",)),
        name="outproj_router",
    )(x2, att, cv, wo_bf, g, rwt_bf, rbt, carry_in)


def _dispatch_kernel(fill_ref, nfill_ref, tot_ref, dst_ref, rt_ref, hn_ref, hn2_ref,
                     xs_ref, zero_ref, stage_ref, sem, zsem, *, td, n_first, chunk):
    i = pl.program_id(0)
    last = pl.num_programs(0) - 1
    par = i % 2
    n_slots = stage_ref.shape[1] // SUBLANES

    def fill(f, s):
        row = pl.multiple_of(fill_ref[f] * (FILL_ROWS * SUBLANES), FILL_ROWS * SUBLANES)
        return pltpu.make_async_copy(zero_ref, xs_ref.at[pl.ds(row, FILL_ROWS * SUBLANES)], zsem.at[s])

    def fills(lo, hi, s, act):
        def body(f, carry):
            act(fill(f, s))
            return carry

        lax.fori_loop(lo, hi, body, 0)

    @pl.when(i == 0)
    def _():
        zero_ref[...] = jnp.zeros(zero_ref.shape, F32)
        fills(0, nfill_ref[0], 0, lambda c: c.start())
        fills(nfill_ref[0], nfill_ref[1], 1, lambda c: c.start())
        fills(0, nfill_ref[0], 0, lambda c: c.wait())

    def group_by_expert(tok_ref):
        hn = tok_ref[...]
        slots = rt_ref[0:TOP_K, :].astype(jnp.int32)
        for c in range(n_slots // chunk):
            s = lax.broadcasted_iota(jnp.int32, (chunk, td), 0) + c * chunk
            hit = s == slots[0:1, :]
            for k in range(1, TOP_K):
                hit = hit | (s == slots[k:k + 1, :])
            rows = jnp.dot(jnp.where(hit, 1.0, 0.0).astype(BF16), hn, preferred_element_type=F32)
            _store_slabs(stage_ref.at[par, pl.ds(c * chunk * SUBLANES, chunk * SUBLANES)], rows, chunk)

    def run_copy(buf, src_slot, dst_row, runs=1):
        src = pl.multiple_of(src_slot * SUBLANES, RUN * SUBLANES)
        dst = pl.multiple_of(dst_row * SUBLANES, SUBLANES)
        return pltpu.make_async_copy(stage_ref.at[buf, pl.ds(src, runs * RUN * SUBLANES)],
                                     xs_ref.at[pl.ds(dst, runs * RUN * SUBLANES)], sem)

    def wait_runs(tile):
        def wait_many(q, carry):
            run_copy(0, 0, 0, ISSUE_UNROLL).wait()
            return carry

        def wait_one(j, carry):
            run_copy(0, 0, 0).wait()
            return carry

        n_many = tot_ref[tile] // ISSUE_UNROLL
        lax.fori_loop(0, n_many, wait_many, 0)
        lax.fori_loop(n_many * ISSUE_UNROLL, tot_ref[tile], wait_one, 0)

    @pl.when(i >= 2)
    def _():
        wait_runs(i - 2)

    @pl.when(i >= 1)
    def _():
        def issue(j):
            run_copy(1 - par, j * RUN, dst_ref[0, 0, j]).start()

        def issue_many(q, carry):
            for u in range(ISSUE_UNROLL):
                issue(q * ISSUE_UNROLL + u)
            return carry

        def issue_one(j, carry):
            issue(j)
            return carry

        n_full = tot_ref[i - 1] // ISSUE_UNROLL
        lax.fori_loop(0, n_full, issue_many, 0)
        lax.fori_loop(n_full * ISSUE_UNROLL, tot_ref[i - 1], issue_one, 0)

    @pl.when(i < n_first)
    def _():
        group_by_expert(hn_ref)

    @pl.when((i >= n_first) & (i < last))
    def _():
        group_by_expert(hn2_ref)

    @pl.when(i == last)
    def _():
        wait_runs(i - 1)
        fills(nfill_ref[0], nfill_ref[1], 1, lambda c: c.wait())


def _stage_slots(td):
    return td * TOP_K + N_EXPERTS * RUN


def _by_tile(field, td):
    n = field.shape[1]
    return field.reshape(TOP_K, n // td, td).transpose(1, 0, 2).reshape(n // td, 1, TOP_K * td)


def _dispatch(hn_a, hn_b, route_t, fill_pieces, n_fill, runs_per_tile, run_dst, *, td, nb, tme):
    na, d = hn_a.shape
    assert na % td == 0 and hn_b.shape[0] % td == 0 and td % RUN == 0
    n_first = na // td
    n_second = hn_b.shape[0] // td
    tiles = n_first + n_second
    n_slots = _stage_slots(td)
    chunk = _tile(n_slots, 768, SUBLANES)
    tok = lambda m: pl.BlockSpec((td, d), m)
    grid_spec = pltpu.PrefetchScalarGridSpec(
        num_scalar_prefetch=3,
        grid=(tiles + 1,),
        in_specs=[pl.BlockSpec((1, 1, n_slots // RUN), lambda i, *_: (jnp.maximum(i - 1, 0), 0, 0),
                               memory_space=pltpu.SMEM),
                  pl.BlockSpec((ROUTE_T_ROWS, td), lambda i, *_: (0, jnp.minimum(i, tiles - 1))),
                  tok(lambda i, *_: (jnp.minimum(i, n_first - 1), 0)),
                  tok(lambda i, *_: (jnp.clip(i - n_first, 0, n_second - 1), 0))],
        out_specs=pl.BlockSpec(memory_space=pl.ANY),
        scratch_shapes=[pltpu.VMEM((FILL_ROWS * SUBLANES, LANES), F32),
                        pltpu.VMEM((2, n_slots * SUBLANES, LANES), F32),
                        pltpu.SemaphoreType.DMA(()), pltpu.SemaphoreType.DMA((2,))],
    )
    return pl.pallas_call(
        functools.partial(_dispatch_kernel, td=td, n_first=n_first, chunk=chunk),
        grid_spec=grid_spec,
        out_shape=jax.ShapeDtypeStruct((nb * tme * SUBLANES, LANES), F32),
        compiler_params=_cparams(("arbitrary",)),
        name="dispatch",
    )(fill_pieces, n_fill, runs_per_tile, run_dst, route_t, hn_a, hn_b)


def _experts_kernel(be_ref, bsrc_ref, nv_ref, nx_ref, nu_ref, x_ref, w1_hbm, b1_ref, w2_hbm, b2_ref,
                    y_ref, w1f_ref, w2f_ref, w1b_ref, w2b_ref, par_ref, wsem, *, tme):
    i = pl.program_id(0)
    e = be_ref[i]
    e_prev = be_ref[jnp.maximum(i - 1, 0)]
    d_ff = w2f_ref.shape[1]
    half = tme // 2

    def fetch(expert, s):
        return (pltpu.make_async_copy(w1_hbm.at[expert], w1f_ref.at[s], wsem.at[0, s]),
                pltpu.make_async_copy(w2_hbm.at[expert], w2f_ref.at[s], wsem.at[1, s]))

    @pl.when(i == 0)
    def _():
        par_ref[0] = 0
        for c in fetch(e, 0):
            c.start()

    @pl.when((i == 0) | (e != e_prev))
    def _():
        s = par_ref[0]
        for c in fetch(e, s):
            c.wait()
        w1b_ref[...] = w1f_ref[s].astype(BF16)
        w2b_ref[...] = w2f_ref[s].astype(BF16)
        nxt = nx_ref[i]

        @pl.when(nxt >= 0)
        def _():
            for c in fetch(nxt, 1 - s):
                c.start()

        par_ref[0] = 1 - s

    def ffn(x):
        h = jnp.dot(x, w1b_ref[...], preferred_element_type=F32) + b1_ref[0]
        x_glu = jnp.minimum(h[:, :d_ff], SWIGLU_LIMIT)
        x_lin = jnp.clip(h[:, d_ff:], -SWIGLU_LIMIT, SWIGLU_LIMIT)
        act = x_glu * _sigmoid(SWIGLU_ALPHA * x_glu) * (x_lin + 1.0)
        return jnp.dot(act.astype(BF16), w2b_ref[...], preferred_element_type=F32) + b2_ref[0]

    used = i < nu_ref[0]
    nv = nv_ref[i]

    @pl.when(used & (nv > half))
    def _():
        _store_slabs(y_ref, ffn(_load_slabs(x_ref, tme, BF16)), tme)

    @pl.when(used & (nv <= half))
    def _():
        rows = half * SUBLANES
        _store_slabs(y_ref.at[pl.ds(0, rows)], ffn(_load_slabs(x_ref.at[pl.ds(0, rows)], half, BF16)), half)
        y_ref[pl.ds(rows, rows), :] = jnp.zeros((rows, LANES), F32)

    @pl.when(i == nu_ref[0])
    def _():
        y_ref[...] = jnp.zeros(y_ref.shape, F32)


def _experts(xs, w1, b1, w2, b2, blk_exp, blk_src, blk_nvalid, blk_next, n_used, *, tme):
    n_exp, d, h2 = w1.shape
    d_ff = w2.shape[1]
    nb = xs.shape[0] // (tme * SUBLANES)
    slab = lambda m: pl.BlockSpec((tme * SUBLANES, LANES), m)
    grid_spec = pltpu.PrefetchScalarGridSpec(
        num_scalar_prefetch=5,
        grid=(nb,),
        in_specs=[slab(lambda i, be, bs, nv, nx, nu: (bs[i], 0)),
                  pl.BlockSpec(memory_space=pl.ANY),
                  pl.BlockSpec((1, 1, h2), lambda i, be, bs, nv, nx, nu: (be[i], 0, 0)),
                  pl.BlockSpec(memory_space=pl.ANY),
                  pl.BlockSpec((1, 1, d), lambda i, be, bs, nv, nx, nu: (be[i], 0, 0))],
        out_specs=slab(lambda i, be, bs, nv, nx, nu: (jnp.minimum(i, nu[0]), 0)),
        scratch_shapes=[pltpu.VMEM((2, d, h2), F32), pltpu.VMEM((2, d_ff, d), F32),
                        pltpu.VMEM((d, h2), BF16), pltpu.VMEM((d_ff, d), BF16),
                        pltpu.SMEM((1,), jnp.int32), pltpu.SemaphoreType.DMA((2, 2))],
    )
    return pl.pallas_call(
        functools.partial(_experts_kernel, tme=tme),
        grid_spec=grid_spec,
        out_shape=jax.ShapeDtypeStruct(xs.shape, F32),
        input_output_aliases={5: 0},
        compiler_params=_cparams(("arbitrary",)),
        name="experts",
    )(blk_exp, blk_src, blk_nvalid, blk_next, n_used, xs, w1, b1.reshape(n_exp, 1, h2), w2,
      b2.reshape(n_exp, 1, d))


def _combine_kernel(tot_ref, rcur_ref, rnext_ref, slot_ref, gate_ref, x1_ref, g_ref, ys_ref, o_ref,
                    stage_ref, moe_ref, sem, *, tr, first, sub):
    i = pl.program_id(0)
    par = i % 2

    def run_copy(buf, j, src_row, runs=1):
        src = pl.multiple_of(src_row * SUBLANES, SUBLANES)
        dst = pl.multiple_of(j * (RUN * SUBLANES), RUN * SUBLANES)
        return pltpu.make_async_copy(ys_ref.at[pl.ds(src, runs * RUN * SUBLANES)],
                                     stage_ref.at[buf, pl.ds(dst, runs * RUN * SUBLANES)], sem.at[buf])

    def gather(rref, buf, n):
        def issue_many(q, carry):
            for u in range(ISSUE_UNROLL):
                j = q * ISSUE_UNROLL + u
                run_copy(buf, j, rref[0, 0, j]).start()
            return carry

        def issue_one(j, carry):
            run_copy(buf, j, rref[0, 0, j]).start()
            return carry

        lax.fori_loop(0, n // ISSUE_UNROLL, issue_many, 0)
        lax.fori_loop((n // ISSUE_UNROLL) * ISSUE_UNROLL, n, issue_one, 0)

    @pl.when(i == 0)
    def _():
        gather(rcur_ref, 0, tot_ref[first])

    @pl.when(i + 1 < pl.num_programs(0))
    def _():
        gather(rnext_ref, 1 - par, tot_ref[first + i + 1])

    def wait_many(q, carry):
        run_copy(par, 0, 0, ISSUE_UNROLL).wait()
        return carry

    def wait_one(j, carry):
        run_copy(par, 0, 0).wait()
        return carry

    n_runs = tot_ref[first + i]
    lax.fori_loop(0, n_runs // ISSUE_UNROLL, wait_many, 0)
    lax.fori_loop((n_runs // ISSUE_UNROLL) * ISSUE_UNROLL, n_runs, wait_one, 0)

    def token(buf, r, carry):
        acc = None
        for k in range(TOP_K):
            s = pl.multiple_of(slot_ref[0, 0, k * tr + r], SUBLANES)
            term = gate_ref[0, 0, k * tr + r] * stage_ref[buf, pl.ds(s, SUBLANES), :]
            acc = term if acc is None else acc + term
        moe_ref[pl.ds(pl.multiple_of(r * SUBLANES, SUBLANES), SUBLANES), :] = acc
        return carry

    for buf in range(2):
        @pl.when(par == buf)
        def _():
            lax.fori_loop(0, tr, functools.partial(token, buf), 0, unroll=ISSUE_UNROLL)

    for c in range(tr // sub):
        rows = slice(c * sub, (c + 1) * sub)
        y = x1_ref[rows, :] + _load_slabs(moe_ref.at[pl.ds(c * sub * SUBLANES, sub * SUBLANES)], sub, F32)
        ms = jnp.mean(y * y, axis=-1, keepdims=True)
        o_ref[rows, :] = y * lax.rsqrt(ms + RMS_EPS) * g_ref[...]


def _combine(x1, slots, gates, runs_per_tile, run_src, ys, g, *, tr, first):
    n, d = x1.shape
    steps = n // tr
    n_slots = _stage_slots(tr)
    smem = lambda a, m: pl.BlockSpec((1, 1, a.shape[2]), m, memory_space=pltpu.SMEM)
    cur = lambda i, tot: (i, 0, 0)
    nxt = lambda i, tot: (jnp.minimum(i + 1, steps - 1), 0, 0)
    grid_spec = pltpu.PrefetchScalarGridSpec(
        num_scalar_prefetch=1,
        grid=(steps,),
        in_specs=[smem(run_src, cur), smem(run_src, nxt), smem(slots, cur), smem(gates, cur),
                  pl.BlockSpec((tr, d), lambda i, tot: (i, 0)),
                  pl.BlockSpec((1, d), lambda i, tot: (0, 0)),
                  pl.BlockSpec(memory_space=pl.ANY)],
        out_specs=pl.BlockSpec((tr, d), lambda i, tot: (i, 0)),
        scratch_shapes=[pltpu.VMEM((2, n_slots * SUBLANES, LANES), F32), pltpu.VMEM((tr * SUBLANES, LANES), F32),
                        pltpu.SemaphoreType.DMA((2,))],
    )
    return pl.pallas_call(
        functools.partial(_combine_kernel, tr=tr, first=first, sub=_tile(tr, 128, 8)),
        grid_spec=grid_spec,
        out_shape=jax.ShapeDtypeStruct((n, d), F32),
        compiler_params=_cparams(("arbitrary",)),
        name="combine",
    )(runs_per_tile, run_src, run_src, slots, gates, x1, g, ys)


def _routing_tables(counts_f, *, tme, nb):
    counts = counts_f[:, 0].astype(jnp.int32)
    nblk = jnp.where(counts > 0, (counts + RUN - 1 + tme - 1) // tme, 0)
    blk_end = jnp.cumsum(nblk)
    blk_start = blk_end - nblk
    n_used = blk_end[-1]
    b = jnp.arange(nb, dtype=jnp.int32)
    used = b < n_used
    blk_exp = jnp.minimum(jnp.sum((b[:, None] >= blk_end[None, :]).astype(jnp.int32), axis=1), N_EXPERTS - 1)
    last_exp = jnp.max(jnp.where(nblk > 0, jnp.arange(N_EXPERTS, dtype=jnp.int32), 0))
    blk_exp = jnp.where(used, blk_exp, last_exp).astype(jnp.int32)
    blk_src = jnp.minimum(b, n_used - 1).astype(jnp.int32)
    experts = jnp.arange(N_EXPERTS, dtype=jnp.int32)
    mine = (b[:, None] >= blk_start[None, :]) & (b[:, None] < blk_end[None, :])
    nvalid = jnp.sum(jnp.where(mine, counts[None, :] - (b[:, None] - blk_start[None, :]) * tme, 0), axis=1)
    nvalid = jnp.clip(nvalid, 0, tme).astype(jnp.int32)
    later = (experts[None, :] > experts[:, None]) & (nblk[None, :] > 0)
    nxt_e = jnp.min(jnp.where(later, experts[None, :], N_EXPERTS), axis=1)
    nxt_e = jnp.where(nxt_e == N_EXPERTS, -1, nxt_e)
    blk_next = jnp.sum(jnp.where(blk_exp[:, None] == experts[None, :], nxt_e[None, :], 0), axis=1).astype(jnp.int32)
    per_blk = tme // FILL_ROWS
    piece = jnp.arange(per_blk, dtype=jnp.int32)[None, :]
    tail = (used[:, None] & (piece >= (nvalid // FILL_ROWS)[:, None])).reshape(-1)
    idle = jnp.repeat(~used, per_blk)
    fill_pieces = jnp.argsort(jnp.where(tail, 0, jnp.where(idle, 1, 2)), stable=True).astype(jnp.int32)
    n_fill = jnp.stack([jnp.sum(tail), jnp.sum(tail | idle)]).astype(jnp.int32)
    return (blk_start, blk_exp, blk_src, nvalid, blk_next, n_used.reshape(1).astype(jnp.int32), fill_pieces,
            n_fill)


def _run_tables(tile_carry_f, counts_f, blk_start, *, td, tme):
    before = tile_carry_f[:, :, 0].astype(jnp.int32)
    counts = counts_f[:, 0].astype(jnp.int32)
    in_tile = jnp.concatenate([before[1:], counts[None, :]], axis=0) - before
    n_runs = (in_tile + RUN - 1) // RUN
    ends = jnp.cumsum(n_runs, axis=1)
    j = jnp.arange(_stage_slots(td) // RUN, dtype=jnp.int32)
    owner = jnp.sum((j[None, :, None] >= ends[:, None, :]).astype(jnp.int32), axis=2)
    base = blk_start[None, :] * tme + before - (ends - n_runs) * RUN
    mine = owner[:, :, None] == jnp.arange(N_EXPERTS, dtype=jnp.int32)[None, None, :]
    dst = jnp.sum(jnp.where(mine, base[:, None, :], 0), axis=2) + j[None, :] * RUN
    return ends[:, -1].astype(jnp.int32), dst[:, None, :].astype(jnp.int32)


def kernel(x_prompt, x_sample, cache_k, cache_v, state_conv, attn_norm_g, w_in, attn_sinks, conv_w, conv_b,
           conv_ln_g, conv_ln_b, w_out, ffn_norm_g, router_w, router_b, w1, b1, w2, b2, final_norm_g):
    depth = w_in.shape[0]
    assert depth == 1, "single-layer step"
    bp, sp, d = x_prompt.shape
    bs, ss, _ = x_sample.shape
    cw = conv_w.shape[2]
    np_, ns = bp * sp, bs * ss
    n_tok = np_ + ns
    assert sp % WINDOW == 0

    xp2 = x_prompt.reshape(np_, d)
    xs2 = x_sample.reshape(ns, d)
    w_in_bf = w_in[0].astype(BF16)
    w_out_bf = w_out[0].astype(BF16)
    g_attn = attn_norm_g[0].reshape(1, d)
    g_ffn = ffn_norm_g[0].reshape(1, d)
    sinks = attn_sinks[0]
    vec = lambda a: a.reshape(1, cw)

    tab_p = _rope_tables(jnp.arange(sp, dtype=jnp.int32))
    tms = _tile(ns, 512, max(ss, 16))
    tab_s = _rope_tables(PAST_LEN + (jnp.arange(tms, dtype=jnp.int32) % ss))
    qp, kp, vp, kfp, vfp, ap = _in_proj(xp2, g_attn, w_in_bf, tab_p, seq_period=sp, q_dtype=BF16, conv_width=cw)
    qs, _, _, kfs, vfs, as_ = _in_proj(xs2, g_attn, w_in_bf, tab_s, seq_period=None, q_dtype=F32, conv_width=cw)

    att_p = _attn_prompt(qp, kp, vp, sinks, batch=bp, seq=sp)
    ck = cache_k[0].reshape(bs, WINDOW, KV_WIDTH)
    cv_ = cache_v[0].reshape(bs, WINDOW, KV_WIDTH)
    att_s, nk_s, nv_s = _attn_sample(qs, kfs, vfs, ck, cv_, sinks, batch=bs, ts=ss)

    cv_p = _conv_prompt(ap, conv_w[0], vec(conv_b[0]), vec(conv_ln_g[0]), vec(conv_ln_b[0]), batch=bp, seq=sp)
    cv_s = _conv_sample(as_, state_conv[0], conv_w[0], vec(conv_b[0]), vec(conv_ln_g[0]), vec(conv_ln_b[0]),
                        batch=bs, ts=ss)

    n_exp = router_w.shape[2]
    assert n_exp == N_EXPERTS
    rwt_bf = router_w[0].T.astype(BF16)
    rbt = jnp.broadcast_to(router_b[0][:, None], (n_exp, LANES))
    zero_carry = jnp.zeros((n_exp, LANES), F32)
    tr = _tile(ns, 512, 16)
    assert np_ % tr == 0
    x1p, hnp, rt_p, tcar_p, cnt_p = _outproj_router(xp2, att_p, cv_p, w_out_bf, g_ffn, rwt_bf, rbt, zero_carry, tm=tr)
    x1s, hns, rt_s, tcar_s, cnt = _outproj_router(xs2, att_s, cv_s, w_out_bf, g_ffn, rwt_bf, rbt, cnt_p, tm=tr)

    tme = EXPERT_BLOCK_ROWS
    nb = -(-(n_tok * TOP_K + N_EXPERTS * (tme - 1 + RUN - 1)) // tme)
    blk_start, blk_exp, blk_src, blk_nvalid, blk_next, n_used, fill_pieces, n_fill = _routing_tables(
        cnt, tme=tme, nb=nb)
    runs_per_tile, run_dst = _run_tables(jnp.concatenate([tcar_p, tcar_s], axis=0), cnt, blk_start, td=tr, tme=tme)
    xs_sorted = _dispatch(hnp, hns, jnp.concatenate([rt_p, rt_s], axis=1), fill_pieces, n_fill, runs_per_tile,
                          run_dst, td=tr, nb=nb, tme=tme)
    ys = _experts(xs_sorted, w1[0], b1[0], w2[0], b2[0], blk_exp, blk_src, blk_nvalid, blk_next, n_used, tme=tme)
    g_fin = final_norm_g.reshape(1, d)
    tiles_p = np_ // tr
    copy_slots = lambda rt: _by_tile(rt[0:TOP_K].astype(jnp.int32) * SUBLANES, tr)
    copy_gates = lambda rt: _by_tile(rt[TOP_K:2 * TOP_K], tr)
    y_p = _combine(x1p, copy_slots(rt_p), copy_gates(rt_p), runs_per_tile, run_dst[:tiles_p], ys, g_fin,
                   tr=tr, first=0)
    y_s = _combine(x1s, copy_slots(rt_s), copy_gates(rt_s), runs_per_tile, run_dst[tiles_p:], ys, g_fin,
                   tr=tr, first=tiles_p)

    kv5 = lambda t, bb: t.reshape(bb, -1, KV_WIDTH)[:, -WINDOW:].reshape(bb, WINDOW, N_KV_HEADS, HEAD_DIM)
    new_k_p = kv5(kfp, bp)[None]
    new_v_p = kv5(vfp, bp)[None]
    ctx = CONV_K - 1
    new_c_p = ap.reshape(bp, sp, cw)[:, -ctx:][None]
    new_c_s = jnp.concatenate([state_conv[0], as_.reshape(bs, ss, cw)], axis=1)[:, -ctx:][None]
    return (y_p.reshape(bp, sp, d), y_s.reshape(bs, ss, d), new_k_p, new_v_p, new_c_p,
            kv5(nk_s, bs)[None], kv5(nv_s, bs)[None], new_c_s)
```

```python
import functools

import jax
import jax.numpy as jnp
from jax import lax
from jax.experimental import pallas as pl
from jax.experimental.pallas import tpu as pltpu

F32 = jnp.float32
BF16 = jnp.bfloat16

HEAD_DIM = 64
N_Q_HEADS = 8
N_KV_HEADS = 2
WINDOW = 128
ROPE_THETA = 500000.0
ROPE_DIM = 16
CONV_K = 31
N_EXPERTS = 32
TOP_K = 4
SWIGLU_LIMIT = 7.0
SWIGLU_ALPHA = 1.702
RMS_EPS = 1e-5
LN_EPS = 1e-5
PAST_LEN = 16384

LANES = 128
SUBLANES = 8
CONV_HALO = 32
VMEM_LIMIT = 56 * 1024 * 1024
EXPERT_BLOCK_ROWS = 512
RUN = 8
FILL_ROWS = 64
ROUTE_T_ROWS = 2 * TOP_K
ISSUE_UNROLL = 8

ATTN_WIDTH = N_Q_HEADS * HEAD_DIM
KV_WIDTH = N_KV_HEADS * HEAD_DIM


def _tile(n, pref, mult=8):
    t = min(pref, n)
    while t > 0 and (n % t or t % mult):
        t -= 1
    assert t > 0, (n, pref, mult)
    return t


def _cparams(sem):
    return pltpu.CompilerParams(dimension_semantics=sem, vmem_limit_bytes=VMEM_LIMIT)


def _sigmoid(x):
    return 1.0 / (1.0 + jnp.exp(-x))


def _rope_tables(pos):
    half = ROPE_DIM // 2
    inv_freq = jnp.power(jnp.float32(ROPE_THETA), -jnp.arange(half, dtype=F32) * 2.0 / ROPE_DIM)
    l64 = jnp.arange(LANES) % HEAD_DIM
    assert HEAD_DIM % half == 0
    ang = pos.astype(F32)[:, None] * jnp.tile(inv_freq, LANES // half)[None, :]
    cos_l, sin_l = jnp.cos(ang), jnp.sin(ang)
    c = jnp.where(l64 < ROPE_DIM, cos_l, 1.0)
    s1 = jnp.where(l64 < half, -sin_l, 0.0)
    s2 = jnp.where((l64 >= half) & (l64 < ROPE_DIM), sin_l, 0.0)
    return c.astype(F32), s1.astype(F32), s2.astype(F32)


def _inproj_kernel(x_ref, g_ref, w_ref, c_ref, s1_ref, s2_ref,
                   q_ref, k_ref, v_ref, kf_ref, vf_ref, a_ref, *, conv_width):
    x = x_ref[...]
    ms = jnp.mean(x * x, axis=-1, keepdims=True)
    h = (x * lax.rsqrt(ms + RMS_EPS) * g_ref[...]).astype(BF16)
    z = jnp.dot(h, w_ref[...], preferred_element_type=F32)
    c, s1, s2 = c_ref[...], s1_ref[...], s2_ref[...]
    half = ROPE_DIM // 2

    def rot(t):
        return t * c + pltpu.roll(t, LANES - half, 1) * s1 + pltpu.roll(t, half, 1) * s2

    scale = HEAD_DIM ** -0.5
    for j in range(ATTN_WIDTH // LANES):
        q_ref[:, j * LANES:(j + 1) * LANES] = (rot(z[:, j * LANES:(j + 1) * LANES]) * scale).astype(q_ref.dtype)
    k0 = ATTN_WIDTH
    kr = rot(z[:, k0:k0 + KV_WIDTH])
    k_ref[...] = kr.astype(BF16)
    tail = x.shape[0] - kf_ref.shape[0]
    kf_ref[...] = kr[tail:, :]
    v0 = k0 + KV_WIDTH
    vv = z[:, v0:v0 + KV_WIDTH]
    v_ref[...] = vv.astype(BF16)
    vf_ref[...] = vv[tail:, :]
    u0 = v0 + KV_WIDTH
    g0 = u0 + conv_width
    a_ref[...] = z[:, u0:g0] * _sigmoid(z[:, g0:g0 + conv_width])


def _in_proj(x2, g, w_bf, tables, *, seq_period, q_dtype, conv_width):
    n, d = x2.shape
    in_w = w_bf.shape[1]
    row = lambda w: pl.BlockSpec((tm, w), lambda i: (i, 0))
    if seq_period is None:
        tm = tables[0].shape[0]
        tmap = lambda i: (0, 0)
        kv_f32, kv_rows = row(KV_WIDTH), n
    else:
        tm = _tile(seq_period, 1024, 16)
        per = seq_period // tm
        tmap = lambda i: (i % per, 0)
        assert tm >= WINDOW
        kv_f32, kv_rows = pl.BlockSpec((WINDOW, KV_WIDTH), lambda i: (i // per, 0)), (n // seq_period) * WINDOW
    assert n % tm == 0
    tab = pl.BlockSpec((tm, LANES), tmap)
    return pl.pallas_call(
        functools.partial(_inproj_kernel, conv_width=conv_width),
        grid=(n // tm,),
        in_specs=[row(d), pl.BlockSpec((1, d), lambda i: (0, 0)),
                  pl.BlockSpec((d, in_w), lambda i: (0, 0)), tab, tab, tab],
        out_specs=[row(ATTN_WIDTH), row(KV_WIDTH), row(KV_WIDTH), kv_f32, kv_f32, row(conv_width)],
        out_shape=[jax.ShapeDtypeStruct((n, ATTN_WIDTH), q_dtype),
                   jax.ShapeDtypeStruct((n, KV_WIDTH), BF16),
                   jax.ShapeDtypeStruct((n, KV_WIDTH), BF16),
                   jax.ShapeDtypeStruct((kv_rows, KV_WIDTH), F32),
                   jax.ShapeDtypeStruct((kv_rows, KV_WIDTH), F32),
                   jax.ShapeDtypeStruct((n, conv_width), F32)],
        compiler_params=_cparams(("arbitrary",)),
        name="in_proj",
    )(x2, g, w_bf, *tables)


def _dup_head(t, h):
    sw = pltpu.roll(t, HEAD_DIM, 1)
    low = lax.broadcasted_iota(jnp.int32, t.shape, 1) < HEAD_DIM
    return jnp.where(low, t, sw) if h == 0 else jnp.where(low, sw, t)


def _nt_dot(a, b):
    return lax.dot_general(a, b, (((1,), (1,)), ((), ())), preferred_element_type=F32)


def _attn_prompt_kernel(sink_ref, q_ref, kc_ref, kp_ref, vc_ref, vp_ref, o_ref, *, qb):
    j = pl.program_id(1)
    w = WINDOW
    k_all = jnp.concatenate([kp_ref[...], kc_ref[...]], axis=0).astype(F32)
    v_all = jnp.concatenate([vp_ref[...], vc_ref[...]], axis=0).astype(F32)
    r = lax.broadcasted_iota(jnp.int32, (w, 2 * w), 0)
    kk = lax.broadcasted_iota(jnp.int32, (w, 2 * w), 1)
    band = (kk > r) & (kk <= r + w)
    low = lax.broadcasted_iota(jnp.int32, (w, LANES), 1) < HEAD_DIM
    zero = jnp.zeros((w, LANES), BF16)
    group = N_Q_HEADS // N_KV_HEADS
    for h in range(N_KV_HEADS):
        kd_all = _dup_head(k_all, h).astype(BF16)
        vd_all = _dup_head(v_all, h).astype(BF16)
        for sub in range(qb):
            valid = band & ((kk >= w) | (j > 0)) if sub == 0 else band
            kd = kd_all[sub * w:(sub + 2) * w, :]
            vd = vd_all[sub * w:(sub + 2) * w, :]
            for jj in range(group // 2):
                col = (h * group // 2 + jj) * LANES
                qv = q_ref[sub * w:(sub + 1) * w, col:col + LANES]
                halves = []
                for half in range(2):
                    head = h * group + jj * 2 + half
                    qm = jnp.where(low if half == 0 else ~low, qv, zero)
                    s = jnp.where(valid, _nt_dot(qm, kd), -jnp.inf)
                    sink = sink_ref[head]
                    m = jnp.maximum(jnp.max(s, axis=-1, keepdims=True), sink)
                    p = jnp.exp(s - m)
                    den = jnp.sum(p, axis=-1, keepdims=True) + jnp.exp(sink - m)
                    o = jnp.dot(p.astype(BF16), vd, preferred_element_type=F32)
                    halves.append(o / den)
                o_ref[sub * w:(sub + 1) * w, col:col + LANES] = (
                    jnp.where(low, halves[0], halves[1]).astype(o_ref.dtype))


def _attn_prompt(q, k, v, sinks, *, batch, seq):
    nb = seq // WINDOW
    qb = next(c for c in (4, 2, 1) if nb % c == 0)
    steps = nb // qb
    cur = lambda b, j: (b * steps + j, 0)
    prev = lambda b, j: (b * nb + jnp.maximum(j * qb - 1, 0), 0)
    return pl.pallas_call(
        functools.partial(_attn_prompt_kernel, qb=qb),
        grid=(batch, steps),
        in_specs=[pl.BlockSpec(memory_space=pltpu.SMEM),
                  pl.BlockSpec((qb * WINDOW, ATTN_WIDTH), cur),
                  pl.BlockSpec((qb * WINDOW, KV_WIDTH), cur), pl.BlockSpec((WINDOW, KV_WIDTH), prev),
                  pl.BlockSpec((qb * WINDOW, KV_WIDTH), cur), pl.BlockSpec((WINDOW, KV_WIDTH), prev)],
        out_specs=pl.BlockSpec((qb * WINDOW, ATTN_WIDTH), cur),
        out_shape=jax.ShapeDtypeStruct((batch * seq, ATTN_WIDTH), BF16),
        compiler_params=_cparams(("parallel", "parallel")),
        name="attn_prompt",
    )(sinks, q, k, k, v, v)


def _attn_sample_kernel(sink_ref, q_ref, kn_ref, vn_ref, ck_ref, cv_ref, o_ref, nk_ref, nv_ref, *, gb, ts):
    w = WINDOW
    group = N_Q_HEADS // N_KV_HEADS
    rows = group * ts
    low = lax.broadcasted_iota(jnp.int32, (ts, LANES), 1) < HEAD_DIM
    pad = jnp.zeros((ts, LANES), F32)
    s_c, s_n, v_dup = [], [], []
    for b in range(gb):
        kc, vc = ck_ref[b], cv_ref[b]
        kn, vn = kn_ref[b * ts:(b + 1) * ts, :], vn_ref[b * ts:(b + 1) * ts, :]
        nk_ref[b, 0:w - ts, :] = kc[ts:, :]
        nk_ref[b, w - ts:, :] = kn
        nv_ref[b, 0:w - ts, :] = vc[ts:, :]
        nv_ref[b, w - ts:, :] = vn
        knp = jnp.concatenate([kn, pad], axis=0)
        vnp = jnp.concatenate([vn, pad], axis=0)
        qb = q_ref[b * ts:(b + 1) * ts, :]
        for h in range(N_KV_HEADS):
            parts = []
            for jj in range(group // 2):
                col = (h * group // 2 + jj) * LANES
                qv = qb[:, col:col + LANES]
                parts += [jnp.where(low, qv, 0.0), jnp.where(low, 0.0, qv)]
            lhs = jnp.concatenate(parts, axis=0).astype(BF16)
            s_c.append(_nt_dot(lhs, _dup_head(kc, h).astype(BF16)))
            s_n.append(_nt_dot(lhs, _dup_head(knp, h).astype(BF16)))
            v_dup.append((_dup_head(vc, h).astype(BF16), _dup_head(vnp, h).astype(BF16)))
    s_c = jnp.concatenate(s_c, axis=0)
    s_n = jnp.concatenate(s_n, axis=0)
    n_rows = s_c.shape[0]
    ridx = lax.broadcasted_iota(jnp.int32, (n_rows, 1), 0)
    t_row = ridx % ts
    head_row = (ridx // ts) % N_Q_HEADS
    sink = jnp.zeros((n_rows, 1), F32)
    for hd in range(N_Q_HEADS):
        sink = jnp.where(head_row == hd, sink_ref[hd], sink)
    c_idx = lax.broadcasted_iota(jnp.int32, (n_rows, w), 1)
    n_idx = lax.broadcasted_iota(jnp.int32, (n_rows, 2 * ts), 1)
    s_c = jnp.where(c_idx > t_row, s_c, -jnp.inf)
    s_n = jnp.where(n_idx <= t_row, s_n, -jnp.inf)
    m = jnp.maximum(jnp.maximum(jnp.max(s_c, axis=-1, keepdims=True), jnp.max(s_n, axis=-1, keepdims=True)), sink)
    p_c = jnp.exp(s_c - m)
    p_n = jnp.exp(s_n - m)
    den = jnp.sum(p_c, axis=-1, keepdims=True) + jnp.sum(p_n, axis=-1, keepdims=True) + jnp.exp(sink - m)
    p_c = p_c.astype(BF16)
    p_n = p_n.astype(BF16)
    outs = []
    for b in range(gb):
        cols = []
        for h in range(N_KV_HEADS):
            ci = b * N_KV_HEADS + h
            sl = slice(ci * rows, (ci + 1) * rows)
            vdc, vdn = v_dup[ci]
            o = (jnp.dot(p_c[sl], vdc, preferred_element_type=F32)
                 + jnp.dot(p_n[sl], vdn, preferred_element_type=F32)) / den[sl]
            for jj in range(group // 2):
                lo_part = o[(2 * jj) * ts:(2 * jj + 1) * ts, :]
                hi_part = o[(2 * jj + 1) * ts:(2 * jj + 2) * ts, :]
                cols.append(jnp.where(low, lo_part, hi_part))
        outs.append(jnp.concatenate(cols, axis=1))
    o_ref[...] = jnp.concatenate(outs, axis=0).astype(o_ref.dtype)


def _attn_sample(q, kf, vf, cache_k, cache_v, sinks, *, batch, ts):
    assert ts % 8 == 0 and ts <= WINDOW
    gb = _tile(batch, 16, 2)
    tok = lambda w: pl.BlockSpec((gb * ts, w), lambda i: (i, 0))
    cache = pl.BlockSpec((gb, WINDOW, KV_WIDTH), lambda i: (i, 0, 0))
    cshape = jax.ShapeDtypeStruct((batch, WINDOW, KV_WIDTH), F32)
    return pl.pallas_call(
        functools.partial(_attn_sample_kernel, gb=gb, ts=ts),
        grid=(batch // gb,),
        in_specs=[pl.BlockSpec(memory_space=pltpu.SMEM), tok(ATTN_WIDTH), tok(KV_WIDTH), tok(KV_WIDTH),
                  cache, cache],
        out_specs=[tok(ATTN_WIDTH), cache, cache],
        out_shape=[jax.ShapeDtypeStruct((batch * ts, ATTN_WIDTH), BF16), cshape, cshape],
        compiler_params=_cparams(("parallel",)),
        name="attn_sample",
    )(sinks, q, kf, vf, cache_k, cache_v)


def _ln_swish(acc, b, lg, lb):
    y = acc + b
    mu = jnp.mean(y, axis=-1, keepdims=True)
    yc = y - mu
    var = jnp.mean(yc * yc, axis=-1, keepdims=True)
    yn = yc * lax.rsqrt(var + LN_EPS) * lg + lb
    return yn * _sigmoid(yn)


def _conv_prompt_kernel(a_ref, ap_ref, w_ref, b_ref, lg_ref, lb_ref, o_ref, win_ref, *, tt, rc):
    j = pl.program_id(1)
    n = CONV_HALO + tt
    win = jnp.concatenate([jnp.where(j > 0, ap_ref[...], 0.0), a_ref[...]], axis=0)
    win_ref[0] = win
    for r in range(1, SUBLANES):
        win_ref[r] = pltpu.roll(win, n - r, 0)
    off = CONV_HALO - (CONV_K - 1)
    b, lg, lb = b_ref[...], lg_ref[...], lb_ref[...]
    for c in range(tt // rc):
        acc = jnp.zeros((rc, a_ref.shape[1]), F32)
        for k in range(CONV_K):
            s = off + k
            base = c * rc + (s // SUBLANES) * SUBLANES
            wk = jnp.concatenate([w_ref[k]] * (rc // SUBLANES), axis=0)
            acc = acc + wk * win_ref[s % SUBLANES, base:base + rc, :]
        o_ref[c * rc:(c + 1) * rc, :] = _ln_swish(acc, b, lg, lb).astype(o_ref.dtype)


def _conv_prompt(a, w, b, lg, lb, *, batch, seq):
    cw = a.shape[1]
    tt = _tile(seq, 512, CONV_HALO)
    rc = _tile(tt, 32, 16)
    nt = seq // tt
    per = tt // CONV_HALO
    cur = lambda bb, j: (bb * nt + j, 0)
    prev = lambda bb, j: (jnp.maximum((bb * nt + j) * per - 1, 0), 0)
    vec = pl.BlockSpec((1, cw), lambda bb, j: (0, 0))
    return pl.pallas_call(
        functools.partial(_conv_prompt_kernel, tt=tt, rc=rc),
        grid=(batch, nt),
        in_specs=[pl.BlockSpec((tt, cw), cur), pl.BlockSpec((CONV_HALO, cw), prev),
                  pl.BlockSpec((CONV_K, SUBLANES, cw), lambda bb, j: (0, 0, 0)), vec, vec, vec],
        out_specs=pl.BlockSpec((tt, cw), cur),
        out_shape=jax.ShapeDtypeStruct((batch * seq, cw), BF16),
        scratch_shapes=[pltpu.VMEM((SUBLANES, CONV_HALO + tt, cw), F32)],
        compiler_params=_cparams(("parallel", "parallel")),
        name="conv_prompt",
    )(a, a, jnp.broadcast_to(w[:, None, :], (CONV_K, SUBLANES, cw)), b, lg, lb)


def _conv_sample_kernel(a_ref, st_ref, w_ref, b_ref, lg_ref, lb_ref, o_ref, win_ref, *, gb, ts):
    ctx = CONV_K - 1
    b, lg, lb = b_ref[...], lg_ref[...], lb_ref[...]
    for bb in range(gb):
        win_ref[bb, 0:ctx, :] = st_ref[bb]
        win_ref[bb, ctx:ctx + ts, :] = a_ref[bb * ts:(bb + 1) * ts, :]
    outs = []
    for bb in range(gb):
        acc = jnp.zeros((ts, a_ref.shape[1]), F32)
        for k in range(CONV_K):
            wk = jnp.concatenate([w_ref[k]] * (ts // SUBLANES), axis=0)
            acc = acc + wk * win_ref[bb, k:k + ts, :]
        outs.append(_ln_swish(acc, b, lg, lb))
    o_ref[...] = jnp.concatenate(outs, axis=0).astype(o_ref.dtype)


def _conv_sample(a, state, w, b, lg, lb, *, batch, ts):
    cw = a.shape[1]
    ctx = CONV_K - 1
    gb = _tile(batch, 16, 2)
    vec = pl.BlockSpec((1, cw), lambda i: (0, 0))
    return pl.pallas_call(
        functools.partial(_conv_sample_kernel, gb=gb, ts=ts),
        grid=(batch // gb,),
        in_specs=[pl.BlockSpec((gb * ts, cw), lambda i: (i, 0)),
                  pl.BlockSpec((gb, ctx, cw), lambda i: (i, 0, 0)),
                  pl.BlockSpec((CONV_K, SUBLANES, cw), lambda i: (0, 0, 0)), vec, vec, vec],
        out_specs=pl.BlockSpec((gb * ts, cw), lambda i: (i, 0)),
        out_shape=jax.ShapeDtypeStruct((batch * ts, cw), BF16),
        scratch_shapes=[pltpu.VMEM((gb, ctx + ts + 2, cw), F32)],
        compiler_params=_cparams(("parallel",)),
        name="conv_sample",
    )(a, state, jnp.broadcast_to(w[:, None, :], (CONV_K, SUBLANES, cw)), b, lg, lb)


def _outproj_router_kernel(x_ref, att_ref, cv_ref, wo_ref, g_ref, rw_ref, rb_ref, cin_ref,
                           x1_ref, hn_ref, rt_ref, tcar_ref, cnt_ref, carry_ref):
    i = pl.program_id(0)

    @pl.when(i == 0)
    def _():
        carry_ref[...] = cin_ref[...]

    tm = x_ref.shape[0]
    aw = att_ref.shape[1]
    n_exp = rw_ref.shape[0]
    r = lax.broadcasted_iota(jnp.int32, (tm, tm), 0)
    c = lax.broadcasted_iota(jnp.int32, (tm, tm), 1)
    before = (r < c).astype(BF16)
    eid = lax.broadcasted_iota(jnp.int32, (n_exp, tm), 0)
    carry = carry_ref[...][:, 0:1]
    tcar_ref[0] = carry_ref[...]
    mix = (jnp.dot(att_ref[...], wo_ref[0:aw, :], preferred_element_type=F32)
           + jnp.dot(cv_ref[...], wo_ref[aw:, :], preferred_element_type=F32))
    x1 = x_ref[...] + mix
    x1_ref[...] = x1
    ms = jnp.mean(x1 * x1, axis=-1, keepdims=True)
    hn = (x1 * lax.rsqrt(ms + RMS_EPS) * g_ref[...]).astype(BF16)
    hn_ref[...] = hn
    logits = _nt_dot(rw_ref[...], hn) + rb_ref[...][:, 0:1]
    onehot = jnp.zeros((n_exp, tm), F32)
    vals, idxs = [], []
    for _ in range(TOP_K):
        m = jnp.max(logits, axis=0, keepdims=True)
        idx = jnp.min(jnp.where(logits == m, eid, n_exp), axis=0, keepdims=True)
        sel = eid == idx
        onehot = onehot + sel.astype(F32)
        logits = jnp.where(sel, -jnp.inf, logits)
        vals.append(m)
        idxs.append(idx)
    es = [jnp.exp(v - vals[0]) for v in vals]
    den = es[0] + es[1] + es[2] + es[3]
    local = jnp.dot(onehot.astype(BF16), before, preferred_element_type=F32)
    count = jnp.sum(onehot, axis=1, keepdims=True)
    padded = jnp.ceil(count * (1.0 / RUN)) * RUN
    er = lax.broadcasted_iota(jnp.int32, (n_exp, n_exp), 0)
    ec = lax.broadcasted_iota(jnp.int32, (n_exp, n_exp), 1)
    start = jnp.dot((ec < er).astype(BF16), jnp.broadcast_to(padded, (n_exp, LANES)).astype(BF16),
                    preferred_element_type=F32)[:, 0:1]
    slot_of = local + start
    slots = [jnp.sum(jnp.where(eid == idxs[k], slot_of, 0.0), axis=0, keepdims=True) for k in range(TOP_K)]
    rt_ref[...] = jnp.concatenate(slots + [e / den for e in es], axis=0)
    carry = carry + count
    carry_ref[...] = jnp.broadcast_to(carry, carry_ref.shape)
    cnt_ref[...] = jnp.broadcast_to(carry, cnt_ref.shape)


def _store_slabs(ref, val, rows):
    for j in range(val.shape[1] // LANES):
        ref[pl.ds(j, rows, stride=SUBLANES), :] = val[:, j * LANES:(j + 1) * LANES]


def _load_slabs(ref, rows, dtype):
    return jnp.concatenate([ref[pl.ds(j, rows, stride=SUBLANES), :].astype(dtype) for j in range(SUBLANES)],
                           axis=1)


def _outproj_router(x2, att, cv, wo_bf, g, rwt_bf, rbt, carry_in, *, tm):
    n, d = x2.shape
    assert d == SUBLANES * LANES and n % tm == 0
    n_exp = rwt_bf.shape[0]
    row = lambda w: pl.BlockSpec((tm, w), lambda i: (i, 0))
    full = lambda a: pl.BlockSpec(a.shape, lambda i: (0,) * a.ndim)
    return pl.pallas_call(
        _outproj_router_kernel,
        grid=(n // tm,),
        in_specs=[row(d), row(att.shape[1]), row(cv.shape[1]), full(wo_bf), full(g), full(rwt_bf), full(rbt),
                  full(carry_in)],
        out_specs=[row(d), row(d),
                   pl.BlockSpec((ROUTE_T_ROWS, tm), lambda i: (0, i)),
                   pl.BlockSpec((1, n_exp, LANES), lambda i: (i, 0, 0)),
                   pl.BlockSpec((n_exp, LANES), lambda i: (0, 0))],
        out_shape=[jax.ShapeDtypeStruct((n, d), F32), jax.ShapeDtypeStruct((n, d), BF16),
                   jax.ShapeDtypeStruct((ROUTE_T_ROWS, n), F32),
                   jax.ShapeDtypeStruct((n // tm, n_exp, LANES), F32),
                   jax.ShapeDtypeStruct((n_exp, LANES), F32)],
        scratch_shapes=[pltpu.VMEM((n_exp, LANES), F32)],
        compiler_params=_cparams(("arbitrary",)),
        name="outproj_router",
    )(x2, att, cv, wo_bf, g, rwt_bf, rbt, carry_in)


def _dispatch_kernel(fill_ref, nfill_ref, tot_ref, dst_ref, rt_ref, hn_ref, hn2_ref,
                     xs_ref, zero_ref, stage_ref, sem, zsem, *, td, n_first, chunk):
    i = pl.program_id(0)
    last = pl.num_programs(0) - 1
    par = i % 2
    n_slots = stage_ref.shape[1] // SUBLANES

    def fill(f, s):
        row = pl.multiple_of(fill_ref[f] * (FILL_ROWS * SUBLANES), FILL_ROWS * SUBLANES)
        return pltpu.make_async_copy(zero_ref, xs_ref.at[pl.ds(row, FILL_ROWS * SUBLANES)], zsem.at[s])

    def fills(lo, hi, s, act):
        def body(f, carry):
            act(fill(f, s))
            return carry

        lax.fori_loop(lo, hi, body, 0)

    @pl.when(i == 0)
    def _():
        zero_ref[...] = jnp.zeros(zero_ref.shape, F32)
        fills(0, nfill_ref[0], 0, lambda c: c.start())
        fills(nfill_ref[0], nfill_ref[1], 1, lambda c: c.start())
        fills(0, nfill_ref[0], 0, lambda c: c.wait())

    def group_by_expert(tok_ref):
        hn = tok_ref[...]
        slots = rt_ref[0:TOP_K, :].astype(jnp.int32)
        for c in range(n_slots // chunk):
            s = lax.broadcasted_iota(jnp.int32, (chunk, td), 0) + c * chunk
            hit = s == slots[0:1, :]
            for k in range(1, TOP_K):
                hit = hit | (s == slots[k:k + 1, :])
            rows = jnp.dot(jnp.where(hit, 1.0, 0.0).astype(BF16), hn, preferred_element_type=F32)
            _store_slabs(stage_ref.at[par, pl.ds(c * chunk * SUBLANES, chunk * SUBLANES)], rows, chunk)

    def run_copy(buf, src_slot, dst_row, runs=1):
        src = pl.multiple_of(src_slot * SUBLANES, RUN * SUBLANES)
        dst = pl.multiple_of(dst_row * SUBLANES, SUBLANES)
        return pltpu.make_async_copy(stage_ref.at[buf, pl.ds(src, runs * RUN * SUBLANES)],
                                     xs_ref.at[pl.ds(dst, runs * RUN * SUBLANES)], sem)

    def wait_runs(tile):
        def wait_many(q, carry):
            run_copy(0, 0, 0, ISSUE_UNROLL).wait()
            return carry

        def wait_one(j, carry):
            run_copy(0, 0, 0).wait()
            return carry

        n_many = tot_ref[tile] // ISSUE_UNROLL
        lax.fori_loop(0, n_many, wait_many, 0)
        lax.fori_loop(n_many * ISSUE_UNROLL, tot_ref[tile], wait_one, 0)

    @pl.when(i >= 2)
    def _():
        wait_runs(i - 2)

    @pl.when(i >= 1)
    def _():
        def issue(j):
            run_copy(1 - par, j * RUN, dst_ref[0, 0, j]).start()

        def issue_many(q, carry):
            for u in range(ISSUE_UNROLL):
                issue(q * ISSUE_UNROLL + u)
            return carry

        def issue_one(j, carry):
            issue(j)
            return carry

        n_full = tot_ref[i - 1] // ISSUE_UNROLL
        lax.fori_loop(0, n_full, issue_many, 0)
        lax.fori_loop(n_full * ISSUE_UNROLL, tot_ref[i - 1], issue_one, 0)

    @pl.when(i < n_first)
    def _():
        group_by_expert(hn_ref)

    @pl.when((i >= n_first) & (i < last))
    def _():
        group_by_expert(hn2_ref)

    @pl.when(i == last)
    def _():
        wait_runs(i - 1)
        fills(nfill_ref[0], nfill_ref[1], 1, lambda c: c.wait())


def _stage_slots(td):
    return td * TOP_K + N_EXPERTS * RUN


def _by_tile(field, td):
    n = field.shape[1]
    return field.reshape(TOP_K, n // td, td).transpose(1, 0, 2).reshape(n // td, 1, TOP_K * td)


def _dispatch(hn_a, hn_b, route_t, fill_pieces, n_fill, runs_per_tile, run_dst, *, td, nb, tme):
    na, d = hn_a.shape
    assert na % td == 0 and hn_b.shape[0] % td == 0 and td % RUN == 0
    n_first = na // td
    n_second = hn_b.shape[0] // td
    tiles = n_first + n_second
    n_slots = _stage_slots(td)
    chunk = _tile(n_slots, 768, SUBLANES)
    tok = lambda m: pl.BlockSpec((td, d), m)
    grid_spec = pltpu.PrefetchScalarGridSpec(
        num_scalar_prefetch=3,
        grid=(tiles + 1,),
        in_specs=[pl.BlockSpec((1, 1, n_slots // RUN), lambda i, *_: (jnp.maximum(i - 1, 0), 0, 0),
                               memory_space=pltpu.SMEM),
                  pl.BlockSpec((ROUTE_T_ROWS, td), lambda i, *_: (0, jnp.minimum(i, tiles - 1))),
                  tok(lambda i, *_: (jnp.minimum(i, n_first - 1), 0)),
                  tok(lambda i, *_: (jnp.clip(i - n_first, 0, n_second - 1), 0))],
        out_specs=pl.BlockSpec(memory_space=pl.ANY),
        scratch_shapes=[pltpu.VMEM((FILL_ROWS * SUBLANES, LANES), F32),
                        pltpu.VMEM((2, n_slots * SUBLANES, LANES), F32),
                        pltpu.SemaphoreType.DMA(()), pltpu.SemaphoreType.DMA((2,))],
    )
    return pl.pallas_call(
        functools.partial(_dispatch_kernel, td=td, n_first=n_first, chunk=chunk),
        grid_spec=grid_spec,
        out_shape=jax.ShapeDtypeStruct((nb * tme * SUBLANES, LANES), F32),
        compiler_params=_cparams(("arbitrary",)),
        name="dispatch",
    )(fill_pieces, n_fill, runs_per_tile, run_dst, route_t, hn_a, hn_b)


def _experts_kernel(be_ref, bsrc_ref, nv_ref, nx_ref, nu_ref, x_ref, w1_hbm, b1_ref, w2_hbm, b2_ref,
                    y_ref, w1f_ref, w2f_ref, w1b_ref, w2b_ref, par_ref, wsem, *, tme):
    i = pl.program_id(0)
    e = be_ref[i]
    e_prev = be_ref[jnp.maximum(i - 1, 0)]
    d_ff = w2f_ref.shape[1]
    half = tme // 2

    def fetch(expert, s):
        return (pltpu.make_async_copy(w1_hbm.at[expert], w1f_ref.at[s], wsem.at[0, s]),
                pltpu.make_async_copy(w2_hbm.at[expert], w2f_ref.at[s], wsem.at[1, s]))

    @pl.when(i == 0)
    def _():
        par_ref[0] = 0
        for c in fetch(e, 0):
            c.start()

    @pl.when((i == 0) | (e != e_prev))
    def _():
        s = par_ref[0]
        for c in fetch(e, s):
            c.wait()
        w1b_ref[...] = w1f_ref[s].astype(BF16)
        w2b_ref[...] = w2f_ref[s].astype(BF16)
        nxt = nx_ref[i]

        @pl.when(nxt >= 0)
        def _():
            for c in fetch(nxt, 1 - s):
                c.start()

        par_ref[0] = 1 - s

    def ffn(x):
        h = jnp.dot(x, w1b_ref[...], preferred_element_type=F32) + b1_ref[0]
        x_glu = jnp.minimum(h[:, :d_ff], SWIGLU_LIMIT)
        x_lin = jnp.clip(h[:, d_ff:], -SWIGLU_LIMIT, SWIGLU_LIMIT)
        act = x_glu * _sigmoid(SWIGLU_ALPHA * x_glu) * (x_lin + 1.0)
        return jnp.dot(act.astype(BF16), w2b_ref[...], preferred_element_type=F32) + b2_ref[0]

    used = i < nu_ref[0]
    nv = nv_ref[i]

    @pl.when(used & (nv > half))
    def _():
        _store_slabs(y_ref, ffn(_load_slabs(x_ref, tme, BF16)), tme)

    @pl.when(used & (nv <= half))
    def _():
        rows = half * SUBLANES
        _store_slabs(y_ref.at[pl.ds(0, rows)], ffn(_load_slabs(x_ref.at[pl.ds(0, rows)], half, BF16)), half)
        y_ref[pl.ds(rows, rows), :] = jnp.zeros((rows, LANES), F32)

    @pl.when(i == nu_ref[0])
    def _():
        y_ref[...] = jnp.zeros(y_ref.shape, F32)


def _experts(xs, w1, b1, w2, b2, blk_exp, blk_src, blk_nvalid, blk_next, n_used, *, tme):
    n_exp, d, h2 = w1.shape
    d_ff = w2.shape[1]
    nb = xs.shape[0] // (tme * SUBLANES)
    slab = lambda m: pl.BlockSpec((tme * SUBLANES, LANES), m)
    grid_spec = pltpu.PrefetchScalarGridSpec(
        num_scalar_prefetch=5,
        grid=(nb,),
        in_specs=[slab(lambda i, be, bs, nv, nx, nu: (bs[i], 0)),
                  pl.BlockSpec(memory_space=pl.ANY),
                  pl.BlockSpec((1, 1, h2), lambda i, be, bs, nv, nx, nu: (be[i], 0, 0)),
                  pl.BlockSpec(memory_space=pl.ANY),
                  pl.BlockSpec((1, 1, d), lambda i, be, bs, nv, nx, nu: (be[i], 0, 0))],
        out_specs=slab(lambda i, be, bs, nv, nx, nu: (jnp.minimum(i, nu[0]), 0)),
        scratch_shapes=[pltpu.VMEM((2, d, h2), F32), pltpu.VMEM((2, d_ff, d), F32),
                        pltpu.VMEM((d, h2), BF16), pltpu.VMEM((d_ff, d), BF16),
                        pltpu.SMEM((1,), jnp.int32), pltpu.SemaphoreType.DMA((2, 2))],
    )
    return pl.pallas_call(
        functools.partial(_experts_kernel, tme=tme),
        grid_spec=grid_spec,
        out_shape=jax.ShapeDtypeStruct(xs.shape, F32),
        input_output_aliases={5: 0},
        compiler_params=_cparams(("arbitrary",)),
        name="experts",
    )(blk_exp, blk_src, blk_nvalid, blk_next, n_used, xs, w1, b1.reshape(n_exp, 1, h2), w2,
      b2.reshape(n_exp, 1, d))


def _combine_kernel(tot_ref, rcur_ref, rnext_ref, slot_ref, gate_ref, x1_ref, g_ref, ys_ref, o_ref,
                    stage_ref, moe_ref, sem, *, tr, first, sub):
    i = pl.program_id(0)
    par = i % 2

    def run_copy(buf, j, src_row, runs=1):
        src = pl.multiple_of(src_row * SUBLANES, SUBLANES)
        dst = pl.multiple_of(j * (RUN * SUBLANES), RUN * SUBLANES)
        return pltpu.make_async_copy(ys_ref.at[pl.ds(src, runs * RUN * SUBLANES)],
                                     stage_ref.at[buf, pl.ds(dst, runs * RUN * SUBLANES)], sem.at[buf])

    def gather(rref, buf, n):
        def issue_many(q, carry):
            for u in range(ISSUE_UNROLL):
                j = q * ISSUE_UNROLL + u
                run_copy(buf, j, rref[0, 0, j]).start()
            return carry

        def issue_one(j, carry):
            run_copy(buf, j, rref[0, 0, j]).start()
            return carry

        lax.fori_loop(0, n // ISSUE_UNROLL, issue_many, 0)
        lax.fori_loop((n // ISSUE_UNROLL) * ISSUE_UNROLL, n, issue_one, 0)

    @pl.when(i == 0)
    def _():
        gather(rcur_ref, 0, tot_ref[first])

    @pl.when(i + 1 < pl.num_programs(0))
    def _():
        gather(rnext_ref, 1 - par, tot_ref[first + i + 1])

    def wait_many(q, carry):
        run_copy(par, 0, 0, ISSUE_UNROLL).wait()
        return carry

    def wait_one(j, carry):
        run_copy(par, 0, 0).wait()
        return carry

    n_runs = tot_ref[first + i]
    lax.fori_loop(0, n_runs // ISSUE_UNROLL, wait_many, 0)
    lax.fori_loop((n_runs // ISSUE_UNROLL) * ISSUE_UNROLL, n_runs, wait_one, 0)

    def token(buf, r, carry):
        acc = None
        for k in range(TOP_K):
            s = pl.multiple_of(slot_ref[0, 0, k * tr + r], SUBLANES)
            term = gate_ref[0, 0, k * tr + r] * stage_ref[buf, pl.ds(s, SUBLANES), :]
            acc = term if acc is None else acc + term
        moe_ref[pl.ds(pl.multiple_of(r * SUBLANES, SUBLANES), SUBLANES), :] = acc
        return carry

    for buf in range(2):
        @pl.when(par == buf)
        def _():
            lax.fori_loop(0, tr, functools.partial(token, buf), 0, unroll=ISSUE_UNROLL)

    for c in range(tr // sub):
        rows = slice(c * sub, (c + 1) * sub)
        y = x1_ref[rows, :] + _load_slabs(moe_ref.at[pl.ds(c * sub * SUBLANES, sub * SUBLANES)], sub, F32)
        ms = jnp.mean(y * y, axis=-1, keepdims=True)
        o_ref[rows, :] = y * lax.rsqrt(ms + RMS_EPS) * g_ref[...]


def _combine(x1, slots, gates, runs_per_tile, run_src, ys, g, *, tr, first):
    n, d = x1.shape
    steps = n // tr
    n_slots = _stage_slots(tr)
    smem = lambda a, m: pl.BlockSpec((1, 1, a.shape[2]), m, memory_space=pltpu.SMEM)
    cur = lambda i, tot: (i, 0, 0)
    nxt = lambda i, tot: (jnp.minimum(i + 1, steps - 1), 0, 0)
    grid_spec = pltpu.PrefetchScalarGridSpec(
        num_scalar_prefetch=1,
        grid=(steps,),
        in_specs=[smem(run_src, cur), smem(run_src, nxt), smem(slots, cur), smem(gates, cur),
                  pl.BlockSpec((tr, d), lambda i, tot: (i, 0)),
                  pl.BlockSpec((1, d), lambda i, tot: (0, 0)),
                  pl.BlockSpec(memory_space=pl.ANY)],
        out_specs=pl.BlockSpec((tr, d), lambda i, tot: (i, 0)),
        scratch_shapes=[pltpu.VMEM((2, n_slots * SUBLANES, LANES), F32), pltpu.VMEM((tr * SUBLANES, LANES), F32),
                        pltpu.SemaphoreType.DMA((2,))],
    )
    return pl.pallas_call(
        functools.partial(_combine_kernel, tr=tr, first=first, sub=_tile(tr, 128, 8)),
        grid_spec=grid_spec,
        out_shape=jax.ShapeDtypeStruct((n, d), F32),
        compiler_params=_cparams(("arbitrary",)),
        name="combine",
    )(runs_per_tile, run_src, run_src, slots, gates, x1, g, ys)


def _routing_tables(counts_f, *, tme, nb):
    counts = counts_f[:, 0].astype(jnp.int32)
    nblk = jnp.where(counts > 0, (counts + RUN - 1 + tme - 1) // tme, 0)
    blk_end = jnp.cumsum(nblk)
    blk_start = blk_end - nblk
    n_used = blk_end[-1]
    b = jnp.arange(nb, dtype=jnp.int32)
    used = b < n_used
    blk_exp = jnp.minimum(jnp.sum((b[:, None] >= blk_end[None, :]).astype(jnp.int32), axis=1), N_EXPERTS - 1)
    last_exp = jnp.max(jnp.where(nblk > 0, jnp.arange(N_EXPERTS, dtype=jnp.int32), 0))
    blk_exp = jnp.where(used, blk_exp, last_exp).astype(jnp.int32)
    blk_src = jnp.minimum(b, n_used - 1).astype(jnp.int32)
    experts = jnp.arange(N_EXPERTS, dtype=jnp.int32)
    mine = (b[:, None] >= blk_start[None, :]) & (b[:, None] < blk_end[None, :])
    nvalid = jnp.sum(jnp.where(mine, counts[None, :] - (b[:, None] - blk_start[None, :]) * tme, 0), axis=1)
    nvalid = jnp.clip(nvalid, 0, tme).astype(jnp.int32)
    later = (experts[None, :] > experts[:, None]) & (nblk[None, :] > 0)
    nxt_e = jnp.min(jnp.where(later, experts[None, :], N_EXPERTS), axis=1)
    nxt_e = jnp.where(nxt_e == N_EXPERTS, -1, nxt_e)
    blk_next = jnp.sum(jnp.where(blk_exp[:, None] == experts[None, :], nxt_e[None, :], 0), axis=1).astype(jnp.int32)
    per_blk = tme // FILL_ROWS
    piece = jnp.arange(per_blk, dtype=jnp.int32)[None, :]
    tail = (used[:, None] & (piece >= (nvalid // FILL_ROWS)[:, None])).reshape(-1)
    idle = jnp.repeat(~used, per_blk)
    fill_pieces = jnp.argsort(jnp.where(tail, 0, jnp.where(idle, 1, 2)), stable=True).astype(jnp.int32)
    n_fill = jnp.stack([jnp.sum(tail), jnp.sum(tail | idle)]).astype(jnp.int32)
    return (blk_start, blk_exp, blk_src, nvalid, blk_next, n_used.reshape(1).astype(jnp.int32), fill_pieces,
            n_fill)


def _run_tables(tile_carry_f, counts_f, blk_start, *, td, tme):
    before = tile_carry_f[:, :, 0].astype(jnp.int32)
    counts = counts_f[:, 0].astype(jnp.int32)
    in_tile = jnp.concatenate([before[1:], counts[None, :]], axis=0) - before
    n_runs = (in_tile + RUN - 1) // RUN
    ends = jnp.cumsum(n_runs, axis=1)
    j = jnp.arange(_stage_slots(td) // RUN, dtype=jnp.int32)
    owner = jnp.sum((j[None, :, None] >= ends[:, None, :]).astype(jnp.int32), axis=2)
    base = blk_start[None, :] * tme + before - (ends - n_runs) * RUN
    mine = owner[:, :, None] == jnp.arange(N_EXPERTS, dtype=jnp.int32)[None, None, :]
    dst = jnp.sum(jnp.where(mine, base[:, None, :], 0), axis=2) + j[None, :] * RUN
    return ends[:, -1].astype(jnp.int32), dst[:, None, :].astype(jnp.int32)


def kernel(x_prompt, x_sample, cache_k, cache_v, state_conv, attn_norm_g, w_in, attn_sinks, conv_w, conv_b,
           conv_ln_g, conv_ln_b, w_out, ffn_norm_g, router_w, router_b, w1, b1, w2, b2, final_norm_g):
    depth = w_in.shape[0]
    assert depth == 1, "single-layer step"
    bp, sp, d = x_prompt.shape
    bs, ss, _ = x_sample.shape
    cw = conv_w.shape[2]
    np_, ns = bp * sp, bs * ss
    n_tok = np_ + ns
    assert sp % WINDOW == 0

    xp2 = x_prompt.reshape(np_, d)
    xs2 = x_sample.reshape(ns, d)
    w_in_bf = w_in[0].astype(BF16)
    w_out_bf = w_out[0].astype(BF16)
    g_attn = attn_norm_g[0].reshape(1, d)
    g_ffn = ffn_norm_g[0].reshape(1, d)
    sinks = attn_sinks[0]
    vec = lambda a: a.reshape(1, cw)

    tab_p = _rope_tables(jnp.arange(sp, dtype=jnp.int32))
    tms = _tile(ns, 512, max(ss, 16))
    tab_s = _rope_tables(PAST_LEN + (jnp.arange(tms, dtype=jnp.int32) % ss))
    qp, kp, vp, kfp, vfp, ap = _in_proj(xp2, g_attn, w_in_bf, tab_p, seq_period=sp, q_dtype=BF16, conv_width=cw)
    qs, _, _, kfs, vfs, as_ = _in_proj(xs2, g_attn, w_in_bf, tab_s, seq_period=None, q_dtype=F32, conv_width=cw)

    att_p = _attn_prompt(qp, kp, vp, sinks, batch=bp, seq=sp)
    ck = cache_k[0].reshape(bs, WINDOW, KV_WIDTH)
    cv_ = cache_v[0].reshape(bs, WINDOW, KV_WIDTH)
    att_s, nk_s, nv_s = _attn_sample(qs, kfs, vfs, ck, cv_, sinks, batch=bs, ts=ss)

    cv_p = _conv_prompt(ap, conv_w[0], vec(conv_b[0]), vec(conv_ln_g[0]), vec(conv_ln_b[0]), batch=bp, seq=sp)
    cv_s = _conv_sample(as_, state_conv[0], conv_w[0], vec(conv_b[0]), vec(conv_ln_g[0]), vec(conv_ln_b[0]),
                        batch=bs, ts=ss)

    n_exp = router_w.shape[2]
    assert n_exp == N_EXPERTS
    rwt_bf = router_w[0].T.astype(BF16)
    rbt = jnp.broadcast_to(router_b[0][:, None], (n_exp, LANES))
    zero_carry = jnp.zeros((n_exp, LANES), F32)
    tr = _tile(ns, 512, 16)
    assert np_ % tr == 0
    x1p, hnp, rt_p, tcar_p, cnt_p = _outproj_router(xp2, att_p, cv_p, w_out_bf, g_ffn, rwt_bf, rbt, zero_carry, tm=tr)
    x1s, hns, rt_s, tcar_s, cnt = _outproj_router(xs2, att_s, cv_s, w_out_bf, g_ffn, rwt_bf, rbt, cnt_p, tm=tr)

    tme = EXPERT_BLOCK_ROWS
    nb = -(-(n_tok * TOP_K + N_EXPERTS * (tme - 1 + RUN - 1)) // tme)
    blk_start, blk_exp, blk_src, blk_nvalid, blk_next, n_used, fill_pieces, n_fill = _routing_tables(
        cnt, tme=tme, nb=nb)
    runs_per_tile, run_dst = _run_tables(jnp.concatenate([tcar_p, tcar_s], axis=0), cnt, blk_start, td=tr, tme=tme)
    xs_sorted = _dispatch(hnp, hns, jnp.concatenate([rt_p, rt_s], axis=1), fill_pieces, n_fill, runs_per_tile,
                          run_dst, td=tr, nb=nb, tme=tme)
    ys = _experts(xs_sorted, w1[0], b1[0], w2[0], b2[0], blk_exp, blk_src, blk_nvalid, blk_next, n_used, tme=tme)
    g_fin = final_norm_g.reshape(1, d)
    tiles_p = np_ // tr
    copy_slots = lambda rt: _by_tile(rt[0:TOP_K].astype(jnp.int32) * SUBLANES, tr)
    copy_gates = lambda rt: _by_tile(rt[TOP_K:2 * TOP_K], tr)
    y_p = _combine(x1p, copy_slots(rt_p), copy_gates(rt_p), runs_per_tile, run_dst[:tiles_p], ys, g_fin,
                   tr=tr, first=0)
    y_s = _combine(x1s, copy_slots(rt_s), copy_gates(rt_s), runs_per_tile, run_dst[tiles_p:], ys, g_fin,
                   tr=tr, first=tiles_p)

    kv5 = lambda t, bb: t.reshape(bb, -1, KV_WIDTH)[:, -WINDOW:].reshape(bb, WINDOW, N_KV_HEADS, HEAD_DIM)
    new_k_p = kv5(kfp, bp)[None]
    new_v_p = kv5(vfp, bp)[None]
    ctx = CONV_K - 1
    new_c_p = ap.reshape(bp, sp, cw)[:, -ctx:][None]
    new_c_s = jnp.concatenate([state_conv[0], as_.reshape(bs, ss, cw)], axis=1)[:, -ctx:][None]
    return (y_p.reshape(bp, sp, d), y_s.reshape(bs, ss, d), new_k_p, new_v_p, new_c_p,
            kv5(nk_s, bs)[None], kv5(nv_s, bs)[None], new_c_s)
```

```python
import functools

import jax
import jax.numpy as jnp
from jax import lax
from jax.experimental import pallas as pl
from jax.experimental.pallas import tpu as pltpu

F32 = jnp.float32
BF16 = jnp.bfloat16

HEAD_DIM = 64
N_Q_HEADS = 8
N_KV_HEADS = 2
WINDOW = 128
ROPE_THETA = 500000.0
ROPE_DIM = 16
CONV_K = 31
N_EXPERTS = 32
TOP_K = 4
SWIGLU_LIMIT = 7.0
SWIGLU_ALPHA = 1.702
RMS_EPS = 1e-5
LN_EPS = 1e-5
PAST_LEN = 16384

LANES = 128
SUBLANES = 8
CONV_HALO = 32
VMEM_LIMIT = 56 * 1024 * 1024
EXPERT_BLOCK_ROWS = 512
RUN = 8
FILL_ROWS = 64
ROUTE_T_ROWS = 2 * TOP_K
ISSUE_UNROLL = 8
X_RING = 3

ATTN_WIDTH = N_Q_HEADS * HEAD_DIM
KV_WIDTH = N_KV_HEADS * HEAD_DIM


def _tile(n, pref, mult=8):
    t = min(pref, n)
    while t > 0 and (n % t or t % mult):
        t -= 1
    assert t > 0, (n, pref, mult)
    return t


def _cparams(sem):
    return pltpu.CompilerParams(dimension_semantics=sem, vmem_limit_bytes=VMEM_LIMIT)


def _sigmoid(x):
    return 1.0 / (1.0 + jnp.exp(-x))


def _rope_tables(pos):
    half = ROPE_DIM // 2
    inv_freq = jnp.power(jnp.float32(ROPE_THETA), -jnp.arange(half, dtype=F32) * 2.0 / ROPE_DIM)
    l64 = jnp.arange(LANES) % HEAD_DIM
    assert HEAD_DIM % half == 0
    ang = pos.astype(F32)[:, None] * jnp.tile(inv_freq, LANES // half)[None, :]
    cos_l, sin_l = jnp.cos(ang), jnp.sin(ang)
    c = jnp.where(l64 < ROPE_DIM, cos_l, 1.0)
    s1 = jnp.where(l64 < half, -sin_l, 0.0)
    s2 = jnp.where((l64 >= half) & (l64 < ROPE_DIM), sin_l, 0.0)
    return c.astype(F32), s1.astype(F32), s2.astype(F32)


def _ring_fetch(hbm_ref, buf_ref, sem, rows, step):
    slot = step % X_RING
    return pltpu.make_async_copy(hbm_ref.at[pl.ds(pl.multiple_of(step * rows, rows), rows)], buf_ref.at[slot],
                                 sem.at[slot])


def _ring_advance(hbm_ref, buf_ref, sem, rows):
    i = pl.program_id(0)
    steps = pl.num_programs(0)

    @pl.when(i == 0)
    def _():
        for s in range(X_RING - 1):
            @pl.when(s < steps)
            def _():
                _ring_fetch(hbm_ref, buf_ref, sem, rows, s).start()

    @pl.when(i + X_RING - 1 < steps)
    def _():
        _ring_fetch(hbm_ref, buf_ref, sem, rows, i + X_RING - 1).start()

    _ring_fetch(hbm_ref, buf_ref, sem, rows, i).wait()
    return buf_ref.at[i % X_RING]


def _inproj_kernel(x_hbm, g_ref, w_ref, c_ref, s1_ref, s2_ref,
                   q_ref, k_ref, v_ref, kf_ref, vf_ref, a_ref, xbuf_ref, xsem, *, conv_width):
    x = _ring_advance(x_hbm, xbuf_ref, xsem, xbuf_ref.shape[1])[...]
    ms = jnp.mean(x * x, axis=-1, keepdims=True)
    h = (x * lax.rsqrt(ms + RMS_EPS) * g_ref[...]).astype(BF16)
    z = jnp.dot(h, w_ref[...], preferred_element_type=F32)
    c, s1, s2 = c_ref[...], s1_ref[...], s2_ref[...]
    half = ROPE_DIM // 2

    def rot(t):
        return t * c + pltpu.roll(t, LANES - half, 1) * s1 + pltpu.roll(t, half, 1) * s2

    scale = HEAD_DIM ** -0.5
    for j in range(ATTN_WIDTH // LANES):
        q_ref[:, j * LANES:(j + 1) * LANES] = (rot(z[:, j * LANES:(j + 1) * LANES]) * scale).astype(q_ref.dtype)
    k0 = ATTN_WIDTH
    kr = rot(z[:, k0:k0 + KV_WIDTH])
    k_ref[...] = kr.astype(BF16)
    tail = x.shape[0] - kf_ref.shape[0]
    kf_ref[...] = kr[tail:, :]
    v0 = k0 + KV_WIDTH
    vv = z[:, v0:v0 + KV_WIDTH]
    v_ref[...] = vv.astype(BF16)
    vf_ref[...] = vv[tail:, :]
    u0 = v0 + KV_WIDTH
    g0 = u0 + conv_width
    a_ref[...] = z[:, u0:g0] * _sigmoid(z[:, g0:g0 + conv_width])


def _in_proj(x2, g, w_bf, tables, *, seq_period, q_dtype, conv_width):
    n, d = x2.shape
    in_w = w_bf.shape[1]
    row = lambda w: pl.BlockSpec((tm, w), lambda i: (i, 0))
    if seq_period is None:
        tm = tables[0].shape[0]
        tmap = lambda i: (0, 0)
        kv_f32, kv_rows = row(KV_WIDTH), n
    else:
        tm = _tile(seq_period, 1024, 16)
        per = seq_period // tm
        tmap = lambda i: (i % per, 0)
        assert tm >= WINDOW
        kv_f32, kv_rows = pl.BlockSpec((WINDOW, KV_WIDTH), lambda i: (i // per, 0)), (n // seq_period) * WINDOW
    assert n % tm == 0
    tab = pl.BlockSpec((tm, LANES), tmap)
    return pl.pallas_call(
        functools.partial(_inproj_kernel, conv_width=conv_width),
        grid=(n // tm,),
        in_specs=[pl.BlockSpec(memory_space=pl.ANY), pl.BlockSpec((1, d), lambda i: (0, 0)),
                  pl.BlockSpec((d, in_w), lambda i: (0, 0)), tab, tab, tab],
        out_specs=[row(ATTN_WIDTH), row(KV_WIDTH), row(KV_WIDTH), kv_f32, kv_f32, row(conv_width)],
        scratch_shapes=[pltpu.VMEM((X_RING, tm, d), F32), pltpu.SemaphoreType.DMA((X_RING,))],
        out_shape=[jax.ShapeDtypeStruct((n, ATTN_WIDTH), q_dtype),
                   jax.ShapeDtypeStruct((n, KV_WIDTH), BF16),
                   jax.ShapeDtypeStruct((n, KV_WIDTH), BF16),
                   jax.ShapeDtypeStruct((kv_rows, KV_WIDTH), F32),
                   jax.ShapeDtypeStruct((kv_rows, KV_WIDTH), F32),
                   jax.ShapeDtypeStruct((n, conv_width), F32)],
        compiler_params=_cparams(("arbitrary",)),
        name="in_proj",
    )(x2, g, w_bf, *tables)


def _dup_head(t, h):
    sw = pltpu.roll(t, HEAD_DIM, 1)
    low = lax.broadcasted_iota(jnp.int32, t.shape, 1) < HEAD_DIM
    return jnp.where(low, t, sw) if h == 0 else jnp.where(low, sw, t)


def _nt_dot(a, b):
    return lax.dot_general(a, b, (((1,), (1,)), ((), ())), preferred_element_type=F32)


def _attn_prompt_kernel(sink_ref, q_ref, kc_ref, kp_ref, vc_ref, vp_ref, o_ref, *, qb):
    j = pl.program_id(1)
    w = WINDOW
    k_all = jnp.concatenate([kp_ref[...], kc_ref[...]], axis=0).astype(F32)
    v_all = jnp.concatenate([vp_ref[...], vc_ref[...]], axis=0).astype(F32)
    r = lax.broadcasted_iota(jnp.int32, (w, 2 * w), 0)
    kk = lax.broadcasted_iota(jnp.int32, (w, 2 * w), 1)
    band = (kk > r) & (kk <= r + w)
    low = lax.broadcasted_iota(jnp.int32, (w, LANES), 1) < HEAD_DIM
    zero = jnp.zeros((w, LANES), BF16)
    group = N_Q_HEADS // N_KV_HEADS
    for h in range(N_KV_HEADS):
        kd_all = _dup_head(k_all, h).astype(BF16)
        vd_all = _dup_head(v_all, h).astype(BF16)
        for sub in range(qb):
            valid = band & ((kk >= w) | (j > 0)) if sub == 0 else band
            kd = kd_all[sub * w:(sub + 2) * w, :]
            vd = vd_all[sub * w:(sub + 2) * w, :]
            for jj in range(group // 2):
                col = (h * group // 2 + jj) * LANES
                qv = q_ref[sub * w:(sub + 1) * w, col:col + LANES]
                halves = []
                for half in range(2):
                    head = h * group + jj * 2 + half
                    qm = jnp.where(low if half == 0 else ~low, qv, zero)
                    s = jnp.where(valid, _nt_dot(qm, kd), -jnp.inf)
                    sink = sink_ref[head]
                    m = jnp.maximum(jnp.max(s, axis=-1, keepdims=True), sink)
                    p = jnp.exp(s - m)
                    den = jnp.sum(p, axis=-1, keepdims=True) + jnp.exp(sink - m)
                    o = jnp.dot(p.astype(BF16), vd, preferred_element_type=F32)
                    halves.append(o / den)
                o_ref[sub * w:(sub + 1) * w, col:col + LANES] = (
                    jnp.where(low, halves[0], halves[1]).astype(o_ref.dtype))


def _attn_prompt(q, k, v, sinks, *, batch, seq):
    nb = seq // WINDOW
    qb = next(c for c in (4, 2, 1) if nb % c == 0)
    steps = nb // qb
    cur = lambda b, j: (b * steps + j, 0)
    prev = lambda b, j: (b * nb + jnp.maximum(j * qb - 1, 0), 0)
    return pl.pallas_call(
        functools.partial(_attn_prompt_kernel, qb=qb),
        grid=(batch, steps),
        in_specs=[pl.BlockSpec(memory_space=pltpu.SMEM),
                  pl.BlockSpec((qb * WINDOW, ATTN_WIDTH), cur),
                  pl.BlockSpec((qb * WINDOW, KV_WIDTH), cur), pl.BlockSpec((WINDOW, KV_WIDTH), prev),
                  pl.BlockSpec((qb * WINDOW, KV_WIDTH), cur), pl.BlockSpec((WINDOW, KV_WIDTH), prev)],
        out_specs=pl.BlockSpec((qb * WINDOW, ATTN_WIDTH), cur),
        out_shape=jax.ShapeDtypeStruct((batch * seq, ATTN_WIDTH), BF16),
        compiler_params=_cparams(("parallel", "parallel")),
        name="attn_prompt",
    )(sinks, q, k, k, v, v)


def _attn_sample_kernel(sink_ref, q_ref, kn_ref, vn_ref, ck_ref, cv_ref, o_ref, nk_ref, nv_ref, *, gb, ts):
    w = WINDOW
    group = N_Q_HEADS // N_KV_HEADS
    rows = group * ts
    low = lax.broadcasted_iota(jnp.int32, (ts, LANES), 1) < HEAD_DIM
    pad = jnp.zeros((ts, LANES), F32)
    s_c, s_n, v_dup = [], [], []
    for b in range(gb):
        kc, vc = ck_ref[b], cv_ref[b]
        kn, vn = kn_ref[b * ts:(b + 1) * ts, :], vn_ref[b * ts:(b + 1) * ts, :]
        nk_ref[b, 0:w - ts, :] = kc[ts:, :]
        nk_ref[b, w - ts:, :] = kn
        nv_ref[b, 0:w - ts, :] = vc[ts:, :]
        nv_ref[b, w - ts:, :] = vn
        knp = jnp.concatenate([kn, pad], axis=0)
        vnp = jnp.concatenate([vn, pad], axis=0)
        qb = q_ref[b * ts:(b + 1) * ts, :]
        for h in range(N_KV_HEADS):
            parts = []
            for jj in range(group // 2):
                col = (h * group // 2 + jj) * LANES
                qv = qb[:, col:col + LANES]
                parts += [jnp.where(low, qv, 0.0), jnp.where(low, 0.0, qv)]
            lhs = jnp.concatenate(parts, axis=0).astype(BF16)
            s_c.append(_nt_dot(lhs, _dup_head(kc, h).astype(BF16)))
            s_n.append(_nt_dot(lhs, _dup_head(knp, h).astype(BF16)))
            v_dup.append((_dup_head(vc, h).astype(BF16), _dup_head(vnp, h).astype(BF16)))
    s_c = jnp.concatenate(s_c, axis=0)
    s_n = jnp.concatenate(s_n, axis=0)
    n_rows = s_c.shape[0]
    ridx = lax.broadcasted_iota(jnp.int32, (n_rows, 1), 0)
    t_row = ridx % ts
    head_row = (ridx // ts) % N_Q_HEADS
    sink = jnp.zeros((n_rows, 1), F32)
    for hd in range(N_Q_HEADS):
        sink = jnp.where(head_row == hd, sink_ref[hd], sink)
    c_idx = lax.broadcasted_iota(jnp.int32, (n_rows, w), 1)
    n_idx = lax.broadcasted_iota(jnp.int32, (n_rows, 2 * ts), 1)
    s_c = jnp.where(c_idx > t_row, s_c, -jnp.inf)
    s_n = jnp.where(n_idx <= t_row, s_n, -jnp.inf)
    m = jnp.maximum(jnp.maximum(jnp.max(s_c, axis=-1, keepdims=True), jnp.max(s_n, axis=-1, keepdims=True)), sink)
    p_c = jnp.exp(s_c - m)
    p_n = jnp.exp(s_n - m)
    den = jnp.sum(p_c, axis=-1, keepdims=True) + jnp.sum(p_n, axis=-1, keepdims=True) + jnp.exp(sink - m)
    p_c = p_c.astype(BF16)
    p_n = p_n.astype(BF16)
    outs = []
    for b in range(gb):
        cols = []
        for h in range(N_KV_HEADS):
            ci = b * N_KV_HEADS + h
            sl = slice(ci * rows, (ci + 1) * rows)
            vdc, vdn = v_dup[ci]
            o = (jnp.dot(p_c[sl], vdc, preferred_element_type=F32)
                 + jnp.dot(p_n[sl], vdn, preferred_element_type=F32)) / den[sl]
            for jj in range(group // 2):
                lo_part = o[(2 * jj) * ts:(2 * jj + 1) * ts, :]
                hi_part = o[(2 * jj + 1) * ts:(2 * jj + 2) * ts, :]
                cols.append(jnp.where(low, lo_part, hi_part))
        outs.append(jnp.concatenate(cols, axis=1))
    o_ref[...] = jnp.concatenate(outs, axis=0).astype(o_ref.dtype)


def _attn_sample(q, kf, vf, cache_k, cache_v, sinks, *, batch, ts):
    assert ts % 8 == 0 and ts <= WINDOW
    gb = _tile(batch, 16, 2)
    tok = lambda w: pl.BlockSpec((gb * ts, w), lambda i: (i, 0))
    cache = pl.BlockSpec((gb, WINDOW, KV_WIDTH), lambda i: (i, 0, 0))
    cshape = jax.ShapeDtypeStruct((batch, WINDOW, KV_WIDTH), F32)
    return pl.pallas_call(
        functools.partial(_attn_sample_kernel, gb=gb, ts=ts),
        grid=(batch // gb,),
        in_specs=[pl.BlockSpec(memory_space=pltpu.SMEM), tok(ATTN_WIDTH), tok(KV_WIDTH), tok(KV_WIDTH),
                  cache, cache],
        out_specs=[tok(ATTN_WIDTH), cache, cache],
        out_shape=[jax.ShapeDtypeStruct((batch * ts, ATTN_WIDTH), BF16), cshape, cshape],
        compiler_params=_cparams(("parallel",)),
        name="attn_sample",
    )(sinks, q, kf, vf, cache_k, cache_v)


def _ln_swish(acc, b, lg, lb):
    y = acc + b
    mu = jnp.mean(y, axis=-1, keepdims=True)
    yc = y - mu
    var = jnp.mean(yc * yc, axis=-1, keepdims=True)
    yn = yc * lax.rsqrt(var + LN_EPS) * lg + lb
    return yn * _sigmoid(yn)


def _conv_prompt_kernel(a_ref, ap_ref, w_ref, b_ref, lg_ref, lb_ref, o_ref, win_ref, *, tt, rc):
    j = pl.program_id(1)
    n = CONV_HALO + tt
    win = jnp.concatenate([jnp.where(j > 0, ap_ref[...], 0.0), a_ref[...]], axis=0)
    win_ref[0] = win
    for r in range(1, SUBLANES):
        win_ref[r] = pltpu.roll(win, n - r, 0)
    off = CONV_HALO - (CONV_K - 1)
    b, lg, lb = b_ref[...], lg_ref[...], lb_ref[...]
    for c in range(tt // rc):
        acc = jnp.zeros((rc, a_ref.shape[1]), F32)
        for k in range(CONV_K):
            s = off + k
            base = c * rc + (s // SUBLANES) * SUBLANES
            wk = jnp.concatenate([w_ref[k]] * (rc // SUBLANES), axis=0)
            acc = acc + wk * win_ref[s % SUBLANES, base:base + rc, :]
        o_ref[c * rc:(c + 1) * rc, :] = _ln_swish(acc, b, lg, lb).astype(o_ref.dtype)


def _conv_prompt(a, w, b, lg, lb, *, batch, seq):
    cw = a.shape[1]
    tt = _tile(seq, 512, CONV_HALO)
    rc = _tile(tt, 32, 16)
    nt = seq // tt
    per = tt // CONV_HALO
    cur = lambda bb, j: (bb * nt + j, 0)
    prev = lambda bb, j: (jnp.maximum((bb * nt + j) * per - 1, 0), 0)
    vec = pl.BlockSpec((1, cw), lambda bb, j: (0, 0))
    return pl.pallas_call(
        functools.partial(_conv_prompt_kernel, tt=tt, rc=rc),
        grid=(batch, nt),
        in_specs=[pl.BlockSpec((tt, cw), cur), pl.BlockSpec((CONV_HALO, cw), prev),
                  pl.BlockSpec((CONV_K, SUBLANES, cw), lambda bb, j: (0, 0, 0)), vec, vec, vec],
        out_specs=pl.BlockSpec((tt, cw), cur),
        out_shape=jax.ShapeDtypeStruct((batch * seq, cw), BF16),
        scratch_shapes=[pltpu.VMEM((SUBLANES, CONV_HALO + tt, cw), F32)],
        compiler_params=_cparams(("parallel", "parallel")),
        name="conv_prompt",
    )(a, a, jnp.broadcast_to(w[:, None, :], (CONV_K, SUBLANES, cw)), b, lg, lb)


def _conv_sample_kernel(a_ref, st_ref, w_ref, b_ref, lg_ref, lb_ref, o_ref, win_ref, *, gb, ts):
    ctx = CONV_K - 1
    b, lg, lb = b_ref[...], lg_ref[...], lb_ref[...]
    for bb in range(gb):
        win_ref[bb, 0:ctx, :] = st_ref[bb]
        win_ref[bb, ctx:ctx + ts, :] = a_ref[bb * ts:(bb + 1) * ts, :]
    outs = []
    for bb in range(gb):
        acc = jnp.zeros((ts, a_ref.shape[1]), F32)
        for k in range(CONV_K):
            wk = jnp.concatenate([w_ref[k]] * (ts // SUBLANES), axis=0)
            acc = acc + wk * win_ref[bb, k:k + ts, :]
        outs.append(_ln_swish(acc, b, lg, lb))
    o_ref[...] = jnp.concatenate(outs, axis=0).astype(o_ref.dtype)


def _conv_sample(a, state, w, b, lg, lb, *, batch, ts):
    cw = a.shape[1]
    ctx = CONV_K - 1
    gb = _tile(batch, 16, 2)
    vec = pl.BlockSpec((1, cw), lambda i: (0, 0))
    return pl.pallas_call(
        functools.partial(_conv_sample_kernel, gb=gb, ts=ts),
        grid=(batch // gb,),
        in_specs=[pl.BlockSpec((gb * ts, cw), lambda i: (i, 0)),
                  pl.BlockSpec((gb, ctx, cw), lambda i: (i, 0, 0)),
                  pl.BlockSpec((CONV_K, SUBLANES, cw), lambda i: (0, 0, 0)), vec, vec, vec],
        out_specs=pl.BlockSpec((gb * ts, cw), lambda i: (i, 0)),
        out_shape=jax.ShapeDtypeStruct((batch * ts, cw), BF16),
        scratch_shapes=[pltpu.VMEM((gb, ctx + ts + 2, cw), F32)],
        compiler_params=_cparams(("parallel",)),
        name="conv_sample",
    )(a, state, jnp.broadcast_to(w[:, None, :], (CONV_K, SUBLANES, cw)), b, lg, lb)


def _outproj_router_kernel(x_hbm, att_ref, cv_ref, wo_ref, g_ref, rw_ref, rb_ref, cin_ref,
                           x1_ref, hn_ref, rt_ref, tcar_ref, cnt_ref, carry_ref, xbuf_ref, xsem):
    i = pl.program_id(0)
    tm = xbuf_ref.shape[1]

    @pl.when(i == 0)
    def _():
        carry_ref[...] = cin_ref[...]

    x_ref = _ring_advance(x_hbm, xbuf_ref, xsem, tm)
    aw = att_ref.shape[1]
    n_exp = rw_ref.shape[0]
    r = lax.broadcasted_iota(jnp.int32, (tm, tm), 0)
    c = lax.broadcasted_iota(jnp.int32, (tm, tm), 1)
    before = (r < c).astype(BF16)
    eid = lax.broadcasted_iota(jnp.int32, (n_exp, tm), 0)
    carry = carry_ref[...][:, 0:1]
    tcar_ref[0] = carry_ref[...]
    mix = (jnp.dot(att_ref[...], wo_ref[0:aw, :], preferred_element_type=F32)
           + jnp.dot(cv_ref[...], wo_ref[aw:, :], preferred_element_type=F32))
    x1 = x_ref[...] + mix
    x1_ref[...] = x1
    ms = jnp.mean(x1 * x1, axis=-1, keepdims=True)
    hn = (x1 * lax.rsqrt(ms + RMS_EPS) * g_ref[...]).astype(BF16)
    hn_ref[...] = hn
    logits = _nt_dot(rw_ref[...], hn) + rb_ref[...][:, 0:1]
    onehot = jnp.zeros((n_exp, tm), F32)
    vals, idxs = [], []
    for _ in range(TOP_K):
        m = jnp.max(logits, axis=0, keepdims=True)
        idx = jnp.min(jnp.where(logits == m, eid, n_exp), axis=0, keepdims=True)
        sel = eid == idx
        onehot = onehot + sel.astype(F32)
        logits = jnp.where(sel, -jnp.inf, logits)
        vals.append(m)
        idxs.append(idx)
    es = [jnp.exp(v - vals[0]) for v in vals]
    den = es[0] + es[1] + es[2] + es[3]
    local = jnp.dot(onehot.astype(BF16), before, preferred_element_type=F32)
    count = jnp.sum(onehot, axis=1, keepdims=True)
    padded = jnp.ceil(count * (1.0 / RUN)) * RUN
    er = lax.broadcasted_iota(jnp.int32, (n_exp, n_exp), 0)
    ec = lax.broadcasted_iota(jnp.int32, (n_exp, n_exp), 1)
    start = jnp.dot((ec < er).astype(BF16), jnp.broadcast_to(padded, (n_exp, LANES)).astype(BF16),
                    preferred_element_type=F32)[:, 0:1]
    slot_of = local + start
    slots = [jnp.sum(jnp.where(eid == idxs[k], slot_of, 0.0), axis=0, keepdims=True) for k in range(TOP_K)]
    rt_ref[...] = jnp.concatenate(slots + [e / den for e in es], axis=0)
    carry = carry + count
    carry_ref[...] = jnp.broadcast_to(carry, carry_ref.shape)
    cnt_ref[...] = jnp.broadcast_to(carry, cnt_ref.shape)


def _store_slabs(ref, val, rows):
    for j in range(val.shape[1] // LANES):
        ref[pl.ds(j, rows, stride=SUBLANES), :] = val[:, j * LANES:(j + 1) * LANES]


def _load_slabs(ref, rows, dtype):
    return jnp.concatenate([ref[pl.ds(j, rows, stride=SUBLANES), :].astype(dtype) for j in range(SUBLANES)],
                           axis=1)


def _outproj_router(x2, att, cv, wo_bf, g, rwt_bf, rbt, carry_in, *, tm):
    n, d = x2.shape
    assert d == SUBLANES * LANES and n % tm == 0
    n_exp = rwt_bf.shape[0]
    row = lambda w: pl.BlockSpec((tm, w), lambda i: (i, 0))
    full = lambda a: pl.BlockSpec(a.shape, lambda i: (0,) * a.ndim)
    return pl.pallas_call(
        _outproj_router_kernel,
        grid=(n // tm,),
        in_specs=[pl.BlockSpec(memory_space=pl.ANY), row(att.shape[1]), row(cv.shape[1]), full(wo_bf), full(g),
                  full(rwt_bf), full(rbt), full(carry_in)],
        out_specs=[row(d), row(d),
                   pl.BlockSpec((ROUTE_T_ROWS, tm), lambda i: (0, i)),
                   pl.BlockSpec((1, n_exp, LANES), lambda i: (i, 0, 0)),
                   pl.BlockSpec((n_exp, LANES), lambda i: (0, 0))],
        out_shape=[jax.ShapeDtypeStruct((n, d), F32), jax.ShapeDtypeStruct((n, d), BF16),
                   jax.ShapeDtypeStruct((ROUTE_T_ROWS, n), F32),
                   jax.ShapeDtypeStruct((n // tm, n_exp, LANES), F32),
                   jax.ShapeDtypeStruct((n_exp, LANES), F32)],
        scratch_shapes=[pltpu.VMEM((n_exp, LANES), F32), pltpu.VMEM((X_RING, tm, d), F32),
                        pltpu.SemaphoreType.DMA((X_RING,))],
        compiler_params=_cparams(("arbitrary",)),
        name="outproj_router",
    )(x2, att, cv, wo_bf, g, rwt_bf, rbt, carry_in)


def _dispatch_kernel(fill_ref, nfill_ref, tot_ref, dst_ref, rt_ref, hn_ref, hn2_ref,
                     xs_ref, zero_ref, stage_ref, sem, zsem, *, td, n_first, chunk):
    i = pl.program_id(0)
    last = pl.num_programs(0) - 1
    par = i % 2
    n_slots = stage_ref.shape[1] // SUBLANES

    def fill(f, s):
        row = pl.multiple_of(fill_ref[f] * (FILL_ROWS * SUBLANES), FILL_ROWS * SUBLANES)
        return pltpu.make_async_copy(zero_ref, xs_ref.at[pl.ds(row, FILL_ROWS * SUBLANES)], zsem.at[s])

    def fills(lo, hi, s, act):
        def body(f, carry):
            act(fill(f, s))
            return carry

        lax.fori_loop(lo, hi, body, 0)

    @pl.when(i == 0)
    def _():
        zero_ref[...] = jnp.zeros(zero_ref.shape, F32)
        fills(0, nfill_ref[0], 0, lambda c: c.start())
        fills(nfill_ref[0], nfill_ref[1], 1, lambda c: c.start())
        fills(0, nfill_ref[0], 0, lambda c: c.wait())

    def group_by_expert(tok_ref):
        hn = tok_ref[...]
        slots = rt_ref[0:TOP_K, :].astype(jnp.int32)
        for c in range(n_slots // chunk):
            s = lax.broadcasted_iota(jnp.int32, (chunk, td), 0) + c * chunk
            hit = s == slots[0:1, :]
            for k in range(1, TOP_K):
                hit = hit | (s == slots[k:k + 1, :])
            rows = jnp.dot(jnp.where(hit, 1.0, 0.0).astype(BF16), hn, preferred_element_type=F32)
            _store_slabs(stage_ref.at[par, pl.ds(c * chunk * SUBLANES, chunk * SUBLANES)], rows, chunk)

    def run_copy(buf, src_slot, dst_row, runs=1):
        src = pl.multiple_of(src_slot * SUBLANES, RUN * SUBLANES)
        dst = pl.multiple_of(dst_row * SUBLANES, SUBLANES)
        return pltpu.make_async_copy(stage_ref.at[buf, pl.ds(src, runs * RUN * SUBLANES)],
                                     xs_ref.at[pl.ds(dst, runs * RUN * SUBLANES)], sem)

    def wait_runs(tile):
        def wait_many(q, carry):
            run_copy(0, 0, 0, ISSUE_UNROLL).wait()
            return carry

        def wait_one(j, carry):
            run_copy(0, 0, 0).wait()
            return carry

        n_many = tot_ref[tile] // ISSUE_UNROLL
        lax.fori_loop(0, n_many, wait_many, 0)
        lax.fori_loop(n_many * ISSUE_UNROLL, tot_ref[tile], wait_one, 0)

    @pl.when(i >= 2)
    def _():
        wait_runs(i - 2)

    @pl.when(i >= 1)
    def _():
        def issue(j):
            run_copy(1 - par, j * RUN, dst_ref[0, 0, j]).start()

        def issue_many(q, carry):
            for u in range(ISSUE_UNROLL):
                issue(q * ISSUE_UNROLL + u)
            return carry

        def issue_one(j, carry):
            issue(j)
            return carry

        n_full = tot_ref[i - 1] // ISSUE_UNROLL
        lax.fori_loop(0, n_full, issue_many, 0)
        lax.fori_loop(n_full * ISSUE_UNROLL, tot_ref[i - 1], issue_one, 0)

    @pl.when(i < n_first)
    def _():
        group_by_expert(hn_ref)

    @pl.when((i >= n_first) & (i < last))
    def _():
        group_by_expert(hn2_ref)

    @pl.when(i == last)
    def _():
        wait_runs(i - 1)
        fills(nfill_ref[0], nfill_ref[1], 1, lambda c: c.wait())


def _stage_slots(td):
    return td * TOP_K + N_EXPERTS * RUN


def _by_tile(field, td):
    n = field.shape[1]
    return field.reshape(TOP_K, n // td, td).transpose(1, 0, 2).reshape(n // td, 1, TOP_K * td)


def _dispatch(hn_a, hn_b, route_t, fill_pieces, n_fill, runs_per_tile, run_dst, *, td, nb, tme):
    na, d = hn_a.shape
    assert na % td == 0 and hn_b.shape[0] % td == 0 and td % RUN == 0
    n_first = na // td
    n_second = hn_b.shape[0] // td
    tiles = n_first + n_second
    n_slots = _stage_slots(td)
    chunk = _tile(n_slots, 768, SUBLANES)
    tok = lambda m: pl.BlockSpec((td, d), m)
    grid_spec = pltpu.PrefetchScalarGridSpec(
        num_scalar_prefetch=3,
        grid=(tiles + 1,),
        in_specs=[pl.BlockSpec((1, 1, n_slots // RUN), lambda i, *_: (jnp.maximum(i - 1, 0), 0, 0),
                               memory_space=pltpu.SMEM),
                  pl.BlockSpec((ROUTE_T_ROWS, td), lambda i, *_: (0, jnp.minimum(i, tiles - 1))),
                  tok(lambda i, *_: (jnp.minimum(i, n_first - 1), 0)),
                  tok(lambda i, *_: (jnp.clip(i - n_first, 0, n_second - 1), 0))],
        out_specs=pl.BlockSpec(memory_space=pl.ANY),
        scratch_shapes=[pltpu.VMEM((FILL_ROWS * SUBLANES, LANES), F32),
                        pltpu.VMEM((2, n_slots * SUBLANES, LANES), F32),
                        pltpu.SemaphoreType.DMA(()), pltpu.SemaphoreType.DMA((2,))],
    )
    return pl.pallas_call(
        functools.partial(_dispatch_kernel, td=td, n_first=n_first, chunk=chunk),
        grid_spec=grid_spec,
        out_shape=jax.ShapeDtypeStruct((nb * tme * SUBLANES, LANES), F32),
        compiler_params=_cparams(("arbitrary",)),
        name="dispatch",
    )(fill_pieces, n_fill, runs_per_tile, run_dst, route_t, hn_a, hn_b)


def _experts_kernel(be_ref, bsrc_ref, nv_ref, nx_ref, nu_ref, x_ref, w1_hbm, b1_ref, w2_hbm, b2_ref,
                    y_ref, w1f_ref, w2f_ref, w1b_ref, w2b_ref, par_ref, wsem, *, tme):
    i = pl.program_id(0)
    e = be_ref[i]
    e_prev = be_ref[jnp.maximum(i - 1, 0)]
    d_ff = w2f_ref.shape[1]
    half = tme // 2

    def fetch(expert, s):
        return (pltpu.make_async_copy(w1_hbm.at[expert], w1f_ref.at[s], wsem.at[0, s]),
                pltpu.make_async_copy(w2_hbm.at[expert], w2f_ref.at[s], wsem.at[1, s]))

    @pl.when(i == 0)
    def _():
        par_ref[0] = 0
        for c in fetch(e, 0):
            c.start()

    @pl.when((i == 0) | (e != e_prev))
    def _():
        s = par_ref[0]
        for c in fetch(e, s):
            c.wait()
        w1b_ref[...] = w1f_ref[s].astype(BF16)
        w2b_ref[...] = w2f_ref[s].astype(BF16)
        nxt = nx_ref[i]

        @pl.when(nxt >= 0)
        def _():
            for c in fetch(nxt, 1 - s):
                c.start()

        par_ref[0] = 1 - s

    def ffn(x):
        h = jnp.dot(x, w1b_ref[...], preferred_element_type=F32) + b1_ref[0]
        x_glu = jnp.minimum(h[:, :d_ff], SWIGLU_LIMIT)
        x_lin = jnp.clip(h[:, d_ff:], -SWIGLU_LIMIT, SWIGLU_LIMIT)
        act = x_glu * _sigmoid(SWIGLU_ALPHA * x_glu) * (x_lin + 1.0)
        return jnp.dot(act.astype(BF16), w2b_ref[...], preferred_element_type=F32) + b2_ref[0]

    used = i < nu_ref[0]
    nv = nv_ref[i]

    @pl.when(used & (nv > half))
    def _():
        _store_slabs(y_ref, ffn(_load_slabs(x_ref, tme, BF16)), tme)

    @pl.when(used & (nv <= half))
    def _():
        rows = half * SUBLANES
        _store_slabs(y_ref.at[pl.ds(0, rows)], ffn(_load_slabs(x_ref.at[pl.ds(0, rows)], half, BF16)), half)
        y_ref[pl.ds(rows, rows), :] = jnp.zeros((rows, LANES), F32)

    @pl.when(i == nu_ref[0])
    def _():
        y_ref[...] = jnp.zeros(y_ref.shape, F32)


def _experts(xs, w1, b1, w2, b2, blk_exp, blk_src, blk_nvalid, blk_next, n_used, *, tme):
    n_exp, d, h2 = w1.shape
    d_ff = w2.shape[1]
    nb = xs.shape[0] // (tme * SUBLANES)
    slab = lambda m: pl.BlockSpec((tme * SUBLANES, LANES), m)
    grid_spec = pltpu.PrefetchScalarGridSpec(
        num_scalar_prefetch=5,
        grid=(nb,),
        in_specs=[slab(lambda i, be, bs, nv, nx, nu: (bs[i], 0)),
                  pl.BlockSpec(memory_space=pl.ANY),
                  pl.BlockSpec((1, 1, h2), lambda i, be, bs, nv, nx, nu: (be[i], 0, 0)),
                  pl.BlockSpec(memory_space=pl.ANY),
                  pl.BlockSpec((1, 1, d), lambda i, be, bs, nv, nx, nu: (be[i], 0, 0))],
        out_specs=slab(lambda i, be, bs, nv, nx, nu: (jnp.minimum(i, nu[0]), 0)),
        scratch_shapes=[pltpu.VMEM((2, d, h2), F32), pltpu.VMEM((2, d_ff, d), F32),
                        pltpu.VMEM((d, h2), BF16), pltpu.VMEM((d_ff, d), BF16),
                        pltpu.SMEM((1,), jnp.int32), pltpu.SemaphoreType.DMA((2, 2))],
    )
    return pl.pallas_call(
        functools.partial(_experts_kernel, tme=tme),
        grid_spec=grid_spec,
        out_shape=jax.ShapeDtypeStruct(xs.shape, F32),
        input_output_aliases={5: 0},
        compiler_params=_cparams(("arbitrary",)),
        name="experts",
    )(blk_exp, blk_src, blk_nvalid, blk_next, n_used, xs, w1, b1.reshape(n_exp, 1, h2), w2,
      b2.reshape(n_exp, 1, d))


def _combine_kernel(tot_ref, rcur_ref, rnext_ref, slot_ref, gate_ref, x1_ref, g_ref, ys_ref, o_ref,
                    stage_ref, moe_ref, sem, *, tr, first, sub):
    i = pl.program_id(0)
    par = i % 2

    def run_copy(buf, j, src_row, runs=1):
        src = pl.multiple_of(src_row * SUBLANES, SUBLANES)
        dst = pl.multiple_of(j * (RUN * SUBLANES), RUN * SUBLANES)
        return pltpu.make_async_copy(ys_ref.at[pl.ds(src, runs * RUN * SUBLANES)],
                                     stage_ref.at[buf, pl.ds(dst, runs * RUN * SUBLANES)], sem.at[buf])

    def gather(rref, buf, n):
        def issue_many(q, carry):
            for u in range(ISSUE_UNROLL):
                j = q * ISSUE_UNROLL + u
                run_copy(buf, j, rref[0, 0, j]).start()
            return carry

        def issue_one(j, carry):
            run_copy(buf, j, rref[0, 0, j]).start()
            return carry

        lax.fori_loop(0, n // ISSUE_UNROLL, issue_many, 0)
        lax.fori_loop((n // ISSUE_UNROLL) * ISSUE_UNROLL, n, issue_one, 0)

    @pl.when(i == 0)
    def _():
        gather(rcur_ref, 0, tot_ref[first])

    @pl.when(i + 1 < pl.num_programs(0))
    def _():
        gather(rnext_ref, 1 - par, tot_ref[first + i + 1])

    def wait_many(q, carry):
        run_copy(par, 0, 0, ISSUE_UNROLL).wait()
        return carry

    def wait_one(j, carry):
        run_copy(par, 0, 0).wait()
        return carry

    n_runs = tot_ref[first + i]
    lax.fori_loop(0, n_runs // ISSUE_UNROLL, wait_many, 0)
    lax.fori_loop((n_runs // ISSUE_UNROLL) * ISSUE_UNROLL, n_runs, wait_one, 0)

    def token(buf, r, carry):
        acc = None
        for k in range(TOP_K):
            s = pl.multiple_of(slot_ref[0, 0, k * tr + r], SUBLANES)
            term = gate_ref[0, 0, k * tr + r] * stage_ref[buf, pl.ds(s, SUBLANES), :]
            acc = term if acc is None else acc + term
        moe_ref[pl.ds(pl.multiple_of(r * SUBLANES, SUBLANES), SUBLANES), :] = acc
        return carry

    for buf in range(2):
        @pl.when(par == buf)
        def _():
            lax.fori_loop(0, tr, functools.partial(token, buf), 0, unroll=ISSUE_UNROLL)

    for c in range(tr // sub):
        rows = slice(c * sub, (c + 1) * sub)
        y = x1_ref[rows, :] + _load_slabs(moe_ref.at[pl.ds(c * sub * SUBLANES, sub * SUBLANES)], sub, F32)
        ms = jnp.mean(y * y, axis=-1, keepdims=True)
        o_ref[rows, :] = y * lax.rsqrt(ms + RMS_EPS) * g_ref[...]


def _combine(x1, slots, gates, runs_per_tile, run_src, ys, g, *, tr, first):
    n, d = x1.shape
    steps = n // tr
    n_slots = _stage_slots(tr)
    smem = lambda a, m: pl.BlockSpec((1, 1, a.shape[2]), m, memory_space=pltpu.SMEM)
    cur = lambda i, tot: (i, 0, 0)
    nxt = lambda i, tot: (jnp.minimum(i + 1, steps - 1), 0, 0)
    grid_spec = pltpu.PrefetchScalarGridSpec(
        num_scalar_prefetch=1,
        grid=(steps,),
        in_specs=[smem(run_src, cur), smem(run_src, nxt), smem(slots, cur), smem(gates, cur),
                  pl.BlockSpec((tr, d), lambda i, tot: (i, 0)),
                  pl.BlockSpec((1, d), lambda i, tot: (0, 0)),
                  pl.BlockSpec(memory_space=pl.ANY)],
        out_specs=pl.BlockSpec((tr, d), lambda i, tot: (i, 0)),
        scratch_shapes=[pltpu.VMEM((2, n_slots * SUBLANES, LANES), F32), pltpu.VMEM((tr * SUBLANES, LANES), F32),
                        pltpu.SemaphoreType.DMA((2,))],
    )
    return pl.pallas_call(
        functools.partial(_combine_kernel, tr=tr, first=first, sub=_tile(tr, 128, 8)),
        grid_spec=grid_spec,
        out_shape=jax.ShapeDtypeStruct((n, d), F32),
        compiler_params=_cparams(("arbitrary",)),
        name="combine",
    )(runs_per_tile, run_src, run_src, slots, gates, x1, g, ys)


def _routing_tables(counts_f, *, tme, nb):
    counts = counts_f[:, 0].astype(jnp.int32)
    nblk = jnp.where(counts > 0, (counts + RUN - 1 + tme - 1) // tme, 0)
    blk_end = jnp.cumsum(nblk)
    blk_start = blk_end - nblk
    n_used = blk_end[-1]
    b = jnp.arange(nb, dtype=jnp.int32)
    used = b < n_used
    blk_exp = jnp.minimum(jnp.sum((b[:, None] >= blk_end[None, :]).astype(jnp.int32), axis=1), N_EXPERTS - 1)
    last_exp = jnp.max(jnp.where(nblk > 0, jnp.arange(N_EXPERTS, dtype=jnp.int32), 0))
    blk_exp = jnp.where(used, blk_exp, last_exp).astype(jnp.int32)
    blk_src = jnp.minimum(b, n_used - 1).astype(jnp.int32)
    experts = jnp.arange(N_EXPERTS, dtype=jnp.int32)
    mine = (b[:, None] >= blk_start[None, :]) & (b[:, None] < blk_end[None, :])
    nvalid = jnp.sum(jnp.where(mine, counts[None, :] - (b[:, None] - blk_start[None, :]) * tme, 0), axis=1)
    nvalid = jnp.clip(nvalid, 0, tme).astype(jnp.int32)
    later = (experts[None, :] > experts[:, None]) & (nblk[None, :] > 0)
    nxt_e = jnp.min(jnp.where(later, experts[None, :], N_EXPERTS), axis=1)
    nxt_e = jnp.where(nxt_e == N_EXPERTS, -1, nxt_e)
    blk_next = jnp.sum(jnp.where(blk_exp[:, None] == experts[None, :], nxt_e[None, :], 0), axis=1).astype(jnp.int32)
    per_blk = tme // FILL_ROWS
    piece = jnp.arange(per_blk, dtype=jnp.int32)[None, :]
    tail = (used[:, None] & (piece >= (nvalid // FILL_ROWS)[:, None])).reshape(-1)
    idle = jnp.repeat(~used, per_blk)
    fill_pieces = jnp.argsort(jnp.where(tail, 0, jnp.where(idle, 1, 2)), stable=True).astype(jnp.int32)
    n_fill = jnp.stack([jnp.sum(tail), jnp.sum(tail | idle)]).astype(jnp.int32)
    return (blk_start, blk_exp, blk_src, nvalid, blk_next, n_used.reshape(1).astype(jnp.int32), fill_pieces,
            n_fill)


def _run_tables(tile_carry_f, counts_f, blk_start, *, td, tme):
    before = tile_carry_f[:, :, 0].astype(jnp.int32)
    counts = counts_f[:, 0].astype(jnp.int32)
    in_tile = jnp.concatenate([before[1:], counts[None, :]], axis=0) - before
    n_runs = (in_tile + RUN - 1) // RUN
    ends = jnp.cumsum(n_runs, axis=1)
    j = jnp.arange(_stage_slots(td) // RUN, dtype=jnp.int32)
    owner = jnp.sum((j[None, :, None] >= ends[:, None, :]).astype(jnp.int32), axis=2)
    base = blk_start[None, :] * tme + before - (ends - n_runs) * RUN
    mine = owner[:, :, None] == jnp.arange(N_EXPERTS, dtype=jnp.int32)[None, None, :]
    dst = jnp.sum(jnp.where(mine, base[:, None, :], 0), axis=2) + j[None, :] * RUN
    return ends[:, -1].astype(jnp.int32), dst[:, None, :].astype(jnp.int32)


def kernel(x_prompt, x_sample, cache_k, cache_v, state_conv, attn_norm_g, w_in, attn_sinks, conv_w, conv_b,
           conv_ln_g, conv_ln_b, w_out, ffn_norm_g, router_w, router_b, w1, b1, w2, b2, final_norm_g):
    depth = w_in.shape[0]
    assert depth == 1, "single-layer step"
    bp, sp, d = x_prompt.shape
    bs, ss, _ = x_sample.shape
    cw = conv_w.shape[2]
    np_, ns = bp * sp, bs * ss
    n_tok = np_ + ns
    assert sp % WINDOW == 0

    xp2 = x_prompt.reshape(np_, d)
    xs2 = x_sample.reshape(ns, d)
    w_in_bf = w_in[0].astype(BF16)
    w_out_bf = w_out[0].astype(BF16)
    g_attn = attn_norm_g[0].reshape(1, d)
    g_ffn = ffn_norm_g[0].reshape(1, d)
    sinks = attn_sinks[0]
    vec = lambda a: a.reshape(1, cw)

    tab_p = _rope_tables(jnp.arange(sp, dtype=jnp.int32))
    tms = _tile(ns, 512, max(ss, 16))
    tab_s = _rope_tables(PAST_LEN + (jnp.arange(tms, dtype=jnp.int32) % ss))
    qp, kp, vp, kfp, vfp, ap = _in_proj(xp2, g_attn, w_in_bf, tab_p, seq_period=sp, q_dtype=BF16, conv_width=cw)
    qs, _, _, kfs, vfs, as_ = _in_proj(xs2, g_attn, w_in_bf, tab_s, seq_period=None, q_dtype=F32, conv_width=cw)

    att_p = _attn_prompt(qp, kp, vp, sinks, batch=bp, seq=sp)
    ck = cache_k[0].reshape(bs, WINDOW, KV_WIDTH)
    cv_ = cache_v[0].reshape(bs, WINDOW, KV_WIDTH)
    att_s, nk_s, nv_s = _attn_sample(qs, kfs, vfs, ck, cv_, sinks, batch=bs, ts=ss)

    cv_p = _conv_prompt(ap, conv_w[0], vec(conv_b[0]), vec(conv_ln_g[0]), vec(conv_ln_b[0]), batch=bp, seq=sp)
    cv_s = _conv_sample(as_, state_conv[0], conv_w[0], vec(conv_b[0]), vec(conv_ln_g[0]), vec(conv_ln_b[0]),
                        batch=bs, ts=ss)

    n_exp = router_w.shape[2]
    assert n_exp == N_EXPERTS
    rwt_bf = router_w[0].T.astype(BF16)
    rbt = jnp.broadcast_to(router_b[0][:, None], (n_exp, LANES))
    zero_carry = jnp.zeros((n_exp, LANES), F32)
    tr = _tile(ns, 512, 16)
    assert np_ % tr == 0
    x1p, hnp, rt_p, tcar_p, cnt_p = _outproj_router(xp2, att_p, cv_p, w_out_bf, g_ffn, rwt_bf, rbt, zero_carry, tm=tr)
    x1s, hns, rt_s, tcar_s, cnt = _outproj_router(xs2, att_s, cv_s, w_out_bf, g_ffn, rwt_bf, rbt, cnt_p, tm=tr)

    tme = EXPERT_BLOCK_ROWS
    nb = -(-(n_tok * TOP_K + N_EXPERTS * (tme - 1 + RUN - 1)) // tme)
    blk_start, blk_exp, blk_src, blk_nvalid, blk_next, n_used, fill_pieces, n_fill = _routing_tables(
        cnt, tme=tme, nb=nb)
    runs_per_tile, run_dst = _run_tables(jnp.concatenate([tcar_p, tcar_s], axis=0), cnt, blk_start, td=tr, tme=tme)
    xs_sorted = _dispatch(hnp, hns, jnp.concatenate([rt_p, rt_s], axis=1), fill_pieces, n_fill, runs_per_tile,
                          run_dst, td=tr, nb=nb, tme=tme)
    ys = _experts(xs_sorted, w1[0], b1[0], w2[0], b2[0], blk_exp, blk_src, blk_nvalid, blk_next, n_used, tme=tme)
    g_fin = final_norm_g.reshape(1, d)
    tiles_p = np_ // tr
    copy_slots = lambda rt: _by_tile(rt[0:TOP_K].astype(jnp.int32) * SUBLANES, tr)
    copy_gates = lambda rt: _by_tile(rt[TOP_K:2 * TOP_K], tr)
    y_p = _combine(x1p, copy_slots(rt_p), copy_gates(rt_p), runs_per_tile, run_dst[:tiles_p], ys, g_fin,
                   tr=tr, first=0)
    y_s = _combine(x1s, copy_slots(rt_s), copy_gates(rt_s), runs_per_tile, run_dst[tiles_p:], ys, g_fin,
                   tr=tr, first=tiles_p)

    kv5 = lambda t, bb: t.reshape(bb, -1, KV_WIDTH)[:, -WINDOW:].reshape(bb, WINDOW, N_KV_HEADS, HEAD_DIM)
    new_k_p = kv5(kfp, bp)[None]
    new_v_p = kv5(vfp, bp)[None]
    ctx = CONV_K - 1
    new_c_p = ap.reshape(bp, sp, cw)[:, -ctx:][None]
    new_c_s = jnp.concatenate([state_conv[0], as_.reshape(bs, ss, cw)], axis=1)[:, -ctx:][None]
    return (y_p.reshape(bp, sp, d), y_s.reshape(bs, ss, d), new_k_p, new_v_p, new_c_p,
            kv5(nk_s, bs)[None], kv5(nv_s, bs)[None], new_c_s)
```

```python
import functools

import jax
import jax.numpy as jnp
from jax import lax
from jax.experimental import pallas as pl
from jax.experimental.pallas import tpu as pltpu

F32 = jnp.float32
BF16 = jnp.bfloat16

HEAD_DIM = 64
N_Q_HEADS = 8
N_KV_HEADS = 2
WINDOW = 128
ROPE_THETA = 500000.0
ROPE_DIM = 16
CONV_K = 31
N_EXPERTS = 32
TOP_K = 4
SWIGLU_LIMIT = 7.0
SWIGLU_ALPHA = 1.702
RMS_EPS = 1e-5
LN_EPS = 1e-5
PAST_LEN = 16384

LANES = 128
SUBLANES = 8
CONV_HALO = 32
VMEM_LIMIT = 56 * 1024 * 1024
EXPERT_BLOCK_ROWS = 512
RUN = 16
FILL_ROWS = 64
ROUTE_T_ROWS = 2 * TOP_K
ISSUE_UNROLL = 8
X_RING = 3

ATTN_WIDTH = N_Q_HEADS * HEAD_DIM
KV_WIDTH = N_KV_HEADS * HEAD_DIM


def _tile(n, pref, mult=8):
    t = min(pref, n)
    while t > 0 and (n % t or t % mult):
        t -= 1
    assert t > 0, (n, pref, mult)
    return t


def _cparams(sem):
    return pltpu.CompilerParams(dimension_semantics=sem, vmem_limit_bytes=VMEM_LIMIT)


def _sigmoid(x):
    return 1.0 / (1.0 + jnp.exp(-x))


def _rope_tables(pos):
    half = ROPE_DIM // 2
    inv_freq = jnp.power(jnp.float32(ROPE_THETA), -jnp.arange(half, dtype=F32) * 2.0 / ROPE_DIM)
    l64 = jnp.arange(LANES) % HEAD_DIM
    assert HEAD_DIM % half == 0
    ang = pos.astype(F32)[:, None] * jnp.tile(inv_freq, LANES // half)[None, :]
    cos_l, sin_l = jnp.cos(ang), jnp.sin(ang)
    c = jnp.where(l64 < ROPE_DIM, cos_l, 1.0)
    s1 = jnp.where(l64 < half, -sin_l, 0.0)
    s2 = jnp.where((l64 >= half) & (l64 < ROPE_DIM), sin_l, 0.0)
    return c.astype(F32), s1.astype(F32), s2.astype(F32)


def _inproj_kernel(x_ref, g_ref, w_ref, c_ref, s1_ref, s2_ref,
                   q_ref, k_ref, v_ref, kf_ref, vf_ref, a_ref, *, conv_width):
    x = x_ref[...]
    ms = jnp.mean(x * x, axis=-1, keepdims=True)
    h = (x * lax.rsqrt(ms + RMS_EPS) * g_ref[...]).astype(BF16)
    z = jnp.dot(h, w_ref[...], preferred_element_type=F32)
    c, s1, s2 = c_ref[...], s1_ref[...], s2_ref[...]
    half = ROPE_DIM // 2

    def rot(t):
        return t * c + pltpu.roll(t, LANES - half, 1) * s1 + pltpu.roll(t, half, 1) * s2

    scale = HEAD_DIM ** -0.5
    for j in range(ATTN_WIDTH // LANES):
        q_ref[:, j * LANES:(j + 1) * LANES] = (rot(z[:, j * LANES:(j + 1) * LANES]) * scale).astype(q_ref.dtype)
    k0 = ATTN_WIDTH
    kr = rot(z[:, k0:k0 + KV_WIDTH])
    k_ref[...] = kr.astype(BF16)
    tail = x.shape[0] - kf_ref.shape[0]
    kf_ref[...] = kr[tail:, :]
    v0 = k0 + KV_WIDTH
    vv = z[:, v0:v0 + KV_WIDTH]
    v_ref[...] = vv.astype(BF16)
    vf_ref[...] = vv[tail:, :]
    u0 = v0 + KV_WIDTH
    g0 = u0 + conv_width
    a_ref[...] = z[:, u0:g0] * _sigmoid(z[:, g0:g0 + conv_width])


def _in_proj(x2, g, w_bf, tables, *, seq_period, q_dtype, conv_width):
    n, d = x2.shape
    in_w = w_bf.shape[1]
    row = lambda w: pl.BlockSpec((tm, w), lambda i: (i, 0))
    if seq_period is None:
        tm = tables[0].shape[0]
        tmap = lambda i: (0, 0)
        kv_f32, kv_rows = row(KV_WIDTH), n
    else:
        tm = _tile(seq_period, 1024, 16)
        per = seq_period // tm
        tmap = lambda i: (i % per, 0)
        assert tm >= WINDOW
        kv_f32, kv_rows = pl.BlockSpec((WINDOW, KV_WIDTH), lambda i: (i // per, 0)), (n // seq_period) * WINDOW
    assert n % tm == 0
    tab = pl.BlockSpec((tm, LANES), tmap)
    return pl.pallas_call(
        functools.partial(_inproj_kernel, conv_width=conv_width),
        grid=(n // tm,),
        in_specs=[row(d), pl.BlockSpec((1, d), lambda i: (0, 0)),
                  pl.BlockSpec((d, in_w), lambda i: (0, 0)), tab, tab, tab],
        out_specs=[row(ATTN_WIDTH), row(KV_WIDTH), row(KV_WIDTH), kv_f32, kv_f32, row(conv_width)],
        out_shape=[jax.ShapeDtypeStruct((n, ATTN_WIDTH), q_dtype),
                   jax.ShapeDtypeStruct((n, KV_WIDTH), BF16),
                   jax.ShapeDtypeStruct((n, KV_WIDTH), BF16),
                   jax.ShapeDtypeStruct((kv_rows, KV_WIDTH), F32),
                   jax.ShapeDtypeStruct((kv_rows, KV_WIDTH), F32),
                   jax.ShapeDtypeStruct((n, conv_width), F32)],
        compiler_params=_cparams(("arbitrary",)),
        name="in_proj",
    )(x2, g, w_bf, *tables)


def _dup_head(t, h):
    sw = pltpu.roll(t, HEAD_DIM, 1)
    low = lax.broadcasted_iota(jnp.int32, t.shape, 1) < HEAD_DIM
    return jnp.where(low, t, sw) if h == 0 else jnp.where(low, sw, t)


def _nt_dot(a, b):
    return lax.dot_general(a, b, (((1,), (1,)), ((), ())), preferred_element_type=F32)


def _attn_prompt_kernel(sink_ref, q_ref, kc_ref, kp_ref, vc_ref, vp_ref, o_ref, *, qb):
    j = pl.program_id(1)
    w = WINDOW
    k_all = jnp.concatenate([kp_ref[...], kc_ref[...]], axis=0).astype(F32)
    v_all = jnp.concatenate([vp_ref[...], vc_ref[...]], axis=0).astype(F32)
    r = lax.broadcasted_iota(jnp.int32, (w, 2 * w), 0)
    kk = lax.broadcasted_iota(jnp.int32, (w, 2 * w), 1)
    band = (kk > r) & (kk <= r + w)
    low = lax.broadcasted_iota(jnp.int32, (w, LANES), 1) < HEAD_DIM
    zero = jnp.zeros((w, LANES), BF16)
    group = N_Q_HEADS // N_KV_HEADS
    for h in range(N_KV_HEADS):
        kd_all = _dup_head(k_all, h).astype(BF16)
        vd_all = _dup_head(v_all, h).astype(BF16)
        for sub in range(qb):
            valid = band & ((kk >= w) | (j > 0)) if sub == 0 else band
            kd = kd_all[sub * w:(sub + 2) * w, :]
            vd = vd_all[sub * w:(sub + 2) * w, :]
            for jj in range(group // 2):
                col = (h * group // 2 + jj) * LANES
                qv = q_ref[sub * w:(sub + 1) * w, col:col + LANES]
                halves = []
                for half in range(2):
                    head = h * group + jj * 2 + half
                    qm = jnp.where(low if half == 0 else ~low, qv, zero)
                    s = jnp.where(valid, _nt_dot(qm, kd), -jnp.inf)
                    sink = sink_ref[head]
                    m = jnp.maximum(jnp.max(s, axis=-1, keepdims=True), sink)
                    p = jnp.exp(s - m)
                    den = jnp.sum(p, axis=-1, keepdims=True) + jnp.exp(sink - m)
                    o = jnp.dot(p.astype(BF16), vd, preferred_element_type=F32)
                    halves.append(o / den)
                o_ref[sub * w:(sub + 1) * w, col:col + LANES] = (
                    jnp.where(low, halves[0], halves[1]).astype(o_ref.dtype))


def _attn_prompt(q, k, v, sinks, *, batch, seq):
    nb = seq // WINDOW
    qb = next(c for c in (4, 2, 1) if nb % c == 0)
    steps = nb // qb
    cur = lambda b, j: (b * steps + j, 0)
    prev = lambda b, j: (b * nb + jnp.maximum(j * qb - 1, 0), 0)
    return pl.pallas_call(
        functools.partial(_attn_prompt_kernel, qb=qb),
        grid=(batch, steps),
        in_specs=[pl.BlockSpec(memory_space=pltpu.SMEM),
                  pl.BlockSpec((qb * WINDOW, ATTN_WIDTH), cur),
                  pl.BlockSpec((qb * WINDOW, KV_WIDTH), cur), pl.BlockSpec((WINDOW, KV_WIDTH), prev),
                  pl.BlockSpec((qb * WINDOW, KV_WIDTH), cur), pl.BlockSpec((WINDOW, KV_WIDTH), prev)],
        out_specs=pl.BlockSpec((qb * WINDOW, ATTN_WIDTH), cur),
        out_shape=jax.ShapeDtypeStruct((batch * seq, ATTN_WIDTH), BF16),
        compiler_params=_cparams(("parallel", "parallel")),
        name="attn_prompt",
    )(sinks, q, k, k, v, v)


def _attn_sample_kernel(sink_ref, q_ref, kn_ref, vn_ref, ck_ref, cv_ref, o_ref, nk_ref, nv_ref, *, gb, ts):
    w = WINDOW
    group = N_Q_HEADS // N_KV_HEADS
    rows = group * ts
    low = lax.broadcasted_iota(jnp.int32, (ts, LANES), 1) < HEAD_DIM
    pad = jnp.zeros((ts, LANES), F32)
    s_c, s_n, v_dup = [], [], []
    for b in range(gb):
        kc, vc = ck_ref[b], cv_ref[b]
        kn, vn = kn_ref[b * ts:(b + 1) * ts, :], vn_ref[b * ts:(b + 1) * ts, :]
        nk_ref[b, 0:w - ts, :] = kc[ts:, :]
        nk_ref[b, w - ts:, :] = kn
        nv_ref[b, 0:w - ts, :] = vc[ts:, :]
        nv_ref[b, w - ts:, :] = vn
        knp = jnp.concatenate([kn, pad], axis=0)
        vnp = jnp.concatenate([vn, pad], axis=0)
        qb = q_ref[b * ts:(b + 1) * ts, :]
        for h in range(N_KV_HEADS):
            parts = []
            for jj in range(group // 2):
                col = (h * group // 2 + jj) * LANES
                qv = qb[:, col:col + LANES]
                parts += [jnp.where(low, qv, 0.0), jnp.where(low, 0.0, qv)]
            lhs = jnp.concatenate(parts, axis=0).astype(BF16)
            s_c.append(_nt_dot(lhs, _dup_head(kc, h).astype(BF16)))
            s_n.append(_nt_dot(lhs, _dup_head(knp, h).astype(BF16)))
            v_dup.append((_dup_head(vc, h).astype(BF16), _dup_head(vnp, h).astype(BF16)))
    s_c = jnp.concatenate(s_c, axis=0)
    s_n = jnp.concatenate(s_n, axis=0)
    n_rows = s_c.shape[0]
    ridx = lax.broadcasted_iota(jnp.int32, (n_rows, 1), 0)
    t_row = ridx % ts
    head_row = (ridx // ts) % N_Q_HEADS
    sink = jnp.zeros((n_rows, 1), F32)
    for hd in range(N_Q_HEADS):
        sink = jnp.where(head_row == hd, sink_ref[hd], sink)
    c_idx = lax.broadcasted_iota(jnp.int32, (n_rows, w), 1)
    n_idx = lax.broadcasted_iota(jnp.int32, (n_rows, 2 * ts), 1)
    s_c = jnp.where(c_idx > t_row, s_c, -jnp.inf)
    s_n = jnp.where(n_idx <= t_row, s_n, -jnp.inf)
    m = jnp.maximum(jnp.maximum(jnp.max(s_c, axis=-1, keepdims=True), jnp.max(s_n, axis=-1, keepdims=True)), sink)
    p_c = jnp.exp(s_c - m)
    p_n = jnp.exp(s_n - m)
    den = jnp.sum(p_c, axis=-1, keepdims=True) + jnp.sum(p_n, axis=-1, keepdims=True) + jnp.exp(sink - m)
    p_c = p_c.astype(BF16)
    p_n = p_n.astype(BF16)
    outs = []
    for b in range(gb):
        cols = []
        for h in range(N_KV_HEADS):
            ci = b * N_KV_HEADS + h
            sl = slice(ci * rows, (ci + 1) * rows)
            vdc, vdn = v_dup[ci]
            o = (jnp.dot(p_c[sl], vdc, preferred_element_type=F32)
                 + jnp.dot(p_n[sl], vdn, preferred_element_type=F32)) / den[sl]
            for jj in range(group // 2):
                lo_part = o[(2 * jj) * ts:(2 * jj + 1) * ts, :]
                hi_part = o[(2 * jj + 1) * ts:(2 * jj + 2) * ts, :]
                cols.append(jnp.where(low, lo_part, hi_part))
        outs.append(jnp.concatenate(cols, axis=1))
    o_ref[...] = jnp.concatenate(outs, axis=0).astype(o_ref.dtype)


def _attn_sample(q, kf, vf, cache_k, cache_v, sinks, *, batch, ts):
    assert ts % 8 == 0 and ts <= WINDOW
    gb = _tile(batch, 16, 2)
    tok = lambda w: pl.BlockSpec((gb * ts, w), lambda i: (i, 0))
    cache = pl.BlockSpec((gb, WINDOW, KV_WIDTH), lambda i: (i, 0, 0))
    cshape = jax.ShapeDtypeStruct((batch, WINDOW, KV_WIDTH), F32)
    return pl.pallas_call(
        functools.partial(_attn_sample_kernel, gb=gb, ts=ts),
        grid=(batch // gb,),
        in_specs=[pl.BlockSpec(memory_space=pltpu.SMEM), tok(ATTN_WIDTH), tok(KV_WIDTH), tok(KV_WIDTH),
                  cache, cache],
        out_specs=[tok(ATTN_WIDTH), cache, cache],
        out_shape=[jax.ShapeDtypeStruct((batch * ts, ATTN_WIDTH), BF16), cshape, cshape],
        compiler_params=_cparams(("parallel",)),
        name="attn_sample",
    )(sinks, q, kf, vf, cache_k, cache_v)


def _ln_swish(acc, b, lg, lb):
    y = acc + b
    mu = jnp.mean(y, axis=-1, keepdims=True)
    yc = y - mu
    var = jnp.mean(yc * yc, axis=-1, keepdims=True)
    yn = yc * lax.rsqrt(var + LN_EPS) * lg + lb
    return yn * _sigmoid(yn)


def _conv_prompt_kernel(a_ref, ap_ref, w_ref, b_ref, lg_ref, lb_ref, o_ref, win_ref, *, tt, rc):
    j = pl.program_id(1)
    n = CONV_HALO + tt
    win = jnp.concatenate([jnp.where(j > 0, ap_ref[...], 0.0), a_ref[...]], axis=0)
    win_ref[0] = win
    for r in range(1, SUBLANES):
        win_ref[r] = pltpu.roll(win, n - r, 0)
    off = CONV_HALO - (CONV_K - 1)
    b, lg, lb = b_ref[...], lg_ref[...], lb_ref[...]
    for c in range(tt // rc):
        acc = jnp.zeros((rc, a_ref.shape[1]), F32)
        for k in range(CONV_K):
            s = off + k
            base = c * rc + (s // SUBLANES) * SUBLANES
            wk = jnp.concatenate([w_ref[k]] * (rc // SUBLANES), axis=0)
            acc = acc + wk * win_ref[s % SUBLANES, base:base + rc, :]
        o_ref[c * rc:(c + 1) * rc, :] = _ln_swish(acc, b, lg, lb).astype(o_ref.dtype)


def _conv_prompt(a, w, b, lg, lb, *, batch, seq):
    cw = a.shape[1]
    tt = _tile(seq, 512, CONV_HALO)
    rc = _tile(tt, 32, 16)
    nt = seq // tt
    per = tt // CONV_HALO
    cur = lambda bb, j: (bb * nt + j, 0)
    prev = lambda bb, j: (jnp.maximum((bb * nt + j) * per - 1, 0), 0)
    vec = pl.BlockSpec((1, cw), lambda bb, j: (0, 0))
    return pl.pallas_call(
        functools.partial(_conv_prompt_kernel, tt=tt, rc=rc),
        grid=(batch, nt),
        in_specs=[pl.BlockSpec((tt, cw), cur), pl.BlockSpec((CONV_HALO, cw), prev),
                  pl.BlockSpec((CONV_K, SUBLANES, cw), lambda bb, j: (0, 0, 0)), vec, vec, vec],
        out_specs=pl.BlockSpec((tt, cw), cur),
        out_shape=jax.ShapeDtypeStruct((batch * seq, cw), BF16),
        scratch_shapes=[pltpu.VMEM((SUBLANES, CONV_HALO + tt, cw), F32)],
        compiler_params=_cparams(("parallel", "parallel")),
        name="conv_prompt",
    )(a, a, jnp.broadcast_to(w[:, None, :], (CONV_K, SUBLANES, cw)), b, lg, lb)


def _conv_sample_kernel(a_ref, st_ref, w_ref, b_ref, lg_ref, lb_ref, o_ref, win_ref, *, gb, ts):
    ctx = CONV_K - 1
    b, lg, lb = b_ref[...], lg_ref[...], lb_ref[...]
    for bb in range(gb):
        win_ref[bb, 0:ctx, :] = st_ref[bb]
        win_ref[bb, ctx:ctx + ts, :] = a_ref[bb * ts:(bb + 1) * ts, :]
    outs = []
    for bb in range(gb):
        acc = jnp.zeros((ts, a_ref.shape[1]), F32)
        for k in range(CONV_K):
            wk = jnp.concatenate([w_ref[k]] * (ts // SUBLANES), axis=0)
            acc = acc + wk * win_ref[bb, k:k + ts, :]
        outs.append(_ln_swish(acc, b, lg, lb))
    o_ref[...] = jnp.concatenate(outs, axis=0).astype(o_ref.dtype)


def _conv_sample(a, state, w, b, lg, lb, *, batch, ts):
    cw = a.shape[1]
    ctx = CONV_K - 1
    gb = _tile(batch, 16, 2)
    vec = pl.BlockSpec((1, cw), lambda i: (0, 0))
    return pl.pallas_call(
        functools.partial(_conv_sample_kernel, gb=gb, ts=ts),
        grid=(batch // gb,),
        in_specs=[pl.BlockSpec((gb * ts, cw), lambda i: (i, 0)),
                  pl.BlockSpec((gb, ctx, cw), lambda i: (i, 0, 0)),
                  pl.BlockSpec((CONV_K, SUBLANES, cw), lambda i: (0, 0, 0)), vec, vec, vec],
        out_specs=pl.BlockSpec((gb * ts, cw), lambda i: (i, 0)),
        out_shape=jax.ShapeDtypeStruct((batch * ts, cw), BF16),
        scratch_shapes=[pltpu.VMEM((gb, ctx + ts + 2, cw), F32)],
        compiler_params=_cparams(("parallel",)),
        name="conv_sample",
    )(a, state, jnp.broadcast_to(w[:, None, :], (CONV_K, SUBLANES, cw)), b, lg, lb)


def _outproj_router_kernel(x_hbm, att_ref, cv_ref, wo_ref, g_ref, rw_ref, rb_ref, cin_ref,
                           x1_ref, hn_ref, rt_ref, tcar_ref, cnt_ref, carry_ref, xbuf_ref, xsem):
    i = pl.program_id(0)
    steps = pl.num_programs(0)
    tm = xbuf_ref.shape[1]

    def fetch(step):
        slot = step % X_RING
        return pltpu.make_async_copy(x_hbm.at[pl.ds(pl.multiple_of(step * tm, tm), tm)], xbuf_ref.at[slot],
                                     xsem.at[slot])

    @pl.when(i == 0)
    def _():
        carry_ref[...] = cin_ref[...]
        for s in range(X_RING - 1):
            @pl.when(s < steps)
            def _():
                fetch(s).start()

    @pl.when(i + X_RING - 1 < steps)
    def _():
        fetch(i + X_RING - 1).start()

    fetch(i).wait()
    x_ref = xbuf_ref.at[i % X_RING]
    aw = att_ref.shape[1]
    n_exp = rw_ref.shape[0]
    r = lax.broadcasted_iota(jnp.int32, (tm, tm), 0)
    c = lax.broadcasted_iota(jnp.int32, (tm, tm), 1)
    before = (r < c).astype(BF16)
    eid = lax.broadcasted_iota(jnp.int32, (n_exp, tm), 0)
    carry = carry_ref[...][:, 0:1]
    tcar_ref[0] = carry_ref[...]
    mix = (jnp.dot(att_ref[...], wo_ref[0:aw, :], preferred_element_type=F32)
           + jnp.dot(cv_ref[...], wo_ref[aw:, :], preferred_element_type=F32))
    x1 = x_ref[...] + mix
    x1_ref[...] = x1
    ms = jnp.mean(x1 * x1, axis=-1, keepdims=True)
    hn = (x1 * lax.rsqrt(ms + RMS_EPS) * g_ref[...]).astype(BF16)
    hn_ref[...] = hn
    logits = _nt_dot(rw_ref[...], hn) + rb_ref[...][:, 0:1]
    onehot = jnp.zeros((n_exp, tm), F32)
    vals, idxs = [], []
    for _ in range(TOP_K):
        m = jnp.max(logits, axis=0, keepdims=True)
        idx = jnp.min(jnp.where(logits == m, eid, n_exp), axis=0, keepdims=True)
        sel = eid == idx
        onehot = onehot + sel.astype(F32)
        logits = jnp.where(sel, -jnp.inf, logits)
        vals.append(m)
        idxs.append(idx)
    es = [jnp.exp(v - vals[0]) for v in vals]
    den = es[0] + es[1] + es[2] + es[3]
    local = jnp.dot(onehot.astype(BF16), before, preferred_element_type=F32)
    count = jnp.sum(onehot, axis=1, keepdims=True)
    padded = jnp.ceil(count * (1.0 / RUN)) * RUN
    er = lax.broadcasted_iota(jnp.int32, (n_exp, n_exp), 0)
    ec = lax.broadcasted_iota(jnp.int32, (n_exp, n_exp), 1)
    start = jnp.dot((ec < er).astype(BF16), jnp.broadcast_to(padded, (n_exp, LANES)).astype(BF16),
                    preferred_element_type=F32)[:, 0:1]
    slot_of = local + start
    slots = [jnp.sum(jnp.where(eid == idxs[k], slot_of, 0.0), axis=0, keepdims=True) for k in range(TOP_K)]
    rt_ref[...] = jnp.concatenate(slots + [e / den for e in es], axis=0)
    carry = carry + count
    carry_ref[...] = jnp.broadcast_to(carry, carry_ref.shape)
    cnt_ref[...] = jnp.broadcast_to(carry, cnt_ref.shape)


def _store_slabs(ref, val, rows):
    for j in range(val.shape[1] // LANES):
        ref[pl.ds(j, rows, stride=SUBLANES), :] = val[:, j * LANES:(j + 1) * LANES]


def _load_slabs(ref, rows, dtype):
    return jnp.concatenate([ref[pl.ds(j, rows, stride=SUBLANES), :].astype(dtype) for j in range(SUBLANES)],
                           axis=1)


def _outproj_router(x2, att, cv, wo_bf, g, rwt_bf, rbt, carry_in, *, tm):
    n, d = x2.shape
    assert d == SUBLANES * LANES and n % tm == 0
    n_exp = rwt_bf.shape[0]
    row = lambda w: pl.BlockSpec((tm, w), lambda i: (i, 0))
    full = lambda a: pl.BlockSpec(a.shape, lambda i: (0,) * a.ndim)
    return pl.pallas_call(
        _outproj_router_kernel,
        grid=(n // tm,),
        in_specs=[pl.BlockSpec(memory_space=pl.ANY), row(att.shape[1]), row(cv.shape[1]), full(wo_bf), full(g),
                  full(rwt_bf), full(rbt), full(carry_in)],
        out_specs=[row(d), row(d),
                   pl.BlockSpec((ROUTE_T_ROWS, tm), lambda i: (0, i)),
                   pl.BlockSpec((1, n_exp, LANES), lambda i: (i, 0, 0)),
                   pl.BlockSpec((n_exp, LANES), lambda i: (0, 0))],
        out_shape=[jax.ShapeDtypeStruct((n, d), F32), jax.ShapeDtypeStruct((n, d), BF16),
                   jax.ShapeDtypeStruct((ROUTE_T_ROWS, n), F32),
                   jax.ShapeDtypeStruct((n // tm, n_exp, LANES), F32),
                   jax.ShapeDtypeStruct((n_exp, LANES), F32)],
        scratch_shapes=[pltpu.VMEM((n_exp, LANES), F32), pltpu.VMEM((X_RING, tm, d), F32),
                        pltpu.SemaphoreType.DMA((X_RING,))],
        compiler_params=_cparams(("arbitrary",)),
        name="outproj_router",
    )(x2, att, cv, wo_bf, g, rwt_bf, rbt, carry_in)


def _dispatch_kernel(fill_ref, nfill_ref, tot_ref, dst_ref, rt_ref, hn_ref, hn2_ref,
                     xs_ref, zero_ref, stage_ref, sem, zsem, *, td, n_first, chunk):
    i = pl.program_id(0)
    last = pl.num_programs(0) - 1
    par = i % 2
    n_slots = stage_ref.shape[1] // SUBLANES

    def fill(f, s):
        row = pl.multiple_of(fill_ref[f] * (FILL_ROWS * SUBLANES), FILL_ROWS * SUBLANES)
        return pltpu.make_async_copy(zero_ref, xs_ref.at[pl.ds(row, FILL_ROWS * SUBLANES)], zsem.at[s])

    def fills(lo, hi, s, act):
        def body(f, carry):
            act(fill(f, s))
            return carry

        lax.fori_loop(lo, hi, body, 0)

    @pl.when(i == 0)
    def _():
        zero_ref[...] = jnp.zeros(zero_ref.shape, F32)
        fills(0, nfill_ref[0], 0, lambda c: c.start())
        fills(nfill_ref[0], nfill_ref[1], 1, lambda c: c.start())
        fills(0, nfill_ref[0], 0, lambda c: c.wait())

    def group_by_expert(tok_ref):
        hn = tok_ref[...]
        slots = rt_ref[0:TOP_K, :].astype(jnp.int32)
        for c in range(n_slots // chunk):
            s = lax.broadcasted_iota(jnp.int32, (chunk, td), 0) + c * chunk
            hit = s == slots[0:1, :]
            for k in range(1, TOP_K):
                hit = hit | (s == slots[k:k + 1, :])
            rows = jnp.dot(jnp.where(hit, 1.0, 0.0).astype(BF16), hn, preferred_element_type=F32)
            _store_slabs(stage_ref.at[par, pl.ds(c * chunk * SUBLANES, chunk * SUBLANES)], rows, chunk)

    def run_copy(buf, src_slot, dst_row, runs=1):
        src = pl.multiple_of(src_slot * SUBLANES, RUN * SUBLANES)
        dst = pl.multiple_of(dst_row * SUBLANES, SUBLANES)
        return pltpu.make_async_copy(stage_ref.at[buf, pl.ds(src, runs * RUN * SUBLANES)],
                                     xs_ref.at[pl.ds(dst, runs * RUN * SUBLANES)], sem)

    def wait_runs(tile):
        def wait_many(q, carry):
            run_copy(0, 0, 0, ISSUE_UNROLL).wait()
            return carry

        def wait_one(j, carry):
            run_copy(0, 0, 0).wait()
            return carry

        n_many = tot_ref[tile] // ISSUE_UNROLL
        lax.fori_loop(0, n_many, wait_many, 0)
        lax.fori_loop(n_many * ISSUE_UNROLL, tot_ref[tile], wait_one, 0)

    @pl.when(i >= 2)
    def _():
        wait_runs(i - 2)

    @pl.when(i >= 1)
    def _():
        def issue(j):
            run_copy(1 - par, j * RUN, dst_ref[0, 0, j]).start()

        def issue_many(q, carry):
            for u in range(ISSUE_UNROLL):
                issue(q * ISSUE_UNROLL + u)
            return carry

        def issue_one(j, carry):
            issue(j)
            return carry

        n_full = tot_ref[i - 1] // ISSUE_UNROLL
        lax.fori_loop(0, n_full, issue_many, 0)
        lax.fori_loop(n_full * ISSUE_UNROLL, tot_ref[i - 1], issue_one, 0)

    @pl.when(i < n_first)
    def _():
        group_by_expert(hn_ref)

    @pl.when((i >= n_first) & (i < last))
    def _():
        group_by_expert(hn2_ref)

    @pl.when(i == last)
    def _():
        wait_runs(i - 1)
        fills(nfill_ref[0], nfill_ref[1], 1, lambda c: c.wait())


def _stage_slots(td):
    return td * TOP_K + N_EXPERTS * RUN


def _by_tile(field, td):
    n = field.shape[1]
    return field.reshape(TOP_K, n // td, td).transpose(1, 0, 2).reshape(n // td, 1, TOP_K * td)


def _dispatch(hn_a, hn_b, route_t, fill_pieces, n_fill, runs_per_tile, run_dst, *, td, nb, tme):
    na, d = hn_a.shape
    assert na % td == 0 and hn_b.shape[0] % td == 0 and td % RUN == 0
    n_first = na // td
    n_second = hn_b.shape[0] // td
    tiles = n_first + n_second
    n_slots = _stage_slots(td)
    chunk = _tile(n_slots, 768, SUBLANES)
    tok = lambda m: pl.BlockSpec((td, d), m)
    grid_spec = pltpu.PrefetchScalarGridSpec(
        num_scalar_prefetch=3,
        grid=(tiles + 1,),
        in_specs=[pl.BlockSpec((1, 1, n_slots // RUN), lambda i, *_: (jnp.maximum(i - 1, 0), 0, 0),
                               memory_space=pltpu.SMEM),
                  pl.BlockSpec((ROUTE_T_ROWS, td), lambda i, *_: (0, jnp.minimum(i, tiles - 1))),
                  tok(lambda i, *_: (jnp.minimum(i, n_first - 1), 0)),
                  tok(lambda i, *_: (jnp.clip(i - n_first, 0, n_second - 1), 0))],
        out_specs=pl.BlockSpec(memory_space=pl.ANY),
        scratch_shapes=[pltpu.VMEM((FILL_ROWS * SUBLANES, LANES), F32),
                        pltpu.VMEM((2, n_slots * SUBLANES, LANES), F32),
                        pltpu.SemaphoreType.DMA(()), pltpu.SemaphoreType.DMA((2,))],
    )
    return pl.pallas_call(
        functools.partial(_dispatch_kernel, td=td, n_first=n_first, chunk=chunk),
        grid_spec=grid_spec,
        out_shape=jax.ShapeDtypeStruct((nb * tme * SUBLANES, LANES), F32),
        compiler_params=_cparams(("arbitrary",)),
        name="dispatch",
    )(fill_pieces, n_fill, runs_per_tile, run_dst, route_t, hn_a, hn_b)


def _experts_kernel(be_ref, bsrc_ref, nv_ref, nx_ref, nu_ref, x_ref, w1_hbm, b1_ref, w2_hbm, b2_ref,
                    y_ref, w1f_ref, w2f_ref, w1b_ref, w2b_ref, par_ref, wsem, *, tme):
    i = pl.program_id(0)
    e = be_ref[i]
    e_prev = be_ref[jnp.maximum(i - 1, 0)]
    d_ff = w2f_ref.shape[1]
    half = tme // 2

    def fetch(expert, s):
        return (pltpu.make_async_copy(w1_hbm.at[expert], w1f_ref.at[s], wsem.at[0, s]),
                pltpu.make_async_copy(w2_hbm.at[expert], w2f_ref.at[s], wsem.at[1, s]))

    @pl.when(i == 0)
    def _():
        par_ref[0] = 0
        for c in fetch(e, 0):
            c.start()

    @pl.when((i == 0) | (e != e_prev))
    def _():
        s = par_ref[0]
        for c in fetch(e, s):
            c.wait()
        w1b_ref[...] = w1f_ref[s].astype(BF16)
        w2b_ref[...] = w2f_ref[s].astype(BF16)
        nxt = nx_ref[i]

        @pl.when(nxt >= 0)
        def _():
            for c in fetch(nxt, 1 - s):
                c.start()

        par_ref[0] = 1 - s

    def ffn(x):
        h = jnp.dot(x, w1b_ref[...], preferred_element_type=F32) + b1_ref[0]
        x_glu = jnp.minimum(h[:, :d_ff], SWIGLU_LIMIT)
        x_lin = jnp.clip(h[:, d_ff:], -SWIGLU_LIMIT, SWIGLU_LIMIT)
        act = x_glu * _sigmoid(SWIGLU_ALPHA * x_glu) * (x_lin + 1.0)
        return jnp.dot(act.astype(BF16), w2b_ref[...], preferred_element_type=F32) + b2_ref[0]

    used = i < nu_ref[0]
    nv = nv_ref[i]

    @pl.when(used & (nv > half))
    def _():
        _store_slabs(y_ref, ffn(_load_slabs(x_ref, tme, BF16)), tme)

    @pl.when(used & (nv <= half))
    def _():
        rows = half * SUBLANES
        _store_slabs(y_ref.at[pl.ds(0, rows)], ffn(_load_slabs(x_ref.at[pl.ds(0, rows)], half, BF16)), half)
        y_ref[pl.ds(rows, rows), :] = jnp.zeros((rows, LANES), F32)

    @pl.when(i == nu_ref[0])
    def _():
        y_ref[...] = jnp.zeros(y_ref.shape, F32)


def _experts(xs, w1, b1, w2, b2, blk_exp, blk_src, blk_nvalid, blk_next, n_used, *, tme):
    n_exp, d, h2 = w1.shape
    d_ff = w2.shape[1]
    nb = xs.shape[0] // (tme * SUBLANES)
    slab = lambda m: pl.BlockSpec((tme * SUBLANES, LANES), m)
    grid_spec = pltpu.PrefetchScalarGridSpec(
        num_scalar_prefetch=5,
        grid=(nb,),
        in_specs=[slab(lambda i, be, bs, nv, nx, nu: (bs[i], 0)),
                  pl.BlockSpec(memory_space=pl.ANY),
                  pl.BlockSpec((1, 1, h2), lambda i, be, bs, nv, nx, nu: (be[i], 0, 0)),
                  pl.BlockSpec(memory_space=pl.ANY),
                  pl.BlockSpec((1, 1, d), lambda i, be, bs, nv, nx, nu: (be[i], 0, 0))],
        out_specs=slab(lambda i, be, bs, nv, nx, nu: (jnp.minimum(i, nu[0]), 0)),
        scratch_shapes=[pltpu.VMEM((2, d, h2), F32), pltpu.VMEM((2, d_ff, d), F32),
                        pltpu.VMEM((d, h2), BF16), pltpu.VMEM((d_ff, d), BF16),
                        pltpu.SMEM((1,), jnp.int32), pltpu.SemaphoreType.DMA((2, 2))],
    )
    return pl.pallas_call(
        functools.partial(_experts_kernel, tme=tme),
        grid_spec=grid_spec,
        out_shape=jax.ShapeDtypeStruct(xs.shape, F32),
        input_output_aliases={5: 0},
        compiler_params=_cparams(("arbitrary",)),
        name="experts",
    )(blk_exp, blk_src, blk_nvalid, blk_next, n_used, xs, w1, b1.reshape(n_exp, 1, h2), w2,
      b2.reshape(n_exp, 1, d))


def _combine_kernel(tot_ref, rcur_ref, rnext_ref, slot_ref, gate_ref, x1_ref, g_ref, ys_ref, o_ref,
                    stage_ref, moe_ref, sem, *, tr, first, sub):
    i = pl.program_id(0)
    par = i % 2

    def run_copy(buf, j, src_row, runs=1):
        src = pl.multiple_of(src_row * SUBLANES, SUBLANES)
        dst = pl.multiple_of(j * (RUN * SUBLANES), RUN * SUBLANES)
        return pltpu.make_async_copy(ys_ref.at[pl.ds(src, runs * RUN * SUBLANES)],
                                     stage_ref.at[buf, pl.ds(dst, runs * RUN * SUBLANES)], sem.at[buf])

    def gather(rref, buf, n):
        def issue_many(q, carry):
            for u in range(ISSUE_UNROLL):
                j = q * ISSUE_UNROLL + u
                run_copy(buf, j, rref[0, 0, j]).start()
            return carry

        def issue_one(j, carry):
            run_copy(buf, j, rref[0, 0, j]).start()
            return carry

        lax.fori_loop(0, n // ISSUE_UNROLL, issue_many, 0)
        lax.fori_loop((n // ISSUE_UNROLL) * ISSUE_UNROLL, n, issue_one, 0)

    @pl.when(i == 0)
    def _():
        gather(rcur_ref, 0, tot_ref[first])

    @pl.when(i + 1 < pl.num_programs(0))
    def _():
        gather(rnext_ref, 1 - par, tot_ref[first + i + 1])

    def wait_many(q, carry):
        run_copy(par, 0, 0, ISSUE_UNROLL).wait()
        return carry

    def wait_one(j, carry):
        run_copy(par, 0, 0).wait()
        return carry

    n_runs = tot_ref[first + i]
    lax.fori_loop(0, n_runs // ISSUE_UNROLL, wait_many, 0)
    lax.fori_loop((n_runs // ISSUE_UNROLL) * ISSUE_UNROLL, n_runs, wait_one, 0)

    def token(buf, r, carry):
        acc = None
        for k in range(TOP_K):
            s = pl.multiple_of(slot_ref[0, 0, k * tr + r], SUBLANES)
            term = gate_ref[0, 0, k * tr + r] * stage_ref[buf, pl.ds(s, SUBLANES), :]
            acc = term if acc is None else acc + term
        moe_ref[pl.ds(pl.multiple_of(r * SUBLANES, SUBLANES), SUBLANES), :] = acc
        return carry

    for buf in range(2):
        @pl.when(par == buf)
        def _():
            lax.fori_loop(0, tr, functools.partial(token, buf), 0, unroll=ISSUE_UNROLL)

    for c in range(tr // sub):
        rows = slice(c * sub, (c + 1) * sub)
        y = x1_ref[rows, :] + _load_slabs(moe_ref.at[pl.ds(c * sub * SUBLANES, sub * SUBLANES)], sub, F32)
        ms = jnp.mean(y * y, axis=-1, keepdims=True)
        o_ref[rows, :] = y * lax.rsqrt(ms + RMS_EPS) * g_ref[...]


def _combine(x1, slots, gates, runs_per_tile, run_src, ys, g, *, tr, first):
    n, d = x1.shape
    steps = n // tr
    n_slots = _stage_slots(tr)
    smem = lambda a, m: pl.BlockSpec((1, 1, a.shape[2]), m, memory_space=pltpu.SMEM)
    cur = lambda i, tot: (i, 0, 0)
    nxt = lambda i, tot: (jnp.minimum(i + 1, steps - 1), 0, 0)
    grid_spec = pltpu.PrefetchScalarGridSpec(
        num_scalar_prefetch=1,
        grid=(steps,),
        in_specs=[smem(run_src, cur), smem(run_src, nxt), smem(slots, cur), smem(gates, cur),
                  pl.BlockSpec((tr, d), lambda i, tot: (i, 0)),
                  pl.BlockSpec((1, d), lambda i, tot: (0, 0)),
                  pl.BlockSpec(memory_space=pl.ANY)],
        out_specs=pl.BlockSpec((tr, d), lambda i, tot: (i, 0)),
        scratch_shapes=[pltpu.VMEM((2, n_slots * SUBLANES, LANES), F32), pltpu.VMEM((tr * SUBLANES, LANES), F32),
                        pltpu.SemaphoreType.DMA((2,))],
    )
    return pl.pallas_call(
        functools.partial(_combine_kernel, tr=tr, first=first, sub=_tile(tr, 128, 8)),
        grid_spec=grid_spec,
        out_shape=jax.ShapeDtypeStruct((n, d), F32),
        compiler_params=_cparams(("arbitrary",)),
        name="combine",
    )(runs_per_tile, run_src, run_src, slots, gates, x1, g, ys)


def _routing_tables(counts_f, *, tme, nb):
    counts = counts_f[:, 0].astype(jnp.int32)
    nblk = jnp.where(counts > 0, (counts + RUN - 1 + tme - 1) // tme, 0)
    blk_end = jnp.cumsum(nblk)
    blk_start = blk_end - nblk
    n_used = blk_end[-1]
    b = jnp.arange(nb, dtype=jnp.int32)
    used = b < n_used
    blk_exp = jnp.minimum(jnp.sum((b[:, None] >= blk_end[None, :]).astype(jnp.int32), axis=1), N_EXPERTS - 1)
    last_exp = jnp.max(jnp.where(nblk > 0, jnp.arange(N_EXPERTS, dtype=jnp.int32), 0))
    blk_exp = jnp.where(used, blk_exp, last_exp).astype(jnp.int32)
    blk_src = jnp.minimum(b, n_used - 1).astype(jnp.int32)
    experts = jnp.arange(N_EXPERTS, dtype=jnp.int32)
    mine = (b[:, None] >= blk_start[None, :]) & (b[:, None] < blk_end[None, :])
    nvalid = jnp.sum(jnp.where(mine, counts[None, :] - (b[:, None] - blk_start[None, :]) * tme, 0), axis=1)
    nvalid = jnp.clip(nvalid, 0, tme).astype(jnp.int32)
    later = (experts[None, :] > experts[:, None]) & (nblk[None, :] > 0)
    nxt_e = jnp.min(jnp.where(later, experts[None, :], N_EXPERTS), axis=1)
    nxt_e = jnp.where(nxt_e == N_EXPERTS, -1, nxt_e)
    blk_next = jnp.sum(jnp.where(blk_exp[:, None] == experts[None, :], nxt_e[None, :], 0), axis=1).astype(jnp.int32)
    per_blk = tme // FILL_ROWS
    piece = jnp.arange(per_blk, dtype=jnp.int32)[None, :]
    tail = (used[:, None] & (piece >= (nvalid // FILL_ROWS)[:, None])).reshape(-1)
    idle = jnp.repeat(~used, per_blk)
    fill_pieces = jnp.argsort(jnp.where(tail, 0, jnp.where(idle, 1, 2)), stable=True).astype(jnp.int32)
    n_fill = jnp.stack([jnp.sum(tail), jnp.sum(tail | idle)]).astype(jnp.int32)
    return (blk_start, blk_exp, blk_src, nvalid, blk_next, n_used.reshape(1).astype(jnp.int32), fill_pieces,
            n_fill)


def _run_tables(tile_carry_f, counts_f, blk_start, *, td, tme):
    before = tile_carry_f[:, :, 0].astype(jnp.int32)
    counts = counts_f[:, 0].astype(jnp.int32)
    in_tile = jnp.concatenate([before[1:], counts[None, :]], axis=0) - before
    n_runs = (in_tile + RUN - 1) // RUN
    ends = jnp.cumsum(n_runs, axis=1)
    j = jnp.arange(_stage_slots(td) // RUN, dtype=jnp.int32)
    owner = jnp.sum((j[None, :, None] >= ends[:, None, :]).astype(jnp.int32), axis=2)
    base = blk_start[None, :] * tme + before - (ends - n_runs) * RUN
    mine = owner[:, :, None] == jnp.arange(N_EXPERTS, dtype=jnp.int32)[None, None, :]
    dst = jnp.sum(jnp.where(mine, base[:, None, :], 0), axis=2) + j[None, :] * RUN
    return ends[:, -1].astype(jnp.int32), dst[:, None, :].astype(jnp.int32)


def kernel(x_prompt, x_sample, cache_k, cache_v, state_conv, attn_norm_g, w_in, attn_sinks, conv_w, conv_b,
           conv_ln_g, conv_ln_b, w_out, ffn_norm_g, router_w, router_b, w1, b1, w2, b2, final_norm_g):
    depth = w_in.shape[0]
    assert depth == 1, "single-layer step"
    bp, sp, d = x_prompt.shape
    bs, ss, _ = x_sample.shape
    cw = conv_w.shape[2]
    np_, ns = bp * sp, bs * ss
    n_tok = np_ + ns
    assert sp % WINDOW == 0

    xp2 = x_prompt.reshape(np_, d)
    xs2 = x_sample.reshape(ns, d)
    w_in_bf = w_in[0].astype(BF16)
    w_out_bf = w_out[0].astype(BF16)
    g_attn = attn_norm_g[0].reshape(1, d)
    g_ffn = ffn_norm_g[0].reshape(1, d)
    sinks = attn_sinks[0]
    vec = lambda a: a.reshape(1, cw)

    tab_p = _rope_tables(jnp.arange(sp, dtype=jnp.int32))
    tms = _tile(ns, 512, max(ss, 16))
    tab_s = _rope_tables(PAST_LEN + (jnp.arange(tms, dtype=jnp.int32) % ss))
    qp, kp, vp, kfp, vfp, ap = _in_proj(xp2, g_attn, w_in_bf, tab_p, seq_period=sp, q_dtype=BF16, conv_width=cw)
    qs, _, _, kfs, vfs, as_ = _in_proj(xs2, g_attn, w_in_bf, tab_s, seq_period=None, q_dtype=F32, conv_width=cw)

    att_p = _attn_prompt(qp, kp, vp, sinks, batch=bp, seq=sp)
    ck = cache_k[0].reshape(bs, WINDOW, KV_WIDTH)
    cv_ = cache_v[0].reshape(bs, WINDOW, KV_WIDTH)
    att_s, nk_s, nv_s = _attn_sample(qs, kfs, vfs, ck, cv_, sinks, batch=bs, ts=ss)

    cv_p = _conv_prompt(ap, conv_w[0], vec(conv_b[0]), vec(conv_ln_g[0]), vec(conv_ln_b[0]), batch=bp, seq=sp)
    cv_s = _conv_sample(as_, state_conv[0], conv_w[0], vec(conv_b[0]), vec(conv_ln_g[0]), vec(conv_ln_b[0]),
                        batch=bs, ts=ss)

    n_exp = router_w.shape[2]
    assert n_exp == N_EXPERTS
    rwt_bf = router_w[0].T.astype(BF16)
    rbt = jnp.broadcast_to(router_b[0][:, None], (n_exp, LANES))
    zero_carry = jnp.zeros((n_exp, LANES), F32)
    tr = _tile(ns, 512, 16)
    assert np_ % tr == 0
    x1p, hnp, rt_p, tcar_p, cnt_p = _outproj_router(xp2, att_p, cv_p, w_out_bf, g_ffn, rwt_bf, rbt, zero_carry, tm=tr)
    x1s, hns, rt_s, tcar_s, cnt = _outproj_router(xs2, att_s, cv_s, w_out_bf, g_ffn, rwt_bf, rbt, cnt_p, tm=tr)

    tme = EXPERT_BLOCK_ROWS
    nb = -(-(n_tok * TOP_K + N_EXPERTS * (tme - 1 + RUN - 1)) // tme)
    blk_start, blk_exp, blk_src, blk_nvalid, blk_next, n_used, fill_pieces, n_fill = _routing_tables(
        cnt, tme=tme, nb=nb)
    runs_per_tile, run_dst = _run_tables(jnp.concatenate([tcar_p, tcar_s], axis=0), cnt, blk_start, td=tr, tme=tme)
    xs_sorted = _dispatch(hnp, hns, jnp.concatenate([rt_p, rt_s], axis=1), fill_pieces, n_fill, runs_per_tile,
                          run_dst, td=tr, nb=nb, tme=tme)
    ys = _experts(xs_sorted, w1[0], b1[0], w2[0], b2[0], blk_exp, blk_src, blk_nvalid, blk_next, n_used, tme=tme)
    g_fin = final_norm_g.reshape(1, d)
    tiles_p = np_ // tr
    copy_slots = lambda rt: _by_tile(rt[0:TOP_K].astype(jnp.int32) * SUBLANES, tr)
    copy_gates = lambda rt: _by_tile(rt[TOP_K:2 * TOP_K], tr)
    y_p = _combine(x1p, copy_slots(rt_p), copy_gates(rt_p), runs_per_tile, run_dst[:tiles_p], ys, g_fin,
                   tr=tr, first=0)
    y_s = _combine(x1s, copy_slots(rt_s), copy_gates(rt_s), runs_per_tile, run_dst[tiles_p:], ys, g_fin,
                   tr=tr, first=tiles_p)

    kv5 = lambda t, bb: t.reshape(bb, -1, KV_WIDTH)[:, -WINDOW:].reshape(bb, WINDOW, N_KV_HEADS, HEAD_DIM)
    new_k_p = kv5(kfp, bp)[None]
    new_v_p = kv5(vfp, bp)[None]
    ctx = CONV_K - 1
    new_c_p = ap.reshape(bp, sp, cw)[:, -ctx:][None]
    new_c_s = jnp.concatenate([state_conv[0], as_.reshape(bs, ss, cw)], axis=1)[:, -ctx:][None]
    return (y_p.reshape(bp, sp, d), y_s.reshape(bs, ss, d), new_k_p, new_v_p, new_c_p,
            kv5(nk_s, bs)[None], kv5(nv_s, bs)[None], new_c_s)
```

```python
import functools

import jax
import jax.numpy as jnp
from jax import lax
from jax.experimental import pallas as pl
from jax.experimental.pallas import tpu as pltpu

F32 = jnp.float32
BF16 = jnp.bfloat16

HEAD_DIM = 64
N_Q_HEADS = 8
N_KV_HEADS = 2
WINDOW = 128
ROPE_THETA = 500000.0
ROPE_DIM = 16
CONV_K = 31
N_EXPERTS = 32
TOP_K = 4
SWIGLU_LIMIT = 7.0
SWIGLU_ALPHA = 1.702
RMS_EPS = 1e-5
LN_EPS = 1e-5
PAST_LEN = 16384

LANES = 128
SUBLANES = 8
CONV_HALO = 32
VMEM_LIMIT = 56 * 1024 * 1024
EXPERT_BLOCK_ROWS = 512
RUN = 8
FILL_ROWS = 64
ROUTE_T_ROWS = 2 * TOP_K
ISSUE_UNROLL = 8
X_RING = 3

ATTN_WIDTH = N_Q_HEADS * HEAD_DIM
KV_WIDTH = N_KV_HEADS * HEAD_DIM


def _tile(n, pref, mult=8):
    t = min(pref, n)
    while t > 0 and (n % t or t % mult):
        t -= 1
    assert t > 0, (n, pref, mult)
    return t


def _cparams(sem):
    return pltpu.CompilerParams(dimension_semantics=sem, vmem_limit_bytes=VMEM_LIMIT)


def _sigmoid(x):
    return 1.0 / (1.0 + jnp.exp(-x))


def _rope_tables(pos):
    half = ROPE_DIM // 2
    inv_freq = jnp.power(jnp.float32(ROPE_THETA), -jnp.arange(half, dtype=F32) * 2.0 / ROPE_DIM)
    l64 = jnp.arange(LANES) % HEAD_DIM
    assert HEAD_DIM % half == 0
    ang = pos.astype(F32)[:, None] * jnp.tile(inv_freq, LANES // half)[None, :]
    cos_l, sin_l = jnp.cos(ang), jnp.sin(ang)
    c = jnp.where(l64 < ROPE_DIM, cos_l, 1.0)
    s1 = jnp.where(l64 < half, -sin_l, 0.0)
    s2 = jnp.where((l64 >= half) & (l64 < ROPE_DIM), sin_l, 0.0)
    return c.astype(F32), s1.astype(F32), s2.astype(F32)


def _inproj_kernel(x_ref, g_ref, w_ref, c_ref, s1_ref, s2_ref,
                   q_ref, k_ref, v_ref, kf_ref, vf_ref, a_ref, *, conv_width):
    x = x_ref[...]
    ms = jnp.mean(x * x, axis=-1, keepdims=True)
    h = (x * lax.rsqrt(ms + RMS_EPS) * g_ref[...]).astype(BF16)
    z = jnp.dot(h, w_ref[...], preferred_element_type=F32)
    c, s1, s2 = c_ref[...], s1_ref[...], s2_ref[...]
    half = ROPE_DIM // 2

    def rot(t):
        return t * c + pltpu.roll(t, LANES - half, 1) * s1 + pltpu.roll(t, half, 1) * s2

    scale = HEAD_DIM ** -0.5
    for j in range(ATTN_WIDTH // LANES):
        q_ref[:, j * LANES:(j + 1) * LANES] = (rot(z[:, j * LANES:(j + 1) * LANES]) * scale).astype(q_ref.dtype)
    k0 = ATTN_WIDTH
    kr = rot(z[:, k0:k0 + KV_WIDTH])
    k_ref[...] = kr.astype(BF16)
    tail = x.shape[0] - kf_ref.shape[0]
    kf_ref[...] = kr[tail:, :]
    v0 = k0 + KV_WIDTH
    vv = z[:, v0:v0 + KV_WIDTH]
    v_ref[...] = vv.astype(BF16)
    vf_ref[...] = vv[tail:, :]
    u0 = v0 + KV_WIDTH
    g0 = u0 + conv_width
    a_ref[...] = z[:, u0:g0] * _sigmoid(z[:, g0:g0 + conv_width])


def _in_proj(x2, g, w_bf, tables, *, seq_period, q_dtype, conv_width):
    n, d = x2.shape
    in_w = w_bf.shape[1]
    row = lambda w: pl.BlockSpec((tm, w), lambda i: (i, 0))
    if seq_period is None:
        tm = tables[0].shape[0]
        tmap = lambda i: (0, 0)
        kv_f32, kv_rows = row(KV_WIDTH), n
    else:
        tm = _tile(seq_period, 1024, 16)
        per = seq_period // tm
        tmap = lambda i: (i % per, 0)
        assert tm >= WINDOW
        kv_f32, kv_rows = pl.BlockSpec((WINDOW, KV_WIDTH), lambda i: (i // per, 0)), (n // seq_period) * WINDOW
    assert n % tm == 0
    tab = pl.BlockSpec((tm, LANES), tmap)
    return pl.pallas_call(
        functools.partial(_inproj_kernel, conv_width=conv_width),
        grid=(n // tm,),
        in_specs=[row(d), pl.BlockSpec((1, d), lambda i: (0, 0)),
                  pl.BlockSpec((d, in_w), lambda i: (0, 0)), tab, tab, tab],
        out_specs=[row(ATTN_WIDTH), row(KV_WIDTH), row(KV_WIDTH), kv_f32, kv_f32, row(conv_width)],
        out_shape=[jax.ShapeDtypeStruct((n, ATTN_WIDTH), q_dtype),
                   jax.ShapeDtypeStruct((n, KV_WIDTH), BF16),
                   jax.ShapeDtypeStruct((n, KV_WIDTH), BF16),
                   jax.ShapeDtypeStruct((kv_rows, KV_WIDTH), F32),
                   jax.ShapeDtypeStruct((kv_rows, KV_WIDTH), F32),
                   jax.ShapeDtypeStruct((n, conv_width), F32)],
        compiler_params=_cparams(("arbitrary",)),
        name="in_proj",
    )(x2, g, w_bf, *tables)


def _dup_head(t, h):
    sw = pltpu.roll(t, HEAD_DIM, 1)
    low = lax.broadcasted_iota(jnp.int32, t.shape, 1) < HEAD_DIM
    return jnp.where(low, t, sw) if h == 0 else jnp.where(low, sw, t)


def _nt_dot(a, b):
    return lax.dot_general(a, b, (((1,), (1,)), ((), ())), preferred_element_type=F32)


def _attn_prompt_kernel(sink_ref, q_ref, kc_ref, kp_ref, vc_ref, vp_ref, o_ref, *, qb):
    j = pl.program_id(1)
    w = WINDOW
    k_all = jnp.concatenate([kp_ref[...], kc_ref[...]], axis=0).astype(F32)
    v_all = jnp.concatenate([vp_ref[...], vc_ref[...]], axis=0).astype(F32)
    r = lax.broadcasted_iota(jnp.int32, (w, 2 * w), 0)
    kk = lax.broadcasted_iota(jnp.int32, (w, 2 * w), 1)
    band = (kk > r) & (kk <= r + w)
    low = lax.broadcasted_iota(jnp.int32, (w, LANES), 1) < HEAD_DIM
    zero = jnp.zeros((w, LANES), BF16)
    group = N_Q_HEADS // N_KV_HEADS
    for h in range(N_KV_HEADS):
        kd_all = _dup_head(k_all, h).astype(BF16)
        vd_all = _dup_head(v_all, h).astype(BF16)
        for sub in range(qb):
            valid = band & ((kk >= w) | (j > 0)) if sub == 0 else band
            kd = kd_all[sub * w:(sub + 2) * w, :]
            vd = vd_all[sub * w:(sub + 2) * w, :]
            for jj in range(group // 2):
                col = (h * group // 2 + jj) * LANES
                qv = q_ref[sub * w:(sub + 1) * w, col:col + LANES]
                halves = []
                for half in range(2):
                    head = h * group + jj * 2 + half
                    qm = jnp.where(low if half == 0 else ~low, qv, zero)
                    s = jnp.where(valid, _nt_dot(qm, kd), -jnp.inf)
                    sink = sink_ref[head]
                    m = jnp.maximum(jnp.max(s, axis=-1, keepdims=True), sink)
                    p = jnp.exp(s - m)
                    den = jnp.sum(p, axis=-1, keepdims=True) + jnp.exp(sink - m)
                    o = jnp.dot(p.astype(BF16), vd, preferred_element_type=F32)
                    halves.append(o / den)
                o_ref[sub * w:(sub + 1) * w, col:col + LANES] = (
                    jnp.where(low, halves[0], halves[1]).astype(o_ref.dtype))


def _attn_prompt(q, k, v, sinks, *, batch, seq):
    nb = seq // WINDOW
    qb = next(c for c in (4, 2, 1) if nb % c == 0)
    steps = nb // qb
    cur = lambda b, j: (b * steps + j, 0)
    prev = lambda b, j: (b * nb + jnp.maximum(j * qb - 1, 0), 0)
    return pl.pallas_call(
        functools.partial(_attn_prompt_kernel, qb=qb),
        grid=(batch, steps),
        in_specs=[pl.BlockSpec(memory_space=pltpu.SMEM),
                  pl.BlockSpec((qb * WINDOW, ATTN_WIDTH), cur),
                  pl.BlockSpec((qb * WINDOW, KV_WIDTH), cur), pl.BlockSpec((WINDOW, KV_WIDTH), prev),
                  pl.BlockSpec((qb * WINDOW, KV_WIDTH), cur), pl.BlockSpec((WINDOW, KV_WIDTH), prev)],
        out_specs=pl.BlockSpec((qb * WINDOW, ATTN_WIDTH), cur),
        out_shape=jax.ShapeDtypeStruct((batch * seq, ATTN_WIDTH), BF16),
        compiler_params=_cparams(("parallel", "parallel")),
        name="attn_prompt",
    )(sinks, q, k, k, v, v)


def _attn_sample_kernel(sink_ref, q_ref, kn_ref, vn_ref, ck_ref, cv_ref, o_ref, nk_ref, nv_ref, *, gb, ts):
    w = WINDOW
    group = N_Q_HEADS // N_KV_HEADS
    rows = group * ts
    low = lax.broadcasted_iota(jnp.int32, (ts, LANES), 1) < HEAD_DIM
    pad = jnp.zeros((ts, LANES), F32)
    s_c, s_n, v_dup = [], [], []
    for b in range(gb):
        kc, vc = ck_ref[b], cv_ref[b]
        kn, vn = kn_ref[b * ts:(b + 1) * ts, :], vn_ref[b * ts:(b + 1) * ts, :]
        nk_ref[b, 0:w - ts, :] = kc[ts:, :]
        nk_ref[b, w - ts:, :] = kn
        nv_ref[b, 0:w - ts, :] = vc[ts:, :]
        nv_ref[b, w - ts:, :] = vn
        knp = jnp.concatenate([kn, pad], axis=0)
        vnp = jnp.concatenate([vn, pad], axis=0)
        qb = q_ref[b * ts:(b + 1) * ts, :]
        for h in range(N_KV_HEADS):
            parts = []
            for jj in range(group // 2):
                col = (h * group // 2 + jj) * LANES
                qv = qb[:, col:col + LANES]
                parts += [jnp.where(low, qv, 0.0), jnp.where(low, 0.0, qv)]
            lhs = jnp.concatenate(parts, axis=0).astype(BF16)
            s_c.append(_nt_dot(lhs, _dup_head(kc, h).astype(BF16)))
            s_n.append(_nt_dot(lhs, _dup_head(knp, h).astype(BF16)))
            v_dup.append((_dup_head(vc, h).astype(BF16), _dup_head(vnp, h).astype(BF16)))
    s_c = jnp.concatenate(s_c, axis=0)
    s_n = jnp.concatenate(s_n, axis=0)
    n_rows = s_c.shape[0]
    ridx = lax.broadcasted_iota(jnp.int32, (n_rows, 1), 0)
    t_row = ridx % ts
    head_row = (ridx // ts) % N_Q_HEADS
    sink = jnp.zeros((n_rows, 1), F32)
    for hd in range(N_Q_HEADS):
        sink = jnp.where(head_row == hd, sink_ref[hd], sink)
    c_idx = lax.broadcasted_iota(jnp.int32, (n_rows, w), 1)
    n_idx = lax.broadcasted_iota(jnp.int32, (n_rows, 2 * ts), 1)
    s_c = jnp.where(c_idx > t_row, s_c, -jnp.inf)
    s_n = jnp.where(n_idx <= t_row, s_n, -jnp.inf)
    m = jnp.maximum(jnp.maximum(jnp.max(s_c, axis=-1, keepdims=True), jnp.max(s_n, axis=-1, keepdims=True)), sink)
    p_c = jnp.exp(s_c - m)
    p_n = jnp.exp(s_n - m)
    den = jnp.sum(p_c, axis=-1, keepdims=True) + jnp.sum(p_n, axis=-1, keepdims=True) + jnp.exp(sink - m)
    p_c = p_c.astype(BF16)
    p_n = p_n.astype(BF16)
    outs = []
    for b in range(gb):
        cols = []
        for h in range(N_KV_HEADS):
            ci = b * N_KV_HEADS + h
            sl = slice(ci * rows, (ci + 1) * rows)
            vdc, vdn = v_dup[ci]
            o = (jnp.dot(p_c[sl], vdc, preferred_element_type=F32)
                 + jnp.dot(p_n[sl], vdn, preferred_element_type=F32)) / den[sl]
            for jj in range(group // 2):
                lo_part = o[(2 * jj) * ts:(2 * jj + 1) * ts, :]
                hi_part = o[(2 * jj + 1) * ts:(2 * jj + 2) * ts, :]
                cols.append(jnp.where(low, lo_part, hi_part))
        outs.append(jnp.concatenate(cols, axis=1))
    o_ref[...] = jnp.concatenate(outs, axis=0).astype(o_ref.dtype)


def _attn_sample(q, kf, vf, cache_k, cache_v, sinks, *, batch, ts):
    assert ts % 8 == 0 and ts <= WINDOW
    gb = _tile(batch, 16, 2)
    tok = lambda w: pl.BlockSpec((gb * ts, w), lambda i: (i, 0))
    cache = pl.BlockSpec((gb, WINDOW, KV_WIDTH), lambda i: (i, 0, 0))
    cshape = jax.ShapeDtypeStruct((batch, WINDOW, KV_WIDTH), F32)
    return pl.pallas_call(
        functools.partial(_attn_sample_kernel, gb=gb, ts=ts),
        grid=(batch // gb,),
        in_specs=[pl.BlockSpec(memory_space=pltpu.SMEM), tok(ATTN_WIDTH), tok(KV_WIDTH), tok(KV_WIDTH),
                  cache, cache],
        out_specs=[tok(ATTN_WIDTH), cache, cache],
        out_shape=[jax.ShapeDtypeStruct((batch * ts, ATTN_WIDTH), BF16), cshape, cshape],
        compiler_params=_cparams(("parallel",)),
        name="attn_sample",
    )(sinks, q, kf, vf, cache_k, cache_v)


def _ln_swish(acc, b, lg, lb):
    y = acc + b
    mu = jnp.mean(y, axis=-1, keepdims=True)
    yc = y - mu
    var = jnp.mean(yc * yc, axis=-1, keepdims=True)
    yn = yc * lax.rsqrt(var + LN_EPS) * lg + lb
    return yn * _sigmoid(yn)


def _conv_prompt_kernel(a_ref, ap_ref, w_ref, b_ref, lg_ref, lb_ref, o_ref, win_ref, *, tt, rc):
    j = pl.program_id(1)
    n = CONV_HALO + tt
    win = jnp.concatenate([jnp.where(j > 0, ap_ref[...], 0.0), a_ref[...]], axis=0)
    win_ref[0] = win
    for r in range(1, SUBLANES):
        win_ref[r] = pltpu.roll(win, n - r, 0)
    off = CONV_HALO - (CONV_K - 1)
    b, lg, lb = b_ref[...], lg_ref[...], lb_ref[...]
    for c in range(tt // rc):
        acc = jnp.zeros((rc, a_ref.shape[1]), F32)
        for k in range(CONV_K):
            s = off + k
            base = c * rc + (s // SUBLANES) * SUBLANES
            wk = jnp.concatenate([w_ref[k]] * (rc // SUBLANES), axis=0)
            acc = acc + wk * win_ref[s % SUBLANES, base:base + rc, :]
        o_ref[c * rc:(c + 1) * rc, :] = _ln_swish(acc, b, lg, lb).astype(o_ref.dtype)


def _conv_prompt(a, w, b, lg, lb, *, batch, seq):
    cw = a.shape[1]
    tt = _tile(seq, 512, CONV_HALO)
    rc = _tile(tt, 32, 16)
    nt = seq // tt
    per = tt // CONV_HALO
    cur = lambda bb, j: (bb * nt + j, 0)
    prev = lambda bb, j: (jnp.maximum((bb * nt + j) * per - 1, 0), 0)
    vec = pl.BlockSpec((1, cw), lambda bb, j: (0, 0))
    return pl.pallas_call(
        functools.partial(_conv_prompt_kernel, tt=tt, rc=rc),
        grid=(batch, nt),
        in_specs=[pl.BlockSpec((tt, cw), cur), pl.BlockSpec((CONV_HALO, cw), prev),
                  pl.BlockSpec((CONV_K, SUBLANES, cw), lambda bb, j: (0, 0, 0)), vec, vec, vec],
        out_specs=pl.BlockSpec((tt, cw), cur),
        out_shape=jax.ShapeDtypeStruct((batch * seq, cw), BF16),
        scratch_shapes=[pltpu.VMEM((SUBLANES, CONV_HALO + tt, cw), F32)],
        compiler_params=_cparams(("parallel", "parallel")),
        name="conv_prompt",
    )(a, a, jnp.broadcast_to(w[:, None, :], (CONV_K, SUBLANES, cw)), b, lg, lb)


def _conv_sample_kernel(a_ref, st_ref, w_ref, b_ref, lg_ref, lb_ref, o_ref, win_ref, *, gb, ts):
    ctx = CONV_K - 1
    b, lg, lb = b_ref[...], lg_ref[...], lb_ref[...]
    for bb in range(gb):
        win_ref[bb, 0:ctx, :] = st_ref[bb]
        win_ref[bb, ctx:ctx + ts, :] = a_ref[bb * ts:(bb + 1) * ts, :]
    outs = []
    for bb in range(gb):
        acc = jnp.zeros((ts, a_ref.shape[1]), F32)
        for k in range(CONV_K):
            wk = jnp.concatenate([w_ref[k]] * (ts // SUBLANES), axis=0)
            acc = acc + wk * win_ref[bb, k:k + ts, :]
        outs.append(_ln_swish(acc, b, lg, lb))
    o_ref[...] = jnp.concatenate(outs, axis=0).astype(o_ref.dtype)


def _conv_sample(a, state, w, b, lg, lb, *, batch, ts):
    cw = a.shape[1]
    ctx = CONV_K - 1
    gb = _tile(batch, 16, 2)
    vec = pl.BlockSpec((1, cw), lambda i: (0, 0))
    return pl.pallas_call(
        functools.partial(_conv_sample_kernel, gb=gb, ts=ts),
        grid=(batch // gb,),
        in_specs=[pl.BlockSpec((gb * ts, cw), lambda i: (i, 0)),
                  pl.BlockSpec((gb, ctx, cw), lambda i: (i, 0, 0)),
                  pl.BlockSpec((CONV_K, SUBLANES, cw), lambda i: (0, 0, 0)), vec, vec, vec],
        out_specs=pl.BlockSpec((gb * ts, cw), lambda i: (i, 0)),
        out_shape=jax.ShapeDtypeStruct((batch * ts, cw), BF16),
        scratch_shapes=[pltpu.VMEM((gb, ctx + ts + 2, cw), F32)],
        compiler_params=_cparams(("parallel",)),
        name="conv_sample",
    )(a, state, jnp.broadcast_to(w[:, None, :], (CONV_K, SUBLANES, cw)), b, lg, lb)


def _outproj_router_kernel(x_hbm, att_ref, cv_ref, wo_ref, g_ref, rw_ref, rb_ref, cin_ref,
                           x1_ref, hn_ref, rt_ref, tcar_ref, cnt_ref, carry_ref, xbuf_ref, xsem):
    i = pl.program_id(0)
    steps = pl.num_programs(0)
    tm = xbuf_ref.shape[1]

    def fetch(step):
        slot = step % X_RING
        return pltpu.make_async_copy(x_hbm.at[pl.ds(pl.multiple_of(step * tm, tm), tm)], xbuf_ref.at[slot],
                                     xsem.at[slot])

    @pl.when(i == 0)
    def _():
        carry_ref[...] = cin_ref[...]
        for s in range(X_RING - 1):
            @pl.when(s < steps)
            def _():
                fetch(s).start()

    @pl.when(i + X_RING - 1 < steps)
    def _():
        fetch(i + X_RING - 1).start()

    fetch(i).wait()
    x_ref = xbuf_ref.at[i % X_RING]
    aw = att_ref.shape[1]
    n_exp = rw_ref.shape[0]
    r = lax.broadcasted_iota(jnp.int32, (tm, tm), 0)
    c = lax.broadcasted_iota(jnp.int32, (tm, tm), 1)
    before = (r < c).astype(BF16)
    eid = lax.broadcasted_iota(jnp.int32, (n_exp, tm), 0)
    carry = carry_ref[...][:, 0:1]
    tcar_ref[0] = carry_ref[...]
    mix = (jnp.dot(att_ref[...], wo_ref[0:aw, :], preferred_element_type=F32)
           + jnp.dot(cv_ref[...], wo_ref[aw:, :], preferred_element_type=F32))
    x1 = x_ref[...] + mix
    x1_ref[...] = x1
    ms = jnp.mean(x1 * x1, axis=-1, keepdims=True)
    hn = (x1 * lax.rsqrt(ms + RMS_EPS) * g_ref[...]).astype(BF16)
    hn_ref[...] = hn
    logits = _nt_dot(rw_ref[...], hn) + rb_ref[...][:, 0:1]
    onehot = jnp.zeros((n_exp, tm), F32)
    vals, idxs = [], []
    for _ in range(TOP_K):
        m = jnp.max(logits, axis=0, keepdims=True)
        idx = jnp.min(jnp.where(logits == m, eid, n_exp), axis=0, keepdims=True)
        sel = eid == idx
        onehot = onehot + sel.astype(F32)
        logits = jnp.where(sel, -jnp.inf, logits)
        vals.append(m)
        idxs.append(idx)
    es = [jnp.exp(v - vals[0]) for v in vals]
    den = es[0] + es[1] + es[2] + es[3]
    local = jnp.dot(onehot.astype(BF16), before, preferred_element_type=F32)
    count = jnp.sum(onehot, axis=1, keepdims=True)
    padded = jnp.ceil(count * (1.0 / RUN)) * RUN
    er = lax.broadcasted_iota(jnp.int32, (n_exp, n_exp), 0)
    ec = lax.broadcasted_iota(jnp.int32, (n_exp, n_exp), 1)
    start = jnp.dot((ec < er).astype(BF16), jnp.broadcast_to(padded, (n_exp, LANES)).astype(BF16),
                    preferred_element_type=F32)[:, 0:1]
    slot_of = local + start
    slots = [jnp.sum(jnp.where(eid == idxs[k], slot_of, 0.0), axis=0, keepdims=True) for k in range(TOP_K)]
    rt_ref[...] = jnp.concatenate(slots + [e / den for e in es], axis=0)
    carry = carry + count
    carry_ref[...] = jnp.broadcast_to(carry, carry_ref.shape)
    cnt_ref[...] = jnp.broadcast_to(carry, cnt_ref.shape)


def _store_slabs(ref, val, rows):
    for j in range(val.shape[1] // LANES):
        ref[pl.ds(j, rows, stride=SUBLANES), :] = val[:, j * LANES:(j + 1) * LANES]


def _load_slabs(ref, rows, dtype):
    return jnp.concatenate([ref[pl.ds(j, rows, stride=SUBLANES), :].astype(dtype) for j in range(SUBLANES)],
                           axis=1)


def _outproj_router(x2, att, cv, wo_bf, g, rwt_bf, rbt, carry_in, *, tm):
    n, d = x2.shape
    assert d == SUBLANES * LANES and n % tm == 0
    n_exp = rwt_bf.shape[0]
    row = lambda w: pl.BlockSpec((tm, w), lambda i: (i, 0))
    full = lambda a: pl.BlockSpec(a.shape, lambda i: (0,) * a.ndim)
    return pl.pallas_call(
        _outproj_router_kernel,
        grid=(n // tm,),
        in_specs=[pl.BlockSpec(memory_space=pl.ANY), row(att.shape[1]), row(cv.shape[1]), full(wo_bf), full(g),
                  full(rwt_bf), full(rbt), full(carry_in)],
        out_specs=[row(d), row(d),
                   pl.BlockSpec((ROUTE_T_ROWS, tm), lambda i: (0, i)),
                   pl.BlockSpec((1, n_exp, LANES), lambda i: (i, 0, 0)),
                   pl.BlockSpec((n_exp, LANES), lambda i: (0, 0))],
        out_shape=[jax.ShapeDtypeStruct((n, d), F32), jax.ShapeDtypeStruct((n, d), BF16),
                   jax.ShapeDtypeStruct((ROUTE_T_ROWS, n), F32),
                   jax.ShapeDtypeStruct((n // tm, n_exp, LANES), F32),
                   jax.ShapeDtypeStruct((n_exp, LANES), F32)],
        scratch_shapes=[pltpu.VMEM((n_exp, LANES), F32), pltpu.VMEM((X_RING, tm, d), F32),
                        pltpu.SemaphoreType.DMA((X_RING,))],
        compiler_params=_cparams(("arbitrary",)),
        name="outproj_router",
    )(x2, att, cv, wo_bf, g, rwt_bf, rbt, carry_in)


def _dispatch_kernel(fill_ref, nfill_ref, tot_ref, dst_ref, rt_ref, hn_ref, hn2_ref,
                     xs_ref, zero_ref, stage_ref, sem, zsem, *, td, n_first, chunk):
    i = pl.program_id(0)
    last = pl.num_programs(0) - 1
    par = i % 2
    n_slots = stage_ref.shape[1] // SUBLANES

    def fill(f, s):
        row = pl.multiple_of(fill_ref[f] * (FILL_ROWS * SUBLANES), FILL_ROWS * SUBLANES)
        return pltpu.make_async_copy(zero_ref, xs_ref.at[pl.ds(row, FILL_ROWS * SUBLANES)], zsem.at[s])

    def fills(lo, hi, s, act):
        def body(f, carry):
            act(fill(f, s))
            return carry

        lax.fori_loop(lo, hi, body, 0)

    @pl.when(i == 0)
    def _():
        zero_ref[...] = jnp.zeros(zero_ref.shape, F32)
        fills(0, nfill_ref[0], 0, lambda c: c.start())
        fills(nfill_ref[0], nfill_ref[1], 1, lambda c: c.start())

    @pl.when(i == 1)
    def _():
        fills(0, nfill_ref[0], 0, lambda c: c.wait())

    def group_by_expert(tok_ref):
        hn = tok_ref[...]
        slots = rt_ref[0:TOP_K, :].astype(jnp.int32)
        for c in range(n_slots // chunk):
            s = lax.broadcasted_iota(jnp.int32, (chunk, td), 0) + c * chunk
            hit = s == slots[0:1, :]
            for k in range(1, TOP_K):
                hit = hit | (s == slots[k:k + 1, :])
            rows = jnp.dot(jnp.where(hit, 1.0, 0.0).astype(BF16), hn, preferred_element_type=F32)
            _store_slabs(stage_ref.at[par, pl.ds(c * chunk * SUBLANES, chunk * SUBLANES)], rows, chunk)

    def run_copy(buf, src_slot, dst_row, runs=1):
        src = pl.multiple_of(src_slot * SUBLANES, RUN * SUBLANES)
        dst = pl.multiple_of(dst_row * SUBLANES, SUBLANES)
        return pltpu.make_async_copy(stage_ref.at[buf, pl.ds(src, runs * RUN * SUBLANES)],
                                     xs_ref.at[pl.ds(dst, runs * RUN * SUBLANES)], sem)

    def wait_runs(tile):
        def wait_many(q, carry):
            run_copy(0, 0, 0, ISSUE_UNROLL).wait()
            return carry

        def wait_one(j, carry):
            run_copy(0, 0, 0).wait()
            return carry

        n_many = tot_ref[tile] // ISSUE_UNROLL
        lax.fori_loop(0, n_many, wait_many, 0)
        lax.fori_loop(n_many * ISSUE_UNROLL, tot_ref[tile], wait_one, 0)

    @pl.when(i >= 2)
    def _():
        wait_runs(i - 2)

    @pl.when(i >= 1)
    def _():
        def issue(j):
            run_copy(1 - par, j * RUN, dst_ref[0, 0, j]).start()

        def issue_many(q, carry):
            for u in range(ISSUE_UNROLL):
                issue(q * ISSUE_UNROLL + u)
            return carry

        def issue_one(j, carry):
            issue(j)
            return carry

        n_full = tot_ref[i - 1] // ISSUE_UNROLL
        lax.fori_loop(0, n_full, issue_many, 0)
        lax.fori_loop(n_full * ISSUE_UNROLL, tot_ref[i - 1], issue_one, 0)

    @pl.when(i < n_first)
    def _():
        group_by_expert(hn_ref)

    @pl.when((i >= n_first) & (i < last))
    def _():
        group_by_expert(hn2_ref)

    @pl.when(i == last)
    def _():
        wait_runs(i - 1)
        fills(nfill_ref[0], nfill_ref[1], 1, lambda c: c.wait())


def _stage_slots(td):
    return td * TOP_K + N_EXPERTS * RUN


def _by_tile(field, td):
    n = field.shape[1]
    return field.reshape(TOP_K, n // td, td).transpose(1, 0, 2).reshape(n // td, 1, TOP_K * td)


def _dispatch(hn_a, hn_b, route_t, fill_pieces, n_fill, runs_per_tile, run_dst, *, td, nb, tme):
    na, d = hn_a.shape
    assert na % td == 0 and hn_b.shape[0] % td == 0 and td % RUN == 0
    n_first = na // td
    n_second = hn_b.shape[0] // td
    tiles = n_first + n_second
    n_slots = _stage_slots(td)
    chunk = _tile(n_slots, 768, SUBLANES)
    tok = lambda m: pl.BlockSpec((td, d), m)
    grid_spec = pltpu.PrefetchScalarGridSpec(
        num_scalar_prefetch=3,
        grid=(tiles + 1,),
        in_specs=[pl.BlockSpec((1, 1, n_slots // RUN), lambda i, *_: (jnp.maximum(i - 1, 0), 0, 0),
                               memory_space=pltpu.SMEM),
                  pl.BlockSpec((ROUTE_T_ROWS, td), lambda i, *_: (0, jnp.minimum(i, tiles - 1))),
                  tok(lambda i, *_: (jnp.minimum(i, n_first - 1), 0)),
                  tok(lambda i, *_: (jnp.clip(i - n_first, 0, n_second - 1), 0))],
        out_specs=pl.BlockSpec(memory_space=pl.ANY),
        scratch_shapes=[pltpu.VMEM((FILL_ROWS * SUBLANES, LANES), F32),
                        pltpu.VMEM((2, n_slots * SUBLANES, LANES), F32),
                        pltpu.SemaphoreType.DMA(()), pltpu.SemaphoreType.DMA((2,))],
    )
    return pl.pallas_call(
        functools.partial(_dispatch_kernel, td=td, n_first=n_first, chunk=chunk),
        grid_spec=grid_spec,
        out_shape=jax.ShapeDtypeStruct((nb * tme * SUBLANES, LANES), F32),
        compiler_params=_cparams(("arbitrary",)),
        name="dispatch",
    )(fill_pieces, n_fill, runs_per_tile, run_dst, route_t, hn_a, hn_b)


def _experts_kernel(be_ref, bsrc_ref, nv_ref, nx_ref, nu_ref, x_ref, w1_hbm, b1_ref, w2_hbm, b2_ref,
                    y_ref, w1f_ref, w2f_ref, w1b_ref, w2b_ref, par_ref, wsem, *, tme):
    i = pl.program_id(0)
    e = be_ref[i]
    e_prev = be_ref[jnp.maximum(i - 1, 0)]
    d_ff = w2f_ref.shape[1]
    half = tme // 2

    def fetch(expert, s):
        return (pltpu.make_async_copy(w1_hbm.at[expert], w1f_ref.at[s], wsem.at[0, s]),
                pltpu.make_async_copy(w2_hbm.at[expert], w2f_ref.at[s], wsem.at[1, s]))

    @pl.when(i == 0)
    def _():
        par_ref[0] = 0
        for c in fetch(e, 0):
            c.start()

    @pl.when((i == 0) | (e != e_prev))
    def _():
        s = par_ref[0]
        for c in fetch(e, s):
            c.wait()
        w1b_ref[...] = w1f_ref[s].astype(BF16)
        w2b_ref[...] = w2f_ref[s].astype(BF16)
        nxt = nx_ref[i]

        @pl.when(nxt >= 0)
        def _():
            for c in fetch(nxt, 1 - s):
                c.start()

        par_ref[0] = 1 - s

    def ffn(x):
        h = jnp.dot(x, w1b_ref[...], preferred_element_type=F32) + b1_ref[0]
        x_glu = jnp.minimum(h[:, :d_ff], SWIGLU_LIMIT)
        x_lin = jnp.clip(h[:, d_ff:], -SWIGLU_LIMIT, SWIGLU_LIMIT)
        act = x_glu * _sigmoid(SWIGLU_ALPHA * x_glu) * (x_lin + 1.0)
        return jnp.dot(act.astype(BF16), w2b_ref[...], preferred_element_type=F32) + b2_ref[0]

    used = i < nu_ref[0]
    nv = nv_ref[i]

    @pl.when(used & (nv > half))
    def _():
        _store_slabs(y_ref, ffn(_load_slabs(x_ref, tme, BF16)), tme)

    @pl.when(used & (nv <= half))
    def _():
        rows = half * SUBLANES
        _store_slabs(y_ref.at[pl.ds(0, rows)], ffn(_load_slabs(x_ref.at[pl.ds(0, rows)], half, BF16)), half)
        y_ref[pl.ds(rows, rows), :] = jnp.zeros((rows, LANES), F32)

    @pl.when(i == nu_ref[0])
    def _():
        y_ref[...] = jnp.zeros(y_ref.shape, F32)


def _experts(xs, w1, b1, w2, b2, blk_exp, blk_src, blk_nvalid, blk_next, n_used, *, tme):
    n_exp, d, h2 = w1.shape
    d_ff = w2.shape[1]
    nb = xs.shape[0] // (tme * SUBLANES)
    slab = lambda m: pl.BlockSpec((tme * SUBLANES, LANES), m)
    grid_spec = pltpu.PrefetchScalarGridSpec(
        num_scalar_prefetch=5,
        grid=(nb,),
        in_specs=[slab(lambda i, be, bs, nv, nx, nu: (bs[i], 0)),
                  pl.BlockSpec(memory_space=pl.ANY),
                  pl.BlockSpec((1, 1, h2), lambda i, be, bs, nv, nx, nu: (be[i], 0, 0)),
                  pl.BlockSpec(memory_space=pl.ANY),
                  pl.BlockSpec((1, 1, d), lambda i, be, bs, nv, nx, nu: (be[i], 0, 0))],
        out_specs=slab(lambda i, be, bs, nv, nx, nu: (jnp.minimum(i, nu[0]), 0)),
        scratch_shapes=[pltpu.VMEM((2, d, h2), F32), pltpu.VMEM((2, d_ff, d), F32),
                        pltpu.VMEM((d, h2), BF16), pltpu.VMEM((d_ff, d), BF16),
                        pltpu.SMEM((1,), jnp.int32), pltpu.SemaphoreType.DMA((2, 2))],
    )
    return pl.pallas_call(
        functools.partial(_experts_kernel, tme=tme),
        grid_spec=grid_spec,
        out_shape=jax.ShapeDtypeStruct(xs.shape, F32),
        input_output_aliases={5: 0},
        compiler_params=_cparams(("arbitrary",)),
        name="experts",
    )(blk_exp, blk_src, blk_nvalid, blk_next, n_used, xs, w1, b1.reshape(n_exp, 1, h2), w2,
      b2.reshape(n_exp, 1, d))


def _combine_kernel(tot_ref, rcur_ref, rnext_ref, slot_ref, gate_ref, x1_ref, g_ref, ys_ref, o_ref,
                    stage_ref, moe_ref, sem, *, tr, first, sub):
    i = pl.program_id(0)
    par = i % 2

    def run_copy(buf, j, src_row, runs=1):
        src = pl.multiple_of(src_row * SUBLANES, SUBLANES)
        dst = pl.multiple_of(j * (RUN * SUBLANES), RUN * SUBLANES)
        return pltpu.make_async_copy(ys_ref.at[pl.ds(src, runs * RUN * SUBLANES)],
                                     stage_ref.at[buf, pl.ds(dst, runs * RUN * SUBLANES)], sem.at[buf])

    def gather(rref, buf, n):
        def issue_many(q, carry):
            for u in range(ISSUE_UNROLL):
                j = q * ISSUE_UNROLL + u
                run_copy(buf, j, rref[0, 0, j]).start()
            return carry

        def issue_one(j, carry):
            run_copy(buf, j, rref[0, 0, j]).start()
            return carry

        lax.fori_loop(0, n // ISSUE_UNROLL, issue_many, 0)
        lax.fori_loop((n // ISSUE_UNROLL) * ISSUE_UNROLL, n, issue_one, 0)

    @pl.when(i == 0)
    def _():
        gather(rcur_ref, 0, tot_ref[first])

    @pl.when(i + 1 < pl.num_programs(0))
    def _():
        gather(rnext_ref, 1 - par, tot_ref[first + i + 1])

    def wait_many(q, carry):
        run_copy(par, 0, 0, ISSUE_UNROLL).wait()
        return carry

    def wait_one(j, carry):
        run_copy(par, 0, 0).wait()
        return carry

    n_runs = tot_ref[first + i]
    lax.fori_loop(0, n_runs // ISSUE_UNROLL, wait_many, 0)
    lax.fori_loop((n_runs // ISSUE_UNROLL) * ISSUE_UNROLL, n_runs, wait_one, 0)

    def token(buf, r, carry):
        acc = None
        for k in range(TOP_K):
            s = pl.multiple_of(slot_ref[0, 0, k * tr + r], SUBLANES)
            term = gate_ref[0, 0, k * tr + r] * stage_ref[buf, pl.ds(s, SUBLANES), :]
            acc = term if acc is None else acc + term
        moe_ref[pl.ds(pl.multiple_of(r * SUBLANES, SUBLANES), SUBLANES), :] = acc
        return carry

    for buf in range(2):
        @pl.when(par == buf)
        def _():
            lax.fori_loop(0, tr, functools.partial(token, buf), 0, unroll=ISSUE_UNROLL)

    for c in range(tr // sub):
        rows = slice(c * sub, (c + 1) * sub)
        y = x1_ref[rows, :] + _load_slabs(moe_ref.at[pl.ds(c * sub * SUBLANES, sub * SUBLANES)], sub, F32)
        ms = jnp.mean(y * y, axis=-1, keepdims=True)
        o_ref[rows, :] = y * lax.rsqrt(ms + RMS_EPS) * g_ref[...]


def _combine(x1, slots, gates, runs_per_tile, run_src, ys, g, *, tr, first):
    n, d = x1.shape
    steps = n // tr
    n_slots = _stage_slots(tr)
    smem = lambda a, m: pl.BlockSpec((1, 1, a.shape[2]), m, memory_space=pltpu.SMEM)
    cur = lambda i, tot: (i, 0, 0)
    nxt = lambda i, tot: (jnp.minimum(i + 1, steps - 1), 0, 0)
    grid_spec = pltpu.PrefetchScalarGridSpec(
        num_scalar_prefetch=1,
        grid=(steps,),
        in_specs=[smem(run_src, cur), smem(run_src, nxt), smem(slots, cur), smem(gates, cur),
                  pl.BlockSpec((tr, d), lambda i, tot: (i, 0)),
                  pl.BlockSpec((1, d), lambda i, tot: (0, 0)),
                  pl.BlockSpec(memory_space=pl.ANY)],
        out_specs=pl.BlockSpec((tr, d), lambda i, tot: (i, 0)),
        scratch_shapes=[pltpu.VMEM((2, n_slots * SUBLANES, LANES), F32), pltpu.VMEM((tr * SUBLANES, LANES), F32),
                        pltpu.SemaphoreType.DMA((2,))],
    )
    return pl.pallas_call(
        functools.partial(_combine_kernel, tr=tr, first=first, sub=_tile(tr, 128, 8)),
        grid_spec=grid_spec,
        out_shape=jax.ShapeDtypeStruct((n, d), F32),
        compiler_params=_cparams(("arbitrary",)),
        name="combine",
    )(runs_per_tile, run_src, run_src, slots, gates, x1, g, ys)


def _routing_tables(counts_f, *, tme, nb):
    counts = counts_f[:, 0].astype(jnp.int32)
    nblk = jnp.where(counts > 0, (counts + RUN - 1 + tme - 1) // tme, 0)
    blk_end = jnp.cumsum(nblk)
    blk_start = blk_end - nblk
    n_used = blk_end[-1]
    b = jnp.arange(nb, dtype=jnp.int32)
    used = b < n_used
    blk_exp = jnp.minimum(jnp.sum((b[:, None] >= blk_end[None, :]).astype(jnp.int32), axis=1), N_EXPERTS - 1)
    last_exp = jnp.max(jnp.where(nblk > 0, jnp.arange(N_EXPERTS, dtype=jnp.int32), 0))
    blk_exp = jnp.where(used, blk_exp, last_exp).astype(jnp.int32)
    blk_src = jnp.minimum(b, n_used - 1).astype(jnp.int32)
    experts = jnp.arange(N_EXPERTS, dtype=jnp.int32)
    mine = (b[:, None] >= blk_start[None, :]) & (b[:, None] < blk_end[None, :])
    nvalid = jnp.sum(jnp.where(mine, counts[None, :] - (b[:, None] - blk_start[None, :]) * tme, 0), axis=1)
    nvalid = jnp.clip(nvalid, 0, tme).astype(jnp.int32)
    later = (experts[None, :] > experts[:, None]) & (nblk[None, :] > 0)
    nxt_e = jnp.min(jnp.where(later, experts[None, :], N_EXPERTS), axis=1)
    nxt_e = jnp.where(nxt_e == N_EXPERTS, -1, nxt_e)
    blk_next = jnp.sum(jnp.where(blk_exp[:, None] == experts[None, :], nxt_e[None, :], 0), axis=1).astype(jnp.int32)
    per_blk = tme // FILL_ROWS
    piece = jnp.arange(per_blk, dtype=jnp.int32)[None, :]
    first_pad = (nvalid // FILL_ROWS)[:, None]
    last_reached = ((nvalid + RUN - 1) // FILL_ROWS)[:, None]
    reached = (used[:, None] & (piece >= first_pad) & (piece <= last_reached)).reshape(-1)
    quiet = (used[:, None] & (piece > last_reached)).reshape(-1) | jnp.repeat(~used, per_blk)
    fill_pieces = jnp.argsort(jnp.where(reached, 0, jnp.where(quiet, 1, 2)), stable=True).astype(jnp.int32)
    n_fill = jnp.stack([jnp.sum(reached), jnp.sum(reached | quiet)]).astype(jnp.int32)
    return (blk_start, blk_exp, blk_src, nvalid, blk_next, n_used.reshape(1).astype(jnp.int32), fill_pieces,
            n_fill)


def _run_tables(tile_carry_f, counts_f, blk_start, *, td, tme):
    before = tile_carry_f[:, :, 0].astype(jnp.int32)
    counts = counts_f[:, 0].astype(jnp.int32)
    in_tile = jnp.concatenate([before[1:], counts[None, :]], axis=0) - before
    n_runs = (in_tile + RUN - 1) // RUN
    ends = jnp.cumsum(n_runs, axis=1)
    j = jnp.arange(_stage_slots(td) // RUN, dtype=jnp.int32)
    owner = jnp.sum((j[None, :, None] >= ends[:, None, :]).astype(jnp.int32), axis=2)
    base = blk_start[None, :] * tme + before - (ends - n_runs) * RUN
    mine = owner[:, :, None] == jnp.arange(N_EXPERTS, dtype=jnp.int32)[None, None, :]
    dst = jnp.sum(jnp.where(mine, base[:, None, :], 0), axis=2) + j[None, :] * RUN
    return ends[:, -1].astype(jnp.int32), dst[:, None, :].astype(jnp.int32)


def kernel(x_prompt, x_sample, cache_k, cache_v, state_conv, attn_norm_g, w_in, attn_sinks, conv_w, conv_b,
           conv_ln_g, conv_ln_b, w_out, ffn_norm_g, router_w, router_b, w1, b1, w2, b2, final_norm_g):
    depth = w_in.shape[0]
    assert depth == 1, "single-layer step"
    bp, sp, d = x_prompt.shape
    bs, ss, _ = x_sample.shape
    cw = conv_w.shape[2]
    np_, ns = bp * sp, bs * ss
    n_tok = np_ + ns
    assert sp % WINDOW == 0

    xp2 = x_prompt.reshape(np_, d)
    xs2 = x_sample.reshape(ns, d)
    w_in_bf = w_in[0].astype(BF16)
    w_out_bf = w_out[0].astype(BF16)
    g_attn = attn_norm_g[0].reshape(1, d)
    g_ffn = ffn_norm_g[0].reshape(1, d)
    sinks = attn_sinks[0]
    vec = lambda a: a.reshape(1, cw)

    tab_p = _rope_tables(jnp.arange(sp, dtype=jnp.int32))
    tms = _tile(ns, 512, max(ss, 16))
    tab_s = _rope_tables(PAST_LEN + (jnp.arange(tms, dtype=jnp.int32) % ss))
    qp, kp, vp, kfp, vfp, ap = _in_proj(xp2, g_attn, w_in_bf, tab_p, seq_period=sp, q_dtype=BF16, conv_width=cw)
    qs, _, _, kfs, vfs, as_ = _in_proj(xs2, g_attn, w_in_bf, tab_s, seq_period=None, q_dtype=F32, conv_width=cw)

    att_p = _attn_prompt(qp, kp, vp, sinks, batch=bp, seq=sp)
    ck = cache_k[0].reshape(bs, WINDOW, KV_WIDTH)
    cv_ = cache_v[0].reshape(bs, WINDOW, KV_WIDTH)
    att_s, nk_s, nv_s = _attn_sample(qs, kfs, vfs, ck, cv_, sinks, batch=bs, ts=ss)

    cv_p = _conv_prompt(ap, conv_w[0], vec(conv_b[0]), vec(conv_ln_g[0]), vec(conv_ln_b[0]), batch=bp, seq=sp)
    cv_s = _conv_sample(as_, state_conv[0], conv_w[0], vec(conv_b[0]), vec(conv_ln_g[0]), vec(conv_ln_b[0]),
                        batch=bs, ts=ss)

    n_exp = router_w.shape[2]
    assert n_exp == N_EXPERTS
    rwt_bf = router_w[0].T.astype(BF16)
    rbt = jnp.broadcast_to(router_b[0][:, None], (n_exp, LANES))
    zero_carry = jnp.zeros((n_exp, LANES), F32)
    tr = _tile(ns, 512, 16)
    assert np_ % tr == 0
    x1p, hnp, rt_p, tcar_p, cnt_p = _outproj_router(xp2, att_p, cv_p, w_out_bf, g_ffn, rwt_bf, rbt, zero_carry, tm=tr)
    x1s, hns, rt_s, tcar_s, cnt = _outproj_router(xs2, att_s, cv_s, w_out_bf, g_ffn, rwt_bf, rbt, cnt_p, tm=tr)

    tme = EXPERT_BLOCK_ROWS
    nb = -(-(n_tok * TOP_K + N_EXPERTS * (tme - 1 + RUN - 1)) // tme)
    blk_start, blk_exp, blk_src, blk_nvalid, blk_next, n_used, fill_pieces, n_fill = _routing_tables(
        cnt, tme=tme, nb=nb)
    runs_per_tile, run_dst = _run_tables(jnp.concatenate([tcar_p, tcar_s], axis=0), cnt, blk_start, td=tr, tme=tme)
    xs_sorted = _dispatch(hnp, hns, jnp.concatenate([rt_p, rt_s], axis=1), fill_pieces, n_fill, runs_per_tile,
                          run_dst, td=tr, nb=nb, tme=tme)
    ys = _experts(xs_sorted, w1[0], b1[0], w2[0], b2[0], blk_exp, blk_src, blk_nvalid, blk_next, n_used, tme=tme)
    g_fin = final_norm_g.reshape(1, d)
    tiles_p = np_ // tr
    copy_slots = lambda rt: _by_tile(rt[0:TOP_K].astype(jnp.int32) * SUBLANES, tr)
    copy_gates = lambda rt: _by_tile(rt[TOP_K:2 * TOP_K], tr)
    y_p = _combine(x1p, copy_slots(rt_p), copy_gates(rt_p), runs_per_tile, run_dst[:tiles_p], ys, g_fin,
                   tr=tr, first=0)
    y_s = _combine(x1s, copy_slots(rt_s), copy_gates(rt_s), runs_per_tile, run_dst[tiles_p:], ys, g_fin,
                   tr=tr, first=tiles_p)

    kv5 = lambda t, bb: t.reshape(bb, -1, KV_WIDTH)[:, -WINDOW:].reshape(bb, WINDOW, N_KV_HEADS, HEAD_DIM)
    new_k_p = kv5(kfp, bp)[None]
    new_v_p = kv5(vfp, bp)[None]
    ctx = CONV_K - 1
    new_c_p = ap.reshape(bp, sp, cw)[:, -ctx:][None]
    new_c_s = jnp.concatenate([state_conv[0], as_.reshape(bs, ss, cw)], axis=1)[:, -ctx:][None]
    return (y_p.reshape(bp, sp, d), y_s.reshape(bs, ss, d), new_k_p, new_v_p, new_c_p,
            kv5(nk_s, bs)[None], kv5(nv_s, bs)[None], new_c_s)
```

```python
import functools

import jax
import jax.numpy as jnp
from jax import lax
from jax.experimental import pallas as pl
from jax.experimental.pallas import tpu as pltpu

F32 = jnp.float32
BF16 = jnp.bfloat16

HEAD_DIM = 64
N_Q_HEADS = 8
N_KV_HEADS = 2
WINDOW = 128
ROPE_THETA = 500000.0
ROPE_DIM = 16
CONV_K = 31
N_EXPERTS = 32
TOP_K = 4
SWIGLU_LIMIT = 7.0
SWIGLU_ALPHA = 1.702
RMS_EPS = 1e-5
LN_EPS = 1e-5
PAST_LEN = 16384

LANES = 128
SUBLANES = 8
CONV_HALO = 32
VMEM_LIMIT = 56 * 1024 * 1024
EXPERT_BLOCK_ROWS = 512
RUN = 8
FILL_ROWS = 64
ROUTE_T_ROWS = 2 * TOP_K
ISSUE_UNROLL = 8
X_RING = 3

ATTN_WIDTH = N_Q_HEADS * HEAD_DIM
KV_WIDTH = N_KV_HEADS * HEAD_DIM


def _tile(n, pref, mult=8):
    t = min(pref, n)
    while t > 0 and (n % t or t % mult):
        t -= 1
    assert t > 0, (n, pref, mult)
    return t


def _cparams(sem):
    return pltpu.CompilerParams(dimension_semantics=sem, vmem_limit_bytes=VMEM_LIMIT)


def _sigmoid(x):
    return 1.0 / (1.0 + jnp.exp(-x))


def _rope_tables(pos):
    half = ROPE_DIM // 2
    inv_freq = jnp.power(jnp.float32(ROPE_THETA), -jnp.arange(half, dtype=F32) * 2.0 / ROPE_DIM)
    l64 = jnp.arange(LANES) % HEAD_DIM
    assert HEAD_DIM % half == 0
    ang = pos.astype(F32)[:, None] * jnp.tile(inv_freq, LANES // half)[None, :]
    cos_l, sin_l = jnp.cos(ang), jnp.sin(ang)
    c = jnp.where(l64 < ROPE_DIM, cos_l, 1.0)
    s1 = jnp.where(l64 < half, -sin_l, 0.0)
    s2 = jnp.where((l64 >= half) & (l64 < ROPE_DIM), sin_l, 0.0)
    return c.astype(F32), s1.astype(F32), s2.astype(F32)


def _inproj_kernel(x_ref, g_ref, w_ref, c_ref, s1_ref, s2_ref,
                   q_ref, k_ref, v_ref, kf_ref, vf_ref, a_ref, *, conv_width):
    x = x_ref[...]
    ms = jnp.mean(x * x, axis=-1, keepdims=True)
    h = (x * lax.rsqrt(ms + RMS_EPS) * g_ref[...]).astype(BF16)
    z = jnp.dot(h, w_ref[...], preferred_element_type=F32)
    c, s1, s2 = c_ref[...], s1_ref[...], s2_ref[...]
    half = ROPE_DIM // 2

    def rot(t):
        return t * c + pltpu.roll(t, LANES - half, 1) * s1 + pltpu.roll(t, half, 1) * s2

    scale = HEAD_DIM ** -0.5
    for j in range(ATTN_WIDTH // LANES):
        q_ref[:, j * LANES:(j + 1) * LANES] = (rot(z[:, j * LANES:(j + 1) * LANES]) * scale).astype(q_ref.dtype)
    k0 = ATTN_WIDTH
    kr = rot(z[:, k0:k0 + KV_WIDTH])
    k_ref[...] = kr.astype(BF16)
    tail = x.shape[0] - kf_ref.shape[0]
    kf_ref[...] = kr[tail:, :]
    v0 = k0 + KV_WIDTH
    vv = z[:, v0:v0 + KV_WIDTH]
    v_ref[...] = vv.astype(BF16)
    vf_ref[...] = vv[tail:, :]
    u0 = v0 + KV_WIDTH
    g0 = u0 + conv_width
    a_ref[...] = z[:, u0:g0] * _sigmoid(z[:, g0:g0 + conv_width])


def _in_proj(x2, g, w_bf, tables, *, seq_period, q_dtype, conv_width):
    n, d = x2.shape
    in_w = w_bf.shape[1]
    row = lambda w: pl.BlockSpec((tm, w), lambda i: (i, 0))
    if seq_period is None:
        tm = tables[0].shape[0]
        tmap = lambda i: (0, 0)
        kv_f32, kv_rows = row(KV_WIDTH), n
    else:
        tm = _tile(seq_period, 1024, 16)
        per = seq_period // tm
        tmap = lambda i: (i % per, 0)
        assert tm >= WINDOW
        kv_f32, kv_rows = pl.BlockSpec((WINDOW, KV_WIDTH), lambda i: (i // per, 0)), (n // seq_period) * WINDOW
    assert n % tm == 0
    tab = pl.BlockSpec((tm, LANES), tmap)
    return pl.pallas_call(
        functools.partial(_inproj_kernel, conv_width=conv_width),
        grid=(n // tm,),
        in_specs=[row(d), pl.BlockSpec((1, d), lambda i: (0, 0)),
                  pl.BlockSpec((d, in_w), lambda i: (0, 0)), tab, tab, tab],
        out_specs=[row(ATTN_WIDTH), row(KV_WIDTH), row(KV_WIDTH), kv_f32, kv_f32, row(conv_width)],
        out_shape=[jax.ShapeDtypeStruct((n, ATTN_WIDTH), q_dtype),
                   jax.ShapeDtypeStruct((n, KV_WIDTH), BF16),
                   jax.ShapeDtypeStruct((n, KV_WIDTH), BF16),
                   jax.ShapeDtypeStruct((kv_rows, KV_WIDTH), F32),
                   jax.ShapeDtypeStruct((kv_rows, KV_WIDTH), F32),
                   jax.ShapeDtypeStruct((n, conv_width), F32)],
        compiler_params=_cparams(("arbitrary",)),
        name="in_proj",
    )(x2, g, w_bf, *tables)


def _dup_head(t, h):
    sw = pltpu.roll(t, HEAD_DIM, 1)
    low = lax.broadcasted_iota(jnp.int32, t.shape, 1) < HEAD_DIM
    return jnp.where(low, t, sw) if h == 0 else jnp.where(low, sw, t)


def _nt_dot(a, b):
    return lax.dot_general(a, b, (((1,), (1,)), ((), ())), preferred_element_type=F32)


def _attn_prompt_kernel(sink_ref, q_ref, kc_ref, kp_ref, vc_ref, vp_ref, o_ref, *, qb):
    j = pl.program_id(1)
    w = WINDOW
    k_all = jnp.concatenate([kp_ref[...], kc_ref[...]], axis=0).astype(F32)
    v_all = jnp.concatenate([vp_ref[...], vc_ref[...]], axis=0).astype(F32)
    r = lax.broadcasted_iota(jnp.int32, (w, 2 * w), 0)
    kk = lax.broadcasted_iota(jnp.int32, (w, 2 * w), 1)
    band = (kk > r) & (kk <= r + w)
    low = lax.broadcasted_iota(jnp.int32, (w, LANES), 1) < HEAD_DIM
    zero = jnp.zeros((w, LANES), BF16)
    group = N_Q_HEADS // N_KV_HEADS
    for h in range(N_KV_HEADS):
        kd_all = _dup_head(k_all, h).astype(BF16)
        vd_all = _dup_head(v_all, h).astype(BF16)
        for sub in range(qb):
            valid = band & ((kk >= w) | (j > 0)) if sub == 0 else band
            kd = kd_all[sub * w:(sub + 2) * w, :]
            vd = vd_all[sub * w:(sub + 2) * w, :]
            for jj in range(group // 2):
                col = (h * group // 2 + jj) * LANES
                qv = q_ref[sub * w:(sub + 1) * w, col:col + LANES]
                halves = []
                for half in range(2):
                    head = h * group + jj * 2 + half
                    qm = jnp.where(low if half == 0 else ~low, qv, zero)
                    s = jnp.where(valid, _nt_dot(qm, kd), -jnp.inf)
                    sink = sink_ref[head]
                    m = jnp.maximum(jnp.max(s, axis=-1, keepdims=True), sink)
                    p = jnp.exp(s - m)
                    den = jnp.sum(p, axis=-1, keepdims=True) + jnp.exp(sink - m)
                    o = jnp.dot(p.astype(BF16), vd, preferred_element_type=F32)
                    halves.append(o / den)
                o_ref[sub * w:(sub + 1) * w, col:col + LANES] = (
                    jnp.where(low, halves[0], halves[1]).astype(o_ref.dtype))


def _attn_prompt(q, k, v, sinks, *, batch, seq):
    nb = seq // WINDOW
    qb = next(c for c in (4, 2, 1) if nb % c == 0)
    steps = nb // qb
    cur = lambda b, j: (b * steps + j, 0)
    prev = lambda b, j: (b * nb + jnp.maximum(j * qb - 1, 0), 0)
    return pl.pallas_call(
        functools.partial(_attn_prompt_kernel, qb=qb),
        grid=(batch, steps),
        in_specs=[pl.BlockSpec(memory_space=pltpu.SMEM),
                  pl.BlockSpec((qb * WINDOW, ATTN_WIDTH), cur),
                  pl.BlockSpec((qb * WINDOW, KV_WIDTH), cur), pl.BlockSpec((WINDOW, KV_WIDTH), prev),
                  pl.BlockSpec((qb * WINDOW, KV_WIDTH), cur), pl.BlockSpec((WINDOW, KV_WIDTH), prev)],
        out_specs=pl.BlockSpec((qb * WINDOW, ATTN_WIDTH), cur),
        out_shape=jax.ShapeDtypeStruct((batch * seq, ATTN_WIDTH), BF16),
        compiler_params=_cparams(("parallel", "parallel")),
        name="attn_prompt",
    )(sinks, q, k, k, v, v)


def _attn_sample_kernel(sink_ref, q_ref, kn_ref, vn_ref, ck_ref, cv_ref, o_ref, nk_ref, nv_ref, *, gb, ts):
    w = WINDOW
    group = N_Q_HEADS // N_KV_HEADS
    rows = group * ts
    low = lax.broadcasted_iota(jnp.int32, (ts, LANES), 1) < HEAD_DIM
    pad = jnp.zeros((ts, LANES), F32)
    s_c, s_n, v_dup = [], [], []
    for b in range(gb):
        kc, vc = ck_ref[b], cv_ref[b]
        kn, vn = kn_ref[b * ts:(b + 1) * ts, :], vn_ref[b * ts:(b + 1) * ts, :]
        nk_ref[b, 0:w - ts, :] = kc[ts:, :]
        nk_ref[b, w - ts:, :] = kn
        nv_ref[b, 0:w - ts, :] = vc[ts:, :]
        nv_ref[b, w - ts:, :] = vn
        knp = jnp.concatenate([kn, pad], axis=0)
        vnp = jnp.concatenate([vn, pad], axis=0)
        qb = q_ref[b * ts:(b + 1) * ts, :]
        for h in range(N_KV_HEADS):
            parts = []
            for jj in range(group // 2):
                col = (h * group // 2 + jj) * LANES
                qv = qb[:, col:col + LANES]
                parts += [jnp.where(low, qv, 0.0), jnp.where(low, 0.0, qv)]
            lhs = jnp.concatenate(parts, axis=0).astype(BF16)
            s_c.append(_nt_dot(lhs, _dup_head(kc, h).astype(BF16)))
            s_n.append(_nt_dot(lhs, _dup_head(knp, h).astype(BF16)))
            v_dup.append((_dup_head(vc, h).astype(BF16), _dup_head(vnp, h).astype(BF16)))
    s_c = jnp.concatenate(s_c, axis=0)
    s_n = jnp.concatenate(s_n, axis=0)
    n_rows = s_c.shape[0]
    ridx = lax.broadcasted_iota(jnp.int32, (n_rows, 1), 0)
    t_row = ridx % ts
    head_row = (ridx // ts) % N_Q_HEADS
    sink = jnp.zeros((n_rows, 1), F32)
    for hd in range(N_Q_HEADS):
        sink = jnp.where(head_row == hd, sink_ref[hd], sink)
    c_idx = lax.broadcasted_iota(jnp.int32, (n_rows, w), 1)
    n_idx = lax.broadcasted_iota(jnp.int32, (n_rows, 2 * ts), 1)
    s_c = jnp.where(c_idx > t_row, s_c, -jnp.inf)
    s_n = jnp.where(n_idx <= t_row, s_n, -jnp.inf)
    m = jnp.maximum(jnp.maximum(jnp.max(s_c, axis=-1, keepdims=True), jnp.max(s_n, axis=-1, keepdims=True)), sink)
    p_c = jnp.exp(s_c - m)
    p_n = jnp.exp(s_n - m)
    den = jnp.sum(p_c, axis=-1, keepdims=True) + jnp.sum(p_n, axis=-1, keepdims=True) + jnp.exp(sink - m)
    p_c = p_c.astype(BF16)
    p_n = p_n.astype(BF16)
    outs = []
    for b in range(gb):
        cols = []
        for h in range(N_KV_HEADS):
            ci = b * N_KV_HEADS + h
            sl = slice(ci * rows, (ci + 1) * rows)
            vdc, vdn = v_dup[ci]
            o = (jnp.dot(p_c[sl], vdc, preferred_element_type=F32)
                 + jnp.dot(p_n[sl], vdn, preferred_element_type=F32)) / den[sl]
            for jj in range(group // 2):
                lo_part = o[(2 * jj) * ts:(2 * jj + 1) * ts, :]
                hi_part = o[(2 * jj + 1) * ts:(2 * jj + 2) * ts, :]
                cols.append(jnp.where(low, lo_part, hi_part))
        outs.append(jnp.concatenate(cols, axis=1))
    o_ref[...] = jnp.concatenate(outs, axis=0).astype(o_ref.dtype)


def _attn_sample(q, kf, vf, cache_k, cache_v, sinks, *, batch, ts):
    assert ts % 8 == 0 and ts <= WINDOW
    gb = _tile(batch, 16, 2)
    tok = lambda w: pl.BlockSpec((gb * ts, w), lambda i: (i, 0))
    cache = pl.BlockSpec((gb, WINDOW, KV_WIDTH), lambda i: (i, 0, 0))
    cshape = jax.ShapeDtypeStruct((batch, WINDOW, KV_WIDTH), F32)
    return pl.pallas_call(
        functools.partial(_attn_sample_kernel, gb=gb, ts=ts),
        grid=(batch // gb,),
        in_specs=[pl.BlockSpec(memory_space=pltpu.SMEM), tok(ATTN_WIDTH), tok(KV_WIDTH), tok(KV_WIDTH),
                  cache, cache],
        out_specs=[tok(ATTN_WIDTH), cache, cache],
        out_shape=[jax.ShapeDtypeStruct((batch * ts, ATTN_WIDTH), BF16), cshape, cshape],
        compiler_params=_cparams(("parallel",)),
        name="attn_sample",
    )(sinks, q, kf, vf, cache_k, cache_v)


def _ln_swish(acc, b, lg, lb):
    y = acc + b
    mu = jnp.mean(y, axis=-1, keepdims=True)
    yc = y - mu
    var = jnp.mean(yc * yc, axis=-1, keepdims=True)
    yn = yc * lax.rsqrt(var + LN_EPS) * lg + lb
    return yn * _sigmoid(yn)


def _conv_prompt_kernel(a_ref, ap_ref, w_ref, b_ref, lg_ref, lb_ref, o_ref, win_ref, *, tt, rc):
    j = pl.program_id(1)
    n = CONV_HALO + tt
    win = jnp.concatenate([jnp.where(j > 0, ap_ref[...], 0.0), a_ref[...]], axis=0)
    win_ref[0] = win
    for r in range(1, SUBLANES):
        win_ref[r] = pltpu.roll(win, n - r, 0)
    off = CONV_HALO - (CONV_K - 1)
    b, lg, lb = b_ref[...], lg_ref[...], lb_ref[...]
    for c in range(tt // rc):
        acc = jnp.zeros((rc, a_ref.shape[1]), F32)
        for k in range(CONV_K):
            s = off + k
            base = c * rc + (s // SUBLANES) * SUBLANES
            wk = jnp.concatenate([w_ref[k]] * (rc // SUBLANES), axis=0)
            acc = acc + wk * win_ref[s % SUBLANES, base:base + rc, :]
        o_ref[c * rc:(c + 1) * rc, :] = _ln_swish(acc, b, lg, lb).astype(o_ref.dtype)


def _conv_prompt(a, w, b, lg, lb, *, batch, seq):
    cw = a.shape[1]
    tt = _tile(seq, 512, CONV_HALO)
    rc = _tile(tt, 32, 16)
    nt = seq // tt
    per = tt // CONV_HALO
    cur = lambda bb, j: (bb * nt + j, 0)
    prev = lambda bb, j: (jnp.maximum((bb * nt + j) * per - 1, 0), 0)
    vec = pl.BlockSpec((1, cw), lambda bb, j: (0, 0))
    return pl.pallas_call(
        functools.partial(_conv_prompt_kernel, tt=tt, rc=rc),
        grid=(batch, nt),
        in_specs=[pl.BlockSpec((tt, cw), cur), pl.BlockSpec((CONV_HALO, cw), prev),
                  pl.BlockSpec((CONV_K, SUBLANES, cw), lambda bb, j: (0, 0, 0)), vec, vec, vec],
        out_specs=pl.BlockSpec((tt, cw), cur),
        out_shape=jax.ShapeDtypeStruct((batch * seq, cw), BF16),
        scratch_shapes=[pltpu.VMEM((SUBLANES, CONV_HALO + tt, cw), F32)],
        compiler_params=_cparams(("parallel", "parallel")),
        name="conv_prompt",
    )(a, a, jnp.broadcast_to(w[:, None, :], (CONV_K, SUBLANES, cw)), b, lg, lb)


def _conv_sample_kernel(a_ref, st_ref, w_ref, b_ref, lg_ref, lb_ref, o_ref, win_ref, *, gb, ts):
    ctx = CONV_K - 1
    b, lg, lb = b_ref[...], lg_ref[...], lb_ref[...]
    for bb in range(gb):
        win_ref[bb, 0:ctx, :] = st_ref[bb]
        win_ref[bb, ctx:ctx + ts, :] = a_ref[bb * ts:(bb + 1) * ts, :]
    outs = []
    for bb in range(gb):
        acc = jnp.zeros((ts, a_ref.shape[1]), F32)
        for k in range(CONV_K):
            wk = jnp.concatenate([w_ref[k]] * (ts // SUBLANES), axis=0)
            acc = acc + wk * win_ref[bb, k:k + ts, :]
        outs.append(_ln_swish(acc, b, lg, lb))
    o_ref[...] = jnp.concatenate(outs, axis=0).astype(o_ref.dtype)


def _conv_sample(a, state, w, b, lg, lb, *, batch, ts):
    cw = a.shape[1]
    ctx = CONV_K - 1
    gb = _tile(batch, 16, 2)
    vec = pl.BlockSpec((1, cw), lambda i: (0, 0))
    return pl.pallas_call(
        functools.partial(_conv_sample_kernel, gb=gb, ts=ts),
        grid=(batch // gb,),
        in_specs=[pl.BlockSpec((gb * ts, cw), lambda i: (i, 0)),
                  pl.BlockSpec((gb, ctx, cw), lambda i: (i, 0, 0)),
                  pl.BlockSpec((CONV_K, SUBLANES, cw), lambda i: (0, 0, 0)), vec, vec, vec],
        out_specs=pl.BlockSpec((gb * ts, cw), lambda i: (i, 0)),
        out_shape=jax.ShapeDtypeStruct((batch * ts, cw), BF16),
        scratch_shapes=[pltpu.VMEM((gb, ctx + ts + 2, cw), F32)],
        compiler_params=_cparams(("parallel",)),
        name="conv_sample",
    )(a, state, jnp.broadcast_to(w[:, None, :], (CONV_K, SUBLANES, cw)), b, lg, lb)


def _outproj_router_kernel(x_hbm, att_ref, cv_ref, wo_ref, g_ref, rw_ref, rb_ref, cin_ref, before_ref,
                           x1_ref, hn_ref, rt_ref, tcar_ref, cnt_ref, carry_ref, xbuf_ref, xsem):
    i = pl.program_id(0)
    steps = pl.num_programs(0)
    tm = xbuf_ref.shape[1]

    def fetch(step):
        slot = step % X_RING
        return pltpu.make_async_copy(x_hbm.at[pl.ds(pl.multiple_of(step * tm, tm), tm)], xbuf_ref.at[slot],
                                     xsem.at[slot])

    @pl.when(i == 0)
    def _():
        carry_ref[...] = cin_ref[...]
        for s in range(X_RING - 1):
            @pl.when(s < steps)
            def _():
                fetch(s).start()

    @pl.when(i + X_RING - 1 < steps)
    def _():
        fetch(i + X_RING - 1).start()

    fetch(i).wait()
    x_ref = xbuf_ref.at[i % X_RING]
    aw = att_ref.shape[1]
    n_exp = rw_ref.shape[0]
    before = before_ref[...]
    eid = lax.broadcasted_iota(jnp.int32, (n_exp, tm), 0)
    carry = carry_ref[...][:, 0:1]
    tcar_ref[0] = carry_ref[...]
    mix = (jnp.dot(att_ref[...], wo_ref[0:aw, :], preferred_element_type=F32)
           + jnp.dot(cv_ref[...], wo_ref[aw:, :], preferred_element_type=F32))
    x1 = x_ref[...] + mix
    x1_ref[...] = x1
    ms = jnp.mean(x1 * x1, axis=-1, keepdims=True)
    hn = (x1 * lax.rsqrt(ms + RMS_EPS) * g_ref[...]).astype(BF16)
    hn_ref[...] = hn
    logits = _nt_dot(rw_ref[...], hn) + rb_ref[...][:, 0:1]
    onehot = jnp.zeros((n_exp, tm), F32)
    vals, idxs = [], []
    for _ in range(TOP_K):
        m = jnp.max(logits, axis=0, keepdims=True)
        idx = jnp.min(jnp.where(logits == m, eid, n_exp), axis=0, keepdims=True)
        sel = eid == idx
        onehot = onehot + sel.astype(F32)
        logits = jnp.where(sel, -jnp.inf, logits)
        vals.append(m)
        idxs.append(idx)
    es = [jnp.exp(v - vals[0]) for v in vals]
    den = es[0] + es[1] + es[2] + es[3]
    local = jnp.dot(onehot.astype(BF16), before, preferred_element_type=F32)
    count = jnp.sum(onehot, axis=1, keepdims=True)
    padded = jnp.ceil(count * (1.0 / RUN)) * RUN
    er = lax.broadcasted_iota(jnp.int32, (n_exp, n_exp), 0)
    ec = lax.broadcasted_iota(jnp.int32, (n_exp, n_exp), 1)
    start = jnp.dot((ec < er).astype(BF16), jnp.broadcast_to(padded, (n_exp, LANES)).astype(BF16),
                    preferred_element_type=F32)[:, 0:1]
    slot_of = local + start
    slots = [jnp.sum(jnp.where(eid == idxs[k], slot_of, 0.0), axis=0, keepdims=True) for k in range(TOP_K)]
    rt_ref[...] = jnp.concatenate(slots + [e / den for e in es], axis=0)
    carry = carry + count
    carry_ref[...] = jnp.broadcast_to(carry, carry_ref.shape)
    cnt_ref[...] = jnp.broadcast_to(carry, cnt_ref.shape)


def _store_slabs(ref, val, rows):
    for j in range(val.shape[1] // LANES):
        ref[pl.ds(j, rows, stride=SUBLANES), :] = val[:, j * LANES:(j + 1) * LANES]


def _load_slabs(ref, rows, dtype):
    return jnp.concatenate([ref[pl.ds(j, rows, stride=SUBLANES), :].astype(dtype) for j in range(SUBLANES)],
                           axis=1)


def _outproj_router(x2, att, cv, wo_bf, g, rwt_bf, rbt, carry_in, *, tm):
    n, d = x2.shape
    assert d == SUBLANES * LANES and n % tm == 0
    n_exp = rwt_bf.shape[0]
    row = lambda w: pl.BlockSpec((tm, w), lambda i: (i, 0))
    full = lambda a: pl.BlockSpec(a.shape, lambda i: (0,) * a.ndim)
    before = (jnp.arange(tm)[:, None] < jnp.arange(tm)[None, :]).astype(BF16)
    return pl.pallas_call(
        _outproj_router_kernel,
        grid=(n // tm,),
        in_specs=[pl.BlockSpec(memory_space=pl.ANY), row(att.shape[1]), row(cv.shape[1]), full(wo_bf), full(g),
                  full(rwt_bf), full(rbt), full(carry_in), full(before)],
        out_specs=[row(d), row(d),
                   pl.BlockSpec((ROUTE_T_ROWS, tm), lambda i: (0, i)),
                   pl.BlockSpec((1, n_exp, LANES), lambda i: (i, 0, 0)),
                   pl.BlockSpec((n_exp, LANES), lambda i: (0, 0))],
        out_shape=[jax.ShapeDtypeStruct((n, d), F32), jax.ShapeDtypeStruct((n, d), BF16),
                   jax.ShapeDtypeStruct((ROUTE_T_ROWS, n), F32),
                   jax.ShapeDtypeStruct((n // tm, n_exp, LANES), F32),
                   jax.ShapeDtypeStruct((n_exp, LANES), F32)],
        scratch_shapes=[pltpu.VMEM((n_exp, LANES), F32), pltpu.VMEM((X_RING, tm, d), F32),
                        pltpu.SemaphoreType.DMA((X_RING,))],
        compiler_params=_cparams(("arbitrary",)),
        name="outproj_router",
    )(x2, att, cv, wo_bf, g, rwt_bf, rbt, carry_in, before)


def _dispatch_kernel(fill_ref, nfill_ref, tot_ref, dst_ref, rt_ref, hn_ref, hn2_ref,
                     xs_ref, zero_ref, stage_ref, sem, zsem, *, td, n_first, chunk):
    i = pl.program_id(0)
    last = pl.num_programs(0) - 1
    par = i % 2
    n_slots = stage_ref.shape[1] // SUBLANES

    def fill(f, s):
        row = pl.multiple_of(fill_ref[f] * (FILL_ROWS * SUBLANES), FILL_ROWS * SUBLANES)
        return pltpu.make_async_copy(zero_ref, xs_ref.at[pl.ds(row, FILL_ROWS * SUBLANES)], zsem.at[s])

    def fills(lo, hi, s, act):
        def body(f, carry):
            act(fill(f, s))
            return carry

        lax.fori_loop(lo, hi, body, 0)

    @pl.when(i == 0)
    def _():
        zero_ref[...] = jnp.zeros(zero_ref.shape, F32)
        fills(0, nfill_ref[0], 0, lambda c: c.start())
        fills(nfill_ref[0], nfill_ref[1], 1, lambda c: c.start())
        fills(0, nfill_ref[0], 0, lambda c: c.wait())

    def group_by_expert(tok_ref):
        hn = tok_ref[...]
        slots = rt_ref[0:TOP_K, :].astype(jnp.int32)
        for c in range(n_slots // chunk):
            s = lax.broadcasted_iota(jnp.int32, (chunk, td), 0) + c * chunk
            hit = s == slots[0:1, :]
            for k in range(1, TOP_K):
                hit = hit | (s == slots[k:k + 1, :])
            rows = jnp.dot(jnp.where(hit, 1.0, 0.0).astype(BF16), hn, preferred_element_type=F32)
            _store_slabs(stage_ref.at[par, pl.ds(c * chunk * SUBLANES, chunk * SUBLANES)], rows, chunk)

    def run_copy(buf, src_slot, dst_row, runs=1):
        src = pl.multiple_of(src_slot * SUBLANES, RUN * SUBLANES)
        dst = pl.multiple_of(dst_row * SUBLANES, SUBLANES)
        return pltpu.make_async_copy(stage_ref.at[buf, pl.ds(src, runs * RUN * SUBLANES)],
                                     xs_ref.at[pl.ds(dst, runs * RUN * SUBLANES)], sem)

    def wait_runs(tile):
        def wait_many(q, carry):
            run_copy(0, 0, 0, ISSUE_UNROLL).wait()
            return carry

        def wait_one(j, carry):
            run_copy(0, 0, 0).wait()
            return carry

        n_many = tot_ref[tile] // ISSUE_UNROLL
        lax.fori_loop(0, n_many, wait_many, 0)
        lax.fori_loop(n_many * ISSUE_UNROLL, tot_ref[tile], wait_one, 0)

    @pl.when(i >= 2)
    def _():
        wait_runs(i - 2)

    @pl.when(i >= 1)
    def _():
        def issue(j):
            run_copy(1 - par, j * RUN, dst_ref[0, 0, j]).start()

        def issue_many(q, carry):
            for u in range(ISSUE_UNROLL):
                issue(q * ISSUE_UNROLL + u)
            return carry

        def issue_one(j, carry):
            issue(j)
            return carry

        n_full = tot_ref[i - 1] // ISSUE_UNROLL
        lax.fori_loop(0, n_full, issue_many, 0)
        lax.fori_loop(n_full * ISSUE_UNROLL, tot_ref[i - 1], issue_one, 0)

    @pl.when(i < n_first)
    def _():
        group_by_expert(hn_ref)

    @pl.when((i >= n_first) & (i < last))
    def _():
        group_by_expert(hn2_ref)

    @pl.when(i == last)
    def _():
        wait_runs(i - 1)
        fills(nfill_ref[0], nfill_ref[1], 1, lambda c: c.wait())


def _stage_slots(td):
    return td * TOP_K + N_EXPERTS * RUN


def _by_tile(field, td):
    n = field.shape[1]
    return field.reshape(TOP_K, n // td, td).transpose(1, 0, 2).reshape(n // td, 1, TOP_K * td)


def _dispatch(hn_a, hn_b, route_t, fill_pieces, n_fill, runs_per_tile, run_dst, *, td, nb, tme):
    na, d = hn_a.shape
    assert na % td == 0 and hn_b.shape[0] % td == 0 and td % RUN == 0
    n_first = na // td
    n_second = hn_b.shape[0] // td
    tiles = n_first + n_second
    n_slots = _stage_slots(td)
    chunk = _tile(n_slots, 768, SUBLANES)
    tok = lambda m: pl.BlockSpec((td, d), m)
    grid_spec = pltpu.PrefetchScalarGridSpec(
        num_scalar_prefetch=3,
        grid=(tiles + 1,),
        in_specs=[pl.BlockSpec((1, 1, n_slots // RUN), lambda i, *_: (jnp.maximum(i - 1, 0), 0, 0),
                               memory_space=pltpu.SMEM),
                  pl.BlockSpec((ROUTE_T_ROWS, td), lambda i, *_: (0, jnp.minimum(i, tiles - 1))),
                  tok(lambda i, *_: (jnp.minimum(i, n_first - 1), 0)),
                  tok(lambda i, *_: (jnp.clip(i - n_first, 0, n_second - 1), 0))],
        out_specs=pl.BlockSpec(memory_space=pl.ANY),
        scratch_shapes=[pltpu.VMEM((FILL_ROWS * SUBLANES, LANES), F32),
                        pltpu.VMEM((2, n_slots * SUBLANES, LANES), F32),
                        pltpu.SemaphoreType.DMA(()), pltpu.SemaphoreType.DMA((2,))],
    )
    return pl.pallas_call(
        functools.partial(_dispatch_kernel, td=td, n_first=n_first, chunk=chunk),
        grid_spec=grid_spec,
        out_shape=jax.ShapeDtypeStruct((nb * tme * SUBLANES, LANES), F32),
        compiler_params=_cparams(("arbitrary",)),
        name="dispatch",
    )(fill_pieces, n_fill, runs_per_tile, run_dst, route_t, hn_a, hn_b)


def _experts_kernel(be_ref, bsrc_ref, nv_ref, nx_ref, nu_ref, x_ref, w1_hbm, b1_ref, w2_hbm, b2_ref,
                    y_ref, w1f_ref, w2f_ref, w1b_ref, w2b_ref, par_ref, wsem, *, tme):
    i = pl.program_id(0)
    e = be_ref[i]
    e_prev = be_ref[jnp.maximum(i - 1, 0)]
    d_ff = w2f_ref.shape[1]
    half = tme // 2

    def fetch(expert, s):
        return (pltpu.make_async_copy(w1_hbm.at[expert], w1f_ref.at[s], wsem.at[0, s]),
                pltpu.make_async_copy(w2_hbm.at[expert], w2f_ref.at[s], wsem.at[1, s]))

    @pl.when(i == 0)
    def _():
        par_ref[0] = 0
        for c in fetch(e, 0):
            c.start()

    @pl.when((i == 0) | (e != e_prev))
    def _():
        s = par_ref[0]
        for c in fetch(e, s):
            c.wait()
        w1b_ref[...] = w1f_ref[s].astype(BF16)
        w2b_ref[...] = w2f_ref[s].astype(BF16)
        nxt = nx_ref[i]

        @pl.when(nxt >= 0)
        def _():
            for c in fetch(nxt, 1 - s):
                c.start()

        par_ref[0] = 1 - s

    def ffn(x):
        h = jnp.dot(x, w1b_ref[...], preferred_element_type=F32) + b1_ref[0]
        x_glu = jnp.minimum(h[:, :d_ff], SWIGLU_LIMIT)
        x_lin = jnp.clip(h[:, d_ff:], -SWIGLU_LIMIT, SWIGLU_LIMIT)
        act = x_glu * _sigmoid(SWIGLU_ALPHA * x_glu) * (x_lin + 1.0)
        return jnp.dot(act.astype(BF16), w2b_ref[...], preferred_element_type=F32) + b2_ref[0]

    used = i < nu_ref[0]
    nv = nv_ref[i]

    @pl.when(used & (nv > half))
    def _():
        _store_slabs(y_ref, ffn(_load_slabs(x_ref, tme, BF16)), tme)

    @pl.when(used & (nv <= half))
    def _():
        rows = half * SUBLANES
        _store_slabs(y_ref.at[pl.ds(0, rows)], ffn(_load_slabs(x_ref.at[pl.ds(0, rows)], half, BF16)), half)
        y_ref[pl.ds(rows, rows), :] = jnp.zeros((rows, LANES), F32)

    @pl.when(i == nu_ref[0])
    def _():
        y_ref[...] = jnp.zeros(y_ref.shape, F32)


def _experts(xs, w1, b1, w2, b2, blk_exp, blk_src, blk_nvalid, blk_next, n_used, *, tme):
    n_exp, d, h2 = w1.shape
    d_ff = w2.shape[1]
    nb = xs.shape[0] // (tme * SUBLANES)
    slab = lambda m: pl.BlockSpec((tme * SUBLANES, LANES), m)
    grid_spec = pltpu.PrefetchScalarGridSpec(
        num_scalar_prefetch=5,
        grid=(nb,),
        in_specs=[slab(lambda i, be, bs, nv, nx, nu: (bs[i], 0)),
                  pl.BlockSpec(memory_space=pl.ANY),
                  pl.BlockSpec((1, 1, h2), lambda i, be, bs, nv, nx, nu: (be[i], 0, 0)),
                  pl.BlockSpec(memory_space=pl.ANY),
                  pl.BlockSpec((1, 1, d), lambda i, be, bs, nv, nx, nu: (be[i], 0, 0))],
        out_specs=slab(lambda i, be, bs, nv, nx, nu: (jnp.minimum(i, nu[0]), 0)),
        scratch_shapes=[pltpu.VMEM((2, d, h2), F32), pltpu.VMEM((2, d_ff, d), F32),
                        pltpu.VMEM((d, h2), BF16), pltpu.VMEM((d_ff, d), BF16),
                        pltpu.SMEM((1,), jnp.int32), pltpu.SemaphoreType.DMA((2, 2))],
    )
    return pl.pallas_call(
        functools.partial(_experts_kernel, tme=tme),
        grid_spec=grid_spec,
        out_shape=jax.ShapeDtypeStruct(xs.shape, F32),
        input_output_aliases={5: 0},
        compiler_params=_cparams(("arbitrary",)),
        name="experts",
    )(blk_exp, blk_src, blk_nvalid, blk_next, n_used, xs, w1, b1.reshape(n_exp, 1, h2), w2,
      b2.reshape(n_exp, 1, d))


def _combine_kernel(tot_ref, rcur_ref, rnext_ref, slot_ref, gate_ref, x1_ref, g_ref, ys_ref, o_ref,
                    stage_ref, moe_ref, sem, *, tr, first, sub):
    i = pl.program_id(0)
    par = i % 2

    def run_copy(buf, j, src_row, runs=1):
        src = pl.multiple_of(src_row * SUBLANES, SUBLANES)
        dst = pl.multiple_of(j * (RUN * SUBLANES), RUN * SUBLANES)
        return pltpu.make_async_copy(ys_ref.at[pl.ds(src, runs * RUN * SUBLANES)],
                                     stage_ref.at[buf, pl.ds(dst, runs * RUN * SUBLANES)], sem.at[buf])

    def gather(rref, buf, n):
        def issue_many(q, carry):
            for u in range(ISSUE_UNROLL):
                j = q * ISSUE_UNROLL + u
                run_copy(buf, j, rref[0, 0, j]).start()
            return carry

        def issue_one(j, carry):
            run_copy(buf, j, rref[0, 0, j]).start()
            return carry

        lax.fori_loop(0, n // ISSUE_UNROLL, issue_many, 0)
        lax.fori_loop((n // ISSUE_UNROLL) * ISSUE_UNROLL, n, issue_one, 0)

    @pl.when(i == 0)
    def _():
        gather(rcur_ref, 0, tot_ref[first])

    @pl.when(i + 1 < pl.num_programs(0))
    def _():
        gather(rnext_ref, 1 - par, tot_ref[first + i + 1])

    def wait_many(q, carry):
        run_copy(par, 0, 0, ISSUE_UNROLL).wait()
        return carry

    def wait_one(j, carry):
        run_copy(par, 0, 0).wait()
        return carry

    n_runs = tot_ref[first + i]
    lax.fori_loop(0, n_runs // ISSUE_UNROLL, wait_many, 0)
    lax.fori_loop((n_runs // ISSUE_UNROLL) * ISSUE_UNROLL, n_runs, wait_one, 0)

    def token(buf, r, carry):
        acc = None
        for k in range(TOP_K):
            s = pl.multiple_of(slot_ref[0, 0, k * tr + r], SUBLANES)
            term = gate_ref[0, 0, k * tr + r] * stage_ref[buf, pl.ds(s, SUBLANES), :]
            acc = term if acc is None else acc + term
        moe_ref[pl.ds(pl.multiple_of(r * SUBLANES, SUBLANES), SUBLANES), :] = acc
        return carry

    for buf in range(2):
        @pl.when(par == buf)
        def _():
            lax.fori_loop(0, tr, functools.partial(token, buf), 0, unroll=ISSUE_UNROLL)

    for c in range(tr // sub):
        rows = slice(c * sub, (c + 1) * sub)
        y = x1_ref[rows, :] + _load_slabs(moe_ref.at[pl.ds(c * sub * SUBLANES, sub * SUBLANES)], sub, F32)
        ms = jnp.mean(y * y, axis=-1, keepdims=True)
        o_ref[rows, :] = y * lax.rsqrt(ms + RMS_EPS) * g_ref[...]


def _combine(x1, slots, gates, runs_per_tile, run_src, ys, g, *, tr, first):
    n, d = x1.shape
    steps = n // tr
    n_slots = _stage_slots(tr)
    smem = lambda a, m: pl.BlockSpec((1, 1, a.shape[2]), m, memory_space=pltpu.SMEM)
    cur = lambda i, tot: (i, 0, 0)
    nxt = lambda i, tot: (jnp.minimum(i + 1, steps - 1), 0, 0)
    grid_spec = pltpu.PrefetchScalarGridSpec(
        num_scalar_prefetch=1,
        grid=(steps,),
        in_specs=[smem(run_src, cur), smem(run_src, nxt), smem(slots, cur), smem(gates, cur),
                  pl.BlockSpec((tr, d), lambda i, tot: (i, 0)),
                  pl.BlockSpec((1, d), lambda i, tot: (0, 0)),
                  pl.BlockSpec(memory_space=pl.ANY)],
        out_specs=pl.BlockSpec((tr, d), lambda i, tot: (i, 0)),
        scratch_shapes=[pltpu.VMEM((2, n_slots * SUBLANES, LANES), F32), pltpu.VMEM((tr * SUBLANES, LANES), F32),
                        pltpu.SemaphoreType.DMA((2,))],
    )
    return pl.pallas_call(
        functools.partial(_combine_kernel, tr=tr, first=first, sub=_tile(tr, 128, 8)),
        grid_spec=grid_spec,
        out_shape=jax.ShapeDtypeStruct((n, d), F32),
        compiler_params=_cparams(("arbitrary",)),
        name="combine",
    )(runs_per_tile, run_src, run_src, slots, gates, x1, g, ys)


def _routing_tables(counts_f, *, tme, nb):
    counts = counts_f[:, 0].astype(jnp.int32)
    nblk = jnp.where(counts > 0, (counts + RUN - 1 + tme - 1) // tme, 0)
    blk_end = jnp.cumsum(nblk)
    blk_start = blk_end - nblk
    n_used = blk_end[-1]
    b = jnp.arange(nb, dtype=jnp.int32)
    used = b < n_used
    blk_exp = jnp.minimum(jnp.sum((b[:, None] >= blk_end[None, :]).astype(jnp.int32), axis=1), N_EXPERTS - 1)
    last_exp = jnp.max(jnp.where(nblk > 0, jnp.arange(N_EXPERTS, dtype=jnp.int32), 0))
    blk_exp = jnp.where(used, blk_exp, last_exp).astype(jnp.int32)
    blk_src = jnp.minimum(b, n_used - 1).astype(jnp.int32)
    experts = jnp.arange(N_EXPERTS, dtype=jnp.int32)
    mine = (b[:, None] >= blk_start[None, :]) & (b[:, None] < blk_end[None, :])
    nvalid = jnp.sum(jnp.where(mine, counts[None, :] - (b[:, None] - blk_start[None, :]) * tme, 0), axis=1)
    nvalid = jnp.clip(nvalid, 0, tme).astype(jnp.int32)
    later = (experts[None, :] > experts[:, None]) & (nblk[None, :] > 0)
    nxt_e = jnp.min(jnp.where(later, experts[None, :], N_EXPERTS), axis=1)
    nxt_e = jnp.where(nxt_e == N_EXPERTS, -1, nxt_e)
    blk_next = jnp.sum(jnp.where(blk_exp[:, None] == experts[None, :], nxt_e[None, :], 0), axis=1).astype(jnp.int32)
    per_blk = tme // FILL_ROWS
    piece = jnp.arange(per_blk, dtype=jnp.int32)[None, :]
    tail = (used[:, None] & (piece >= (nvalid // FILL_ROWS)[:, None])).reshape(-1)
    idle = jnp.repeat(~used, per_blk)
    fill_pieces = jnp.argsort(jnp.where(tail, 0, jnp.where(idle, 1, 2)), stable=True).astype(jnp.int32)
    n_fill = jnp.stack([jnp.sum(tail), jnp.sum(tail | idle)]).astype(jnp.int32)
    return (blk_start, blk_exp, blk_src, nvalid, blk_next, n_used.reshape(1).astype(jnp.int32), fill_pieces,
            n_fill)


def _run_tables(tile_carry_f, counts_f, blk_start, *, td, tme):
    before = tile_carry_f[:, :, 0].astype(jnp.int32)
    counts = counts_f[:, 0].astype(jnp.int32)
    in_tile = jnp.concatenate([before[1:], counts[None, :]], axis=0) - before
    n_runs = (in_tile + RUN - 1) // RUN
    ends = jnp.cumsum(n_runs, axis=1)
    j = jnp.arange(_stage_slots(td) // RUN, dtype=jnp.int32)
    owner = jnp.sum((j[None, :, None] >= ends[:, None, :]).astype(jnp.int32), axis=2)
    base = blk_start[None, :] * tme + before - (ends - n_runs) * RUN
    mine = owner[:, :, None] == jnp.arange(N_EXPERTS, dtype=jnp.int32)[None, None, :]
    dst = jnp.sum(jnp.where(mine, base[:, None, :], 0), axis=2) + j[None, :] * RUN
    return ends[:, -1].astype(jnp.int32), dst[:, None, :].astype(jnp.int32)


def kernel(x_prompt, x_sample, cache_k, cache_v, state_conv, attn_norm_g, w_in, attn_sinks, conv_w, conv_b,
           conv_ln_g, conv_ln_b, w_out, ffn_norm_g, router_w, router_b, w1, b1, w2, b2, final_norm_g):
    depth = w_in.shape[0]
    assert depth == 1, "single-layer step"
    bp, sp, d = x_prompt.shape
    bs, ss, _ = x_sample.shape
    cw = conv_w.shape[2]
    np_, ns = bp * sp, bs * ss
    n_tok = np_ + ns
    assert sp % WINDOW == 0

    xp2 = x_prompt.reshape(np_, d)
    xs2 = x_sample.reshape(ns, d)
    w_in_bf = w_in[0].astype(BF16)
    w_out_bf = w_out[0].astype(BF16)
    g_attn = attn_norm_g[0].reshape(1, d)
    g_ffn = ffn_norm_g[0].reshape(1, d)
    sinks = attn_sinks[0]
    vec = lambda a: a.reshape(1, cw)

    tab_p = _rope_tables(jnp.arange(sp, dtype=jnp.int32))
    tms = _tile(ns, 512, max(ss, 16))
    tab_s = _rope_tables(PAST_LEN + (jnp.arange(tms, dtype=jnp.int32) % ss))
    qp, kp, vp, kfp, vfp, ap = _in_proj(xp2, g_attn, w_in_bf, tab_p, seq_period=sp, q_dtype=BF16, conv_width=cw)
    qs, _, _, kfs, vfs, as_ = _in_proj(xs2, g_attn, w_in_bf, tab_s, seq_period=None, q_dtype=F32, conv_width=cw)

    att_p = _attn_prompt(qp, kp, vp, sinks, batch=bp, seq=sp)
    ck = cache_k[0].reshape(bs, WINDOW, KV_WIDTH)
    cv_ = cache_v[0].reshape(bs, WINDOW, KV_WIDTH)
    att_s, nk_s, nv_s = _attn_sample(qs, kfs, vfs, ck, cv_, sinks, batch=bs, ts=ss)

    cv_p = _conv_prompt(ap, conv_w[0], vec(conv_b[0]), vec(conv_ln_g[0]), vec(conv_ln_b[0]), batch=bp, seq=sp)
    cv_s = _conv_sample(as_, state_conv[0], conv_w[0], vec(conv_b[0]), vec(conv_ln_g[0]), vec(conv_ln_b[0]),
                        batch=bs, ts=ss)

    n_exp = router_w.shape[2]
    assert n_exp == N_EXPERTS
    rwt_bf = router_w[0].T.astype(BF16)
    rbt = jnp.broadcast_to(router_b[0][:, None], (n_exp, LANES))
    zero_carry = jnp.zeros((n_exp, LANES), F32)
    tr = _tile(ns, 512, 16)
    assert np_ % tr == 0
    x1p, hnp, rt_p, tcar_p, cnt_p = _outproj_router(xp2, att_p, cv_p, w_out_bf, g_ffn, rwt_bf, rbt, zero_carry, tm=tr)
    x1s, hns, rt_s, tcar_s, cnt = _outproj_router(xs2, att_s, cv_s, w_out_bf, g_ffn, rwt_bf, rbt, cnt_p, tm=tr)

    tme = EXPERT_BLOCK_ROWS
    nb = -(-(n_tok * TOP_K + N_EXPERTS * (tme - 1 + RUN - 1)) // tme)
    blk_start, blk_exp, blk_src, blk_nvalid, blk_next, n_used, fill_pieces, n_fill = _routing_tables(
        cnt, tme=tme, nb=nb)
    runs_per_tile, run_dst = _run_tables(jnp.concatenate([tcar_p, tcar_s], axis=0), cnt, blk_start, td=tr, tme=tme)
    xs_sorted = _dispatch(hnp, hns, jnp.concatenate([rt_p, rt_s], axis=1), fill_pieces, n_fill, runs_per_tile,
                          run_dst, td=tr, nb=nb, tme=tme)
    ys = _experts(xs_sorted, w1[0], b1[0], w2[0], b2[0], blk_exp, blk_src, blk_nvalid, blk_next, n_used, tme=tme)
    g_fin = final_norm_g.reshape(1, d)
    tiles_p = np_ // tr
    copy_slots = lambda rt: _by_tile(rt[0:TOP_K].astype(jnp.int32) * SUBLANES, tr)
    copy_gates = lambda rt: _by_tile(rt[TOP_K:2 * TOP_K], tr)
    y_p = _combine(x1p, copy_slots(rt_p), copy_gates(rt_p), runs_per_tile, run_dst[:tiles_p], ys, g_fin,
                   tr=tr, first=0)
    y_s = _combine(x1s, copy_slots(rt_s), copy_gates(rt_s), runs_per_tile, run_dst[tiles_p:], ys, g_fin,
                   tr=tr, first=tiles_p)

    kv5 = lambda t, bb: t.reshape(bb, -1, KV_WIDTH)[:, -WINDOW:].reshape(bb, WINDOW, N_KV_HEADS, HEAD_DIM)
    new_k_p = kv5(kfp, bp)[None]
    new_v_p = kv5(vfp, bp)[None]
    ctx = CONV_K - 1
    new_c_p = ap.reshape(bp, sp, cw)[:, -ctx:][None]
    new_c_s = jnp.concatenate([state_conv[0], as_.reshape(bs, ss, cw)], axis=1)[:, -ctx:][None]
    return (y_p.reshape(bp, sp, d), y_s.reshape(bs, ss, d), new_k_p, new_v_p, new_c_p,
            kv5(nk_s, bs)[None], kv5(nv_s, bs)[None], new_c_s)
```
